```python
import math
import jax, jax.numpy as jnp
from jax import lax
import numpy as np

D_MODEL = 1024
BATCH = 8
SEQ = 4096
DEPTH = 2

CHUNK = 64
N_MIXERS = 2
N_S5 = (DEPTH + 1) // 2
N_GLA = DEPTH // 2
S5_GROUP = 16
S5_GROUPS = D_MODEL // S5_GROUP
S5_STATE = 64
DT_MIN = 1e-3
DT_MAX = 1e-1
GLA_HEADS = 4
GLA_QK = D_MODEL // 2
GLA_DK = GLA_QK // GLA_HEADS
GLA_DV = D_MODEL // GLA_HEADS
GLA_GATE_RANK = 16
GLA_GATE_TAU = 16.0
GLA_IN = 2 * GLA_QK + D_MODEL + GLA_GATE_RANK + D_MODEL
D_FF = 4 * D_MODEL
EPS = 1e-6

kernel_name = "chunk_causal_s5_gla_hybrid"


def rmsnorm(x, g):
    xf = x.astype(jnp.float32)
    y = xf * lax.rsqrt(jnp.mean(xf * xf, axis=-1, keepdims=True) + EPS) * g.astype(jnp.float32)
    return y.astype(x.dtype)


def modulate(h, shift, scale):
    return h * (1.0 + scale[:, None, :]) + shift[:, None, :]


def s5_mixer(u, a_re, a_im, log_dt, b_re, b_im, c_re, c_im, d_skip, w_glu):
    bsz, seq, _ = u.shape
    f32 = jnp.float32
    uf = u.astype(f32).reshape(bsz, seq, S5_GROUPS, S5_GROUP)
    dt = jnp.exp(log_dt.astype(f32))[:, None]
    ar = a_re.astype(f32)
    ai = a_im.astype(f32)
    mag = jnp.exp(ar * dt)
    ph = ai * dt
    lb_re = mag * jnp.cos(ph)
    lb_im = mag * jnp.sin(ph)
    den = ar * ar + ai * ai
    nr = lb_re - 1.0
    ni = lb_im
    f_re = (nr * ar + ni * ai) / den
    f_im = (ni * ar - nr * ai) / den
    br = b_re.astype(f32)
    bi = b_im.astype(f32)
    bb_re = f_re[..., None] * br - f_im[..., None] * bi
    bb_im = f_re[..., None] * bi + f_im[..., None] * br
    bu_re = jnp.einsum('blgh,gph->blgp', uf, bb_re)
    bu_im = jnp.einsum('blgh,gph->blgp', uf, bb_im)
    la_re = jnp.broadcast_to(lb_re, bu_re.shape)
    la_im = jnp.broadcast_to(lb_im, bu_im.shape)

    def combine(left, right):
        a1r, a1i, b1r, b1i = left
        a2r, a2i, b2r, b2i = right
        return (a2r * a1r - a2i * a1i,
                a2r * a1i + a2i * a1r,
                a2r * b1r - a2i * b1i + b2r,
                a2r * b1i + a2i * b1r + b2i)

    _, _, xr, xi = lax.associative_scan(combine, (la_re, la_im, bu_re, bu_im), axis=1)
    y = (jnp.einsum('blgp,ghp->blgh', xr, c_re.astype(f32))
         - jnp.einsum('blgp,ghp->blgh', xi, c_im.astype(f32))
         + d_skip.astype(f32).reshape(S5_GROUPS, S5_GROUP) * uf)
    z = jax.nn.gelu(y.reshape(bsz, seq, D_MODEL)).astype(u.dtype)
    val, gate = jnp.split(z @ w_glu, 2, axis=-1)
    return val * jax.nn.sigmoid(gate)


def gla_mixer(h, w_in, w_gate2, b_gate, g_norm, w_out):
    bsz, seq, _ = h.shape
    n = seq // CHUNK
    f32 = jnp.float32
    proj = h @ w_in
    q, k, v, glr, r = jnp.split(
        proj, [GLA_QK, 2 * GLA_QK, 2 * GLA_QK + D_MODEL, 2 * GLA_QK + D_MODEL + GLA_GATE_RANK], axis=-1)
    log_a = jax.nn.log_sigmoid((glr @ w_gate2 + b_gate).astype(f32)) / GLA_GATE_TAU

    def heads(t, dh):
        return t.reshape(bsz, n, CHUNK, GLA_HEADS, dh).transpose(0, 3, 1, 2, 4).astype(f32)

    q = heads(q, GLA_DK) * (GLA_DK ** -0.5)
    k = heads(k, GLA_DK)
    v = heads(v, GLA_DV)
    gc = jnp.cumsum(heads(log_a, GLA_DK), axis=3)
    g_end = gc[:, :, :, -1:, :]
    k_dec = k * jnp.exp(g_end - gc)
    scores = jnp.einsum('bhncd,bhnsd->bhncs', q, k_dec)
    o_intra = jnp.einsum('bhncs,bhnse->bhnce', scores, v)
    kv = jnp.einsum('bhnsd,bhnse->bhnde', k_dec, v)
    decay = jnp.exp(g_end[:, :, :, 0, :])

    def step(state, inp):
        kv_c, dec_c = inp
        return dec_c[..., None] * state + kv_c, state

    s0 = jnp.zeros((bsz, GLA_HEADS, GLA_DK, GLA_DV), f32)
    _, s_prev = lax.scan(step, s0, (kv.transpose(2, 0, 1, 3, 4), decay.transpose(2, 0, 1, 3)))
    s_prev = s_prev.transpose(1, 2, 0, 3, 4)
    o_inter = jnp.einsum('bhncd,bhnde->bhnce', q * jnp.exp(g_end), s_prev)
    o = o_intra + o_inter
    o = o * lax.rsqrt(jnp.mean(o * o, axis=-1, keepdims=True) + EPS)
    o = o.transpose(0, 2, 3, 1, 4).reshape(bsz, seq, D_MODEL) * g_norm.astype(f32)
    o = o.astype(h.dtype) * jax.nn.silu(r)
    return o @ w_out


def sqrelu_mlp(h, w1, w2):
    a = jax.nn.relu(h @ w1)
    return (a * a) @ w2


def _fwd_setup_inputs(seed: int = 0) -> dict:
    key = jax.random.key(seed)
    ks = jax.random.split(key, 24)
    f32 = jnp.float32
    nrm = lambda k, shape, s: jax.random.normal(k, shape, f32) * s
    x = jax.random.normal(ks[0], (BATCH, SEQ, D_MODEL), f32)
    c = jax.random.normal(ks[1], (BATCH, D_MODEL), f32)
    w_ada = nrm(ks[2], (DEPTH, D_MODEL, 6 * D_MODEL), 0.5 * D_MODEL ** -0.5)
    b_ada = nrm(ks[3], (DEPTH, 6 * D_MODEL), 0.02)
    norm_mix = 1.0 + nrm(ks[4], (DEPTH, D_MODEL), 0.02)
    norm_mlp = 1.0 + nrm(ks[5], (DEPTH, D_MODEL), 0.02)
    n_idx = jnp.arange(S5_STATE, dtype=f32)
    s5_a_re = -0.5 * jnp.exp(nrm(ks[6], (N_S5, S5_GROUPS, S5_STATE), 0.05))
    s5_a_im = math.pi * n_idx + nrm(ks[7], (N_S5, S5_GROUPS, S5_STATE), 0.05)
    s5_log_dt = jax.random.uniform(ks[8], (N_S5, S5_GROUPS), f32, math.log(DT_MIN), math.log(DT_MAX))
    s5_b_re = nrm(ks[9], (N_S5, S5_GROUPS, S5_STATE, S5_GROUP), (2.0 * S5_GROUP) ** -0.5)
    s5_b_im = nrm(ks[10], (N_S5, S5_GROUPS, S5_STATE, S5_GROUP), (2.0 * S5_GROUP) ** -0.5)
    s5_c_re = nrm(ks[11], (N_S5, S5_GROUPS, S5_GROUP, S5_STATE), S5_STATE ** -0.5)
    s5_c_im = nrm(ks[12], (N_S5, S5_GROUPS, S5_GROUP, S5_STATE), S5_STATE ** -0.5)
    s5_d = nrm(ks[13], (N_S5, D_MODEL), 1.0)
    s5_w_glu = nrm(ks[14], (N_S5, D_MODEL, 2 * D_MODEL), D_MODEL ** -0.5)
    gla_w_in = nrm(ks[15], (N_GLA, D_MODEL, GLA_IN), D_MODEL ** -0.5)
    gla_w_gate2 = nrm(ks[16], (N_GLA, GLA_GATE_RANK, GLA_QK), GLA_GATE_RANK ** -0.5)
    gla_b_gate = nrm(ks[17], (N_GLA, GLA_QK), 0.1)
    gla_g_norm = 1.0 + nrm(ks[18], (N_GLA, D_MODEL), 0.02)
    gla_w_out = nrm(ks[19], (N_GLA, D_MODEL, D_MODEL), D_MODEL ** -0.5)
    w_ff1 = nrm(ks[20], (DEPTH, D_MODEL, D_FF), D_MODEL ** -0.5)
    w_ff2 = nrm(ks[21], (DEPTH, D_FF, D_MODEL), D_FF ** -0.5)
    norm_final = 1.0 + nrm(ks[22], (D_MODEL,), 0.02)
    return {"x": x, "c": c, "w_ada": w_ada, "b_ada": b_ada, "norm_mix": norm_mix, "norm_mlp": norm_mlp,
            "s5_a_re": s5_a_re, "s5_a_im": s5_a_im, "s5_log_dt": s5_log_dt, "s5_b_re": s5_b_re,
            "s5_b_im": s5_b_im, "s5_c_re": s5_c_re, "s5_c_im": s5_c_im, "s5_d": s5_d, "s5_w_glu": s5_w_glu,
            "gla_w_in": gla_w_in, "gla_w_gate2": gla_w_gate2, "gla_b_gate": gla_b_gate,
            "gla_g_norm": gla_g_norm, "gla_w_out": gla_w_out, "w_ff1": w_ff1, "w_ff2": w_ff2,
            "norm_final": norm_final}


def _fwd_reference(x, c, w_ada, b_ada, norm_mix, norm_mlp, s5_a_re, s5_a_im, s5_log_dt, s5_b_re, s5_b_im,
              s5_c_re, s5_c_im, s5_d, s5_w_glu, gla_w_in, gla_w_gate2, gla_b_gate, gla_g_norm, gla_w_out,
              w_ff1, w_ff2, norm_final):
    cs = jax.nn.silu(c)
    for i in range(DEPTH):
        mod = cs @ w_ada[i] + b_ada[i]
        sh1, sc1, gt1, sh2, sc2, gt2 = jnp.split(mod, 6, axis=-1)
        h = modulate(rmsnorm(x, norm_mix[i]), sh1, sc1)
        j = i // N_MIXERS
        if i % N_MIXERS == 0:
            y = s5_mixer(h, s5_a_re[j], s5_a_im[j], s5_log_dt[j], s5_b_re[j], s5_b_im[j],
                         s5_c_re[j], s5_c_im[j], s5_d[j], s5_w_glu[j])
        else:
            y = gla_mixer(h, gla_w_in[j], gla_w_gate2[j], gla_b_gate[j], gla_g_norm[j], gla_w_out[j])
        x = x + gt1[:, None, :] * y
        h = modulate(rmsnorm(x, norm_mlp[i]), sh2, sc2)
        x = x + gt2[:, None, :] * sqrelu_mlp(h, w_ff1[i], w_ff2[i])
    return rmsnorm(x, norm_final)


import jax as _jax
import jax.numpy as _jnp

TWIN_FORMAT = 'train_step'
FWD_PARAMS = ['x', 'c', 'w_ada', 'b_ada', 'norm_mix', 'norm_mlp', 's5_a_re', 's5_a_im', 's5_log_dt', 's5_b_re', 's5_b_im', 's5_c_re', 's5_c_im', 's5_d', 's5_w_glu', 'gla_w_in', 'gla_w_gate2', 'gla_b_gate', 'gla_g_norm', 'gla_w_out', 'w_ff1', 'w_ff2', 'norm_final']
TWIN_WEIGHTS = ['w_ada', 'b_ada', 'norm_mix', 'norm_mlp', 's5_a_re', 's5_a_im', 's5_log_dt', 's5_b_re', 's5_b_im', 's5_c_re', 's5_c_im', 's5_d', 's5_w_glu', 'gla_w_in', 'gla_w_gate2', 'gla_b_gate', 'gla_g_norm', 'gla_w_out', 'w_ff1', 'w_ff2', 'norm_final']
TWIN_DIFF_INPUT = 'x'
TWIN_INPUTS = ['x', 'c', 'w_ada', 'b_ada', 'norm_mix', 'norm_mlp', 's5_a_re', 's5_a_im', 's5_log_dt', 's5_b_re', 's5_b_im', 's5_c_re', 's5_c_im', 's5_d', 's5_w_glu', 'gla_w_in', 'gla_w_gate2', 'gla_b_gate', 'gla_g_norm', 'gla_w_out', 'w_ff1', 'w_ff2', 'norm_final', 'loss_target', 'm_w_ada', 'm_b_ada', 'm_norm_mix', 'm_norm_mlp', 'm_s5_a_re', 'm_s5_a_im', 'm_s5_log_dt', 'm_s5_b_re', 'm_s5_b_im', 'm_s5_c_re', 'm_s5_c_im', 'm_s5_d', 'm_s5_w_glu', 'm_gla_w_in', 'm_gla_w_gate2', 'm_gla_b_gate', 'm_gla_g_norm', 'm_gla_w_out', 'm_w_ff1', 'm_w_ff2', 'm_norm_final', 'v_w_ada', 'v_b_ada', 'v_norm_mix', 'v_norm_mlp', 'v_s5_a_re', 'v_s5_a_im', 'v_s5_log_dt', 'v_s5_b_re', 'v_s5_b_im', 'v_s5_c_re', 'v_s5_c_im', 'v_s5_d', 'v_s5_w_glu', 'v_gla_w_in', 'v_gla_w_gate2', 'v_gla_b_gate', 'v_gla_g_norm', 'v_gla_w_out', 'v_w_ff1', 'v_w_ff2', 'v_norm_final']
TWIN_OUTPUTS = ['loss', 'grad_x', 'grad_w_ada', 'grad_b_ada', 'grad_norm_mix', 'grad_norm_mlp', 'grad_s5_a_re', 'grad_s5_a_im', 'grad_s5_log_dt', 'grad_s5_b_re', 'grad_s5_b_im', 'grad_s5_c_re', 'grad_s5_c_im', 'grad_s5_d', 'grad_s5_w_glu', 'grad_gla_w_in', 'grad_gla_w_gate2', 'grad_gla_b_gate', 'grad_gla_g_norm', 'grad_gla_w_out', 'grad_w_ff1', 'grad_w_ff2', 'grad_norm_final', 'delta_w_ada', 'delta_b_ada', 'delta_norm_mix', 'delta_norm_mlp', 'delta_s5_a_re', 'delta_s5_a_im', 'delta_s5_log_dt', 'delta_s5_b_re', 'delta_s5_b_im', 'delta_s5_c_re', 'delta_s5_c_im', 'delta_s5_d', 'delta_s5_w_glu', 'delta_gla_w_in', 'delta_gla_w_gate2', 'delta_gla_b_gate', 'delta_gla_g_norm', 'delta_gla_w_out', 'delta_w_ff1', 'delta_w_ff2', 'delta_norm_final', 'new_m_w_ada', 'new_m_b_ada', 'new_m_norm_mix', 'new_m_norm_mlp', 'new_m_s5_a_re', 'new_m_s5_a_im', 'new_m_s5_log_dt', 'new_m_s5_b_re', 'new_m_s5_b_im', 'new_m_s5_c_re', 'new_m_s5_c_im', 'new_m_s5_d', 'new_m_s5_w_glu', 'new_m_gla_w_in', 'new_m_gla_w_gate2', 'new_m_gla_b_gate', 'new_m_gla_g_norm', 'new_m_gla_w_out', 'new_m_w_ff1', 'new_m_w_ff2', 'new_m_norm_final', 'new_v_w_ada', 'new_v_b_ada', 'new_v_norm_mix', 'new_v_norm_mlp', 'new_v_s5_a_re', 'new_v_s5_a_im', 'new_v_s5_log_dt', 'new_v_s5_b_re', 'new_v_s5_b_im', 'new_v_s5_c_re', 'new_v_s5_c_im', 'new_v_s5_d', 'new_v_s5_w_glu', 'new_v_gla_w_in', 'new_v_gla_w_gate2', 'new_v_gla_b_gate', 'new_v_gla_g_norm', 'new_v_gla_w_out', 'new_v_w_ff1', 'new_v_w_ff2', 'new_v_norm_final']
TWIN_LEAF_KINDS = {'loss': 'loss', 'grad_x': 'grad_x', 'grad_w_ada': 'grad_w', 'grad_b_ada': 'grad_w', 'grad_norm_mix': 'grad_w', 'grad_norm_mlp': 'grad_w', 'grad_s5_a_re': 'grad_w', 'grad_s5_a_im': 'grad_w', 'grad_s5_log_dt': 'grad_w', 'grad_s5_b_re': 'grad_w', 'grad_s5_b_im': 'grad_w', 'grad_s5_c_re': 'grad_w', 'grad_s5_c_im': 'grad_w', 'grad_s5_d': 'grad_w', 'grad_s5_w_glu': 'grad_w', 'grad_gla_w_in': 'grad_w', 'grad_gla_w_gate2': 'grad_w', 'grad_gla_b_gate': 'grad_w', 'grad_gla_g_norm': 'grad_w', 'grad_gla_w_out': 'grad_w', 'grad_w_ff1': 'grad_w', 'grad_w_ff2': 'grad_w', 'grad_norm_final': 'grad_w', 'delta_w_ada': 'delta_w', 'delta_b_ada': 'delta_w', 'delta_norm_mix': 'delta_w', 'delta_norm_mlp': 'delta_w', 'delta_s5_a_re': 'delta_w', 'delta_s5_a_im': 'delta_w', 'delta_s5_log_dt': 'delta_w', 'delta_s5_b_re': 'delta_w', 'delta_s5_b_im': 'delta_w', 'delta_s5_c_re': 'delta_w', 'delta_s5_c_im': 'delta_w', 'delta_s5_d': 'delta_w', 'delta_s5_w_glu': 'delta_w', 'delta_gla_w_in': 'delta_w', 'delta_gla_w_gate2': 'delta_w', 'delta_gla_b_gate': 'delta_w', 'delta_gla_g_norm': 'delta_w', 'delta_gla_w_out': 'delta_w', 'delta_w_ff1': 'delta_w', 'delta_w_ff2': 'delta_w', 'delta_norm_final': 'delta_w', 'new_m_w_ada': 'new_m', 'new_m_b_ada': 'new_m', 'new_m_norm_mix': 'new_m', 'new_m_norm_mlp': 'new_m', 'new_m_s5_a_re': 'new_m', 'new_m_s5_a_im': 'new_m', 'new_m_s5_log_dt': 'new_m', 'new_m_s5_b_re': 'new_m', 'new_m_s5_b_im': 'new_m', 'new_m_s5_c_re': 'new_m', 'new_m_s5_c_im': 'new_m', 'new_m_s5_d': 'new_m', 'new_m_s5_w_glu': 'new_m', 'new_m_gla_w_in': 'new_m', 'new_m_gla_w_gate2': 'new_m', 'new_m_gla_b_gate': 'new_m', 'new_m_gla_g_norm': 'new_m', 'new_m_gla_w_out': 'new_m', 'new_m_w_ff1': 'new_m', 'new_m_w_ff2': 'new_m', 'new_m_norm_final': 'new_m', 'new_v_w_ada': 'new_v', 'new_v_b_ada': 'new_v', 'new_v_norm_mix': 'new_v', 'new_v_norm_mlp': 'new_v', 'new_v_s5_a_re': 'new_v', 'new_v_s5_a_im': 'new_v', 'new_v_s5_log_dt': 'new_v', 'new_v_s5_b_re': 'new_v', 'new_v_s5_b_im': 'new_v', 'new_v_s5_c_re': 'new_v', 'new_v_s5_c_im': 'new_v', 'new_v_s5_d': 'new_v', 'new_v_s5_w_glu': 'new_v', 'new_v_gla_w_in': 'new_v', 'new_v_gla_w_gate2': 'new_v', 'new_v_gla_b_gate': 'new_v', 'new_v_gla_g_norm': 'new_v', 'new_v_gla_w_out': 'new_v', 'new_v_w_ff1': 'new_v', 'new_v_w_ff2': 'new_v', 'new_v_norm_final': 'new_v'}


def _forward(args):
    return _fwd_reference(*[args[k] for k in FWD_PARAMS])


def _output_shape():
    def fwd():
        inp = _fwd_setup_inputs(0)
        return _fwd_reference(*[inp[k] for k in FWD_PARAMS])
    out = _jax.eval_shape(fwd)
    return out.shape, out.dtype

N_MICROBATCH = 1
ADAM_LR = 0.001
ADAM_B1 = 0.9
ADAM_B2 = 0.999
ADAM_EPS = 1e-08
ADAM_WD = 0.01
ADAM_STEP = 10
PER_EXAMPLE_BATCH_AXIS = {'x': 0, 'c': 0, 'loss_target': 0}
SHARED_INPUTS = []
_WEIGHT_DTYPES = {'w_ada': _jnp.float32, 'b_ada': _jnp.float32, 'norm_mix': _jnp.float32, 'norm_mlp': _jnp.float32, 's5_a_re': _jnp.float32, 's5_a_im': _jnp.float32, 's5_log_dt': _jnp.float32, 's5_b_re': _jnp.float32, 's5_b_im': _jnp.float32, 's5_c_re': _jnp.float32, 's5_c_im': _jnp.float32, 's5_d': _jnp.float32, 's5_w_glu': _jnp.float32, 'gla_w_in': _jnp.float32, 'gla_w_gate2': _jnp.float32, 'gla_b_gate': _jnp.float32, 'gla_g_norm': _jnp.float32, 'gla_w_out': _jnp.float32, 'w_ff1': _jnp.float32, 'w_ff2': _jnp.float32, 'norm_final': _jnp.float32}
MOMENT_SCALE = {'w_ada': 7.956642e-02, 'b_ada': 1.415042e-01, 'norm_mix': 5.180595e-02, 'norm_mlp': 7.940977e-02, 's5_a_re': 2.135361e-03, 's5_a_im': 1.846805e-03, 's5_log_dt': 6.405436e-01, 's5_b_re': 1.248690e-03, 's5_b_im': 1.326151e-03, 's5_c_re': 1.934156e-03, 's5_c_im': 1.815320e-03, 's5_d': 2.523194e-02, 's5_w_glu': 1.763186e-02, 'gla_w_in': 3.951626e-02, 'gla_w_gate2': 9.854161e-03, 'gla_b_gate': 2.509806e-02, 'gla_g_norm': 3.318015e-02, 'gla_w_out': 3.255066e-02, 'w_ff1': 4.075456e-02, 'w_ff2': 7.428793e-02, 'norm_final': 3.220999e+01}


def _to_microbatches(a, axis):
    t = _jnp.moveaxis(a, axis, 0)
    t = t.reshape((N_MICROBATCH, t.shape[0] // N_MICROBATCH) + t.shape[1:])
    return _jnp.moveaxis(t, 1, axis + 1)


def setup_inputs(seed: int = 0) -> dict:
    inp = _fwd_setup_inputs(seed)
    key = _jax.random.fold_in(_jax.random.key(seed), 7919)
    shape, _ = _output_shape()
    out = dict(inp)
    out["loss_target"] = _jax.random.normal(_jax.random.fold_in(key, 0), shape, _jnp.float32)
    for i, name in enumerate(TWIN_WEIGHTS):
        w = inp[name].astype(_jnp.float32)
        if MOMENT_SCALE is None:
            s = _jnp.sqrt(_jnp.mean(_jnp.square(w)) + 1e-30)
        else:
            s = MOMENT_SCALE[name]
        km, kv = _jax.random.split(_jax.random.fold_in(key, i + 1))
        out[name] = w
        out["m_" + name] = s * _jax.random.normal(km, w.shape, _jnp.float32)
        out["v_" + name] = (s * s) * _jax.random.uniform(kv, w.shape, _jnp.float32, 0.5, 1.5)
    if N_MICROBATCH > 1:
        for name, axis in PER_EXAMPLE_BATCH_AXIS.items():
            out[name] = _to_microbatches(out[name], axis)
    return {'x': out['x'], 'c': out['c'], 'w_ada': out['w_ada'], 'b_ada': out['b_ada'], 'norm_mix': out['norm_mix'], 'norm_mlp': out['norm_mlp'], 's5_a_re': out['s5_a_re'], 's5_a_im': out['s5_a_im'], 's5_log_dt': out['s5_log_dt'], 's5_b_re': out['s5_b_re'], 's5_b_im': out['s5_b_im'], 's5_c_re': out['s5_c_re'], 's5_c_im': out['s5_c_im'], 's5_d': out['s5_d'], 's5_w_glu': out['s5_w_glu'], 'gla_w_in': out['gla_w_in'], 'gla_w_gate2': out['gla_w_gate2'], 'gla_b_gate': out['gla_b_gate'], 'gla_g_norm': out['gla_g_norm'], 'gla_w_out': out['gla_w_out'], 'w_ff1': out['w_ff1'], 'w_ff2': out['w_ff2'], 'norm_final': out['norm_final'], 'loss_target': out['loss_target'], 'm_w_ada': out['m_w_ada'], 'm_b_ada': out['m_b_ada'], 'm_norm_mix': out['m_norm_mix'], 'm_norm_mlp': out['m_norm_mlp'], 'm_s5_a_re': out['m_s5_a_re'], 'm_s5_a_im': out['m_s5_a_im'], 'm_s5_log_dt': out['m_s5_log_dt'], 'm_s5_b_re': out['m_s5_b_re'], 'm_s5_b_im': out['m_s5_b_im'], 'm_s5_c_re': out['m_s5_c_re'], 'm_s5_c_im': out['m_s5_c_im'], 'm_s5_d': out['m_s5_d'], 'm_s5_w_glu': out['m_s5_w_glu'], 'm_gla_w_in': out['m_gla_w_in'], 'm_gla_w_gate2': out['m_gla_w_gate2'], 'm_gla_b_gate': out['m_gla_b_gate'], 'm_gla_g_norm': out['m_gla_g_norm'], 'm_gla_w_out': out['m_gla_w_out'], 'm_w_ff1': out['m_w_ff1'], 'm_w_ff2': out['m_w_ff2'], 'm_norm_final': out['m_norm_final'], 'v_w_ada': out['v_w_ada'], 'v_b_ada': out['v_b_ada'], 'v_norm_mix': out['v_norm_mix'], 'v_norm_mlp': out['v_norm_mlp'], 'v_s5_a_re': out['v_s5_a_re'], 'v_s5_a_im': out['v_s5_a_im'], 'v_s5_log_dt': out['v_s5_log_dt'], 'v_s5_b_re': out['v_s5_b_re'], 'v_s5_b_im': out['v_s5_b_im'], 'v_s5_c_re': out['v_s5_c_re'], 'v_s5_c_im': out['v_s5_c_im'], 'v_s5_d': out['v_s5_d'], 'v_s5_w_glu': out['v_s5_w_glu'], 'v_gla_w_in': out['v_gla_w_in'], 'v_gla_w_gate2': out['v_gla_w_gate2'], 'v_gla_b_gate': out['v_gla_b_gate'], 'v_gla_g_norm': out['v_gla_g_norm'], 'v_gla_w_out': out['v_gla_w_out'], 'v_w_ff1': out['v_w_ff1'], 'v_w_ff2': out['v_w_ff2'], 'v_norm_final': out['v_norm_final']}


def _loss(weights, diff, rest, loss_target):
    with _jax.named_scope("forward"):
        args = {**rest, TWIN_DIFF_INPUT: diff, **{k: w.astype(_WEIGHT_DTYPES[k]) for k, w in weights.items()}}
        y = _forward(args)
    with _jax.named_scope("loss_head"):
        err = _jnp.square(y.astype(_jnp.float32) - loss_target)
        return 0.5 * _jnp.sum(_jnp.mean(err, axis=-1)) if err.ndim else 0.5 * err


def _adamw(w, g, m, v):
    m = ADAM_B1 * m + (1.0 - ADAM_B1) * g
    v = ADAM_B2 * v + (1.0 - ADAM_B2) * _jnp.square(g)
    m_hat = m / (1.0 - ADAM_B1 ** ADAM_STEP)
    v_hat = v / (1.0 - ADAM_B2 ** ADAM_STEP)
    delta = -ADAM_LR * (m_hat / (_jnp.sqrt(v_hat) + ADAM_EPS) + ADAM_WD * w)
    return delta, m, v


def reference(x, c, w_ada, b_ada, norm_mix, norm_mlp, s5_a_re, s5_a_im, s5_log_dt, s5_b_re, s5_b_im, s5_c_re, s5_c_im, s5_d, s5_w_glu, gla_w_in, gla_w_gate2, gla_b_gate, gla_g_norm, gla_w_out, w_ff1, w_ff2, norm_final, loss_target, m_w_ada, m_b_ada, m_norm_mix, m_norm_mlp, m_s5_a_re, m_s5_a_im, m_s5_log_dt, m_s5_b_re, m_s5_b_im, m_s5_c_re, m_s5_c_im, m_s5_d, m_s5_w_glu, m_gla_w_in, m_gla_w_gate2, m_gla_b_gate, m_gla_g_norm, m_gla_w_out, m_w_ff1, m_w_ff2, m_norm_final, v_w_ada, v_b_ada, v_norm_mix, v_norm_mlp, v_s5_a_re, v_s5_a_im, v_s5_log_dt, v_s5_b_re, v_s5_b_im, v_s5_c_re, v_s5_c_im, v_s5_d, v_s5_w_glu, v_gla_w_in, v_gla_w_gate2, v_gla_b_gate, v_gla_g_norm, v_gla_w_out, v_w_ff1, v_w_ff2, v_norm_final):
    given = dict(x=x, c=c, w_ada=w_ada, b_ada=b_ada, norm_mix=norm_mix, norm_mlp=norm_mlp, s5_a_re=s5_a_re, s5_a_im=s5_a_im, s5_log_dt=s5_log_dt, s5_b_re=s5_b_re, s5_b_im=s5_b_im, s5_c_re=s5_c_re, s5_c_im=s5_c_im, s5_d=s5_d, s5_w_glu=s5_w_glu, gla_w_in=gla_w_in, gla_w_gate2=gla_w_gate2, gla_b_gate=gla_b_gate, gla_g_norm=gla_g_norm, gla_w_out=gla_w_out, w_ff1=w_ff1, w_ff2=w_ff2, norm_final=norm_final, loss_target=loss_target, m_w_ada=m_w_ada, m_b_ada=m_b_ada, m_norm_mix=m_norm_mix, m_norm_mlp=m_norm_mlp, m_s5_a_re=m_s5_a_re, m_s5_a_im=m_s5_a_im, m_s5_log_dt=m_s5_log_dt, m_s5_b_re=m_s5_b_re, m_s5_b_im=m_s5_b_im, m_s5_c_re=m_s5_c_re, m_s5_c_im=m_s5_c_im, m_s5_d=m_s5_d, m_s5_w_glu=m_s5_w_glu, m_gla_w_in=m_gla_w_in, m_gla_w_gate2=m_gla_w_gate2, m_gla_b_gate=m_gla_b_gate, m_gla_g_norm=m_gla_g_norm, m_gla_w_out=m_gla_w_out, m_w_ff1=m_w_ff1, m_w_ff2=m_w_ff2, m_norm_final=m_norm_final, v_w_ada=v_w_ada, v_b_ada=v_b_ada, v_norm_mix=v_norm_mix, v_norm_mlp=v_norm_mlp, v_s5_a_re=v_s5_a_re, v_s5_a_im=v_s5_a_im, v_s5_log_dt=v_s5_log_dt, v_s5_b_re=v_s5_b_re, v_s5_b_im=v_s5_b_im, v_s5_c_re=v_s5_c_re, v_s5_c_im=v_s5_c_im, v_s5_d=v_s5_d, v_s5_w_glu=v_s5_w_glu, v_gla_w_in=v_gla_w_in, v_gla_w_gate2=v_gla_w_gate2, v_gla_b_gate=v_gla_b_gate, v_gla_g_norm=v_gla_g_norm, v_gla_w_out=v_gla_w_out, v_w_ff1=v_w_ff1, v_w_ff2=v_w_ff2, v_norm_final=v_norm_final)
    weights = {n: given[n] for n in TWIN_WEIGHTS}
    shared = {n: given[n] for n in SHARED_INPUTS}
    per_example = {n: given[n] for n in ['x', 'c']}
    grad_fn = _jax.value_and_grad(_loss, argnums=(0, 1))

    def one_microbatch(ex, loss_target):
        ex = dict(ex)
        diff = ex.pop(TWIN_DIFF_INPUT)
        return grad_fn(weights, diff, {**shared, **ex}, loss_target)

    if N_MICROBATCH == 1:
        loss, (grad_w, grad_x) = one_microbatch(per_example, given["loss_target"])
    else:
        def body(carry, xs):
            loss_sum, grad_sum = carry
            l_k, (gw_k, gx_k) = one_microbatch(xs[0], xs[1])
            with _jax.named_scope("update"):
                return (loss_sum + l_k, _jax.tree.map(_jnp.add, grad_sum, gw_k)), gx_k

        init = (_jnp.zeros((), _jnp.float32), _jax.tree.map(_jnp.zeros_like, weights))
        (loss, grad_w), grad_x = _jax.lax.scan(body, init, (per_example, given["loss_target"]))
    with _jax.named_scope("update"):
        delta_w, new_m, new_v = {}, {}, {}
        for n in TWIN_WEIGHTS:
            delta_w[n], new_m[n], new_v[n] = _adamw(weights[n], grad_w[n], given["m_" + n], given["v_" + n])
    return (loss, grad_x, *[grad_w[n] for n in TWIN_WEIGHTS], *[delta_w[n] for n in TWIN_WEIGHTS],
            *[new_m[n] for n in TWIN_WEIGHTS], *[new_v[n] for n in TWIN_WEIGHTS])
```

```python
import functools
import math

import jax
import jax.numpy as jnp
from jax import lax
from jax.experimental import pallas as pl
from jax.experimental.pallas import tpu as pltpu

F32 = jnp.float32
BF16 = jnp.bfloat16
MESH = pl.DeviceIdType.MESH
HI = lax.Precision.HIGHEST

EPS = 1e-6
CHUNK = 64
S5_H = 16
S5_P = 64
GPB = 8
NSEG = 8
HEADS = 4
GATE_RANK = 16
GATE_TAU = 16.0
LR, B1, B2, AEPS, WD, ASTEP = 0.001, 0.9, 0.999, 1e-08, 0.01, 10
VMEM_LIMIT = 56 << 20

MASK_C = ((0, 0, 1),)
MASK_CHIPS = ((1, 0, 0), (0, 1, 0), (1, 1, 0))
MASK_ALL = ((0, 0, 1), (0, 1, 0), (0, 1, 1), (1, 0, 0), (1, 0, 1), (1, 1, 0), (1, 1, 1))


def _params(*sem):
    return pltpu.CompilerParams(dimension_semantics=sem or None, vmem_limit_bytes=VMEM_LIMIT)


def _matmul(a, b, *, name, ta=False, tb=False, tm=512, tn=512, tk=512, out_dtypes=(F32,),
            a_fn=None, epi=None, epi_ins=()):
    M, K = (a.shape[1], a.shape[0]) if ta else a.shape
    N = b.shape[0] if tb else b.shape[1]
    tm, tn, tk = min(tm, M), min(tn, N), min(tk, K)
    assert M % tm == 0 and N % tn == 0 and K % tk == 0, (name, M, N, K)
    nk = K // tk
    ne = len(epi_ins)
    dn = (((0 if ta else 1,), (1 if tb else 0,)), ((), ()))

    def body(a_ref, b_ref, *rest):
        e_refs, o_refs, acc = rest[:ne], rest[ne:-1], rest[-1]
        k = pl.program_id(2)

        @pl.when(k == 0)
        def _():
            acc[...] = jnp.zeros_like(acc)

        at = a_ref[...]
        if a_fn is not None:
            at = a_fn(at)
        acc[...] += lax.dot_general(at.astype(BF16), b_ref[...].astype(BF16), dn, preferred_element_type=F32)

        @pl.when(k == nk - 1)
        def _():
            outs = (acc[...],) if epi is None else epi(acc[...], *[r[...] for r in e_refs])
            for r, o in zip(o_refs, outs):
                r[...] = o.astype(r.dtype)

    a_spec = pl.BlockSpec((tk, tm), lambda i, j, k: (k, i)) if ta else pl.BlockSpec((tm, tk), lambda i, j, k: (i, k))
    b_spec = pl.BlockSpec((tn, tk), lambda i, j, k: (j, k)) if tb else pl.BlockSpec((tk, tn), lambda i, j, k: (k, j))
    o_spec = pl.BlockSpec((tm, tn), lambda i, j, k: (i, j))
    outs = pl.pallas_call(
        body, name=name, grid=(M // tm, N // tn, nk),
        in_specs=[a_spec, b_spec] + [o_spec] * ne,
        out_specs=[o_spec] * len(out_dtypes),
        out_shape=[jax.ShapeDtypeStruct((M, N), d) for d in out_dtypes],
        scratch_shapes=[pltpu.VMEM((tm, tn), F32)],
        compiler_params=_params("parallel", "parallel", "arbitrary"),
    )(a, b, *epi_ins)
    return outs[0] if len(outs) == 1 else outs


def _rows(fn, rows_in, vecs_in, rows_out, acc_out, *, name, tm=256):
    L = rows_in[0].shape[0]
    tm = min(tm, L)
    assert L % tm == 0
    nr, nv, no, na = len(rows_in), len(vecs_in), len(rows_out), len(acc_out)

    def body(*refs):
        rin, vin = refs[:nr], refs[nr:nr + nv]
        rout, aout = refs[nr + nv:nr + nv + no], refs[nr + nv + no:]
        outs = fn(*[r[...] for r in rin], *[v[...] for v in vin])
        for r, o in zip(rout, outs[:no]):
            r[...] = o.astype(r.dtype)
        if na:
            @pl.when(pl.program_id(0) == 0)
            def _():
                for r in aout:
                    r[...] = jnp.zeros_like(r)

            for r, o in zip(aout, outs[no:]):
                r[...] += o

    def whole(shape):
        return pl.BlockSpec(shape, lambda i, _n=len(shape): (0,) * _n)

    outs = pl.pallas_call(
        body, name=name, grid=(L // tm,),
        in_specs=[pl.BlockSpec((tm, r.shape[1]), lambda i: (i, 0)) for r in rows_in] + [whole(v.shape) for v in vecs_in],
        out_specs=[pl.BlockSpec((tm, c), lambda i: (i, 0)) for c, _ in rows_out] + [whole(s) for s in acc_out],
        out_shape=[jax.ShapeDtypeStruct((L, c), d) for c, d in rows_out] + [jax.ShapeDtypeStruct(s, F32) for s in acc_out],
        compiler_params=_params("arbitrary"),
    )(*rows_in, *vecs_in)
    return outs


def _rsum(t):
    return jnp.sum(t, axis=0, keepdims=True)


def _norm_mod(x, g, sc, sh):
    rs = lax.rsqrt(jnp.mean(x * x, axis=-1, keepdims=True) + EPS)
    return x * rs * g * (1.0 + sc) + sh


def _norm_mod_bwd(x, dh, g, sc):
    rs = lax.rsqrt(jnp.mean(x * x, axis=-1, keepdims=True) + EPS)
    xh = x * rs
    dn = dh * (1.0 + sc)
    dxh = dn * g
    dx = rs * (dxh - xh * jnp.mean(dxh * xh, axis=-1, keepdims=True))
    return dx, _rsum(dh), _rsum(dh * xh * g), _rsum(dn * xh)


def _sigmoid(x):
    return jax.nn.sigmoid(x)


def _gelu(y):
    return jax.nn.gelu(y, approximate=True)


def _gelu_grad(y):
    c = math.sqrt(2.0 / math.pi)
    t = jnp.tanh(c * (y + 0.044715 * y * y * y))
    return 0.5 * (1.0 + t) + 0.5 * y * (1.0 - t * t) * c * (1.0 + 3.0 * 0.044715 * y * y)


def _s5_tables(lb_re, lb_im, bb_re, bb_im, c_re, c_im, seg_len):
    G = lb_re.shape[0]
    nb = G // GPB
    eye = jnp.eye(GPB, dtype=F32)

    def bdiag(t):
        a, b = t.shape[1:]
        t = t.reshape(nb, GPB, a, b)
        return (t[:, :, :, None, :] * eye[None, :, None, :, None]).reshape(nb, GPB * a, GPB * b)

    bbd = jnp.concatenate([bdiag(bb_re.transpose(0, 2, 1)), bdiag(bb_im.transpose(0, 2, 1))], axis=2)
    cbd = jnp.concatenate([bdiag(c_re.transpose(0, 2, 1)), -bdiag(c_im.transpose(0, 2, 1))], axis=1)

    def lanes(re, im):
        t = jnp.concatenate([re.reshape(nb, GPB * S5_P), im.reshape(nb, GPB * S5_P)], axis=1)
        return jnp.repeat(t, NSEG, axis=0)

    tr, ti = lb_re, lb_im
    for _ in range(int(math.log2(seg_len))):
        tr, ti = tr * tr - ti * ti, 2.0 * tr * ti
    return dict(bbd=bbd.astype(BF16), bbdT=bbd.transpose(0, 2, 1).astype(BF16), cbd=cbd.astype(BF16),
                cbdT=cbd.transpose(0, 2, 1).astype(BF16), lam=lanes(lb_re, lb_im), lamT=lanes(tr, ti))


def _s5_untable(acc):
    nb = acc.shape[0]
    t = acc.reshape(nb, GPB, S5_H, 2, GPB, S5_P)
    d = jnp.diagonal(t, axis1=1, axis2=4)
    d = d.transpose(0, 4, 2, 1, 3).reshape(nb * GPB, 2, S5_H, S5_P)
    return d[:, 0], d[:, 1]


def _s5_fix(ends, lamT, *, reverse, name):
    nrow = ends.shape[0]
    half = ends.shape[1] // 2

    def body(e_ref, t_ref, o_ref):
        for gb in range(nrow // NSEG):
            r0 = gb * NSEG
            tr, ti = t_ref[r0:r0 + 1, :half], t_ref[r0:r0 + 1, half:]
            cr = jnp.zeros((1, half), F32)
            ci = jnp.zeros((1, half), F32)
            order = range(NSEG - 1, -1, -1) if reverse else range(NSEG)
            for n, s in enumerate(order):
                if n > 0:
                    p = s + 1 if reverse else s - 1
                    er, ei = e_ref[r0 + p:r0 + p + 1, :half], e_ref[r0 + p:r0 + p + 1, half:]
                    if reverse:
                        cr, ci = tr * cr + ti * ci + er, tr * ci - ti * cr + ei
                    else:
                        cr, ci = tr * cr - ti * ci + er, tr * ci + ti * cr + ei
                o_ref[r0 + s:r0 + s + 1, :half] = cr
                o_ref[r0 + s:r0 + s + 1, half:] = ci

    return pl.pallas_call(body, name=name, out_shape=jax.ShapeDtypeStruct(ends.shape, F32),
                          compiler_params=_params())(ends, lamT)


def _whole(shape):
    return pl.BlockSpec(shape, lambda i, _n=len(shape): (0,) * _n)


def _s5_fwd(up, carry_in, tb, dvec, *, emit, name, R=256):
    L, D = up.shape
    R = min(R, L)
    nb, ta, ngb = L // R, R // NSEG, D // 128
    SW = GPB * S5_P
    crows = ngb * NSEG

    def body(u_ref, cin_ref, lam_ref, b_ref, c_ref, d_ref, *rest):
        if emit:
            y_ref, z_ref, ck_ref, carry, xbuf = rest
        else:
            cout_ref, carry, xbuf = rest
        i = pl.program_id(0)

        @pl.when(i == 0)
        def _():
            carry[...] = cin_ref[...]

        if emit:
            ck_ref[0] = carry[...]
        for gb in range(ngb):
            cols = slice(gb * 128, (gb + 1) * 128)
            rws = slice(gb * NSEG, (gb + 1) * NSEG)
            ug = u_ref[:, cols]
            xbuf[...] = jnp.dot(ug.astype(BF16), b_ref[gb], preferred_element_type=F32)
            lr, li = lam_ref[rws, :SW], lam_ref[rws, SW:]

            def step(a, c, lr=lr, li=li):
                cr, ci = c
                o = pl.multiple_of(a * NSEG, NSEG)
                nr = lr * cr - li * ci + xbuf[pl.ds(o, NSEG), :SW]
                ni = lr * ci + li * cr + xbuf[pl.ds(o, NSEG), SW:]
                if emit:
                    xbuf[pl.ds(o, NSEG), :SW] = nr
                    xbuf[pl.ds(o, NSEG), SW:] = ni
                return nr, ni

            cr, ci = lax.fori_loop(0, ta, step, (carry[rws, :SW], carry[rws, SW:]), unroll=2)
            carry[rws, :SW] = cr
            carry[rws, SW:] = ci
            if emit:
                y = jnp.dot(xbuf[...].astype(BF16), c_ref[gb], preferred_element_type=F32) + d_ref[:, cols] * ug
                y_ref[:, cols] = y
                z_ref[:, cols] = _gelu(y).astype(BF16)
        if not emit:
            cout_ref[...] = carry[...]

    rowblk = pl.BlockSpec((R, D), lambda i: (i, 0))
    if emit:
        out_shape = [jax.ShapeDtypeStruct((L, D), F32), jax.ShapeDtypeStruct((L, D), BF16),
                     jax.ShapeDtypeStruct((nb, crows, 2 * SW), F32)]
        out_specs = [rowblk, rowblk, pl.BlockSpec((1, crows, 2 * SW), lambda i: (i, 0, 0))]
    else:
        out_shape = [jax.ShapeDtypeStruct((crows, 2 * SW), F32)]
        out_specs = [_whole((crows, 2 * SW))]
    return pl.pallas_call(
        body, name=name, grid=(nb,),
        in_specs=[rowblk, _whole(carry_in.shape), _whole(tb["lam"].shape), _whole(tb["bbd"].shape),
                  _whole(tb["cbd"].shape), _whole(dvec.shape)],
        out_specs=out_specs, out_shape=out_shape,
        scratch_shapes=[pltpu.VMEM((crows, 2 * SW), F32), pltpu.VMEM((R, 2 * SW), F32)],
        compiler_params=_params("arbitrary"),
    )(up, carry_in, tb["lam"], tb["bbd"], tb["cbd"], dvec)


def _s5_bwd(up, y, dz, ck, gcarry_in, tb, dvec, *, emit, name, R=256):
    L, D = up.shape
    R = min(R, L)
    nb, ta, ngb = L // R, R // NSEG, D // 128
    SW = GPB * S5_P
    crows = ngb * NSEG

    def body(u_ref, y_ref, dz_ref, ck_ref, gin_ref, lam_ref, b_ref, bt_ref, ct_ref, d_ref, *rest):
        if emit:
            du_ref, db_ref, dc_ref, dl_ref, dd_ref, gcarry, xbuf, gbuf, dybuf = rest
        else:
            gout_ref, gcarry, gbuf, dybuf = rest
        i = pl.program_id(0)

        @pl.when(i == 0)
        def _():
            gcarry[...] = gin_ref[...]
            if emit:
                db_ref[...] = jnp.zeros_like(db_ref)
                dc_ref[...] = jnp.zeros_like(dc_ref)
                dl_ref[...] = jnp.zeros_like(dl_ref)
                dd_ref[...] = jnp.zeros_like(dd_ref)

        dybuf[...] = dz_ref[...] * _gelu_grad(y_ref[...])
        for gb in range(ngb):
            cols = slice(gb * 128, (gb + 1) * 128)
            rws = slice(gb * NSEG, (gb + 1) * NSEG)
            dyg = dybuf[:, cols]
            lr, li = lam_ref[rws, :SW], lam_ref[rws, SW:]
            gbuf[...] = jnp.dot(dyg.astype(BF16), ct_ref[gb], preferred_element_type=F32)
            if emit:
                ug = u_ref[:, cols]
                xbuf[0:NSEG, :] = ck_ref[0, rws, :]
                xbuf[NSEG:, :] = jnp.dot(ug.astype(BF16), b_ref[gb], preferred_element_type=F32)

                def fstep(a, c, lr=lr, li=li):
                    cr, ci = c
                    o = pl.multiple_of(a * NSEG + NSEG, NSEG)
                    nr = lr * cr - li * ci + xbuf[pl.ds(o, NSEG), :SW]
                    ni = lr * ci + li * cr + xbuf[pl.ds(o, NSEG), SW:]
                    xbuf[pl.ds(o, NSEG), :SW] = nr
                    xbuf[pl.ds(o, NSEG), SW:] = ni
                    return nr, ni

                lax.fori_loop(0, ta, fstep, (xbuf[0:NSEG, :SW], xbuf[0:NSEG, SW:]), unroll=2)

            def rstep(k, c, lr=lr, li=li):
                o = pl.multiple_of((ta - 1 - k) * NSEG, NSEG)
                gr_n, gi_n = c[0], c[1]
                gr = gbuf[pl.ds(o, NSEG), :SW] + lr * gr_n + li * gi_n
                gi = gbuf[pl.ds(o, NSEG), SW:] - li * gr_n + lr * gi_n
                if not emit:
                    return gr, gi
                gbuf[pl.ds(o, NSEG), :SW] = gr
                gbuf[pl.ds(o, NSEG), SW:] = gi
                xr, xi = xbuf[pl.ds(o, NSEG), :SW], xbuf[pl.ds(o, NSEG), SW:]
                return gr, gi, c[2] + gr * xr + gi * xi, c[3] + gi * xr - gr * xi

            c0 = (gcarry[rws, :SW], gcarry[rws, SW:])
            if emit:
                c0 = c0 + (jnp.zeros((NSEG, SW), F32), jnp.zeros((NSEG, SW), F32))
            cf = lax.fori_loop(0, ta, rstep, c0, unroll=2)
            gcarry[rws, :SW] = cf[0]
            gcarry[rws, SW:] = cf[1]
            if emit:
                dl_ref[rws, :SW] += cf[2]
                dl_ref[rws, SW:] += cf[3]
                gb16 = gbuf[...].astype(BF16)
                du_ref[:, cols] = jnp.dot(gb16, bt_ref[gb], preferred_element_type=F32) + d_ref[:, cols] * dyg
                tn = (((0,), (0,)), ((), ()))
                db_ref[gb] += lax.dot_general(ug.astype(BF16), gb16, tn, preferred_element_type=F32)
                dc_ref[gb] += lax.dot_general(dyg.astype(BF16), xbuf[NSEG:, :].astype(BF16), tn,
                                              preferred_element_type=F32)
                dd_ref[:, cols] += _rsum(dyg * ug)
        if not emit:
            gout_ref[...] = gcarry[...]

    rev = pl.BlockSpec((R, D), lambda i: (nb - 1 - i, 0))
    acc3 = (ngb, 128, 2 * SW)
    if emit:
        out_shape = [jax.ShapeDtypeStruct((L, D), F32), jax.ShapeDtypeStruct(acc3, F32), jax.ShapeDtypeStruct(acc3, F32),
                     jax.ShapeDtypeStruct((crows, 2 * SW), F32), jax.ShapeDtypeStruct((1, D), F32)]
        out_specs = [rev, _whole(acc3), _whole(acc3), _whole((crows, 2 * SW)), _whole((1, D))]
        scratch = [pltpu.VMEM((crows, 2 * SW), F32), pltpu.VMEM((R + NSEG, 2 * SW), F32),
                   pltpu.VMEM((R, 2 * SW), F32), pltpu.VMEM((R, D), F32)]
    else:
        out_shape = [jax.ShapeDtypeStruct((crows, 2 * SW), F32)]
        out_specs = [_whole((crows, 2 * SW))]
        scratch = [pltpu.VMEM((crows, 2 * SW), F32), pltpu.VMEM((R, 2 * SW), F32), pltpu.VMEM((R, D), F32)]
    return pl.pallas_call(
        body, name=name, grid=(nb,),
        in_specs=[rev, rev, rev, pl.BlockSpec((1, crows, 2 * SW), lambda i: (nb - 1 - i, 0, 0)),
                  _whole(gcarry_in.shape), _whole(tb["lam"].shape), _whole(tb["bbd"].shape),
                  _whole(tb["bbdT"].shape), _whole(tb["cbdT"].shape), _whole(dvec.shape)],
        out_specs=out_specs, out_shape=out_shape, scratch_shapes=scratch,
        compiler_params=_params("arbitrary"),
    )(up, y, dz, ck, gcarry_in, tb["lam"], tb["bbd"], tb["bbdT"], tb["cbdT"], dvec)


def _log_sigmoid(x):
    return jnp.minimum(x, 0.0) - jnp.log(1.0 + jnp.exp(-jnp.abs(x)))


def _gla_gates(p, wg_ref, bg_ref, QK):
    C = p.shape[0]
    glr = p[:, 6 * QK:6 * QK + 128].astype(BF16)
    gpre = jnp.dot(glr, wg_ref[...], preferred_element_type=F32) + bg_ref[...]
    la = _log_sigmoid(gpre) * (1.0 / GATE_TAU)
    row = lax.broadcasted_iota(jnp.int32, (C, C), 0)
    col = lax.broadcasted_iota(jnp.int32, (C, C), 1)
    gc = jnp.dot((row >= col).astype(F32), la, precision=HI, preferred_element_type=F32)
    ge = gc[C - 1:C, :]
    w = jnp.exp(ge - gc)
    return glr, gpre, la, ge, w


def _gla_fwd(proj, wg2p, bg, gn, *, name):
    L = proj.shape[0]
    QK = wg2p.shape[1]
    DK, DV = QK // HEADS, 2 * QK // HEADS
    nC = L // CHUNK
    scale = DK ** -0.5
    tn = (((0,), (0,)), ((), ()))

    def body(p_ref, wg_ref, bg_ref, gn_ref, og_ref, s_ref, sst):
        @pl.when(pl.program_id(0) == 0)
        def _():
            sst[...] = jnp.zeros_like(sst)

        p = p_ref[...]
        _, _, la, _, w = _gla_gates(p, wg_ref, bg_ref, QK)
        ones = jnp.ones((CHUNK, DV), F32)
        for h in range(HEADS):
            q = p[:, h * DK:(h + 1) * DK] * scale
            kd = p[:, QK + h * DK:QK + (h + 1) * DK] * w[:, h * DK:(h + 1) * DK]
            v = p[:, 2 * QK + h * DV:2 * QK + (h + 1) * DV]
            r = p[:, 4 * QK + h * DV:4 * QK + (h + 1) * DV]
            dec = jnp.exp(lax.dot_general(la[:, h * DK:(h + 1) * DK], ones, tn, precision=HI, preferred_element_type=F32))
            kv = lax.dot_general(kd.astype(BF16), v.astype(BF16), tn, preferred_element_type=F32)
            S = dec * sst[h * DK:(h + 1) * DK, :] + kv
            sst[h * DK:(h + 1) * DK, :] = S
            s_ref[0, h * DK:(h + 1) * DK, :] = S
            o = jnp.dot(q.astype(BF16), S.astype(BF16), preferred_element_type=F32)
            on = o * lax.rsqrt(jnp.mean(o * o, axis=-1, keepdims=True) + EPS)
            og_ref[:, h * DV:(h + 1) * DV] = (on * gn_ref[:, h * DV:(h + 1) * DV] * (r * _sigmoid(r))).astype(BF16)

    return pl.pallas_call(
        body, name=name, grid=(nC,),
        in_specs=[pl.BlockSpec((CHUNK, proj.shape[1]), lambda i: (i, 0)), _whole(wg2p.shape), _whole(bg.shape), _whole(gn.shape)],
        out_specs=[pl.BlockSpec((CHUNK, 2 * QK), lambda i: (i, 0)), pl.BlockSpec((1, QK, DV), lambda i: (i, 0, 0))],
        out_shape=[jax.ShapeDtypeStruct((L, 2 * QK), BF16), jax.ShapeDtypeStruct((nC, QK, DV), F32)],
        scratch_shapes=[pltpu.VMEM((QK, DV), F32)],
        compiler_params=_params("arbitrary"),
    )(proj, wg2p, bg, gn)


def _gla_bwd(proj, dog, states, wg2p, bg, gn, *, name):
    L, W = proj.shape
    QK = wg2p.shape[1]
    DK, DV = QK // HEADS, 2 * QK // HEADS
    nC = L // CHUNK
    scale = DK ** -0.5
    tn = (((0,), (0,)), ((), ()))
    nt = (((1,), (1,)), ((), ()))

    def body(p_ref, dog_ref, sc_ref, sp_ref, wg_ref, bg_ref, gn_ref, dp_ref, dwg_ref, dbg_ref, dgn_ref, gst):
        i = pl.program_id(0)

        @pl.when(i == 0)
        def _():
            gst[...] = jnp.zeros_like(gst)
            dwg_ref[...] = jnp.zeros_like(dwg_ref)
            dbg_ref[...] = jnp.zeros_like(dbg_ref)
            dgn_ref[...] = jnp.zeros_like(dgn_ref)

        has_prev = i < nC - 1
        p = p_ref[...]
        glr, gpre, la, ge, w = _gla_gates(p, wg_ref, bg_ref, QK)
        row = lax.broadcasted_iota(jnp.int32, (CHUNK, CHUNK), 0)
        col = lax.broadcasted_iota(jnp.int32, (CHUNK, CHUNK), 1)
        tri_u = (col >= row).astype(F32)
        ones = jnp.ones((CHUNK, DV), F32)
        ones8 = jnp.ones((8, DV), F32)
        dla_heads = []
        for h in range(HEADS):
            ks, vs = slice(h * DK, (h + 1) * DK), slice(h * DV, (h + 1) * DV)
            qs = p[:, h * DK:(h + 1) * DK] * scale
            k = p[:, QK + h * DK:QK + (h + 1) * DK]
            v = p[:, 2 * QK + h * DV:2 * QK + (h + 1) * DV]
            r = p[:, 4 * QK + h * DV:4 * QK + (h + 1) * DV]
            wh = w[:, ks]
            kd = k * wh
            S = sc_ref[0, ks, :]
            Sp = jnp.where(has_prev, sp_ref[0, ks, :], 0.0)
            o = jnp.dot(qs.astype(BF16), S.astype(BF16), preferred_element_type=F32)
            rs = lax.rsqrt(jnp.mean(o * o, axis=-1, keepdims=True) + EPS)
            on = o * rs
            sr = _sigmoid(r)
            dg = dog_ref[:, vs]
            gnh = gn_ref[:, vs]
            dp_ref[:, 4 * QK + h * DV:4 * QK + (h + 1) * DV] = (dg * on * gnh * (sr * (1.0 + r * (1.0 - sr)))).astype(BF16)
            dt = dg * (r * sr)
            dgn_ref[:, vs] += _rsum(dt * on)
            don = dt * gnh
            do = (rs * (don - on * jnp.mean(don * on, axis=-1, keepdims=True))).astype(BF16)
            Gc = gst[ks, :] + lax.dot_general(qs.astype(BF16), do, tn, preferred_element_type=F32)
            G16 = Gc.astype(BF16)
            dp_ref[:, h * DK:(h + 1) * DK] = (lax.dot_general(do, S.astype(BF16), nt, preferred_element_type=F32) * scale).astype(BF16)
            dkd = lax.dot_general(v.astype(BF16), G16, nt, preferred_element_type=F32)
            dp_ref[:, 2 * QK + h * DV:2 * QK + (h + 1) * DV] = jnp.dot(kd.astype(BF16), G16, preferred_element_type=F32).astype(BF16)
            dec_col = jnp.exp(lax.dot_general(la[:, ks], ones, tn, precision=HI, preferred_element_type=F32))
            gst[ks, :] = dec_col * Gc
            ddec = lax.dot_general(ones8, Gc * Sp, nt, precision=HI, preferred_element_type=F32)[0:1, :]
            dp_ref[:, QK + h * DK:QK + (h + 1) * DK] = (dkd * wh).astype(BF16)
            dww = dkd * kd
            dge = jnp.exp(ge[:, ks]) * ddec + _rsum(dww)
            dla_heads.append(dge - jnp.dot(tri_u, dww, precision=HI, preferred_element_type=F32))
        dla = jnp.concatenate(dla_heads, axis=1)
        dgpre = dla * (1.0 / GATE_TAU) * (1.0 - _sigmoid(gpre))
        d16 = dgpre.astype(BF16)
        dp_ref[:, 6 * QK:6 * QK + 128] = lax.dot_general(d16, wg_ref[...], nt, preferred_element_type=F32).astype(BF16)
        dwg_ref[...] += lax.dot_general(glr, d16, tn, preferred_element_type=F32)
        dbg_ref[...] += _rsum(dgpre)

    rev = lambda i: (nC - 1 - i, 0)
    return pl.pallas_call(
        body, name=name, grid=(nC,),
        in_specs=[pl.BlockSpec((CHUNK, W), rev), pl.BlockSpec((CHUNK, 2 * QK), rev),
                  pl.BlockSpec((1, QK, DV), lambda i: (nC - 1 - i, 0, 0)),
                  pl.BlockSpec((1, QK, DV), lambda i: (jnp.maximum(nC - 2 - i, 0), 0, 0)),
                  _whole(wg2p.shape), _whole(bg.shape), _whole(gn.shape)],
        out_specs=[pl.BlockSpec((CHUNK, W), rev), _whole((128, QK)), _whole((1, QK)), _whole((1, 2 * QK))],
        out_shape=[jax.ShapeDtypeStruct((L, W), BF16), jax.ShapeDtypeStruct((128, QK), F32),
                   jax.ShapeDtypeStruct((1, QK), F32), jax.ShapeDtypeStruct((1, 2 * QK), F32)],
        scratch_shapes=[pltpu.VMEM((QK, DV), F32)],
        compiler_params=_params("arbitrary"),
    )(proj, dog, states, states, wg2p, bg, gn)


def _exchange(src, masks, *, scatter, name):
    vary = [any(m[k] for m in masks) for k in range(3)]
    nslots = 2 ** sum(vary)
    blk = src.shape[1:] if scatter else src.shape
    n = len(masks)

    def slot(coords):
        s = 0
        for k in range(3):
            if vary[k]:
                s = s * 2 + coords[k]
        return s

    def body(src_ref, dst_ref, send_sems, recv_sems, loc_sem):
        me = (lax.axis_index("x"), lax.axis_index("y"), lax.axis_index("c"))
        mine = slot(me)
        loc = pltpu.make_async_copy(src_ref.at[mine] if scatter else src_ref, dst_ref.at[mine], loc_sem)
        loc.start()
        copies = []
        for k, m in enumerate(masks):
            peer = tuple(1 - me[d] if m[d] else me[d] for d in range(3))
            cp = pltpu.make_async_remote_copy(
                src_ref=src_ref.at[slot(peer)] if scatter else src_ref, dst_ref=dst_ref.at[mine],
                send_sem=send_sems.at[k], recv_sem=recv_sems.at[k], device_id=peer, device_id_type=MESH)
            cp.start()
            copies.append(cp)
        for cp in copies:
            cp.wait()
        loc.wait()

    return pl.pallas_call(
        body, name=name,
        in_specs=[pl.BlockSpec(memory_space=pl.ANY)], out_specs=pl.BlockSpec(memory_space=pl.ANY),
        out_shape=jax.ShapeDtypeStruct((nslots,) + tuple(blk), src.dtype),
        scratch_shapes=[pltpu.SemaphoreType.DMA((n,)), pltpu.SemaphoreType.DMA((n,)), pltpu.SemaphoreType.DMA(())],
    )(src)


def _sum_slots(parts, out_dtype, *, name, tm=256):
    n, R, C = parts.shape
    tm = min(tm, R)
    assert R % tm == 0

    def body(p_ref, o_ref):
        t = p_ref[0].astype(F32)
        for s in range(1, n):
            t = t + p_ref[s].astype(F32)
        o_ref[...] = t.astype(o_ref.dtype)

    return pl.pallas_call(
        body, name=name, grid=(R // tm,),
        in_specs=[pl.BlockSpec((n, tm, C), lambda i: (0, i, 0))], out_specs=pl.BlockSpec((tm, C), lambda i: (i, 0)),
        out_shape=jax.ShapeDtypeStruct((R, C), out_dtype), compiler_params=_params("parallel"),
    )(parts)


def _adamw(w, g, m, v, *, name, tm=256):
    R, C = w.shape
    tm = tm if (R % tm == 0 and R > tm) else R

    def body(w_ref, g_ref, m_ref, v_ref, d_ref, nm_ref, nv_ref):
        gg = g_ref[...]
        nm = B1 * m_ref[...] + (1.0 - B1) * gg
        nv = B2 * v_ref[...] + (1.0 - B2) * (gg * gg)
        m_hat = nm / (1.0 - B1 ** ASTEP)
        v_hat = nv / (1.0 - B2 ** ASTEP)
        d_ref[...] = -LR * (m_hat / (jnp.sqrt(v_hat) + AEPS) + WD * w_ref[...])
        nm_ref[...] = nm
        nv_ref[...] = nv

    blk = pl.BlockSpec((tm, C), lambda i: (i, 0))
    return pl.pallas_call(
        body, name=name, grid=(R // tm,), in_specs=[blk] * 4, out_specs=[blk] * 3,
        out_shape=[jax.ShapeDtypeStruct((R, C), F32)] * 3, compiler_params=_params("parallel"),
    )(w, g, m, v)


def _mod_cols(c_all, w_ada, b_cols, *, name):
    nl, D, cols = w_ada.shape
    B = c_all.shape[0]

    def body(c_ref, w_ref, b_ref, o_ref):
        cc = c_ref[...]
        cs = (cc * _sigmoid(cc)).astype(BF16)
        o_ref[0] = jnp.dot(cs, w_ref[0].astype(BF16), preferred_element_type=F32) + b_ref[0]

    return pl.pallas_call(
        body, name=name, grid=(nl,),
        in_specs=[_whole(c_all.shape), pl.BlockSpec((1, D, cols), lambda i: (i, 0, 0)), pl.BlockSpec((1, 1, cols), lambda i: (i, 0, 0))],
        out_specs=pl.BlockSpec((1, B, cols), lambda i: (i, 0, 0)),
        out_shape=jax.ShapeDtypeStruct((nl, B, cols), F32), compiler_params=_params("arbitrary"),
    )(c_all, w_ada, b_cols)


def _ada_grad(c_all, dmod_cols, *, name):
    nl, B, cols = dmod_cols.shape
    D = c_all.shape[1]
    tn = (((0,), (0,)), ((), ()))

    def body(c_ref, d_ref, o_ref):
        cc = c_ref[...]
        cs = (cc * _sigmoid(cc)).astype(BF16)
        o_ref[0] = lax.dot_general(cs, d_ref[0].astype(BF16), tn, preferred_element_type=F32)

    return pl.pallas_call(
        body, name=name, grid=(nl,),
        in_specs=[_whole(c_all.shape), pl.BlockSpec((1, B, cols), lambda i: (i, 0, 0))],
        out_specs=pl.BlockSpec((1, D, cols), lambda i: (i, 0, 0)),
        out_shape=jax.ShapeDtypeStruct((nl, D, cols), F32), compiler_params=_params("arbitrary"),
    )(c_all, dmod_cols)


def _cast(x, dtype, *, name, tm=256):
    return _rows(lambda t: (t,), [x], [], [(x.shape[1], dtype)], [], name=name, tm=tm)[0]


def _s5_disc(a_re, a_im, log_dt, b_re, b_im):
    dt = jnp.exp(log_dt)[:, None]
    mag = jnp.exp(a_re * dt)
    ph = a_im * dt
    lb_re = mag * jnp.cos(ph)
    lb_im = mag * jnp.sin(ph)
    den = a_re * a_re + a_im * a_im
    nr = lb_re - 1.0
    ni = lb_im
    f_re = (nr * a_re + ni * a_im) / den
    f_im = (ni * a_re - nr * a_im) / den
    bb_re = f_re[..., None] * b_re - f_im[..., None] * b_im
    bb_im = f_re[..., None] * b_im + f_im[..., None] * b_re
    return lb_re, lb_im, bb_re, bb_im


def _to_segments(t):
    L, D = t.shape
    return t.reshape(NSEG, L // NSEG, D).transpose(1, 0, 2).reshape(L, D)


def _from_segments(t):
    L, D = t.shape
    return t.reshape(L // NSEG, NSEG, D).transpose(1, 0, 2).reshape(L, D)


def _mlp_fwd(h2, w1, w2, tag):
    apre, a2 = _matmul(h2, w1, name=f"ff1_{tag}", tk=1024, out_dtypes=(F32, BF16),
                       epi=lambda acc: (acc, jnp.square(jnp.maximum(acc, 0.0))))
    f = _matmul(a2, w2, name=f"ff2_{tag}", tk=1024)
    return apre, a2, f


def _mlp_bwd(df, h2, apre, a2, w1, w2, tag):
    da = _matmul(df, w2, tb=True, name=f"ff2_dx_{tag}", tk=1024, out_dtypes=(BF16,), epi_ins=(apre,),
                 epi=lambda acc, ap: (acc * (2.0 * jnp.maximum(ap, 0.0)),))
    dw2 = _matmul(a2, df, ta=True, name=f"ff2_dw_{tag}")
    dh2 = _matmul(da, w1, tb=True, name=f"ff1_dx_{tag}", tk=1024)
    dw1 = _matmul(h2, da, ta=True, name=f"ff1_dw_{tag}")
    return dh2, dw1, dw2


def kernel(x, c, w_ada, b_ada, norm_mix, norm_mlp, s5_a_re, s5_a_im, s5_log_dt, s5_b_re, s5_b_im, s5_c_re, s5_c_im, s5_d, s5_w_glu, gla_w_in, gla_w_gate2, gla_b_gate, gla_g_norm, gla_w_out, w_ff1, w_ff2, norm_final, loss_target, m_w_ada, m_b_ada, m_norm_mix, m_norm_mlp, m_s5_a_re, m_s5_a_im, m_s5_log_dt, m_s5_b_re, m_s5_b_im, m_s5_c_re, m_s5_c_im, m_s5_d, m_s5_w_glu, m_gla_w_in, m_gla_w_gate2, m_gla_b_gate, m_gla_g_norm, m_gla_w_out, m_w_ff1, m_w_ff2, m_norm_final, v_w_ada, v_b_ada, v_norm_mix, v_norm_mlp, v_s5_a_re, v_s5_a_im, v_s5_log_dt, v_s5_b_re, v_s5_b_im, v_s5_c_re, v_s5_c_im, v_s5_d, v_s5_w_glu, v_gla_w_in, v_gla_w_gate2, v_gla_b_gate, v_gla_g_norm, v_gla_w_out, v_w_ff1, v_w_ff2, v_norm_final):
    args = dict(locals())
    L, D = x.shape[1], x.shape[2]
    QK = D // 2
    FF = w_ff1.shape[2] * 4
    xi, yi, ci = lax.axis_index("x"), lax.axis_index("y"), lax.axis_index("c")
    chip = 2 * xi + yi
    dev = 2 * chip + ci
    NCH = 4

    c_all = _exchange(c.reshape(8, D // 8), MASK_ALL, scatter=False, name="gather_c").reshape(8, D)
    acols = w_ada.shape[2]
    b_cols = lax.dynamic_slice_in_dim(b_ada, chip * acols, acols, axis=1)[:, None, :]
    mod_cols = _mod_cols(c_all, w_ada, b_cols, name="ada_mod")
    mod_all = _exchange(mod_cols.reshape(16, acols), MASK_CHIPS, scatter=False, name="gather_mod")
    mod_all = mod_all.reshape(NCH, 2, 8, acols).transpose(1, 2, 0, 3).reshape(2, 8, NCH * acols)
    mod = lax.dynamic_index_in_dim(mod_all, dev, axis=1, keepdims=False).reshape(2, 6, 1, D)

    big = [("s5_w_glu", s5_w_glu[0], 1), ("gla_w_in", gla_w_in[0], 1), ("gla_w_out", gla_w_out[0], 0),
           ("w_ff1_0", w_ff1[0], 1), ("w_ff1_1", w_ff1[1], 1), ("w_ff2_0", w_ff2[0], 0), ("w_ff2_1", w_ff2[1], 0)]
    sizes = [t.size // D for _, t, _ in big]
    offs = [sum(sizes[:k]) for k in range(len(big))]
    rows_used = sum(sizes)
    rows_pad = -(-rows_used // 128) * 128
    half = rows_pad // 2
    flat = jnp.concatenate([t.reshape(-1, D) for _, t, _ in big] + [jnp.zeros((rows_pad - rows_used, D), F32)], axis=0)
    flat16 = _cast(flat, BF16, name="cast_w", tm=_tile_rows(rows_pad))
    mine16 = lax.dynamic_slice_in_dim(flat16, ci * half, half, axis=0)
    g1 = _exchange(mine16, MASK_CHIPS, scatter=False, name="gather_w_chips")
    g2 = _exchange(g1, MASK_C, scatter=False, name="gather_w_cores")
    wfull = g2.transpose(1, 0, 2, 3).reshape(NCH, rows_pad, D)
    W = {}
    for (nm, t, ax), o, s in zip(big, offs, sizes):
        sh = wfull[:, o:o + s, :].reshape((NCH,) + t.shape)
        W[nm] = sh.transpose(1, 0, 2).reshape(t.shape[0], NCH * t.shape[1]) if ax == 1 else sh.reshape(NCH * t.shape[0], t.shape[1])
    w_in = W["gla_w_in"]
    w_in_r = jnp.concatenate([w_in[:, :4 * QK], w_in[:, 4 * QK + GATE_RANK:], w_in[:, 4 * QK:4 * QK + GATE_RANK],
                              jnp.zeros((D, 128 - GATE_RANK), BF16)], axis=1)

    cat = jnp.concatenate([gla_w_gate2[0].reshape(1, -1), gla_b_gate, gla_g_norm], axis=1)
    cat_all = _exchange(jnp.tile(cat, (8, 1)), MASK_CHIPS, scatter=False, name="gather_gla_small")[:, 0, :]
    qk4 = QK // NCH
    wg2 = cat_all[:, :GATE_RANK * qk4].reshape(NCH, GATE_RANK, qk4).transpose(1, 0, 2).reshape(GATE_RANK, QK)
    bg = cat_all[:, GATE_RANK * qk4:(GATE_RANK + 1) * qk4].reshape(1, QK)
    gn = cat_all[:, (GATE_RANK + 1) * qk4:].reshape(1, D)
    wg2p = jnp.concatenate([wg2, jnp.zeros((128 - GATE_RANK, QK), F32)], axis=0).astype(BF16)

    lb_re, lb_im, bb_re, bb_im = _s5_disc(s5_a_re[0], s5_a_im[0], s5_log_dt[0], s5_b_re[0], s5_b_im[0])
    tb = _s5_tables(lb_re, lb_im, bb_re, bb_im, s5_c_re[0], s5_c_im[0], L // NSEG)
    lamT_conj = tb["lamT"]
    zero_carry = jnp.zeros_like(tb["lam"])

    def vec(t):
        return t.reshape(1, -1)

    xp = _to_segments(x[0])
    m0, m1 = mod[0], mod[1]
    (u0,) = _rows(lambda t, g, sc, sh: (_norm_mod(t, g, sc, sh),), [xp], [vec(norm_mix[0]), m0[1], m0[0]],
                  [(D, F32)], [], name="pre_mix0")
    (ends,) = _s5_fwd(u0, zero_carry, tb, s5_d, emit=False, name="s5_fwd_ends")
    carry0 = _s5_fix(ends, tb["lamT"], reverse=False, name="s5_fix_fwd")
    y0, z0, ck0 = _s5_fwd(u0, carry0, tb, s5_d, emit=True, name="s5_fwd")
    vg0 = _matmul(z0, W["s5_w_glu"], name="glu", tk=1024)

    def res_glu_pre(xt, vgt, gt, g, sc, sh):
        xn = xt + gt * (vgt[:, :D] * _sigmoid(vgt[:, D:]))
        return xn, _norm_mod(xn, g, sc, sh)

    x2_0, h2_0 = _rows(res_glu_pre, [xp, vg0], [m0[2], vec(norm_mlp[0]), m0[4], m0[3]], [(D, F32), (D, BF16)], [],
                       name="res_mix0")
    apre0, a2_0, f0 = _mlp_fwd(h2_0, W["w_ff1_0"], W["w_ff2_0"], "0")

    def res_pre(xt, bt, gt, g, sc, sh):
        xn = xt + gt * bt
        return xn, _norm_mod(xn, g, sc, sh)

    x3p, h1p = _rows(res_pre, [x2_0, f0], [m0[5], vec(norm_mix[1]), m1[1], m1[0]], [(D, F32), (D, BF16)], [],
                     name="res_mlp0")
    x3 = _from_segments(x3p)
    h1 = _from_segments(h1p)
    proj = _matmul(h1, w_in_r, name="gla_in", tk=1024, tn=640)
    og, states = _gla_fwd(proj, wg2p, bg, gn, name="gla_fwd")
    ymix = _matmul(og, W["gla_w_out"], name="gla_out", tk=1024)
    x2_1, h2_1 = _rows(res_pre, [x3, ymix], [m1[2], vec(norm_mlp[1]), m1[4], m1[3]], [(D, F32), (D, BF16)], [],
                       name="res_mix1")
    apre1, a2_1, f1 = _mlp_fwd(h2_1, W["w_ff1_1"], W["w_ff2_1"], "1")

    def final(xt, ft, tgt, gt, g):
        xn = xt + gt * ft
        rs = lax.rsqrt(jnp.mean(xn * xn, axis=-1, keepdims=True) + EPS)
        xh = xn * rs
        e = xh * g - tgt
        dout = e * (1.0 / D)
        dxh = dout * g
        dx = rs * (dxh - xh * jnp.mean(dxh * xh, axis=-1, keepdims=True))
        lsum = 0.5 * jnp.sum(jnp.sum(e * e, axis=-1, keepdims=True), axis=0, keepdims=True) * (1.0 / D)
        return dx, jnp.broadcast_to(lsum, (1, 128)), _rsum(dout * xh)

    dx, loss_part, d_norm_final = _rows(final, [x2_1, f1, loss_target[0]], [m1[5], vec(norm_final)], [(D, F32)],
                                        [(1, 128), (1, D)], name="loss_head")
    loss = lax.psum(loss_part[0, 0], ("x", "y", "c"))

    def gate_bwd(dxt, bt, gt):
        return dxt * gt, _rsum(dxt * bt)

    def norm_bwd(xt, dht, drt, g, sc):
        dxn, dsh, dsc, dg = _norm_mod_bwd(xt, dht, g, sc)
        return drt + dxn, dsh, dsc, dg

    def norm_gate_bwd(xt, dht, drt, bt, g, sc, gt):
        dxn, dsh, dsc, dg = _norm_mod_bwd(xt, dht, g, sc)
        dxt = drt + dxn
        return dxt, dxt * gt, dsh, dsc, dg, _rsum(dxt * bt)

    vD = [(1, D)]
    df1, dgt2_1 = _rows(gate_bwd, [dx, f1], [m1[5]], [(D, BF16)], vD, name="gate_mlp1")
    dh2_1, dw_ff1_1, dw_ff2_1 = _mlp_bwd(df1, h2_1, apre1, a2_1, W["w_ff1_1"], W["w_ff2_1"], "1")
    dx, dmix1, dsh2_1, dsc2_1, dg_mlp1, dgt1_1 = _rows(
        norm_gate_bwd, [x2_1, dh2_1, dx, ymix], [vec(norm_mlp[1]), m1[4], m1[2]], [(D, F32), (D, BF16)], vD * 4,
        name="norm_mlp1_bwd")
    dog = _matmul(dmix1, W["gla_w_out"], tb=True, name="gla_out_dx", tk=1024)
    dw_out = _matmul(og, dmix1, ta=True, name="gla_out_dw")
    dproj, dwg2p, dbg, dgn = _gla_bwd(proj, dog, states, wg2p, bg, gn, name="gla_bwd")
    dh1 = _matmul(dproj, w_in_r, tb=True, name="gla_in_dx", tk=640)
    dw_in_r = _matmul(h1, dproj, ta=True, name="gla_in_dw", tn=640)
    dx, dsh1_1, dsc1_1, dg_mix1 = _rows(norm_bwd, [x3, dh1, dx], [vec(norm_mix[1]), m1[1]], [(D, F32)], vD * 3,
                                        name="norm_mix1_bwd")
    dxp = _to_segments(dx)
    df0, dgt2_0 = _rows(gate_bwd, [dxp, f0], [m0[5]], [(D, BF16)], vD, name="gate_mlp0")
    dh2_0, dw_ff1_0, dw_ff2_0 = _mlp_bwd(df0, h2_0, apre0, a2_0, W["w_ff1_0"], W["w_ff2_0"], "0")

    def norm_glu_bwd(xt, dht, drt, vgt, g, sc, gt):
        dxn, dsh, dsc, dg = _norm_mod_bwd(xt, dht, g, sc)
        dxt = drt + dxn
        val, sg = vgt[:, :D], _sigmoid(vgt[:, D:])
        dbr = dxt * gt
        dvg = jnp.concatenate([dbr * sg, dbr * val * sg * (1.0 - sg)], axis=1)
        return dxt, dvg, dsh, dsc, dg, _rsum(dxt * val * sg)

    dxp, dvg0, dsh2_0, dsc2_0, dg_mlp0, dgt1_0 = _rows(
        norm_glu_bwd, [x2_0, dh2_0, dxp, vg0], [vec(norm_mlp[0]), m0[4], m0[2]], [(D, F32), (2 * D, BF16)], vD * 4,
        name="norm_mlp0_bwd")
    dz0 = _matmul(dvg0, W["s5_w_glu"], tb=True, name="glu_dx", tk=1024)
    dw_glu = _matmul(z0, dvg0, ta=True, name="glu_dw")
    (gends,) = _s5_bwd(u0, y0, dz0, ck0, zero_carry, tb, s5_d, emit=False, name="s5_bwd_ends")
    gcarry0 = _s5_fix(gends, lamT_conj, reverse=True, name="s5_fix_bwd")
    du0, db_acc, dc_acc, dl_acc, dd_s5 = _s5_bwd(u0, y0, dz0, ck0, gcarry0, tb, s5_d, emit=True, name="s5_bwd")
    dxp, dsh1_0, dsc1_0, dg_mix0 = _rows(norm_bwd, [xp, du0, dxp], [vec(norm_mix[0]), m0[1]], [(D, F32)], vD * 3,
                                         name="norm_mix0_bwd")
    grad_x = _from_segments(dxp)[None]

    dmod = jnp.concatenate([dsh1_0, dsc1_0, dgt1_0, dsh2_0, dsc2_0, dgt2_0,
                            dsh1_1, dsc1_1, dgt1_1, dsh2_1, dsc2_1, dgt2_1], axis=1)
    dbb_re, dbb_im = _s5_untable(db_acc)
    dc_re, dc_im_neg = _s5_untable(dc_acc)
    nbk = D // 128
    dl = dl_acc.reshape(nbk, NSEG, 2, GPB * S5_P).sum(axis=1)
    smalls = [dmod, dg_mix0, dg_mix1, dg_mlp0, dg_mlp1, d_norm_final, dd_s5, dbg, dgn,
              dwg2p[:GATE_RANK].reshape(1, -1), dbb_re.reshape(1, -1), dbb_im.reshape(1, -1),
              dc_re.reshape(1, -1), dc_im_neg.reshape(1, -1), dl.reshape(1, -1)]
    ssz = [t.shape[1] for t in smalls]
    stot = sum(ssz)
    spad = -(-stot // 1024) * 1024
    svec = jnp.concatenate(smalls + [jnp.zeros((1, spad - stot), F32)], axis=1).reshape(spad // 128, 128)
    s_all = _exchange(svec, MASK_ALL, scatter=False, name="gather_small_grads")
    s_sum = _sum_slots(s_all, F32, name="sum_small_grads", tm=spad // 128).reshape(-1)
    so = [sum(ssz[:k]) for k in range(len(ssz))]
    sm = [s_sum[o:o + n] for o, n in zip(so, ssz)]
    (dmod_s, g_mix0, g_mix1, g_mlp0, g_mlp1, g_nf, g_d, g_bg, g_gn, g_wg2, g_bbre, g_bbim, g_cre, g_cimn, g_dl) = sm

    dmod_all = _exchange(dmod.reshape(12 * D // 128, 128), MASK_ALL, scatter=False, name="gather_dmod").reshape(8, 2, 6 * D)
    dmod_cols = lax.dynamic_slice_in_dim(dmod_all, chip * acols, acols, axis=2).transpose(1, 0, 2)
    g_w_ada = _ada_grad(c_all, dmod_cols, name="ada_grad")
    g_b_ada = dmod_s.reshape(2, 6 * D)

    G = D // S5_H
    _, disc_vjp = jax.vjp(_s5_disc, s5_a_re[0], s5_a_im[0], s5_log_dt[0], s5_b_re[0], s5_b_im[0])
    g_dl = g_dl.reshape(nbk, 2, GPB, S5_P)
    ct = (g_dl[:, 0].reshape(G, S5_P), g_dl[:, 1].reshape(G, S5_P),
          g_bbre.reshape(G, S5_H, S5_P).transpose(0, 2, 1), g_bbim.reshape(G, S5_H, S5_P).transpose(0, 2, 1))
    g_a_re, g_a_im, g_log_dt, g_b_re, g_b_im = disc_vjp(ct)
    g_c_re = g_cre.reshape(G, S5_H, S5_P)
    g_c_im = -g_cimn.reshape(G, S5_H, S5_P)

    dw_in = jnp.concatenate([dw_in_r[:, :4 * QK], dw_in_r[:, 6 * QK:6 * QK + GATE_RANK], dw_in_r[:, 4 * QK:6 * QK]], axis=1)
    gbig = [dw_glu, dw_in, dw_out, dw_ff1_0, dw_ff1_1, dw_ff2_0, dw_ff2_1]
    pieces = []
    for (nm, t, ax), gfull in zip(big, gbig):
        if ax == 1:
            pieces.append(gfull.reshape(t.shape[0], NCH, t.shape[1]).transpose(1, 0, 2).reshape(NCH, -1, D))
        else:
            pieces.append(gfull.reshape(NCH, -1, D))
    gflat = jnp.concatenate(pieces + [jnp.zeros((NCH, rows_pad - rows_used, D), F32)], axis=1)
    gsrc = gflat.reshape(NCH, 2, half, D).transpose(1, 0, 2, 3).reshape(2, NCH * half, D)
    r1 = _exchange(gsrc, MASK_C, scatter=True, name="rs_cores")
    s1 = _sum_slots(r1, BF16, name="rs_sum_cores", tm=_tile_rows(NCH * half)).reshape(NCH, half, D)
    r2 = _exchange(s1, MASK_CHIPS, scatter=True, name="rs_chips")
    s2 = _sum_slots(r2, F32, name="rs_sum_chips", tm=_tile_rows(half))
    r3 = _exchange(s2, MASK_C, scatter=False, name="rs_gather_cores").reshape(rows_pad, D)
    gsh = {nm: r3[o:o + s].reshape(t.shape) for (nm, t, ax), o, s in zip(big, offs, sizes)}
    g_w_ff1 = jnp.stack([gsh["w_ff1_0"], gsh["w_ff1_1"]])
    g_w_ff2 = jnp.stack([gsh["w_ff2_0"], gsh["w_ff2_1"]])
    g_wg2_s = lax.dynamic_slice_in_dim(g_wg2.reshape(GATE_RANK, QK), chip * qk4, qk4, axis=1)
    g_bg_s = lax.dynamic_slice_in_dim(g_bg.reshape(1, QK), chip * qk4, qk4, axis=1)
    g_gn_s = lax.dynamic_slice_in_dim(g_gn.reshape(1, D), chip * (D // NCH), D // NCH, axis=1)

    grads = dict(
        w_ada=g_w_ada, b_ada=g_b_ada, norm_mix=jnp.stack([g_mix0, g_mix1]), norm_mlp=jnp.stack([g_mlp0, g_mlp1]),
        s5_a_re=g_a_re[None], s5_a_im=g_a_im[None], s5_log_dt=g_log_dt[None], s5_b_re=g_b_re[None], s5_b_im=g_b_im[None],
        s5_c_re=g_c_re[None], s5_c_im=g_c_im[None], s5_d=g_d[None], s5_w_glu=gsh["s5_w_glu"][None],
        gla_w_in=gsh["gla_w_in"][None], gla_w_gate2=g_wg2_s[None], gla_b_gate=g_bg_s, gla_g_norm=g_gn_s,
        gla_w_out=gsh["gla_w_out"][None], w_ff1=g_w_ff1, w_ff2=g_w_ff2, norm_final=g_nf)

    names = list(grads)
    large = ("w_ada", "s5_w_glu", "gla_w_in", "gla_w_out", "w_ff1", "w_ff2")
    delta, new_m, new_v = {}, {}, {}
    for nm in large:
        w = args[nm]
        shp = w.shape
        two = lambda t: t.reshape(-1, shp[-1])
        d, m2, v2 = _adamw(two(w), two(grads[nm]), two(args["m_" + nm]), two(args["v_" + nm]), name=f"adamw_{nm}")
        delta[nm], new_m[nm], new_v[nm] = d.reshape(shp), m2.reshape(shp), v2.reshape(shp)
    small = [nm for nm in names if nm not in large]
    szs = [args[nm].size for nm in small]
    tot = sum(szs)
    pad = -(-tot // 1024) * 1024

    def pack(prefix, src):
        return jnp.concatenate([src[prefix + nm].reshape(-1) for nm in small] + [jnp.ones((pad - tot,), F32)]).reshape(-1, 128)

    d, m2, v2 = _adamw(pack("", args), pack("", grads), pack("m_", args), pack("v_", args), name="adamw_small", tm=pad // 128)
    o = 0
    for nm, n in zip(small, szs):
        shp = args[nm].shape
        delta[nm], new_m[nm], new_v[nm] = (t.reshape(-1)[o:o + n].reshape(shp) for t in (d, m2, v2))
        o += n
    grads = {nm: grads[nm].reshape(args[nm].shape) for nm in names}
    return (loss, grad_x, *[grads[n] for n in names], *[delta[n] for n in names], *[new_m[n] for n in names],
            *[new_v[n] for n in names])


def _tile_rows(rows, cap=512):
    best = 8
    for t in range(8, cap + 1, 8):
        if rows % t == 0:
            best = t
    return best
```

```python
import functools
import math

import jax
import jax.numpy as jnp
from jax import lax
from jax.experimental import pallas as pl
from jax.experimental.pallas import tpu as pltpu

F32 = jnp.float32
BF16 = jnp.bfloat16
MESH = pl.DeviceIdType.MESH
HI = lax.Precision.HIGHEST

EPS = 1e-6
CHUNK = 64
S5_H = 16
S5_P = 64
GPB = 8
NSEG = 8
HEADS = 4
GATE_RANK = 16
GATE_TAU = 16.0
LR, B1, B2, AEPS, WD, ASTEP = 0.001, 0.9, 0.999, 1e-08, 0.01, 10
VMEM_LIMIT = 56 << 20

MASK_C = ((0, 0, 1),)
MASK_CHIPS = ((1, 0, 0), (0, 1, 0), (1, 1, 0))
MASK_ALL = ((0, 0, 1), (0, 1, 0), (0, 1, 1), (1, 0, 0), (1, 0, 1), (1, 1, 0), (1, 1, 1))


def _params(*sem):
    return pltpu.CompilerParams(dimension_semantics=sem or None, vmem_limit_bytes=VMEM_LIMIT)


def _matmul(a, b, *, name, ta=False, tb=False, tm=1024, tn=1024, tk=1024, out_dtypes=(F32,),
            a_fn=None, epi=None, epi_ins=(), col_shards=1):
    M, K = (a.shape[1], a.shape[0]) if ta else a.shape
    N = b.shape[0] if tb else b.shape[1]
    tm, tn, tk = min(tm, M), min(tn, N), min(tk, K)
    assert M % tm == 0 and N % tn == 0 and K % tk == 0, (name, M, N, K)
    nk = K // tk
    ne = len(epi_ins)
    dn = (((0 if ta else 1,), (1 if tb else 0,)), ((), ()))

    def body(a_ref, b_ref, *rest):
        e_refs, o_refs, acc = rest[:ne], rest[ne:-1], rest[-1]
        k = pl.program_id(2)

        @pl.when(k == 0)
        def _():
            acc[...] = jnp.zeros_like(acc)

        at = a_ref[...]
        if a_fn is not None:
            at = a_fn(at)
        acc[...] += lax.dot_general(at.astype(BF16), b_ref[...].astype(BF16), dn, preferred_element_type=F32)

        @pl.when(k == nk - 1)
        def _():
            outs = (acc[...],) if epi is None else epi(acc[...], *[r[...] for r in e_refs])
            for r, o in zip(o_refs, outs):
                r[...] = o.astype(r.dtype)

    a_spec = pl.BlockSpec((tk, tm), lambda i, j, k: (k, i)) if ta else pl.BlockSpec((tm, tk), lambda i, j, k: (i, k))
    b_spec = pl.BlockSpec((tn, tk), lambda i, j, k: (j, k)) if tb else pl.BlockSpec((tk, tn), lambda i, j, k: (k, j))
    o_spec = pl.BlockSpec((tm, tn), lambda i, j, k: (i, j))
    if col_shards > 1:
        per = N // col_shards // tn
        assert ne == 0 and per * tn * col_shards == N
        w_spec = pl.BlockSpec((None, tm, tn), lambda i, j, k: (j // per, i, j % per))
        o_shape = (col_shards, M, N // col_shards)
    else:
        w_spec, o_shape = o_spec, (M, N)
    outs = pl.pallas_call(
        body, name=name, grid=(M // tm, N // tn, nk),
        in_specs=[a_spec, b_spec] + [o_spec] * ne,
        out_specs=[w_spec] * len(out_dtypes),
        out_shape=[jax.ShapeDtypeStruct(o_shape, d) for d in out_dtypes],
        scratch_shapes=[pltpu.VMEM((tm, tn), F32)],
        compiler_params=_params("parallel", "parallel", "arbitrary"),
    )(a, b, *epi_ins)
    return outs[0] if len(outs) == 1 else outs


def _rows(fn, rows_in, vecs_in, rows_out, acc_out, *, name, tm=256):
    L = rows_in[0].shape[0]
    tm = min(tm, L)
    assert L % tm == 0
    nr, nv, no, na = len(rows_in), len(vecs_in), len(rows_out), len(acc_out)

    def body(*refs):
        rin, vin = refs[:nr], refs[nr:nr + nv]
        rout, aout = refs[nr + nv:nr + nv + no], refs[nr + nv + no:]
        outs = fn(*[r[...] for r in rin], *[v[...] for v in vin])
        for r, o in zip(rout, outs[:no]):
            r[...] = o.astype(r.dtype)
        if na:
            @pl.when(pl.program_id(0) == 0)
            def _():
                for r in aout:
                    r[...] = jnp.zeros_like(r)

            for r, o in zip(aout, outs[no:]):
                r[...] += o

    def whole(shape):
        return pl.BlockSpec(shape, lambda i, _n=len(shape): (0,) * _n)

    outs = pl.pallas_call(
        body, name=name, grid=(L // tm,),
        in_specs=[pl.BlockSpec((tm, r.shape[1]), lambda i: (i, 0)) for r in rows_in] + [whole(v.shape) for v in vecs_in],
        out_specs=[pl.BlockSpec((tm, c), lambda i: (i, 0)) for c, _ in rows_out] + [whole(s) for s in acc_out],
        out_shape=[jax.ShapeDtypeStruct((L, c), d) for c, d in rows_out] + [jax.ShapeDtypeStruct(s, F32) for s in acc_out],
        compiler_params=_params("arbitrary"),
    )(*rows_in, *vecs_in)
    return outs


def _rsum(t):
    return jnp.sum(t, axis=0, keepdims=True)


def _norm_mod(x, g, sc, sh):
    rs = lax.rsqrt(jnp.mean(x * x, axis=-1, keepdims=True) + EPS)
    return x * rs * g * (1.0 + sc) + sh


def _norm_mod_bwd(x, dh, g, sc):
    rs = lax.rsqrt(jnp.mean(x * x, axis=-1, keepdims=True) + EPS)
    xh = x * rs
    dn = dh * (1.0 + sc)
    dxh = dn * g
    dx = rs * (dxh - xh * jnp.mean(dxh * xh, axis=-1, keepdims=True))
    return dx, _rsum(dh), _rsum(dh * xh * g), _rsum(dn * xh)


def _sigmoid(x):
    return jax.nn.sigmoid(x)


def _gelu(y):
    return jax.nn.gelu(y, approximate=True)


def _gelu_grad(y):
    c = math.sqrt(2.0 / math.pi)
    t = jnp.tanh(c * (y + 0.044715 * y * y * y))
    return 0.5 * (1.0 + t) + 0.5 * y * (1.0 - t * t) * c * (1.0 + 3.0 * 0.044715 * y * y)


def _s5_tables(lb_re, lb_im, bb_re, bb_im, c_re, c_im, seg_len):
    G = lb_re.shape[0]
    nb = G // GPB
    eye = jnp.eye(GPB, dtype=F32)

    def bdiag(t):
        a, b = t.shape[1:]
        t = t.reshape(nb, GPB, a, b)
        return (t[:, :, :, None, :] * eye[None, :, None, :, None]).reshape(nb, GPB * a, GPB * b)

    bbd = jnp.concatenate([bdiag(bb_re.transpose(0, 2, 1)), bdiag(bb_im.transpose(0, 2, 1))], axis=2)
    cbd = jnp.concatenate([bdiag(c_re.transpose(0, 2, 1)), -bdiag(c_im.transpose(0, 2, 1))], axis=1)

    def lanes(re, im):
        t = jnp.concatenate([re.reshape(nb, GPB * S5_P), im.reshape(nb, GPB * S5_P)], axis=1)
        return jnp.repeat(t, NSEG, axis=0)

    tr, ti = lb_re, lb_im
    for _ in range(int(math.log2(seg_len))):
        tr, ti = tr * tr - ti * ti, 2.0 * tr * ti
    return dict(bbd=bbd.astype(BF16), bbdT=bbd.transpose(0, 2, 1).astype(BF16), cbd=cbd.astype(BF16),
                cbdT=cbd.transpose(0, 2, 1).astype(BF16), lam=lanes(lb_re, lb_im), lamT=lanes(tr, ti))


def _s5_untable(acc):
    nb = acc.shape[0]
    t = acc.reshape(nb, GPB, S5_H, 2, GPB, S5_P)
    d = jnp.diagonal(t, axis1=1, axis2=4)
    d = d.transpose(0, 4, 2, 1, 3).reshape(nb * GPB, 2, S5_H, S5_P)
    return d[:, 0], d[:, 1]


def _s5_fix(ends, lamT, *, reverse, name):
    nrow = ends.shape[0]
    half = ends.shape[1] // 2

    def body(e_ref, t_ref, o_ref):
        for gb in range(nrow // NSEG):
            r0 = gb * NSEG
            tr, ti = t_ref[r0:r0 + 1, :half], t_ref[r0:r0 + 1, half:]
            cr = jnp.zeros((1, half), F32)
            ci = jnp.zeros((1, half), F32)
            order = range(NSEG - 1, -1, -1) if reverse else range(NSEG)
            for n, s in enumerate(order):
                if n > 0:
                    p = s + 1 if reverse else s - 1
                    er, ei = e_ref[r0 + p:r0 + p + 1, :half], e_ref[r0 + p:r0 + p + 1, half:]
                    if reverse:
                        cr, ci = tr * cr + ti * ci + er, tr * ci - ti * cr + ei
                    else:
                        cr, ci = tr * cr - ti * ci + er, tr * ci + ti * cr + ei
                o_ref[r0 + s:r0 + s + 1, :half] = cr
                o_ref[r0 + s:r0 + s + 1, half:] = ci

    return pl.pallas_call(body, name=name, out_shape=jax.ShapeDtypeStruct(ends.shape, F32),
                          compiler_params=_params())(ends, lamT)


def _whole(shape):
    return pl.BlockSpec(shape, lambda i, _n=len(shape): (0,) * _n)


def _s5_fwd(up, carry_in, tb, dvec, *, emit, name, R=256):
    L, D = up.shape
    R = min(R, L)
    nb, ta, ngb = L // R, R // NSEG, D // 128
    SW = GPB * S5_P
    crows = ngb * NSEG

    def body(u_ref, cin_ref, lam_ref, b_ref, c_ref, d_ref, *rest):
        if emit:
            y_ref, z_ref, ck_ref, carry, xbuf = rest
        else:
            cout_ref, carry, xbuf = rest
        i = pl.program_id(0)

        @pl.when(i == 0)
        def _():
            carry[...] = cin_ref[...]

        if emit:
            ck_ref[0] = carry[...]
        for gb in range(ngb):
            cols = slice(gb * 128, (gb + 1) * 128)
            rws = slice(gb * NSEG, (gb + 1) * NSEG)
            ug = u_ref[:, cols]
            xbuf[...] = jnp.dot(ug.astype(BF16), b_ref[gb], preferred_element_type=F32)
            lr, li = lam_ref[rws, :SW], lam_ref[rws, SW:]

            def step(a, c, lr=lr, li=li):
                cr, ci = c
                o = pl.multiple_of(a * NSEG, NSEG)
                nr = lr * cr - li * ci + xbuf[pl.ds(o, NSEG), :SW]
                ni = lr * ci + li * cr + xbuf[pl.ds(o, NSEG), SW:]
                if emit:
                    xbuf[pl.ds(o, NSEG), :SW] = nr
                    xbuf[pl.ds(o, NSEG), SW:] = ni
                return nr, ni

            cr, ci = lax.fori_loop(0, ta, step, (carry[rws, :SW], carry[rws, SW:]), unroll=2)
            carry[rws, :SW] = cr
            carry[rws, SW:] = ci
            if emit:
                y = jnp.dot(xbuf[...].astype(BF16), c_ref[gb], preferred_element_type=F32) + d_ref[:, cols] * ug
                y_ref[:, cols] = y
                z_ref[:, cols] = _gelu(y).astype(BF16)
        if not emit:
            cout_ref[...] = carry[...]

    rowblk = pl.BlockSpec((R, D), lambda i: (i, 0))
    if emit:
        out_shape = [jax.ShapeDtypeStruct((L, D), F32), jax.ShapeDtypeStruct((L, D), BF16),
                     jax.ShapeDtypeStruct((nb, crows, 2 * SW), F32)]
        out_specs = [rowblk, rowblk, pl.BlockSpec((1, crows, 2 * SW), lambda i: (i, 0, 0))]
    else:
        out_shape = [jax.ShapeDtypeStruct((crows, 2 * SW), F32)]
        out_specs = [_whole((crows, 2 * SW))]
    return pl.pallas_call(
        body, name=name, grid=(nb,),
        in_specs=[rowblk, _whole(carry_in.shape), _whole(tb["lam"].shape), _whole(tb["bbd"].shape),
                  _whole(tb["cbd"].shape), _whole(dvec.shape)],
        out_specs=out_specs, out_shape=out_shape,
        scratch_shapes=[pltpu.VMEM((crows, 2 * SW), F32), pltpu.VMEM((R, 2 * SW), F32)],
        compiler_params=_params("arbitrary"),
    )(up, carry_in, tb["lam"], tb["bbd"], tb["cbd"], dvec)


def _s5_bwd(up, y, dz, ck, gcarry_in, tb, dvec, *, emit, name, R=256):
    L, D = up.shape
    R = min(R, L)
    nb, ta, ngb = L // R, R // NSEG, D // 128
    SW = GPB * S5_P
    crows = ngb * NSEG

    def body(u_ref, y_ref, dz_ref, ck_ref, gin_ref, lam_ref, b_ref, bt_ref, ct_ref, d_ref, *rest):
        if emit:
            du_ref, db_ref, dc_ref, dl_ref, dd_ref, gcarry, xbuf, gbuf, dybuf = rest
        else:
            gout_ref, gcarry, gbuf, dybuf = rest
        i = pl.program_id(0)

        @pl.when(i == 0)
        def _():
            gcarry[...] = gin_ref[...]
            if emit:
                db_ref[...] = jnp.zeros_like(db_ref)
                dc_ref[...] = jnp.zeros_like(dc_ref)
                dl_ref[...] = jnp.zeros_like(dl_ref)
                dd_ref[...] = jnp.zeros_like(dd_ref)

        dybuf[...] = dz_ref[...] * _gelu_grad(y_ref[...])
        for gb in range(ngb):
            cols = slice(gb * 128, (gb + 1) * 128)
            rws = slice(gb * NSEG, (gb + 1) * NSEG)
            dyg = dybuf[:, cols]
            lr, li = lam_ref[rws, :SW], lam_ref[rws, SW:]
            gbuf[...] = jnp.dot(dyg.astype(BF16), ct_ref[gb], preferred_element_type=F32)
            if emit:
                ug = u_ref[:, cols]
                xbuf[0:NSEG, :] = ck_ref[0, rws, :]
                xbuf[NSEG:, :] = jnp.dot(ug.astype(BF16), b_ref[gb], preferred_element_type=F32)

                def fstep(a, c, lr=lr, li=li):
                    cr, ci = c
                    o = pl.multiple_of(a * NSEG + NSEG, NSEG)
                    nr = lr * cr - li * ci + xbuf[pl.ds(o, NSEG), :SW]
                    ni = lr * ci + li * cr + xbuf[pl.ds(o, NSEG), SW:]
                    xbuf[pl.ds(o, NSEG), :SW] = nr
                    xbuf[pl.ds(o, NSEG), SW:] = ni
                    return nr, ni

                lax.fori_loop(0, ta, fstep, (xbuf[0:NSEG, :SW], xbuf[0:NSEG, SW:]), unroll=2)

            def rstep(k, c, lr=lr, li=li):
                o = pl.multiple_of((ta - 1 - k) * NSEG, NSEG)
                gr_n, gi_n = c[0], c[1]
                gr = gbuf[pl.ds(o, NSEG), :SW] + lr * gr_n + li * gi_n
                gi = gbuf[pl.ds(o, NSEG), SW:] - li * gr_n + lr * gi_n
                if not emit:
                    return gr, gi
                gbuf[pl.ds(o, NSEG), :SW] = gr
                gbuf[pl.ds(o, NSEG), SW:] = gi
                xr, xi = xbuf[pl.ds(o, NSEG), :SW], xbuf[pl.ds(o, NSEG), SW:]
                return gr, gi, c[2] + gr * xr + gi * xi, c[3] + gi * xr - gr * xi

            c0 = (gcarry[rws, :SW], gcarry[rws, SW:])
            if emit:
                c0 = c0 + (jnp.zeros((NSEG, SW), F32), jnp.zeros((NSEG, SW), F32))
            cf = lax.fori_loop(0, ta, rstep, c0, unroll=2)
            gcarry[rws, :SW] = cf[0]
            gcarry[rws, SW:] = cf[1]
            if emit:
                dl_ref[rws, :SW] += cf[2]
                dl_ref[rws, SW:] += cf[3]
                gb16 = gbuf[...].astype(BF16)
                du_ref[:, cols] = jnp.dot(gb16, bt_ref[gb], preferred_element_type=F32) + d_ref[:, cols] * dyg
                tn = (((0,), (0,)), ((), ()))
                db_ref[gb] += lax.dot_general(ug.astype(BF16), gb16, tn, preferred_element_type=F32)
                dc_ref[gb] += lax.dot_general(dyg.astype(BF16), xbuf[NSEG:, :].astype(BF16), tn,
                                              preferred_element_type=F32)
                dd_ref[:, cols] += _rsum(dyg * ug)
        if not emit:
            gout_ref[...] = gcarry[...]

    rev = pl.BlockSpec((R, D), lambda i: (nb - 1 - i, 0))
    acc3 = (ngb, 128, 2 * SW)
    if emit:
        out_shape = [jax.ShapeDtypeStruct((L, D), F32), jax.ShapeDtypeStruct(acc3, F32), jax.ShapeDtypeStruct(acc3, F32),
                     jax.ShapeDtypeStruct((crows, 2 * SW), F32), jax.ShapeDtypeStruct((1, D), F32)]
        out_specs = [rev, _whole(acc3), _whole(acc3), _whole((crows, 2 * SW)), _whole((1, D))]
        scratch = [pltpu.VMEM((crows, 2 * SW), F32), pltpu.VMEM((R + NSEG, 2 * SW), F32),
                   pltpu.VMEM((R, 2 * SW), F32), pltpu.VMEM((R, D), F32)]
    else:
        out_shape = [jax.ShapeDtypeStruct((crows, 2 * SW), F32)]
        out_specs = [_whole((crows, 2 * SW))]
        scratch = [pltpu.VMEM((crows, 2 * SW), F32), pltpu.VMEM((R, 2 * SW), F32), pltpu.VMEM((R, D), F32)]
    return pl.pallas_call(
        body, name=name, grid=(nb,),
        in_specs=[rev, rev, rev, pl.BlockSpec((1, crows, 2 * SW), lambda i: (nb - 1 - i, 0, 0)),
                  _whole(gcarry_in.shape), _whole(tb["lam"].shape), _whole(tb["bbd"].shape),
                  _whole(tb["bbdT"].shape), _whole(tb["cbdT"].shape), _whole(dvec.shape)],
        out_specs=out_specs, out_shape=out_shape, scratch_shapes=scratch,
        compiler_params=_params("arbitrary"),
    )(up, y, dz, ck, gcarry_in, tb["lam"], tb["bbd"], tb["bbdT"], tb["cbdT"], dvec)


def _log_sigmoid(x):
    return jnp.minimum(x, 0.0) - jnp.log(1.0 + jnp.exp(-jnp.abs(x)))


def _gla_gates(p, wg_ref, bg_ref, QK):
    C = p.shape[0]
    glr = p[:, 6 * QK:6 * QK + 128].astype(BF16)
    gpre = jnp.dot(glr, wg_ref[...], preferred_element_type=F32) + bg_ref[...]
    la = _log_sigmoid(gpre) * (1.0 / GATE_TAU)
    row = lax.broadcasted_iota(jnp.int32, (C, C), 0)
    col = lax.broadcasted_iota(jnp.int32, (C, C), 1)
    gc = jnp.dot((row >= col).astype(F32), la, precision=HI, preferred_element_type=F32)
    ge = gc[C - 1:C, :]
    w = jnp.exp(ge - gc)
    return glr, gpre, la, ge, w


def _gla_fwd(proj, wg2p, bg, gn, *, name):
    L = proj.shape[0]
    QK = wg2p.shape[1]
    DK, DV = QK // HEADS, 2 * QK // HEADS
    nC = L // CHUNK
    scale = DK ** -0.5
    tn = (((0,), (0,)), ((), ()))

    def body(p_ref, wg_ref, bg_ref, gn_ref, og_ref, s_ref, sst):
        @pl.when(pl.program_id(0) == 0)
        def _():
            sst[...] = jnp.zeros_like(sst)

        p = p_ref[...]
        _, _, la, _, w = _gla_gates(p, wg_ref, bg_ref, QK)
        ones = jnp.ones((CHUNK, DV), F32)
        for h in range(HEADS):
            q = p[:, h * DK:(h + 1) * DK] * scale
            kd = p[:, QK + h * DK:QK + (h + 1) * DK] * w[:, h * DK:(h + 1) * DK]
            v = p[:, 2 * QK + h * DV:2 * QK + (h + 1) * DV]
            r = p[:, 4 * QK + h * DV:4 * QK + (h + 1) * DV]
            dec = jnp.exp(lax.dot_general(la[:, h * DK:(h + 1) * DK], ones, tn, precision=HI, preferred_element_type=F32))
            kv = lax.dot_general(kd.astype(BF16), v.astype(BF16), tn, preferred_element_type=F32)
            S = dec * sst[h * DK:(h + 1) * DK, :] + kv
            sst[h * DK:(h + 1) * DK, :] = S
            s_ref[0, h * DK:(h + 1) * DK, :] = S
            o = jnp.dot(q.astype(BF16), S.astype(BF16), preferred_element_type=F32)
            on = o * lax.rsqrt(jnp.mean(o * o, axis=-1, keepdims=True) + EPS)
            og_ref[:, h * DV:(h + 1) * DV] = (on * gn_ref[:, h * DV:(h + 1) * DV] * (r * _sigmoid(r))).astype(BF16)

    return pl.pallas_call(
        body, name=name, grid=(nC,),
        in_specs=[pl.BlockSpec((CHUNK, proj.shape[1]), lambda i: (i, 0)), _whole(wg2p.shape), _whole(bg.shape), _whole(gn.shape)],
        out_specs=[pl.BlockSpec((CHUNK, 2 * QK), lambda i: (i, 0)), pl.BlockSpec((1, QK, DV), lambda i: (i, 0, 0))],
        out_shape=[jax.ShapeDtypeStruct((L, 2 * QK), BF16), jax.ShapeDtypeStruct((nC, QK, DV), F32)],
        scratch_shapes=[pltpu.VMEM((QK, DV), F32)],
        compiler_params=_params("arbitrary"),
    )(proj, wg2p, bg, gn)


def _gla_bwd(proj, dog, states, wg2p, bg, gn, *, name):
    L, W = proj.shape
    QK = wg2p.shape[1]
    DK, DV = QK // HEADS, 2 * QK // HEADS
    nC = L // CHUNK
    scale = DK ** -0.5
    tn = (((0,), (0,)), ((), ()))
    nt = (((1,), (1,)), ((), ()))

    def body(p_ref, dog_ref, sc_ref, sp_ref, wg_ref, bg_ref, gn_ref, dp_ref, dwg_ref, dbg_ref, dgn_ref, gst):
        i = pl.program_id(0)

        @pl.when(i == 0)
        def _():
            gst[...] = jnp.zeros_like(gst)
            dwg_ref[...] = jnp.zeros_like(dwg_ref)
            dbg_ref[...] = jnp.zeros_like(dbg_ref)
            dgn_ref[...] = jnp.zeros_like(dgn_ref)

        has_prev = i < nC - 1
        p = p_ref[...]
        glr, gpre, la, ge, w = _gla_gates(p, wg_ref, bg_ref, QK)
        row = lax.broadcasted_iota(jnp.int32, (CHUNK, CHUNK), 0)
        col = lax.broadcasted_iota(jnp.int32, (CHUNK, CHUNK), 1)
        tri_u = (col >= row).astype(F32)
        ones = jnp.ones((CHUNK, DV), F32)
        ones8 = jnp.ones((8, DV), F32)
        dla_heads = []
        for h in range(HEADS):
            ks, vs = slice(h * DK, (h + 1) * DK), slice(h * DV, (h + 1) * DV)
            qs = p[:, h * DK:(h + 1) * DK] * scale
            k = p[:, QK + h * DK:QK + (h + 1) * DK]
            v = p[:, 2 * QK + h * DV:2 * QK + (h + 1) * DV]
            r = p[:, 4 * QK + h * DV:4 * QK + (h + 1) * DV]
            wh = w[:, ks]
            kd = k * wh
            S = sc_ref[0, ks, :]
            Sp = jnp.where(has_prev, sp_ref[0, ks, :], 0.0)
            o = jnp.dot(qs.astype(BF16), S.astype(BF16), preferred_element_type=F32)
            rs = lax.rsqrt(jnp.mean(o * o, axis=-1, keepdims=True) + EPS)
            on = o * rs
            sr = _sigmoid(r)
            dg = dog_ref[:, vs]
            gnh = gn_ref[:, vs]
            dp_ref[:, 4 * QK + h * DV:4 * QK + (h + 1) * DV] = (dg * on * gnh * (sr * (1.0 + r * (1.0 - sr)))).astype(BF16)
            dt = dg * (r * sr)
            dgn_ref[:, vs] += _rsum(dt * on)
            don = dt * gnh
            do = (rs * (don - on * jnp.mean(don * on, axis=-1, keepdims=True))).astype(BF16)
            Gc = gst[ks, :] + lax.dot_general(qs.astype(BF16), do, tn, preferred_element_type=F32)
            G16 = Gc.astype(BF16)
            dp_ref[:, h * DK:(h + 1) * DK] = (lax.dot_general(do, S.astype(BF16), nt, preferred_element_type=F32) * scale).astype(BF16)
            dkd = lax.dot_general(v.astype(BF16), G16, nt, preferred_element_type=F32)
            dp_ref[:, 2 * QK + h * DV:2 * QK + (h + 1) * DV] = jnp.dot(kd.astype(BF16), G16, preferred_element_type=F32).astype(BF16)
            dec_col = jnp.exp(lax.dot_general(la[:, ks], ones, tn, precision=HI, preferred_element_type=F32))
            gst[ks, :] = dec_col * Gc
            ddec = lax.dot_general(ones8, Gc * Sp, nt, precision=HI, preferred_element_type=F32)[0:1, :]
            dp_ref[:, QK + h * DK:QK + (h + 1) * DK] = (dkd * wh).astype(BF16)
            dww = dkd * kd
            dge = jnp.exp(ge[:, ks]) * ddec + _rsum(dww)
            dla_heads.append(dge - jnp.dot(tri_u, dww, precision=HI, preferred_element_type=F32))
        dla = jnp.concatenate(dla_heads, axis=1)
        dgpre = dla * (1.0 / GATE_TAU) * (1.0 - _sigmoid(gpre))
        d16 = dgpre.astype(BF16)
        dp_ref[:, 6 * QK:6 * QK + 128] = lax.dot_general(d16, wg_ref[...], nt, preferred_element_type=F32).astype(BF16)
        dwg_ref[...] += lax.dot_general(glr, d16, tn, preferred_element_type=F32)
        dbg_ref[...] += _rsum(dgpre)

    rev = lambda i: (nC - 1 - i, 0)
    return pl.pallas_call(
        body, name=name, grid=(nC,),
        in_specs=[pl.BlockSpec((CHUNK, W), rev), pl.BlockSpec((CHUNK, 2 * QK), rev),
                  pl.BlockSpec((1, QK, DV), lambda i: (nC - 1 - i, 0, 0)),
                  pl.BlockSpec((1, QK, DV), lambda i: (jnp.maximum(nC - 2 - i, 0), 0, 0)),
                  _whole(wg2p.shape), _whole(bg.shape), _whole(gn.shape)],
        out_specs=[pl.BlockSpec((CHUNK, W), rev), _whole((128, QK)), _whole((1, QK)), _whole((1, 2 * QK))],
        out_shape=[jax.ShapeDtypeStruct((L, W), BF16), jax.ShapeDtypeStruct((128, QK), F32),
                   jax.ShapeDtypeStruct((1, QK), F32), jax.ShapeDtypeStruct((1, 2 * QK), F32)],
        scratch_shapes=[pltpu.VMEM((QK, DV), F32)],
        compiler_params=_params("arbitrary"),
    )(proj, dog, states, states, wg2p, bg, gn)


def _exchange(src, masks, *, scatter, name):
    vary = [any(m[k] for m in masks) for k in range(3)]
    nslots = 2 ** sum(vary)
    blk = src.shape[1:] if scatter else src.shape
    n = len(masks)

    def slot(coords):
        s = 0
        for k in range(3):
            if vary[k]:
                s = s * 2 + coords[k]
        return s

    def body(src_ref, dst_ref, send_sems, recv_sems, loc_sem):
        me = (lax.axis_index("x"), lax.axis_index("y"), lax.axis_index("c"))
        mine = slot(me)
        loc = pltpu.make_async_copy(src_ref.at[mine] if scatter else src_ref, dst_ref.at[mine], loc_sem)
        loc.start()
        copies = []
        for k, m in enumerate(masks):
            peer = tuple(1 - me[d] if m[d] else me[d] for d in range(3))
            cp = pltpu.make_async_remote_copy(
                src_ref=src_ref.at[slot(peer)] if scatter else src_ref, dst_ref=dst_ref.at[mine],
                send_sem=send_sems.at[k], recv_sem=recv_sems.at[k], device_id=peer, device_id_type=MESH)
            cp.start()
            copies.append(cp)
        for cp in copies:
            cp.wait()
        loc.wait()

    return pl.pallas_call(
        body, name=name,
        in_specs=[pl.BlockSpec(memory_space=pl.ANY)], out_specs=pl.BlockSpec(memory_space=pl.ANY),
        out_shape=jax.ShapeDtypeStruct((nslots,) + tuple(blk), src.dtype),
        scratch_shapes=[pltpu.SemaphoreType.DMA((n,)), pltpu.SemaphoreType.DMA((n,)), pltpu.SemaphoreType.DMA(())],
    )(src)


def _sum_slots(parts, out_dtype, *, name, tm=256):
    n, R, C = parts.shape
    tm = min(tm, R)
    assert R % tm == 0

    def body(p_ref, o_ref):
        t = p_ref[0].astype(F32)
        for s in range(1, n):
            t = t + p_ref[s].astype(F32)
        o_ref[...] = t.astype(o_ref.dtype)

    return pl.pallas_call(
        body, name=name, grid=(R // tm,),
        in_specs=[pl.BlockSpec((n, tm, C), lambda i: (0, i, 0))], out_specs=pl.BlockSpec((tm, C), lambda i: (i, 0)),
        out_shape=jax.ShapeDtypeStruct((R, C), out_dtype), compiler_params=_params("parallel"),
    )(parts)


def _coords():
    return lax.axis_index("x"), lax.axis_index("y"), lax.axis_index("c")


def _other_chips(x, y):
    return [(1 - x, y, 2 * (1 - x) + y), (x, 1 - y, 2 * x + 1 - y), (1 - x, 1 - y, 2 * (1 - x) + 1 - y)]


def _hbm_call(body, ins, out_shapes, n_sems, *, name):
    any_spec = pl.BlockSpec(memory_space=pl.ANY)
    return pl.pallas_call(
        body, name=name, in_specs=[any_spec] * len(ins), out_specs=[any_spec] * len(out_shapes), out_shape=out_shapes,
        scratch_shapes=[pltpu.SemaphoreType.DMA((n,)) for n in n_sems],
    )(*ins)


def _gather_weights(shards, kinds, *, name):
    n = len(shards)
    outs = []
    for s, kd in zip(shards, kinds):
        r, cc = s.shape
        shp = {"col": (r, 4 * cc), "row": (4 * r, cc), "slot": (4, r, cc)}[kd]
        outs.append(jax.ShapeDtypeStruct(shp, s.dtype))

    def body(*refs):
        src, dst = refs[:n], refs[n:2 * n]
        send_sems, recv_sems, loc_sems = refs[2 * n:]
        x, y, c = _coords()
        chip = 2 * x + y
        others = _other_chips(x, y)
        sib = (x, y, 1 - c)

        def window(p, chip_id, cc):
            r, cols = shards[p].shape
            h = r // 2
            if kinds[p] == "col":
                return dst[p].at[pl.ds(cc * h, h), pl.ds(pl.multiple_of(chip_id * cols, 128), cols)]
            if kinds[p] == "row":
                return dst[p].at[pl.ds(chip_id * r + cc * h, h), :]
            return dst[p].at[chip_id, pl.ds(cc * h, h), :]

        def copy(p, k, s_ref, d_ref, to):
            return pltpu.make_async_remote_copy(src_ref=s_ref, dst_ref=d_ref, send_sem=send_sems.at[7 * p + k],
                                                recv_sem=recv_sems.at[7 * p + k], device_id=to, device_id_type=MESH)

        locs, sends = [], []
        for p in range(n):
            h = shards[p].shape[0] // 2
            mine = src[p].at[pl.ds(c * h, h), :]
            lc = pltpu.make_async_copy(mine, window(p, chip, c), loc_sems.at[p])
            lc.start()
            locs.append(lc)
            first = [copy(p, 0, mine, window(p, chip, c), sib)]
            first += [copy(p, 1 + j, mine, window(p, chip, c), (ox, oy, c)) for j, (ox, oy, _) in enumerate(others)]
            for cp in first:
                cp.start()
            sends += first
        for j, (ox, oy, oc) in enumerate(others):
            for p in range(n):
                landed = window(p, oc, c)
                copy(p, 1 + j, landed, landed, (x, y, c)).wait_recv()
                fw = copy(p, 4 + j, landed, landed, sib)
                fw.start()
                sends.append(fw)
        for p in range(n):
            w0 = window(p, chip, 1 - c)
            copy(p, 0, w0, w0, sib).wait_recv()
            for j, (_, _, oc) in enumerate(others):
                wj = window(p, oc, 1 - c)
                copy(p, 4 + j, wj, wj, sib).wait_recv()
        for cp in sends:
            cp.wait_send()
        for lc in locs:
            lc.wait()

    return _hbm_call(body, shards, outs, (7 * n, 7 * n, n), name=name)


def _rs_cores(grads, *, name):
    n = len(grads)
    outs = [jax.ShapeDtypeStruct((g.shape[0], g.shape[1] // 2, g.shape[2]), g.dtype) for g in grads]

    def body(*refs):
        src, dst = refs[:n], refs[n:2 * n]
        send_sems, recv_sems = refs[2 * n:]
        x, y, c = _coords()
        copies = []
        for p in range(n):
            nsh, r, _ = grads[p].shape
            h = r // 2
            for j in range(nsh):
                cp = pltpu.make_async_remote_copy(
                    src_ref=src[p].at[j, pl.ds((1 - c) * h, h), :], dst_ref=dst[p].at[j],
                    send_sem=send_sems.at[nsh * p + j], recv_sem=recv_sems.at[nsh * p + j],
                    device_id=(x, y, 1 - c), device_id_type=MESH)
                cp.start()
                copies.append(cp)
        for cp in copies:
            cp.wait()

    tot = sum(g.shape[0] for g in grads)
    return _hbm_call(body, grads, outs, (tot, tot), name=name)


def _rs_chips(parts, *, name):
    n = len(parts)
    outs = [jax.ShapeDtypeStruct(t.shape, t.dtype) for t in parts]

    def body(*refs):
        src, dst = refs[:n], refs[n:2 * n]
        send_sems, recv_sems, loc_sems = refs[2 * n:]
        x, y, c = _coords()
        chip = 2 * x + y
        copies, locs = [], []
        for p in range(n):
            lc = pltpu.make_async_copy(src[p].at[chip], dst[p].at[chip], loc_sems.at[p])
            lc.start()
            locs.append(lc)
            for j, (ox, oy, oc) in enumerate(_other_chips(x, y)):
                cp = pltpu.make_async_remote_copy(
                    src_ref=src[p].at[oc], dst_ref=dst[p].at[chip], send_sem=send_sems.at[3 * p + j],
                    recv_sem=recv_sems.at[3 * p + j], device_id=(ox, oy, c), device_id_type=MESH)
                cp.start()
                copies.append(cp)
        for cp in copies:
            cp.wait()
        for lc in locs:
            lc.wait()

    return _hbm_call(body, parts, outs, (3 * n, 3 * n, n), name=name)


def _rs_gather(halves, *, name, nchunk=4):
    n = len(halves)
    outs = [jax.ShapeDtypeStruct((2 * t.shape[0], t.shape[1]), t.dtype) for t in halves]

    def body(*refs):
        src, dst = refs[:n], refs[n:2 * n]
        send_sems, recv_sems, loc_sems = refs[2 * n:]
        x, y, c = _coords()
        copies, locs = [], []
        for p in range(n):
            h = halves[p].shape[0]
            q = h // nchunk
            lc = pltpu.make_async_copy(src[p], dst[p].at[pl.ds(c * h, h), :], loc_sems.at[p])
            lc.start()
            locs.append(lc)
            for k in range(nchunk):
                cp = pltpu.make_async_remote_copy(
                    src_ref=src[p].at[pl.ds(k * q, q), :], dst_ref=dst[p].at[pl.ds(c * h + k * q, q), :],
                    send_sem=send_sems.at[nchunk * p + k], recv_sem=recv_sems.at[nchunk * p + k],
                    device_id=(x, y, 1 - c), device_id_type=MESH)
                cp.start()
                copies.append(cp)
        for cp in copies:
            cp.wait()
        for lc in locs:
            lc.wait()

    return _hbm_call(body, halves, outs, (nchunk * n, nchunk * n, n), name=name)


def _sum_own_half(full, recv, ci, *, name, tm=256):
    nsh, h, cols = recv.shape
    tm = min(tm, h)
    assert h % tm == 0
    nblk = h // tm

    def body(c_ref, f_ref, r_ref, o_ref):
        o_ref[...] = (f_ref[...] + r_ref[...]).astype(o_ref.dtype)

    return pl.pallas_call(
        body, name=name,
        grid_spec=pltpu.PrefetchScalarGridSpec(
            num_scalar_prefetch=1, grid=(nsh, nblk),
            in_specs=[pl.BlockSpec((1, tm, cols), lambda j, i, c_ref: (j, c_ref[0] * nblk + i, 0)),
                      pl.BlockSpec((1, tm, cols), lambda j, i, c_ref: (j, i, 0))],
            out_specs=pl.BlockSpec((1, tm, cols), lambda j, i, c_ref: (j, i, 0))),
        out_shape=jax.ShapeDtypeStruct((nsh, h, cols), BF16), compiler_params=_params("parallel", "parallel"),
    )(ci.reshape(1).astype(jnp.int32), full, recv)


def _adamw(w, g, m, v, *, name, tm=256):
    R, C = w.shape
    tm = tm if (R % tm == 0 and R > tm) else R

    def body(w_ref, g_ref, m_ref, v_ref, d_ref, nm_ref, nv_ref):
        gg = g_ref[...]
        nm = B1 * m_ref[...] + (1.0 - B1) * gg
        nv = B2 * v_ref[...] + (1.0 - B2) * (gg * gg)
        m_hat = nm / (1.0 - B1 ** ASTEP)
        v_hat = nv / (1.0 - B2 ** ASTEP)
        d_ref[...] = -LR * (m_hat / (jnp.sqrt(v_hat) + AEPS) + WD * w_ref[...])
        nm_ref[...] = nm
        nv_ref[...] = nv

    blk = pl.BlockSpec((tm, C), lambda i: (i, 0))
    return pl.pallas_call(
        body, name=name, grid=(R // tm,), in_specs=[blk] * 4, out_specs=[blk] * 3,
        out_shape=[jax.ShapeDtypeStruct((R, C), F32)] * 3, compiler_params=_params("parallel"),
    )(w, g, m, v)


def _mod_cols(c_all, w_ada, b_cols, *, name):
    nl, D, cols = w_ada.shape
    B = c_all.shape[0]

    def body(c_ref, w_ref, b_ref, o_ref):
        cc = c_ref[...]
        cs = (cc * _sigmoid(cc)).astype(BF16)
        o_ref[0] = jnp.dot(cs, w_ref[0].astype(BF16), preferred_element_type=F32) + b_ref[0]

    return pl.pallas_call(
        body, name=name, grid=(nl,),
        in_specs=[_whole(c_all.shape), pl.BlockSpec((1, D, cols), lambda i: (i, 0, 0)), pl.BlockSpec((1, 1, cols), lambda i: (i, 0, 0))],
        out_specs=pl.BlockSpec((1, B, cols), lambda i: (i, 0, 0)),
        out_shape=jax.ShapeDtypeStruct((nl, B, cols), F32), compiler_params=_params("arbitrary"),
    )(c_all, w_ada, b_cols)


def _ada_grad(c_all, dmod_cols, *, name):
    nl, B, cols = dmod_cols.shape
    D = c_all.shape[1]
    tn = (((0,), (0,)), ((), ()))

    def body(c_ref, d_ref, o_ref):
        cc = c_ref[...]
        cs = (cc * _sigmoid(cc)).astype(BF16)
        o_ref[0] = lax.dot_general(cs, d_ref[0].astype(BF16), tn, preferred_element_type=F32)

    return pl.pallas_call(
        body, name=name, grid=(nl,),
        in_specs=[_whole(c_all.shape), pl.BlockSpec((1, B, cols), lambda i: (i, 0, 0))],
        out_specs=pl.BlockSpec((1, D, cols), lambda i: (i, 0, 0)),
        out_shape=jax.ShapeDtypeStruct((nl, D, cols), F32), compiler_params=_params("arbitrary"),
    )(c_all, dmod_cols)


def _cast(x, dtype, *, name, tm=256):
    return _rows(lambda t: (t,), [x], [], [(x.shape[1], dtype)], [], name=name, tm=tm)[0]


def _s5_disc(a_re, a_im, log_dt, b_re, b_im):
    dt = jnp.exp(log_dt)[:, None]
    mag = jnp.exp(a_re * dt)
    ph = a_im * dt
    lb_re = mag * jnp.cos(ph)
    lb_im = mag * jnp.sin(ph)
    den = a_re * a_re + a_im * a_im
    nr = lb_re - 1.0
    ni = lb_im
    f_re = (nr * a_re + ni * a_im) / den
    f_im = (ni * a_re - nr * a_im) / den
    bb_re = f_re[..., None] * b_re - f_im[..., None] * b_im
    bb_im = f_re[..., None] * b_im + f_im[..., None] * b_re
    return lb_re, lb_im, bb_re, bb_im


def _to_segments(t):
    L, D = t.shape
    return t.reshape(NSEG, L // NSEG, D).transpose(1, 0, 2).reshape(L, D)


def _from_segments(t):
    L, D = t.shape
    return t.reshape(L // NSEG, NSEG, D).transpose(1, 0, 2).reshape(L, D)


def _mlp_fwd(h2, w1, w2, tag):
    a = _matmul(h2, w1, name=f"ff1_{tag}", out_dtypes=(BF16,), epi=lambda acc: (jnp.maximum(acc, 0.0),))
    f = _matmul(a, w2, name=f"ff2_{tag}", a_fn=jnp.square)
    return a, f


def _mlp_bwd(df, h2, a, w1, w2, tag):
    da = _matmul(df, w2, tb=True, name=f"ff2_dx_{tag}", out_dtypes=(BF16,), epi_ins=(a,),
                 epi=lambda acc, at: (acc * (2.0 * at.astype(F32)),))
    dw2 = _matmul(a, df, ta=True, name=f"ff2_dw_{tag}", a_fn=jnp.square)
    dh2 = _matmul(da, w1, tb=True, name=f"ff1_dx_{tag}")
    dw1 = _matmul(h2, da, ta=True, name=f"ff1_dw_{tag}", col_shards=4)
    return dh2, dw1, dw2


def kernel(x, c, w_ada, b_ada, norm_mix, norm_mlp, s5_a_re, s5_a_im, s5_log_dt, s5_b_re, s5_b_im, s5_c_re, s5_c_im, s5_d, s5_w_glu, gla_w_in, gla_w_gate2, gla_b_gate, gla_g_norm, gla_w_out, w_ff1, w_ff2, norm_final, loss_target, m_w_ada, m_b_ada, m_norm_mix, m_norm_mlp, m_s5_a_re, m_s5_a_im, m_s5_log_dt, m_s5_b_re, m_s5_b_im, m_s5_c_re, m_s5_c_im, m_s5_d, m_s5_w_glu, m_gla_w_in, m_gla_w_gate2, m_gla_b_gate, m_gla_g_norm, m_gla_w_out, m_w_ff1, m_w_ff2, m_norm_final, v_w_ada, v_b_ada, v_norm_mix, v_norm_mlp, v_s5_a_re, v_s5_a_im, v_s5_log_dt, v_s5_b_re, v_s5_b_im, v_s5_c_re, v_s5_c_im, v_s5_d, v_s5_w_glu, v_gla_w_in, v_gla_w_gate2, v_gla_b_gate, v_gla_g_norm, v_gla_w_out, v_w_ff1, v_w_ff2, v_norm_final):
    args = dict(locals())
    L, D = x.shape[1], x.shape[2]
    QK = D // 2
    FF = w_ff1.shape[2] * 4
    xi, yi, ci = lax.axis_index("x"), lax.axis_index("y"), lax.axis_index("c")
    chip = 2 * xi + yi
    dev = 2 * chip + ci
    NCH = 4

    c_all = _exchange(c.reshape(8, D // 8), MASK_ALL, scatter=False, name="gather_c").reshape(8, D)
    acols = w_ada.shape[2]
    b_cols = lax.dynamic_slice_in_dim(b_ada, chip * acols, acols, axis=1)[:, None, :]
    mod_cols = _mod_cols(c_all, w_ada, b_cols, name="ada_mod")
    mod_all = _exchange(mod_cols.reshape(16, acols), MASK_CHIPS, scatter=False, name="gather_mod")
    mod_all = mod_all.reshape(NCH, 2, 8, acols).transpose(1, 2, 0, 3).reshape(2, 8, NCH * acols)
    mod = lax.dynamic_index_in_dim(mod_all, dev, axis=1, keepdims=False).reshape(2, 6, 1, D)

    big = [("s5_w_glu", s5_w_glu[0], 1), ("gla_w_in", gla_w_in[0], 1), ("gla_w_out", gla_w_out[0], 0),
           ("w_ff1_0", w_ff1[0], 1), ("w_ff1_1", w_ff1[1], 1), ("w_ff2_0", w_ff2[0], 0), ("w_ff2_1", w_ff2[1], 0)]
    kinds = ["slot" if nm == "gla_w_in" else ("col" if ax == 1 else "row") for nm, _, ax in big]
    shards16 = [_cast(t, BF16, name=f"cast_{nm}") for nm, t, _ in big]
    W = dict(zip([nm for nm, _, _ in big], _gather_weights(shards16, kinds, name="gather_w")))
    w_in = W["gla_w_in"].transpose(1, 0, 2).reshape(D, -1)
    w_in_r = jnp.concatenate([w_in[:, :4 * QK], w_in[:, 4 * QK + GATE_RANK:], w_in[:, 4 * QK:4 * QK + GATE_RANK],
                              jnp.zeros((D, 128 - GATE_RANK), BF16)], axis=1)

    cat = jnp.concatenate([gla_w_gate2[0].reshape(1, -1), gla_b_gate, gla_g_norm], axis=1)
    cat_all = _exchange(jnp.tile(cat, (8, 1)), MASK_CHIPS, scatter=False, name="gather_gla_small")[:, 0, :]
    qk4 = QK // NCH
    wg2 = cat_all[:, :GATE_RANK * qk4].reshape(NCH, GATE_RANK, qk4).transpose(1, 0, 2).reshape(GATE_RANK, QK)
    bg = cat_all[:, GATE_RANK * qk4:(GATE_RANK + 1) * qk4].reshape(1, QK)
    gn = cat_all[:, (GATE_RANK + 1) * qk4:].reshape(1, D)
    wg2p = jnp.concatenate([wg2, jnp.zeros((128 - GATE_RANK, QK), F32)], axis=0).astype(BF16)

    lb_re, lb_im, bb_re, bb_im = _s5_disc(s5_a_re[0], s5_a_im[0], s5_log_dt[0], s5_b_re[0], s5_b_im[0])
    tb = _s5_tables(lb_re, lb_im, bb_re, bb_im, s5_c_re[0], s5_c_im[0], L // NSEG)
    lamT_conj = tb["lamT"]
    zero_carry = jnp.zeros_like(tb["lam"])

    def vec(t):
        return t.reshape(1, -1)

    xp = _to_segments(x[0])
    m0, m1 = mod[0], mod[1]
    (u0,) = _rows(lambda t, g, sc, sh: (_norm_mod(t, g, sc, sh),), [xp], [vec(norm_mix[0]), m0[1], m0[0]],
                  [(D, F32)], [], name="pre_mix0")
    (ends,) = _s5_fwd(u0, zero_carry, tb, s5_d, emit=False, name="s5_fwd_ends")
    carry0 = _s5_fix(ends, tb["lamT"], reverse=False, name="s5_fix_fwd")
    y0, z0, ck0 = _s5_fwd(u0, carry0, tb, s5_d, emit=True, name="s5_fwd")
    vg0 = _matmul(z0, W["s5_w_glu"], name="glu", tk=1024)

    def res_glu_pre(xt, vgt, gt, g, sc, sh):
        xn = xt + gt * (vgt[:, :D] * _sigmoid(vgt[:, D:]))
        return xn, _norm_mod(xn, g, sc, sh)

    x2_0, h2_0 = _rows(res_glu_pre, [xp, vg0], [m0[2], vec(norm_mlp[0]), m0[4], m0[3]], [(D, F32), (D, BF16)], [],
                       name="res_mix0")
    a_0, f0 = _mlp_fwd(h2_0, W["w_ff1_0"], W["w_ff2_0"], "0")

    def res_pre(xt, bt, gt, g, sc, sh):
        xn = xt + gt * bt
        return xn, _norm_mod(xn, g, sc, sh)

    x3p, h1p = _rows(res_pre, [x2_0, f0], [m0[5], vec(norm_mix[1]), m1[1], m1[0]], [(D, F32), (D, BF16)], [],
                     name="res_mlp0")
    x3 = _from_segments(x3p)
    h1 = _from_segments(h1p)
    proj = _matmul(h1, w_in_r, name="gla_in", tk=1024, tn=640)
    og, states = _gla_fwd(proj, wg2p, bg, gn, name="gla_fwd")
    ymix = _matmul(og, W["gla_w_out"], name="gla_out", tk=1024)
    x2_1, h2_1 = _rows(res_pre, [x3, ymix], [m1[2], vec(norm_mlp[1]), m1[4], m1[3]], [(D, F32), (D, BF16)], [],
                       name="res_mix1")
    a_1, f1 = _mlp_fwd(h2_1, W["w_ff1_1"], W["w_ff2_1"], "1")

    def final(xt, ft, tgt, gt, g):
        xn = xt + gt * ft
        rs = lax.rsqrt(jnp.mean(xn * xn, axis=-1, keepdims=True) + EPS)
        xh = xn * rs
        e = xh * g - tgt
        dout = e * (1.0 / D)
        dxh = dout * g
        dx = rs * (dxh - xh * jnp.mean(dxh * xh, axis=-1, keepdims=True))
        lsum = 0.5 * jnp.sum(jnp.sum(e * e, axis=-1, keepdims=True), axis=0, keepdims=True) * (1.0 / D)
        return dx, jnp.broadcast_to(lsum, (1, 128)), _rsum(dout * xh)

    dx, loss_part, d_norm_final = _rows(final, [x2_1, f1, loss_target[0]], [m1[5], vec(norm_final)], [(D, F32)],
                                        [(1, 128), (1, D)], name="loss_head")
    loss = lax.psum(loss_part[0, 0], ("x", "y", "c"))

    def gate_bwd(dxt, bt, gt):
        return dxt * gt, _rsum(dxt * bt)

    def norm_bwd(xt, dht, drt, g, sc):
        dxn, dsh, dsc, dg = _norm_mod_bwd(xt, dht, g, sc)
        return drt + dxn, dsh, dsc, dg

    def norm_gate_bwd(xt, dht, drt, bt, g, sc, gt):
        dxn, dsh, dsc, dg = _norm_mod_bwd(xt, dht, g, sc)
        dxt = drt + dxn
        return dxt, dxt * gt, dsh, dsc, dg, _rsum(dxt * bt)

    vD = [(1, D)]
    df1, dgt2_1 = _rows(gate_bwd, [dx, f1], [m1[5]], [(D, BF16)], vD, name="gate_mlp1")
    dh2_1, dw_ff1_1, dw_ff2_1 = _mlp_bwd(df1, h2_1, a_1, W["w_ff1_1"], W["w_ff2_1"], "1")
    dx, dmix1, dsh2_1, dsc2_1, dg_mlp1, dgt1_1 = _rows(
        norm_gate_bwd, [x2_1, dh2_1, dx, ymix], [vec(norm_mlp[1]), m1[4], m1[2]], [(D, F32), (D, BF16)], vD * 4,
        name="norm_mlp1_bwd")
    dog = _matmul(dmix1, W["gla_w_out"], tb=True, name="gla_out_dx", tk=1024)
    dw_out = _matmul(og, dmix1, ta=True, name="gla_out_dw")
    dproj, dwg2p, dbg, dgn = _gla_bwd(proj, dog, states, wg2p, bg, gn, name="gla_bwd")
    dh1 = _matmul(dproj, w_in_r, tb=True, name="gla_in_dx", tk=640)
    dw_in_r = _matmul(h1, dproj, ta=True, name="gla_in_dw", tn=640)
    dx, dsh1_1, dsc1_1, dg_mix1 = _rows(norm_bwd, [x3, dh1, dx], [vec(norm_mix[1]), m1[1]], [(D, F32)], vD * 3,
                                        name="norm_mix1_bwd")
    dxp = _to_segments(dx)
    df0, dgt2_0 = _rows(gate_bwd, [dxp, f0], [m0[5]], [(D, BF16)], vD, name="gate_mlp0")
    dh2_0, dw_ff1_0, dw_ff2_0 = _mlp_bwd(df0, h2_0, a_0, W["w_ff1_0"], W["w_ff2_0"], "0")

    def norm_glu_bwd(xt, dht, drt, vgt, g, sc, gt):
        dxn, dsh, dsc, dg = _norm_mod_bwd(xt, dht, g, sc)
        dxt = drt + dxn
        val, sg = vgt[:, :D], _sigmoid(vgt[:, D:])
        dbr = dxt * gt
        dvg = jnp.concatenate([dbr * sg, dbr * val * sg * (1.0 - sg)], axis=1)
        return dxt, dvg, dsh, dsc, dg, _rsum(dxt * val * sg)

    dxp, dvg0, dsh2_0, dsc2_0, dg_mlp0, dgt1_0 = _rows(
        norm_glu_bwd, [x2_0, dh2_0, dxp, vg0], [vec(norm_mlp[0]), m0[4], m0[2]], [(D, F32), (2 * D, BF16)], vD * 4,
        name="norm_mlp0_bwd")
    dz0 = _matmul(dvg0, W["s5_w_glu"], tb=True, name="glu_dx", tk=1024)
    dw_glu = _matmul(z0, dvg0, ta=True, name="glu_dw", tn=512, col_shards=4)
    (gends,) = _s5_bwd(u0, y0, dz0, ck0, zero_carry, tb, s5_d, emit=False, name="s5_bwd_ends")
    gcarry0 = _s5_fix(gends, lamT_conj, reverse=True, name="s5_fix_bwd")
    du0, db_acc, dc_acc, dl_acc, dd_s5 = _s5_bwd(u0, y0, dz0, ck0, gcarry0, tb, s5_d, emit=True, name="s5_bwd")
    dxp, dsh1_0, dsc1_0, dg_mix0 = _rows(norm_bwd, [xp, du0, dxp], [vec(norm_mix[0]), m0[1]], [(D, F32)], vD * 3,
                                         name="norm_mix0_bwd")
    grad_x = _from_segments(dxp)[None]

    dmod = jnp.concatenate([dsh1_0, dsc1_0, dgt1_0, dsh2_0, dsc2_0, dgt2_0,
                            dsh1_1, dsc1_1, dgt1_1, dsh2_1, dsc2_1, dgt2_1], axis=1)
    dbb_re, dbb_im = _s5_untable(db_acc)
    dc_re, dc_im_neg = _s5_untable(dc_acc)
    nbk = D // 128
    dl = dl_acc.reshape(nbk, NSEG, 2, GPB * S5_P).sum(axis=1)
    smalls = [dmod, dg_mix0, dg_mix1, dg_mlp0, dg_mlp1, d_norm_final, dd_s5, dbg, dgn,
              dwg2p[:GATE_RANK].reshape(1, -1), dbb_re.reshape(1, -1), dbb_im.reshape(1, -1),
              dc_re.reshape(1, -1), dc_im_neg.reshape(1, -1), dl.reshape(1, -1)]
    ssz = [t.shape[1] for t in smalls]
    stot = sum(ssz)
    spad = -(-stot // 1024) * 1024
    svec = jnp.concatenate(smalls + [jnp.zeros((1, spad - stot), F32)], axis=1).reshape(spad // 128, 128)
    s_all = _exchange(svec, MASK_ALL, scatter=False, name="gather_small_grads")
    s_sum = _sum_slots(s_all, F32, name="sum_small_grads", tm=spad // 128).reshape(-1)
    so = [sum(ssz[:k]) for k in range(len(ssz))]
    sm = [s_sum[o:o + n] for o, n in zip(so, ssz)]
    (dmod_s, g_mix0, g_mix1, g_mlp0, g_mlp1, g_nf, g_d, g_bg, g_gn, g_wg2, g_bbre, g_bbim, g_cre, g_cimn, g_dl) = sm

    dmod_all = _exchange(dmod.reshape(12 * D // 128, 128), MASK_ALL, scatter=False, name="gather_dmod").reshape(8, 2, 6 * D)
    dmod_cols = lax.dynamic_slice_in_dim(dmod_all, chip * acols, acols, axis=2).transpose(1, 0, 2)
    g_w_ada = _ada_grad(c_all, dmod_cols, name="ada_grad")
    g_b_ada = dmod_s.reshape(2, 6 * D)

    G = D // S5_H
    _, disc_vjp = jax.vjp(_s5_disc, s5_a_re[0], s5_a_im[0], s5_log_dt[0], s5_b_re[0], s5_b_im[0])
    g_dl = g_dl.reshape(nbk, 2, GPB, S5_P)
    ct = (g_dl[:, 0].reshape(G, S5_P), g_dl[:, 1].reshape(G, S5_P),
          g_bbre.reshape(G, S5_H, S5_P).transpose(0, 2, 1), g_bbim.reshape(G, S5_H, S5_P).transpose(0, 2, 1))
    g_a_re, g_a_im, g_log_dt, g_b_re, g_b_im = disc_vjp(ct)
    g_c_re = g_cre.reshape(G, S5_H, S5_P)
    g_c_im = -g_cimn.reshape(G, S5_H, S5_P)

    dw_in = jnp.concatenate([dw_in_r[:, :4 * QK], dw_in_r[:, 6 * QK:6 * QK + GATE_RANK], dw_in_r[:, 4 * QK:6 * QK]], axis=1)
    dw_in = dw_in.reshape(D, NCH, -1).transpose(1, 0, 2)
    gbig = [dw_glu, dw_in, dw_out.reshape(NCH, -1, D), dw_ff1_0, dw_ff1_1, dw_ff2_0.reshape(NCH, -1, D),
            dw_ff2_1.reshape(NCH, -1, D)]
    r1 = _rs_cores(gbig, name="rs_cores")
    s1 = [_sum_own_half(g, r, ci, name=f"rs_sum_cores_{nm}") for g, r, (nm, _, _) in zip(gbig, r1, big)]
    r2 = _rs_chips(s1, name="rs_chips")
    s2 = [_sum_slots(r, F32, name=f"rs_sum_chips_{nm}") for r, (nm, _, _) in zip(r2, big)]
    r3 = _rs_gather(s2, name="rs_gather_cores")
    gsh = {nm: r for r, (nm, _, _) in zip(r3, big)}
    g_w_ff1 = jnp.stack([gsh["w_ff1_0"], gsh["w_ff1_1"]])
    g_w_ff2 = jnp.stack([gsh["w_ff2_0"], gsh["w_ff2_1"]])
    g_wg2_s = lax.dynamic_slice_in_dim(g_wg2.reshape(GATE_RANK, QK), chip * qk4, qk4, axis=1)
    g_bg_s = lax.dynamic_slice_in_dim(g_bg.reshape(1, QK), chip * qk4, qk4, axis=1)
    g_gn_s = lax.dynamic_slice_in_dim(g_gn.reshape(1, D), chip * (D // NCH), D // NCH, axis=1)

    grads = dict(
        w_ada=g_w_ada, b_ada=g_b_ada, norm_mix=jnp.stack([g_mix0, g_mix1]), norm_mlp=jnp.stack([g_mlp0, g_mlp1]),
        s5_a_re=g_a_re[None], s5_a_im=g_a_im[None], s5_log_dt=g_log_dt[None], s5_b_re=g_b_re[None], s5_b_im=g_b_im[None],
        s5_c_re=g_c_re[None], s5_c_im=g_c_im[None], s5_d=g_d[None], s5_w_glu=gsh["s5_w_glu"][None],
        gla_w_in=gsh["gla_w_in"][None], gla_w_gate2=g_wg2_s[None], gla_b_gate=g_bg_s, gla_g_norm=g_gn_s,
        gla_w_out=gsh["gla_w_out"][None], w_ff1=g_w_ff1, w_ff2=g_w_ff2, norm_final=g_nf)

    names = list(grads)
    large = ("w_ada", "s5_w_glu", "gla_w_in", "gla_w_out", "w_ff1", "w_ff2")
    delta, new_m, new_v = {}, {}, {}
    for nm in large:
        w = args[nm]
        shp = w.shape
        two = lambda t: t.reshape(-1, shp[-1])
        d, m2, v2 = _adamw(two(w), two(grads[nm]), two(args["m_" + nm]), two(args["v_" + nm]), name=f"adamw_{nm}")
        delta[nm], new_m[nm], new_v[nm] = d.reshape(shp), m2.reshape(shp), v2.reshape(shp)
    small = [nm for nm in names if nm not in large]
    szs = [args[nm].size for nm in small]
    tot = sum(szs)
    pad = -(-tot // 1024) * 1024

    def pack(prefix, src):
        return jnp.concatenate([src[prefix + nm].reshape(-1) for nm in small] + [jnp.ones((pad - tot,), F32)]).reshape(-1, 128)

    d, m2, v2 = _adamw(pack("", args), pack("", grads), pack("m_", args), pack("v_", args), name="adamw_small", tm=pad // 128)
    o = 0
    for nm, n in zip(small, szs):
        shp = args[nm].shape
        delta[nm], new_m[nm], new_v[nm] = (t.reshape(-1)[o:o + n].reshape(shp) for t in (d, m2, v2))
        o += n
    grads = {nm: grads[nm].reshape(args[nm].shape) for nm in names}
    return (loss, grad_x, *[grads[n] for n in names], *[delta[n] for n in names], *[new_m[n] for n in names],
            *[new_v[n] for n in names])


def _tile_rows(rows, cap=512):
    best = 8
    for t in range(8, cap + 1, 8):
        if rows % t == 0:
            best = t
    return best
```

```python
import math

import jax
import jax.numpy as jnp
from jax import lax
from jax.experimental import pallas as pl
from jax.experimental.pallas import tpu as pltpu

F32 = jnp.float32
BF16 = jnp.bfloat16
MESH = pl.DeviceIdType.MESH

EPS = 1e-6
CHUNK = 64
GLA_NB = 4
S5_H = 16
S5_P = 64
GPB = 8
NSEG = 8
HEADS = 4
GATE_RANK = 16
GATE_TAU = 16.0
NCH = 4
LR, B1, B2, AEPS, WD, ASTEP = 0.001, 0.9, 0.999, 1e-08, 0.01, 10
VMEM_LIMIT = 56 << 20

MASK_CHIPS = ((1, 0, 0), (0, 1, 0), (1, 1, 0))
MASK_ALL = ((0, 0, 1), (0, 1, 0), (0, 1, 1), (1, 0, 0), (1, 0, 1), (1, 1, 0), (1, 1, 1))


def _params(*sem):
    return pltpu.CompilerParams(dimension_semantics=sem or None, vmem_limit_bytes=VMEM_LIMIT)


def _tile_rows(rows, cap=512):
    best = 8
    for t in range(8, cap + 1, 8):
        if rows % t == 0:
            best = t
    return best


def _whole(shape):
    return pl.BlockSpec(shape, lambda i, _n=len(shape): (0,) * _n)


def _matmul(a, b, *, name, ta=False, tb=False, tm=1024, tn=1024, tk=1024, out_dtypes=(F32,),
            a_fn=None, epi=None, epi_ins=(), col_shards=1):
    M, K = (a.shape[1], a.shape[0]) if ta else a.shape
    N = b.shape[0] if tb else b.shape[1]
    tm, tn, tk = min(tm, M), min(tn, N), min(tk, K)
    assert M % tm == 0 and N % tn == 0 and K % tk == 0, (name, M, N, K)
    nk = K // tk
    ne = len(epi_ins)
    dn = (((0 if ta else 1,), (1 if tb else 0,)), ((), ()))

    def body(a_ref, b_ref, *rest):
        e_refs, o_refs, acc = rest[:ne], rest[ne:-1], rest[-1]
        k = pl.program_id(2)

        @pl.when(k == 0)
        def _():
            acc[...] = jnp.zeros_like(acc)

        at = a_ref[...]
        if a_fn is not None:
            at = a_fn(at)
        acc[...] += lax.dot_general(at.astype(BF16), b_ref[...].astype(BF16), dn, preferred_element_type=F32)

        @pl.when(k == nk - 1)
        def _():
            outs = (acc[...],) if epi is None else epi(acc[...], *[r[...] for r in e_refs])
            for r, o in zip(o_refs, outs):
                r[...] = o.astype(r.dtype)

    a_spec = pl.BlockSpec((tk, tm), lambda i, j, k: (k, i)) if ta else pl.BlockSpec((tm, tk), lambda i, j, k: (i, k))
    b_spec = pl.BlockSpec((tn, tk), lambda i, j, k: (j, k)) if tb else pl.BlockSpec((tk, tn), lambda i, j, k: (k, j))
    o_spec = pl.BlockSpec((tm, tn), lambda i, j, k: (i, j))
    if col_shards > 1:
        per = N // col_shards // tn
        assert ne == 0 and per * tn * col_shards == N
        w_spec = pl.BlockSpec((None, tm, tn), lambda i, j, k: (j // per, i, j % per))
        o_shape = (col_shards, M, N // col_shards)
    else:
        w_spec, o_shape = o_spec, (M, N)
    outs = pl.pallas_call(
        body, name=name, grid=(M // tm, N // tn, nk),
        in_specs=[a_spec, b_spec] + [o_spec] * ne,
        out_specs=[w_spec] * len(out_dtypes),
        out_shape=[jax.ShapeDtypeStruct(o_shape, d) for d in out_dtypes],
        scratch_shapes=[pltpu.VMEM((tm, tn), F32)],
        compiler_params=_params("parallel", "parallel", "arbitrary"),
    )(a, b, *epi_ins)
    return outs[0] if len(outs) == 1 else outs


def _rows(fn, rows_in, vecs_in, rows_out, acc_out, *, name, tm=256):
    L = rows_in[0].shape[0]
    tm = min(tm, L)
    assert L % tm == 0
    nr, nv, no, na = len(rows_in), len(vecs_in), len(rows_out), len(acc_out)

    def body(*refs):
        rin, vin = refs[:nr], refs[nr:nr + nv]
        rout, aout = refs[nr + nv:nr + nv + no], refs[nr + nv + no:]
        outs = fn(*[r[...] for r in rin], *[v[...] for v in vin])
        for r, o in zip(rout, outs[:no]):
            r[...] = o.astype(r.dtype)
        if na:
            @pl.when(pl.program_id(0) == 0)
            def _():
                for r in aout:
                    r[...] = jnp.zeros_like(r)

            for r, o in zip(aout, outs[no:]):
                r[...] += o

    outs = pl.pallas_call(
        body, name=name, grid=(L // tm,),
        in_specs=[pl.BlockSpec((tm, r.shape[1]), lambda i: (i, 0)) for r in rows_in] + [_whole(v.shape) for v in vecs_in],
        out_specs=[pl.BlockSpec((tm, c), lambda i: (i, 0)) for c, _ in rows_out] + [_whole(s) for s in acc_out],
        out_shape=[jax.ShapeDtypeStruct((L, c), d) for c, d in rows_out] + [jax.ShapeDtypeStruct(s, F32) for s in acc_out],
        compiler_params=_params("arbitrary"),
    )(*rows_in, *vecs_in)
    return outs


def _rsum(t):
    return jnp.sum(t, axis=0, keepdims=True)


def _norm_mod(x, g, sc, sh):
    rs = lax.rsqrt(jnp.mean(x * x, axis=-1, keepdims=True) + EPS)
    return x * rs * g * (1.0 + sc) + sh


def _norm_mod_bwd(x, dh, g, sc):
    rs = lax.rsqrt(jnp.mean(x * x, axis=-1, keepdims=True) + EPS)
    xh = x * rs
    dn = dh * (1.0 + sc)
    dxh = dn * g
    dx = rs * (dxh - xh * jnp.mean(dxh * xh, axis=-1, keepdims=True))
    return dx, _rsum(dh), _rsum(dh * xh * g), _rsum(dn * xh)


def _sigmoid(x):
    return jax.nn.sigmoid(x)


def _gelu(y):
    return jax.nn.gelu(y, approximate=True)


def _gelu_grad(y):
    c = math.sqrt(2.0 / math.pi)
    t = jnp.tanh(c * (y + 0.044715 * y * y * y))
    return 0.5 * (1.0 + t) + 0.5 * y * (1.0 - t * t) * c * (1.0 + 3.0 * 0.044715 * y * y)


def _s5_tables(lb_re, lb_im, bb_re, bb_im, c_re, c_im, seg_len):
    G = lb_re.shape[0]
    nb = G // GPB
    eye = jnp.eye(GPB, dtype=F32)

    def bdiag(t):
        a, b = t.shape[1:]
        t = t.reshape(nb, GPB, a, b)
        return (t[:, :, :, None, :] * eye[None, :, None, :, None]).reshape(nb, GPB * a, GPB * b)

    bbd = jnp.concatenate([bdiag(bb_re.transpose(0, 2, 1)), bdiag(bb_im.transpose(0, 2, 1))], axis=2)
    cbd = jnp.concatenate([bdiag(c_re.transpose(0, 2, 1)), -bdiag(c_im.transpose(0, 2, 1))], axis=1)

    def lanes(re, im):
        t = jnp.concatenate([re.reshape(nb, GPB * S5_P), im.reshape(nb, GPB * S5_P)], axis=1)
        return jnp.repeat(t, NSEG, axis=0)

    tr, ti = lb_re, lb_im
    for _ in range(int(math.log2(seg_len))):
        tr, ti = tr * tr - ti * ti, 2.0 * tr * ti
    return dict(bbd=bbd.astype(BF16), bbdT=bbd.transpose(0, 2, 1).astype(BF16), cbd=cbd.astype(BF16),
                cbdT=cbd.transpose(0, 2, 1).astype(BF16), lam=lanes(lb_re, lb_im), lamT=lanes(tr, ti))


def _s5_untable(acc):
    nb = acc.shape[0]
    t = acc.reshape(nb, GPB, S5_H, 2, GPB, S5_P)
    d = jnp.diagonal(t, axis1=1, axis2=4)
    d = d.transpose(0, 4, 2, 1, 3).reshape(nb * GPB, 2, S5_H, S5_P)
    return d[:, 0], d[:, 1]


def _s5_fix(ends, lamT, *, reverse, name):
    nrow = ends.shape[0]
    half = ends.shape[1] // 2

    def body(e_ref, t_ref, o_ref):
        for gb in range(nrow // NSEG):
            r0 = gb * NSEG
            tr, ti = t_ref[r0:r0 + 1, :half], t_ref[r0:r0 + 1, half:]
            cr = jnp.zeros((1, half), F32)
            ci = jnp.zeros((1, half), F32)
            order = range(NSEG - 1, -1, -1) if reverse else range(NSEG)
            for n, s in enumerate(order):
                if n > 0:
                    p = s + 1 if reverse else s - 1
                    er, ei = e_ref[r0 + p:r0 + p + 1, :half], e_ref[r0 + p:r0 + p + 1, half:]
                    if reverse:
                        cr, ci = tr * cr + ti * ci + er, tr * ci - ti * cr + ei
                    else:
                        cr, ci = tr * cr - ti * ci + er, tr * ci + ti * cr + ei
                o_ref[r0 + s:r0 + s + 1, :half] = cr
                o_ref[r0 + s:r0 + s + 1, half:] = ci

    return pl.pallas_call(body, name=name, out_shape=jax.ShapeDtypeStruct(ends.shape, F32),
                          compiler_params=_params())(ends, lamT)


def _s5_fwd(up, carry_in, tb, dvec, *, emit, name, R=256):
    L, D = up.shape
    R = min(R, L)
    nb, ta, ngb = L // R, R // NSEG, D // 128
    SW = GPB * S5_P
    crows = ngb * NSEG

    def body(u_ref, cin_ref, lam_ref, b_ref, c_ref, d_ref, *rest):
        if emit:
            y_ref, z_ref, ck_ref, carry, xbuf = rest
        else:
            cout_ref, carry, xbuf = rest
        i = pl.program_id(0)

        @pl.when(i == 0)
        def _():
            carry[...] = cin_ref[...]

        if emit:
            ck_ref[0] = carry[...]
        for gb in range(ngb):
            cols = slice(gb * 128, (gb + 1) * 128)
            rws = slice(gb * NSEG, (gb + 1) * NSEG)
            ug = u_ref[:, cols]
            xbuf[...] = jnp.dot(ug.astype(BF16), b_ref[gb], preferred_element_type=F32)
            lr, li = lam_ref[rws, :SW], lam_ref[rws, SW:]

            def step(a, c, lr=lr, li=li):
                cr, ci = c
                o = pl.multiple_of(a * NSEG, NSEG)
                nr = lr * cr - li * ci + xbuf[pl.ds(o, NSEG), :SW]
                ni = lr * ci + li * cr + xbuf[pl.ds(o, NSEG), SW:]
                if emit:
                    xbuf[pl.ds(o, NSEG), :SW] = nr
                    xbuf[pl.ds(o, NSEG), SW:] = ni
                return nr, ni

            cr, ci = lax.fori_loop(0, ta, step, (carry[rws, :SW], carry[rws, SW:]), unroll=2)
            carry[rws, :SW] = cr
            carry[rws, SW:] = ci
            if emit:
                y = jnp.dot(xbuf[...].astype(BF16), c_ref[gb], preferred_element_type=F32) + d_ref[:, cols] * ug
                y_ref[:, cols] = y
                z_ref[:, cols] = _gelu(y).astype(BF16)
        if not emit:
            cout_ref[...] = carry[...]

    rowblk = pl.BlockSpec((R, D), lambda i: (i, 0))
    if emit:
        out_shape = [jax.ShapeDtypeStruct((L, D), F32), jax.ShapeDtypeStruct((L, D), BF16),
                     jax.ShapeDtypeStruct((nb, crows, 2 * SW), F32)]
        out_specs = [rowblk, rowblk, pl.BlockSpec((1, crows, 2 * SW), lambda i: (i, 0, 0))]
    else:
        out_shape = [jax.ShapeDtypeStruct((crows, 2 * SW), F32)]
        out_specs = [_whole((crows, 2 * SW))]
    return pl.pallas_call(
        body, name=name, grid=(nb,),
        in_specs=[rowblk, _whole(carry_in.shape), _whole(tb["lam"].shape), _whole(tb["bbd"].shape),
                  _whole(tb["cbd"].shape), _whole(dvec.shape)],
        out_specs=out_specs, out_shape=out_shape,
        scratch_shapes=[pltpu.VMEM((crows, 2 * SW), F32), pltpu.VMEM((R, 2 * SW), F32)],
        compiler_params=_params("arbitrary"),
    )(up, carry_in, tb["lam"], tb["bbd"], tb["cbd"], dvec)


def _s5_bwd(up, y, dz, ck, gcarry_in, tb, dvec, *, emit, name, R=256):
    L, D = up.shape
    R = min(R, L)
    nb, ta, ngb = L // R, R // NSEG, D // 128
    SW = GPB * S5_P
    crows = ngb * NSEG

    def body(u_ref, y_ref, dz_ref, ck_ref, gin_ref, lam_ref, b_ref, bt_ref, ct_ref, d_ref, *rest):
        if emit:
            du_ref, db_ref, dc_ref, dl_ref, dd_ref, gcarry, xbuf, gbuf, dybuf = rest
        else:
            gout_ref, gcarry, gbuf, dybuf = rest
        i = pl.program_id(0)

        @pl.when(i == 0)
        def _():
            gcarry[...] = gin_ref[...]
            if emit:
                db_ref[...] = jnp.zeros_like(db_ref)
                dc_ref[...] = jnp.zeros_like(dc_ref)
                dl_ref[...] = jnp.zeros_like(dl_ref)
                dd_ref[...] = jnp.zeros_like(dd_ref)

        dybuf[...] = dz_ref[...] * _gelu_grad(y_ref[...])
        for gb in range(ngb):
            cols = slice(gb * 128, (gb + 1) * 128)
            rws = slice(gb * NSEG, (gb + 1) * NSEG)
            dyg = dybuf[:, cols]
            lr, li = lam_ref[rws, :SW], lam_ref[rws, SW:]
            gbuf[...] = jnp.dot(dyg.astype(BF16), ct_ref[gb], preferred_element_type=F32)
            if emit:
                ug = u_ref[:, cols]
                xbuf[0:NSEG, :] = ck_ref[0, rws, :]
                xbuf[NSEG:, :] = jnp.dot(ug.astype(BF16), b_ref[gb], preferred_element_type=F32)

                def fstep(a, c, lr=lr, li=li):
                    cr, ci = c
                    o = pl.multiple_of(a * NSEG + NSEG, NSEG)
                    nr = lr * cr - li * ci + xbuf[pl.ds(o, NSEG), :SW]
                    ni = lr * ci + li * cr + xbuf[pl.ds(o, NSEG), SW:]
                    xbuf[pl.ds(o, NSEG), :SW] = nr
                    xbuf[pl.ds(o, NSEG), SW:] = ni
                    return nr, ni

                lax.fori_loop(0, ta, fstep, (xbuf[0:NSEG, :SW], xbuf[0:NSEG, SW:]), unroll=2)

            def rstep(k, c, lr=lr, li=li):
                o = pl.multiple_of((ta - 1 - k) * NSEG, NSEG)
                gr_n, gi_n = c[0], c[1]
                gr = gbuf[pl.ds(o, NSEG), :SW] + lr * gr_n + li * gi_n
                gi = gbuf[pl.ds(o, NSEG), SW:] - li * gr_n + lr * gi_n
                if not emit:
                    return gr, gi
                gbuf[pl.ds(o, NSEG), :SW] = gr
                gbuf[pl.ds(o, NSEG), SW:] = gi
                xr, xi = xbuf[pl.ds(o, NSEG), :SW], xbuf[pl.ds(o, NSEG), SW:]
                return gr, gi, c[2] + gr * xr + gi * xi, c[3] + gi * xr - gr * xi

            c0 = (gcarry[rws, :SW], gcarry[rws, SW:])
            if emit:
                c0 = c0 + (jnp.zeros((NSEG, SW), F32), jnp.zeros((NSEG, SW), F32))
            cf = lax.fori_loop(0, ta, rstep, c0, unroll=2)
            gcarry[rws, :SW] = cf[0]
            gcarry[rws, SW:] = cf[1]
            if emit:
                dl_ref[rws, :SW] += cf[2]
                dl_ref[rws, SW:] += cf[3]
                gb16 = gbuf[...].astype(BF16)
                du_ref[:, cols] = jnp.dot(gb16, bt_ref[gb], preferred_element_type=F32) + d_ref[:, cols] * dyg
                tn = (((0,), (0,)), ((), ()))
                db_ref[gb] += lax.dot_general(ug.astype(BF16), gb16, tn, preferred_element_type=F32)
                dc_ref[gb] += lax.dot_general(dyg.astype(BF16), xbuf[NSEG:, :].astype(BF16), tn,
                                              preferred_element_type=F32)
                dd_ref[:, cols] += _rsum(dyg * ug)
        if not emit:
            gout_ref[...] = gcarry[...]

    rev = pl.BlockSpec((R, D), lambda i: (nb - 1 - i, 0))
    acc3 = (ngb, 128, 2 * SW)
    if emit:
        out_shape = [jax.ShapeDtypeStruct((L, D), F32), jax.ShapeDtypeStruct(acc3, F32), jax.ShapeDtypeStruct(acc3, F32),
                     jax.ShapeDtypeStruct((crows, 2 * SW), F32), jax.ShapeDtypeStruct((1, D), F32)]
        out_specs = [rev, _whole(acc3), _whole(acc3), _whole((crows, 2 * SW)), _whole((1, D))]
        scratch = [pltpu.VMEM((crows, 2 * SW), F32), pltpu.VMEM((R + NSEG, 2 * SW), F32),
                   pltpu.VMEM((R, 2 * SW), F32), pltpu.VMEM((R, D), F32)]
    else:
        out_shape = [jax.ShapeDtypeStruct((crows, 2 * SW), F32)]
        out_specs = [_whole((crows, 2 * SW))]
        scratch = [pltpu.VMEM((crows, 2 * SW), F32), pltpu.VMEM((R, 2 * SW), F32), pltpu.VMEM((R, D), F32)]
    return pl.pallas_call(
        body, name=name, grid=(nb,),
        in_specs=[rev, rev, rev, pl.BlockSpec((1, crows, 2 * SW), lambda i: (nb - 1 - i, 0, 0)),
                  _whole(gcarry_in.shape), _whole(tb["lam"].shape), _whole(tb["bbd"].shape),
                  _whole(tb["bbdT"].shape), _whole(tb["cbdT"].shape), _whole(dvec.shape)],
        out_specs=out_specs, out_shape=out_shape, scratch_shapes=scratch,
        compiler_params=_params("arbitrary"),
    )(up, y, dz, ck, gcarry_in, tb["lam"], tb["bbd"], tb["bbdT"], tb["cbdT"], dvec)


NN = (((1,), (0,)), ((), ()))
TN = (((0,), (0,)), ((), ()))
NT = (((1,), (1,)), ((), ()))


def _dot3(lhs, rhs, dn, split):
    x = rhs if split == "rhs" else lhs
    hi = x.astype(BF16)
    r1 = x - hi.astype(F32)
    mid = r1.astype(BF16)
    lo = (r1 - mid.astype(F32)).astype(BF16)
    out = None
    for part in (hi, mid, lo):
        ops = (lhs, part) if split == "rhs" else (part, rhs)
        t = lax.dot_general(ops[0], ops[1], dn, preferred_element_type=F32)
        out = t if out is None else out + t
    return out


def _log_sigmoid(x):
    return jnp.minimum(x, 0.0) - jnp.log(1.0 + jnp.exp(-jnp.abs(x)))


def _gla_gates(p, wg_ref, bg_ref, QK):
    C = p.shape[0]
    glr = p[:, 6 * QK:6 * QK + 128].astype(BF16)
    gpre = jnp.dot(glr, wg_ref[...], preferred_element_type=F32) + bg_ref[...]
    la = _log_sigmoid(gpre) * (1.0 / GATE_TAU)
    row = lax.broadcasted_iota(jnp.int32, (C, C), 0)
    col = lax.broadcasted_iota(jnp.int32, (C, C), 1)
    gc = _dot3((row >= col).astype(BF16), la, NN, "rhs")
    ge = gc[C - 1:C, :]
    w = jnp.exp(ge - gc)
    return glr, gpre, la, ge, w


def _gla_fwd(proj, wg2p, bg, gn, *, name):
    L = proj.shape[0]
    QK = wg2p.shape[1]
    DK, DV = QK // HEADS, 2 * QK // HEADS
    nC = L // CHUNK
    NB = min(GLA_NB, nC)
    assert nC % NB == 0
    scale = DK ** -0.5

    def body(p_ref, wg_ref, bg_ref, gn_ref, og_ref, s_ref, sst):
        @pl.when(pl.program_id(0) == 0)
        def _():
            sst[...] = jnp.zeros_like(sst)

        ones = jnp.ones((CHUNK, DV), BF16)
        for cc in range(NB):
            rows = slice(cc * CHUNK, (cc + 1) * CHUNK)
            p = p_ref[rows, :]
            _, _, la, _, w = _gla_gates(p, wg_ref, bg_ref, QK)
            for h in range(HEADS):
                ks, vs = slice(h * DK, (h + 1) * DK), slice(h * DV, (h + 1) * DV)
                q = p[:, h * DK:(h + 1) * DK] * scale
                kd = p[:, QK + h * DK:QK + (h + 1) * DK] * w[:, ks]
                v = p[:, 2 * QK + h * DV:2 * QK + (h + 1) * DV]
                r = p[:, 4 * QK + h * DV:4 * QK + (h + 1) * DV]
                dec = jnp.exp(_dot3(la[:, ks], ones, TN, "lhs"))
                kv = lax.dot_general(kd.astype(BF16), v.astype(BF16), TN, preferred_element_type=F32)
                S = dec * sst[ks, :] + kv
                sst[ks, :] = S
                s_ref[cc, ks, :] = S
                o = jnp.dot(q.astype(BF16), S.astype(BF16), preferred_element_type=F32)
                on = o * lax.rsqrt(jnp.mean(o * o, axis=-1, keepdims=True) + EPS)
                og_ref[rows, vs] = (on * gn_ref[:, vs] * (r * _sigmoid(r))).astype(BF16)

    RB = NB * CHUNK
    return pl.pallas_call(
        body, name=name, grid=(nC // NB,),
        in_specs=[pl.BlockSpec((RB, proj.shape[1]), lambda i: (i, 0)), _whole(wg2p.shape), _whole(bg.shape), _whole(gn.shape)],
        out_specs=[pl.BlockSpec((RB, 2 * QK), lambda i: (i, 0)), pl.BlockSpec((NB, QK, DV), lambda i: (i, 0, 0))],
        out_shape=[jax.ShapeDtypeStruct((L, 2 * QK), BF16), jax.ShapeDtypeStruct((nC, QK, DV), F32)],
        scratch_shapes=[pltpu.VMEM((QK, DV), F32)],
        compiler_params=_params("arbitrary"),
    )(proj, wg2p, bg, gn)


def _gla_bwd(proj, dog, states, wg2p, bg, gn, *, name):
    L, W = proj.shape
    QK = wg2p.shape[1]
    DK, DV = QK // HEADS, 2 * QK // HEADS
    nC = L // CHUNK
    NB = min(GLA_NB, nC)
    nB = nC // NB
    scale = DK ** -0.5

    def body(p_ref, dog_ref, sc_ref, sp_ref, wg_ref, bg_ref, gn_ref, dp_ref, dwg_ref, dbg_ref, dgn_ref, gst):
        i = pl.program_id(0)

        @pl.when(i == 0)
        def _():
            gst[...] = jnp.zeros_like(gst)
            dwg_ref[...] = jnp.zeros_like(dwg_ref)
            dbg_ref[...] = jnp.zeros_like(dbg_ref)
            dgn_ref[...] = jnp.zeros_like(dgn_ref)

        row = lax.broadcasted_iota(jnp.int32, (CHUNK, CHUNK), 0)
        col = lax.broadcasted_iota(jnp.int32, (CHUNK, CHUNK), 1)
        tri_u = (col >= row).astype(BF16)
        ones = jnp.ones((CHUNK, DV), BF16)
        ones8 = jnp.ones((8, DV), BF16)
        for cc in range(NB - 1, -1, -1):
            rows = slice(cc * CHUNK, (cc + 1) * CHUNK)
            p = p_ref[rows, :]
            glr, gpre, la, ge, w = _gla_gates(p, wg_ref, bg_ref, QK)
            dla_heads = []
            for h in range(HEADS):
                ks, vs = slice(h * DK, (h + 1) * DK), slice(h * DV, (h + 1) * DV)
                qs = p[:, h * DK:(h + 1) * DK] * scale
                k = p[:, QK + h * DK:QK + (h + 1) * DK]
                v = p[:, 2 * QK + h * DV:2 * QK + (h + 1) * DV]
                r = p[:, 4 * QK + h * DV:4 * QK + (h + 1) * DV]
                wh = w[:, ks]
                kd = k * wh
                S = sc_ref[cc, ks, :]
                if cc > 0:
                    Sp = sc_ref[cc - 1, ks, :]
                else:
                    Sp = jnp.where(i < nB - 1, sp_ref[0, ks, :], 0.0)
                o = jnp.dot(qs.astype(BF16), S.astype(BF16), preferred_element_type=F32)
                rs = lax.rsqrt(jnp.mean(o * o, axis=-1, keepdims=True) + EPS)
                on = o * rs
                sr = _sigmoid(r)
                dg = dog_ref[rows, vs]
                gnh = gn_ref[:, vs]
                dp_ref[rows, 4 * QK + h * DV:4 * QK + (h + 1) * DV] = (
                    dg * on * gnh * (sr * (1.0 + r * (1.0 - sr)))).astype(BF16)
                dt = dg * (r * sr)
                dgn_ref[:, vs] += _rsum(dt * on)
                don = dt * gnh
                do = (rs * (don - on * jnp.mean(don * on, axis=-1, keepdims=True))).astype(BF16)
                Gc = gst[ks, :] + lax.dot_general(qs.astype(BF16), do, TN, preferred_element_type=F32)
                G16 = Gc.astype(BF16)
                dp_ref[rows, h * DK:(h + 1) * DK] = (
                    lax.dot_general(do, S.astype(BF16), NT, preferred_element_type=F32) * scale).astype(BF16)
                dkd = lax.dot_general(v.astype(BF16), G16, NT, preferred_element_type=F32)
                dp_ref[rows, 2 * QK + h * DV:2 * QK + (h + 1) * DV] = jnp.dot(
                    kd.astype(BF16), G16, preferred_element_type=F32).astype(BF16)
                gst[ks, :] = jnp.exp(_dot3(la[:, ks], ones, TN, "lhs")) * Gc
                ddec = _dot3(ones8, Gc * Sp, NT, "rhs")[0:1, :]
                dp_ref[rows, QK + h * DK:QK + (h + 1) * DK] = (dkd * wh).astype(BF16)
                dww = dkd * kd
                dge = jnp.exp(ge[:, ks]) * ddec + _rsum(dww)
                dla_heads.append(dge - _dot3(tri_u, dww, NN, "rhs"))
            dla = jnp.concatenate(dla_heads, axis=1)
            dgpre = dla * (1.0 / GATE_TAU) * (1.0 - _sigmoid(gpre))
            d16 = dgpre.astype(BF16)
            dp_ref[rows, 6 * QK:6 * QK + 128] = lax.dot_general(d16, wg_ref[...], NT, preferred_element_type=F32).astype(BF16)
            dwg_ref[...] += lax.dot_general(glr, d16, TN, preferred_element_type=F32)
            dbg_ref[...] += _rsum(dgpre)

    RB = NB * CHUNK
    rev = lambda i: (nB - 1 - i, 0)
    return pl.pallas_call(
        body, name=name, grid=(nB,),
        in_specs=[pl.BlockSpec((RB, W), rev), pl.BlockSpec((RB, 2 * QK), rev),
                  pl.BlockSpec((NB, QK, DV), lambda i: (nB - 1 - i, 0, 0)),
                  pl.BlockSpec((1, QK, DV), lambda i: (jnp.maximum(NB * (nB - 1 - i) - 1, 0), 0, 0)),
                  _whole(wg2p.shape), _whole(bg.shape), _whole(gn.shape)],
        out_specs=[pl.BlockSpec((RB, W), rev), _whole((128, QK)), _whole((1, QK)), _whole((1, 2 * QK))],
        out_shape=[jax.ShapeDtypeStruct((L, W), BF16), jax.ShapeDtypeStruct((128, QK), F32),
                   jax.ShapeDtypeStruct((1, QK), F32), jax.ShapeDtypeStruct((1, 2 * QK), F32)],
        scratch_shapes=[pltpu.VMEM((QK, DV), F32)],
        compiler_params=_params("arbitrary"),
    )(proj, dog, states, states, wg2p, bg, gn)


def _coords():
    return lax.axis_index("x"), lax.axis_index("y"), lax.axis_index("c")


def _other_chips(x, y):
    return [(1 - x, y, 2 * (1 - x) + y), (x, 1 - y, 2 * x + 1 - y), (1 - x, 1 - y, 2 * (1 - x) + 1 - y)]


def _hbm_call(body, ins, out_shapes, n_sems, *, name, alias=False):
    any_spec = pl.BlockSpec(memory_space=pl.ANY)
    return pl.pallas_call(
        body, name=name, in_specs=[any_spec] * len(ins), out_specs=[any_spec] * len(out_shapes), out_shape=out_shapes,
        scratch_shapes=[pltpu.SemaphoreType.DMA((n,)) for n in n_sems],
        input_output_aliases={k: k for k in range(len(ins))} if alias else {},
    )(*ins)


def _exchange(src, masks, *, name):
    vary = [any(m[k] for m in masks) for k in range(3)]
    nslots = 2 ** sum(vary)
    n = len(masks)

    def slot(coords):
        s = 0
        for k in range(3):
            if vary[k]:
                s = s * 2 + coords[k]
        return s

    def body(src_ref, dst_ref, send_sems, recv_sems, loc_sem):
        me = _coords()
        mine = slot(me)
        loc = pltpu.make_async_copy(src_ref, dst_ref.at[mine], loc_sem.at[0])
        loc.start()
        copies = []
        for k, m in enumerate(masks):
            peer = tuple(1 - me[d] if m[d] else me[d] for d in range(3))
            cp = pltpu.make_async_remote_copy(src_ref=src_ref, dst_ref=dst_ref.at[mine], send_sem=send_sems.at[k],
                                              recv_sem=recv_sems.at[k], device_id=peer, device_id_type=MESH)
            cp.start()
            copies.append(cp)
        for cp in copies:
            cp.wait()
        loc.wait()

    return _hbm_call(body, [src], [jax.ShapeDtypeStruct((nslots,) + tuple(src.shape), src.dtype)], (n, n, 1), name=name)[0]


def _cast_into(t, lead, kind, chip, *, name, tm=256):
    r, cc = t.shape[-2:]
    tm = min(tm, r)
    nblk = r // tm
    if kind == "col":
        shp, o_spec = (r, NCH * cc), pl.BlockSpec((tm, cc), lambda i, s: (i, s[0]))
    elif kind == "row":
        shp, o_spec = (NCH * r, cc), pl.BlockSpec((tm, cc), lambda i, s: (s[0] * nblk + i, 0))
    else:
        shp, o_spec = (NCH, r, cc), pl.BlockSpec((None, tm, cc), lambda i, s: (s[0], i, 0))

    def body(s_ref, t_ref, o_ref):
        o_ref[...] = t_ref[...].astype(o_ref.dtype)

    return pl.pallas_call(
        body, name=name,
        grid_spec=pltpu.PrefetchScalarGridSpec(
            num_scalar_prefetch=1, grid=(nblk,),
            in_specs=[pl.BlockSpec((None, tm, cc), lambda i, s: (lead, i, 0))], out_specs=o_spec),
        out_shape=jax.ShapeDtypeStruct(shp, BF16), compiler_params=_params("parallel"),
    )(chip.reshape(1).astype(jnp.int32), t)


def _gather_weights(arrs, shard_shapes, kinds, *, name):
    n = len(arrs)

    def body(*refs):
        dst = refs[n:2 * n]
        send_sems, recv_sems = refs[2 * n:]
        x, y, c = _coords()
        chip = 2 * x + y
        others = _other_chips(x, y)
        sib = (x, y, 1 - c)

        def window(p, chip_id, cc):
            r, cols = shard_shapes[p]
            h = r // 2
            if kinds[p] == "col":
                return dst[p].at[pl.ds(cc * h, h), pl.ds(pl.multiple_of(chip_id * cols, 128), cols)]
            if kinds[p] == "row":
                return dst[p].at[pl.ds(chip_id * r + cc * h, h), :]
            return dst[p].at[chip_id, pl.ds(cc * h, h), :]

        def copy(p, k, win, to):
            return pltpu.make_async_remote_copy(src_ref=win, dst_ref=win, send_sem=send_sems.at[6 * p + k],
                                                recv_sem=recv_sems.at[6 * p + k], device_id=to, device_id_type=MESH)

        sends = []
        for p in range(n):
            for j, (ox, oy, _) in enumerate(others):
                cp = copy(p, j, window(p, chip, c), (ox, oy, c))
                cp.start()
                sends.append(cp)
        for j, (_, _, oc) in enumerate(others):
            for p in range(n):
                copy(p, j, window(p, oc, c), (x, y, c)).wait_recv()
                fw = copy(p, 3 + j, window(p, oc, c), sib)
                fw.start()
                sends.append(fw)
        for p in range(n):
            for j, (_, _, oc) in enumerate(others):
                copy(p, 3 + j, window(p, oc, 1 - c), sib).wait_recv()
        for cp in sends:
            cp.wait_send()

    outs = [jax.ShapeDtypeStruct(a.shape, a.dtype) for a in arrs]
    return _hbm_call(body, arrs, outs, (6 * n, 6 * n), name=name, alias=True)


def _rs_cores(grads, *, name):
    n = len(grads)
    outs = [jax.ShapeDtypeStruct((g.shape[0], g.shape[1] // 2, g.shape[2]), g.dtype) for g in grads]

    def body(*refs):
        src, dst = refs[:n], refs[n:2 * n]
        send_sems, recv_sems = refs[2 * n:]
        x, y, c = _coords()
        copies = []
        for p in range(n):
            nsh, r, _ = grads[p].shape
            h = r // 2
            for j in range(nsh):
                cp = pltpu.make_async_remote_copy(
                    src_ref=src[p].at[j, pl.ds((1 - c) * h, h), :], dst_ref=dst[p].at[j],
                    send_sem=send_sems.at[nsh * p + j], recv_sem=recv_sems.at[nsh * p + j],
                    device_id=(x, y, 1 - c), device_id_type=MESH)
                cp.start()
                copies.append(cp)
        for cp in copies:
            cp.wait()

    tot = sum(g.shape[0] for g in grads)
    return _hbm_call(body, grads, outs, (tot, tot), name=name)


def _sum_own_half(full, recv, ci, out_dtype, *, name):
    nsh, h, cols = recv.shape
    tm = _tile_rows(h, 256)
    nblk = h // tm

    def body(c_ref, f_ref, r_ref, o_ref):
        o_ref[...] = (f_ref[...] + r_ref[...]).astype(o_ref.dtype)

    return pl.pallas_call(
        body, name=name,
        grid_spec=pltpu.PrefetchScalarGridSpec(
            num_scalar_prefetch=1, grid=(nsh, nblk),
            in_specs=[pl.BlockSpec((1, tm, cols), lambda j, i, c_ref: (j, c_ref[0] * nblk + i, 0)),
                      pl.BlockSpec((1, tm, cols), lambda j, i, c_ref: (j, i, 0))],
            out_specs=pl.BlockSpec((1, tm, cols), lambda j, i, c_ref: (j, i, 0))),
        out_shape=jax.ShapeDtypeStruct((nsh, h, cols), out_dtype), compiler_params=_params("parallel", "parallel"),
    )(ci.reshape(1).astype(jnp.int32), full, recv)


def _rs_chips(parts, *, name):
    n = len(parts)
    outs = [jax.ShapeDtypeStruct(t.shape, t.dtype) for t in parts]

    def body(*refs):
        src, dst = refs[:n], refs[n:2 * n]
        send_sems, recv_sems = refs[2 * n:]
        x, y, c = _coords()
        chip = 2 * x + y
        copies = []
        for p in range(n):
            for j, (ox, oy, oc) in enumerate(_other_chips(x, y)):
                cp = pltpu.make_async_remote_copy(
                    src_ref=src[p].at[oc], dst_ref=dst[p].at[chip], send_sem=send_sems.at[3 * p + j],
                    recv_sem=recv_sems.at[3 * p + j], device_id=(ox, oy, c), device_id_type=MESH)
                cp.start()
                copies.append(cp)
        for cp in copies:
            cp.wait()

    return _hbm_call(body, parts, outs, (3 * n, 3 * n), name=name)


def _sum_chips(recv, own, chip, ci, *, name, nlead=1, lead=0, prev=None, spread=False):
    nsh, h, cols = recv.shape
    tm = _tile_rows(h, 256)
    nblk = h // tm
    rows_out = 2 * h * (nsh if spread else 1)

    def body(s_ref, r_ref, o_ref, *rest):
        out_ref = rest[-1]
        t = None
        for s in range(nsh):
            v = jnp.where(s_ref[0] == s, o_ref[s], r_ref[s]).astype(F32)
            t = v if t is None else t + v
        out_ref[...] = t

    def out_idx(i, s):
        return (lead, (s[0] * 2 * nblk if spread else 0) + s[1] * nblk + i, 0)

    blk = pl.BlockSpec((nsh, tm, cols), lambda i, s: (0, i, 0))
    ins = [recv, own] + ([prev] if prev is not None else [])
    return pl.pallas_call(
        body, name=name,
        grid_spec=pltpu.PrefetchScalarGridSpec(
            num_scalar_prefetch=1, grid=(nblk,),
            in_specs=[blk, blk] + ([pl.BlockSpec(memory_space=pl.ANY)] if prev is not None else []),
            out_specs=pl.BlockSpec((None, tm, cols), out_idx)),
        out_shape=jax.ShapeDtypeStruct((nlead, rows_out, cols), F32),
        input_output_aliases={3: 0} if prev is not None else {},
        compiler_params=_params("arbitrary"),
    )(jnp.stack([chip, ci]).astype(jnp.int32), *ins)


def _rs_gather(arrs, halves, spread, *, name, nchunk=4):
    n = len(arrs)
    per = [a.shape[0] * nchunk for a in arrs]
    offs = [sum(per[:p]) for p in range(n)]

    def body(*refs):
        dst = refs[n:2 * n]
        send_sems, recv_sems = refs[2 * n:]
        x, y, c = _coords()
        chip = 2 * x + y
        copies = []
        for p in range(n):
            h = halves[p]
            q = h // nchunk
            base = chip * 2 * h if spread[p] else 0
            for l in range(arrs[p].shape[0]):
                for k in range(nchunk):
                    win = dst[p].at[l, pl.ds(base + c * h + k * q, q), :]
                    sem = offs[p] + l * nchunk + k
                    cp = pltpu.make_async_remote_copy(src_ref=win, dst_ref=win, send_sem=send_sems.at[sem],
                                                      recv_sem=recv_sems.at[sem], device_id=(x, y, 1 - c),
                                                      device_id_type=MESH)
                    cp.start()
                    copies.append(cp)
        for cp in copies:
            cp.wait_send()
        for p in range(n):
            h = halves[p]
            q = h // nchunk
            base = chip * 2 * h if spread[p] else 0
            for l in range(arrs[p].shape[0]):
                for k in range(nchunk):
                    win = dst[p].at[l, pl.ds(base + (1 - c) * h + k * q, q), :]
                    sem = offs[p] + l * nchunk + k
                    pltpu.make_async_remote_copy(src_ref=win, dst_ref=win, send_sem=send_sems.at[sem],
                                                 recv_sem=recv_sems.at[sem], device_id=(x, y, 1 - c),
                                                 device_id_type=MESH).wait_recv()

    outs = [jax.ShapeDtypeStruct(a.shape, a.dtype) for a in arrs]
    return _hbm_call(body, arrs, outs, (sum(per), sum(per)), name=name, alias=True)


def _adamw(w, g, m, v, *, name):
    nl, R, C = w.shape
    tm = _tile_rows(R, 256)

    def body(w_ref, g_ref, m_ref, v_ref, d_ref, nm_ref, nv_ref):
        gg = g_ref[...]
        nm = B1 * m_ref[...] + (1.0 - B1) * gg
        nv = B2 * v_ref[...] + (1.0 - B2) * (gg * gg)
        m_hat = nm / (1.0 - B1 ** ASTEP)
        v_hat = nv / (1.0 - B2 ** ASTEP)
        d_ref[...] = -LR * (m_hat / (jnp.sqrt(v_hat) + AEPS) + WD * w_ref[...])
        nm_ref[...] = nm
        nv_ref[...] = nv

    blk = pl.BlockSpec((None, tm, C), lambda l, i: (l, i, 0))
    return pl.pallas_call(
        body, name=name, grid=(nl, R // tm), in_specs=[blk] * 4, out_specs=[blk] * 3,
        out_shape=[jax.ShapeDtypeStruct((nl, R, C), F32)] * 3, compiler_params=_params("parallel", "parallel"),
    )(w, g, m, v)


def _mod_cols(c_all, w_ada, b_cols, *, name):
    nl, D, cols = w_ada.shape
    B = c_all.shape[0]

    def body(c_ref, w_ref, b_ref, o_ref):
        cc = c_ref[...]
        cs = (cc * _sigmoid(cc)).astype(BF16)
        o_ref[0] = jnp.dot(cs, w_ref[0].astype(BF16), preferred_element_type=F32) + b_ref[0]

    return pl.pallas_call(
        body, name=name, grid=(nl,),
        in_specs=[_whole(c_all.shape), pl.BlockSpec((1, D, cols), lambda i: (i, 0, 0)), pl.BlockSpec((1, 1, cols), lambda i: (i, 0, 0))],
        out_specs=pl.BlockSpec((1, B, cols), lambda i: (i, 0, 0)),
        out_shape=jax.ShapeDtypeStruct((nl, B, cols), F32), compiler_params=_params("arbitrary"),
    )(c_all, w_ada, b_cols)


def _ada_grad(c_all, dmod_cols, *, name):
    nl, B, cols = dmod_cols.shape
    D = c_all.shape[1]

    def body(c_ref, d_ref, o_ref):
        cc = c_ref[...]
        cs = (cc * _sigmoid(cc)).astype(BF16)
        o_ref[0] = lax.dot_general(cs, d_ref[0].astype(BF16), TN, preferred_element_type=F32)

    return pl.pallas_call(
        body, name=name, grid=(nl,),
        in_specs=[_whole(c_all.shape), pl.BlockSpec((1, B, cols), lambda i: (i, 0, 0))],
        out_specs=pl.BlockSpec((1, D, cols), lambda i: (i, 0, 0)),
        out_shape=jax.ShapeDtypeStruct((nl, D, cols), F32), compiler_params=_params("arbitrary"),
    )(c_all, dmod_cols)


def _s5_disc(a_re, a_im, log_dt, b_re, b_im):
    dt = jnp.exp(log_dt)[:, None]
    mag = jnp.exp(a_re * dt)
    ph = a_im * dt
    lb_re = mag * jnp.cos(ph)
    lb_im = mag * jnp.sin(ph)
    den = a_re * a_re + a_im * a_im
    nr = lb_re - 1.0
    ni = lb_im
    f_re = (nr * a_re + ni * a_im) / den
    f_im = (ni * a_re - nr * a_im) / den
    bb_re = f_re[..., None] * b_re - f_im[..., None] * b_im
    bb_im = f_re[..., None] * b_im + f_im[..., None] * b_re
    return lb_re, lb_im, bb_re, bb_im


def _to_segments(t):
    L, D = t.shape
    return t.reshape(NSEG, L // NSEG, D).transpose(1, 0, 2).reshape(L, D)


def _from_segments(t):
    L, D = t.shape
    return t.reshape(L // NSEG, NSEG, D).transpose(1, 0, 2).reshape(L, D)


def _mlp_fwd(h2, w1, w2, tag):
    a = _matmul(h2, w1, name=f"ff1_{tag}", out_dtypes=(BF16,), epi=lambda acc: (jnp.maximum(acc, 0.0),))
    f = _matmul(a, w2, name=f"ff2_{tag}", a_fn=jnp.square)
    return a, f


def _mlp_bwd(df, h2, a, w1, w2, tag):
    da = _matmul(df, w2, tb=True, name=f"ff2_dx_{tag}", out_dtypes=(BF16,), epi_ins=(a,),
                 epi=lambda acc, at: (acc * (2.0 * at.astype(F32)),))
    dw2 = _matmul(a, df, ta=True, name=f"ff2_dw_{tag}", a_fn=jnp.square)
    dh2 = _matmul(da, w1, tb=True, name=f"ff1_dx_{tag}")
    dw1 = _matmul(h2, da, ta=True, name=f"ff1_dw_{tag}", col_shards=NCH)
    return dh2, dw1, dw2


def kernel(x, c, w_ada, b_ada, norm_mix, norm_mlp, s5_a_re, s5_a_im, s5_log_dt, s5_b_re, s5_b_im, s5_c_re, s5_c_im, s5_d, s5_w_glu, gla_w_in, gla_w_gate2, gla_b_gate, gla_g_norm, gla_w_out, w_ff1, w_ff2, norm_final, loss_target, m_w_ada, m_b_ada, m_norm_mix, m_norm_mlp, m_s5_a_re, m_s5_a_im, m_s5_log_dt, m_s5_b_re, m_s5_b_im, m_s5_c_re, m_s5_c_im, m_s5_d, m_s5_w_glu, m_gla_w_in, m_gla_w_gate2, m_gla_b_gate, m_gla_g_norm, m_gla_w_out, m_w_ff1, m_w_ff2, m_norm_final, v_w_ada, v_b_ada, v_norm_mix, v_norm_mlp, v_s5_a_re, v_s5_a_im, v_s5_log_dt, v_s5_b_re, v_s5_b_im, v_s5_c_re, v_s5_c_im, v_s5_d, v_s5_w_glu, v_gla_w_in, v_gla_w_gate2, v_gla_b_gate, v_gla_g_norm, v_gla_w_out, v_w_ff1, v_w_ff2, v_norm_final):
    args = dict(locals())
    L, D = x.shape[1], x.shape[2]
    QK = D // 2
    xi, yi, ci = _coords()
    chip = 2 * xi + yi
    dev = 2 * chip + ci

    c_all = _exchange(c.reshape(8, D // 8), MASK_ALL, name="gather_c").reshape(8, D)
    acols = w_ada.shape[2]
    b_cols = lax.dynamic_slice_in_dim(b_ada, chip * acols, acols, axis=1)[:, None, :]
    mod_cols = _mod_cols(c_all, w_ada, b_cols, name="ada_mod")
    mod_all = _exchange(mod_cols.reshape(16, acols), MASK_CHIPS, name="gather_mod")
    mod_all = mod_all.reshape(NCH, 2, 8, acols).transpose(1, 2, 0, 3).reshape(2, 8, NCH * acols)
    mod = lax.dynamic_index_in_dim(mod_all, dev, axis=1, keepdims=False).reshape(2, 6, 1, D)

    big = [("s5_w_glu", s5_w_glu, 0, "col"), ("gla_w_in", gla_w_in, 0, "slot"), ("gla_w_out", gla_w_out, 0, "row"),
           ("w_ff1_0", w_ff1, 0, "col"), ("w_ff1_1", w_ff1, 1, "col"), ("w_ff2_0", w_ff2, 0, "row"), ("w_ff2_1", w_ff2, 1, "row")]
    own16 = [_cast_into(t, lead, kind, chip, name=f"cast_{nm}") for nm, t, lead, kind in big]
    W = dict(zip([b[0] for b in big], _gather_weights(own16, [b[1].shape[-2:] for b in big], [b[3] for b in big],
                                                      name="gather_w")))
    w_in = W["gla_w_in"].transpose(1, 0, 2).reshape(D, -1)
    w_in_r = jnp.concatenate([w_in[:, :4 * QK], w_in[:, 4 * QK + GATE_RANK:], w_in[:, 4 * QK:4 * QK + GATE_RANK],
                              jnp.zeros((D, 128 - GATE_RANK), BF16)], axis=1)

    cat = jnp.concatenate([gla_w_gate2[0].reshape(1, -1), gla_b_gate, gla_g_norm], axis=1)
    cat_all = _exchange(jnp.tile(cat, (8, 1)), MASK_CHIPS, name="gather_gla_small")[:, 0, :]
    qk4 = QK // NCH
    wg2 = cat_all[:, :GATE_RANK * qk4].reshape(NCH, GATE_RANK, qk4).transpose(1, 0, 2).reshape(GATE_RANK, QK)
    bg = cat_all[:, GATE_RANK * qk4:(GATE_RANK + 1) * qk4].reshape(1, QK)
    gn = cat_all[:, (GATE_RANK + 1) * qk4:].reshape(1, D)
    wg2p = jnp.concatenate([wg2, jnp.zeros((128 - GATE_RANK, QK), F32)], axis=0).astype(BF16)

    lb_re, lb_im, bb_re, bb_im = _s5_disc(s5_a_re[0], s5_a_im[0], s5_log_dt[0], s5_b_re[0], s5_b_im[0])
    tb = _s5_tables(lb_re, lb_im, bb_re, bb_im, s5_c_re[0], s5_c_im[0], L // NSEG)
    zero_carry = jnp.zeros_like(tb["lam"])

    def vec(t):
        return t.reshape(1, -1)

    xp = _to_segments(x[0])
    m0, m1 = mod[0], mod[1]
    (u0,) = _rows(lambda t, g, sc, sh: (_norm_mod(t, g, sc, sh),), [xp], [vec(norm_mix[0]), m0[1], m0[0]],
                  [(D, F32)], [], name="pre_mix0")
    (ends,) = _s5_fwd(u0, zero_carry, tb, s5_d, emit=False, name="s5_fwd_ends")
    carry0 = _s5_fix(ends, tb["lamT"], reverse=False, name="s5_fix_fwd")
    y0, z0, ck0 = _s5_fwd(u0, carry0, tb, s5_d, emit=True, name="s5_fwd")
    vg0 = _matmul(z0, W["s5_w_glu"], name="glu")

    def res_glu_pre(xt, vgt, gt, g, sc, sh):
        xn = xt + gt * (vgt[:, :D] * _sigmoid(vgt[:, D:]))
        return xn, _norm_mod(xn, g, sc, sh)

    x2_0, h2_0 = _rows(res_glu_pre, [xp, vg0], [m0[2], vec(norm_mlp[0]), m0[4], m0[3]], [(D, F32), (D, BF16)], [],
                       name="res_mix0")
    a_0, f0 = _mlp_fwd(h2_0, W["w_ff1_0"], W["w_ff2_0"], "0")

    def res_pre(xt, bt, gt, g, sc, sh):
        xn = xt + gt * bt
        return xn, _norm_mod(xn, g, sc, sh)

    x3p, h1p = _rows(res_pre, [x2_0, f0], [m0[5], vec(norm_mix[1]), m1[1], m1[0]], [(D, F32), (D, BF16)], [],
                     name="res_mlp0")
    x3 = _from_segments(x3p)
    h1 = _from_segments(h1p)
    proj = _matmul(h1, w_in_r, name="gla_in", tn=640)
    og, states = _gla_fwd(proj, wg2p, bg, gn, name="gla_fwd")
    ymix = _matmul(og, W["gla_w_out"], name="gla_out")
    x2_1, h2_1 = _rows(res_pre, [x3, ymix], [m1[2], vec(norm_mlp[1]), m1[4], m1[3]], [(D, F32), (D, BF16)], [],
                       name="res_mix1")
    a_1, f1 = _mlp_fwd(h2_1, W["w_ff1_1"], W["w_ff2_1"], "1")

    def final(xt, ft, tgt, gt, g):
        xn = xt + gt * ft
        rs = lax.rsqrt(jnp.mean(xn * xn, axis=-1, keepdims=True) + EPS)
        xh = xn * rs
        e = xh * g - tgt
        dout = e * (1.0 / D)
        dxh = dout * g
        dx = rs * (dxh - xh * jnp.mean(dxh * xh, axis=-1, keepdims=True))
        lsum = 0.5 * jnp.sum(jnp.sum(e * e, axis=-1, keepdims=True), axis=0, keepdims=True) * (1.0 / D)
        return dx, jnp.broadcast_to(lsum, (1, 128)), _rsum(dout * xh)

    dx, loss_part, d_norm_final = _rows(final, [x2_1, f1, loss_target[0]], [m1[5], vec(norm_final)], [(D, F32)],
                                        [(1, 128), (1, D)], name="loss_head")
    loss = lax.psum(loss_part[0, 0], ("x", "y", "c"))

    def gate_bwd(dxt, bt, gt):
        return dxt * gt, _rsum(dxt * bt)

    def norm_bwd(xt, dht, drt, g, sc):
        dxn, dsh, dsc, dg = _norm_mod_bwd(xt, dht, g, sc)
        return drt + dxn, dsh, dsc, dg

    def norm_gate_bwd(xt, dht, drt, bt, g, sc, gt):
        dxn, dsh, dsc, dg = _norm_mod_bwd(xt, dht, g, sc)
        dxt = drt + dxn
        return dxt, dxt * gt, dsh, dsc, dg, _rsum(dxt * bt)

    vD = [(1, D)]
    df1, dgt2_1 = _rows(gate_bwd, [dx, f1], [m1[5]], [(D, BF16)], vD, name="gate_mlp1")
    dh2_1, dw_ff1_1, dw_ff2_1 = _mlp_bwd(df1, h2_1, a_1, W["w_ff1_1"], W["w_ff2_1"], "1")
    dx, dmix1, dsh2_1, dsc2_1, dg_mlp1, dgt1_1 = _rows(
        norm_gate_bwd, [x2_1, dh2_1, dx, ymix], [vec(norm_mlp[1]), m1[4], m1[2]], [(D, F32), (D, BF16)], vD * 4,
        name="norm_mlp1_bwd")
    dog = _matmul(dmix1, W["gla_w_out"], tb=True, name="gla_out_dx")
    dw_out = _matmul(og, dmix1, ta=True, name="gla_out_dw")
    dproj, dwg2p, dbg, dgn = _gla_bwd(proj, dog, states, wg2p, bg, gn, name="gla_bwd")
    dh1 = _matmul(dproj, w_in_r, tb=True, name="gla_in_dx", tk=640)
    dw_in_r = _matmul(h1, dproj, ta=True, name="gla_in_dw", tn=640)
    dx, dsh1_1, dsc1_1, dg_mix1 = _rows(norm_bwd, [x3, dh1, dx], [vec(norm_mix[1]), m1[1]], [(D, F32)], vD * 3,
                                        name="norm_mix1_bwd")
    dxp = _to_segments(dx)
    df0, dgt2_0 = _rows(gate_bwd, [dxp, f0], [m0[5]], [(D, BF16)], vD, name="gate_mlp0")
    dh2_0, dw_ff1_0, dw_ff2_0 = _mlp_bwd(df0, h2_0, a_0, W["w_ff1_0"], W["w_ff2_0"], "0")

    def norm_glu_bwd(xt, dht, drt, vgt, g, sc, gt):
        dxn, dsh, dsc, dg = _norm_mod_bwd(xt, dht, g, sc)
        dxt = drt + dxn
        val, sg = vgt[:, :D], _sigmoid(vgt[:, D:])
        dbr = dxt * gt
        dvg = jnp.concatenate([dbr * sg, dbr * val * sg * (1.0 - sg)], axis=1)
        return dxt, dvg, dsh, dsc, dg, _rsum(dxt * val * sg)

    dxp, dvg0, dsh2_0, dsc2_0, dg_mlp0, dgt1_0 = _rows(
        norm_glu_bwd, [x2_0, dh2_0, dxp, vg0], [vec(norm_mlp[0]), m0[4], m0[2]], [(D, F32), (2 * D, BF16)], vD * 4,
        name="norm_mlp0_bwd")
    dz0 = _matmul(dvg0, W["s5_w_glu"], tb=True, name="glu_dx")
    dw_glu = _matmul(z0, dvg0, ta=True, name="glu_dw", tn=512, col_shards=NCH)
    (gends,) = _s5_bwd(u0, y0, dz0, ck0, zero_carry, tb, s5_d, emit=False, name="s5_bwd_ends")
    gcarry0 = _s5_fix(gends, tb["lamT"], reverse=True, name="s5_fix_bwd")
    du0, db_acc, dc_acc, dl_acc, dd_s5 = _s5_bwd(u0, y0, dz0, ck0, gcarry0, tb, s5_d, emit=True, name="s5_bwd")
    dxp, dsh1_0, dsc1_0, dg_mix0 = _rows(norm_bwd, [xp, du0, dxp], [vec(norm_mix[0]), m0[1]], [(D, F32)], vD * 3,
                                         name="norm_mix0_bwd")
    grad_x = _from_segments(dxp)[None]

    dmod = jnp.concatenate([dsh1_0, dsc1_0, dgt1_0, dsh2_0, dsc2_0, dgt2_0,
                            dsh1_1, dsc1_1, dgt1_1, dsh2_1, dsc2_1, dgt2_1], axis=1)
    dbb_re, dbb_im = _s5_untable(db_acc)
    dc_re, dc_im_neg = _s5_untable(dc_acc)
    nbk = D // 128
    dl = dl_acc.reshape(nbk, NSEG, 2, GPB * S5_P).sum(axis=1)
    smalls = [dmod, dg_mix0, dg_mix1, dg_mlp0, dg_mlp1, d_norm_final, dd_s5, dbg, dgn,
              dwg2p[:GATE_RANK].reshape(1, -1), dbb_re.reshape(1, -1), dbb_im.reshape(1, -1),
              dc_re.reshape(1, -1), dc_im_neg.reshape(1, -1), dl.reshape(1, -1)]
    ssz = [t.shape[1] for t in smalls]
    stot = sum(ssz)
    spad = -(-stot // 8192) * 8192
    svec = jnp.concatenate(smalls + [jnp.zeros((1, spad - stot), F32)], axis=1).reshape(NCH, spad // (128 * NCH), 128)

    dmod_all = _exchange(dmod.reshape(12 * D // 128, 128), MASK_ALL, name="gather_dmod").reshape(8, 2, 6 * D)
    dmod_cols = lax.dynamic_slice_in_dim(dmod_all, chip * acols, acols, axis=2).transpose(1, 0, 2)
    g_w_ada = _ada_grad(c_all, dmod_cols, name="ada_grad")

    dw_in = jnp.concatenate([dw_in_r[:, :4 * QK], dw_in_r[:, 6 * QK:6 * QK + GATE_RANK], dw_in_r[:, 4 * QK:6 * QK]], axis=1)
    dw_in = dw_in.reshape(D, NCH, -1).transpose(1, 0, 2)
    gsrc = [dw_glu, dw_in, dw_out.reshape(NCH, -1, D), dw_ff1_0, dw_ff1_1, dw_ff2_0.reshape(NCH, -1, D),
            dw_ff2_1.reshape(NCH, -1, D), svec]
    tags = [b[0] for b in big] + ["small"]
    r1 = _rs_cores(gsrc, name="rs_cores")
    s1 = [_sum_own_half(g, r, ci, F32 if nm == "small" else BF16, name=f"rs_sum_cores_{nm}")
          for g, r, nm in zip(gsrc, r1, tags)]
    r2 = _rs_chips(s1, name="rs_chips")

    def fin(k, **kw):
        return _sum_chips(r2[k], s1[k], chip, ci, name=f"rs_sum_chips_{tags[k]}", **kw)

    f_ff1 = fin(4, nlead=2, lead=1, prev=fin(3, nlead=2, lead=0))
    f_ff2 = fin(6, nlead=2, lead=1, prev=fin(5, nlead=2, lead=0))
    finals = [fin(0), fin(1), fin(2), f_ff1, f_ff2, fin(7, spread=True)]
    halves = [t.shape[1] for t in (s1[0], s1[1], s1[2], s1[3], s1[5], s1[7])]
    g_glu, g_in, g_out, g_w_ff1, g_w_ff2, s_own = _rs_gather(finals, halves, [False] * 5 + [True], name="rs_gather_cores")
    srows = spad // (128 * NCH)
    (s_sum,) = _gather_weights([s_own.reshape(NCH * srows, 128)], [(srows, 128)], ["row"], name="gather_small_grads")
    s_sum = s_sum.reshape(-1)
    so = [sum(ssz[:k]) for k in range(len(ssz))]
    sm = [s_sum[o:o + n] for o, n in zip(so, ssz)]
    (dmod_s, g_mix0, g_mix1, g_mlp0, g_mlp1, g_nf, g_d, g_bg, g_gn, g_wg2, g_bbre, g_bbim, g_cre, g_cimn, g_dl) = sm
    g_b_ada = dmod_s.reshape(2, 6 * D)

    G = D // S5_H
    _, disc_vjp = jax.vjp(_s5_disc, s5_a_re[0], s5_a_im[0], s5_log_dt[0], s5_b_re[0], s5_b_im[0])
    g_dl = g_dl.reshape(nbk, 2, GPB, S5_P)
    ct = (g_dl[:, 0].reshape(G, S5_P), g_dl[:, 1].reshape(G, S5_P),
          g_bbre.reshape(G, S5_H, S5_P).transpose(0, 2, 1), g_bbim.reshape(G, S5_H, S5_P).transpose(0, 2, 1))
    g_a_re, g_a_im, g_log_dt, g_b_re, g_b_im = disc_vjp(ct)
    g_c_re = g_cre.reshape(G, S5_H, S5_P)
    g_c_im = -g_cimn.reshape(G, S5_H, S5_P)
    g_wg2_s = lax.dynamic_slice_in_dim(g_wg2.reshape(GATE_RANK, QK), chip * qk4, qk4, axis=1)
    g_bg_s = lax.dynamic_slice_in_dim(g_bg.reshape(1, QK), chip * qk4, qk4, axis=1)
    g_gn_s = lax.dynamic_slice_in_dim(g_gn.reshape(1, D), chip * (D // NCH), D // NCH, axis=1)

    grads = dict(
        w_ada=g_w_ada, b_ada=g_b_ada, norm_mix=jnp.stack([g_mix0, g_mix1]), norm_mlp=jnp.stack([g_mlp0, g_mlp1]),
        s5_a_re=g_a_re[None], s5_a_im=g_a_im[None], s5_log_dt=g_log_dt[None], s5_b_re=g_b_re[None], s5_b_im=g_b_im[None],
        s5_c_re=g_c_re[None], s5_c_im=g_c_im[None], s5_d=g_d[None], s5_w_glu=g_glu,
        gla_w_in=g_in, gla_w_gate2=g_wg2_s[None], gla_b_gate=g_bg_s, gla_g_norm=g_gn_s,
        gla_w_out=g_out, w_ff1=g_w_ff1, w_ff2=g_w_ff2, norm_final=g_nf)

    names = list(grads)
    large = ("w_ada", "s5_w_glu", "gla_w_in", "gla_w_out", "w_ff1", "w_ff2")
    delta, new_m, new_v = {}, {}, {}
    for nm in large:
        delta[nm], new_m[nm], new_v[nm] = _adamw(args[nm], grads[nm], args["m_" + nm], args["v_" + nm], name=f"adamw_{nm}")
    small = [nm for nm in names if nm not in large]
    szs = [args[nm].size for nm in small]
    tot = sum(szs)
    pad = -(-tot // 32768) * 32768

    def pack(prefix, src):
        return jnp.concatenate([src[prefix + nm].reshape(-1) for nm in small] + [jnp.ones((pad - tot,), F32)]).reshape(1, -1, 128)

    d, m2, v2 = _adamw(pack("", args), pack("", grads), pack("m_", args), pack("v_", args), name="adamw_small")
    o = 0
    for nm, n in zip(small, szs):
        shp = args[nm].shape
        delta[nm], new_m[nm], new_v[nm] = (t.reshape(-1)[o:o + n].reshape(shp) for t in (d, m2, v2))
        o += n
    grads = {nm: grads[nm].reshape(args[nm].shape) for nm in names}
    return (loss, grad_x, *[grads[n] for n in names], *[delta[n] for n in names], *[new_m[n] for n in names],
            *[new_v[n] for n in names])
```

```python
import math

import jax
import jax.numpy as jnp
from jax import lax
from jax.experimental import pallas as pl
from jax.experimental.pallas import tpu as pltpu

F32 = jnp.float32
BF16 = jnp.bfloat16
MESH = pl.DeviceIdType.MESH

EPS = 1e-6
CHUNK = 64
GLA_NB = 4
S5_H = 16
S5_P = 64
GPB = 8
NSEG = 8
HEADS = 4
GATE_RANK = 16
GATE_TAU = 16.0
NCH = 4
LR, B1, B2, AEPS, WD, ASTEP = 0.001, 0.9, 0.999, 1e-08, 0.01, 10
VMEM_LIMIT = 56 << 20

MASK_CHIPS = ((1, 0, 0), (0, 1, 0), (1, 1, 0))
MASK_ALL = ((0, 0, 1), (0, 1, 0), (0, 1, 1), (1, 0, 0), (1, 0, 1), (1, 1, 0), (1, 1, 1))


def _params(*sem):
    return pltpu.CompilerParams(dimension_semantics=sem or None, vmem_limit_bytes=VMEM_LIMIT)


def _tile_rows(rows, cap=512):
    best = 8
    for t in range(8, cap + 1, 8):
        if rows % t == 0:
            best = t
    return best


def _whole(shape):
    return pl.BlockSpec(shape, lambda i, _n=len(shape): (0,) * _n)


def _matmul(a, b, *, name, ta=False, tb=False, tm=1024, tn=1024, tk=1024, out_dtypes=(F32,),
            a_fn=None, epi=None, epi_ins=(), col_shards=1):
    M, K = (a.shape[1], a.shape[0]) if ta else a.shape
    N = b.shape[0] if tb else b.shape[1]
    tm, tn, tk = min(tm, M), min(tn, N), min(tk, K)
    assert M % tm == 0 and N % tn == 0 and K % tk == 0, (name, M, N, K)
    nk = K // tk
    ne = len(epi_ins)
    dn = (((0 if ta else 1,), (1 if tb else 0,)), ((), ()))

    def body(a_ref, b_ref, *rest):
        e_refs, o_refs, acc = rest[:ne], rest[ne:-1], rest[-1]
        k = pl.program_id(2)

        @pl.when(k == 0)
        def _():
            acc[...] = jnp.zeros_like(acc)

        at = a_ref[...]
        if a_fn is not None:
            at = a_fn(at)
        acc[...] += lax.dot_general(at.astype(BF16), b_ref[...].astype(BF16), dn, preferred_element_type=F32)

        @pl.when(k == nk - 1)
        def _():
            outs = (acc[...],) if epi is None else epi(acc[...], *[r[...] for r in e_refs])
            for r, o in zip(o_refs, outs):
                r[...] = o.astype(r.dtype)

    a_spec = pl.BlockSpec((tk, tm), lambda i, j, k: (k, i)) if ta else pl.BlockSpec((tm, tk), lambda i, j, k: (i, k))
    b_spec = pl.BlockSpec((tn, tk), lambda i, j, k: (j, k)) if tb else pl.BlockSpec((tk, tn), lambda i, j, k: (k, j))
    o_spec = pl.BlockSpec((tm, tn), lambda i, j, k: (i, j))
    if col_shards > 1:
        per = N // col_shards // tn
        assert ne == 0 and per * tn * col_shards == N
        w_spec = pl.BlockSpec((None, tm, tn), lambda i, j, k: (j // per, i, j % per))
        o_shape = (col_shards, M, N // col_shards)
    else:
        w_spec, o_shape = o_spec, (M, N)
    outs = pl.pallas_call(
        body, name=name, grid=(M // tm, N // tn, nk),
        in_specs=[a_spec, b_spec] + [o_spec] * ne,
        out_specs=[w_spec] * len(out_dtypes),
        out_shape=[jax.ShapeDtypeStruct(o_shape, d) for d in out_dtypes],
        scratch_shapes=[pltpu.VMEM((tm, tn), F32)],
        compiler_params=_params("parallel", "parallel", "arbitrary"),
    )(a, b, *epi_ins)
    return outs[0] if len(outs) == 1 else outs


def _rows(fn, rows_in, vecs_in, rows_out, acc_out, *, name, tm=256):
    L = rows_in[0].shape[0]
    tm = min(tm, L)
    assert L % tm == 0
    nr, nv, no, na = len(rows_in), len(vecs_in), len(rows_out), len(acc_out)

    def body(*refs):
        rin, vin = refs[:nr], refs[nr:nr + nv]
        rout, aout = refs[nr + nv:nr + nv + no], refs[nr + nv + no:]
        outs = fn(*[r[...] for r in rin], *[v[...] for v in vin])
        for r, o in zip(rout, outs[:no]):
            r[...] = o.astype(r.dtype)
        if na:
            @pl.when(pl.program_id(0) == 0)
            def _():
                for r in aout:
                    r[...] = jnp.zeros_like(r)

            for r, o in zip(aout, outs[no:]):
                r[...] += o

    outs = pl.pallas_call(
        body, name=name, grid=(L // tm,),
        in_specs=[pl.BlockSpec((tm, r.shape[1]), lambda i: (i, 0)) for r in rows_in] + [_whole(v.shape) for v in vecs_in],
        out_specs=[pl.BlockSpec((tm, c), lambda i: (i, 0)) for c, _ in rows_out] + [_whole(s) for s in acc_out],
        out_shape=[jax.ShapeDtypeStruct((L, c), d) for c, d in rows_out] + [jax.ShapeDtypeStruct(s, F32) for s in acc_out],
        compiler_params=_params("arbitrary"),
    )(*rows_in, *vecs_in)
    return outs


def _rsum(t):
    return jnp.sum(t, axis=0, keepdims=True)


def _norm_mod(x, g, sc, sh):
    rs = lax.rsqrt(jnp.mean(x * x, axis=-1, keepdims=True) + EPS)
    return x * rs * g * (1.0 + sc) + sh


def _norm_mod_bwd(x, dh, g, sc):
    rs = lax.rsqrt(jnp.mean(x * x, axis=-1, keepdims=True) + EPS)
    xh = x * rs
    dn = dh * (1.0 + sc)
    dxh = dn * g
    dx = rs * (dxh - xh * jnp.mean(dxh * xh, axis=-1, keepdims=True))
    return dx, _rsum(dh), _rsum(dh * xh * g), _rsum(dn * xh)


def _sigmoid(x):
    return jax.nn.sigmoid(x)


def _gelu(y):
    return jax.nn.gelu(y, approximate=True)


def _gelu_grad(y):
    c = math.sqrt(2.0 / math.pi)
    t = jnp.tanh(c * (y + 0.044715 * y * y * y))
    return 0.5 * (1.0 + t) + 0.5 * y * (1.0 - t * t) * c * (1.0 + 3.0 * 0.044715 * y * y)


def _s5_tables(lb_re, lb_im, bb_re, bb_im, c_re, c_im, seg_len):
    G = lb_re.shape[0]
    nb = G // GPB
    eye = jnp.eye(GPB, dtype=F32)

    def bdiag(t):
        a, b = t.shape[1:]
        t = t.reshape(nb, GPB, a, b)
        return (t[:, :, :, None, :] * eye[None, :, None, :, None]).reshape(nb, GPB * a, GPB * b)

    bbd = jnp.concatenate([bdiag(bb_re.transpose(0, 2, 1)), bdiag(bb_im.transpose(0, 2, 1))], axis=2)
    cbd = jnp.concatenate([bdiag(c_re.transpose(0, 2, 1)), -bdiag(c_im.transpose(0, 2, 1))], axis=1)

    def lanes(re, im):
        t = jnp.concatenate([re.reshape(nb, GPB * S5_P), im.reshape(nb, GPB * S5_P)], axis=1)
        return jnp.repeat(t, NSEG, axis=0)

    tr, ti = lb_re, lb_im
    for _ in range(int(math.log2(seg_len))):
        tr, ti = tr * tr - ti * ti, 2.0 * tr * ti
    return dict(bbd=bbd.astype(BF16), bbdT=bbd.transpose(0, 2, 1).astype(BF16), cbd=cbd.astype(BF16),
                cbdT=cbd.transpose(0, 2, 1).astype(BF16), lam=lanes(lb_re, lb_im), lamT=lanes(tr, ti))


def _s5_untable(acc):
    nb = acc.shape[0]
    t = acc.reshape(nb, GPB, S5_H, 2, GPB, S5_P)
    d = jnp.diagonal(t, axis1=1, axis2=4)
    d = d.transpose(0, 4, 2, 1, 3).reshape(nb * GPB, 2, S5_H, S5_P)
    return d[:, 0], d[:, 1]


def _s5_fix(ends, lamT, *, reverse, name):
    nrow = ends.shape[0]
    half = ends.shape[1] // 2

    def body(e_ref, t_ref, o_ref):
        for gb in range(nrow // NSEG):
            r0 = gb * NSEG
            tr, ti = t_ref[r0:r0 + 1, :half], t_ref[r0:r0 + 1, half:]
            cr = jnp.zeros((1, half), F32)
            ci = jnp.zeros((1, half), F32)
            order = range(NSEG - 1, -1, -1) if reverse else range(NSEG)
            for n, s in enumerate(order):
                if n > 0:
                    p = s + 1 if reverse else s - 1
                    er, ei = e_ref[r0 + p:r0 + p + 1, :half], e_ref[r0 + p:r0 + p + 1, half:]
                    if reverse:
                        cr, ci = tr * cr + ti * ci + er, tr * ci - ti * cr + ei
                    else:
                        cr, ci = tr * cr - ti * ci + er, tr * ci + ti * cr + ei
                o_ref[r0 + s:r0 + s + 1, :half] = cr
                o_ref[r0 + s:r0 + s + 1, half:] = ci

    return pl.pallas_call(body, name=name, out_shape=jax.ShapeDtypeStruct(ends.shape, F32),
                          compiler_params=_params())(ends, lamT)


def _s5_fwd(up, carry_in, tb, dvec, *, emit, name, R=256):
    L, D = up.shape
    R = min(R, L)
    nb, ta, ngb = L // R, R // NSEG, D // 128
    SW = GPB * S5_P
    crows = ngb * NSEG

    def body(u_ref, cin_ref, lam_ref, b_ref, c_ref, d_ref, *rest):
        if emit:
            y_ref, z_ref, ck_ref, carry, xbuf = rest
        else:
            cout_ref, carry, xbuf = rest
        i = pl.program_id(0)

        @pl.when(i == 0)
        def _():
            carry[...] = cin_ref[...]

        if emit:
            ck_ref[0] = carry[...]
        for gb in range(ngb):
            cols = slice(gb * 128, (gb + 1) * 128)
            rws = slice(gb * NSEG, (gb + 1) * NSEG)
            ug = u_ref[:, cols]
            xbuf[...] = jnp.dot(ug.astype(BF16), b_ref[gb], preferred_element_type=F32)
            lr, li = lam_ref[rws, :SW], lam_ref[rws, SW:]

            def step(a, c, lr=lr, li=li):
                cr, ci = c
                o = pl.multiple_of(a * NSEG, NSEG)
                nr = lr * cr - li * ci + xbuf[pl.ds(o, NSEG), :SW]
                ni = lr * ci + li * cr + xbuf[pl.ds(o, NSEG), SW:]
                if emit:
                    xbuf[pl.ds(o, NSEG), :SW] = nr
                    xbuf[pl.ds(o, NSEG), SW:] = ni
                return nr, ni

            cr, ci = lax.fori_loop(0, ta, step, (carry[rws, :SW], carry[rws, SW:]), unroll=2)
            carry[rws, :SW] = cr
            carry[rws, SW:] = ci
            if emit:
                y = jnp.dot(xbuf[...].astype(BF16), c_ref[gb], preferred_element_type=F32) + d_ref[:, cols] * ug
                y_ref[:, cols] = y
                z_ref[:, cols] = _gelu(y).astype(BF16)
        if not emit:
            cout_ref[...] = carry[...]

    rowblk = pl.BlockSpec((R, D), lambda i: (i, 0))
    if emit:
        out_shape = [jax.ShapeDtypeStruct((L, D), F32), jax.ShapeDtypeStruct((L, D), BF16),
                     jax.ShapeDtypeStruct((nb, crows, 2 * SW), F32)]
        out_specs = [rowblk, rowblk, pl.BlockSpec((1, crows, 2 * SW), lambda i: (i, 0, 0))]
    else:
        out_shape = [jax.ShapeDtypeStruct((crows, 2 * SW), F32)]
        out_specs = [_whole((crows, 2 * SW))]
    return pl.pallas_call(
        body, name=name, grid=(nb,),
        in_specs=[rowblk, _whole(carry_in.shape), _whole(tb["lam"].shape), _whole(tb["bbd"].shape),
                  _whole(tb["cbd"].shape), _whole(dvec.shape)],
        out_specs=out_specs, out_shape=out_shape,
        scratch_shapes=[pltpu.VMEM((crows, 2 * SW), F32), pltpu.VMEM((R, 2 * SW), F32)],
        compiler_params=_params("arbitrary"),
    )(up, carry_in, tb["lam"], tb["bbd"], tb["cbd"], dvec)


def _s5_bwd(up, y, dz, ck, gcarry_in, tb, dvec, *, emit, name, R=256):
    L, D = up.shape
    R = min(R, L)
    nb, ta, ngb = L // R, R // NSEG, D // 128
    SW = GPB * S5_P
    crows = ngb * NSEG

    def body(u_ref, y_ref, dz_ref, ck_ref, gin_ref, lam_ref, b_ref, bt_ref, ct_ref, d_ref, *rest):
        if emit:
            du_ref, db_ref, dc_ref, dl_ref, dd_ref, gcarry, xbuf, gbuf, dybuf = rest
        else:
            gout_ref, gcarry, gbuf, dybuf = rest
        i = pl.program_id(0)

        @pl.when(i == 0)
        def _():
            gcarry[...] = gin_ref[...]
            if emit:
                db_ref[...] = jnp.zeros_like(db_ref)
                dc_ref[...] = jnp.zeros_like(dc_ref)
                dl_ref[...] = jnp.zeros_like(dl_ref)
                dd_ref[...] = jnp.zeros_like(dd_ref)

        dybuf[...] = dz_ref[...] * _gelu_grad(y_ref[...])
        for gb in range(ngb):
            cols = slice(gb * 128, (gb + 1) * 128)
            rws = slice(gb * NSEG, (gb + 1) * NSEG)
            dyg = dybuf[:, cols]
            lr, li = lam_ref[rws, :SW], lam_ref[rws, SW:]
            gbuf[...] = jnp.dot(dyg.astype(BF16), ct_ref[gb], preferred_element_type=F32)
            if emit:
                ug = u_ref[:, cols]
                xbuf[0:NSEG, :] = ck_ref[0, rws, :]
                xbuf[NSEG:, :] = jnp.dot(ug.astype(BF16), b_ref[gb], preferred_element_type=F32)

                def fstep(a, c, lr=lr, li=li):
                    cr, ci = c
                    o = pl.multiple_of(a * NSEG + NSEG, NSEG)
                    nr = lr * cr - li * ci + xbuf[pl.ds(o, NSEG), :SW]
                    ni = lr * ci + li * cr + xbuf[pl.ds(o, NSEG), SW:]
                    xbuf[pl.ds(o, NSEG), :SW] = nr
                    xbuf[pl.ds(o, NSEG), SW:] = ni
                    return nr, ni

                lax.fori_loop(0, ta, fstep, (xbuf[0:NSEG, :SW], xbuf[0:NSEG, SW:]), unroll=2)

            def rstep(k, c, lr=lr, li=li):
                o = pl.multiple_of((ta - 1 - k) * NSEG, NSEG)
                gr_n, gi_n = c[0], c[1]
                gr = gbuf[pl.ds(o, NSEG), :SW] + lr * gr_n + li * gi_n
                gi = gbuf[pl.ds(o, NSEG), SW:] - li * gr_n + lr * gi_n
                if not emit:
                    return gr, gi
                gbuf[pl.ds(o, NSEG), :SW] = gr
                gbuf[pl.ds(o, NSEG), SW:] = gi
                xr, xi = xbuf[pl.ds(o, NSEG), :SW], xbuf[pl.ds(o, NSEG), SW:]
                return gr, gi, c[2] + gr * xr + gi * xi, c[3] + gi * xr - gr * xi

            c0 = (gcarry[rws, :SW], gcarry[rws, SW:])
            if emit:
                c0 = c0 + (jnp.zeros((NSEG, SW), F32), jnp.zeros((NSEG, SW), F32))
            cf = lax.fori_loop(0, ta, rstep, c0, unroll=2)
            gcarry[rws, :SW] = cf[0]
            gcarry[rws, SW:] = cf[1]
            if emit:
                dl_ref[rws, :SW] += cf[2]
                dl_ref[rws, SW:] += cf[3]
                gb16 = gbuf[...].astype(BF16)
                du_ref[:, cols] = jnp.dot(gb16, bt_ref[gb], preferred_element_type=F32) + d_ref[:, cols] * dyg
                tn = (((0,), (0,)), ((), ()))
                db_ref[gb] += lax.dot_general(ug.astype(BF16), gb16, tn, preferred_element_type=F32)
                dc_ref[gb] += lax.dot_general(dyg.astype(BF16), xbuf[NSEG:, :].astype(BF16), tn,
                                              preferred_element_type=F32)
                dd_ref[:, cols] += _rsum(dyg * ug)
        if not emit:
            gout_ref[...] = gcarry[...]

    rev = pl.BlockSpec((R, D), lambda i: (nb - 1 - i, 0))
    acc3 = (ngb, 128, 2 * SW)
    if emit:
        out_shape = [jax.ShapeDtypeStruct((L, D), F32), jax.ShapeDtypeStruct(acc3, F32), jax.ShapeDtypeStruct(acc3, F32),
                     jax.ShapeDtypeStruct((crows, 2 * SW), F32), jax.ShapeDtypeStruct((1, D), F32)]
        out_specs = [rev, _whole(acc3), _whole(acc3), _whole((crows, 2 * SW)), _whole((1, D))]
        scratch = [pltpu.VMEM((crows, 2 * SW), F32), pltpu.VMEM((R + NSEG, 2 * SW), F32),
                   pltpu.VMEM((R, 2 * SW), F32), pltpu.VMEM((R, D), F32)]
    else:
        out_shape = [jax.ShapeDtypeStruct((crows, 2 * SW), F32)]
        out_specs = [_whole((crows, 2 * SW))]
        scratch = [pltpu.VMEM((crows, 2 * SW), F32), pltpu.VMEM((R, 2 * SW), F32), pltpu.VMEM((R, D), F32)]
    return pl.pallas_call(
        body, name=name, grid=(nb,),
        in_specs=[rev, rev, rev, pl.BlockSpec((1, crows, 2 * SW), lambda i: (nb - 1 - i, 0, 0)),
                  _whole(gcarry_in.shape), _whole(tb["lam"].shape), _whole(tb["bbd"].shape),
                  _whole(tb["bbdT"].shape), _whole(tb["cbdT"].shape), _whole(dvec.shape)],
        out_specs=out_specs, out_shape=out_shape, scratch_shapes=scratch,
        compiler_params=_params("arbitrary"),
    )(up, y, dz, ck, gcarry_in, tb["lam"], tb["bbd"], tb["bbdT"], tb["cbdT"], dvec)


NN = (((1,), (0,)), ((), ()))
TN = (((0,), (0,)), ((), ()))
NT = (((1,), (1,)), ((), ()))


def _dot3(lhs, rhs, dn, split):
    x = rhs if split == "rhs" else lhs
    hi = x.astype(BF16)
    r1 = x - hi.astype(F32)
    mid = r1.astype(BF16)
    lo = (r1 - mid.astype(F32)).astype(BF16)
    out = None
    for part in (hi, mid, lo):
        ops = (lhs, part) if split == "rhs" else (part, rhs)
        t = lax.dot_general(ops[0], ops[1], dn, preferred_element_type=F32)
        out = t if out is None else out + t
    return out


def _log_sigmoid(x):
    return jnp.minimum(x, 0.0) - jnp.log(1.0 + jnp.exp(-jnp.abs(x)))


def _gla_gates(p, wg_ref, bg_ref, QK):
    C = p.shape[0]
    glr = p[:, 6 * QK:6 * QK + 128].astype(BF16)
    gpre = jnp.dot(glr, wg_ref[...], preferred_element_type=F32) + bg_ref[...]
    la = _log_sigmoid(gpre) * (1.0 / GATE_TAU)
    row = lax.broadcasted_iota(jnp.int32, (C, C), 0)
    col = lax.broadcasted_iota(jnp.int32, (C, C), 1)
    gc = _dot3((row >= col).astype(BF16), la, NN, "rhs")
    ge = gc[C - 1:C, :]
    w = jnp.exp(ge - gc)
    return glr, gpre, la, ge, w


def _gla_fwd(proj, wg2p, bg, gn, *, name):
    L = proj.shape[0]
    QK = wg2p.shape[1]
    DK, DV = QK // HEADS, 2 * QK // HEADS
    nC = L // CHUNK
    NB = min(GLA_NB, nC)
    assert nC % NB == 0
    scale = DK ** -0.5

    def body(p_ref, wg_ref, bg_ref, gn_ref, og_ref, s_ref, sst):
        @pl.when(pl.program_id(0) == 0)
        def _():
            sst[...] = jnp.zeros_like(sst)

        ones = jnp.ones((CHUNK, DV), BF16)
        for cc in range(NB):
            rows = slice(cc * CHUNK, (cc + 1) * CHUNK)
            p = p_ref[rows, :]
            _, _, la, _, w = _gla_gates(p, wg_ref, bg_ref, QK)
            for h in range(HEADS):
                ks, vs = slice(h * DK, (h + 1) * DK), slice(h * DV, (h + 1) * DV)
                q = p[:, h * DK:(h + 1) * DK] * scale
                kd = p[:, QK + h * DK:QK + (h + 1) * DK] * w[:, ks]
                v = p[:, 2 * QK + h * DV:2 * QK + (h + 1) * DV]
                r = p[:, 4 * QK + h * DV:4 * QK + (h + 1) * DV]
                dec = jnp.exp(_dot3(la[:, ks], ones, TN, "lhs"))
                kv = lax.dot_general(kd.astype(BF16), v.astype(BF16), TN, preferred_element_type=F32)
                S = dec * sst[ks, :] + kv
                sst[ks, :] = S
                s_ref[cc, ks, :] = S
                o = jnp.dot(q.astype(BF16), S.astype(BF16), preferred_element_type=F32)
                on = o * lax.rsqrt(jnp.mean(o * o, axis=-1, keepdims=True) + EPS)
                og_ref[rows, vs] = (on * gn_ref[:, vs] * (r * _sigmoid(r))).astype(BF16)

    RB = NB * CHUNK
    return pl.pallas_call(
        body, name=name, grid=(nC // NB,),
        in_specs=[pl.BlockSpec((RB, proj.shape[1]), lambda i: (i, 0)), _whole(wg2p.shape), _whole(bg.shape), _whole(gn.shape)],
        out_specs=[pl.BlockSpec((RB, 2 * QK), lambda i: (i, 0)), pl.BlockSpec((NB, QK, DV), lambda i: (i, 0, 0))],
        out_shape=[jax.ShapeDtypeStruct((L, 2 * QK), BF16), jax.ShapeDtypeStruct((nC, QK, DV), F32)],
        scratch_shapes=[pltpu.VMEM((QK, DV), F32)],
        compiler_params=_params("arbitrary"),
    )(proj, wg2p, bg, gn)


def _gla_bwd(proj, dog, states, wg2p, bg, gn, *, name):
    L, W = proj.shape
    QK = wg2p.shape[1]
    DK, DV = QK // HEADS, 2 * QK // HEADS
    nC = L // CHUNK
    NB = min(GLA_NB, nC)
    nB = nC // NB
    scale = DK ** -0.5

    def body(p_ref, dog_ref, sc_ref, sp_ref, wg_ref, bg_ref, gn_ref, dp_ref, dwg_ref, dbg_ref, dgn_ref, gst):
        i = pl.program_id(0)

        @pl.when(i == 0)
        def _():
            gst[...] = jnp.zeros_like(gst)
            dwg_ref[...] = jnp.zeros_like(dwg_ref)
            dbg_ref[...] = jnp.zeros_like(dbg_ref)
            dgn_ref[...] = jnp.zeros_like(dgn_ref)

        row = lax.broadcasted_iota(jnp.int32, (CHUNK, CHUNK), 0)
        col = lax.broadcasted_iota(jnp.int32, (CHUNK, CHUNK), 1)
        tri_u = (col >= row).astype(BF16)
        ones = jnp.ones((CHUNK, DV), BF16)
        ones8 = jnp.ones((8, DV), BF16)
        for cc in range(NB - 1, -1, -1):
            rows = slice(cc * CHUNK, (cc + 1) * CHUNK)
            p = p_ref[rows, :]
            glr, gpre, la, ge, w = _gla_gates(p, wg_ref, bg_ref, QK)
            dla_heads = []
            for h in range(HEADS):
                ks, vs = slice(h * DK, (h + 1) * DK), slice(h * DV, (h + 1) * DV)
                qs = p[:, h * DK:(h + 1) * DK] * scale
                k = p[:, QK + h * DK:QK + (h + 1) * DK]
                v = p[:, 2 * QK + h * DV:2 * QK + (h + 1) * DV]
                r = p[:, 4 * QK + h * DV:4 * QK + (h + 1) * DV]
                wh = w[:, ks]
                kd = k * wh
                S = sc_ref[cc, ks, :]
                if cc > 0:
                    Sp = sc_ref[cc - 1, ks, :]
                else:
                    Sp = jnp.where(i < nB - 1, sp_ref[0, ks, :], 0.0)
                o = jnp.dot(qs.astype(BF16), S.astype(BF16), preferred_element_type=F32)
                rs = lax.rsqrt(jnp.mean(o * o, axis=-1, keepdims=True) + EPS)
                on = o * rs
                sr = _sigmoid(r)
                dg = dog_ref[rows, vs]
                gnh = gn_ref[:, vs]
                dp_ref[rows, 4 * QK + h * DV:4 * QK + (h + 1) * DV] = (
                    dg * on * gnh * (sr * (1.0 + r * (1.0 - sr)))).astype(BF16)
                dt = dg * (r * sr)
                dgn_ref[:, vs] += _rsum(dt * on)
                don = dt * gnh
                do = (rs * (don - on * jnp.mean(don * on, axis=-1, keepdims=True))).astype(BF16)
                Gc = gst[ks, :] + lax.dot_general(qs.astype(BF16), do, TN, preferred_element_type=F32)
                G16 = Gc.astype(BF16)
                dp_ref[rows, h * DK:(h + 1) * DK] = (
                    lax.dot_general(do, S.astype(BF16), NT, preferred_element_type=F32) * scale).astype(BF16)
                dkd = lax.dot_general(v.astype(BF16), G16, NT, preferred_element_type=F32)
                dp_ref[rows, 2 * QK + h * DV:2 * QK + (h + 1) * DV] = jnp.dot(
                    kd.astype(BF16), G16, preferred_element_type=F32).astype(BF16)
                gst[ks, :] = jnp.exp(_dot3(la[:, ks], ones, TN, "lhs")) * Gc
                ddec = _dot3(ones8, Gc * Sp, NT, "rhs")[0:1, :]
                dp_ref[rows, QK + h * DK:QK + (h + 1) * DK] = (dkd * wh).astype(BF16)
                dww = dkd * kd
                dge = jnp.exp(ge[:, ks]) * ddec + _rsum(dww)
                dla_heads.append(dge - _dot3(tri_u, dww, NN, "rhs"))
            dla = jnp.concatenate(dla_heads, axis=1)
            dgpre = dla * (1.0 / GATE_TAU) * (1.0 - _sigmoid(gpre))
            d16 = dgpre.astype(BF16)
            dp_ref[rows, 6 * QK:6 * QK + 128] = lax.dot_general(d16, wg_ref[...], NT, preferred_element_type=F32).astype(BF16)
            dwg_ref[...] += lax.dot_general(glr, d16, TN, preferred_element_type=F32)
            dbg_ref[...] += _rsum(dgpre)

    RB = NB * CHUNK
    rev = lambda i: (nB - 1 - i, 0)
    return pl.pallas_call(
        body, name=name, grid=(nB,),
        in_specs=[pl.BlockSpec((RB, W), rev), pl.BlockSpec((RB, 2 * QK), rev),
                  pl.BlockSpec((NB, QK, DV), lambda i: (nB - 1 - i, 0, 0)),
                  pl.BlockSpec((1, QK, DV), lambda i: (jnp.maximum(NB * (nB - 1 - i) - 1, 0), 0, 0)),
                  _whole(wg2p.shape), _whole(bg.shape), _whole(gn.shape)],
        out_specs=[pl.BlockSpec((RB, W), rev), _whole((128, QK)), _whole((1, QK)), _whole((1, 2 * QK))],
        out_shape=[jax.ShapeDtypeStruct((L, W), BF16), jax.ShapeDtypeStruct((128, QK), F32),
                   jax.ShapeDtypeStruct((1, QK), F32), jax.ShapeDtypeStruct((1, 2 * QK), F32)],
        scratch_shapes=[pltpu.VMEM((QK, DV), F32)],
        compiler_params=_params("arbitrary"),
    )(proj, dog, states, states, wg2p, bg, gn)


def _coords():
    return lax.axis_index("x"), lax.axis_index("y"), lax.axis_index("c")


def _other_chips(x, y):
    return [(1 - x, y, 2 * (1 - x) + y), (x, 1 - y, 2 * x + 1 - y), (1 - x, 1 - y, 2 * (1 - x) + 1 - y)]


def _hbm_call(body, ins, out_shapes, n_sems, *, name, alias=False):
    any_spec = pl.BlockSpec(memory_space=pl.ANY)
    return pl.pallas_call(
        body, name=name, in_specs=[any_spec] * len(ins), out_specs=[any_spec] * len(out_shapes), out_shape=out_shapes,
        scratch_shapes=[pltpu.SemaphoreType.DMA((n,)) for n in n_sems],
        input_output_aliases={k: k for k in range(len(ins))} if alias else {},
    )(*ins)


def _exchange(src, masks, *, name):
    vary = [any(m[k] for m in masks) for k in range(3)]
    nslots = 2 ** sum(vary)
    n = len(masks)

    def slot(coords):
        s = 0
        for k in range(3):
            if vary[k]:
                s = s * 2 + coords[k]
        return s

    def body(src_ref, dst_ref, send_sems, recv_sems, loc_sem):
        me = _coords()
        mine = slot(me)
        loc = pltpu.make_async_copy(src_ref, dst_ref.at[mine], loc_sem.at[0])
        loc.start()
        copies = []
        for k, m in enumerate(masks):
            peer = tuple(1 - me[d] if m[d] else me[d] for d in range(3))
            cp = pltpu.make_async_remote_copy(src_ref=src_ref, dst_ref=dst_ref.at[mine], send_sem=send_sems.at[k],
                                              recv_sem=recv_sems.at[k], device_id=peer, device_id_type=MESH)
            cp.start()
            copies.append(cp)
        for cp in copies:
            cp.wait()
        loc.wait()

    return _hbm_call(body, [src], [jax.ShapeDtypeStruct((nslots,) + tuple(src.shape), src.dtype)], (n, n, 1), name=name)[0]


def _cast_into(t, lead, kind, chip, *, name, tm=256):
    r, cc = t.shape[-2:]
    tm = min(tm, r)
    nblk = r // tm
    if kind == "col":
        shp, o_spec = (r, NCH * cc), pl.BlockSpec((tm, cc), lambda i, s: (i, s[0]))
    elif kind == "row":
        shp, o_spec = (NCH * r, cc), pl.BlockSpec((tm, cc), lambda i, s: (s[0] * nblk + i, 0))
    else:
        shp, o_spec = (NCH, r, cc), pl.BlockSpec((None, tm, cc), lambda i, s: (s[0], i, 0))

    def body(s_ref, t_ref, o_ref):
        o_ref[...] = t_ref[...].astype(o_ref.dtype)

    return pl.pallas_call(
        body, name=name,
        grid_spec=pltpu.PrefetchScalarGridSpec(
            num_scalar_prefetch=1, grid=(nblk,),
            in_specs=[pl.BlockSpec((None, tm, cc), lambda i, s: (lead, i, 0))], out_specs=o_spec),
        out_shape=jax.ShapeDtypeStruct(shp, BF16), compiler_params=_params("parallel"),
    )(chip.reshape(1).astype(jnp.int32), t)


def _gather_weights(arrs, shard_shapes, kinds, *, name):
    n = len(arrs)

    def body(*refs):
        dst = refs[n:2 * n]
        send_sems, recv_sems = refs[2 * n:]
        x, y, c = _coords()
        chip = 2 * x + y
        others = _other_chips(x, y)
        sib = (x, y, 1 - c)

        def window(p, chip_id, cc):
            r, cols = shard_shapes[p]
            h = r // 2
            if kinds[p] == "col":
                return dst[p].at[pl.ds(cc * h, h), pl.ds(pl.multiple_of(chip_id * cols, 128), cols)]
            if kinds[p] == "row":
                return dst[p].at[pl.ds(chip_id * r + cc * h, h), :]
            return dst[p].at[chip_id, pl.ds(cc * h, h), :]

        def copy(p, k, win, to):
            return pltpu.make_async_remote_copy(src_ref=win, dst_ref=win, send_sem=send_sems.at[6 * p + k],
                                                recv_sem=recv_sems.at[6 * p + k], device_id=to, device_id_type=MESH)

        sends = []
        for p in range(n):
            for j, (ox, oy, _) in enumerate(others):
                cp = copy(p, j, window(p, chip, c), (ox, oy, c))
                cp.start()
                sends.append(cp)
        for j, (_, _, oc) in enumerate(others):
            for p in range(n):
                copy(p, j, window(p, oc, c), (x, y, c)).wait_recv()
                fw = copy(p, 3 + j, window(p, oc, c), sib)
                fw.start()
                sends.append(fw)
        for p in range(n):
            for j, (_, _, oc) in enumerate(others):
                copy(p, 3 + j, window(p, oc, 1 - c), sib).wait_recv()
        for cp in sends:
            cp.wait_send()

    outs = [jax.ShapeDtypeStruct(a.shape, a.dtype) for a in arrs]
    return _hbm_call(body, arrs, outs, (6 * n, 6 * n), name=name, alias=True)


HBM_SPEC = pl.BlockSpec(memory_space=pltpu.HBM)
SEM_SPEC = pl.BlockSpec(memory_space=pltpu.SEMAPHORE)
EFFECT = pltpu.SideEffectType.DATAFLOW_SIDE_EFFECTING


def _window(ref, shard_shape, kind, chip_id, cc):
    r, cols = shard_shape
    h = r // 2
    if kind == "col":
        return ref.at[pl.ds(cc * h, h), pl.ds(pl.multiple_of(chip_id * cols, 128), cols)]
    if kind == "row":
        return ref.at[pl.ds(chip_id * r + cc * h, h), :]
    return ref.at[chip_id, pl.ds(cc * h, h), :]


def _split_start(start, arrs, n_sems, *, name):
    n, ns = len(arrs), len(n_sems)

    def body(*refs):
        start(refs[:n], refs[n:n + ns])
        refs[-1][...] = jnp.zeros_like(refs[-1])

    outs = pl.pallas_call(
        body, name=name,
        out_shape=tuple([pltpu.SemaphoreType.DMA((k,)) for k in n_sems] + [pltpu.HBM(a.shape, a.dtype) for a in arrs]
                        + [jax.ShapeDtypeStruct((8, 128), F32)]),
        in_specs=[HBM_SPEC] * n, out_specs=tuple([SEM_SPEC] * ns + [HBM_SPEC] * n + [pl.BlockSpec(memory_space=pltpu.VMEM)]),
        input_output_aliases={k: ns + k for k in range(n)},
        compiler_params=pltpu.CompilerParams(has_side_effects=EFFECT),
    )(*[pltpu.with_memory_space_constraint(a, pltpu.HBM) for a in arrs])
    return list(outs[:ns]), list(outs[ns:ns + n]), outs[-1]


def _split_wait(wait, arrs, sems, after, *, name):
    n, ns = len(arrs), len(sems)

    def body(*refs):
        wait(refs[:n], refs[n:n + ns])

    return pl.pallas_call(
        body, name=name, out_shape=tuple(pltpu.HBM(a.shape, a.dtype) for a in arrs),
        in_specs=[HBM_SPEC] * n + [SEM_SPEC] * ns + [pl.BlockSpec(memory_space=pl.ANY)], out_specs=tuple([HBM_SPEC] * n),
        input_output_aliases={k: k for k in range(n)},
        compiler_params=pltpu.CompilerParams(has_side_effects=EFFECT),
    )(*arrs, *sems, after)


def _gw_copies(refs, send_sems, recv_sems, shard_shapes, kinds, outgoing):
    x, y, c = _coords()
    chip = 2 * x + y
    out = []
    for p in range(len(refs)):
        for j, (ox, oy, oc) in enumerate(_other_chips(x, y)):
            win = _window(refs[p], shard_shapes[p], kinds[p], chip if outgoing else oc, c)
            out.append(pltpu.make_async_remote_copy(
                src_ref=win, dst_ref=win, send_sem=send_sems.at[3 * p + j], recv_sem=recv_sems.at[3 * p + j],
                device_id=(ox, oy, c), device_id_type=MESH))
    return out


def _gw_start(arrs, shard_shapes, kinds, groups, *, name):
    def start(refs, sems):
        for g, idx in enumerate(groups):
            for cp in _gw_copies([refs[p] for p in idx], sems[2 * g], sems[2 * g + 1], [shard_shapes[p] for p in idx],
                                 [kinds[p] for p in idx], True):
                cp.start()

    n_sems = [3 * len(idx) for idx in groups for _ in range(2)]
    sems, thru, token = _split_start(start, arrs, n_sems, name=name)
    return [(sems[2 * g], sems[2 * g + 1]) for g in range(len(groups))], thru, token


def _gw_wait(arrs, shard_shapes, kinds, sem_pair, after, *, name):
    def wait(refs, sems):
        for cp in _gw_copies(refs, sems[0], sems[1], shard_shapes, kinds, True):
            cp.wait_send()
        for cp in _gw_copies(refs, sems[0], sems[1], shard_shapes, kinds, False):
            cp.wait_recv()

    return _split_wait(wait, arrs, list(sem_pair), after, name=name)


def _gw_forward(arrs, shard_shapes, kinds, *, name):
    n = len(arrs)

    def body(*refs):
        dst = refs[n:2 * n]
        send_sems, recv_sems = refs[2 * n:]
        x, y, c = _coords()
        sends = []
        for p in range(n):
            for j, (_, _, oc) in enumerate(_other_chips(x, y)):
                win = _window(dst[p], shard_shapes[p], kinds[p], oc, c)
                cp = pltpu.make_async_remote_copy(src_ref=win, dst_ref=win, send_sem=send_sems.at[3 * p + j],
                                                  recv_sem=recv_sems.at[3 * p + j], device_id=(x, y, 1 - c),
                                                  device_id_type=MESH)
                cp.start()
                sends.append(cp)
        for p in range(n):
            for j, (_, _, oc) in enumerate(_other_chips(x, y)):
                win = _window(dst[p], shard_shapes[p], kinds[p], oc, 1 - c)
                pltpu.make_async_remote_copy(src_ref=win, dst_ref=win, send_sem=send_sems.at[3 * p + j],
                                             recv_sem=recv_sems.at[3 * p + j], device_id=(x, y, 1 - c),
                                             device_id_type=MESH).wait_recv()
        for cp in sends:
            cp.wait_send()

    outs = [jax.ShapeDtypeStruct(a.shape, a.dtype) for a in arrs]
    return _hbm_call(body, arrs, outs, (3 * n, 3 * n), name=name, alias=True)


def _rs_chips_copies(parts, lands, send_sems, recv_sems):
    x, y, c = _coords()
    chip = 2 * x + y
    out = []
    for p in range(len(parts)):
        for j, (ox, oy, oc) in enumerate(_other_chips(x, y)):
            out.append(pltpu.make_async_remote_copy(
                src_ref=parts[p].at[oc], dst_ref=lands[p].at[chip], send_sem=send_sems.at[3 * p + j],
                recv_sem=recv_sems.at[3 * p + j], device_id=(ox, oy, c), device_id_type=MESH))
    return out


def _rs_chips_start(parts, *, name):
    n = len(parts)

    def start(refs, sems):
        for cp in _rs_chips_copies(refs[:n], refs[n:], sems[0], sems[1]):
            cp.start()

    lands = [lax.empty(t.shape, t.dtype) for t in parts]
    sems, thru, token = _split_start(start, list(parts) + lands, [3 * n, 3 * n], name=name)
    return (sems[0], sems[1]), thru[:n], thru[n:], token


def _rs_chips_wait(groups, after, *, name):
    sizes = [len(g[1]) for g in groups]
    arrs = [a for g in groups for a in list(g[1]) + list(g[2])]
    sems = [s for g in groups for s in g[0]]

    def wait(refs, sem_refs):
        o = 0
        for k, n in enumerate(sizes):
            for cp in _rs_chips_copies(refs[o:o + n], refs[o + n:o + 2 * n], sem_refs[2 * k], sem_refs[2 * k + 1]):
                cp.wait()
            o += 2 * n

    outs = _split_wait(wait, arrs, sems, after, name=name)
    res, o = [], 0
    for n in sizes:
        res.append((list(outs[o:o + n]), list(outs[o + n:o + 2 * n])))
        o += 2 * n
    return res


def _rs_cores(grads, *, name):
    n = len(grads)
    outs = [jax.ShapeDtypeStruct((g.shape[0], g.shape[1] // 2, g.shape[2]), g.dtype) for g in grads]

    def body(*refs):
        src, dst = refs[:n], refs[n:2 * n]
        send_sems, recv_sems = refs[2 * n:]
        x, y, c = _coords()
        copies = []
        for p in range(n):
            nsh, r, _ = grads[p].shape
            h = r // 2
            for j in range(nsh):
                cp = pltpu.make_async_remote_copy(
                    src_ref=src[p].at[j, pl.ds((1 - c) * h, h), :], dst_ref=dst[p].at[j],
                    send_sem=send_sems.at[nsh * p + j], recv_sem=recv_sems.at[nsh * p + j],
                    device_id=(x, y, 1 - c), device_id_type=MESH)
                cp.start()
                copies.append(cp)
        for cp in copies:
            cp.wait()

    tot = sum(g.shape[0] for g in grads)
    return _hbm_call(body, grads, outs, (tot, tot), name=name)


def _sum_own_half(full, recv, ci, out_dtype, *, name):
    nsh, h, cols = recv.shape
    tm = _tile_rows(h, 256)
    nblk = h // tm

    def body(c_ref, f_ref, r_ref, o_ref):
        o_ref[...] = (f_ref[...] + r_ref[...]).astype(o_ref.dtype)

    return pl.pallas_call(
        body, name=name,
        grid_spec=pltpu.PrefetchScalarGridSpec(
            num_scalar_prefetch=1, grid=(nsh, nblk),
            in_specs=[pl.BlockSpec((1, tm, cols), lambda j, i, c_ref: (j, c_ref[0] * nblk + i, 0)),
                      pl.BlockSpec((1, tm, cols), lambda j, i, c_ref: (j, i, 0))],
            out_specs=pl.BlockSpec((1, tm, cols), lambda j, i, c_ref: (j, i, 0))),
        out_shape=jax.ShapeDtypeStruct((nsh, h, cols), out_dtype), compiler_params=_params("parallel", "parallel"),
    )(ci.reshape(1).astype(jnp.int32), full, recv)


def _rs_chips(parts, *, name):
    n = len(parts)
    outs = [jax.ShapeDtypeStruct(t.shape, t.dtype) for t in parts]

    def body(*refs):
        src, dst = refs[:n], refs[n:2 * n]
        send_sems, recv_sems = refs[2 * n:]
        x, y, c = _coords()
        chip = 2 * x + y
        copies = []
        for p in range(n):
            for j, (ox, oy, oc) in enumerate(_other_chips(x, y)):
                cp = pltpu.make_async_remote_copy(
                    src_ref=src[p].at[oc], dst_ref=dst[p].at[chip], send_sem=send_sems.at[3 * p + j],
                    recv_sem=recv_sems.at[3 * p + j], device_id=(ox, oy, c), device_id_type=MESH)
                cp.start()
                copies.append(cp)
        for cp in copies:
            cp.wait()

    return _hbm_call(body, parts, outs, (3 * n, 3 * n), name=name)


def _sum_chips(recv, own, chip, ci, *, name, nlead=1, lead=0, prev=None, spread=False):
    nsh, h, cols = recv.shape
    tm = _tile_rows(h, 256)
    nblk = h // tm
    rows_out = 2 * h * (nsh if spread else 1)

    def body(s_ref, r_ref, o_ref, *rest):
        out_ref = rest[-1]
        t = None
        for s in range(nsh):
            v = jnp.where(s_ref[0] == s, o_ref[s], r_ref[s]).astype(F32)
            t = v if t is None else t + v
        out_ref[...] = t

    def out_idx(i, s):
        return (lead, (s[0] * 2 * nblk if spread else 0) + s[1] * nblk + i, 0)

    blk = pl.BlockSpec((nsh, tm, cols), lambda i, s: (0, i, 0))
    ins = [recv, own] + ([prev] if prev is not None else [])
    return pl.pallas_call(
        body, name=name,
        grid_spec=pltpu.PrefetchScalarGridSpec(
            num_scalar_prefetch=1, grid=(nblk,),
            in_specs=[blk, blk] + ([pl.BlockSpec(memory_space=pl.ANY)] if prev is not None else []),
            out_specs=pl.BlockSpec((None, tm, cols), out_idx)),
        out_shape=jax.ShapeDtypeStruct((nlead, rows_out, cols), F32),
        input_output_aliases={3: 0} if prev is not None else {},
        compiler_params=_params("arbitrary"),
    )(jnp.stack([chip, ci]).astype(jnp.int32), *ins)


def _rs_gather(arrs, halves, spread, *, name, nchunk=4):
    n = len(arrs)
    per = [a.shape[0] * nchunk for a in arrs]
    offs = [sum(per[:p]) for p in range(n)]

    def body(*refs):
        dst = refs[n:2 * n]
        send_sems, recv_sems = refs[2 * n:]
        x, y, c = _coords()
        chip = 2 * x + y
        copies = []
        for p in range(n):
            h = halves[p]
            q = h // nchunk
            base = chip * 2 * h if spread[p] else 0
            for l in range(arrs[p].shape[0]):
                for k in range(nchunk):
                    win = dst[p].at[l, pl.ds(base + c * h + k * q, q), :]
                    sem = offs[p] + l * nchunk + k
                    cp = pltpu.make_async_remote_copy(src_ref=win, dst_ref=win, send_sem=send_sems.at[sem],
                                                      recv_sem=recv_sems.at[sem], device_id=(x, y, 1 - c),
                                                      device_id_type=MESH)
                    cp.start()
                    copies.append(cp)
        for cp in copies:
            cp.wait_send()
        for p in range(n):
            h = halves[p]
            q = h // nchunk
            base = chip * 2 * h if spread[p] else 0
            for l in range(arrs[p].shape[0]):
                for k in range(nchunk):
                    win = dst[p].at[l, pl.ds(base + (1 - c) * h + k * q, q), :]
                    sem = offs[p] + l * nchunk + k
                    pltpu.make_async_remote_copy(src_ref=win, dst_ref=win, send_sem=send_sems.at[sem],
                                                 recv_sem=recv_sems.at[sem], device_id=(x, y, 1 - c),
                                                 device_id_type=MESH).wait_recv()

    outs = [jax.ShapeDtypeStruct(a.shape, a.dtype) for a in arrs]
    return _hbm_call(body, arrs, outs, (sum(per), sum(per)), name=name, alias=True)


def _adamw(w, g, m, v, *, name):
    nl, R, C = w.shape
    tm = _tile_rows(R, 256)

    def body(w_ref, g_ref, m_ref, v_ref, d_ref, nm_ref, nv_ref):
        gg = g_ref[...]
        nm = B1 * m_ref[...] + (1.0 - B1) * gg
        nv = B2 * v_ref[...] + (1.0 - B2) * (gg * gg)
        m_hat = nm / (1.0 - B1 ** ASTEP)
        v_hat = nv / (1.0 - B2 ** ASTEP)
        d_ref[...] = -LR * (m_hat / (jnp.sqrt(v_hat) + AEPS) + WD * w_ref[...])
        nm_ref[...] = nm
        nv_ref[...] = nv

    blk = pl.BlockSpec((None, tm, C), lambda l, i: (l, i, 0))
    return pl.pallas_call(
        body, name=name, grid=(nl, R // tm), in_specs=[blk] * 4, out_specs=[blk] * 3,
        out_shape=[jax.ShapeDtypeStruct((nl, R, C), F32)] * 3, compiler_params=_params("parallel", "parallel"),
    )(w, g, m, v)


def _mod_cols(c_all, w_ada, b_cols, *, name):
    nl, D, cols = w_ada.shape
    B = c_all.shape[0]

    def body(c_ref, w_ref, b_ref, o_ref):
        cc = c_ref[...]
        cs = (cc * _sigmoid(cc)).astype(BF16)
        o_ref[0] = jnp.dot(cs, w_ref[0].astype(BF16), preferred_element_type=F32) + b_ref[0]

    return pl.pallas_call(
        body, name=name, grid=(nl,),
        in_specs=[_whole(c_all.shape), pl.BlockSpec((1, D, cols), lambda i: (i, 0, 0)), pl.BlockSpec((1, 1, cols), lambda i: (i, 0, 0))],
        out_specs=pl.BlockSpec((1, B, cols), lambda i: (i, 0, 0)),
        out_shape=jax.ShapeDtypeStruct((nl, B, cols), F32), compiler_params=_params("arbitrary"),
    )(c_all, w_ada, b_cols)


def _ada_grad(c_all, dmod_cols, *, name):
    nl, B, cols = dmod_cols.shape
    D = c_all.shape[1]

    def body(c_ref, d_ref, o_ref):
        cc = c_ref[...]
        cs = (cc * _sigmoid(cc)).astype(BF16)
        o_ref[0] = lax.dot_general(cs, d_ref[0].astype(BF16), TN, preferred_element_type=F32)

    return pl.pallas_call(
        body, name=name, grid=(nl,),
        in_specs=[_whole(c_all.shape), pl.BlockSpec((1, B, cols), lambda i: (i, 0, 0))],
        out_specs=pl.BlockSpec((1, D, cols), lambda i: (i, 0, 0)),
        out_shape=jax.ShapeDtypeStruct((nl, D, cols), F32), compiler_params=_params("arbitrary"),
    )(c_all, dmod_cols)


def _s5_disc(a_re, a_im, log_dt, b_re, b_im):
    dt = jnp.exp(log_dt)[:, None]
    mag = jnp.exp(a_re * dt)
    ph = a_im * dt
    lb_re = mag * jnp.cos(ph)
    lb_im = mag * jnp.sin(ph)
    den = a_re * a_re + a_im * a_im
    nr = lb_re - 1.0
    ni = lb_im
    f_re = (nr * a_re + ni * a_im) / den
    f_im = (ni * a_re - nr * a_im) / den
    bb_re = f_re[..., None] * b_re - f_im[..., None] * b_im
    bb_im = f_re[..., None] * b_im + f_im[..., None] * b_re
    return lb_re, lb_im, bb_re, bb_im


def _to_segments(t):
    L, D = t.shape
    return t.reshape(NSEG, L // NSEG, D).transpose(1, 0, 2).reshape(L, D)


def _from_segments(t):
    L, D = t.shape
    return t.reshape(L // NSEG, NSEG, D).transpose(1, 0, 2).reshape(L, D)


def _mlp_fwd(h2, w1, w2, tag):
    a = _matmul(h2, w1, name=f"ff1_{tag}", out_dtypes=(BF16,), epi=lambda acc: (jnp.maximum(acc, 0.0),))
    f = _matmul(a, w2, name=f"ff2_{tag}", a_fn=jnp.square)
    return a, f


def _mlp_bwd(df, h2, a, w1, w2, tag):
    da = _matmul(df, w2, tb=True, name=f"ff2_dx_{tag}", out_dtypes=(BF16,), epi_ins=(a,),
                 epi=lambda acc, at: (acc * (2.0 * at.astype(F32)),))
    dw2 = _matmul(a, df, ta=True, name=f"ff2_dw_{tag}", a_fn=jnp.square)
    dh2 = _matmul(da, w1, tb=True, name=f"ff1_dx_{tag}")
    dw1 = _matmul(h2, da, ta=True, name=f"ff1_dw_{tag}", col_shards=NCH)
    return dh2, dw1, dw2


def kernel(x, c, w_ada, b_ada, norm_mix, norm_mlp, s5_a_re, s5_a_im, s5_log_dt, s5_b_re, s5_b_im, s5_c_re, s5_c_im, s5_d, s5_w_glu, gla_w_in, gla_w_gate2, gla_b_gate, gla_g_norm, gla_w_out, w_ff1, w_ff2, norm_final, loss_target, m_w_ada, m_b_ada, m_norm_mix, m_norm_mlp, m_s5_a_re, m_s5_a_im, m_s5_log_dt, m_s5_b_re, m_s5_b_im, m_s5_c_re, m_s5_c_im, m_s5_d, m_s5_w_glu, m_gla_w_in, m_gla_w_gate2, m_gla_b_gate, m_gla_g_norm, m_gla_w_out, m_w_ff1, m_w_ff2, m_norm_final, v_w_ada, v_b_ada, v_norm_mix, v_norm_mlp, v_s5_a_re, v_s5_a_im, v_s5_log_dt, v_s5_b_re, v_s5_b_im, v_s5_c_re, v_s5_c_im, v_s5_d, v_s5_w_glu, v_gla_w_in, v_gla_w_gate2, v_gla_b_gate, v_gla_g_norm, v_gla_w_out, v_w_ff1, v_w_ff2, v_norm_final):
    args = dict(locals())
    L, D = x.shape[1], x.shape[2]
    QK = D // 2
    xi, yi, ci = _coords()
    chip = 2 * xi + yi
    dev = 2 * chip + ci

    c_all = _exchange(c.reshape(8, D // 8), MASK_ALL, name="gather_c").reshape(8, D)
    acols = w_ada.shape[2]
    b_cols = lax.dynamic_slice_in_dim(b_ada, chip * acols, acols, axis=1)[:, None, :]
    mod_cols = _mod_cols(c_all, w_ada, b_cols, name="ada_mod")
    mod_all = _exchange(mod_cols.reshape(16, acols), MASK_CHIPS, name="gather_mod")
    mod_all = mod_all.reshape(NCH, 2, 8, acols).transpose(1, 2, 0, 3).reshape(2, 8, NCH * acols)
    mod = lax.dynamic_index_in_dim(mod_all, dev, axis=1, keepdims=False).reshape(2, 6, 1, D)

    big = [("s5_w_glu", s5_w_glu, 0, "col"), ("gla_w_in", gla_w_in, 0, "slot"), ("gla_w_out", gla_w_out, 0, "row"),
           ("w_ff1_0", w_ff1, 0, "col"), ("w_ff1_1", w_ff1, 1, "col"), ("w_ff2_0", w_ff2, 0, "row"), ("w_ff2_1", w_ff2, 1, "row")]
    own16 = [_cast_into(t, lead, kind, chip, name=f"cast_{nm}") for nm, t, lead, kind in big]
    wshapes, wkinds = [b[1].shape[-2:] for b in big], [b[3] for b in big]
    wgroups = [[0, 3, 5], [1, 2, 4, 6]]
    wsems, wthru, wtoken = _gw_start(own16, wshapes, wkinds, wgroups, name="gather_w_start")
    W = {}

    def finish_weights(g, after):
        idx = wgroups[g]
        shp, knd = [wshapes[p] for p in idx], [wkinds[p] for p in idx]
        got = _gw_wait([wthru[p] for p in idx], shp, knd, wsems[g], after, name=f"gather_w_wait{g}")
        for p, w in zip(idx, _gw_forward(got, shp, knd, name=f"gather_w_cores{g}")):
            W[big[p][0]] = w

    cat = jnp.concatenate([gla_w_gate2[0].reshape(1, -1), gla_b_gate, gla_g_norm], axis=1)
    cat_all = _exchange(jnp.tile(cat, (8, 1)), MASK_CHIPS, name="gather_gla_small")[:, 0, :]
    qk4 = QK // NCH
    wg2 = cat_all[:, :GATE_RANK * qk4].reshape(NCH, GATE_RANK, qk4).transpose(1, 0, 2).reshape(GATE_RANK, QK)
    bg = cat_all[:, GATE_RANK * qk4:(GATE_RANK + 1) * qk4].reshape(1, QK)
    gn = cat_all[:, (GATE_RANK + 1) * qk4:].reshape(1, D)
    wg2p = jnp.concatenate([wg2, jnp.zeros((128 - GATE_RANK, QK), F32)], axis=0).astype(BF16)

    lb_re, lb_im, bb_re, bb_im = _s5_disc(s5_a_re[0], s5_a_im[0], s5_log_dt[0], s5_b_re[0], s5_b_im[0])
    tb = _s5_tables(lb_re, lb_im, bb_re, bb_im, s5_c_re[0], s5_c_im[0], L // NSEG)
    zero_carry = jnp.zeros_like(tb["lam"]) + wtoken[0, 0]

    def vec(t):
        return t.reshape(1, -1)

    xp = _to_segments(x[0])
    m0, m1 = mod[0], mod[1]
    (u0,) = _rows(lambda t, g, sc, sh: (_norm_mod(t, g, sc, sh),), [xp], [vec(norm_mix[0]), m0[1], m0[0]],
                  [(D, F32)], [], name="pre_mix0")
    (ends,) = _s5_fwd(u0, zero_carry, tb, s5_d, emit=False, name="s5_fwd_ends")
    carry0 = _s5_fix(ends, tb["lamT"], reverse=False, name="s5_fix_fwd")
    y0, z0, ck0 = _s5_fwd(u0, carry0, tb, s5_d, emit=True, name="s5_fwd")
    finish_weights(0, z0)
    vg0 = _matmul(z0, W["s5_w_glu"], name="glu")

    def res_glu_pre(xt, vgt, gt, g, sc, sh):
        xn = xt + gt * (vgt[:, :D] * _sigmoid(vgt[:, D:]))
        return xn, _norm_mod(xn, g, sc, sh)

    x2_0, h2_0 = _rows(res_glu_pre, [xp, vg0], [m0[2], vec(norm_mlp[0]), m0[4], m0[3]], [(D, F32), (D, BF16)], [],
                       name="res_mix0")
    a_0, f0 = _mlp_fwd(h2_0, W["w_ff1_0"], W["w_ff2_0"], "0")

    def res_pre(xt, bt, gt, g, sc, sh):
        xn = xt + gt * bt
        return xn, _norm_mod(xn, g, sc, sh)

    x3p, h1p = _rows(res_pre, [x2_0, f0], [m0[5], vec(norm_mix[1]), m1[1], m1[0]], [(D, F32), (D, BF16)], [],
                     name="res_mlp0")
    x3 = _from_segments(x3p)
    h1 = _from_segments(h1p)
    finish_weights(1, f0)
    w_in = W["gla_w_in"].transpose(1, 0, 2).reshape(D, -1)
    w_in_r = jnp.concatenate([w_in[:, :4 * QK], w_in[:, 4 * QK + GATE_RANK:], w_in[:, 4 * QK:4 * QK + GATE_RANK],
                              jnp.zeros((D, 128 - GATE_RANK), BF16)], axis=1)
    proj = _matmul(h1, w_in_r, name="gla_in", tn=640)
    og, states = _gla_fwd(proj, wg2p, bg, gn, name="gla_fwd")
    ymix = _matmul(og, W["gla_w_out"], name="gla_out")
    x2_1, h2_1 = _rows(res_pre, [x3, ymix], [m1[2], vec(norm_mlp[1]), m1[4], m1[3]], [(D, F32), (D, BF16)], [],
                       name="res_mix1")
    a_1, f1 = _mlp_fwd(h2_1, W["w_ff1_1"], W["w_ff2_1"], "1")

    def final(xt, ft, tgt, gt, g):
        xn = xt + gt * ft
        rs = lax.rsqrt(jnp.mean(xn * xn, axis=-1, keepdims=True) + EPS)
        xh = xn * rs
        e = xh * g - tgt
        dout = e * (1.0 / D)
        dxh = dout * g
        dx = rs * (dxh - xh * jnp.mean(dxh * xh, axis=-1, keepdims=True))
        lsum = 0.5 * jnp.sum(jnp.sum(e * e, axis=-1, keepdims=True), axis=0, keepdims=True) * (1.0 / D)
        return dx, jnp.broadcast_to(lsum, (1, 128)), _rsum(dout * xh)

    dx, loss_part, d_norm_final = _rows(final, [x2_1, f1, loss_target[0]], [m1[5], vec(norm_final)], [(D, F32)],
                                        [(1, 128), (1, D)], name="loss_head")
    loss = lax.psum(loss_part[0, 0], ("x", "y", "c"))

    def gate_bwd(dxt, bt, gt):
        return dxt * gt, _rsum(dxt * bt)

    def norm_bwd(xt, dht, drt, g, sc):
        dxn, dsh, dsc, dg = _norm_mod_bwd(xt, dht, g, sc)
        return drt + dxn, dsh, dsc, dg

    def norm_gate_bwd(xt, dht, drt, bt, g, sc, gt):
        dxn, dsh, dsc, dg = _norm_mod_bwd(xt, dht, g, sc)
        dxt = drt + dxn
        return dxt, dxt * gt, dsh, dsc, dg, _rsum(dxt * bt)

    vD = [(1, D)]
    df1, dgt2_1 = _rows(gate_bwd, [dx, f1], [m1[5]], [(D, BF16)], vD, name="gate_mlp1")
    dh2_1, dw_ff1_1, dw_ff2_1 = _mlp_bwd(df1, h2_1, a_1, W["w_ff1_1"], W["w_ff2_1"], "1")
    dx, dmix1, dsh2_1, dsc2_1, dg_mlp1, dgt1_1 = _rows(
        norm_gate_bwd, [x2_1, dh2_1, dx, ymix], [vec(norm_mlp[1]), m1[4], m1[2]], [(D, F32), (D, BF16)], vD * 4,
        name="norm_mlp1_bwd")
    dog = _matmul(dmix1, W["gla_w_out"], tb=True, name="gla_out_dx")
    dw_out = _matmul(og, dmix1, ta=True, name="gla_out_dw")
    dproj, dwg2p, dbg, dgn = _gla_bwd(proj, dog, states, wg2p, bg, gn, name="gla_bwd")
    dh1 = _matmul(dproj, w_in_r, tb=True, name="gla_in_dx", tk=640)
    dw_in_r = _matmul(h1, dproj, ta=True, name="gla_in_dw", tn=640)
    dx, dsh1_1, dsc1_1, dg_mix1 = _rows(norm_bwd, [x3, dh1, dx], [vec(norm_mix[1]), m1[1]], [(D, F32)], vD * 3,
                                        name="norm_mix1_bwd")
    tags = [b[0] for b in big] + ["small"]
    rs_groups = []

    def rs_begin(idx, srcs, gname):
        r1 = _rs_cores(srcs, name=f"rs_cores_{gname}")
        s1 = [_sum_own_half(g, r, ci, F32 if tags[k] == "small" else BF16, name=f"rs_sum_cores_{tags[k]}")
              for g, r, k in zip(srcs, r1, idx)]
        pair, parts, lands, token = _rs_chips_start(s1, name=f"rs_chips_start_{gname}")
        rs_groups.append((idx, pair, parts, lands))
        return token

    dw_in = jnp.concatenate([dw_in_r[:, :4 * QK], dw_in_r[:, 6 * QK:6 * QK + GATE_RANK], dw_in_r[:, 4 * QK:6 * QK]], axis=1)
    dw_in = dw_in.reshape(D, NCH, -1).transpose(1, 0, 2)
    tok1 = rs_begin([1, 2, 4, 6], [dw_in, dw_out.reshape(NCH, -1, D), dw_ff1_1, dw_ff2_1.reshape(NCH, -1, D)], "l1")

    dxp = _to_segments(dx)
    df0, dgt2_0 = _rows(gate_bwd, [dxp, f0], [m0[5] + tok1[0, 0]], [(D, BF16)], vD, name="gate_mlp0")
    dh2_0, dw_ff1_0, dw_ff2_0 = _mlp_bwd(df0, h2_0, a_0, W["w_ff1_0"], W["w_ff2_0"], "0")
    tok2 = rs_begin([3, 5], [dw_ff1_0, dw_ff2_0.reshape(NCH, -1, D)], "l0")

    def norm_glu_bwd(xt, dht, drt, vgt, g, sc, gt):
        dxn, dsh, dsc, dg = _norm_mod_bwd(xt, dht, g, sc)
        dxt = drt + dxn
        val, sg = vgt[:, :D], _sigmoid(vgt[:, D:])
        dbr = dxt * gt
        dvg = jnp.concatenate([dbr * sg, dbr * val * sg * (1.0 - sg)], axis=1)
        return dxt, dvg, dsh, dsc, dg, _rsum(dxt * val * sg)

    dxp, dvg0, dsh2_0, dsc2_0, dg_mlp0, dgt1_0 = _rows(
        norm_glu_bwd, [x2_0, dh2_0, dxp, vg0], [vec(norm_mlp[0]), m0[4] + tok2[0, 0], m0[2]], [(D, F32), (2 * D, BF16)],
        vD * 4, name="norm_mlp0_bwd")
    dz0 = _matmul(dvg0, W["s5_w_glu"], tb=True, name="glu_dx")
    dw_glu = _matmul(z0, dvg0, ta=True, name="glu_dw", tn=512, col_shards=NCH)
    (gends,) = _s5_bwd(u0, y0, dz0, ck0, zero_carry, tb, s5_d, emit=False, name="s5_bwd_ends")
    gcarry0 = _s5_fix(gends, tb["lamT"], reverse=True, name="s5_fix_bwd")
    du0, db_acc, dc_acc, dl_acc, dd_s5 = _s5_bwd(u0, y0, dz0, ck0, gcarry0, tb, s5_d, emit=True, name="s5_bwd")
    dxp, dsh1_0, dsc1_0, dg_mix0 = _rows(norm_bwd, [xp, du0, dxp], [vec(norm_mix[0]), m0[1]], [(D, F32)], vD * 3,
                                         name="norm_mix0_bwd")
    grad_x = _from_segments(dxp)[None]

    dmod = jnp.concatenate([dsh1_0, dsc1_0, dgt1_0, dsh2_0, dsc2_0, dgt2_0,
                            dsh1_1, dsc1_1, dgt1_1, dsh2_1, dsc2_1, dgt2_1], axis=1)
    dbb_re, dbb_im = _s5_untable(db_acc)
    dc_re, dc_im_neg = _s5_untable(dc_acc)
    nbk = D // 128
    dl = dl_acc.reshape(nbk, NSEG, 2, GPB * S5_P).sum(axis=1)
    smalls = [dmod, dg_mix0, dg_mix1, dg_mlp0, dg_mlp1, d_norm_final, dd_s5, dbg, dgn,
              dwg2p[:GATE_RANK].reshape(1, -1), dbb_re.reshape(1, -1), dbb_im.reshape(1, -1),
              dc_re.reshape(1, -1), dc_im_neg.reshape(1, -1), dl.reshape(1, -1)]
    ssz = [t.shape[1] for t in smalls]
    stot = sum(ssz)
    spad = -(-stot // 8192) * 8192
    svec = jnp.concatenate(smalls + [jnp.zeros((1, spad - stot), F32)], axis=1).reshape(NCH, spad // (128 * NCH), 128)

    dmod_all = _exchange(dmod.reshape(12 * D // 128, 128), MASK_ALL, name="gather_dmod").reshape(8, 2, 6 * D)
    dmod_cols = lax.dynamic_slice_in_dim(dmod_all, chip * acols, acols, axis=2).transpose(1, 0, 2)
    g_w_ada = _ada_grad(c_all, dmod_cols, name="ada_grad")

    rs_begin([0, 7], [dw_glu, svec], "last")
    landed = _rs_chips_wait([(g[1], g[2], g[3]) for g in rs_groups], dxp, name="rs_chips_wait")
    s1, r2 = {}, {}
    for (idx, _, _, _), (parts, lands) in zip(rs_groups, landed):
        for k, part, land in zip(idx, parts, lands):
            s1[k], r2[k] = part, land

    def fin(k, **kw):
        return _sum_chips(r2[k], s1[k], chip, ci, name=f"rs_sum_chips_{tags[k]}", **kw)

    f_ff1 = fin(4, nlead=2, lead=1, prev=fin(3, nlead=2, lead=0))
    f_ff2 = fin(6, nlead=2, lead=1, prev=fin(5, nlead=2, lead=0))
    finals = [fin(0), fin(1), fin(2), f_ff1, f_ff2, fin(7, spread=True)]
    halves = [t.shape[1] for t in (s1[0], s1[1], s1[2], s1[3], s1[5], s1[7])]
    g_glu, g_in, g_out, g_w_ff1, g_w_ff2, s_own = _rs_gather(finals, halves, [False] * 5 + [True], name="rs_gather_cores")
    srows = spad // (128 * NCH)
    (s_sum,) = _gather_weights([s_own.reshape(NCH * srows, 128)], [(srows, 128)], ["row"], name="gather_small_grads")
    s_sum = s_sum.reshape(-1)
    so = [sum(ssz[:k]) for k in range(len(ssz))]
    sm = [s_sum[o:o + n] for o, n in zip(so, ssz)]
    (dmod_s, g_mix0, g_mix1, g_mlp0, g_mlp1, g_nf, g_d, g_bg, g_gn, g_wg2, g_bbre, g_bbim, g_cre, g_cimn, g_dl) = sm
    g_b_ada = dmod_s.reshape(2, 6 * D)

    G = D // S5_H
    _, disc_vjp = jax.vjp(_s5_disc, s5_a_re[0], s5_a_im[0], s5_log_dt[0], s5_b_re[0], s5_b_im[0])
    g_dl = g_dl.reshape(nbk, 2, GPB, S5_P)
    ct = (g_dl[:, 0].reshape(G, S5_P), g_dl[:, 1].reshape(G, S5_P),
          g_bbre.reshape(G, S5_H, S5_P).transpose(0, 2, 1), g_bbim.reshape(G, S5_H, S5_P).transpose(0, 2, 1))
    g_a_re, g_a_im, g_log_dt, g_b_re, g_b_im = disc_vjp(ct)
    g_c_re = g_cre.reshape(G, S5_H, S5_P)
    g_c_im = -g_cimn.reshape(G, S5_H, S5_P)
    g_wg2_s = lax.dynamic_slice_in_dim(g_wg2.reshape(GATE_RANK, QK), chip * qk4, qk4, axis=1)
    g_bg_s = lax.dynamic_slice_in_dim(g_bg.reshape(1, QK), chip * qk4, qk4, axis=1)
    g_gn_s = lax.dynamic_slice_in_dim(g_gn.reshape(1, D), chip * (D // NCH), D // NCH, axis=1)

    grads = dict(
        w_ada=g_w_ada, b_ada=g_b_ada, norm_mix=jnp.stack([g_mix0, g_mix1]), norm_mlp=jnp.stack([g_mlp0, g_mlp1]),
        s5_a_re=g_a_re[None], s5_a_im=g_a_im[None], s5_log_dt=g_log_dt[None], s5_b_re=g_b_re[None], s5_b_im=g_b_im[None],
        s5_c_re=g_c_re[None], s5_c_im=g_c_im[None], s5_d=g_d[None], s5_w_glu=g_glu,
        gla_w_in=g_in, gla_w_gate2=g_wg2_s[None], gla_b_gate=g_bg_s, gla_g_norm=g_gn_s,
        gla_w_out=g_out, w_ff1=g_w_ff1, w_ff2=g_w_ff2, norm_final=g_nf)

    names = list(grads)
    large = ("w_ada", "s5_w_glu", "gla_w_in", "gla_w_out", "w_ff1", "w_ff2")
    delta, new_m, new_v = {}, {}, {}
    for nm in large:
        delta[nm], new_m[nm], new_v[nm] = _adamw(args[nm], grads[nm], args["m_" + nm], args["v_" + nm], name=f"adamw_{nm}")
    small = [nm for nm in names if nm not in large]
    szs = [args[nm].size for nm in small]
    tot = sum(szs)
    pad = -(-tot // 32768) * 32768

    def pack(prefix, src):
        return jnp.concatenate([src[prefix + nm].reshape(-1) for nm in small] + [jnp.ones((pad - tot,), F32)]).reshape(1, -1, 128)

    d, m2, v2 = _adamw(pack("", args), pack("", grads), pack("m_", args), pack("v_", args), name="adamw_small")
    o = 0
    for nm, n in zip(small, szs):
        shp = args[nm].shape
        delta[nm], new_m[nm], new_v[nm] = (t.reshape(-1)[o:o + n].reshape(shp) for t in (d, m2, v2))
        o += n
    grads = {nm: grads[nm].reshape(args[nm].shape) for nm in names}
    return (loss, grad_x, *[grads[n] for n in names], *[delta[n] for n in names], *[new_m[n] for n in names],
            *[new_v[n] for n in names])
```

```python
import math

import jax
import jax.numpy as jnp
from jax import lax
from jax.experimental import pallas as pl
from jax.experimental.pallas import tpu as pltpu

F32 = jnp.float32
BF16 = jnp.bfloat16
MESH = pl.DeviceIdType.MESH

EPS = 1e-6
CHUNK = 64
GLA_NB = 4
S5_H = 16
S5_P = 64
GPB = 8
NSEG = 8
HEADS = 4
GATE_RANK = 16
GATE_TAU = 16.0
NCH = 4
LR, B1, B2, AEPS, WD, ASTEP = 0.001, 0.9, 0.999, 1e-08, 0.01, 10
VMEM_LIMIT = 56 << 20

MASK_CHIPS = ((1, 0, 0), (0, 1, 0), (1, 1, 0))
MASK_ALL = ((0, 0, 1), (0, 1, 0), (0, 1, 1), (1, 0, 0), (1, 0, 1), (1, 1, 0), (1, 1, 1))


def _params(*sem):
    return pltpu.CompilerParams(dimension_semantics=sem or None, vmem_limit_bytes=VMEM_LIMIT)


def _tile_rows(rows, cap=512):
    best = 8
    for t in range(8, cap + 1, 8):
        if rows % t == 0:
            best = t
    return best


def _whole(shape):
    return pl.BlockSpec(shape, lambda i, _n=len(shape): (0,) * _n)


def _matmul(a, b, *, name, ta=False, tb=False, tm=1024, tn=1024, tk=2048, out_dtypes=(F32,),
            a_fn=None, epi=None, epi_ins=(), col_shards=1):
    M, K = (a.shape[1], a.shape[0]) if ta else a.shape
    N = b.shape[0] if tb else b.shape[1]
    tm, tn, tk = min(tm, M), min(tn, N), min(tk, K)
    assert M % tm == 0 and N % tn == 0 and K % tk == 0, (name, M, N, K)
    nk = K // tk
    ne = len(epi_ins)
    dn = (((0 if ta else 1,), (1 if tb else 0,)), ((), ()))

    def body(a_ref, b_ref, *rest):
        e_refs, o_refs, acc = rest[:ne], rest[ne:-1], rest[-1]
        k = pl.program_id(2)

        @pl.when(k == 0)
        def _():
            acc[...] = jnp.zeros_like(acc)

        at = a_ref[...]
        if a_fn is not None:
            at = a_fn(at)
        acc[...] += lax.dot_general(at.astype(BF16), b_ref[...].astype(BF16), dn, preferred_element_type=F32)

        @pl.when(k == nk - 1)
        def _():
            outs = (acc[...],) if epi is None else epi(acc[...], *[r[...] for r in e_refs])
            for r, o in zip(o_refs, outs):
                r[...] = o.astype(r.dtype)

    a_spec = pl.BlockSpec((tk, tm), lambda i, j, k: (k, i)) if ta else pl.BlockSpec((tm, tk), lambda i, j, k: (i, k))
    b_spec = pl.BlockSpec((tn, tk), lambda i, j, k: (j, k)) if tb else pl.BlockSpec((tk, tn), lambda i, j, k: (k, j))
    o_spec = pl.BlockSpec((tm, tn), lambda i, j, k: (i, j))
    if col_shards > 1:
        per = N // col_shards // tn
        assert ne == 0 and per * tn * col_shards == N
        w_spec = pl.BlockSpec((None, tm, tn), lambda i, j, k: (j // per, i, j % per))
        o_shape = (col_shards, M, N // col_shards)
    else:
        w_spec, o_shape = o_spec, (M, N)
    outs = pl.pallas_call(
        body, name=name, grid=(M // tm, N // tn, nk),
        in_specs=[a_spec, b_spec] + [o_spec] * ne,
        out_specs=[w_spec] * len(out_dtypes),
        out_shape=[jax.ShapeDtypeStruct(o_shape, d) for d in out_dtypes],
        scratch_shapes=[pltpu.VMEM((tm, tn), F32)],
        compiler_params=_params("parallel", "parallel", "arbitrary"),
    )(a, b, *epi_ins)
    return outs[0] if len(outs) == 1 else outs


def _rows(fn, rows_in, vecs_in, rows_out, acc_out, *, name, tm=256):
    L = rows_in[0].shape[0]
    tm = min(tm, L)
    assert L % tm == 0
    nr, nv, no, na = len(rows_in), len(vecs_in), len(rows_out), len(acc_out)

    def body(*refs):
        rin, vin = refs[:nr], refs[nr:nr + nv]
        rout, aout = refs[nr + nv:nr + nv + no], refs[nr + nv + no:]
        outs = fn(*[r[...] for r in rin], *[v[...] for v in vin])
        for r, o in zip(rout, outs[:no]):
            r[...] = o.astype(r.dtype)
        if na:
            @pl.when(pl.program_id(0) == 0)
            def _():
                for r in aout:
                    r[...] = jnp.zeros_like(r)

            for r, o in zip(aout, outs[no:]):
                r[...] += o

    outs = pl.pallas_call(
        body, name=name, grid=(L // tm,),
        in_specs=[pl.BlockSpec((tm, r.shape[1]), lambda i: (i, 0)) for r in rows_in] + [_whole(v.shape) for v in vecs_in],
        out_specs=[pl.BlockSpec((tm, c), lambda i: (i, 0)) for c, _ in rows_out] + [_whole(s) for s in acc_out],
        out_shape=[jax.ShapeDtypeStruct((L, c), d) for c, d in rows_out] + [jax.ShapeDtypeStruct(s, F32) for s in acc_out],
        compiler_params=_params("arbitrary"),
    )(*rows_in, *vecs_in)
    return outs


def _rsum(t):
    return jnp.sum(t, axis=0, keepdims=True)


def _norm_mod(x, g, sc, sh):
    rs = lax.rsqrt(jnp.mean(x * x, axis=-1, keepdims=True) + EPS)
    return x * rs * g * (1.0 + sc) + sh


def _norm_mod_bwd(x, dh, g, sc):
    rs = lax.rsqrt(jnp.mean(x * x, axis=-1, keepdims=True) + EPS)
    xh = x * rs
    dn = dh * (1.0 + sc)
    dxh = dn * g
    dx = rs * (dxh - xh * jnp.mean(dxh * xh, axis=-1, keepdims=True))
    return dx, _rsum(dh), _rsum(dh * xh * g), _rsum(dn * xh)


def _sigmoid(x):
    return jax.nn.sigmoid(x)


def _gelu(y):
    return jax.nn.gelu(y, approximate=True)


def _gelu_grad(y):
    c = math.sqrt(2.0 / math.pi)
    t = jnp.tanh(c * (y + 0.044715 * y * y * y))
    return 0.5 * (1.0 + t) + 0.5 * y * (1.0 - t * t) * c * (1.0 + 3.0 * 0.044715 * y * y)


def _s5_tables(lb_re, lb_im, bb_re, bb_im, c_re, c_im, seg_len):
    G = lb_re.shape[0]
    nb = G // GPB
    eye = jnp.eye(GPB, dtype=F32)

    def bdiag(t):
        a, b = t.shape[1:]
        t = t.reshape(nb, GPB, a, b)
        return (t[:, :, :, None, :] * eye[None, :, None, :, None]).reshape(nb, GPB * a, GPB * b)

    bbd = jnp.concatenate([bdiag(bb_re.transpose(0, 2, 1)), bdiag(bb_im.transpose(0, 2, 1))], axis=2)
    cbd = jnp.concatenate([bdiag(c_re.transpose(0, 2, 1)), -bdiag(c_im.transpose(0, 2, 1))], axis=1)

    def lanes(re, im):
        t = jnp.concatenate([re.reshape(nb, GPB * S5_P), im.reshape(nb, GPB * S5_P)], axis=1)
        return jnp.repeat(t, NSEG, axis=0)

    tr, ti = lb_re, lb_im
    for _ in range(int(math.log2(seg_len))):
        tr, ti = tr * tr - ti * ti, 2.0 * tr * ti
    return dict(bbd=bbd.astype(BF16), bbdT=bbd.transpose(0, 2, 1).astype(BF16), cbd=cbd.astype(BF16),
                cbdT=cbd.transpose(0, 2, 1).astype(BF16), lam=lanes(lb_re, lb_im), lamT=lanes(tr, ti))


def _s5_untable(acc):
    nb = acc.shape[0]
    t = acc.reshape(nb, GPB, S5_H, 2, GPB, S5_P)
    d = jnp.diagonal(t, axis1=1, axis2=4)
    d = d.transpose(0, 4, 2, 1, 3).reshape(nb * GPB, 2, S5_H, S5_P)
    return d[:, 0], d[:, 1]


def _s5_fix(ends, lamT, *, reverse, name):
    nrow = ends.shape[0]
    half = ends.shape[1] // 2

    def body(e_ref, t_ref, o_ref):
        for gb in range(nrow // NSEG):
            r0 = gb * NSEG
            tr, ti = t_ref[r0:r0 + 1, :half], t_ref[r0:r0 + 1, half:]
            cr = jnp.zeros((1, half), F32)
            ci = jnp.zeros((1, half), F32)
            order = range(NSEG - 1, -1, -1) if reverse else range(NSEG)
            for n, s in enumerate(order):
                if n > 0:
                    p = s + 1 if reverse else s - 1
                    er, ei = e_ref[r0 + p:r0 + p + 1, :half], e_ref[r0 + p:r0 + p + 1, half:]
                    if reverse:
                        cr, ci = tr * cr + ti * ci + er, tr * ci - ti * cr + ei
                    else:
                        cr, ci = tr * cr - ti * ci + er, tr * ci + ti * cr + ei
                o_ref[r0 + s:r0 + s + 1, :half] = cr
                o_ref[r0 + s:r0 + s + 1, half:] = ci

    return pl.pallas_call(body, name=name, out_shape=jax.ShapeDtypeStruct(ends.shape, F32),
                          compiler_params=_params())(ends, lamT)


def _s5_fwd(up, carry_in, tb, dvec, *, emit, name, R=256):
    L, D = up.shape
    R = min(R, L)
    nb, ta, ngb = L // R, R // NSEG, D // 128
    SW = GPB * S5_P
    crows = ngb * NSEG

    def body(u_ref, cin_ref, lam_ref, b_ref, c_ref, d_ref, *rest):
        if emit:
            y_ref, z_ref, ck_ref, carry, xbuf = rest
        else:
            cout_ref, carry, xbuf = rest
        i = pl.program_id(0)

        @pl.when(i == 0)
        def _():
            carry[...] = cin_ref[...]

        if emit:
            ck_ref[0] = carry[...]
        for gb in range(ngb):
            cols = slice(gb * 128, (gb + 1) * 128)
            rws = slice(gb * NSEG, (gb + 1) * NSEG)
            ug = u_ref[:, cols]
            xbuf[...] = jnp.dot(ug.astype(BF16), b_ref[gb], preferred_element_type=F32)
            lr, li = lam_ref[rws, :SW], lam_ref[rws, SW:]

            def step(a, c, lr=lr, li=li):
                cr, ci = c
                o = pl.multiple_of(a * NSEG, NSEG)
                nr = lr * cr - li * ci + xbuf[pl.ds(o, NSEG), :SW]
                ni = lr * ci + li * cr + xbuf[pl.ds(o, NSEG), SW:]
                if emit:
                    xbuf[pl.ds(o, NSEG), :SW] = nr
                    xbuf[pl.ds(o, NSEG), SW:] = ni
                return nr, ni

            cr, ci = lax.fori_loop(0, ta, step, (carry[rws, :SW], carry[rws, SW:]), unroll=2)
            carry[rws, :SW] = cr
            carry[rws, SW:] = ci
            if emit:
                y = jnp.dot(xbuf[...].astype(BF16), c_ref[gb], preferred_element_type=F32) + d_ref[:, cols] * ug
                y_ref[:, cols] = y
                z_ref[:, cols] = _gelu(y).astype(BF16)
        if not emit:
            cout_ref[...] = carry[...]

    rowblk = pl.BlockSpec((R, D), lambda i: (i, 0))
    if emit:
        out_shape = [jax.ShapeDtypeStruct((L, D), F32), jax.ShapeDtypeStruct((L, D), BF16),
                     jax.ShapeDtypeStruct((nb, crows, 2 * SW), F32)]
        out_specs = [rowblk, rowblk, pl.BlockSpec((1, crows, 2 * SW), lambda i: (i, 0, 0))]
    else:
        out_shape = [jax.ShapeDtypeStruct((crows, 2 * SW), F32)]
        out_specs = [_whole((crows, 2 * SW))]
    return pl.pallas_call(
        body, name=name, grid=(nb,),
        in_specs=[rowblk, _whole(carry_in.shape), _whole(tb["lam"].shape), _whole(tb["bbd"].shape),
                  _whole(tb["cbd"].shape), _whole(dvec.shape)],
        out_specs=out_specs, out_shape=out_shape,
        scratch_shapes=[pltpu.VMEM((crows, 2 * SW), F32), pltpu.VMEM((R, 2 * SW), F32)],
        compiler_params=_params("arbitrary"),
    )(up, carry_in, tb["lam"], tb["bbd"], tb["cbd"], dvec)


def _s5_bwd(up, y, dz, ck, gcarry_in, tb, dvec, *, emit, name, R=256):
    L, D = up.shape
    R = min(R, L)
    nb, ta, ngb = L // R, R // NSEG, D // 128
    SW = GPB * S5_P
    crows = ngb * NSEG

    def body(u_ref, y_ref, dz_ref, ck_ref, gin_ref, lam_ref, b_ref, bt_ref, ct_ref, d_ref, *rest):
        if emit:
            du_ref, db_ref, dc_ref, dl_ref, dd_ref, gcarry, xbuf, gbuf, dybuf = rest
        else:
            gout_ref, gcarry, gbuf, dybuf = rest
        i = pl.program_id(0)

        @pl.when(i == 0)
        def _():
            gcarry[...] = gin_ref[...]
            if emit:
                db_ref[...] = jnp.zeros_like(db_ref)
                dc_ref[...] = jnp.zeros_like(dc_ref)
                dl_ref[...] = jnp.zeros_like(dl_ref)
                dd_ref[...] = jnp.zeros_like(dd_ref)

        dybuf[...] = dz_ref[...] * _gelu_grad(y_ref[...])
        for gb in range(ngb):
            cols = slice(gb * 128, (gb + 1) * 128)
            rws = slice(gb * NSEG, (gb + 1) * NSEG)
            dyg = dybuf[:, cols]
            lr, li = lam_ref[rws, :SW], lam_ref[rws, SW:]
            gbuf[...] = jnp.dot(dyg.astype(BF16), ct_ref[gb], preferred_element_type=F32)
            if emit:
                ug = u_ref[:, cols]
                xbuf[0:NSEG, :] = ck_ref[0, rws, :]
                xbuf[NSEG:, :] = jnp.dot(ug.astype(BF16), b_ref[gb], preferred_element_type=F32)

                def fstep(a, c, lr=lr, li=li):
                    cr, ci = c
                    o = pl.multiple_of(a * NSEG + NSEG, NSEG)
                    nr = lr * cr - li * ci + xbuf[pl.ds(o, NSEG), :SW]
                    ni = lr * ci + li * cr + xbuf[pl.ds(o, NSEG), SW:]
                    xbuf[pl.ds(o, NSEG), :SW] = nr
                    xbuf[pl.ds(o, NSEG), SW:] = ni
                    return nr, ni

                lax.fori_loop(0, ta, fstep, (xbuf[0:NSEG, :SW], xbuf[0:NSEG, SW:]), unroll=2)

            def rstep(k, c, lr=lr, li=li):
                o = pl.multiple_of((ta - 1 - k) * NSEG, NSEG)
                gr_n, gi_n = c[0], c[1]
                gr = gbuf[pl.ds(o, NSEG), :SW] + lr * gr_n + li * gi_n
                gi = gbuf[pl.ds(o, NSEG), SW:] - li * gr_n + lr * gi_n
                if not emit:
                    return gr, gi
                gbuf[pl.ds(o, NSEG), :SW] = gr
                gbuf[pl.ds(o, NSEG), SW:] = gi
                xr, xi = xbuf[pl.ds(o, NSEG), :SW], xbuf[pl.ds(o, NSEG), SW:]
                return gr, gi, c[2] + gr * xr + gi * xi, c[3] + gi * xr - gr * xi

            c0 = (gcarry[rws, :SW], gcarry[rws, SW:])
            if emit:
                c0 = c0 + (jnp.zeros((NSEG, SW), F32), jnp.zeros((NSEG, SW), F32))
            cf = lax.fori_loop(0, ta, rstep, c0, unroll=2)
            gcarry[rws, :SW] = cf[0]
            gcarry[rws, SW:] = cf[1]
            if emit:
                dl_ref[rws, :SW] += cf[2]
                dl_ref[rws, SW:] += cf[3]
                gb16 = gbuf[...].astype(BF16)
                du_ref[:, cols] = jnp.dot(gb16, bt_ref[gb], preferred_element_type=F32) + d_ref[:, cols] * dyg
                tn = (((0,), (0,)), ((), ()))
                db_ref[gb] += lax.dot_general(ug.astype(BF16), gb16, tn, preferred_element_type=F32)
                dc_ref[gb] += lax.dot_general(dyg.astype(BF16), xbuf[NSEG:, :].astype(BF16), tn,
                                              preferred_element_type=F32)
                dd_ref[:, cols] += _rsum(dyg * ug)
        if not emit:
            gout_ref[...] = gcarry[...]

    rev = pl.BlockSpec((R, D), lambda i: (nb - 1 - i, 0))
    acc3 = (ngb, 128, 2 * SW)
    if emit:
        out_shape = [jax.ShapeDtypeStruct((L, D), F32), jax.ShapeDtypeStruct(acc3, F32), jax.ShapeDtypeStruct(acc3, F32),
                     jax.ShapeDtypeStruct((crows, 2 * SW), F32), jax.ShapeDtypeStruct((1, D), F32)]
        out_specs = [rev, _whole(acc3), _whole(acc3), _whole((crows, 2 * SW)), _whole((1, D))]
        scratch = [pltpu.VMEM((crows, 2 * SW), F32), pltpu.VMEM((R + NSEG, 2 * SW), F32),
                   pltpu.VMEM((R, 2 * SW), F32), pltpu.VMEM((R, D), F32)]
    else:
        out_shape = [jax.ShapeDtypeStruct((crows, 2 * SW), F32)]
        out_specs = [_whole((crows, 2 * SW))]
        scratch = [pltpu.VMEM((crows, 2 * SW), F32), pltpu.VMEM((R, 2 * SW), F32), pltpu.VMEM((R, D), F32)]
    return pl.pallas_call(
        body, name=name, grid=(nb,),
        in_specs=[rev, rev, rev, pl.BlockSpec((1, crows, 2 * SW), lambda i: (nb - 1 - i, 0, 0)),
                  _whole(gcarry_in.shape), _whole(tb["lam"].shape), _whole(tb["bbd"].shape),
                  _whole(tb["bbdT"].shape), _whole(tb["cbdT"].shape), _whole(dvec.shape)],
        out_specs=out_specs, out_shape=out_shape, scratch_shapes=scratch,
        compiler_params=_params("arbitrary"),
    )(up, y, dz, ck, gcarry_in, tb["lam"], tb["bbd"], tb["bbdT"], tb["cbdT"], dvec)


NN = (((1,), (0,)), ((), ()))
TN = (((0,), (0,)), ((), ()))
NT = (((1,), (1,)), ((), ()))


def _dot3(lhs, rhs, dn, split):
    x = rhs if split == "rhs" else lhs
    hi = x.astype(BF16)
    r1 = x - hi.astype(F32)
    mid = r1.astype(BF16)
    lo = (r1 - mid.astype(F32)).astype(BF16)
    out = None
    for part in (hi, mid, lo):
        ops = (lhs, part) if split == "rhs" else (part, rhs)
        t = lax.dot_general(ops[0], ops[1], dn, preferred_element_type=F32)
        out = t if out is None else out + t
    return out


def _log_sigmoid(x):
    return jnp.minimum(x, 0.0) - jnp.log(1.0 + jnp.exp(-jnp.abs(x)))


def _gla_gates(p, wg_ref, bg_ref, QK):
    C = p.shape[0]
    glr = p[:, 6 * QK:6 * QK + 128].astype(BF16)
    gpre = jnp.dot(glr, wg_ref[...], preferred_element_type=F32) + bg_ref[...]
    la = _log_sigmoid(gpre) * (1.0 / GATE_TAU)
    row = lax.broadcasted_iota(jnp.int32, (C, C), 0)
    col = lax.broadcasted_iota(jnp.int32, (C, C), 1)
    gc = _dot3((row >= col).astype(BF16), la, NN, "rhs")
    ge = gc[C - 1:C, :]
    w = jnp.exp(ge - gc)
    return glr, gpre, la, ge, w


def _gla_fwd(proj, wg2p, bg, gn, *, name):
    L = proj.shape[0]
    QK = wg2p.shape[1]
    DK, DV = QK // HEADS, 2 * QK // HEADS
    nC = L // CHUNK
    NB = min(GLA_NB, nC)
    assert nC % NB == 0
    scale = DK ** -0.5

    def body(p_ref, wg_ref, bg_ref, gn_ref, og_ref, s_ref, sst):
        @pl.when(pl.program_id(0) == 0)
        def _():
            sst[...] = jnp.zeros_like(sst)

        ones = jnp.ones((CHUNK, DV), BF16)
        for cc in range(NB):
            rows = slice(cc * CHUNK, (cc + 1) * CHUNK)
            p = p_ref[rows, :]
            _, _, la, _, w = _gla_gates(p, wg_ref, bg_ref, QK)
            for h in range(HEADS):
                ks, vs = slice(h * DK, (h + 1) * DK), slice(h * DV, (h + 1) * DV)
                q = p[:, h * DK:(h + 1) * DK] * scale
                kd = p[:, QK + h * DK:QK + (h + 1) * DK] * w[:, ks]
                v = p[:, 2 * QK + h * DV:2 * QK + (h + 1) * DV]
                r = p[:, 4 * QK + h * DV:4 * QK + (h + 1) * DV]
                dec = jnp.exp(_dot3(la[:, ks], ones, TN, "lhs"))
                kv = lax.dot_general(kd.astype(BF16), v.astype(BF16), TN, preferred_element_type=F32)
                S = dec * sst[ks, :] + kv
                sst[ks, :] = S
                s_ref[cc, ks, :] = S
                o = jnp.dot(q.astype(BF16), S.astype(BF16), preferred_element_type=F32)
                on = o * lax.rsqrt(jnp.mean(o * o, axis=-1, keepdims=True) + EPS)
                og_ref[rows, vs] = (on * gn_ref[:, vs] * (r * _sigmoid(r))).astype(BF16)

    RB = NB * CHUNK
    return pl.pallas_call(
        body, name=name, grid=(nC // NB,),
        in_specs=[pl.BlockSpec((RB, proj.shape[1]), lambda i: (i, 0)), _whole(wg2p.shape), _whole(bg.shape), _whole(gn.shape)],
        out_specs=[pl.BlockSpec((RB, 2 * QK), lambda i: (i, 0)), pl.BlockSpec((NB, QK, DV), lambda i: (i, 0, 0))],
        out_shape=[jax.ShapeDtypeStruct((L, 2 * QK), BF16), jax.ShapeDtypeStruct((nC, QK, DV), F32)],
        scratch_shapes=[pltpu.VMEM((QK, DV), F32)],
        compiler_params=_params("arbitrary"),
    )(proj, wg2p, bg, gn)


def _gla_bwd(proj, dog, states, wg2p, bg, gn, *, name):
    L, W = proj.shape
    QK = wg2p.shape[1]
    DK, DV = QK // HEADS, 2 * QK // HEADS
    nC = L // CHUNK
    NB = min(GLA_NB, nC)
    nB = nC // NB
    scale = DK ** -0.5

    def body(p_ref, dog_ref, sc_ref, sp_ref, wg_ref, bg_ref, gn_ref, dp_ref, dwg_ref, dbg_ref, dgn_ref, gst):
        i = pl.program_id(0)

        @pl.when(i == 0)
        def _():
            gst[...] = jnp.zeros_like(gst)
            dwg_ref[...] = jnp.zeros_like(dwg_ref)
            dbg_ref[...] = jnp.zeros_like(dbg_ref)
            dgn_ref[...] = jnp.zeros_like(dgn_ref)

        row = lax.broadcasted_iota(jnp.int32, (CHUNK, CHUNK), 0)
        col = lax.broadcasted_iota(jnp.int32, (CHUNK, CHUNK), 1)
        tri_u = (col >= row).astype(BF16)
        ones = jnp.ones((CHUNK, DV), BF16)
        ones8 = jnp.ones((8, DV), BF16)
        for cc in range(NB - 1, -1, -1):
            rows = slice(cc * CHUNK, (cc + 1) * CHUNK)
            p = p_ref[rows, :]
            glr, gpre, la, ge, w = _gla_gates(p, wg_ref, bg_ref, QK)
            dla_heads = []
            for h in range(HEADS):
                ks, vs = slice(h * DK, (h + 1) * DK), slice(h * DV, (h + 1) * DV)
                qs = p[:, h * DK:(h + 1) * DK] * scale
                k = p[:, QK + h * DK:QK + (h + 1) * DK]
                v = p[:, 2 * QK + h * DV:2 * QK + (h + 1) * DV]
                r = p[:, 4 * QK + h * DV:4 * QK + (h + 1) * DV]
                wh = w[:, ks]
                kd = k * wh
                S = sc_ref[cc, ks, :]
                if cc > 0:
                    Sp = sc_ref[cc - 1, ks, :]
                else:
                    Sp = jnp.where(i < nB - 1, sp_ref[0, ks, :], 0.0)
                o = jnp.dot(qs.astype(BF16), S.astype(BF16), preferred_element_type=F32)
                rs = lax.rsqrt(jnp.mean(o * o, axis=-1, keepdims=True) + EPS)
                on = o * rs
                sr = _sigmoid(r)
                dg = dog_ref[rows, vs]
                gnh = gn_ref[:, vs]
                dp_ref[rows, 4 * QK + h * DV:4 * QK + (h + 1) * DV] = (
                    dg * on * gnh * (sr * (1.0 + r * (1.0 - sr)))).astype(BF16)
                dt = dg * (r * sr)
                dgn_ref[:, vs] += _rsum(dt * on)
                don = dt * gnh
                do = (rs * (don - on * jnp.mean(don * on, axis=-1, keepdims=True))).astype(BF16)
                Gc = gst[ks, :] + lax.dot_general(qs.astype(BF16), do, TN, preferred_element_type=F32)
                G16 = Gc.astype(BF16)
                dp_ref[rows, h * DK:(h + 1) * DK] = (
                    lax.dot_general(do, S.astype(BF16), NT, preferred_element_type=F32) * scale).astype(BF16)
                dkd = lax.dot_general(v.astype(BF16), G16, NT, preferred_element_type=F32)
                dp_ref[rows, 2 * QK + h * DV:2 * QK + (h + 1) * DV] = jnp.dot(
                    kd.astype(BF16), G16, preferred_element_type=F32).astype(BF16)
                gst[ks, :] = jnp.exp(_dot3(la[:, ks], ones, TN, "lhs")) * Gc
                ddec = _dot3(ones8, Gc * Sp, NT, "rhs")[0:1, :]
                dp_ref[rows, QK + h * DK:QK + (h + 1) * DK] = (dkd * wh).astype(BF16)
                dww = dkd * kd
                dge = jnp.exp(ge[:, ks]) * ddec + _rsum(dww)
                dla_heads.append(dge - _dot3(tri_u, dww, NN, "rhs"))
            dla = jnp.concatenate(dla_heads, axis=1)
            dgpre = dla * (1.0 / GATE_TAU) * (1.0 - _sigmoid(gpre))
            d16 = dgpre.astype(BF16)
            dp_ref[rows, 6 * QK:6 * QK + 128] = lax.dot_general(d16, wg_ref[...], NT, preferred_element_type=F32).astype(BF16)
            dwg_ref[...] += lax.dot_general(glr, d16, TN, preferred_element_type=F32)
            dbg_ref[...] += _rsum(dgpre)

    RB = NB * CHUNK
    rev = lambda i: (nB - 1 - i, 0)
    return pl.pallas_call(
        body, name=name, grid=(nB,),
        in_specs=[pl.BlockSpec((RB, W), rev), pl.BlockSpec((RB, 2 * QK), rev),
                  pl.BlockSpec((NB, QK, DV), lambda i: (nB - 1 - i, 0, 0)),
                  pl.BlockSpec((1, QK, DV), lambda i: (jnp.maximum(NB * (nB - 1 - i) - 1, 0), 0, 0)),
                  _whole(wg2p.shape), _whole(bg.shape), _whole(gn.shape)],
        out_specs=[pl.BlockSpec((RB, W), rev), _whole((128, QK)), _whole((1, QK)), _whole((1, 2 * QK))],
        out_shape=[jax.ShapeDtypeStruct((L, W), BF16), jax.ShapeDtypeStruct((128, QK), F32),
                   jax.ShapeDtypeStruct((1, QK), F32), jax.ShapeDtypeStruct((1, 2 * QK), F32)],
        scratch_shapes=[pltpu.VMEM((QK, DV), F32)],
        compiler_params=_params("arbitrary"),
    )(proj, dog, states, states, wg2p, bg, gn)


def _coords():
    return lax.axis_index("x"), lax.axis_index("y"), lax.axis_index("c")


def _other_chips(x, y):
    return [(1 - x, y, 2 * (1 - x) + y), (x, 1 - y, 2 * x + 1 - y), (1 - x, 1 - y, 2 * (1 - x) + 1 - y)]


def _hbm_call(body, ins, out_shapes, n_sems, *, name, alias=False):
    any_spec = pl.BlockSpec(memory_space=pl.ANY)
    return pl.pallas_call(
        body, name=name, in_specs=[any_spec] * len(ins), out_specs=[any_spec] * len(out_shapes), out_shape=out_shapes,
        scratch_shapes=[pltpu.SemaphoreType.DMA((n,)) for n in n_sems],
        input_output_aliases={k: k for k in range(len(ins))} if alias else {},
    )(*ins)


def _exchange(src, masks, *, name):
    vary = [any(m[k] for m in masks) for k in range(3)]
    nslots = 2 ** sum(vary)
    n = len(masks)

    def slot(coords):
        s = 0
        for k in range(3):
            if vary[k]:
                s = s * 2 + coords[k]
        return s

    def body(src_ref, dst_ref, send_sems, recv_sems, loc_sem):
        me = _coords()
        mine = slot(me)
        loc = pltpu.make_async_copy(src_ref, dst_ref.at[mine], loc_sem.at[0])
        loc.start()
        copies = []
        for k, m in enumerate(masks):
            peer = tuple(1 - me[d] if m[d] else me[d] for d in range(3))
            cp = pltpu.make_async_remote_copy(src_ref=src_ref, dst_ref=dst_ref.at[mine], send_sem=send_sems.at[k],
                                              recv_sem=recv_sems.at[k], device_id=peer, device_id_type=MESH)
            cp.start()
            copies.append(cp)
        for cp in copies:
            cp.wait()
        loc.wait()

    return _hbm_call(body, [src], [jax.ShapeDtypeStruct((nslots,) + tuple(src.shape), src.dtype)], (n, n, 1), name=name)[0]


def _cast_into(t, lead, kind, chip, *, name, tm=256):
    r, cc = t.shape[-2:]
    tm = min(tm, r)
    nblk = r // tm
    if kind == "col":
        shp, o_spec = (r, NCH * cc), pl.BlockSpec((tm, cc), lambda i, s: (i, s[0]))
    elif kind == "row":
        shp, o_spec = (NCH * r, cc), pl.BlockSpec((tm, cc), lambda i, s: (s[0] * nblk + i, 0))
    else:
        shp, o_spec = (NCH, r, cc), pl.BlockSpec((None, tm, cc), lambda i, s: (s[0], i, 0))

    def body(s_ref, t_ref, o_ref):
        o_ref[...] = t_ref[...].astype(o_ref.dtype)

    return pl.pallas_call(
        body, name=name,
        grid_spec=pltpu.PrefetchScalarGridSpec(
            num_scalar_prefetch=1, grid=(nblk,),
            in_specs=[pl.BlockSpec((None, tm, cc), lambda i, s: (lead, i, 0))], out_specs=o_spec),
        out_shape=jax.ShapeDtypeStruct(shp, BF16), compiler_params=_params("parallel"),
    )(chip.reshape(1).astype(jnp.int32), t)


def _gather_weights(arrs, shard_shapes, kinds, *, name):
    n = len(arrs)

    def body(*refs):
        dst = refs[n:2 * n]
        send_sems, recv_sems = refs[2 * n:]
        x, y, c = _coords()
        chip = 2 * x + y
        others = _other_chips(x, y)
        sib = (x, y, 1 - c)

        def window(p, chip_id, cc):
            r, cols = shard_shapes[p]
            h = r // 2
            if kinds[p] == "col":
                return dst[p].at[pl.ds(cc * h, h), pl.ds(pl.multiple_of(chip_id * cols, 128), cols)]
            if kinds[p] == "row":
                return dst[p].at[pl.ds(chip_id * r + cc * h, h), :]
            return dst[p].at[chip_id, pl.ds(cc * h, h), :]

        def copy(p, k, win, to):
            return pltpu.make_async_remote_copy(src_ref=win, dst_ref=win, send_sem=send_sems.at[6 * p + k],
                                                recv_sem=recv_sems.at[6 * p + k], device_id=to, device_id_type=MESH)

        sends = []
        for p in range(n):
            for j, (ox, oy, _) in enumerate(others):
                cp = copy(p, j, window(p, chip, c), (ox, oy, c))
                cp.start()
                sends.append(cp)
        for j, (_, _, oc) in enumerate(others):
            for p in range(n):
                copy(p, j, window(p, oc, c), (x, y, c)).wait_recv()
                fw = copy(p, 3 + j, window(p, oc, c), sib)
                fw.start()
                sends.append(fw)
        for p in range(n):
            for j, (_, _, oc) in enumerate(others):
                copy(p, 3 + j, window(p, oc, 1 - c), sib).wait_recv()
        for cp in sends:
            cp.wait_send()

    outs = [jax.ShapeDtypeStruct(a.shape, a.dtype) for a in arrs]
    return _hbm_call(body, arrs, outs, (6 * n, 6 * n), name=name, alias=True)


HBM_SPEC = pl.BlockSpec(memory_space=pltpu.HBM)
SEM_SPEC = pl.BlockSpec(memory_space=pltpu.SEMAPHORE)
EFFECT = pltpu.SideEffectType.DATAFLOW_SIDE_EFFECTING


def _window(ref, shard_shape, kind, chip_id, cc):
    r, cols = shard_shape
    h = r // 2
    if kind == "col":
        return ref.at[pl.ds(cc * h, h), pl.ds(pl.multiple_of(chip_id * cols, 128), cols)]
    if kind == "row":
        return ref.at[pl.ds(chip_id * r + cc * h, h), :]
    return ref.at[chip_id, pl.ds(cc * h, h), :]


def _split_start(start, arrs, n_sems, *, name):
    n, ns = len(arrs), len(n_sems)

    def body(*refs):
        start(refs[:n], refs[n:n + ns])
        refs[-1][...] = jnp.zeros_like(refs[-1])

    outs = pl.pallas_call(
        body, name=name,
        out_shape=tuple([pltpu.SemaphoreType.DMA((k,)) for k in n_sems] + [pltpu.HBM(a.shape, a.dtype) for a in arrs]
                        + [jax.ShapeDtypeStruct((8, 128), F32)]),
        in_specs=[HBM_SPEC] * n, out_specs=tuple([SEM_SPEC] * ns + [HBM_SPEC] * n + [pl.BlockSpec(memory_space=pltpu.VMEM)]),
        input_output_aliases={k: ns + k for k in range(n)},
        compiler_params=pltpu.CompilerParams(has_side_effects=EFFECT),
    )(*[pltpu.with_memory_space_constraint(a, pltpu.HBM) for a in arrs])
    return list(outs[:ns]), list(outs[ns:ns + n]), outs[-1]


def _split_wait(wait, arrs, sems, after, *, name):
    n, ns = len(arrs), len(sems)

    def body(*refs):
        wait(refs[:n], refs[n:n + ns])

    return pl.pallas_call(
        body, name=name, out_shape=tuple(pltpu.HBM(a.shape, a.dtype) for a in arrs),
        in_specs=[HBM_SPEC] * n + [SEM_SPEC] * ns + [pl.BlockSpec(memory_space=pl.ANY)], out_specs=tuple([HBM_SPEC] * n),
        input_output_aliases={k: k for k in range(n)},
        compiler_params=pltpu.CompilerParams(has_side_effects=EFFECT),
    )(*arrs, *sems, after)


def _gw_copies(refs, send_sems, recv_sems, shard_shapes, kinds, outgoing):
    x, y, c = _coords()
    chip = 2 * x + y
    out = []
    for p in range(len(refs)):
        for j, (ox, oy, oc) in enumerate(_other_chips(x, y)):
            win = _window(refs[p], shard_shapes[p], kinds[p], chip if outgoing else oc, c)
            out.append(pltpu.make_async_remote_copy(
                src_ref=win, dst_ref=win, send_sem=send_sems.at[3 * p + j], recv_sem=recv_sems.at[3 * p + j],
                device_id=(ox, oy, c), device_id_type=MESH))
    return out


def _gw_start(arrs, shard_shapes, kinds, groups, *, name):
    def start(refs, sems):
        for g, idx in enumerate(groups):
            for cp in _gw_copies([refs[p] for p in idx], sems[2 * g], sems[2 * g + 1], [shard_shapes[p] for p in idx],
                                 [kinds[p] for p in idx], True):
                cp.start()

    n_sems = [3 * len(idx) for idx in groups for _ in range(2)]
    sems, thru, token = _split_start(start, arrs, n_sems, name=name)
    return [(sems[2 * g], sems[2 * g + 1]) for g in range(len(groups))], thru, token


def _gw_wait(arrs, shard_shapes, kinds, sem_pair, after, *, name):
    def wait(refs, sems):
        for cp in _gw_copies(refs, sems[0], sems[1], shard_shapes, kinds, True):
            cp.wait_send()
        for cp in _gw_copies(refs, sems[0], sems[1], shard_shapes, kinds, False):
            cp.wait_recv()

    return _split_wait(wait, arrs, list(sem_pair), after, name=name)


def _gw_forward(arrs, shard_shapes, kinds, *, name):
    n = len(arrs)

    def body(*refs):
        dst = refs[n:2 * n]
        send_sems, recv_sems = refs[2 * n:]
        x, y, c = _coords()
        sends = []
        for p in range(n):
            for j, (_, _, oc) in enumerate(_other_chips(x, y)):
                win = _window(dst[p], shard_shapes[p], kinds[p], oc, c)
                cp = pltpu.make_async_remote_copy(src_ref=win, dst_ref=win, send_sem=send_sems.at[3 * p + j],
                                                  recv_sem=recv_sems.at[3 * p + j], device_id=(x, y, 1 - c),
                                                  device_id_type=MESH)
                cp.start()
                sends.append(cp)
        for p in range(n):
            for j, (_, _, oc) in enumerate(_other_chips(x, y)):
                win = _window(dst[p], shard_shapes[p], kinds[p], oc, 1 - c)
                pltpu.make_async_remote_copy(src_ref=win, dst_ref=win, send_sem=send_sems.at[3 * p + j],
                                             recv_sem=recv_sems.at[3 * p + j], device_id=(x, y, 1 - c),
                                             device_id_type=MESH).wait_recv()
        for cp in sends:
            cp.wait_send()

    outs = [jax.ShapeDtypeStruct(a.shape, a.dtype) for a in arrs]
    return _hbm_call(body, arrs, outs, (3 * n, 3 * n), name=name, alias=True)


def _rs_chips_copies(parts, lands, send_sems, recv_sems):
    x, y, c = _coords()
    chip = 2 * x + y
    out = []
    for p in range(len(parts)):
        for j, (ox, oy, oc) in enumerate(_other_chips(x, y)):
            out.append(pltpu.make_async_remote_copy(
                src_ref=parts[p].at[oc], dst_ref=lands[p].at[chip], send_sem=send_sems.at[3 * p + j],
                recv_sem=recv_sems.at[3 * p + j], device_id=(ox, oy, c), device_id_type=MESH))
    return out


def _rs_chips_start(parts, *, name):
    n = len(parts)

    def start(refs, sems):
        for cp in _rs_chips_copies(refs[:n], refs[n:], sems[0], sems[1]):
            cp.start()

    lands = [lax.empty(t.shape, t.dtype) for t in parts]
    sems, thru, token = _split_start(start, list(parts) + lands, [3 * n, 3 * n], name=name)
    return (sems[0], sems[1]), thru[:n], thru[n:], token


def _rs_chips_wait(groups, after, *, name):
    sizes = [len(g[1]) for g in groups]
    arrs = [a for g in groups for a in list(g[1]) + list(g[2])]
    sems = [s for g in groups for s in g[0]]

    def wait(refs, sem_refs):
        o = 0
        for k, n in enumerate(sizes):
            for cp in _rs_chips_copies(refs[o:o + n], refs[o + n:o + 2 * n], sem_refs[2 * k], sem_refs[2 * k + 1]):
                cp.wait()
            o += 2 * n

    outs = _split_wait(wait, arrs, sems, after, name=name)
    res, o = [], 0
    for n in sizes:
        res.append((list(outs[o:o + n]), list(outs[o + n:o + 2 * n])))
        o += 2 * n
    return res


def _rs_cores(grads, *, name):
    n = len(grads)
    outs = [jax.ShapeDtypeStruct((g.shape[0], g.shape[1] // 2, g.shape[2]), g.dtype) for g in grads]

    def body(*refs):
        src, dst = refs[:n], refs[n:2 * n]
        send_sems, recv_sems = refs[2 * n:]
        x, y, c = _coords()
        copies = []
        for p in range(n):
            nsh, r, _ = grads[p].shape
            h = r // 2
            for j in range(nsh):
                cp = pltpu.make_async_remote_copy(
                    src_ref=src[p].at[j, pl.ds((1 - c) * h, h), :], dst_ref=dst[p].at[j],
                    send_sem=send_sems.at[nsh * p + j], recv_sem=recv_sems.at[nsh * p + j],
                    device_id=(x, y, 1 - c), device_id_type=MESH)
                cp.start()
                copies.append(cp)
        for cp in copies:
            cp.wait()

    tot = sum(g.shape[0] for g in grads)
    return _hbm_call(body, grads, outs, (tot, tot), name=name)


def _sum_own_half(full, recv, ci, out_dtype, *, name):
    nsh, h, cols = recv.shape
    tm = h if nsh * h * cols * 4 <= (2 << 20) else _tile_rows(h, 256)
    nblk = h // tm

    def body(c_ref, f_ref, r_ref, o_ref):
        o_ref[...] = (f_ref[...] + r_ref[...]).astype(o_ref.dtype)

    return pl.pallas_call(
        body, name=name,
        grid_spec=pltpu.PrefetchScalarGridSpec(
            num_scalar_prefetch=1, grid=(nsh, nblk),
            in_specs=[pl.BlockSpec((1, tm, cols), lambda j, i, c_ref: (j, c_ref[0] * nblk + i, 0)),
                      pl.BlockSpec((1, tm, cols), lambda j, i, c_ref: (j, i, 0))],
            out_specs=pl.BlockSpec((1, tm, cols), lambda j, i, c_ref: (j, i, 0))),
        out_shape=jax.ShapeDtypeStruct((nsh, h, cols), out_dtype), compiler_params=_params("parallel", "parallel"),
    )(ci.reshape(1).astype(jnp.int32), full, recv)


def _rs_chips(parts, *, name):
    n = len(parts)
    outs = [jax.ShapeDtypeStruct(t.shape, t.dtype) for t in parts]

    def body(*refs):
        src, dst = refs[:n], refs[n:2 * n]
        send_sems, recv_sems = refs[2 * n:]
        x, y, c = _coords()
        chip = 2 * x + y
        copies = []
        for p in range(n):
            for j, (ox, oy, oc) in enumerate(_other_chips(x, y)):
                cp = pltpu.make_async_remote_copy(
                    src_ref=src[p].at[oc], dst_ref=dst[p].at[chip], send_sem=send_sems.at[3 * p + j],
                    recv_sem=recv_sems.at[3 * p + j], device_id=(ox, oy, c), device_id_type=MESH)
                cp.start()
                copies.append(cp)
        for cp in copies:
            cp.wait()

    return _hbm_call(body, parts, outs, (3 * n, 3 * n), name=name)


def _sum_chips(recv, own, chip, ci, *, name, nlead=1, lead=0, prev=None, spread=False):
    nsh, h, cols = recv.shape
    tm = h if nsh * h * cols * 4 <= (2 << 20) else _tile_rows(h, 256)
    nblk = h // tm
    rows_out = 2 * h * (nsh if spread else 1)

    def body(s_ref, r_ref, o_ref, *rest):
        out_ref = rest[-1]
        t = None
        for s in range(nsh):
            v = jnp.where(s_ref[0] == s, o_ref[s], r_ref[s]).astype(F32)
            t = v if t is None else t + v
        out_ref[...] = t

    def out_idx(i, s):
        return (lead, (s[0] * 2 * nblk if spread else 0) + s[1] * nblk + i, 0)

    blk = pl.BlockSpec((nsh, tm, cols), lambda i, s: (0, i, 0))
    ins = [recv, own] + ([prev] if prev is not None else [])
    return pl.pallas_call(
        body, name=name,
        grid_spec=pltpu.PrefetchScalarGridSpec(
            num_scalar_prefetch=1, grid=(nblk,),
            in_specs=[blk, blk] + ([pl.BlockSpec(memory_space=pl.ANY)] if prev is not None else []),
            out_specs=pl.BlockSpec((None, tm, cols), out_idx)),
        out_shape=jax.ShapeDtypeStruct((nlead, rows_out, cols), F32),
        input_output_aliases={3: 0} if prev is not None else {},
        compiler_params=_params("arbitrary"),
    )(jnp.stack([chip, ci]).astype(jnp.int32), *ins)


def _rs_gather(arrs, halves, spread, *, name, nchunk=4):
    n = len(arrs)
    per = [a.shape[0] * nchunk for a in arrs]
    offs = [sum(per[:p]) for p in range(n)]

    def body(*refs):
        dst = refs[n:2 * n]
        send_sems, recv_sems = refs[2 * n:]
        x, y, c = _coords()
        chip = 2 * x + y
        copies = []
        for p in range(n):
            h = halves[p]
            q = h // nchunk
            base = chip * 2 * h if spread[p] else 0
            for l in range(arrs[p].shape[0]):
                for k in range(nchunk):
                    win = dst[p].at[l, pl.ds(base + c * h + k * q, q), :]
                    sem = offs[p] + l * nchunk + k
                    cp = pltpu.make_async_remote_copy(src_ref=win, dst_ref=win, send_sem=send_sems.at[sem],
                                                      recv_sem=recv_sems.at[sem], device_id=(x, y, 1 - c),
                                                      device_id_type=MESH)
                    cp.start()
                    copies.append(cp)
        for cp in copies:
            cp.wait_send()
        for p in range(n):
            h = halves[p]
            q = h // nchunk
            base = chip * 2 * h if spread[p] else 0
            for l in range(arrs[p].shape[0]):
                for k in range(nchunk):
                    win = dst[p].at[l, pl.ds(base + (1 - c) * h + k * q, q), :]
                    sem = offs[p] + l * nchunk + k
                    pltpu.make_async_remote_copy(src_ref=win, dst_ref=win, send_sem=send_sems.at[sem],
                                                 recv_sem=recv_sems.at[sem], device_id=(x, y, 1 - c),
                                                 device_id_type=MESH).wait_recv()

    outs = [jax.ShapeDtypeStruct(a.shape, a.dtype) for a in arrs]
    return _hbm_call(body, arrs, outs, (sum(per), sum(per)), name=name, alias=True)


def _adamw(w, g, m, v, *, name):
    nl, R, C = w.shape
    tm = _tile_rows(R, 256)

    blk = pl.BlockSpec((None, tm, C), lambda l, i: (l, i, 0))
    return pl.pallas_call(
        _adamw_body_copy(), name=name, grid=(nl, R // tm), in_specs=[blk] * 4, out_specs=[blk] * 3,
        out_shape=[jax.ShapeDtypeStruct((nl, R, C), F32)] * 3, compiler_params=_params("parallel", "parallel"),
    )(w, g, m, v)


def _adamw_body(w_ref, g_ref, m_ref, v_ref, d_ref, nm_ref, nv_ref):
    gg = g_ref[...]
    nm = B1 * m_ref[...] + (1.0 - B1) * gg
    nv = B2 * v_ref[...] + (1.0 - B2) * (gg * gg)
    m_hat = nm / (1.0 - B1 ** ASTEP)
    v_hat = nv / (1.0 - B2 ** ASTEP)
    d_ref[...] = -LR * (m_hat / (jnp.sqrt(v_hat) + AEPS) + WD * w_ref[...])
    nm_ref[...] = nm
    nv_ref[...] = nv


def _adamw_whole(w, g, m, v, *, name):
    return pl.pallas_call(_adamw_body_copy(), name=name, out_shape=[jax.ShapeDtypeStruct(w.shape, F32)] * 3,
                          compiler_params=_params())(w, g, m, v)


def _adamw_body_copy():
    def body(*refs):
        _adamw_body(*refs)
    return body


def _mod_cols(c_all, w_ada, b_cols, *, name):
    nl, D, cols = w_ada.shape
    B = c_all.shape[0]

    def body(c_ref, w_ref, b_ref, o_ref):
        cc = c_ref[...]
        cs = (cc * _sigmoid(cc)).astype(BF16)
        o_ref[0] = jnp.dot(cs, w_ref[0].astype(BF16), preferred_element_type=F32) + b_ref[0]

    return pl.pallas_call(
        body, name=name, grid=(nl,),
        in_specs=[_whole(c_all.shape), pl.BlockSpec((1, D, cols), lambda i: (i, 0, 0)), pl.BlockSpec((1, 1, cols), lambda i: (i, 0, 0))],
        out_specs=pl.BlockSpec((1, B, cols), lambda i: (i, 0, 0)),
        out_shape=jax.ShapeDtypeStruct((nl, B, cols), F32), compiler_params=_params("arbitrary"),
    )(c_all, w_ada, b_cols)


def _ada_grad(c_all, dmod_cols, *, name):
    nl, B, cols = dmod_cols.shape
    D = c_all.shape[1]

    def body(c_ref, d_ref, o_ref):
        cc = c_ref[...]
        cs = (cc * _sigmoid(cc)).astype(BF16)
        o_ref[0] = lax.dot_general(cs, d_ref[0].astype(BF16), TN, preferred_element_type=F32)

    return pl.pallas_call(
        body, name=name, grid=(nl,),
        in_specs=[_whole(c_all.shape), pl.BlockSpec((1, B, cols), lambda i: (i, 0, 0))],
        out_specs=pl.BlockSpec((1, D, cols), lambda i: (i, 0, 0)),
        out_shape=jax.ShapeDtypeStruct((nl, D, cols), F32), compiler_params=_params("arbitrary"),
    )(c_all, dmod_cols)


def _s5_disc(a_re, a_im, log_dt, b_re, b_im):
    dt = jnp.exp(log_dt)[:, None]
    mag = jnp.exp(a_re * dt)
    ph = a_im * dt
    lb_re = mag * jnp.cos(ph)
    lb_im = mag * jnp.sin(ph)
    den = a_re * a_re + a_im * a_im
    nr = lb_re - 1.0
    ni = lb_im
    f_re = (nr * a_re + ni * a_im) / den
    f_im = (ni * a_re - nr * a_im) / den
    bb_re = f_re[..., None] * b_re - f_im[..., None] * b_im
    bb_im = f_re[..., None] * b_im + f_im[..., None] * b_re
    return lb_re, lb_im, bb_re, bb_im


def _to_segments(t):
    L, D = t.shape
    return t.reshape(NSEG, L // NSEG, D).transpose(1, 0, 2).reshape(L, D)


def _from_segments(t):
    L, D = t.shape
    return t.reshape(L // NSEG, NSEG, D).transpose(1, 0, 2).reshape(L, D)


def _mlp_fwd(h2, w1, w2, tag):
    a = _matmul(h2, w1, name=f"ff1_{tag}", out_dtypes=(BF16,), epi=lambda acc: (jnp.maximum(acc, 0.0),))
    f = _matmul(a, w2, name=f"ff2_{tag}", a_fn=jnp.square)
    return a, f


def _mlp_bwd(df, h2, a, w1, w2, tag):
    da = _matmul(df, w2, tb=True, name=f"ff2_dx_{tag}", out_dtypes=(BF16,), epi_ins=(a,),
                 epi=lambda acc, at: (acc * (2.0 * at.astype(F32)),))
    dw2 = _matmul(a, df, ta=True, name=f"ff2_dw_{tag}", a_fn=jnp.square)
    dh2 = _matmul(da, w1, tb=True, name=f"ff1_dx_{tag}")
    dw1 = _matmul(h2, da, ta=True, name=f"ff1_dw_{tag}", col_shards=NCH)
    return dh2, dw1, dw2


def kernel(x, c, w_ada, b_ada, norm_mix, norm_mlp, s5_a_re, s5_a_im, s5_log_dt, s5_b_re, s5_b_im, s5_c_re, s5_c_im, s5_d, s5_w_glu, gla_w_in, gla_w_gate2, gla_b_gate, gla_g_norm, gla_w_out, w_ff1, w_ff2, norm_final, loss_target, m_w_ada, m_b_ada, m_norm_mix, m_norm_mlp, m_s5_a_re, m_s5_a_im, m_s5_log_dt, m_s5_b_re, m_s5_b_im, m_s5_c_re, m_s5_c_im, m_s5_d, m_s5_w_glu, m_gla_w_in, m_gla_w_gate2, m_gla_b_gate, m_gla_g_norm, m_gla_w_out, m_w_ff1, m_w_ff2, m_norm_final, v_w_ada, v_b_ada, v_norm_mix, v_norm_mlp, v_s5_a_re, v_s5_a_im, v_s5_log_dt, v_s5_b_re, v_s5_b_im, v_s5_c_re, v_s5_c_im, v_s5_d, v_s5_w_glu, v_gla_w_in, v_gla_w_gate2, v_gla_b_gate, v_gla_g_norm, v_gla_w_out, v_w_ff1, v_w_ff2, v_norm_final):
    args = dict(locals())
    L, D = x.shape[1], x.shape[2]
    QK = D // 2
    xi, yi, ci = _coords()
    chip = 2 * xi + yi
    dev = 2 * chip + ci

    c_all = _exchange(c.reshape(8, D // 8), MASK_ALL, name="gather_c").reshape(8, D)
    acols = w_ada.shape[2]
    b_cols = lax.dynamic_slice_in_dim(b_ada, chip * acols, acols, axis=1)[:, None, :]
    mod_cols = _mod_cols(c_all, w_ada, b_cols, name="ada_mod")
    mod_all = _exchange(mod_cols.reshape(16, acols), MASK_CHIPS, name="gather_mod")
    mod_all = mod_all.reshape(NCH, 2, 8, acols).transpose(1, 2, 0, 3).reshape(2, 8, NCH * acols)
    mod = lax.dynamic_index_in_dim(mod_all, dev, axis=1, keepdims=False).reshape(2, 6, 1, D)

    big = [("s5_w_glu", s5_w_glu, 0, "col"), ("gla_w_in", gla_w_in, 0, "slot"), ("gla_w_out", gla_w_out, 0, "row"),
           ("w_ff1_0", w_ff1, 0, "col"), ("w_ff1_1", w_ff1, 1, "col"), ("w_ff2_0", w_ff2, 0, "row"), ("w_ff2_1", w_ff2, 1, "row")]
    own16 = [_cast_into(t, lead, kind, chip, name=f"cast_{nm}") for nm, t, lead, kind in big]
    wshapes, wkinds = [b[1].shape[-2:] for b in big], [b[3] for b in big]
    wgroups = [[0, 3, 5], [1, 2, 4, 6]]
    wsems, wthru, wtoken = _gw_start(own16, wshapes, wkinds, wgroups, name="gather_w_start")
    W = {}

    def finish_weights(g, after):
        idx = wgroups[g]
        shp, knd = [wshapes[p] for p in idx], [wkinds[p] for p in idx]
        got = _gw_wait([wthru[p] for p in idx], shp, knd, wsems[g], after, name=f"gather_w_wait{g}")
        for p, w in zip(idx, _gw_forward(got, shp, knd, name=f"gather_w_cores{g}")):
            W[big[p][0]] = w

    cat = jnp.concatenate([gla_w_gate2[0].reshape(1, -1), gla_b_gate, gla_g_norm], axis=1)
    cat_all = _exchange(jnp.tile(cat, (8, 1)), MASK_CHIPS, name="gather_gla_small")[:, 0, :]
    qk4 = QK // NCH
    wg2 = cat_all[:, :GATE_RANK * qk4].reshape(NCH, GATE_RANK, qk4).transpose(1, 0, 2).reshape(GATE_RANK, QK)
    bg = cat_all[:, GATE_RANK * qk4:(GATE_RANK + 1) * qk4].reshape(1, QK)
    gn = cat_all[:, (GATE_RANK + 1) * qk4:].reshape(1, D)
    wg2p = jnp.concatenate([wg2, jnp.zeros((128 - GATE_RANK, QK), F32)], axis=0).astype(BF16)

    lb_re, lb_im, bb_re, bb_im = _s5_disc(s5_a_re[0], s5_a_im[0], s5_log_dt[0], s5_b_re[0], s5_b_im[0])
    tb = _s5_tables(lb_re, lb_im, bb_re, bb_im, s5_c_re[0], s5_c_im[0], L // NSEG)
    zero_carry = jnp.zeros_like(tb["lam"]) + wtoken[0, 0]

    def vec(t):
        return t.reshape(1, -1)

    xp = _to_segments(x[0])
    m0, m1 = mod[0], mod[1]
    (u0,) = _rows(lambda t, g, sc, sh: (_norm_mod(t, g, sc, sh),), [xp], [vec(norm_mix[0]), m0[1], m0[0]],
                  [(D, F32)], [], name="pre_mix0")
    (ends,) = _s5_fwd(u0, zero_carry, tb, s5_d, emit=False, name="s5_fwd_ends")
    carry0 = _s5_fix(ends, tb["lamT"], reverse=False, name="s5_fix_fwd")
    y0, z0, ck0 = _s5_fwd(u0, carry0, tb, s5_d, emit=True, name="s5_fwd")
    finish_weights(0, z0)
    vg0 = _matmul(z0, W["s5_w_glu"], name="glu")

    def res_glu_pre(xt, vgt, gt, g, sc, sh):
        xn = xt + gt * (vgt[:, :D] * _sigmoid(vgt[:, D:]))
        return xn, _norm_mod(xn, g, sc, sh)

    x2_0, h2_0 = _rows(res_glu_pre, [xp, vg0], [m0[2], vec(norm_mlp[0]), m0[4], m0[3]], [(D, F32), (D, BF16)], [],
                       name="res_mix0")
    a_0, f0 = _mlp_fwd(h2_0, W["w_ff1_0"], W["w_ff2_0"], "0")

    def res_pre(xt, bt, gt, g, sc, sh):
        xn = xt + gt * bt
        return xn, _norm_mod(xn, g, sc, sh)

    x3p, h1p = _rows(res_pre, [x2_0, f0], [m0[5], vec(norm_mix[1]), m1[1], m1[0]], [(D, F32), (D, BF16)], [],
                     name="res_mlp0")
    x3 = _from_segments(x3p)
    h1 = _from_segments(h1p)
    finish_weights(1, f0)
    w_in = W["gla_w_in"].transpose(1, 0, 2).reshape(D, -1)
    w_in_r = jnp.concatenate([w_in[:, :4 * QK], w_in[:, 4 * QK + GATE_RANK:], w_in[:, 4 * QK:4 * QK + GATE_RANK],
                              jnp.zeros((D, 128 - GATE_RANK), BF16)], axis=1)
    proj = _matmul(h1, w_in_r, name="gla_in", tn=640)
    og, states = _gla_fwd(proj, wg2p, bg, gn, name="gla_fwd")
    ymix = _matmul(og, W["gla_w_out"], name="gla_out")
    x2_1, h2_1 = _rows(res_pre, [x3, ymix], [m1[2], vec(norm_mlp[1]), m1[4], m1[3]], [(D, F32), (D, BF16)], [],
                       name="res_mix1")
    a_1, f1 = _mlp_fwd(h2_1, W["w_ff1_1"], W["w_ff2_1"], "1")

    def final(xt, ft, tgt, gt, g):
        xn = xt + gt * ft
        rs = lax.rsqrt(jnp.mean(xn * xn, axis=-1, keepdims=True) + EPS)
        xh = xn * rs
        e = xh * g - tgt
        dout = e * (1.0 / D)
        dxh = dout * g
        dx = rs * (dxh - xh * jnp.mean(dxh * xh, axis=-1, keepdims=True))
        lsum = 0.5 * jnp.sum(jnp.sum(e * e, axis=-1, keepdims=True), axis=0, keepdims=True) * (1.0 / D)
        return dx, dx * gt, jnp.broadcast_to(lsum, (1, 128)), _rsum(dout * xh), _rsum(dx * ft)

    dx, df1, loss_part, d_norm_final, dgt2_1 = _rows(
        final, [x2_1, f1, loss_target[0]], [m1[5], vec(norm_final)], [(D, F32), (D, BF16)],
        [(1, 128), (1, D), (1, D)], name="loss_head")
    loss = lax.psum(loss_part[0, 0], ("x", "y", "c"))

    def gate_bwd(dxt, bt, gt):
        return dxt * gt, _rsum(dxt * bt)

    def norm_bwd(xt, dht, drt, g, sc):
        dxn, dsh, dsc, dg = _norm_mod_bwd(xt, dht, g, sc)
        return drt + dxn, dsh, dsc, dg

    def norm_gate_bwd(xt, dht, drt, bt, g, sc, gt):
        dxn, dsh, dsc, dg = _norm_mod_bwd(xt, dht, g, sc)
        dxt = drt + dxn
        return dxt, dxt * gt, dsh, dsc, dg, _rsum(dxt * bt)

    vD = [(1, D)]
    dh2_1, dw_ff1_1, dw_ff2_1 = _mlp_bwd(df1, h2_1, a_1, W["w_ff1_1"], W["w_ff2_1"], "1")
    dx, dmix1, dsh2_1, dsc2_1, dg_mlp1, dgt1_1 = _rows(
        norm_gate_bwd, [x2_1, dh2_1, dx, ymix], [vec(norm_mlp[1]), m1[4], m1[2]], [(D, F32), (D, BF16)], vD * 4,
        name="norm_mlp1_bwd")
    dog = _matmul(dmix1, W["gla_w_out"], tb=True, name="gla_out_dx")
    dw_out = _matmul(og, dmix1, ta=True, name="gla_out_dw")
    dproj, dwg2p, dbg, dgn = _gla_bwd(proj, dog, states, wg2p, bg, gn, name="gla_bwd")
    dh1 = _matmul(dproj, w_in_r, tb=True, name="gla_in_dx", tk=640)
    dw_in_r = _matmul(h1, dproj, ta=True, name="gla_in_dw", tn=640)
    dx, dsh1_1, dsc1_1, dg_mix1 = _rows(norm_bwd, [x3, dh1, dx], [vec(norm_mix[1]), m1[1]], [(D, F32)], vD * 3,
                                        name="norm_mix1_bwd")
    tags = [b[0] for b in big] + ["small"]
    rs_groups = []

    def rs_begin(idx, srcs, gname):
        r1 = _rs_cores(srcs, name=f"rs_cores_{gname}")
        s1 = [_sum_own_half(g, r, ci, F32 if tags[k] == "small" else BF16, name=f"rs_sum_cores_{tags[k]}")
              for g, r, k in zip(srcs, r1, idx)]
        pair, parts, lands, token = _rs_chips_start(s1, name=f"rs_chips_start_{gname}")
        rs_groups.append((idx, pair, parts, lands))
        return token

    dw_in = jnp.concatenate([dw_in_r[:, :4 * QK], dw_in_r[:, 6 * QK:6 * QK + GATE_RANK], dw_in_r[:, 4 * QK:6 * QK]], axis=1)
    dw_in = dw_in.reshape(D, NCH, -1).transpose(1, 0, 2)
    tok1 = rs_begin([1, 2, 4, 6], [dw_in, dw_out.reshape(NCH, -1, D), dw_ff1_1, dw_ff2_1.reshape(NCH, -1, D)], "l1")

    dxp = _to_segments(dx)
    df0, dgt2_0 = _rows(gate_bwd, [dxp, f0], [m0[5] + tok1[0, 0]], [(D, BF16)], vD, name="gate_mlp0")
    dh2_0, dw_ff1_0, dw_ff2_0 = _mlp_bwd(df0, h2_0, a_0, W["w_ff1_0"], W["w_ff2_0"], "0")
    tok2 = rs_begin([3, 5], [dw_ff1_0, dw_ff2_0.reshape(NCH, -1, D)], "l0")

    def norm_glu_bwd(xt, dht, drt, vgt, g, sc, gt):
        dxn, dsh, dsc, dg = _norm_mod_bwd(xt, dht, g, sc)
        dxt = drt + dxn
        val, sg = vgt[:, :D], _sigmoid(vgt[:, D:])
        dbr = dxt * gt
        dvg = jnp.concatenate([dbr * sg, dbr * val * sg * (1.0 - sg)], axis=1)
        return dxt, dvg, dsh, dsc, dg, _rsum(dxt * val * sg)

    dxp, dvg0, dsh2_0, dsc2_0, dg_mlp0, dgt1_0 = _rows(
        norm_glu_bwd, [x2_0, dh2_0, dxp, vg0], [vec(norm_mlp[0]), m0[4] + tok2[0, 0], m0[2]], [(D, F32), (2 * D, BF16)],
        vD * 4, name="norm_mlp0_bwd")
    dz0 = _matmul(dvg0, W["s5_w_glu"], tb=True, name="glu_dx")
    dw_glu = _matmul(z0, dvg0, ta=True, name="glu_dw", tn=512, col_shards=NCH)
    (gends,) = _s5_bwd(u0, y0, dz0, ck0, zero_carry, tb, s5_d, emit=False, name="s5_bwd_ends")
    gcarry0 = _s5_fix(gends, tb["lamT"], reverse=True, name="s5_fix_bwd")
    du0, db_acc, dc_acc, dl_acc, dd_s5 = _s5_bwd(u0, y0, dz0, ck0, gcarry0, tb, s5_d, emit=True, name="s5_bwd")
    dxp, dsh1_0, dsc1_0, dg_mix0 = _rows(norm_bwd, [xp, du0, dxp], [vec(norm_mix[0]), m0[1]], [(D, F32)], vD * 3,
                                         name="norm_mix0_bwd")
    grad_x = _from_segments(dxp)[None]

    dmod = jnp.concatenate([dsh1_0, dsc1_0, dgt1_0, dsh2_0, dsc2_0, dgt2_0,
                            dsh1_1, dsc1_1, dgt1_1, dsh2_1, dsc2_1, dgt2_1], axis=1)
    dbb_re, dbb_im = _s5_untable(db_acc)
    dc_re, dc_im_neg = _s5_untable(dc_acc)
    nbk = D // 128
    dl = dl_acc.reshape(nbk, NSEG, 2, GPB * S5_P).sum(axis=1)
    smalls = [dmod, dg_mix0, dg_mix1, dg_mlp0, dg_mlp1, d_norm_final, dd_s5, dbg, dgn,
              dwg2p[:GATE_RANK].reshape(1, -1), dbb_re.reshape(1, -1), dbb_im.reshape(1, -1),
              dc_re.reshape(1, -1), dc_im_neg.reshape(1, -1), dl.reshape(1, -1)]
    ssz = [t.shape[1] for t in smalls]
    stot = sum(ssz)
    spad = -(-stot // 8192) * 8192
    svec = jnp.concatenate(smalls + [jnp.zeros((1, spad - stot), F32)], axis=1).reshape(NCH, spad // (128 * NCH), 128)

    dmod_all = _exchange(dmod.reshape(12 * D // 128, 128), MASK_ALL, name="gather_dmod").reshape(8, 2, 6 * D)
    dmod_cols = lax.dynamic_slice_in_dim(dmod_all, chip * acols, acols, axis=2).transpose(1, 0, 2)
    g_w_ada = _ada_grad(c_all, dmod_cols, name="ada_grad")

    rs_begin([0, 7], [dw_glu, svec], "last")
    landed = _rs_chips_wait([(g[1], g[2], g[3]) for g in rs_groups], dxp, name="rs_chips_wait")
    s1, r2 = {}, {}
    for (idx, _, _, _), (parts, lands) in zip(rs_groups, landed):
        for k, part, land in zip(idx, parts, lands):
            s1[k], r2[k] = part, land

    def fin(k, **kw):
        return _sum_chips(r2[k], s1[k], chip, ci, name=f"rs_sum_chips_{tags[k]}", **kw)

    f_ff1 = fin(4, nlead=2, lead=1, prev=fin(3, nlead=2, lead=0))
    f_ff2 = fin(6, nlead=2, lead=1, prev=fin(5, nlead=2, lead=0))
    finals = [fin(0), fin(1), fin(2), f_ff1, f_ff2, fin(7, spread=True)]
    halves = [t.shape[1] for t in (s1[0], s1[1], s1[2], s1[3], s1[5], s1[7])]
    g_glu, g_in, g_out, g_w_ff1, g_w_ff2, s_own = _rs_gather(finals, halves, [False] * 5 + [True], name="rs_gather_cores")
    srows = spad // (128 * NCH)
    (s_sum,) = _gather_weights([s_own.reshape(NCH * srows, 128)], [(srows, 128)], ["row"], name="gather_small_grads")
    s_sum = s_sum.reshape(-1)
    so = [sum(ssz[:k]) for k in range(len(ssz))]
    sm = [s_sum[o:o + n] for o, n in zip(so, ssz)]
    (dmod_s, g_mix0, g_mix1, g_mlp0, g_mlp1, g_nf, g_d, g_bg, g_gn, g_wg2, g_bbre, g_bbim, g_cre, g_cimn, g_dl) = sm
    g_b_ada = dmod_s.reshape(2, 6 * D)

    G = D // S5_H
    _, disc_vjp = jax.vjp(_s5_disc, s5_a_re[0], s5_a_im[0], s5_log_dt[0], s5_b_re[0], s5_b_im[0])
    g_dl = g_dl.reshape(nbk, 2, GPB, S5_P)
    ct = (g_dl[:, 0].reshape(G, S5_P), g_dl[:, 1].reshape(G, S5_P),
          g_bbre.reshape(G, S5_H, S5_P).transpose(0, 2, 1), g_bbim.reshape(G, S5_H, S5_P).transpose(0, 2, 1))
    g_a_re, g_a_im, g_log_dt, g_b_re, g_b_im = disc_vjp(ct)
    g_c_re = g_cre.reshape(G, S5_H, S5_P)
    g_c_im = -g_cimn.reshape(G, S5_H, S5_P)
    g_wg2_s = lax.dynamic_slice_in_dim(g_wg2.reshape(GATE_RANK, QK), chip * qk4, qk4, axis=1)
    g_bg_s = lax.dynamic_slice_in_dim(g_bg.reshape(1, QK), chip * qk4, qk4, axis=1)
    g_gn_s = lax.dynamic_slice_in_dim(g_gn.reshape(1, D), chip * (D // NCH), D // NCH, axis=1)

    grads = dict(
        w_ada=g_w_ada, b_ada=g_b_ada, norm_mix=jnp.stack([g_mix0, g_mix1]), norm_mlp=jnp.stack([g_mlp0, g_mlp1]),
        s5_a_re=g_a_re[None], s5_a_im=g_a_im[None], s5_log_dt=g_log_dt[None], s5_b_re=g_b_re[None], s5_b_im=g_b_im[None],
        s5_c_re=g_c_re[None], s5_c_im=g_c_im[None], s5_d=g_d[None], s5_w_glu=g_glu,
        gla_w_in=g_in, gla_w_gate2=g_wg2_s[None], gla_b_gate=g_bg_s, gla_g_norm=g_gn_s,
        gla_w_out=g_out, w_ff1=g_w_ff1, w_ff2=g_w_ff2, norm_final=g_nf)

    names = list(grads)
    large = ("w_ada", "s5_w_glu", "gla_w_in", "gla_w_out", "w_ff1", "w_ff2")
    delta, new_m, new_v = {}, {}, {}
    for nm in large:
        delta[nm], new_m[nm], new_v[nm] = _adamw(args[nm], grads[nm], args["m_" + nm], args["v_" + nm], name=f"adamw_{nm}")
    grads = {nm: grads[nm].reshape(args[nm].shape) for nm in names}
    for nm in names:
        if nm not in large:
            shp = args[nm].shape
            as2d = (1, -1) if len(shp) == 1 else shp
            outs = _adamw_whole(*[t.reshape(as2d) for t in (args[nm], grads[nm], args["m_" + nm], args["v_" + nm])],
                                name=f"adamw_{nm}")
            delta[nm], new_m[nm], new_v[nm] = (t.reshape(shp) for t in outs)
    return (loss, grad_x, *[grads[n] for n in names], *[delta[n] for n in names], *[new_m[n] for n in names],
            *[new_v[n] for n in names])
```

```python
import math

import jax
import jax.numpy as jnp
from jax import lax
from jax.experimental import pallas as pl
from jax.experimental.pallas import tpu as pltpu

F32 = jnp.float32
BF16 = jnp.bfloat16
MESH = pl.DeviceIdType.MESH

EPS = 1e-6
CHUNK = 64
GLA_NB = 4
S5_H = 16
S5_P = 64
GPB = 8
NSEG = 8
HEADS = 4
GATE_RANK = 16
GATE_TAU = 16.0
NCH = 4
LR, B1, B2, AEPS, WD, ASTEP = 0.001, 0.9, 0.999, 1e-08, 0.01, 10
VMEM_LIMIT = 56 << 20

MASK_CHIPS = ((1, 0, 0), (0, 1, 0), (1, 1, 0))
MASK_ALL = ((0, 0, 1), (0, 1, 0), (0, 1, 1), (1, 0, 0), (1, 0, 1), (1, 1, 0), (1, 1, 1))


def _params(*sem):
    return pltpu.CompilerParams(dimension_semantics=sem or None, vmem_limit_bytes=VMEM_LIMIT)


def _tile_rows(rows, cap=512):
    best = 8
    for t in range(8, cap + 1, 8):
        if rows % t == 0:
            best = t
    return best


def _whole(shape):
    return pl.BlockSpec(shape, lambda i, _n=len(shape): (0,) * _n)


def _matmul(a, b, *, name, ta=False, tb=False, tm=1024, tn=1024, tk=2048, out_dtypes=(F32,),
            a_fn=None, epi=None, epi_ins=(), col_shards=1):
    M, K = (a.shape[1], a.shape[0]) if ta else a.shape
    N = b.shape[0] if tb else b.shape[1]
    tm, tn, tk = min(tm, M), min(tn, N), min(tk, K)
    assert M % tm == 0 and N % tn == 0 and K % tk == 0, (name, M, N, K)
    nk = K // tk
    ne = len(epi_ins)
    dn = (((0 if ta else 1,), (1 if tb else 0,)), ((), ()))

    def body(a_ref, b_ref, *rest):
        e_refs, o_refs, acc = rest[:ne], rest[ne:-1], rest[-1]
        k = pl.program_id(2)

        @pl.when(k == 0)
        def _():
            acc[...] = jnp.zeros_like(acc)

        at = a_ref[...]
        if a_fn is not None:
            at = a_fn(at)
        acc[...] += lax.dot_general(at.astype(BF16), b_ref[...].astype(BF16), dn, preferred_element_type=F32)

        @pl.when(k == nk - 1)
        def _():
            outs = (acc[...],) if epi is None else epi(acc[...], *[r[...] for r in e_refs])
            for r, o in zip(o_refs, outs):
                r[...] = o.astype(r.dtype)

    a_spec = pl.BlockSpec((tk, tm), lambda i, j, k: (k, i)) if ta else pl.BlockSpec((tm, tk), lambda i, j, k: (i, k))
    b_spec = pl.BlockSpec((tn, tk), lambda i, j, k: (j, k)) if tb else pl.BlockSpec((tk, tn), lambda i, j, k: (k, j))
    o_spec = pl.BlockSpec((tm, tn), lambda i, j, k: (i, j))
    if col_shards > 1:
        per = N // col_shards // tn
        assert ne == 0 and per * tn * col_shards == N
        w_spec = pl.BlockSpec((None, tm, tn), lambda i, j, k: (j // per, i, j % per))
        o_shape = (col_shards, M, N // col_shards)
    else:
        w_spec, o_shape = o_spec, (M, N)
    outs = pl.pallas_call(
        body, name=name, grid=(M // tm, N // tn, nk),
        in_specs=[a_spec, b_spec] + [o_spec] * ne,
        out_specs=[w_spec] * len(out_dtypes),
        out_shape=[jax.ShapeDtypeStruct(o_shape, d) for d in out_dtypes],
        scratch_shapes=[pltpu.VMEM((tm, tn), F32)],
        compiler_params=_params("parallel", "parallel", "arbitrary"),
    )(a, b, *epi_ins)
    return outs[0] if len(outs) == 1 else outs


def _rows(fn, rows_in, vecs_in, rows_out, acc_out, *, name, tm=256):
    L = rows_in[0].shape[0]
    tm = min(tm, L)
    assert L % tm == 0
    nr, nv, no, na = len(rows_in), len(vecs_in), len(rows_out), len(acc_out)

    def body(*refs):
        rin, vin = refs[:nr], refs[nr:nr + nv]
        rout, aout = refs[nr + nv:nr + nv + no], refs[nr + nv + no:]
        outs = fn(*[r[...] for r in rin], *[v[...] for v in vin])
        for r, o in zip(rout, outs[:no]):
            r[...] = o.astype(r.dtype)
        if na:
            @pl.when(pl.program_id(0) == 0)
            def _():
                for r in aout:
                    r[...] = jnp.zeros_like(r)

            for r, o in zip(aout, outs[no:]):
                r[...] += o

    outs = pl.pallas_call(
        body, name=name, grid=(L // tm,),
        in_specs=[pl.BlockSpec((tm, r.shape[1]), lambda i: (i, 0)) for r in rows_in] + [_whole(v.shape) for v in vecs_in],
        out_specs=[pl.BlockSpec((tm, c), lambda i: (i, 0)) for c, _ in rows_out] + [_whole(s) for s in acc_out],
        out_shape=[jax.ShapeDtypeStruct((L, c), d) for c, d in rows_out] + [jax.ShapeDtypeStruct(s, F32) for s in acc_out],
        compiler_params=_params("arbitrary"),
    )(*rows_in, *vecs_in)
    return outs


def _rsum(t):
    return jnp.sum(t, axis=0, keepdims=True)


def _norm_mod(x, g, sc, sh):
    rs = lax.rsqrt(jnp.mean(x * x, axis=-1, keepdims=True) + EPS)
    return x * rs * g * (1.0 + sc) + sh


def _norm_mod_bwd(x, dh, g, sc):
    rs = lax.rsqrt(jnp.mean(x * x, axis=-1, keepdims=True) + EPS)
    xh = x * rs
    dn = dh * (1.0 + sc)
    dxh = dn * g
    dx = rs * (dxh - xh * jnp.mean(dxh * xh, axis=-1, keepdims=True))
    return dx, _rsum(dh), _rsum(dh * xh * g), _rsum(dn * xh)


def _sigmoid(x):
    return jax.nn.sigmoid(x)


def _gelu(y):
    return jax.nn.gelu(y, approximate=True)


def _gelu_grad(y):
    c = math.sqrt(2.0 / math.pi)
    t = jnp.tanh(c * (y + 0.044715 * y * y * y))
    return 0.5 * (1.0 + t) + 0.5 * y * (1.0 - t * t) * c * (1.0 + 3.0 * 0.044715 * y * y)


def _s5_tables(lb_re, lb_im, bb_re, bb_im, c_re, c_im, seg_len):
    G = lb_re.shape[0]
    nb = G // GPB
    eye = jnp.eye(GPB, dtype=F32)

    def bdiag(t):
        a, b = t.shape[1:]
        t = t.reshape(nb, GPB, a, b)
        return (t[:, :, :, None, :] * eye[None, :, None, :, None]).reshape(nb, GPB * a, GPB * b)

    bbd = jnp.concatenate([bdiag(bb_re.transpose(0, 2, 1)), bdiag(bb_im.transpose(0, 2, 1))], axis=2)
    cbd = jnp.concatenate([bdiag(c_re.transpose(0, 2, 1)), -bdiag(c_im.transpose(0, 2, 1))], axis=1)

    def lanes(re, im):
        t = jnp.concatenate([re.reshape(nb, GPB * S5_P), im.reshape(nb, GPB * S5_P)], axis=1)
        return jnp.repeat(t, NSEG, axis=0)

    tr, ti = lb_re, lb_im
    for _ in range(int(math.log2(seg_len))):
        tr, ti = tr * tr - ti * ti, 2.0 * tr * ti
    return dict(bbd=bbd.astype(BF16), bbdT=bbd.transpose(0, 2, 1).astype(BF16), cbd=cbd.astype(BF16),
                cbdT=cbd.transpose(0, 2, 1).astype(BF16), lam=lanes(lb_re, lb_im), lamT=lanes(tr, ti))


def _s5_untable(acc):
    nb = acc.shape[0]
    t = acc.reshape(nb, GPB, S5_H, 2, GPB, S5_P)
    d = jnp.diagonal(t, axis1=1, axis2=4)
    d = d.transpose(0, 4, 2, 1, 3).reshape(nb * GPB, 2, S5_H, S5_P)
    return d[:, 0], d[:, 1]


def _s5_fix(ends, lamT, *, reverse, name):
    nrow = ends.shape[0]
    half = ends.shape[1] // 2

    def body(e_ref, t_ref, o_ref):
        for gb in range(nrow // NSEG):
            r0 = gb * NSEG
            tr, ti = t_ref[r0:r0 + 1, :half], t_ref[r0:r0 + 1, half:]
            cr = jnp.zeros((1, half), F32)
            ci = jnp.zeros((1, half), F32)
            order = range(NSEG - 1, -1, -1) if reverse else range(NSEG)
            for n, s in enumerate(order):
                if n > 0:
                    p = s + 1 if reverse else s - 1
                    er, ei = e_ref[r0 + p:r0 + p + 1, :half], e_ref[r0 + p:r0 + p + 1, half:]
                    if reverse:
                        cr, ci = tr * cr + ti * ci + er, tr * ci - ti * cr + ei
                    else:
                        cr, ci = tr * cr - ti * ci + er, tr * ci + ti * cr + ei
                o_ref[r0 + s:r0 + s + 1, :half] = cr
                o_ref[r0 + s:r0 + s + 1, half:] = ci

    return pl.pallas_call(body, name=name, out_shape=jax.ShapeDtypeStruct(ends.shape, F32),
                          compiler_params=_params())(ends, lamT)


def _s5_fwd(up, carry_in, tb, dvec, *, emit, name, R=256):
    L, D = up.shape
    R = min(R, L)
    nb, ta, ngb = L // R, R // NSEG, D // 128
    SW = GPB * S5_P
    crows = ngb * NSEG

    def body(u_ref, cin_ref, lam_ref, b_ref, c_ref, d_ref, *rest):
        if emit:
            y_ref, z_ref, ck_ref, carry, xbuf = rest
        else:
            cout_ref, carry, xbuf = rest
        i = pl.program_id(0)

        @pl.when(i == 0)
        def _():
            carry[...] = cin_ref[...]

        if emit:
            ck_ref[0] = carry[...]
        for gb in range(ngb):
            cols = slice(gb * 128, (gb + 1) * 128)
            rws = slice(gb * NSEG, (gb + 1) * NSEG)
            ug = u_ref[:, cols]
            xbuf[...] = jnp.dot(ug.astype(BF16), b_ref[gb], preferred_element_type=F32)
            lr, li = lam_ref[rws, :SW], lam_ref[rws, SW:]

            def step(a, c, lr=lr, li=li):
                cr, ci = c
                o = pl.multiple_of(a * NSEG, NSEG)
                nr = lr * cr - li * ci + xbuf[pl.ds(o, NSEG), :SW]
                ni = lr * ci + li * cr + xbuf[pl.ds(o, NSEG), SW:]
                if emit:
                    xbuf[pl.ds(o, NSEG), :SW] = nr
                    xbuf[pl.ds(o, NSEG), SW:] = ni
                return nr, ni

            cr, ci = lax.fori_loop(0, ta, step, (carry[rws, :SW], carry[rws, SW:]), unroll=2)
            carry[rws, :SW] = cr
            carry[rws, SW:] = ci
            if emit:
                y = jnp.dot(xbuf[...].astype(BF16), c_ref[gb], preferred_element_type=F32) + d_ref[:, cols] * ug
                y_ref[:, cols] = y
                z_ref[:, cols] = _gelu(y).astype(BF16)
        if not emit:
            cout_ref[...] = carry[...]

    rowblk = pl.BlockSpec((R, D), lambda i: (i, 0))
    if emit:
        out_shape = [jax.ShapeDtypeStruct((L, D), F32), jax.ShapeDtypeStruct((L, D), BF16),
                     jax.ShapeDtypeStruct((nb, crows, 2 * SW), F32)]
        out_specs = [rowblk, rowblk, pl.BlockSpec((1, crows, 2 * SW), lambda i: (i, 0, 0))]
    else:
        out_shape = [jax.ShapeDtypeStruct((crows, 2 * SW), F32)]
        out_specs = [_whole((crows, 2 * SW))]
    return pl.pallas_call(
        body, name=name, grid=(nb,),
        in_specs=[rowblk, _whole(carry_in.shape), _whole(tb["lam"].shape), _whole(tb["bbd"].shape),
                  _whole(tb["cbd"].shape), _whole(dvec.shape)],
        out_specs=out_specs, out_shape=out_shape,
        scratch_shapes=[pltpu.VMEM((crows, 2 * SW), F32), pltpu.VMEM((R, 2 * SW), F32)],
        compiler_params=_params("arbitrary"),
    )(up, carry_in, tb["lam"], tb["bbd"], tb["cbd"], dvec)


def _s5_bwd(up, y, dz, ck, gcarry_in, tb, dvec, *, emit, name, R=256):
    L, D = up.shape
    R = min(R, L)
    nb, ta, ngb = L // R, R // NSEG, D // 128
    SW = GPB * S5_P
    crows = ngb * NSEG

    def body(u_ref, y_ref, dz_ref, ck_ref, gin_ref, lam_ref, b_ref, bt_ref, ct_ref, d_ref, *rest):
        if emit:
            du_ref, db_ref, dc_ref, dl_ref, dd_ref, gcarry, xbuf, gbuf, dybuf = rest
        else:
            gout_ref, gcarry, gbuf, dybuf = rest
        i = pl.program_id(0)

        @pl.when(i == 0)
        def _():
            gcarry[...] = gin_ref[...]
            if emit:
                db_ref[...] = jnp.zeros_like(db_ref)
                dc_ref[...] = jnp.zeros_like(dc_ref)
                dl_ref[...] = jnp.zeros_like(dl_ref)
                dd_ref[...] = jnp.zeros_like(dd_ref)

        dybuf[...] = dz_ref[...] * _gelu_grad(y_ref[...])
        for gb in range(ngb):
            cols = slice(gb * 128, (gb + 1) * 128)
            rws = slice(gb * NSEG, (gb + 1) * NSEG)
            dyg = dybuf[:, cols]
            lr, li = lam_ref[rws, :SW], lam_ref[rws, SW:]
            gbuf[...] = jnp.dot(dyg.astype(BF16), ct_ref[gb], preferred_element_type=F32)
            if emit:
                ug = u_ref[:, cols]
                xbuf[0:NSEG, :] = ck_ref[0, rws, :]
                xbuf[NSEG:, :] = jnp.dot(ug.astype(BF16), b_ref[gb], preferred_element_type=F32)

                def fstep(a, c, lr=lr, li=li):
                    cr, ci = c
                    o = pl.multiple_of(a * NSEG + NSEG, NSEG)
                    nr = lr * cr - li * ci + xbuf[pl.ds(o, NSEG), :SW]
                    ni = lr * ci + li * cr + xbuf[pl.ds(o, NSEG), SW:]
                    xbuf[pl.ds(o, NSEG), :SW] = nr
                    xbuf[pl.ds(o, NSEG), SW:] = ni
                    return nr, ni

                lax.fori_loop(0, ta, fstep, (xbuf[0:NSEG, :SW], xbuf[0:NSEG, SW:]), unroll=2)

            def rstep(k, c, lr=lr, li=li):
                o = pl.multiple_of((ta - 1 - k) * NSEG, NSEG)
                gr_n, gi_n = c[0], c[1]
                gr = gbuf[pl.ds(o, NSEG), :SW] + lr * gr_n + li * gi_n
                gi = gbuf[pl.ds(o, NSEG), SW:] - li * gr_n + lr * gi_n
                if not emit:
                    return gr, gi
                gbuf[pl.ds(o, NSEG), :SW] = gr
                gbuf[pl.ds(o, NSEG), SW:] = gi
                xr, xi = xbuf[pl.ds(o, NSEG), :SW], xbuf[pl.ds(o, NSEG), SW:]
                return gr, gi, c[2] + gr * xr + gi * xi, c[3] + gi * xr - gr * xi

            c0 = (gcarry[rws, :SW], gcarry[rws, SW:])
            if emit:
                c0 = c0 + (jnp.zeros((NSEG, SW), F32), jnp.zeros((NSEG, SW), F32))
            cf = lax.fori_loop(0, ta, rstep, c0, unroll=2)
            gcarry[rws, :SW] = cf[0]
            gcarry[rws, SW:] = cf[1]
            if emit:
                dl_ref[rws, :SW] += cf[2]
                dl_ref[rws, SW:] += cf[3]
                gb16 = gbuf[...].astype(BF16)
                du_ref[:, cols] = jnp.dot(gb16, bt_ref[gb], preferred_element_type=F32) + d_ref[:, cols] * dyg
                tn = (((0,), (0,)), ((), ()))
                db_ref[gb] += lax.dot_general(ug.astype(BF16), gb16, tn, preferred_element_type=F32)
                dc_ref[gb] += lax.dot_general(dyg.astype(BF16), xbuf[NSEG:, :].astype(BF16), tn,
                                              preferred_element_type=F32)
                dd_ref[:, cols] += _rsum(dyg * ug)
        if not emit:
            gout_ref[...] = gcarry[...]

    rev = pl.BlockSpec((R, D), lambda i: (nb - 1 - i, 0))
    acc3 = (ngb, 128, 2 * SW)
    if emit:
        out_shape = [jax.ShapeDtypeStruct((L, D), F32), jax.ShapeDtypeStruct(acc3, F32), jax.ShapeDtypeStruct(acc3, F32),
                     jax.ShapeDtypeStruct((crows, 2 * SW), F32), jax.ShapeDtypeStruct((1, D), F32)]
        out_specs = [rev, _whole(acc3), _whole(acc3), _whole((crows, 2 * SW)), _whole((1, D))]
        scratch = [pltpu.VMEM((crows, 2 * SW), F32), pltpu.VMEM((R + NSEG, 2 * SW), F32),
                   pltpu.VMEM((R, 2 * SW), F32), pltpu.VMEM((R, D), F32)]
    else:
        out_shape = [jax.ShapeDtypeStruct((crows, 2 * SW), F32)]
        out_specs = [_whole((crows, 2 * SW))]
        scratch = [pltpu.VMEM((crows, 2 * SW), F32), pltpu.VMEM((R, 2 * SW), F32), pltpu.VMEM((R, D), F32)]
    return pl.pallas_call(
        body, name=name, grid=(nb,),
        in_specs=[rev, rev, rev, pl.BlockSpec((1, crows, 2 * SW), lambda i: (nb - 1 - i, 0, 0)),
                  _whole(gcarry_in.shape), _whole(tb["lam"].shape), _whole(tb["bbd"].shape),
                  _whole(tb["bbdT"].shape), _whole(tb["cbdT"].shape), _whole(dvec.shape)],
        out_specs=out_specs, out_shape=out_shape, scratch_shapes=scratch,
        compiler_params=_params("arbitrary"),
    )(up, y, dz, ck, gcarry_in, tb["lam"], tb["bbd"], tb["bbdT"], tb["cbdT"], dvec)


S5_R = 256


def _s5_carries(ends, first, t_ref, rws, SW, cfx, *, reverse):
    er, ei = ends
    tr, ti = t_ref[rws, :SW][0:1], t_ref[rws, SW:][0:1]
    cr, ci = first
    order = range(NSEG - 1, -1, -1) if reverse else range(NSEG)
    for n, s in enumerate(order):
        if n > 0:
            p = s + 1 if reverse else s - 1
            if reverse:
                cr, ci = tr * cr + ti * ci + er[p:p + 1], tr * ci - ti * cr + ei[p:p + 1]
            else:
                cr, ci = tr * cr - ti * ci + er[p:p + 1], tr * ci + ti * cr + ei[p:p + 1]
        cfx[s:s + 1, :SW] = cr
        cfx[s:s + 1, SW:] = ci


def _s5_fwd2(up, tb, dvec, *, name):
    L, D = up.shape
    R = min(S5_R, L)
    nb, ta, ngb = L // R, R // NSEG, D // 128
    SW = GPB * S5_P
    crows = ngb * NSEG

    def body(u_ref, lam_ref, t_ref, b_ref, c_ref, d_ref, y_ref, z_ref, ck_ref, carry, xbuf, cfx):
        @pl.when(pl.program_id(0) == 0)
        def _():
            carry[...] = jnp.zeros_like(carry)

        zero = jnp.zeros((NSEG, SW), F32)
        for gb in range(ngb):
            cols = slice(gb * 128, (gb + 1) * 128)
            rws = slice(gb * NSEG, (gb + 1) * NSEG)
            ug = u_ref[:, cols]
            xbuf[...] = jnp.dot(ug.astype(BF16), b_ref[gb], preferred_element_type=F32)
            lr, li = lam_ref[rws, :SW], lam_ref[rws, SW:]

            def step(a, c, lr=lr, li=li, store=False):
                cr, ci = c
                o = pl.multiple_of(a * NSEG, NSEG)
                nr = lr * cr - li * ci + xbuf[pl.ds(o, NSEG), :SW]
                ni = lr * ci + li * cr + xbuf[pl.ds(o, NSEG), SW:]
                if store:
                    xbuf[pl.ds(o, NSEG), :SW] = nr
                    xbuf[pl.ds(o, NSEG), SW:] = ni
                return nr, ni

            ends = lax.fori_loop(0, ta, step, (zero, zero), unroll=2)
            prev = (carry[rws, :SW][NSEG - 1:NSEG], carry[rws, SW:][NSEG - 1:NSEG])
            _s5_carries(ends, prev, t_ref, rws, SW, cfx, reverse=False)
            ck_ref[0, rws, :] = cfx[...]
            cr, ci = lax.fori_loop(0, ta, lambda a, c, st=step: st(a, c, store=True), (cfx[:, :SW], cfx[:, SW:]), unroll=2)
            carry[rws, :SW] = cr
            carry[rws, SW:] = ci
            y = jnp.dot(xbuf[...].astype(BF16), c_ref[gb], preferred_element_type=F32) + d_ref[:, cols] * ug
            y_ref[:, cols] = y
            z_ref[:, cols] = _gelu(y).astype(BF16)

    rowblk = pl.BlockSpec((R, D), lambda i: (i, 0))
    return pl.pallas_call(
        body, name=name, grid=(nb,),
        in_specs=[rowblk, _whole(tb["lam"].shape), _whole(tb["lamT"].shape), _whole(tb["bbd"].shape),
                  _whole(tb["cbd"].shape), _whole(dvec.shape)],
        out_specs=[rowblk, rowblk, pl.BlockSpec((1, crows, 2 * SW), lambda i: (i, 0, 0))],
        out_shape=[jax.ShapeDtypeStruct((L, D), F32), jax.ShapeDtypeStruct((L, D), BF16),
                   jax.ShapeDtypeStruct((nb, crows, 2 * SW), F32)],
        scratch_shapes=[pltpu.VMEM((crows, 2 * SW), F32), pltpu.VMEM((R, 2 * SW), F32), pltpu.VMEM((NSEG, 2 * SW), F32)],
        compiler_params=_params("arbitrary"),
    )(up, tb["lam"], tb["lamT"], tb["bbd"], tb["cbd"], dvec)


def _s5_bwd2(up, y, dz, ck, tb, dvec, *, name):
    L, D = up.shape
    R = min(S5_R, L)
    nb, ta, ngb = L // R, R // NSEG, D // 128
    SW = GPB * S5_P
    crows = ngb * NSEG

    def body(u_ref, y_ref, dz_ref, ck_ref, lam_ref, t_ref, b_ref, bt_ref, ct_ref, d_ref,
             du_ref, db_ref, dc_ref, dl_ref, dd_ref, gcarry, xbuf, gbuf, dybuf, cfx):
        @pl.when(pl.program_id(0) == 0)
        def _():
            gcarry[...] = jnp.zeros_like(gcarry)
            db_ref[...] = jnp.zeros_like(db_ref)
            dc_ref[...] = jnp.zeros_like(dc_ref)
            dl_ref[...] = jnp.zeros_like(dl_ref)
            dd_ref[...] = jnp.zeros_like(dd_ref)

        zero = jnp.zeros((NSEG, SW), F32)
        dybuf[...] = dz_ref[...] * _gelu_grad(y_ref[...])
        for gb in range(ngb):
            cols = slice(gb * 128, (gb + 1) * 128)
            rws = slice(gb * NSEG, (gb + 1) * NSEG)
            dyg = dybuf[:, cols]
            ug = u_ref[:, cols]
            lr, li = lam_ref[rws, :SW], lam_ref[rws, SW:]
            gbuf[...] = jnp.dot(dyg.astype(BF16), ct_ref[gb], preferred_element_type=F32)
            xbuf[0:NSEG, :] = ck_ref[0, rws, :]
            xbuf[NSEG:, :] = jnp.dot(ug.astype(BF16), b_ref[gb], preferred_element_type=F32)

            def fstep(a, c, lr=lr, li=li):
                cr, ci = c
                o = pl.multiple_of(a * NSEG + NSEG, NSEG)
                nr = lr * cr - li * ci + xbuf[pl.ds(o, NSEG), :SW]
                ni = lr * ci + li * cr + xbuf[pl.ds(o, NSEG), SW:]
                xbuf[pl.ds(o, NSEG), :SW] = nr
                xbuf[pl.ds(o, NSEG), SW:] = ni
                return nr, ni

            lax.fori_loop(0, ta, fstep, (xbuf[0:NSEG, :SW], xbuf[0:NSEG, SW:]), unroll=2)

            def rstep(k, c, lr=lr, li=li, store=False):
                o = pl.multiple_of((ta - 1 - k) * NSEG, NSEG)
                gr_n, gi_n = c[0], c[1]
                gr = gbuf[pl.ds(o, NSEG), :SW] + lr * gr_n + li * gi_n
                gi = gbuf[pl.ds(o, NSEG), SW:] - li * gr_n + lr * gi_n
                if not store:
                    return gr, gi
                gbuf[pl.ds(o, NSEG), :SW] = gr
                gbuf[pl.ds(o, NSEG), SW:] = gi
                xr, xi = xbuf[pl.ds(o, NSEG), :SW], xbuf[pl.ds(o, NSEG), SW:]
                return gr, gi, c[2] + gr * xr + gi * xi, c[3] + gi * xr - gr * xi

            gends = lax.fori_loop(0, ta, rstep, (zero, zero), unroll=2)
            nxt = (gcarry[rws, :SW][0:1], gcarry[rws, SW:][0:1])
            _s5_carries(gends, nxt, t_ref, rws, SW, cfx, reverse=True)
            cf = lax.fori_loop(0, ta, lambda k, c, st=rstep: st(k, c, store=True),
                               (cfx[:, :SW], cfx[:, SW:], zero, zero), unroll=2)
            gcarry[rws, :SW] = cf[0]
            gcarry[rws, SW:] = cf[1]
            dl_ref[rws, :SW] += cf[2]
            dl_ref[rws, SW:] += cf[3]
            gb16 = gbuf[...].astype(BF16)
            du_ref[:, cols] = jnp.dot(gb16, bt_ref[gb], preferred_element_type=F32) + d_ref[:, cols] * dyg
            db_ref[gb] += lax.dot_general(ug.astype(BF16), gb16, TN, preferred_element_type=F32)
            dc_ref[gb] += lax.dot_general(dyg.astype(BF16), xbuf[NSEG:, :].astype(BF16), TN, preferred_element_type=F32)
            dd_ref[:, cols] += _rsum(dyg * ug)

    rev = pl.BlockSpec((R, D), lambda i: (nb - 1 - i, 0))
    acc3 = (ngb, 128, 2 * SW)
    return pl.pallas_call(
        body, name=name, grid=(nb,),
        in_specs=[rev, rev, rev, pl.BlockSpec((1, crows, 2 * SW), lambda i: (nb - 1 - i, 0, 0)),
                  _whole(tb["lam"].shape), _whole(tb["lamT"].shape), _whole(tb["bbd"].shape),
                  _whole(tb["bbdT"].shape), _whole(tb["cbdT"].shape), _whole(dvec.shape)],
        out_specs=[rev, _whole(acc3), _whole(acc3), _whole((crows, 2 * SW)), _whole((1, D))],
        out_shape=[jax.ShapeDtypeStruct((L, D), F32), jax.ShapeDtypeStruct(acc3, F32), jax.ShapeDtypeStruct(acc3, F32),
                   jax.ShapeDtypeStruct((crows, 2 * SW), F32), jax.ShapeDtypeStruct((1, D), F32)],
        scratch_shapes=[pltpu.VMEM((crows, 2 * SW), F32), pltpu.VMEM((R + NSEG, 2 * SW), F32),
                        pltpu.VMEM((R, 2 * SW), F32), pltpu.VMEM((R, D), F32), pltpu.VMEM((NSEG, 2 * SW), F32)],
        compiler_params=_params("arbitrary"),
    )(up, y, dz, ck, tb["lam"], tb["lamT"], tb["bbd"], tb["bbdT"], tb["cbdT"], dvec)


NN = (((1,), (0,)), ((), ()))
TN = (((0,), (0,)), ((), ()))
NT = (((1,), (1,)), ((), ()))


def _dot3(lhs, rhs, dn, split):
    x = rhs if split == "rhs" else lhs
    hi = x.astype(BF16)
    r1 = x - hi.astype(F32)
    mid = r1.astype(BF16)
    lo = (r1 - mid.astype(F32)).astype(BF16)
    out = None
    for part in (hi, mid, lo):
        ops = (lhs, part) if split == "rhs" else (part, rhs)
        t = lax.dot_general(ops[0], ops[1], dn, preferred_element_type=F32)
        out = t if out is None else out + t
    return out


def _log_sigmoid(x):
    return jnp.minimum(x, 0.0) - jnp.log(1.0 + jnp.exp(-jnp.abs(x)))


def _gla_gates(p, wg_ref, bg_ref, QK):
    C = p.shape[0]
    glr = p[:, 6 * QK:6 * QK + 128].astype(BF16)
    gpre = jnp.dot(glr, wg_ref[...], preferred_element_type=F32) + bg_ref[...]
    la = _log_sigmoid(gpre) * (1.0 / GATE_TAU)
    row = lax.broadcasted_iota(jnp.int32, (C, C), 0)
    col = lax.broadcasted_iota(jnp.int32, (C, C), 1)
    gc = _dot3((row >= col).astype(BF16), la, NN, "rhs")
    ge = gc[C - 1:C, :]
    w = jnp.exp(ge - gc)
    return glr, gpre, la, ge, w


def _gla_fwd(proj, wg2p, bg, gn, *, name):
    L = proj.shape[0]
    QK = wg2p.shape[1]
    DK, DV = QK // HEADS, 2 * QK // HEADS
    nC = L // CHUNK
    NB = min(GLA_NB, nC)
    assert nC % NB == 0
    scale = DK ** -0.5

    def body(p_ref, wg_ref, bg_ref, gn_ref, og_ref, s_ref, sst):
        @pl.when(pl.program_id(0) == 0)
        def _():
            sst[...] = jnp.zeros_like(sst)

        ones = jnp.ones((CHUNK, DV), BF16)
        for cc in range(NB):
            rows = slice(cc * CHUNK, (cc + 1) * CHUNK)
            p = p_ref[rows, :]
            _, _, la, _, w = _gla_gates(p, wg_ref, bg_ref, QK)
            for h in range(HEADS):
                ks, vs = slice(h * DK, (h + 1) * DK), slice(h * DV, (h + 1) * DV)
                q = p[:, h * DK:(h + 1) * DK] * scale
                kd = p[:, QK + h * DK:QK + (h + 1) * DK] * w[:, ks]
                v = p[:, 2 * QK + h * DV:2 * QK + (h + 1) * DV]
                r = p[:, 4 * QK + h * DV:4 * QK + (h + 1) * DV]
                dec = jnp.exp(_dot3(la[:, ks], ones, TN, "lhs"))
                kv = lax.dot_general(kd.astype(BF16), v.astype(BF16), TN, preferred_element_type=F32)
                S = dec * sst[ks, :] + kv
                sst[ks, :] = S
                s_ref[cc, ks, :] = S
                o = jnp.dot(q.astype(BF16), S.astype(BF16), preferred_element_type=F32)
                on = o * lax.rsqrt(jnp.mean(o * o, axis=-1, keepdims=True) + EPS)
                og_ref[rows, vs] = (on * gn_ref[:, vs] * (r * _sigmoid(r))).astype(BF16)

    RB = NB * CHUNK
    return pl.pallas_call(
        body, name=name, grid=(nC // NB,),
        in_specs=[pl.BlockSpec((RB, proj.shape[1]), lambda i: (i, 0)), _whole(wg2p.shape), _whole(bg.shape), _whole(gn.shape)],
        out_specs=[pl.BlockSpec((RB, 2 * QK), lambda i: (i, 0)), pl.BlockSpec((NB, QK, DV), lambda i: (i, 0, 0))],
        out_shape=[jax.ShapeDtypeStruct((L, 2 * QK), BF16), jax.ShapeDtypeStruct((nC, QK, DV), F32)],
        scratch_shapes=[pltpu.VMEM((QK, DV), F32)],
        compiler_params=_params("arbitrary"),
    )(proj, wg2p, bg, gn)


def _gla_bwd(proj, dog, states, wg2p, bg, gn, *, name):
    L, W = proj.shape
    QK = wg2p.shape[1]
    DK, DV = QK // HEADS, 2 * QK // HEADS
    nC = L // CHUNK
    NB = min(GLA_NB, nC)
    nB = nC // NB
    scale = DK ** -0.5

    def body(p_ref, dog_ref, sc_ref, sp_ref, wg_ref, bg_ref, gn_ref, dp_ref, dwg_ref, dbg_ref, dgn_ref, gst):
        i = pl.program_id(0)

        @pl.when(i == 0)
        def _():
            gst[...] = jnp.zeros_like(gst)
            dwg_ref[...] = jnp.zeros_like(dwg_ref)
            dbg_ref[...] = jnp.zeros_like(dbg_ref)
            dgn_ref[...] = jnp.zeros_like(dgn_ref)

        row = lax.broadcasted_iota(jnp.int32, (CHUNK, CHUNK), 0)
        col = lax.broadcasted_iota(jnp.int32, (CHUNK, CHUNK), 1)
        tri_u = (col >= row).astype(BF16)
        ones = jnp.ones((CHUNK, DV), BF16)
        ones8 = jnp.ones((8, DV), BF16)
        for cc in range(NB - 1, -1, -1):
            rows = slice(cc * CHUNK, (cc + 1) * CHUNK)
            p = p_ref[rows, :]
            glr, gpre, la, ge, w = _gla_gates(p, wg_ref, bg_ref, QK)
            dla_heads = []
            for h in range(HEADS):
                ks, vs = slice(h * DK, (h + 1) * DK), slice(h * DV, (h + 1) * DV)
                qs = p[:, h * DK:(h + 1) * DK] * scale
                k = p[:, QK + h * DK:QK + (h + 1) * DK]
                v = p[:, 2 * QK + h * DV:2 * QK + (h + 1) * DV]
                r = p[:, 4 * QK + h * DV:4 * QK + (h + 1) * DV]
                wh = w[:, ks]
                kd = k * wh
                S = sc_ref[cc, ks, :]
                if cc > 0:
                    Sp = sc_ref[cc - 1, ks, :]
                else:
                    Sp = jnp.where(i < nB - 1, sp_ref[0, ks, :], 0.0)
                o = jnp.dot(qs.astype(BF16), S.astype(BF16), preferred_element_type=F32)
                rs = lax.rsqrt(jnp.mean(o * o, axis=-1, keepdims=True) + EPS)
                on = o * rs
                sr = _sigmoid(r)
                dg = dog_ref[rows, vs]
                gnh = gn_ref[:, vs]
                dp_ref[rows, 4 * QK + h * DV:4 * QK + (h + 1) * DV] = (
                    dg * on * gnh * (sr * (1.0 + r * (1.0 - sr)))).astype(BF16)
                dt = dg * (r * sr)
                dgn_ref[:, vs] += _rsum(dt * on)
                don = dt * gnh
                do = (rs * (don - on * jnp.mean(don * on, axis=-1, keepdims=True))).astype(BF16)
                Gc = gst[ks, :] + lax.dot_general(qs.astype(BF16), do, TN, preferred_element_type=F32)
                G16 = Gc.astype(BF16)
                dp_ref[rows, h * DK:(h + 1) * DK] = (
                    lax.dot_general(do, S.astype(BF16), NT, preferred_element_type=F32) * scale).astype(BF16)
                dkd = lax.dot_general(v.astype(BF16), G16, NT, preferred_element_type=F32)
                dp_ref[rows, 2 * QK + h * DV:2 * QK + (h + 1) * DV] = jnp.dot(
                    kd.astype(BF16), G16, preferred_element_type=F32).astype(BF16)
                gst[ks, :] = jnp.exp(_dot3(la[:, ks], ones, TN, "lhs")) * Gc
                ddec = _dot3(ones8, Gc * Sp, NT, "rhs")[0:1, :]
                dp_ref[rows, QK + h * DK:QK + (h + 1) * DK] = (dkd * wh).astype(BF16)
                dww = dkd * kd
                dge = jnp.exp(ge[:, ks]) * ddec + _rsum(dww)
                dla_heads.append(dge - _dot3(tri_u, dww, NN, "rhs"))
            dla = jnp.concatenate(dla_heads, axis=1)
            dgpre = dla * (1.0 / GATE_TAU) * (1.0 - _sigmoid(gpre))
            d16 = dgpre.astype(BF16)
            dp_ref[rows, 6 * QK:6 * QK + 128] = lax.dot_general(d16, wg_ref[...], NT, preferred_element_type=F32).astype(BF16)
            dwg_ref[...] += lax.dot_general(glr, d16, TN, preferred_element_type=F32)
            dbg_ref[...] += _rsum(dgpre)

    RB = NB * CHUNK
    rev = lambda i: (nB - 1 - i, 0)
    return pl.pallas_call(
        body, name=name, grid=(nB,),
        in_specs=[pl.BlockSpec((RB, W), rev), pl.BlockSpec((RB, 2 * QK), rev),
                  pl.BlockSpec((NB, QK, DV), lambda i: (nB - 1 - i, 0, 0)),
                  pl.BlockSpec((1, QK, DV), lambda i: (jnp.maximum(NB * (nB - 1 - i) - 1, 0), 0, 0)),
                  _whole(wg2p.shape), _whole(bg.shape), _whole(gn.shape)],
        out_specs=[pl.BlockSpec((RB, W), rev), _whole((128, QK)), _whole((1, QK)), _whole((1, 2 * QK))],
        out_shape=[jax.ShapeDtypeStruct((L, W), BF16), jax.ShapeDtypeStruct((128, QK), F32),
                   jax.ShapeDtypeStruct((1, QK), F32), jax.ShapeDtypeStruct((1, 2 * QK), F32)],
        scratch_shapes=[pltpu.VMEM((QK, DV), F32)],
        compiler_params=_params("arbitrary"),
    )(proj, dog, states, states, wg2p, bg, gn)


def _coords():
    return lax.axis_index("x"), lax.axis_index("y"), lax.axis_index("c")


def _other_chips(x, y):
    return [(1 - x, y, 2 * (1 - x) + y), (x, 1 - y, 2 * x + 1 - y), (1 - x, 1 - y, 2 * (1 - x) + 1 - y)]


def _hbm_call(body, ins, out_shapes, n_sems, *, name, alias=False):
    any_spec = pl.BlockSpec(memory_space=pl.ANY)
    return pl.pallas_call(
        body, name=name, in_specs=[any_spec] * len(ins), out_specs=[any_spec] * len(out_shapes), out_shape=out_shapes,
        scratch_shapes=[pltpu.SemaphoreType.DMA((n,)) for n in n_sems],
        input_output_aliases={k: k for k in range(len(ins))} if alias else {},
    )(*ins)


def _exchange(src, masks, *, name):
    vary = [any(m[k] for m in masks) for k in range(3)]
    nslots = 2 ** sum(vary)
    n = len(masks)

    def slot(coords):
        s = 0
        for k in range(3):
            if vary[k]:
                s = s * 2 + coords[k]
        return s

    def body(src_ref, dst_ref, send_sems, recv_sems, loc_sem):
        me = _coords()
        mine = slot(me)
        loc = pltpu.make_async_copy(src_ref, dst_ref.at[mine], loc_sem.at[0])
        loc.start()
        copies = []
        for k, m in enumerate(masks):
            peer = tuple(1 - me[d] if m[d] else me[d] for d in range(3))
            cp = pltpu.make_async_remote_copy(src_ref=src_ref, dst_ref=dst_ref.at[mine], send_sem=send_sems.at[k],
                                              recv_sem=recv_sems.at[k], device_id=peer, device_id_type=MESH)
            cp.start()
            copies.append(cp)
        for cp in copies:
            cp.wait()
        loc.wait()

    return _hbm_call(body, [src], [jax.ShapeDtypeStruct((nslots,) + tuple(src.shape), src.dtype)], (n, n, 1), name=name)[0]


def _cast_into(t, lead, kind, chip, *, name, tm=256):
    r, cc = t.shape[-2:]
    tm = min(tm, r)
    nblk = r // tm
    if kind == "col":
        shp, o_spec = (r, NCH * cc), pl.BlockSpec((tm, cc), lambda i, s: (i, s[0]))
    elif kind == "row":
        shp, o_spec = (NCH * r, cc), pl.BlockSpec((tm, cc), lambda i, s: (s[0] * nblk + i, 0))
    else:
        shp, o_spec = (NCH, r, cc), pl.BlockSpec((None, tm, cc), lambda i, s: (s[0], i, 0))

    def body(s_ref, t_ref, o_ref):
        o_ref[...] = t_ref[...].astype(o_ref.dtype)

    return pl.pallas_call(
        body, name=name,
        grid_spec=pltpu.PrefetchScalarGridSpec(
            num_scalar_prefetch=1, grid=(nblk,),
            in_specs=[pl.BlockSpec((None, tm, cc), lambda i, s: (lead, i, 0))], out_specs=o_spec),
        out_shape=jax.ShapeDtypeStruct(shp, BF16), compiler_params=_params("parallel"),
    )(chip.reshape(1).astype(jnp.int32), t)


def _gather_weights(arrs, shard_shapes, kinds, *, name):
    n = len(arrs)

    def body(*refs):
        dst = refs[n:2 * n]
        send_sems, recv_sems = refs[2 * n:]
        x, y, c = _coords()
        chip = 2 * x + y
        others = _other_chips(x, y)
        sib = (x, y, 1 - c)

        def window(p, chip_id, cc):
            r, cols = shard_shapes[p]
            h = r // 2
            if kinds[p] == "col":
                return dst[p].at[pl.ds(cc * h, h), pl.ds(pl.multiple_of(chip_id * cols, 128), cols)]
            if kinds[p] == "row":
                return dst[p].at[pl.ds(chip_id * r + cc * h, h), :]
            return dst[p].at[chip_id, pl.ds(cc * h, h), :]

        def copy(p, k, win, to):
            return pltpu.make_async_remote_copy(src_ref=win, dst_ref=win, send_sem=send_sems.at[6 * p + k],
                                                recv_sem=recv_sems.at[6 * p + k], device_id=to, device_id_type=MESH)

        sends = []
        for p in range(n):
            for j, (ox, oy, _) in enumerate(others):
                cp = copy(p, j, window(p, chip, c), (ox, oy, c))
                cp.start()
                sends.append(cp)
        for j, (_, _, oc) in enumerate(others):
            for p in range(n):
                copy(p, j, window(p, oc, c), (x, y, c)).wait_recv()
                fw = copy(p, 3 + j, window(p, oc, c), sib)
                fw.start()
                sends.append(fw)
        for p in range(n):
            for j, (_, _, oc) in enumerate(others):
                copy(p, 3 + j, window(p, oc, 1 - c), sib).wait_recv()
        for cp in sends:
            cp.wait_send()

    outs = [jax.ShapeDtypeStruct(a.shape, a.dtype) for a in arrs]
    return _hbm_call(body, arrs, outs, (6 * n, 6 * n), name=name, alias=True)


HBM_SPEC = pl.BlockSpec(memory_space=pltpu.HBM)
SEM_SPEC = pl.BlockSpec(memory_space=pltpu.SEMAPHORE)
EFFECT = pltpu.SideEffectType.DATAFLOW_SIDE_EFFECTING


def _window(ref, shard_shape, kind, chip_id, cc):
    r, cols = shard_shape
    h = r // 2
    if kind == "col":
        return ref.at[pl.ds(cc * h, h), pl.ds(pl.multiple_of(chip_id * cols, 128), cols)]
    if kind == "row":
        return ref.at[pl.ds(chip_id * r + cc * h, h), :]
    return ref.at[chip_id, pl.ds(cc * h, h), :]


def _split_start(start, arrs, n_sems, *, name):
    n, ns = len(arrs), len(n_sems)

    def body(*refs):
        start(refs[:n], refs[n:n + ns])
        refs[-1][...] = jnp.zeros_like(refs[-1])

    outs = pl.pallas_call(
        body, name=name,
        out_shape=tuple([pltpu.SemaphoreType.DMA((k,)) for k in n_sems] + [pltpu.HBM(a.shape, a.dtype) for a in arrs]
                        + [jax.ShapeDtypeStruct((8, 128), F32)]),
        in_specs=[HBM_SPEC] * n, out_specs=tuple([SEM_SPEC] * ns + [HBM_SPEC] * n + [pl.BlockSpec(memory_space=pltpu.VMEM)]),
        input_output_aliases={k: ns + k for k in range(n)},
        compiler_params=pltpu.CompilerParams(has_side_effects=EFFECT),
    )(*[pltpu.with_memory_space_constraint(a, pltpu.HBM) for a in arrs])
    return list(outs[:ns]), list(outs[ns:ns + n]), outs[-1]


def _split_wait(wait, arrs, sems, after, *, name):
    n, ns = len(arrs), len(sems)

    def body(*refs):
        wait(refs[:n], refs[n:n + ns])

    return pl.pallas_call(
        body, name=name, out_shape=tuple(pltpu.HBM(a.shape, a.dtype) for a in arrs),
        in_specs=[HBM_SPEC] * n + [SEM_SPEC] * ns + [pl.BlockSpec(memory_space=pl.ANY)], out_specs=tuple([HBM_SPEC] * n),
        input_output_aliases={k: k for k in range(n)},
        compiler_params=pltpu.CompilerParams(has_side_effects=EFFECT),
    )(*arrs, *sems, after)


def _gw_copies(refs, send_sems, recv_sems, shard_shapes, kinds, outgoing):
    x, y, c = _coords()
    chip = 2 * x + y
    out = []
    for p in range(len(refs)):
        for j, (ox, oy, oc) in enumerate(_other_chips(x, y)):
            win = _window(refs[p], shard_shapes[p], kinds[p], chip if outgoing else oc, c)
            out.append(pltpu.make_async_remote_copy(
                src_ref=win, dst_ref=win, send_sem=send_sems.at[3 * p + j], recv_sem=recv_sems.at[3 * p + j],
                device_id=(ox, oy, c), device_id_type=MESH))
    return out


def _gw_start(arrs, shard_shapes, kinds, groups, *, name):
    def start(refs, sems):
        for g, idx in enumerate(groups):
            for cp in _gw_copies([refs[p] for p in idx], sems[2 * g], sems[2 * g + 1], [shard_shapes[p] for p in idx],
                                 [kinds[p] for p in idx], True):
                cp.start()

    n_sems = [3 * len(idx) for idx in groups for _ in range(2)]
    sems, thru, token = _split_start(start, arrs, n_sems, name=name)
    return [(sems[2 * g], sems[2 * g + 1]) for g in range(len(groups))], thru, token


def _gw_wait(arrs, shard_shapes, kinds, sem_pair, after, *, name):
    def wait(refs, sems):
        for cp in _gw_copies(refs, sems[0], sems[1], shard_shapes, kinds, True):
            cp.wait_send()
        for cp in _gw_copies(refs, sems[0], sems[1], shard_shapes, kinds, False):
            cp.wait_recv()

    return _split_wait(wait, arrs, list(sem_pair), after, name=name)


def _gw_forward(arrs, shard_shapes, kinds, *, name):
    n = len(arrs)

    def body(*refs):
        dst = refs[n:2 * n]
        send_sems, recv_sems = refs[2 * n:]
        x, y, c = _coords()
        sends = []
        for p in range(n):
            for j, (_, _, oc) in enumerate(_other_chips(x, y)):
                win = _window(dst[p], shard_shapes[p], kinds[p], oc, c)
                cp = pltpu.make_async_remote_copy(src_ref=win, dst_ref=win, send_sem=send_sems.at[3 * p + j],
                                                  recv_sem=recv_sems.at[3 * p + j], device_id=(x, y, 1 - c),
                                                  device_id_type=MESH)
                cp.start()
                sends.append(cp)
        for p in range(n):
            for j, (_, _, oc) in enumerate(_other_chips(x, y)):
                win = _window(dst[p], shard_shapes[p], kinds[p], oc, 1 - c)
                pltpu.make_async_remote_copy(src_ref=win, dst_ref=win, send_sem=send_sems.at[3 * p + j],
                                             recv_sem=recv_sems.at[3 * p + j], device_id=(x, y, 1 - c),
                                             device_id_type=MESH).wait_recv()
        for cp in sends:
            cp.wait_send()

    outs = [jax.ShapeDtypeStruct(a.shape, a.dtype) for a in arrs]
    return _hbm_call(body, arrs, outs, (3 * n, 3 * n), name=name, alias=True)


def _rs_chips_copies(parts, lands, send_sems, recv_sems):
    x, y, c = _coords()
    chip = 2 * x + y
    out = []
    for p in range(len(parts)):
        for j, (ox, oy, oc) in enumerate(_other_chips(x, y)):
            out.append(pltpu.make_async_remote_copy(
                src_ref=parts[p].at[oc], dst_ref=lands[p].at[chip], send_sem=send_sems.at[3 * p + j],
                recv_sem=recv_sems.at[3 * p + j], device_id=(ox, oy, c), device_id_type=MESH))
    return out


def _rs_chips_start(parts, *, name):
    n = len(parts)

    def start(refs, sems):
        for cp in _rs_chips_copies(refs[:n], refs[n:], sems[0], sems[1]):
            cp.start()

    lands = [lax.empty(t.shape, t.dtype) for t in parts]
    sems, thru, token = _split_start(start, list(parts) + lands, [3 * n, 3 * n], name=name)
    return (sems[0], sems[1]), thru[:n], thru[n:], token


def _rs_chips_wait(groups, after, *, name):
    sizes = [len(g[1]) for g in groups]
    arrs = [a for g in groups for a in list(g[1]) + list(g[2])]
    sems = [s for g in groups for s in g[0]]

    def wait(refs, sem_refs):
        o = 0
        for k, n in enumerate(sizes):
            for cp in _rs_chips_copies(refs[o:o + n], refs[o + n:o + 2 * n], sem_refs[2 * k], sem_refs[2 * k + 1]):
                cp.wait()
            o += 2 * n

    outs = _split_wait(wait, arrs, sems, after, name=name)
    res, o = [], 0
    for n in sizes:
        res.append((list(outs[o:o + n]), list(outs[o + n:o + 2 * n])))
        o += 2 * n
    return res


def _rs_cores_copies(grads, lands, send_sems, recv_sems):
    x, y, c = _coords()
    out, o = [], 0
    for p in range(len(grads)):
        nsh, h = lands[p].shape[0], lands[p].shape[1]
        for j in range(nsh):
            out.append(pltpu.make_async_remote_copy(
                src_ref=grads[p].at[j, pl.ds((1 - c) * h, h), :], dst_ref=lands[p].at[j],
                send_sem=send_sems.at[o + j], recv_sem=recv_sems.at[o + j], device_id=(x, y, 1 - c), device_id_type=MESH))
        o += nsh
    return out


def _rs_cores_start(grads, *, name):
    n = len(grads)
    tot = sum(g.shape[0] for g in grads)

    def start(refs, sems):
        for cp in _rs_cores_copies(refs[:n], refs[n:], sems[0], sems[1]):
            cp.start()

    lands = [lax.empty((g.shape[0], g.shape[1] // 2, g.shape[2]), g.dtype) for g in grads]
    sems, thru, token = _split_start(start, list(grads) + lands, [tot, tot], name=name)
    return (sems[0], sems[1]), thru[:n], thru[n:], token


def _rs_cores_wait(pair, grads, lands, after, *, name):
    n = len(grads)

    def wait(refs, sems):
        for cp in _rs_cores_copies(refs[:n], refs[n:], sems[0], sems[1]):
            cp.wait()

    outs = _split_wait(wait, list(grads) + list(lands), list(pair), after, name=name)
    return list(outs[:n]), list(outs[n:])


def _rs_cores(grads, *, name):
    n = len(grads)
    outs = [jax.ShapeDtypeStruct((g.shape[0], g.shape[1] // 2, g.shape[2]), g.dtype) for g in grads]

    def body(*refs):
        src, dst = refs[:n], refs[n:2 * n]
        send_sems, recv_sems = refs[2 * n:]
        x, y, c = _coords()
        copies = []
        for p in range(n):
            nsh, r, _ = grads[p].shape
            h = r // 2
            for j in range(nsh):
                cp = pltpu.make_async_remote_copy(
                    src_ref=src[p].at[j, pl.ds((1 - c) * h, h), :], dst_ref=dst[p].at[j],
                    send_sem=send_sems.at[nsh * p + j], recv_sem=recv_sems.at[nsh * p + j],
                    device_id=(x, y, 1 - c), device_id_type=MESH)
                cp.start()
                copies.append(cp)
        for cp in copies:
            cp.wait()

    tot = sum(g.shape[0] for g in grads)
    return _hbm_call(body, grads, outs, (tot, tot), name=name)


def _sum_own_half(full, recv, ci, out_dtype, *, name):
    nsh, h, cols = recv.shape
    tm = h if nsh * h * cols * 4 <= (2 << 20) else _tile_rows(h, 256)
    nblk = h // tm

    def body(c_ref, f_ref, r_ref, o_ref):
        o_ref[...] = (f_ref[...] + r_ref[...]).astype(o_ref.dtype)

    return pl.pallas_call(
        body, name=name,
        grid_spec=pltpu.PrefetchScalarGridSpec(
            num_scalar_prefetch=1, grid=(nsh, nblk),
            in_specs=[pl.BlockSpec((1, tm, cols), lambda j, i, c_ref: (j, c_ref[0] * nblk + i, 0)),
                      pl.BlockSpec((1, tm, cols), lambda j, i, c_ref: (j, i, 0))],
            out_specs=pl.BlockSpec((1, tm, cols), lambda j, i, c_ref: (j, i, 0))),
        out_shape=jax.ShapeDtypeStruct((nsh, h, cols), out_dtype), compiler_params=_params("parallel", "parallel"),
    )(ci.reshape(1).astype(jnp.int32), full, recv)


def _rs_chips(parts, *, name):
    n = len(parts)
    outs = [jax.ShapeDtypeStruct(t.shape, t.dtype) for t in parts]

    def body(*refs):
        src, dst = refs[:n], refs[n:2 * n]
        send_sems, recv_sems = refs[2 * n:]
        x, y, c = _coords()
        chip = 2 * x + y
        copies = []
        for p in range(n):
            for j, (ox, oy, oc) in enumerate(_other_chips(x, y)):
                cp = pltpu.make_async_remote_copy(
                    src_ref=src[p].at[oc], dst_ref=dst[p].at[chip], send_sem=send_sems.at[3 * p + j],
                    recv_sem=recv_sems.at[3 * p + j], device_id=(ox, oy, c), device_id_type=MESH)
                cp.start()
                copies.append(cp)
        for cp in copies:
            cp.wait()

    return _hbm_call(body, parts, outs, (3 * n, 3 * n), name=name)


def _sum_chips(recv, own, chip, ci, *, name, nlead=1, lead=0, prev=None, spread=False):
    nsh, h, cols = recv.shape
    tm = h if nsh * h * cols * 4 <= (2 << 20) else _tile_rows(h, 256)
    nblk = h // tm
    rows_out = 2 * h * (nsh if spread else 1)

    def body(s_ref, r_ref, o_ref, *rest):
        out_ref = rest[-1]
        t = None
        for s in range(nsh):
            v = jnp.where(s_ref[0] == s, o_ref[s], r_ref[s]).astype(F32)
            t = v if t is None else t + v
        out_ref[...] = t

    def out_idx(i, s):
        return (lead, (s[0] * 2 * nblk if spread else 0) + s[1] * nblk + i, 0)

    blk = pl.BlockSpec((nsh, tm, cols), lambda i, s: (0, i, 0))
    ins = [recv, own] + ([prev] if prev is not None else [])
    return pl.pallas_call(
        body, name=name,
        grid_spec=pltpu.PrefetchScalarGridSpec(
            num_scalar_prefetch=1, grid=(nblk,),
            in_specs=[blk, blk] + ([pl.BlockSpec(memory_space=pl.ANY)] if prev is not None else []),
            out_specs=pl.BlockSpec((None, tm, cols), out_idx)),
        out_shape=jax.ShapeDtypeStruct((nlead, rows_out, cols), F32),
        input_output_aliases={3: 0} if prev is not None else {},
        compiler_params=_params("arbitrary"),
    )(jnp.stack([chip, ci]).astype(jnp.int32), *ins)


def _rs_gather(arrs, halves, spread, *, name, nchunk=4):
    n = len(arrs)
    per = [a.shape[0] * nchunk for a in arrs]
    offs = [sum(per[:p]) for p in range(n)]

    def body(*refs):
        dst = refs[n:2 * n]
        send_sems, recv_sems = refs[2 * n:]
        x, y, c = _coords()
        chip = 2 * x + y
        copies = []
        for p in range(n):
            h = halves[p]
            q = h // nchunk
            base = chip * 2 * h if spread[p] else 0
            for l in range(arrs[p].shape[0]):
                for k in range(nchunk):
                    win = dst[p].at[l, pl.ds(base + c * h + k * q, q), :]
                    sem = offs[p] + l * nchunk + k
                    cp = pltpu.make_async_remote_copy(src_ref=win, dst_ref=win, send_sem=send_sems.at[sem],
                                                      recv_sem=recv_sems.at[sem], device_id=(x, y, 1 - c),
                                                      device_id_type=MESH)
                    cp.start()
                    copies.append(cp)
        for cp in copies:
            cp.wait_send()
        for p in range(n):
            h = halves[p]
            q = h // nchunk
            base = chip * 2 * h if spread[p] else 0
            for l in range(arrs[p].shape[0]):
                for k in range(nchunk):
                    win = dst[p].at[l, pl.ds(base + (1 - c) * h + k * q, q), :]
                    sem = offs[p] + l * nchunk + k
                    pltpu.make_async_remote_copy(src_ref=win, dst_ref=win, send_sem=send_sems.at[sem],
                                                 recv_sem=recv_sems.at[sem], device_id=(x, y, 1 - c),
                                                 device_id_type=MESH).wait_recv()

    outs = [jax.ShapeDtypeStruct(a.shape, a.dtype) for a in arrs]
    return _hbm_call(body, arrs, outs, (sum(per), sum(per)), name=name, alias=True)


def _adamw(w, g, m, v, *, name):
    nl, R, C = w.shape
    tm = _tile_rows(R, 256)

    blk = pl.BlockSpec((None, tm, C), lambda l, i: (l, i, 0))
    return pl.pallas_call(
        _adamw_body_copy(), name=name, grid=(nl, R // tm), in_specs=[blk] * 4, out_specs=[blk] * 3,
        out_shape=[jax.ShapeDtypeStruct((nl, R, C), F32)] * 3, compiler_params=_params("parallel", "parallel"),
    )(w, g, m, v)


def _adamw_body(w_ref, g_ref, m_ref, v_ref, d_ref, nm_ref, nv_ref):
    gg = g_ref[...]
    nm = B1 * m_ref[...] + (1.0 - B1) * gg
    nv = B2 * v_ref[...] + (1.0 - B2) * (gg * gg)
    m_hat = nm / (1.0 - B1 ** ASTEP)
    v_hat = nv / (1.0 - B2 ** ASTEP)
    d_ref[...] = -LR * (m_hat / (jnp.sqrt(v_hat) + AEPS) + WD * w_ref[...])
    nm_ref[...] = nm
    nv_ref[...] = nv


def _adamw_whole(w, g, m, v, *, name):
    return pl.pallas_call(_adamw_body_copy(), name=name, out_shape=[jax.ShapeDtypeStruct(w.shape, F32)] * 3,
                          compiler_params=_params())(w, g, m, v)


def _adamw_body_copy():
    def body(*refs):
        _adamw_body(*refs)
    return body


def _mod_cols(c_all, w_ada, b_cols, *, name):
    nl, D, cols = w_ada.shape
    B = c_all.shape[0]

    def body(c_ref, w_ref, b_ref, o_ref):
        cc = c_ref[...]
        cs = (cc * _sigmoid(cc)).astype(BF16)
        o_ref[0] = jnp.dot(cs, w_ref[0].astype(BF16), preferred_element_type=F32) + b_ref[0]

    return pl.pallas_call(
        body, name=name, grid=(nl,),
        in_specs=[_whole(c_all.shape), pl.BlockSpec((1, D, cols), lambda i: (i, 0, 0)), pl.BlockSpec((1, 1, cols), lambda i: (i, 0, 0))],
        out_specs=pl.BlockSpec((1, B, cols), lambda i: (i, 0, 0)),
        out_shape=jax.ShapeDtypeStruct((nl, B, cols), F32), compiler_params=_params("arbitrary"),
    )(c_all, w_ada, b_cols)


def _ada_grad(c_all, dmod_cols, *, name):
    nl, B, cols = dmod_cols.shape
    D = c_all.shape[1]

    def body(c_ref, d_ref, o_ref):
        cc = c_ref[...]
        cs = (cc * _sigmoid(cc)).astype(BF16)
        o_ref[0] = lax.dot_general(cs, d_ref[0].astype(BF16), TN, preferred_element_type=F32)

    return pl.pallas_call(
        body, name=name, grid=(nl,),
        in_specs=[_whole(c_all.shape), pl.BlockSpec((1, B, cols), lambda i: (i, 0, 0))],
        out_specs=pl.BlockSpec((1, D, cols), lambda i: (i, 0, 0)),
        out_shape=jax.ShapeDtypeStruct((nl, D, cols), F32), compiler_params=_params("arbitrary"),
    )(c_all, dmod_cols)


def _s5_disc(a_re, a_im, log_dt, b_re, b_im):
    dt = jnp.exp(log_dt)[:, None]
    mag = jnp.exp(a_re * dt)
    ph = a_im * dt
    lb_re = mag * jnp.cos(ph)
    lb_im = mag * jnp.sin(ph)
    den = a_re * a_re + a_im * a_im
    nr = lb_re - 1.0
    ni = lb_im
    f_re = (nr * a_re + ni * a_im) / den
    f_im = (ni * a_re - nr * a_im) / den
    bb_re = f_re[..., None] * b_re - f_im[..., None] * b_im
    bb_im = f_re[..., None] * b_im + f_im[..., None] * b_re
    return lb_re, lb_im, bb_re, bb_im


def _to_segments(t):
    L, D = t.shape
    R = min(S5_R, L)
    return t.reshape(L // R, NSEG, R // NSEG, D).transpose(0, 2, 1, 3).reshape(L, D)


def _from_segments(t):
    L, D = t.shape
    R = min(S5_R, L)
    return t.reshape(L // R, R // NSEG, NSEG, D).transpose(0, 2, 1, 3).reshape(L, D)


def _mlp_fwd(h2, w1, w2, tag):
    a = _matmul(h2, w1, name=f"ff1_{tag}", out_dtypes=(BF16,), epi=lambda acc: (jnp.maximum(acc, 0.0),))
    f = _matmul(a, w2, name=f"ff2_{tag}", a_fn=jnp.square)
    return a, f


def _mlp_bwd(df, h2, a, w1, w2, tag):
    da = _matmul(df, w2, tb=True, name=f"ff2_dx_{tag}", out_dtypes=(BF16,), epi_ins=(a,),
                 epi=lambda acc, at: (acc * (2.0 * at.astype(F32)),))
    dw2 = _matmul(a, df, ta=True, name=f"ff2_dw_{tag}", a_fn=jnp.square)
    dh2 = _matmul(da, w1, tb=True, name=f"ff1_dx_{tag}")
    dw1 = _matmul(h2, da, ta=True, name=f"ff1_dw_{tag}", col_shards=NCH)
    return dh2, dw1, dw2


def kernel(x, c, w_ada, b_ada, norm_mix, norm_mlp, s5_a_re, s5_a_im, s5_log_dt, s5_b_re, s5_b_im, s5_c_re, s5_c_im, s5_d, s5_w_glu, gla_w_in, gla_w_gate2, gla_b_gate, gla_g_norm, gla_w_out, w_ff1, w_ff2, norm_final, loss_target, m_w_ada, m_b_ada, m_norm_mix, m_norm_mlp, m_s5_a_re, m_s5_a_im, m_s5_log_dt, m_s5_b_re, m_s5_b_im, m_s5_c_re, m_s5_c_im, m_s5_d, m_s5_w_glu, m_gla_w_in, m_gla_w_gate2, m_gla_b_gate, m_gla_g_norm, m_gla_w_out, m_w_ff1, m_w_ff2, m_norm_final, v_w_ada, v_b_ada, v_norm_mix, v_norm_mlp, v_s5_a_re, v_s5_a_im, v_s5_log_dt, v_s5_b_re, v_s5_b_im, v_s5_c_re, v_s5_c_im, v_s5_d, v_s5_w_glu, v_gla_w_in, v_gla_w_gate2, v_gla_b_gate, v_gla_g_norm, v_gla_w_out, v_w_ff1, v_w_ff2, v_norm_final):
    args = dict(locals())
    L, D = x.shape[1], x.shape[2]
    QK = D // 2
    xi, yi, ci = _coords()
    chip = 2 * xi + yi
    dev = 2 * chip + ci

    c_all = _exchange(c.reshape(8, D // 8), MASK_ALL, name="gather_c").reshape(8, D)
    acols = w_ada.shape[2]
    b_cols = lax.dynamic_slice_in_dim(b_ada, chip * acols, acols, axis=1)[:, None, :]
    mod_cols = _mod_cols(c_all, w_ada, b_cols, name="ada_mod")
    mod_all = _exchange(mod_cols.reshape(16, acols), MASK_CHIPS, name="gather_mod")
    mod_all = mod_all.reshape(NCH, 2, 8, acols).transpose(1, 2, 0, 3).reshape(2, 8, NCH * acols)
    mod = lax.dynamic_index_in_dim(mod_all, dev, axis=1, keepdims=False).reshape(2, 6, 1, D)

    big = [("s5_w_glu", s5_w_glu, 0, "col"), ("gla_w_in", gla_w_in, 0, "slot"), ("gla_w_out", gla_w_out, 0, "row"),
           ("w_ff1_0", w_ff1, 0, "col"), ("w_ff1_1", w_ff1, 1, "col"), ("w_ff2_0", w_ff2, 0, "row"), ("w_ff2_1", w_ff2, 1, "row")]
    own16 = [_cast_into(t, lead, kind, chip, name=f"cast_{nm}") for nm, t, lead, kind in big]
    wshapes, wkinds = [b[1].shape[-2:] for b in big], [b[3] for b in big]
    wgroups = [[0, 3, 5], [1, 2, 4, 6]]
    wsems, wthru, wtoken = _gw_start(own16, wshapes, wkinds, wgroups, name="gather_w_start")
    W = {}

    def finish_weights(g, after):
        idx = wgroups[g]
        shp, knd = [wshapes[p] for p in idx], [wkinds[p] for p in idx]
        got = _gw_wait([wthru[p] for p in idx], shp, knd, wsems[g], after, name=f"gather_w_wait{g}")
        for p, w in zip(idx, _gw_forward(got, shp, knd, name=f"gather_w_cores{g}")):
            W[big[p][0]] = w

    cat = jnp.concatenate([gla_w_gate2[0].reshape(1, -1), gla_b_gate, gla_g_norm], axis=1)
    cat_all = _exchange(jnp.tile(cat, (8, 1)), MASK_CHIPS, name="gather_gla_small")[:, 0, :]
    qk4 = QK // NCH
    wg2 = cat_all[:, :GATE_RANK * qk4].reshape(NCH, GATE_RANK, qk4).transpose(1, 0, 2).reshape(GATE_RANK, QK)
    bg = cat_all[:, GATE_RANK * qk4:(GATE_RANK + 1) * qk4].reshape(1, QK)
    gn = cat_all[:, (GATE_RANK + 1) * qk4:].reshape(1, D)
    wg2p = jnp.concatenate([wg2, jnp.zeros((128 - GATE_RANK, QK), F32)], axis=0).astype(BF16)

    lb_re, lb_im, bb_re, bb_im = _s5_disc(s5_a_re[0], s5_a_im[0], s5_log_dt[0], s5_b_re[0], s5_b_im[0])
    tb = _s5_tables(lb_re, lb_im, bb_re, bb_im, s5_c_re[0], s5_c_im[0], min(S5_R, L) // NSEG)
    s5_dv = s5_d + wtoken[0, 0]

    def vec(t):
        return t.reshape(1, -1)

    xp = _to_segments(x[0])
    m0, m1 = mod[0], mod[1]
    (u0,) = _rows(lambda t, g, sc, sh: (_norm_mod(t, g, sc, sh),), [xp], [vec(norm_mix[0]), m0[1], m0[0]],
                  [(D, F32)], [], name="pre_mix0")
    y0, z0, ck0 = _s5_fwd2(u0, tb, s5_dv, name="s5_fwd")
    finish_weights(0, z0)
    vg0 = _matmul(z0, W["s5_w_glu"], name="glu")

    def res_glu_pre(xt, vgt, gt, g, sc, sh):
        xn = xt + gt * (vgt[:, :D] * _sigmoid(vgt[:, D:]))
        return xn, _norm_mod(xn, g, sc, sh)

    x2_0, h2_0 = _rows(res_glu_pre, [xp, vg0], [m0[2], vec(norm_mlp[0]), m0[4], m0[3]], [(D, F32), (D, BF16)], [],
                       name="res_mix0")
    a_0, f0 = _mlp_fwd(h2_0, W["w_ff1_0"], W["w_ff2_0"], "0")

    def res_pre(xt, bt, gt, g, sc, sh):
        xn = xt + gt * bt
        return xn, _norm_mod(xn, g, sc, sh)

    x3p, h1p = _rows(res_pre, [x2_0, f0], [m0[5], vec(norm_mix[1]), m1[1], m1[0]], [(D, F32), (D, BF16)], [],
                     name="res_mlp0")
    x3 = _from_segments(x3p)
    h1 = _from_segments(h1p)
    finish_weights(1, f0)
    w_in = W["gla_w_in"].transpose(1, 0, 2).reshape(D, -1)
    w_in_r = jnp.concatenate([w_in[:, :4 * QK], w_in[:, 4 * QK + GATE_RANK:], w_in[:, 4 * QK:4 * QK + GATE_RANK],
                              jnp.zeros((D, 128 - GATE_RANK), BF16)], axis=1)
    proj = _matmul(h1, w_in_r, name="gla_in", tn=640)
    og, states = _gla_fwd(proj, wg2p, bg, gn, name="gla_fwd")
    ymix = _matmul(og, W["gla_w_out"], name="gla_out")
    x2_1, h2_1 = _rows(res_pre, [x3, ymix], [m1[2], vec(norm_mlp[1]), m1[4], m1[3]], [(D, F32), (D, BF16)], [],
                       name="res_mix1")
    a_1, f1 = _mlp_fwd(h2_1, W["w_ff1_1"], W["w_ff2_1"], "1")

    def final(xt, ft, tgt, gt, g):
        xn = xt + gt * ft
        rs = lax.rsqrt(jnp.mean(xn * xn, axis=-1, keepdims=True) + EPS)
        xh = xn * rs
        e = xh * g - tgt
        dout = e * (1.0 / D)
        dxh = dout * g
        dx = rs * (dxh - xh * jnp.mean(dxh * xh, axis=-1, keepdims=True))
        lsum = 0.5 * jnp.sum(jnp.sum(e * e, axis=-1, keepdims=True), axis=0, keepdims=True) * (1.0 / D)
        return dx, dx * gt, jnp.broadcast_to(lsum, (1, 128)), _rsum(dout * xh), _rsum(dx * ft)

    dx, df1, loss_part, d_norm_final, dgt2_1 = _rows(
        final, [x2_1, f1, loss_target[0]], [m1[5], vec(norm_final)], [(D, F32), (D, BF16)],
        [(1, 128), (1, D), (1, D)], name="loss_head")
    loss = lax.psum(loss_part[0, 0], ("x", "y", "c"))

    def gate_bwd(dxt, bt, gt):
        return dxt * gt, _rsum(dxt * bt)

    def norm_bwd(xt, dht, drt, g, sc):
        dxn, dsh, dsc, dg = _norm_mod_bwd(xt, dht, g, sc)
        return drt + dxn, dsh, dsc, dg

    def norm_gate_bwd(xt, dht, drt, bt, g, sc, gt):
        dxn, dsh, dsc, dg = _norm_mod_bwd(xt, dht, g, sc)
        dxt = drt + dxn
        return dxt, dxt * gt, dsh, dsc, dg, _rsum(dxt * bt)

    vD = [(1, D)]
    dh2_1, dw_ff1_1, dw_ff2_1 = _mlp_bwd(df1, h2_1, a_1, W["w_ff1_1"], W["w_ff2_1"], "1")
    dx, dmix1, dsh2_1, dsc2_1, dg_mlp1, dgt1_1 = _rows(
        norm_gate_bwd, [x2_1, dh2_1, dx, ymix], [vec(norm_mlp[1]), m1[4], m1[2]], [(D, F32), (D, BF16)], vD * 4,
        name="norm_mlp1_bwd")
    dog = _matmul(dmix1, W["gla_w_out"], tb=True, name="gla_out_dx")
    dw_out = _matmul(og, dmix1, ta=True, name="gla_out_dw")
    dproj, dwg2p, dbg, dgn = _gla_bwd(proj, dog, states, wg2p, bg, gn, name="gla_bwd")
    dh1 = _matmul(dproj, w_in_r, tb=True, name="gla_in_dx", tk=640)
    dw_in_r = _matmul(h1, dproj, ta=True, name="gla_in_dw", tn=640)
    dx, dsh1_1, dsc1_1, dg_mix1 = _rows(norm_bwd, [x3, dh1, dx], [vec(norm_mix[1]), m1[1]], [(D, F32)], vD * 3,
                                        name="norm_mix1_bwd")
    tags = [b[0] for b in big] + ["small"]
    rs_groups = []

    def rs_chips_begin(idx, srcs, r1, gname):
        s1 = [_sum_own_half(g, r, ci, F32 if tags[k] == "small" else BF16, name=f"rs_sum_cores_{tags[k]}")
              for g, r, k in zip(srcs, r1, idx)]
        pair, parts, lands, token = _rs_chips_start(s1, name=f"rs_chips_start_{gname}")
        rs_groups.append((idx, pair, parts, lands))
        return token

    def rs_begin(idx, srcs, gname):
        return rs_chips_begin(idx, srcs, _rs_cores(srcs, name=f"rs_cores_{gname}"), gname)

    dw_in = jnp.concatenate([dw_in_r[:, :4 * QK], dw_in_r[:, 6 * QK:6 * QK + GATE_RANK], dw_in_r[:, 4 * QK:6 * QK]], axis=1)
    dw_in = dw_in.reshape(D, NCH, -1).transpose(1, 0, 2)
    idx1 = [1, 2, 4, 6]
    pair1, src1, land1, tok1 = _rs_cores_start(
        [dw_in, dw_out.reshape(NCH, -1, D), dw_ff1_1, dw_ff2_1.reshape(NCH, -1, D)], name="rs_cores_start_l1")

    dxp = _to_segments(dx)
    df0, dgt2_0 = _rows(gate_bwd, [dxp, f0], [m0[5] + tok1[0, 0]], [(D, BF16)], vD, name="gate_mlp0")
    dh2_0, dw_ff1_0, dw_ff2_0 = _mlp_bwd(df0, h2_0, a_0, W["w_ff1_0"], W["w_ff2_0"], "0")
    src1, land1 = _rs_cores_wait(pair1, src1, land1, dh2_0, name="rs_cores_wait_l1")
    tok1b = rs_chips_begin(idx1, src1, land1, "l1")
    idx0 = [3, 5]
    pair0, src0, land0, tok0 = _rs_cores_start([dw_ff1_0, dw_ff2_0.reshape(NCH, -1, D)], name="rs_cores_start_l0")
    tok2 = tok1b + tok0

    def norm_glu_bwd(xt, dht, drt, vgt, g, sc, gt):
        dxn, dsh, dsc, dg = _norm_mod_bwd(xt, dht, g, sc)
        dxt = drt + dxn
        val, sg = vgt[:, :D], _sigmoid(vgt[:, D:])
        dbr = dxt * gt
        dvg = jnp.concatenate([dbr * sg, dbr * val * sg * (1.0 - sg)], axis=1)
        return dxt, dvg, dsh, dsc, dg, _rsum(dxt * val * sg)

    dxp, dvg0, dsh2_0, dsc2_0, dg_mlp0, dgt1_0 = _rows(
        norm_glu_bwd, [x2_0, dh2_0, dxp, vg0], [vec(norm_mlp[0]), m0[4] + tok2[0, 0], m0[2]], [(D, F32), (2 * D, BF16)],
        vD * 4, name="norm_mlp0_bwd")
    dz0 = _matmul(dvg0, W["s5_w_glu"], tb=True, name="glu_dx")
    dw_glu = _matmul(z0, dvg0, ta=True, name="glu_dw", tn=512, col_shards=NCH)
    src0, land0 = _rs_cores_wait(pair0, src0, land0, dw_glu, name="rs_cores_wait_l0")
    tok0b = rs_chips_begin(idx0, src0, land0, "l0")
    du0, db_acc, dc_acc, dl_acc, dd_s5 = _s5_bwd2(u0, y0, dz0, ck0, tb, s5_dv + tok0b[0, 0], name="s5_bwd")
    dxp, dsh1_0, dsc1_0, dg_mix0 = _rows(norm_bwd, [xp, du0, dxp], [vec(norm_mix[0]), m0[1]], [(D, F32)], vD * 3,
                                         name="norm_mix0_bwd")
    grad_x = _from_segments(dxp)[None]

    dmod = jnp.concatenate([dsh1_0, dsc1_0, dgt1_0, dsh2_0, dsc2_0, dgt2_0,
                            dsh1_1, dsc1_1, dgt1_1, dsh2_1, dsc2_1, dgt2_1], axis=1)
    dbb_re, dbb_im = _s5_untable(db_acc)
    dc_re, dc_im_neg = _s5_untable(dc_acc)
    nbk = D // 128
    dl = dl_acc.reshape(nbk, NSEG, 2, GPB * S5_P).sum(axis=1)
    smalls = [dmod, dg_mix0, dg_mix1, dg_mlp0, dg_mlp1, d_norm_final, dd_s5, dbg, dgn,
              dwg2p[:GATE_RANK].reshape(1, -1), dbb_re.reshape(1, -1), dbb_im.reshape(1, -1),
              dc_re.reshape(1, -1), dc_im_neg.reshape(1, -1), dl.reshape(1, -1)]
    ssz = [t.shape[1] for t in smalls]
    stot = sum(ssz)
    spad = -(-stot // 8192) * 8192
    svec = jnp.concatenate(smalls + [jnp.zeros((1, spad - stot), F32)], axis=1).reshape(NCH, spad // (128 * NCH), 128)

    dmod_all = _exchange(dmod.reshape(12 * D // 128, 128), MASK_ALL, name="gather_dmod").reshape(8, 2, 6 * D)
    dmod_cols = lax.dynamic_slice_in_dim(dmod_all, chip * acols, acols, axis=2).transpose(1, 0, 2)
    g_w_ada = _ada_grad(c_all, dmod_cols, name="ada_grad")

    rs_begin([0, 7], [dw_glu, svec], "last")
    landed = _rs_chips_wait([(g[1], g[2], g[3]) for g in rs_groups], dxp, name="rs_chips_wait")
    s1, r2 = {}, {}
    for (idx, _, _, _), (parts, lands) in zip(rs_groups, landed):
        for k, part, land in zip(idx, parts, lands):
            s1[k], r2[k] = part, land

    def fin(k, **kw):
        return _sum_chips(r2[k], s1[k], chip, ci, name=f"rs_sum_chips_{tags[k]}", **kw)

    f_ff1 = fin(4, nlead=2, lead=1, prev=fin(3, nlead=2, lead=0))
    f_ff2 = fin(6, nlead=2, lead=1, prev=fin(5, nlead=2, lead=0))
    finals = [fin(0), fin(1), fin(2), f_ff1, f_ff2, fin(7, spread=True)]
    halves = [t.shape[1] for t in (s1[0], s1[1], s1[2], s1[3], s1[5], s1[7])]
    g_glu, g_in, g_out, g_w_ff1, g_w_ff2, s_own = _rs_gather(finals, halves, [False] * 5 + [True], name="rs_gather_cores")
    srows = spad // (128 * NCH)
    (s_sum,) = _gather_weights([s_own.reshape(NCH * srows, 128)], [(srows, 128)], ["row"], name="gather_small_grads")
    s_sum = s_sum.reshape(-1)
    so = [sum(ssz[:k]) for k in range(len(ssz))]
    sm = [s_sum[o:o + n] for o, n in zip(so, ssz)]
    (dmod_s, g_mix0, g_mix1, g_mlp0, g_mlp1, g_nf, g_d, g_bg, g_gn, g_wg2, g_bbre, g_bbim, g_cre, g_cimn, g_dl) = sm
    g_b_ada = dmod_s.reshape(2, 6 * D)

    G = D // S5_H
    _, disc_vjp = jax.vjp(_s5_disc, s5_a_re[0], s5_a_im[0], s5_log_dt[0], s5_b_re[0], s5_b_im[0])
    g_dl = g_dl.reshape(nbk, 2, GPB, S5_P)
    ct = (g_dl[:, 0].reshape(G, S5_P), g_dl[:, 1].reshape(G, S5_P),
          g_bbre.reshape(G, S5_H, S5_P).transpose(0, 2, 1), g_bbim.reshape(G, S5_H, S5_P).transpose(0, 2, 1))
    g_a_re, g_a_im, g_log_dt, g_b_re, g_b_im = disc_vjp(ct)
    g_c_re = g_cre.reshape(G, S5_H, S5_P)
    g_c_im = -g_cimn.reshape(G, S5_H, S5_P)
    g_wg2_s = lax.dynamic_slice_in_dim(g_wg2.reshape(GATE_RANK, QK), chip * qk4, qk4, axis=1)
    g_bg_s = lax.dynamic_slice_in_dim(g_bg.reshape(1, QK), chip * qk4, qk4, axis=1)
    g_gn_s = lax.dynamic_slice_in_dim(g_gn.reshape(1, D), chip * (D // NCH), D // NCH, axis=1)

    grads = dict(
        w_ada=g_w_ada, b_ada=g_b_ada, norm_mix=jnp.stack([g_mix0, g_mix1]), norm_mlp=jnp.stack([g_mlp0, g_mlp1]),
        s5_a_re=g_a_re[None], s5_a_im=g_a_im[None], s5_log_dt=g_log_dt[None], s5_b_re=g_b_re[None], s5_b_im=g_b_im[None],
        s5_c_re=g_c_re[None], s5_c_im=g_c_im[None], s5_d=g_d[None], s5_w_glu=g_glu,
        gla_w_in=g_in, gla_w_gate2=g_wg2_s[None], gla_b_gate=g_bg_s, gla_g_norm=g_gn_s,
        gla_w_out=g_out, w_ff1=g_w_ff1, w_ff2=g_w_ff2, norm_final=g_nf)

    names = list(grads)
    large = ("w_ada", "s5_w_glu", "gla_w_in", "gla_w_out", "w_ff1", "w_ff2")
    delta, new_m, new_v = {}, {}, {}
    for nm in large:
        delta[nm], new_m[nm], new_v[nm] = _adamw(args[nm], grads[nm], args["m_" + nm], args["v_" + nm], name=f"adamw_{nm}")
    grads = {nm: grads[nm].reshape(args[nm].shape) for nm in names}
    for nm in names:
        if nm not in large:
            shp = args[nm].shape
            as2d = (1, -1) if len(shp) == 1 else shp
            outs = _adamw_whole(*[t.reshape(as2d) for t in (args[nm], grads[nm], args["m_" + nm], args["v_" + nm])],
                                name=f"adamw_{nm}")
            delta[nm], new_m[nm], new_v[nm] = (t.reshape(shp) for t in outs)
    return (loss, grad_x, *[grads[n] for n in names], *[delta[n] for n in names], *[new_m[n] for n in names],
            *[new_v[n] for n in names])
```

```python
import math

import jax
import jax.numpy as jnp
from jax import lax
from jax.experimental import pallas as pl
from jax.experimental.pallas import tpu as pltpu

F32 = jnp.float32
BF16 = jnp.bfloat16
MESH = pl.DeviceIdType.MESH

EPS = 1e-6
CHUNK = 64
GLA_NB = 4
S5_H = 16
S5_P = 64
GPB = 8
NSEG = 8
HEADS = 4
GATE_RANK = 16
GATE_TAU = 16.0
NCH = 4
LR, B1, B2, AEPS, WD, ASTEP = 0.001, 0.9, 0.999, 1e-08, 0.01, 10
VMEM_LIMIT = 56 << 20

MASK_CHIPS = ((1, 0, 0), (0, 1, 0), (1, 1, 0))
MASK_ALL = ((0, 0, 1), (0, 1, 0), (0, 1, 1), (1, 0, 0), (1, 0, 1), (1, 1, 0), (1, 1, 1))


def _params(*sem):
    return pltpu.CompilerParams(dimension_semantics=sem or None, vmem_limit_bytes=VMEM_LIMIT)


def _tile_rows(rows, cap=512):
    best = 8
    for t in range(8, cap + 1, 8):
        if rows % t == 0:
            best = t
    return best


def _whole(shape):
    return pl.BlockSpec(shape, lambda i, _n=len(shape): (0,) * _n)


def _matmul(a, b, *, name, ta=False, tb=False, tm=1024, tn=1024, tk=2048, out_dtypes=(F32,),
            a_fn=None, epi=None, epi_ins=(), col_shards=1):
    M, K = (a.shape[1], a.shape[0]) if ta else a.shape
    N = b.shape[0] if tb else b.shape[1]
    tm, tn, tk = min(tm, M), min(tn, N), min(tk, K)
    assert M % tm == 0 and N % tn == 0 and K % tk == 0, (name, M, N, K)
    nk = K // tk
    ne = len(epi_ins)
    dn = (((0 if ta else 1,), (1 if tb else 0,)), ((), ()))

    def body(a_ref, b_ref, *rest):
        e_refs, o_refs, acc = rest[:ne], rest[ne:-1], rest[-1]
        k = pl.program_id(2)

        @pl.when(k == 0)
        def _():
            acc[...] = jnp.zeros_like(acc)

        at = a_ref[...]
        if a_fn is not None:
            at = a_fn(at)
        acc[...] += lax.dot_general(at.astype(BF16), b_ref[...].astype(BF16), dn, preferred_element_type=F32)

        @pl.when(k == nk - 1)
        def _():
            outs = (acc[...],) if epi is None else epi(acc[...], *[r[...] for r in e_refs])
            for r, o in zip(o_refs, outs):
                r[...] = o.astype(r.dtype)

    a_spec = pl.BlockSpec((tk, tm), lambda i, j, k: (k, i)) if ta else pl.BlockSpec((tm, tk), lambda i, j, k: (i, k))
    b_spec = pl.BlockSpec((tn, tk), lambda i, j, k: (j, k)) if tb else pl.BlockSpec((tk, tn), lambda i, j, k: (k, j))
    o_spec = pl.BlockSpec((tm, tn), lambda i, j, k: (i, j))
    if col_shards > 1:
        per = N // col_shards // tn
        assert ne == 0 and per * tn * col_shards == N
        w_spec = pl.BlockSpec((None, tm, tn), lambda i, j, k: (j // per, i, j % per))
        o_shape = (col_shards, M, N // col_shards)
    else:
        w_spec, o_shape = o_spec, (M, N)
    outs = pl.pallas_call(
        body, name=name, grid=(M // tm, N // tn, nk),
        in_specs=[a_spec, b_spec] + [o_spec] * ne,
        out_specs=[w_spec] * len(out_dtypes),
        out_shape=[jax.ShapeDtypeStruct(o_shape, d) for d in out_dtypes],
        scratch_shapes=[pltpu.VMEM((tm, tn), F32)],
        compiler_params=_params("parallel", "parallel", "arbitrary"),
    )(a, b, *epi_ins)
    return outs[0] if len(outs) == 1 else outs


def _rows(fn, rows_in, vecs_in, rows_out, acc_out, *, name, tm=512):
    L = rows_in[0].shape[0]
    tm = min(tm, L)
    assert L % tm == 0
    nr, nv, no, na = len(rows_in), len(vecs_in), len(rows_out), len(acc_out)

    def body(*refs):
        rin, vin = refs[:nr], refs[nr:nr + nv]
        rout, aout = refs[nr + nv:nr + nv + no], refs[nr + nv + no:]
        outs = fn(*[r[...] for r in rin], *[v[...] for v in vin])
        for r, o in zip(rout, outs[:no]):
            r[...] = o.astype(r.dtype)
        if na:
            @pl.when(pl.program_id(0) == 0)
            def _():
                for r in aout:
                    r[...] = jnp.zeros_like(r)

            for r, o in zip(aout, outs[no:]):
                r[...] += o

    outs = pl.pallas_call(
        body, name=name, grid=(L // tm,),
        in_specs=[pl.BlockSpec((tm, r.shape[1]), lambda i: (i, 0)) for r in rows_in] + [_whole(v.shape) for v in vecs_in],
        out_specs=[pl.BlockSpec((tm, c), lambda i: (i, 0)) for c, _ in rows_out] + [_whole(s) for s in acc_out],
        out_shape=[jax.ShapeDtypeStruct((L, c), d) for c, d in rows_out] + [jax.ShapeDtypeStruct(s, F32) for s in acc_out],
        compiler_params=_params("arbitrary"),
    )(*rows_in, *vecs_in)
    return outs


def _rsum(t):
    return jnp.sum(t, axis=0, keepdims=True)


def _norm_mod(x, g, sc, sh):
    rs = lax.rsqrt(jnp.mean(x * x, axis=-1, keepdims=True) + EPS)
    return x * rs * g * (1.0 + sc) + sh


def _norm_mod_bwd(x, dh, g, sc):
    rs = lax.rsqrt(jnp.mean(x * x, axis=-1, keepdims=True) + EPS)
    xh = x * rs
    dn = dh * (1.0 + sc)
    dxh = dn * g
    dx = rs * (dxh - xh * jnp.mean(dxh * xh, axis=-1, keepdims=True))
    return dx, _rsum(dh), _rsum(dh * xh * g), _rsum(dn * xh)


def _sigmoid(x):
    return jax.nn.sigmoid(x)


def _gelu(y):
    return jax.nn.gelu(y, approximate=True)


def _gelu_grad(y):
    c = math.sqrt(2.0 / math.pi)
    t = jnp.tanh(c * (y + 0.044715 * y * y * y))
    return 0.5 * (1.0 + t) + 0.5 * y * (1.0 - t * t) * c * (1.0 + 3.0 * 0.044715 * y * y)


def _s5_tables(lb_re, lb_im, bb_re, bb_im, c_re, c_im, seg_len):
    G = lb_re.shape[0]
    nb = G // GPB
    eye = jnp.eye(GPB, dtype=F32)

    def bdiag(t):
        a, b = t.shape[1:]
        t = t.reshape(nb, GPB, a, b)
        return (t[:, :, :, None, :] * eye[None, :, None, :, None]).reshape(nb, GPB * a, GPB * b)

    bbd = jnp.concatenate([bdiag(bb_re.transpose(0, 2, 1)), bdiag(bb_im.transpose(0, 2, 1))], axis=2)
    cbd = jnp.concatenate([bdiag(c_re.transpose(0, 2, 1)), -bdiag(c_im.transpose(0, 2, 1))], axis=1)

    def lanes(re, im):
        t = jnp.concatenate([re.reshape(nb, GPB * S5_P), im.reshape(nb, GPB * S5_P)], axis=1)
        return jnp.repeat(t, NSEG, axis=0)

    tr, ti = lb_re, lb_im
    for _ in range(int(math.log2(seg_len))):
        tr, ti = tr * tr - ti * ti, 2.0 * tr * ti
    return dict(bbd=bbd.astype(BF16), bbdT=bbd.transpose(0, 2, 1).astype(BF16), cbd=cbd.astype(BF16),
                cbdT=cbd.transpose(0, 2, 1).astype(BF16), lam=lanes(lb_re, lb_im), lamT=lanes(tr, ti))


def _s5_untable(acc):
    nb = acc.shape[0]
    t = acc.reshape(nb, GPB, S5_H, 2, GPB, S5_P)
    d = jnp.diagonal(t, axis1=1, axis2=4)
    d = d.transpose(0, 4, 2, 1, 3).reshape(nb * GPB, 2, S5_H, S5_P)
    return d[:, 0], d[:, 1]


def _s5_fix(ends, lamT, *, reverse, name):
    nrow = ends.shape[0]
    half = ends.shape[1] // 2

    def body(e_ref, t_ref, o_ref):
        for gb in range(nrow // NSEG):
            r0 = gb * NSEG
            tr, ti = t_ref[r0:r0 + 1, :half], t_ref[r0:r0 + 1, half:]
            cr = jnp.zeros((1, half), F32)
            ci = jnp.zeros((1, half), F32)
            order = range(NSEG - 1, -1, -1) if reverse else range(NSEG)
            for n, s in enumerate(order):
                if n > 0:
                    p = s + 1 if reverse else s - 1
                    er, ei = e_ref[r0 + p:r0 + p + 1, :half], e_ref[r0 + p:r0 + p + 1, half:]
                    if reverse:
                        cr, ci = tr * cr + ti * ci + er, tr * ci - ti * cr + ei
                    else:
                        cr, ci = tr * cr - ti * ci + er, tr * ci + ti * cr + ei
                o_ref[r0 + s:r0 + s + 1, :half] = cr
                o_ref[r0 + s:r0 + s + 1, half:] = ci

    return pl.pallas_call(body, name=name, out_shape=jax.ShapeDtypeStruct(ends.shape, F32),
                          compiler_params=_params())(ends, lamT)


def _s5_fwd(up, carry_in, tb, dvec, *, emit, name, R=256):
    L, D = up.shape
    R = min(R, L)
    nb, ta, ngb = L // R, R // NSEG, D // 128
    SW = GPB * S5_P
    crows = ngb * NSEG

    def body(u_ref, cin_ref, lam_ref, b_ref, c_ref, d_ref, *rest):
        if emit:
            y_ref, z_ref, ck_ref, carry, xbuf = rest
        else:
            cout_ref, carry, xbuf = rest
        i = pl.program_id(0)

        @pl.when(i == 0)
        def _():
            carry[...] = cin_ref[...]

        if emit:
            ck_ref[0] = carry[...]
        for gb in range(ngb):
            cols = slice(gb * 128, (gb + 1) * 128)
            rws = slice(gb * NSEG, (gb + 1) * NSEG)
            ug = u_ref[:, cols]
            xbuf[...] = jnp.dot(ug.astype(BF16), b_ref[gb], preferred_element_type=F32)
            lr, li = lam_ref[rws, :SW], lam_ref[rws, SW:]

            def step(a, c, lr=lr, li=li):
                cr, ci = c
                o = pl.multiple_of(a * NSEG, NSEG)
                nr = lr * cr - li * ci + xbuf[pl.ds(o, NSEG), :SW]
                ni = lr * ci + li * cr + xbuf[pl.ds(o, NSEG), SW:]
                if emit:
                    xbuf[pl.ds(o, NSEG), :SW] = nr
                    xbuf[pl.ds(o, NSEG), SW:] = ni
                return nr, ni

            cr, ci = lax.fori_loop(0, ta, step, (carry[rws, :SW], carry[rws, SW:]), unroll=2)
            carry[rws, :SW] = cr
            carry[rws, SW:] = ci
            if emit:
                y = jnp.dot(xbuf[...].astype(BF16), c_ref[gb], preferred_element_type=F32) + d_ref[:, cols] * ug
                y_ref[:, cols] = y
                z_ref[:, cols] = _gelu(y).astype(BF16)
        if not emit:
            cout_ref[...] = carry[...]

    rowblk = pl.BlockSpec((R, D), lambda i: (i, 0))
    if emit:
        out_shape = [jax.ShapeDtypeStruct((L, D), F32), jax.ShapeDtypeStruct((L, D), BF16),
                     jax.ShapeDtypeStruct((nb, crows, 2 * SW), F32)]
        out_specs = [rowblk, rowblk, pl.BlockSpec((1, crows, 2 * SW), lambda i: (i, 0, 0))]
    else:
        out_shape = [jax.ShapeDtypeStruct((crows, 2 * SW), F32)]
        out_specs = [_whole((crows, 2 * SW))]
    return pl.pallas_call(
        body, name=name, grid=(nb,),
        in_specs=[rowblk, _whole(carry_in.shape), _whole(tb["lam"].shape), _whole(tb["bbd"].shape),
                  _whole(tb["cbd"].shape), _whole(dvec.shape)],
        out_specs=out_specs, out_shape=out_shape,
        scratch_shapes=[pltpu.VMEM((crows, 2 * SW), F32), pltpu.VMEM((R, 2 * SW), F32)],
        compiler_params=_params("arbitrary"),
    )(up, carry_in, tb["lam"], tb["bbd"], tb["cbd"], dvec)


def _s5_bwd(up, y, dz, ck, gcarry_in, tb, dvec, *, emit, name, R=256):
    L, D = up.shape
    R = min(R, L)
    nb, ta, ngb = L // R, R // NSEG, D // 128
    SW = GPB * S5_P
    crows = ngb * NSEG

    def body(u_ref, y_ref, dz_ref, ck_ref, gin_ref, lam_ref, b_ref, bt_ref, ct_ref, d_ref, *rest):
        if emit:
            du_ref, db_ref, dc_ref, dl_ref, dd_ref, gcarry, xbuf, gbuf, dybuf = rest
        else:
            gout_ref, gcarry, gbuf, dybuf = rest
        i = pl.program_id(0)

        @pl.when(i == 0)
        def _():
            gcarry[...] = gin_ref[...]
            if emit:
                db_ref[...] = jnp.zeros_like(db_ref)
                dc_ref[...] = jnp.zeros_like(dc_ref)
                dl_ref[...] = jnp.zeros_like(dl_ref)
                dd_ref[...] = jnp.zeros_like(dd_ref)

        dybuf[...] = dz_ref[...] * _gelu_grad(y_ref[...])
        for gb in range(ngb):
            cols = slice(gb * 128, (gb + 1) * 128)
            rws = slice(gb * NSEG, (gb + 1) * NSEG)
            dyg = dybuf[:, cols]
            lr, li = lam_ref[rws, :SW], lam_ref[rws, SW:]
            gbuf[...] = jnp.dot(dyg.astype(BF16), ct_ref[gb], preferred_element_type=F32)
            if emit:
                ug = u_ref[:, cols]
                xbuf[0:NSEG, :] = ck_ref[0, rws, :]
                xbuf[NSEG:, :] = jnp.dot(ug.astype(BF16), b_ref[gb], preferred_element_type=F32)

                def fstep(a, c, lr=lr, li=li):
                    cr, ci = c
                    o = pl.multiple_of(a * NSEG + NSEG, NSEG)
                    nr = lr * cr - li * ci + xbuf[pl.ds(o, NSEG), :SW]
                    ni = lr * ci + li * cr + xbuf[pl.ds(o, NSEG), SW:]
                    xbuf[pl.ds(o, NSEG), :SW] = nr
                    xbuf[pl.ds(o, NSEG), SW:] = ni
                    return nr, ni

                lax.fori_loop(0, ta, fstep, (xbuf[0:NSEG, :SW], xbuf[0:NSEG, SW:]), unroll=2)

            def rstep(k, c, lr=lr, li=li):
                o = pl.multiple_of((ta - 1 - k) * NSEG, NSEG)
                gr_n, gi_n = c[0], c[1]
                gr = gbuf[pl.ds(o, NSEG), :SW] + lr * gr_n + li * gi_n
                gi = gbuf[pl.ds(o, NSEG), SW:] - li * gr_n + lr * gi_n
                if not emit:
                    return gr, gi
                gbuf[pl.ds(o, NSEG), :SW] = gr
                gbuf[pl.ds(o, NSEG), SW:] = gi
                xr, xi = xbuf[pl.ds(o, NSEG), :SW], xbuf[pl.ds(o, NSEG), SW:]
                return gr, gi, c[2] + gr * xr + gi * xi, c[3] + gi * xr - gr * xi

            c0 = (gcarry[rws, :SW], gcarry[rws, SW:])
            if emit:
                c0 = c0 + (jnp.zeros((NSEG, SW), F32), jnp.zeros((NSEG, SW), F32))
            cf = lax.fori_loop(0, ta, rstep, c0, unroll=2)
            gcarry[rws, :SW] = cf[0]
            gcarry[rws, SW:] = cf[1]
            if emit:
                dl_ref[rws, :SW] += cf[2]
                dl_ref[rws, SW:] += cf[3]
                gb16 = gbuf[...].astype(BF16)
                du_ref[:, cols] = jnp.dot(gb16, bt_ref[gb], preferred_element_type=F32) + d_ref[:, cols] * dyg
                tn = (((0,), (0,)), ((), ()))
                db_ref[gb] += lax.dot_general(ug.astype(BF16), gb16, tn, preferred_element_type=F32)
                dc_ref[gb] += lax.dot_general(dyg.astype(BF16), xbuf[NSEG:, :].astype(BF16), tn,
                                              preferred_element_type=F32)
                dd_ref[:, cols] += _rsum(dyg * ug)
        if not emit:
            gout_ref[...] = gcarry[...]

    rev = pl.BlockSpec((R, D), lambda i: (nb - 1 - i, 0))
    acc3 = (ngb, 128, 2 * SW)
    if emit:
        out_shape = [jax.ShapeDtypeStruct((L, D), F32), jax.ShapeDtypeStruct(acc3, F32), jax.ShapeDtypeStruct(acc3, F32),
                     jax.ShapeDtypeStruct((crows, 2 * SW), F32), jax.ShapeDtypeStruct((1, D), F32)]
        out_specs = [rev, _whole(acc3), _whole(acc3), _whole((crows, 2 * SW)), _whole((1, D))]
        scratch = [pltpu.VMEM((crows, 2 * SW), F32), pltpu.VMEM((R + NSEG, 2 * SW), F32),
                   pltpu.VMEM((R, 2 * SW), F32), pltpu.VMEM((R, D), F32)]
    else:
        out_shape = [jax.ShapeDtypeStruct((crows, 2 * SW), F32)]
        out_specs = [_whole((crows, 2 * SW))]
        scratch = [pltpu.VMEM((crows, 2 * SW), F32), pltpu.VMEM((R, 2 * SW), F32), pltpu.VMEM((R, D), F32)]
    return pl.pallas_call(
        body, name=name, grid=(nb,),
        in_specs=[rev, rev, rev, pl.BlockSpec((1, crows, 2 * SW), lambda i: (nb - 1 - i, 0, 0)),
                  _whole(gcarry_in.shape), _whole(tb["lam"].shape), _whole(tb["bbd"].shape),
                  _whole(tb["bbdT"].shape), _whole(tb["cbdT"].shape), _whole(dvec.shape)],
        out_specs=out_specs, out_shape=out_shape, scratch_shapes=scratch,
        compiler_params=_params("arbitrary"),
    )(up, y, dz, ck, gcarry_in, tb["lam"], tb["bbd"], tb["bbdT"], tb["cbdT"], dvec)


S5_R = 256


def _s5_carries(ends, first, t_ref, rws, SW, cfx, *, reverse):
    er, ei = ends
    tr, ti = t_ref[rws, :SW][0:1], t_ref[rws, SW:][0:1]
    cr, ci = first
    order = range(NSEG - 1, -1, -1) if reverse else range(NSEG)
    for n, s in enumerate(order):
        if n > 0:
            p = s + 1 if reverse else s - 1
            if reverse:
                cr, ci = tr * cr + ti * ci + er[p:p + 1], tr * ci - ti * cr + ei[p:p + 1]
            else:
                cr, ci = tr * cr - ti * ci + er[p:p + 1], tr * ci + ti * cr + ei[p:p + 1]
        cfx[s:s + 1, :SW] = cr
        cfx[s:s + 1, SW:] = ci


def _s5_fwd2(up, tb, dvec, *, name):
    L, D = up.shape
    R = min(S5_R, L)
    nb, ta, ngb = L // R, R // NSEG, D // 128
    SW = GPB * S5_P
    crows = ngb * NSEG

    def body(u_ref, lam_ref, t_ref, b_ref, c_ref, d_ref, y_ref, z_ref, ck_ref, carry, xbuf, cfx):
        @pl.when(pl.program_id(0) == 0)
        def _():
            carry[...] = jnp.zeros_like(carry)

        zero = jnp.zeros((NSEG, SW), F32)
        for gb in range(ngb):
            cols = slice(gb * 128, (gb + 1) * 128)
            rws = slice(gb * NSEG, (gb + 1) * NSEG)
            ug = u_ref[:, cols]
            xbuf[...] = jnp.dot(ug.astype(BF16), b_ref[gb], preferred_element_type=F32)
            lr, li = lam_ref[rws, :SW], lam_ref[rws, SW:]

            def step(a, c, lr=lr, li=li, store=False):
                cr, ci = c
                o = pl.multiple_of(a * NSEG, NSEG)
                nr = lr * cr - li * ci + xbuf[pl.ds(o, NSEG), :SW]
                ni = lr * ci + li * cr + xbuf[pl.ds(o, NSEG), SW:]
                if store:
                    xbuf[pl.ds(o, NSEG), :SW] = nr
                    xbuf[pl.ds(o, NSEG), SW:] = ni
                return nr, ni

            ends = lax.fori_loop(0, ta, step, (zero, zero), unroll=2)
            prev = (carry[rws, :SW][NSEG - 1:NSEG], carry[rws, SW:][NSEG - 1:NSEG])
            _s5_carries(ends, prev, t_ref, rws, SW, cfx, reverse=False)
            ck_ref[0, rws, :] = cfx[...]
            cr, ci = lax.fori_loop(0, ta, lambda a, c, st=step: st(a, c, store=True), (cfx[:, :SW], cfx[:, SW:]), unroll=2)
            carry[rws, :SW] = cr
            carry[rws, SW:] = ci
            y = jnp.dot(xbuf[...].astype(BF16), c_ref[gb], preferred_element_type=F32) + d_ref[:, cols] * ug
            y_ref[:, cols] = y
            z_ref[:, cols] = _gelu(y).astype(BF16)

    rowblk = pl.BlockSpec((R, D), lambda i: (i, 0))
    return pl.pallas_call(
        body, name=name, grid=(nb,),
        in_specs=[rowblk, _whole(tb["lam"].shape), _whole(tb["lamT"].shape), _whole(tb["bbd"].shape),
                  _whole(tb["cbd"].shape), _whole(dvec.shape)],
        out_specs=[rowblk, rowblk, pl.BlockSpec((1, crows, 2 * SW), lambda i: (i, 0, 0))],
        out_shape=[jax.ShapeDtypeStruct((L, D), F32), jax.ShapeDtypeStruct((L, D), BF16),
                   jax.ShapeDtypeStruct((nb, crows, 2 * SW), F32)],
        scratch_shapes=[pltpu.VMEM((crows, 2 * SW), F32), pltpu.VMEM((R, 2 * SW), F32), pltpu.VMEM((NSEG, 2 * SW), F32)],
        compiler_params=_params("arbitrary"),
    )(up, tb["lam"], tb["lamT"], tb["bbd"], tb["cbd"], dvec)


def _s5_bwd2(up, y, dz, ck, tb, dvec, *, name):
    L, D = up.shape
    R = min(S5_R, L)
    nb, ta, ngb = L // R, R // NSEG, D // 128
    SW = GPB * S5_P
    crows = ngb * NSEG

    def body(u_ref, y_ref, dz_ref, ck_ref, lam_ref, t_ref, b_ref, bt_ref, ct_ref, d_ref,
             du_ref, db_ref, dc_ref, dl_ref, dd_ref, gcarry, xbuf, gbuf, dybuf, cfx):
        @pl.when(pl.program_id(0) == 0)
        def _():
            gcarry[...] = jnp.zeros_like(gcarry)
            db_ref[...] = jnp.zeros_like(db_ref)
            dc_ref[...] = jnp.zeros_like(dc_ref)
            dl_ref[...] = jnp.zeros_like(dl_ref)
            dd_ref[...] = jnp.zeros_like(dd_ref)

        zero = jnp.zeros((NSEG, SW), F32)
        dybuf[...] = dz_ref[...] * _gelu_grad(y_ref[...])
        for gb in range(ngb):
            cols = slice(gb * 128, (gb + 1) * 128)
            rws = slice(gb * NSEG, (gb + 1) * NSEG)
            dyg = dybuf[:, cols]
            ug = u_ref[:, cols]
            lr, li = lam_ref[rws, :SW], lam_ref[rws, SW:]
            gbuf[...] = jnp.dot(dyg.astype(BF16), ct_ref[gb], preferred_element_type=F32)
            xbuf[0:NSEG, :] = ck_ref[0, rws, :]
            xbuf[NSEG:, :] = jnp.dot(ug.astype(BF16), b_ref[gb], preferred_element_type=F32)

            def fstep(a, c, lr=lr, li=li):
                cr, ci = c
                o = pl.multiple_of(a * NSEG + NSEG, NSEG)
                nr = lr * cr - li * ci + xbuf[pl.ds(o, NSEG), :SW]
                ni = lr * ci + li * cr + xbuf[pl.ds(o, NSEG), SW:]
                xbuf[pl.ds(o, NSEG), :SW] = nr
                xbuf[pl.ds(o, NSEG), SW:] = ni
                return nr, ni

            lax.fori_loop(0, ta, fstep, (xbuf[0:NSEG, :SW], xbuf[0:NSEG, SW:]), unroll=2)

            def rstep(k, c, lr=lr, li=li, store=False):
                o = pl.multiple_of((ta - 1 - k) * NSEG, NSEG)
                gr_n, gi_n = c[0], c[1]
                gr = gbuf[pl.ds(o, NSEG), :SW] + lr * gr_n + li * gi_n
                gi = gbuf[pl.ds(o, NSEG), SW:] - li * gr_n + lr * gi_n
                if not store:
                    return gr, gi
                gbuf[pl.ds(o, NSEG), :SW] = gr
                gbuf[pl.ds(o, NSEG), SW:] = gi
                xr, xi = xbuf[pl.ds(o, NSEG), :SW], xbuf[pl.ds(o, NSEG), SW:]
                return gr, gi, c[2] + gr * xr + gi * xi, c[3] + gi * xr - gr * xi

            gends = lax.fori_loop(0, ta, rstep, (zero, zero), unroll=2)
            nxt = (gcarry[rws, :SW][0:1], gcarry[rws, SW:][0:1])
            _s5_carries(gends, nxt, t_ref, rws, SW, cfx, reverse=True)
            cf = lax.fori_loop(0, ta, lambda k, c, st=rstep: st(k, c, store=True),
                               (cfx[:, :SW], cfx[:, SW:], zero, zero), unroll=2)
            gcarry[rws, :SW] = cf[0]
            gcarry[rws, SW:] = cf[1]
            dl_ref[rws, :SW] += cf[2]
            dl_ref[rws, SW:] += cf[3]
            gb16 = gbuf[...].astype(BF16)
            du_ref[:, cols] = jnp.dot(gb16, bt_ref[gb], preferred_element_type=F32) + d_ref[:, cols] * dyg
            db_ref[gb] += lax.dot_general(ug.astype(BF16), gb16, TN, preferred_element_type=F32)
            dc_ref[gb] += lax.dot_general(dyg.astype(BF16), xbuf[NSEG:, :].astype(BF16), TN, preferred_element_type=F32)
            dd_ref[:, cols] += _rsum(dyg * ug)

    rev = pl.BlockSpec((R, D), lambda i: (nb - 1 - i, 0))
    acc3 = (ngb, 128, 2 * SW)
    return pl.pallas_call(
        body, name=name, grid=(nb,),
        in_specs=[rev, rev, rev, pl.BlockSpec((1, crows, 2 * SW), lambda i: (nb - 1 - i, 0, 0)),
                  _whole(tb["lam"].shape), _whole(tb["lamT"].shape), _whole(tb["bbd"].shape),
                  _whole(tb["bbdT"].shape), _whole(tb["cbdT"].shape), _whole(dvec.shape)],
        out_specs=[rev, _whole(acc3), _whole(acc3), _whole((crows, 2 * SW)), _whole((1, D))],
        out_shape=[jax.ShapeDtypeStruct((L, D), F32), jax.ShapeDtypeStruct(acc3, F32), jax.ShapeDtypeStruct(acc3, F32),
                   jax.ShapeDtypeStruct((crows, 2 * SW), F32), jax.ShapeDtypeStruct((1, D), F32)],
        scratch_shapes=[pltpu.VMEM((crows, 2 * SW), F32), pltpu.VMEM((R + NSEG, 2 * SW), F32),
                        pltpu.VMEM((R, 2 * SW), F32), pltpu.VMEM((R, D), F32), pltpu.VMEM((NSEG, 2 * SW), F32)],
        compiler_params=_params("arbitrary"),
    )(up, y, dz, ck, tb["lam"], tb["lamT"], tb["bbd"], tb["bbdT"], tb["cbdT"], dvec)


NN = (((1,), (0,)), ((), ()))
TN = (((0,), (0,)), ((), ()))
NT = (((1,), (1,)), ((), ()))


def _dot3(lhs, rhs, dn, split):
    x = rhs if split == "rhs" else lhs
    hi = x.astype(BF16)
    r1 = x - hi.astype(F32)
    mid = r1.astype(BF16)
    lo = (r1 - mid.astype(F32)).astype(BF16)
    out = None
    for part in (hi, mid, lo):
        ops = (lhs, part) if split == "rhs" else (part, rhs)
        t = lax.dot_general(ops[0], ops[1], dn, preferred_element_type=F32)
        out = t if out is None else out + t
    return out


def _log_sigmoid(x):
    return jnp.minimum(x, 0.0) - jnp.log(1.0 + jnp.exp(-jnp.abs(x)))


def _gla_gates(p, wg_ref, bg_ref, QK):
    C = p.shape[0]
    glr = p[:, 6 * QK:6 * QK + 128].astype(BF16)
    gpre = jnp.dot(glr, wg_ref[...], preferred_element_type=F32) + bg_ref[...]
    la = _log_sigmoid(gpre) * (1.0 / GATE_TAU)
    row = lax.broadcasted_iota(jnp.int32, (C, C), 0)
    col = lax.broadcasted_iota(jnp.int32, (C, C), 1)
    gc = _dot3((row >= col).astype(BF16), la, NN, "rhs")
    ge = gc[C - 1:C, :]
    w = jnp.exp(ge - gc)
    return glr, gpre, la, ge, w


def _gla_fwd(proj, wg2p, bg, gn, *, name):
    L = proj.shape[0]
    QK = wg2p.shape[1]
    DK, DV = QK // HEADS, 2 * QK // HEADS
    nC = L // CHUNK
    NB = min(GLA_NB, nC)
    assert nC % NB == 0
    scale = DK ** -0.5

    def body(p_ref, wg_ref, bg_ref, gn_ref, og_ref, s_ref, sst):
        @pl.when(pl.program_id(0) == 0)
        def _():
            sst[...] = jnp.zeros_like(sst)

        ones = jnp.ones((CHUNK, DV), BF16)
        for cc in range(NB):
            rows = slice(cc * CHUNK, (cc + 1) * CHUNK)
            p = p_ref[rows, :]
            _, _, la, _, w = _gla_gates(p, wg_ref, bg_ref, QK)
            for h in range(HEADS):
                ks, vs = slice(h * DK, (h + 1) * DK), slice(h * DV, (h + 1) * DV)
                q = p[:, h * DK:(h + 1) * DK] * scale
                kd = p[:, QK + h * DK:QK + (h + 1) * DK] * w[:, ks]
                v = p[:, 2 * QK + h * DV:2 * QK + (h + 1) * DV]
                r = p[:, 4 * QK + h * DV:4 * QK + (h + 1) * DV]
                dec = jnp.exp(_dot3(la[:, ks], ones, TN, "lhs"))
                kv = lax.dot_general(kd.astype(BF16), v.astype(BF16), TN, preferred_element_type=F32)
                S = dec * sst[ks, :] + kv
                sst[ks, :] = S
                s_ref[cc, ks, :] = S
                o = jnp.dot(q.astype(BF16), S.astype(BF16), preferred_element_type=F32)
                on = o * lax.rsqrt(jnp.mean(o * o, axis=-1, keepdims=True) + EPS)
                og_ref[rows, vs] = (on * gn_ref[:, vs] * (r * _sigmoid(r))).astype(BF16)

    RB = NB * CHUNK
    return pl.pallas_call(
        body, name=name, grid=(nC // NB,),
        in_specs=[pl.BlockSpec((RB, proj.shape[1]), lambda i: (i, 0)), _whole(wg2p.shape), _whole(bg.shape), _whole(gn.shape)],
        out_specs=[pl.BlockSpec((RB, 2 * QK), lambda i: (i, 0)), pl.BlockSpec((NB, QK, DV), lambda i: (i, 0, 0))],
        out_shape=[jax.ShapeDtypeStruct((L, 2 * QK), BF16), jax.ShapeDtypeStruct((nC, QK, DV), F32)],
        scratch_shapes=[pltpu.VMEM((QK, DV), F32)],
        compiler_params=_params("arbitrary"),
    )(proj, wg2p, bg, gn)


def _gla_bwd(proj, dog, states, wg2p, bg, gn, *, name):
    L, W = proj.shape
    QK = wg2p.shape[1]
    DK, DV = QK // HEADS, 2 * QK // HEADS
    nC = L // CHUNK
    NB = min(GLA_NB, nC)
    nB = nC // NB
    scale = DK ** -0.5

    def body(p_ref, dog_ref, sc_ref, sp_ref, wg_ref, bg_ref, gn_ref, dp_ref, dwg_ref, dbg_ref, dgn_ref, gst):
        i = pl.program_id(0)

        @pl.when(i == 0)
        def _():
            gst[...] = jnp.zeros_like(gst)
            dwg_ref[...] = jnp.zeros_like(dwg_ref)
            dbg_ref[...] = jnp.zeros_like(dbg_ref)
            dgn_ref[...] = jnp.zeros_like(dgn_ref)

        row = lax.broadcasted_iota(jnp.int32, (CHUNK, CHUNK), 0)
        col = lax.broadcasted_iota(jnp.int32, (CHUNK, CHUNK), 1)
        tri_u = (col >= row).astype(BF16)
        ones = jnp.ones((CHUNK, DV), BF16)
        ones8 = jnp.ones((8, DV), BF16)
        for cc in range(NB - 1, -1, -1):
            rows = slice(cc * CHUNK, (cc + 1) * CHUNK)
            p = p_ref[rows, :]
            glr, gpre, la, ge, w = _gla_gates(p, wg_ref, bg_ref, QK)
            dla_heads = []
            for h in range(HEADS):
                ks, vs = slice(h * DK, (h + 1) * DK), slice(h * DV, (h + 1) * DV)
                qs = p[:, h * DK:(h + 1) * DK] * scale
                k = p[:, QK + h * DK:QK + (h + 1) * DK]
                v = p[:, 2 * QK + h * DV:2 * QK + (h + 1) * DV]
                r = p[:, 4 * QK + h * DV:4 * QK + (h + 1) * DV]
                wh = w[:, ks]
                kd = k * wh
                S = sc_ref[cc, ks, :]
                if cc > 0:
                    Sp = sc_ref[cc - 1, ks, :]
                else:
                    Sp = jnp.where(i < nB - 1, sp_ref[0, ks, :], 0.0)
                o = jnp.dot(qs.astype(BF16), S.astype(BF16), preferred_element_type=F32)
                rs = lax.rsqrt(jnp.mean(o * o, axis=-1, keepdims=True) + EPS)
                on = o * rs
                sr = _sigmoid(r)
                dg = dog_ref[rows, vs]
                gnh = gn_ref[:, vs]
                dp_ref[rows, 4 * QK + h * DV:4 * QK + (h + 1) * DV] = (
                    dg * on * gnh * (sr * (1.0 + r * (1.0 - sr)))).astype(BF16)
                dt = dg * (r * sr)
                dgn_ref[:, vs] += _rsum(dt * on)
                don = dt * gnh
                do = (rs * (don - on * jnp.mean(don * on, axis=-1, keepdims=True))).astype(BF16)
                Gc = gst[ks, :] + lax.dot_general(qs.astype(BF16), do, TN, preferred_element_type=F32)
                G16 = Gc.astype(BF16)
                dp_ref[rows, h * DK:(h + 1) * DK] = (
                    lax.dot_general(do, S.astype(BF16), NT, preferred_element_type=F32) * scale).astype(BF16)
                dkd = lax.dot_general(v.astype(BF16), G16, NT, preferred_element_type=F32)
                dp_ref[rows, 2 * QK + h * DV:2 * QK + (h + 1) * DV] = jnp.dot(
                    kd.astype(BF16), G16, preferred_element_type=F32).astype(BF16)
                gst[ks, :] = jnp.exp(_dot3(la[:, ks], ones, TN, "lhs")) * Gc
                ddec = _dot3(ones8, Gc * Sp, NT, "rhs")[0:1, :]
                dp_ref[rows, QK + h * DK:QK + (h + 1) * DK] = (dkd * wh).astype(BF16)
                dww = dkd * kd
                dge = jnp.exp(ge[:, ks]) * ddec + _rsum(dww)
                dla_heads.append(dge - _dot3(tri_u, dww, NN, "rhs"))
            dla = jnp.concatenate(dla_heads, axis=1)
            dgpre = dla * (1.0 / GATE_TAU) * (1.0 - _sigmoid(gpre))
            d16 = dgpre.astype(BF16)
            dp_ref[rows, 6 * QK:6 * QK + 128] = lax.dot_general(d16, wg_ref[...], NT, preferred_element_type=F32).astype(BF16)
            dwg_ref[...] += lax.dot_general(glr, d16, TN, preferred_element_type=F32)
            dbg_ref[...] += _rsum(dgpre)

    RB = NB * CHUNK
    rev = lambda i: (nB - 1 - i, 0)
    return pl.pallas_call(
        body, name=name, grid=(nB,),
        in_specs=[pl.BlockSpec((RB, W), rev), pl.BlockSpec((RB, 2 * QK), rev),
                  pl.BlockSpec((NB, QK, DV), lambda i: (nB - 1 - i, 0, 0)),
                  pl.BlockSpec((1, QK, DV), lambda i: (jnp.maximum(NB * (nB - 1 - i) - 1, 0), 0, 0)),
                  _whole(wg2p.shape), _whole(bg.shape), _whole(gn.shape)],
        out_specs=[pl.BlockSpec((RB, W), rev), _whole((128, QK)), _whole((1, QK)), _whole((1, 2 * QK))],
        out_shape=[jax.ShapeDtypeStruct((L, W), BF16), jax.ShapeDtypeStruct((128, QK), F32),
                   jax.ShapeDtypeStruct((1, QK), F32), jax.ShapeDtypeStruct((1, 2 * QK), F32)],
        scratch_shapes=[pltpu.VMEM((QK, DV), F32)],
        compiler_params=_params("arbitrary"),
    )(proj, dog, states, states, wg2p, bg, gn)


def _coords():
    return lax.axis_index("x"), lax.axis_index("y"), lax.axis_index("c")


def _other_chips(x, y):
    return [(1 - x, y, 2 * (1 - x) + y), (x, 1 - y, 2 * x + 1 - y), (1 - x, 1 - y, 2 * (1 - x) + 1 - y)]


def _hbm_call(body, ins, out_shapes, n_sems, *, name, alias=False):
    any_spec = pl.BlockSpec(memory_space=pl.ANY)
    return pl.pallas_call(
        body, name=name, in_specs=[any_spec] * len(ins), out_specs=[any_spec] * len(out_shapes), out_shape=out_shapes,
        scratch_shapes=[pltpu.SemaphoreType.DMA((n,)) for n in n_sems],
        input_output_aliases={k: k for k in range(len(ins))} if alias else {},
    )(*ins)


def _exchange(src, masks, *, name):
    vary = [any(m[k] for m in masks) for k in range(3)]
    nslots = 2 ** sum(vary)
    n = len(masks)

    def slot(coords):
        s = 0
        for k in range(3):
            if vary[k]:
                s = s * 2 + coords[k]
        return s

    def body(src_ref, dst_ref, send_sems, recv_sems, loc_sem):
        me = _coords()
        mine = slot(me)
        loc = pltpu.make_async_copy(src_ref, dst_ref.at[mine], loc_sem.at[0])
        loc.start()
        copies = []
        for k, m in enumerate(masks):
            peer = tuple(1 - me[d] if m[d] else me[d] for d in range(3))
            cp = pltpu.make_async_remote_copy(src_ref=src_ref, dst_ref=dst_ref.at[mine], send_sem=send_sems.at[k],
                                              recv_sem=recv_sems.at[k], device_id=peer, device_id_type=MESH)
            cp.start()
            copies.append(cp)
        for cp in copies:
            cp.wait()
        loc.wait()

    return _hbm_call(body, [src], [jax.ShapeDtypeStruct((nslots,) + tuple(src.shape), src.dtype)], (n, n, 1), name=name)[0]


def _cast_into(t, lead, kind, chip, *, name, tm=256):
    r, cc = t.shape[-2:]
    tm = min(tm, r)
    nblk = r // tm
    if kind == "col":
        shp, o_spec = (r, NCH * cc), pl.BlockSpec((tm, cc), lambda i, s: (i, s[0]))
    elif kind == "row":
        shp, o_spec = (NCH * r, cc), pl.BlockSpec((tm, cc), lambda i, s: (s[0] * nblk + i, 0))
    else:
        shp, o_spec = (NCH, r, cc), pl.BlockSpec((None, tm, cc), lambda i, s: (s[0], i, 0))

    def body(s_ref, t_ref, o_ref):
        o_ref[...] = t_ref[...].astype(o_ref.dtype)

    return pl.pallas_call(
        body, name=name,
        grid_spec=pltpu.PrefetchScalarGridSpec(
            num_scalar_prefetch=1, grid=(nblk,),
            in_specs=[pl.BlockSpec((None, tm, cc), lambda i, s: (lead, i, 0))], out_specs=o_spec),
        out_shape=jax.ShapeDtypeStruct(shp, BF16), compiler_params=_params("parallel"),
    )(chip.reshape(1).astype(jnp.int32), t)


def _gather_weights(arrs, shard_shapes, kinds, *, name):
    n = len(arrs)

    def body(*refs):
        dst = refs[n:2 * n]
        send_sems, recv_sems = refs[2 * n:]
        x, y, c = _coords()
        chip = 2 * x + y
        others = _other_chips(x, y)
        sib = (x, y, 1 - c)

        def window(p, chip_id, cc):
            r, cols = shard_shapes[p]
            h = r // 2
            if kinds[p] == "col":
                return dst[p].at[pl.ds(cc * h, h), pl.ds(pl.multiple_of(chip_id * cols, 128), cols)]
            if kinds[p] == "row":
                return dst[p].at[pl.ds(chip_id * r + cc * h, h), :]
            return dst[p].at[chip_id, pl.ds(cc * h, h), :]

        def copy(p, k, win, to):
            return pltpu.make_async_remote_copy(src_ref=win, dst_ref=win, send_sem=send_sems.at[6 * p + k],
                                                recv_sem=recv_sems.at[6 * p + k], device_id=to, device_id_type=MESH)

        sends = []
        for p in range(n):
            for j, (ox, oy, _) in enumerate(others):
                cp = copy(p, j, window(p, chip, c), (ox, oy, c))
                cp.start()
                sends.append(cp)
        for j, (_, _, oc) in enumerate(others):
            for p in range(n):
                copy(p, j, window(p, oc, c), (x, y, c)).wait_recv()
                fw = copy(p, 3 + j, window(p, oc, c), sib)
                fw.start()
                sends.append(fw)
        for p in range(n):
            for j, (_, _, oc) in enumerate(others):
                copy(p, 3 + j, window(p, oc, 1 - c), sib).wait_recv()
        for cp in sends:
            cp.wait_send()

    outs = [jax.ShapeDtypeStruct(a.shape, a.dtype) for a in arrs]
    return _hbm_call(body, arrs, outs, (6 * n, 6 * n), name=name, alias=True)


HBM_SPEC = pl.BlockSpec(memory_space=pltpu.HBM)
SEM_SPEC = pl.BlockSpec(memory_space=pltpu.SEMAPHORE)
EFFECT = pltpu.SideEffectType.DATAFLOW_SIDE_EFFECTING


def _window(ref, shard_shape, kind, chip_id, cc):
    r, cols = shard_shape
    h = r // 2
    if kind == "col":
        return ref.at[pl.ds(cc * h, h), pl.ds(pl.multiple_of(chip_id * cols, 128), cols)]
    if kind == "row":
        return ref.at[pl.ds(chip_id * r + cc * h, h), :]
    return ref.at[chip_id, pl.ds(cc * h, h), :]


def _split_start(start, arrs, n_sems, *, name):
    n, ns = len(arrs), len(n_sems)

    def body(*refs):
        start(refs[:n], refs[n:n + ns])
        refs[-1][...] = jnp.zeros_like(refs[-1])

    outs = pl.pallas_call(
        body, name=name,
        out_shape=tuple([pltpu.SemaphoreType.DMA((k,)) for k in n_sems] + [pltpu.HBM(a.shape, a.dtype) for a in arrs]
                        + [jax.ShapeDtypeStruct((8, 128), F32)]),
        in_specs=[HBM_SPEC] * n, out_specs=tuple([SEM_SPEC] * ns + [HBM_SPEC] * n + [pl.BlockSpec(memory_space=pltpu.VMEM)]),
        input_output_aliases={k: ns + k for k in range(n)},
        compiler_params=pltpu.CompilerParams(has_side_effects=EFFECT),
    )(*[pltpu.with_memory_space_constraint(a, pltpu.HBM) for a in arrs])
    return list(outs[:ns]), list(outs[ns:ns + n]), outs[-1]


def _split_wait(wait, arrs, sems, after, *, name):
    n, ns = len(arrs), len(sems)

    def body(*refs):
        wait(refs[:n], refs[n:n + ns])

    return pl.pallas_call(
        body, name=name, out_shape=tuple(pltpu.HBM(a.shape, a.dtype) for a in arrs),
        in_specs=[HBM_SPEC] * n + [SEM_SPEC] * ns + [pl.BlockSpec(memory_space=pl.ANY)], out_specs=tuple([HBM_SPEC] * n),
        input_output_aliases={k: k for k in range(n)},
        compiler_params=pltpu.CompilerParams(has_side_effects=EFFECT),
    )(*arrs, *sems, after)


def _gw_copies(refs, send_sems, recv_sems, shard_shapes, kinds, outgoing):
    x, y, c = _coords()
    chip = 2 * x + y
    out = []
    for p in range(len(refs)):
        for j, (ox, oy, oc) in enumerate(_other_chips(x, y)):
            win = _window(refs[p], shard_shapes[p], kinds[p], chip if outgoing else oc, c)
            out.append(pltpu.make_async_remote_copy(
                src_ref=win, dst_ref=win, send_sem=send_sems.at[3 * p + j], recv_sem=recv_sems.at[3 * p + j],
                device_id=(ox, oy, c), device_id_type=MESH))
    return out


def _gw_start(arrs, shard_shapes, kinds, groups, *, name):
    def start(refs, sems):
        for g, idx in enumerate(groups):
            for cp in _gw_copies([refs[p] for p in idx], sems[2 * g], sems[2 * g + 1], [shard_shapes[p] for p in idx],
                                 [kinds[p] for p in idx], True):
                cp.start()

    n_sems = [3 * len(idx) for idx in groups for _ in range(2)]
    sems, thru, token = _split_start(start, arrs, n_sems, name=name)
    return [(sems[2 * g], sems[2 * g + 1]) for g in range(len(groups))], thru, token


def _gw_wait(arrs, shard_shapes, kinds, sem_pair, after, *, name):
    def wait(refs, sems):
        for cp in _gw_copies(refs, sems[0], sems[1], shard_shapes, kinds, True):
            cp.wait_send()
        for cp in _gw_copies(refs, sems[0], sems[1], shard_shapes, kinds, False):
            cp.wait_recv()

    return _split_wait(wait, arrs, list(sem_pair), after, name=name)


def _gw_forward(arrs, shard_shapes, kinds, *, name):
    n = len(arrs)

    def body(*refs):
        dst = refs[n:2 * n]
        send_sems, recv_sems = refs[2 * n:]
        x, y, c = _coords()
        sends = []
        for p in range(n):
            for j, (_, _, oc) in enumerate(_other_chips(x, y)):
                win = _window(dst[p], shard_shapes[p], kinds[p], oc, c)
                cp = pltpu.make_async_remote_copy(src_ref=win, dst_ref=win, send_sem=send_sems.at[3 * p + j],
                                                  recv_sem=recv_sems.at[3 * p + j], device_id=(x, y, 1 - c),
                                                  device_id_type=MESH)
                cp.start()
                sends.append(cp)
        for p in range(n):
            for j, (_, _, oc) in enumerate(_other_chips(x, y)):
                win = _window(dst[p], shard_shapes[p], kinds[p], oc, 1 - c)
                pltpu.make_async_remote_copy(src_ref=win, dst_ref=win, send_sem=send_sems.at[3 * p + j],
                                             recv_sem=recv_sems.at[3 * p + j], device_id=(x, y, 1 - c),
                                             device_id_type=MESH).wait_recv()
        for cp in sends:
            cp.wait_send()

    outs = [jax.ShapeDtypeStruct(a.shape, a.dtype) for a in arrs]
    return _hbm_call(body, arrs, outs, (3 * n, 3 * n), name=name, alias=True)


def _rs_chips_copies(parts, lands, send_sems, recv_sems):
    x, y, c = _coords()
    chip = 2 * x + y
    out = []
    for p in range(len(parts)):
        for j, (ox, oy, oc) in enumerate(_other_chips(x, y)):
            out.append(pltpu.make_async_remote_copy(
                src_ref=parts[p].at[oc], dst_ref=lands[p].at[chip], send_sem=send_sems.at[3 * p + j],
                recv_sem=recv_sems.at[3 * p + j], device_id=(ox, oy, c), device_id_type=MESH))
    return out


def _rs_chips_start(parts, *, name):
    n = len(parts)

    def start(refs, sems):
        for cp in _rs_chips_copies(refs[:n], refs[n:], sems[0], sems[1]):
            cp.start()

    lands = [lax.empty(t.shape, t.dtype) for t in parts]
    sems, thru, token = _split_start(start, list(parts) + lands, [3 * n, 3 * n], name=name)
    return (sems[0], sems[1]), thru[:n], thru[n:], token


def _rs_chips_wait(groups, after, *, name):
    sizes = [len(g[1]) for g in groups]
    arrs = [a for g in groups for a in list(g[1]) + list(g[2])]
    sems = [s for g in groups for s in g[0]]

    def wait(refs, sem_refs):
        o = 0
        for k, n in enumerate(sizes):
            for cp in _rs_chips_copies(refs[o:o + n], refs[o + n:o + 2 * n], sem_refs[2 * k], sem_refs[2 * k + 1]):
                cp.wait()
            o += 2 * n

    outs = _split_wait(wait, arrs, sems, after, name=name)
    res, o = [], 0
    for n in sizes:
        res.append((list(outs[o:o + n]), list(outs[o + n:o + 2 * n])))
        o += 2 * n
    return res


def _rs_cores_copies(grads, lands, send_sems, recv_sems):
    x, y, c = _coords()
    out, o = [], 0
    for p in range(len(grads)):
        nsh, h = lands[p].shape[0], lands[p].shape[1]
        for j in range(nsh):
            out.append(pltpu.make_async_remote_copy(
                src_ref=grads[p].at[j, pl.ds((1 - c) * h, h), :], dst_ref=lands[p].at[j],
                send_sem=send_sems.at[o + j], recv_sem=recv_sems.at[o + j], device_id=(x, y, 1 - c), device_id_type=MESH))
        o += nsh
    return out


def _rs_cores_start(grads, *, name):
    n = len(grads)
    tot = sum(g.shape[0] for g in grads)

    def start(refs, sems):
        for cp in _rs_cores_copies(refs[:n], refs[n:], sems[0], sems[1]):
            cp.start()

    lands = [lax.empty((g.shape[0], g.shape[1] // 2, g.shape[2]), g.dtype) for g in grads]
    sems, thru, token = _split_start(start, list(grads) + lands, [tot, tot], name=name)
    return (sems[0], sems[1]), thru[:n], thru[n:], token


def _rs_cores_wait(pair, grads, lands, after, *, name):
    n = len(grads)

    def wait(refs, sems):
        for cp in _rs_cores_copies(refs[:n], refs[n:], sems[0], sems[1]):
            cp.wait()

    outs = _split_wait(wait, list(grads) + list(lands), list(pair), after, name=name)
    return list(outs[:n]), list(outs[n:])


def _rs_cores(grads, *, name):
    n = len(grads)
    outs = [jax.ShapeDtypeStruct((g.shape[0], g.shape[1] // 2, g.shape[2]), g.dtype) for g in grads]

    def body(*refs):
        src, dst = refs[:n], refs[n:2 * n]
        send_sems, recv_sems = refs[2 * n:]
        x, y, c = _coords()
        copies = []
        for p in range(n):
            nsh, r, _ = grads[p].shape
            h = r // 2
            for j in range(nsh):
                cp = pltpu.make_async_remote_copy(
                    src_ref=src[p].at[j, pl.ds((1 - c) * h, h), :], dst_ref=dst[p].at[j],
                    send_sem=send_sems.at[nsh * p + j], recv_sem=recv_sems.at[nsh * p + j],
                    device_id=(x, y, 1 - c), device_id_type=MESH)
                cp.start()
                copies.append(cp)
        for cp in copies:
            cp.wait()

    tot = sum(g.shape[0] for g in grads)
    return _hbm_call(body, grads, outs, (tot, tot), name=name)


def _sum_own_half(full, recv, ci, out_dtype, *, name):
    nsh, h, cols = recv.shape
    tm = h if nsh * h * cols * 4 <= (2 << 20) else _tile_rows(h, 256)
    nblk = h // tm

    def body(c_ref, f_ref, r_ref, o_ref):
        o_ref[...] = (f_ref[...] + r_ref[...]).astype(o_ref.dtype)

    return pl.pallas_call(
        body, name=name,
        grid_spec=pltpu.PrefetchScalarGridSpec(
            num_scalar_prefetch=1, grid=(nsh, nblk),
            in_specs=[pl.BlockSpec((1, tm, cols), lambda j, i, c_ref: (j, c_ref[0] * nblk + i, 0)),
                      pl.BlockSpec((1, tm, cols), lambda j, i, c_ref: (j, i, 0))],
            out_specs=pl.BlockSpec((1, tm, cols), lambda j, i, c_ref: (j, i, 0))),
        out_shape=jax.ShapeDtypeStruct((nsh, h, cols), out_dtype), compiler_params=_params("parallel", "parallel"),
    )(ci.reshape(1).astype(jnp.int32), full, recv)


def _rs_chips(parts, *, name):
    n = len(parts)
    outs = [jax.ShapeDtypeStruct(t.shape, t.dtype) for t in parts]

    def body(*refs):
        src, dst = refs[:n], refs[n:2 * n]
        send_sems, recv_sems = refs[2 * n:]
        x, y, c = _coords()
        chip = 2 * x + y
        copies = []
        for p in range(n):
            for j, (ox, oy, oc) in enumerate(_other_chips(x, y)):
                cp = pltpu.make_async_remote_copy(
                    src_ref=src[p].at[oc], dst_ref=dst[p].at[chip], send_sem=send_sems.at[3 * p + j],
                    recv_sem=recv_sems.at[3 * p + j], device_id=(ox, oy, c), device_id_type=MESH)
                cp.start()
                copies.append(cp)
        for cp in copies:
            cp.wait()

    return _hbm_call(body, parts, outs, (3 * n, 3 * n), name=name)


def _sum_chips(recv, own, chip, ci, *, name, nlead=1, lead=0, prev=None, spread=False):
    nsh, h, cols = recv.shape
    tm = h if nsh * h * cols * 4 <= (2 << 20) else _tile_rows(h, 256)
    nblk = h // tm
    rows_out = 2 * h * (nsh if spread else 1)

    def body(s_ref, r_ref, o_ref, *rest):
        out_ref = rest[-1]
        t = None
        for s in range(nsh):
            v = jnp.where(s_ref[0] == s, o_ref[s], r_ref[s]).astype(F32)
            t = v if t is None else t + v
        out_ref[...] = t

    def out_idx(i, s):
        return (lead, (s[0] * 2 * nblk if spread else 0) + s[1] * nblk + i, 0)

    blk = pl.BlockSpec((nsh, tm, cols), lambda i, s: (0, i, 0))
    ins = [recv, own] + ([prev] if prev is not None else [])
    return pl.pallas_call(
        body, name=name,
        grid_spec=pltpu.PrefetchScalarGridSpec(
            num_scalar_prefetch=1, grid=(nblk,),
            in_specs=[blk, blk] + ([pl.BlockSpec(memory_space=pl.ANY)] if prev is not None else []),
            out_specs=pl.BlockSpec((None, tm, cols), out_idx)),
        out_shape=jax.ShapeDtypeStruct((nlead, rows_out, cols), F32),
        input_output_aliases={3: 0} if prev is not None else {},
        compiler_params=_params("arbitrary"),
    )(jnp.stack([chip, ci]).astype(jnp.int32), *ins)


def _rs_gather(arrs, halves, spread, *, name, nchunk=4):
    n = len(arrs)
    per = [a.shape[0] * nchunk for a in arrs]
    offs = [sum(per[:p]) for p in range(n)]

    def body(*refs):
        dst = refs[n:2 * n]
        send_sems, recv_sems = refs[2 * n:]
        x, y, c = _coords()
        chip = 2 * x + y
        copies = []
        for p in range(n):
            h = halves[p]
            q = h // nchunk
            base = chip * 2 * h if spread[p] else 0
            for l in range(arrs[p].shape[0]):
                for k in range(nchunk):
                    win = dst[p].at[l, pl.ds(base + c * h + k * q, q), :]
                    sem = offs[p] + l * nchunk + k
                    cp = pltpu.make_async_remote_copy(src_ref=win, dst_ref=win, send_sem=send_sems.at[sem],
                                                      recv_sem=recv_sems.at[sem], device_id=(x, y, 1 - c),
                                                      device_id_type=MESH)
                    cp.start()
                    copies.append(cp)
        for cp in copies:
            cp.wait_send()
        for p in range(n):
            h = halves[p]
            q = h // nchunk
            base = chip * 2 * h if spread[p] else 0
            for l in range(arrs[p].shape[0]):
                for k in range(nchunk):
                    win = dst[p].at[l, pl.ds(base + (1 - c) * h + k * q, q), :]
                    sem = offs[p] + l * nchunk + k
                    pltpu.make_async_remote_copy(src_ref=win, dst_ref=win, send_sem=send_sems.at[sem],
                                                 recv_sem=recv_sems.at[sem], device_id=(x, y, 1 - c),
                                                 device_id_type=MESH).wait_recv()

    outs = [jax.ShapeDtypeStruct(a.shape, a.dtype) for a in arrs]
    return _hbm_call(body, arrs, outs, (sum(per), sum(per)), name=name, alias=True)


def _adamw(w, g, m, v, *, name):
    nl, R, C = w.shape
    tm = _tile_rows(R, 256)

    blk = pl.BlockSpec((None, tm, C), lambda l, i: (l, i, 0))
    return pl.pallas_call(
        _adamw_body_copy(), name=name, grid=(nl, R // tm), in_specs=[blk] * 4, out_specs=[blk] * 3,
        out_shape=[jax.ShapeDtypeStruct((nl, R, C), F32)] * 3, compiler_params=_params("parallel", "parallel"),
    )(w, g, m, v)


def _adamw_body(w_ref, g_ref, m_ref, v_ref, d_ref, nm_ref, nv_ref):
    gg = g_ref[...]
    nm = B1 * m_ref[...] + (1.0 - B1) * gg
    nv = B2 * v_ref[...] + (1.0 - B2) * (gg * gg)
    m_hat = nm / (1.0 - B1 ** ASTEP)
    v_hat = nv / (1.0 - B2 ** ASTEP)
    d_ref[...] = -LR * (m_hat / (jnp.sqrt(v_hat) + AEPS) + WD * w_ref[...])
    nm_ref[...] = nm
    nv_ref[...] = nv


def _adamw_whole(w, g, m, v, *, name):
    return pl.pallas_call(_adamw_body_copy(), name=name, out_shape=[jax.ShapeDtypeStruct(w.shape, F32)] * 3,
                          compiler_params=_params())(w, g, m, v)


def _adamw_body_copy():
    def body(*refs):
        _adamw_body(*refs)
    return body


def _mod_cols(c_all, w_ada, b_cols, *, name):
    nl, D, cols = w_ada.shape
    B = c_all.shape[0]

    def body(c_ref, w_ref, b_ref, o_ref):
        cc = c_ref[...]
        cs = (cc * _sigmoid(cc)).astype(BF16)
        o_ref[0] = jnp.dot(cs, w_ref[0].astype(BF16), preferred_element_type=F32) + b_ref[0]

    return pl.pallas_call(
        body, name=name, grid=(nl,),
        in_specs=[_whole(c_all.shape), pl.BlockSpec((1, D, cols), lambda i: (i, 0, 0)), pl.BlockSpec((1, 1, cols), lambda i: (i, 0, 0))],
        out_specs=pl.BlockSpec((1, B, cols), lambda i: (i, 0, 0)),
        out_shape=jax.ShapeDtypeStruct((nl, B, cols), F32), compiler_params=_params("arbitrary"),
    )(c_all, w_ada, b_cols)


def _ada_grad(c_all, dmod_cols, *, name):
    nl, B, cols = dmod_cols.shape
    D = c_all.shape[1]

    def body(c_ref, d_ref, o_ref):
        cc = c_ref[...]
        cs = (cc * _sigmoid(cc)).astype(BF16)
        o_ref[0] = lax.dot_general(cs, d_ref[0].astype(BF16), TN, preferred_element_type=F32)

    return pl.pallas_call(
        body, name=name, grid=(nl,),
        in_specs=[_whole(c_all.shape), pl.BlockSpec((1, B, cols), lambda i: (i, 0, 0))],
        out_specs=pl.BlockSpec((1, D, cols), lambda i: (i, 0, 0)),
        out_shape=jax.ShapeDtypeStruct((nl, D, cols), F32), compiler_params=_params("arbitrary"),
    )(c_all, dmod_cols)


def _s5_disc(a_re, a_im, log_dt, b_re, b_im):
    dt = jnp.exp(log_dt)[:, None]
    mag = jnp.exp(a_re * dt)
    ph = a_im * dt
    lb_re = mag * jnp.cos(ph)
    lb_im = mag * jnp.sin(ph)
    den = a_re * a_re + a_im * a_im
    nr = lb_re - 1.0
    ni = lb_im
    f_re = (nr * a_re + ni * a_im) / den
    f_im = (ni * a_re - nr * a_im) / den
    bb_re = f_re[..., None] * b_re - f_im[..., None] * b_im
    bb_im = f_re[..., None] * b_im + f_im[..., None] * b_re
    return lb_re, lb_im, bb_re, bb_im


def _to_segments(t):
    L, D = t.shape
    R = min(S5_R, L)
    return t.reshape(L // R, NSEG, R // NSEG, D).transpose(0, 2, 1, 3).reshape(L, D)


def _from_segments(t):
    L, D = t.shape
    R = min(S5_R, L)
    return t.reshape(L // R, R // NSEG, NSEG, D).transpose(0, 2, 1, 3).reshape(L, D)


def _mlp_fwd(h2, w1, w2, tag):
    a = _matmul(h2, w1, name=f"ff1_{tag}", out_dtypes=(BF16,), epi=lambda acc: (jnp.maximum(acc, 0.0),))
    f = _matmul(a, w2, name=f"ff2_{tag}", a_fn=jnp.square)
    return a, f


def _mlp_bwd(df, h2, a, w1, w2, tag):
    da = _matmul(df, w2, tb=True, name=f"ff2_dx_{tag}", out_dtypes=(BF16,), epi_ins=(a,),
                 epi=lambda acc, at: (acc * (2.0 * at.astype(F32)),))
    dw2 = _matmul(a, df, ta=True, name=f"ff2_dw_{tag}", a_fn=jnp.square)
    dh2 = _matmul(da, w1, tb=True, name=f"ff1_dx_{tag}")
    dw1 = _matmul(h2, da, ta=True, name=f"ff1_dw_{tag}", col_shards=NCH)
    return dh2, dw1, dw2


def kernel(x, c, w_ada, b_ada, norm_mix, norm_mlp, s5_a_re, s5_a_im, s5_log_dt, s5_b_re, s5_b_im, s5_c_re, s5_c_im, s5_d, s5_w_glu, gla_w_in, gla_w_gate2, gla_b_gate, gla_g_norm, gla_w_out, w_ff1, w_ff2, norm_final, loss_target, m_w_ada, m_b_ada, m_norm_mix, m_norm_mlp, m_s5_a_re, m_s5_a_im, m_s5_log_dt, m_s5_b_re, m_s5_b_im, m_s5_c_re, m_s5_c_im, m_s5_d, m_s5_w_glu, m_gla_w_in, m_gla_w_gate2, m_gla_b_gate, m_gla_g_norm, m_gla_w_out, m_w_ff1, m_w_ff2, m_norm_final, v_w_ada, v_b_ada, v_norm_mix, v_norm_mlp, v_s5_a_re, v_s5_a_im, v_s5_log_dt, v_s5_b_re, v_s5_b_im, v_s5_c_re, v_s5_c_im, v_s5_d, v_s5_w_glu, v_gla_w_in, v_gla_w_gate2, v_gla_b_gate, v_gla_g_norm, v_gla_w_out, v_w_ff1, v_w_ff2, v_norm_final):
    args = dict(locals())
    L, D = x.shape[1], x.shape[2]
    QK = D // 2
    xi, yi, ci = _coords()
    chip = 2 * xi + yi
    dev = 2 * chip + ci

    cat = jnp.concatenate([gla_w_gate2[0].reshape(1, -1), gla_b_gate, gla_g_norm], axis=1)
    first = _exchange(jnp.concatenate([c.reshape(8, D // 8), jnp.tile(cat, (8, 1))], axis=1), MASK_ALL, name="gather_c")
    c_all = first[:, :, :D // 8].reshape(8, D)
    cat_all = first[0::2, 0, D // 8:]
    acols = w_ada.shape[2]
    b_cols = lax.dynamic_slice_in_dim(b_ada, chip * acols, acols, axis=1)[:, None, :]
    mod_cols = _mod_cols(c_all, w_ada, b_cols, name="ada_mod")
    mod_all = _exchange(mod_cols.reshape(16, acols), MASK_CHIPS, name="gather_mod")
    mod_all = mod_all.reshape(NCH, 2, 8, acols).transpose(1, 2, 0, 3).reshape(2, 8, NCH * acols)
    mod = lax.dynamic_index_in_dim(mod_all, dev, axis=1, keepdims=False).reshape(2, 6, 1, D)

    big = [("s5_w_glu", s5_w_glu, 0, "col"), ("gla_w_in", gla_w_in, 0, "slot"), ("gla_w_out", gla_w_out, 0, "row"),
           ("w_ff1_0", w_ff1, 0, "col"), ("w_ff1_1", w_ff1, 1, "col"), ("w_ff2_0", w_ff2, 0, "row"), ("w_ff2_1", w_ff2, 1, "row")]
    own16 = [_cast_into(t, lead, kind, chip, name=f"cast_{nm}") for nm, t, lead, kind in big]
    wshapes, wkinds = [b[1].shape[-2:] for b in big], [b[3] for b in big]
    wgroups = [[0, 3, 5], [1, 2, 4, 6]]
    wsems, wthru, wtoken = _gw_start(own16, wshapes, wkinds, wgroups, name="gather_w_start")
    W = {}

    def finish_weights(g, after):
        idx = wgroups[g]
        shp, knd = [wshapes[p] for p in idx], [wkinds[p] for p in idx]
        got = _gw_wait([wthru[p] for p in idx], shp, knd, wsems[g], after, name=f"gather_w_wait{g}")
        for p, w in zip(idx, _gw_forward(got, shp, knd, name=f"gather_w_cores{g}")):
            W[big[p][0]] = w

    qk4 = QK // NCH
    wg2 = cat_all[:, :GATE_RANK * qk4].reshape(NCH, GATE_RANK, qk4).transpose(1, 0, 2).reshape(GATE_RANK, QK)
    bg = cat_all[:, GATE_RANK * qk4:(GATE_RANK + 1) * qk4].reshape(1, QK)
    gn = cat_all[:, (GATE_RANK + 1) * qk4:].reshape(1, D)
    wg2p = jnp.concatenate([wg2, jnp.zeros((128 - GATE_RANK, QK), F32)], axis=0).astype(BF16)

    lb_re, lb_im, bb_re, bb_im = _s5_disc(s5_a_re[0], s5_a_im[0], s5_log_dt[0], s5_b_re[0], s5_b_im[0])
    tb = _s5_tables(lb_re, lb_im, bb_re, bb_im, s5_c_re[0], s5_c_im[0], min(S5_R, L) // NSEG)
    s5_dv = s5_d + wtoken[0, 0]

    def vec(t):
        return t.reshape(1, -1)

    m0, m1 = mod[0], mod[1]
    xp = _to_segments(x[0])
    (u0,) = _rows(lambda t, g, sc, sh: (_norm_mod(t, g, sc, sh),), [xp], [vec(norm_mix[0]), m0[1], m0[0]],
                  [(D, F32)], [], name="pre_mix0")
    y0, z0, ck0 = _s5_fwd2(u0, tb, s5_dv, name="s5_fwd")
    finish_weights(0, z0)
    vg0 = _matmul(z0, W["s5_w_glu"], name="glu")

    def res_glu_pre(xt, vgt, gt, g, sc, sh):
        xn = xt + gt * (vgt[:, :D] * _sigmoid(vgt[:, D:]))
        return xn, _norm_mod(xn, g, sc, sh)

    x2_0, h2_0 = _rows(res_glu_pre, [xp, vg0], [m0[2], vec(norm_mlp[0]), m0[4], m0[3]], [(D, F32), (D, BF16)], [],
                       name="res_mix0")
    a_0, f0 = _mlp_fwd(h2_0, W["w_ff1_0"], W["w_ff2_0"], "0")

    def res_pre(xt, bt, gt, g, sc, sh):
        xn = xt + gt * bt
        return xn, _norm_mod(xn, g, sc, sh)

    x3p, h1p = _rows(res_pre, [x2_0, f0], [m0[5], vec(norm_mix[1]), m1[1], m1[0]], [(D, F32), (D, BF16)], [],
                     name="res_mlp0")
    x3, h1 = _from_segments(x3p), _from_segments(h1p)
    finish_weights(1, f0)
    w_in = W["gla_w_in"].transpose(1, 0, 2).reshape(D, -1)
    w_in_r = jnp.concatenate([w_in[:, :4 * QK], w_in[:, 4 * QK + GATE_RANK:], w_in[:, 4 * QK:4 * QK + GATE_RANK],
                              jnp.zeros((D, 128 - GATE_RANK), BF16)], axis=1)
    proj = _matmul(h1, w_in_r, name="gla_in", tn=640)
    og, states = _gla_fwd(proj, wg2p, bg, gn, name="gla_fwd")
    ymix = _matmul(og, W["gla_w_out"], name="gla_out")
    x2_1, h2_1 = _rows(res_pre, [x3, ymix], [m1[2], vec(norm_mlp[1]), m1[4], m1[3]], [(D, F32), (D, BF16)], [],
                       name="res_mix1")
    a_1, f1 = _mlp_fwd(h2_1, W["w_ff1_1"], W["w_ff2_1"], "1")

    def final(xt, ft, tgt, gt, g):
        xn = xt + gt * ft
        rs = lax.rsqrt(jnp.mean(xn * xn, axis=-1, keepdims=True) + EPS)
        xh = xn * rs
        e = xh * g - tgt
        dout = e * (1.0 / D)
        dxh = dout * g
        dx = rs * (dxh - xh * jnp.mean(dxh * xh, axis=-1, keepdims=True))
        lsum = 0.5 * jnp.sum(jnp.sum(e * e, axis=-1, keepdims=True), axis=0, keepdims=True) * (1.0 / D)
        return dx, dx * gt, jnp.broadcast_to(lsum, (1, 128)), _rsum(dout * xh), _rsum(dx * ft)

    dx, df1, loss_part, d_norm_final, dgt2_1 = _rows(
        final, [x2_1, f1, loss_target[0]], [m1[5], vec(norm_final)], [(D, F32), (D, BF16)],
        [(1, 128), (1, D), (1, D)], name="loss_head")
    loss = lax.psum(loss_part[0, 0], ("x", "y", "c"))

    def gate_bwd(dxt, bt, gt):
        return dxt * gt, _rsum(dxt * bt)

    def norm_bwd(xt, dht, drt, g, sc):
        dxn, dsh, dsc, dg = _norm_mod_bwd(xt, dht, g, sc)
        return drt + dxn, dsh, dsc, dg

    def norm_gate_bwd(xt, dht, drt, bt, g, sc, gt):
        dxn, dsh, dsc, dg = _norm_mod_bwd(xt, dht, g, sc)
        dxt = drt + dxn
        return dxt, dxt * gt, dsh, dsc, dg, _rsum(dxt * bt)

    vD = [(1, D)]
    dh2_1, dw_ff1_1, dw_ff2_1 = _mlp_bwd(df1, h2_1, a_1, W["w_ff1_1"], W["w_ff2_1"], "1")
    dx, dmix1, dsh2_1, dsc2_1, dg_mlp1, dgt1_1 = _rows(
        norm_gate_bwd, [x2_1, dh2_1, dx, ymix], [vec(norm_mlp[1]), m1[4], m1[2]], [(D, F32), (D, BF16)], vD * 4,
        name="norm_mlp1_bwd")
    dog = _matmul(dmix1, W["gla_w_out"], tb=True, name="gla_out_dx")
    dw_out = _matmul(og, dmix1, ta=True, name="gla_out_dw")
    dproj, dwg2p, dbg, dgn = _gla_bwd(proj, dog, states, wg2p, bg, gn, name="gla_bwd")
    dh1 = _matmul(dproj, w_in_r, tb=True, name="gla_in_dx", tk=640)
    dw_in_r = _matmul(h1, dproj, ta=True, name="gla_in_dw", tn=640)
    dx, dsh1_1, dsc1_1, dg_mix1 = _rows(norm_bwd, [x3, dh1, dx], [vec(norm_mix[1]), m1[1]], [(D, F32)], vD * 3,
                                        name="norm_mix1_bwd")
    dxp = _to_segments(dx)
    tags = [b[0] for b in big] + ["small"]
    rs_groups = []

    def rs_chips_begin(idx, srcs, r1, gname):
        s1 = [_sum_own_half(g, r, ci, F32 if tags[k] == "small" else BF16, name=f"rs_sum_cores_{tags[k]}")
              for g, r, k in zip(srcs, r1, idx)]
        pair, parts, lands, token = _rs_chips_start(s1, name=f"rs_chips_start_{gname}")
        rs_groups.append((idx, pair, parts, lands))
        return token

    def rs_begin(idx, srcs, gname):
        return rs_chips_begin(idx, srcs, _rs_cores(srcs, name=f"rs_cores_{gname}"), gname)

    dw_in = jnp.concatenate([dw_in_r[:, :4 * QK], dw_in_r[:, 6 * QK:6 * QK + GATE_RANK], dw_in_r[:, 4 * QK:6 * QK]], axis=1)
    dw_in = dw_in.reshape(D, NCH, -1).transpose(1, 0, 2)
    idx1 = [1, 2, 4, 6]
    pair1, src1, land1, tok1 = _rs_cores_start(
        [dw_in, dw_out.reshape(NCH, -1, D), dw_ff1_1, dw_ff2_1.reshape(NCH, -1, D)], name="rs_cores_start_l1")

    df0, dgt2_0 = _rows(gate_bwd, [dxp, f0], [m0[5] + tok1[0, 0]], [(D, BF16)], vD, name="gate_mlp0")
    dh2_0, dw_ff1_0, dw_ff2_0 = _mlp_bwd(df0, h2_0, a_0, W["w_ff1_0"], W["w_ff2_0"], "0")
    src1, land1 = _rs_cores_wait(pair1, src1, land1, dh2_0, name="rs_cores_wait_l1")
    tok1b = rs_chips_begin(idx1, src1, land1, "l1")
    idx0 = [3, 5]
    pair0, src0, land0, tok0 = _rs_cores_start([dw_ff1_0, dw_ff2_0.reshape(NCH, -1, D)], name="rs_cores_start_l0")
    tok2 = tok1b + tok0

    def norm_glu_bwd(xt, dht, drt, vgt, g, sc, gt):
        dxn, dsh, dsc, dg = _norm_mod_bwd(xt, dht, g, sc)
        dxt = drt + dxn
        val, sg = vgt[:, :D], _sigmoid(vgt[:, D:])
        dbr = dxt * gt
        dvg = jnp.concatenate([dbr * sg, dbr * val * sg * (1.0 - sg)], axis=1)
        return dxt, dvg, dsh, dsc, dg, _rsum(dxt * val * sg)

    dxp, dvg0, dsh2_0, dsc2_0, dg_mlp0, dgt1_0 = _rows(
        norm_glu_bwd, [x2_0, dh2_0, dxp, vg0], [vec(norm_mlp[0]), m0[4] + tok2[0, 0], m0[2]], [(D, F32), (2 * D, BF16)],
        vD * 4, name="norm_mlp0_bwd")
    dz0 = _matmul(dvg0, W["s5_w_glu"], tb=True, name="glu_dx")
    dw_glu = _matmul(z0, dvg0, ta=True, name="glu_dw", tn=512, col_shards=NCH)
    src0, land0 = _rs_cores_wait(pair0, src0, land0, dw_glu, name="rs_cores_wait_l0")
    tok0b = rs_chips_begin(idx0 + [0], src0 + [dw_glu], land0 + list(_rs_cores([dw_glu], name="rs_cores_glu")), "l0")
    du0, db_acc, dc_acc, dl_acc, dd_s5 = _s5_bwd2(u0, y0, dz0, ck0, tb, s5_dv + tok0b[0, 0], name="s5_bwd")
    dxp, dsh1_0, dsc1_0, dg_mix0 = _rows(norm_bwd, [xp, du0, dxp], [vec(norm_mix[0]), m0[1]], [(D, F32)], vD * 3,
                                         name="norm_mix0_bwd")
    grad_x = _from_segments(dxp)[None]

    dmod = jnp.concatenate([dsh1_0, dsc1_0, dgt1_0, dsh2_0, dsc2_0, dgt2_0,
                            dsh1_1, dsc1_1, dgt1_1, dsh2_1, dsc2_1, dgt2_1], axis=1)
    dbb_re, dbb_im = _s5_untable(db_acc)
    dc_re, dc_im_neg = _s5_untable(dc_acc)
    nbk = D // 128
    dl = dl_acc.reshape(nbk, NSEG, 2, GPB * S5_P).sum(axis=1)
    smalls = [dmod, dg_mix0, dg_mix1, dg_mlp0, dg_mlp1, d_norm_final, dd_s5, dbg, dgn,
              dwg2p[:GATE_RANK].reshape(1, -1), dbb_re.reshape(1, -1), dbb_im.reshape(1, -1),
              dc_re.reshape(1, -1), dc_im_neg.reshape(1, -1), dl.reshape(1, -1)]
    ssz = [t.shape[1] for t in smalls]
    stot = sum(ssz)
    spad = -(-stot // 8192) * 8192
    svec = jnp.concatenate(smalls + [jnp.zeros((1, spad - stot), F32)], axis=1).reshape(NCH, spad // (128 * NCH), 128)

    dmod_all = _exchange(dmod.reshape(12 * D // 128, 128), MASK_ALL, name="gather_dmod").reshape(8, 2, 6 * D)
    dmod_cols = lax.dynamic_slice_in_dim(dmod_all, chip * acols, acols, axis=2).transpose(1, 0, 2)
    g_w_ada = _ada_grad(c_all, dmod_cols, name="ada_grad")

    rs_begin([7], [svec], "last")
    landed = _rs_chips_wait([(g[1], g[2], g[3]) for g in rs_groups], grad_x, name="rs_chips_wait")
    s1, r2 = {}, {}
    for (idx, _, _, _), (parts, lands) in zip(rs_groups, landed):
        for k, part, land in zip(idx, parts, lands):
            s1[k], r2[k] = part, land

    def fin(k, **kw):
        return _sum_chips(r2[k], s1[k], chip, ci, name=f"rs_sum_chips_{tags[k]}", **kw)

    f_ff1 = fin(4, nlead=2, lead=1, prev=fin(3, nlead=2, lead=0))
    f_ff2 = fin(6, nlead=2, lead=1, prev=fin(5, nlead=2, lead=0))
    finals = [fin(0), fin(1), fin(2), f_ff1, f_ff2, fin(7, spread=True)]
    halves = [t.shape[1] for t in (s1[0], s1[1], s1[2], s1[3], s1[5], s1[7])]
    g_glu, g_in, g_out, g_w_ff1, g_w_ff2, s_own = _rs_gather(finals, halves, [False] * 5 + [True], name="rs_gather_cores")
    srows = spad // (128 * NCH)
    (s_sum,) = _gather_weights([s_own.reshape(NCH * srows, 128)], [(srows, 128)], ["row"], name="gather_small_grads")
    s_sum = s_sum.reshape(-1)
    so = [sum(ssz[:k]) for k in range(len(ssz))]
    sm = [s_sum[o:o + n] for o, n in zip(so, ssz)]
    (dmod_s, g_mix0, g_mix1, g_mlp0, g_mlp1, g_nf, g_d, g_bg, g_gn, g_wg2, g_bbre, g_bbim, g_cre, g_cimn, g_dl) = sm
    g_b_ada = dmod_s.reshape(2, 6 * D)

    G = D // S5_H
    _, disc_vjp = jax.vjp(_s5_disc, s5_a_re[0], s5_a_im[0], s5_log_dt[0], s5_b_re[0], s5_b_im[0])
    g_dl = g_dl.reshape(nbk, 2, GPB, S5_P)
    ct = (g_dl[:, 0].reshape(G, S5_P), g_dl[:, 1].reshape(G, S5_P),
          g_bbre.reshape(G, S5_H, S5_P).transpose(0, 2, 1), g_bbim.reshape(G, S5_H, S5_P).transpose(0, 2, 1))
    g_a_re, g_a_im, g_log_dt, g_b_re, g_b_im = disc_vjp(ct)
    g_c_re = g_cre.reshape(G, S5_H, S5_P)
    g_c_im = -g_cimn.reshape(G, S5_H, S5_P)
    g_wg2_s = lax.dynamic_slice_in_dim(g_wg2.reshape(GATE_RANK, QK), chip * qk4, qk4, axis=1)
    g_bg_s = lax.dynamic_slice_in_dim(g_bg.reshape(1, QK), chip * qk4, qk4, axis=1)
    g_gn_s = lax.dynamic_slice_in_dim(g_gn.reshape(1, D), chip * (D // NCH), D // NCH, axis=1)

    grads = dict(
        w_ada=g_w_ada, b_ada=g_b_ada, norm_mix=jnp.stack([g_mix0, g_mix1]), norm_mlp=jnp.stack([g_mlp0, g_mlp1]),
        s5_a_re=g_a_re[None], s5_a_im=g_a_im[None], s5_log_dt=g_log_dt[None], s5_b_re=g_b_re[None], s5_b_im=g_b_im[None],
        s5_c_re=g_c_re[None], s5_c_im=g_c_im[None], s5_d=g_d[None], s5_w_glu=g_glu,
        gla_w_in=g_in, gla_w_gate2=g_wg2_s[None], gla_b_gate=g_bg_s, gla_g_norm=g_gn_s,
        gla_w_out=g_out, w_ff1=g_w_ff1, w_ff2=g_w_ff2, norm_final=g_nf)

    names = list(grads)
    large = ("w_ada", "s5_w_glu", "gla_w_in", "gla_w_out", "w_ff1", "w_ff2")
    delta, new_m, new_v = {}, {}, {}
    for nm in large:
        delta[nm], new_m[nm], new_v[nm] = _adamw(args[nm], grads[nm], args["m_" + nm], args["v_" + nm], name=f"adamw_{nm}")
    grads = {nm: grads[nm].reshape(args[nm].shape) for nm in names}
    for nm in names:
        if nm not in large:
            shp = args[nm].shape
            as2d = (1, -1) if len(shp) == 1 else shp
            outs = _adamw_whole(*[t.reshape(as2d) for t in (args[nm], grads[nm], args["m_" + nm], args["v_" + nm])],
                                name=f"adamw_{nm}")
            delta[nm], new_m[nm], new_v[nm] = (t.reshape(shp) for t in outs)
    return (loss, grad_x, *[grads[n] for n in names], *[delta[n] for n in names], *[new_m[n] for n in names],
            *[new_v[n] for n in names])
```

```python
import math

import jax
import jax.numpy as jnp
from jax import lax
from jax.experimental import pallas as pl
from jax.experimental.pallas import tpu as pltpu

F32 = jnp.float32
BF16 = jnp.bfloat16
MESH = pl.DeviceIdType.MESH

EPS = 1e-6
CHUNK = 64
GLA_NB = 4
S5_H = 16
S5_P = 64
GPB = 8
NSEG = 8
HEADS = 4
GATE_RANK = 16
GATE_TAU = 16.0
NCH = 4
LR, B1, B2, AEPS, WD, ASTEP = 0.001, 0.9, 0.999, 1e-08, 0.01, 10
VMEM_LIMIT = 56 << 20

MASK_CHIPS = ((1, 0, 0), (0, 1, 0), (1, 1, 0))
MASK_ALL = ((0, 0, 1), (0, 1, 0), (0, 1, 1), (1, 0, 0), (1, 0, 1), (1, 1, 0), (1, 1, 1))


def _params(*sem):
    return pltpu.CompilerParams(dimension_semantics=sem or None, vmem_limit_bytes=VMEM_LIMIT)


def _tile_rows(rows, cap=512):
    best = 8
    for t in range(8, cap + 1, 8):
        if rows % t == 0:
            best = t
    return best


def _whole(shape):
    return pl.BlockSpec(shape, lambda i, _n=len(shape): (0,) * _n)


def _matmul(a, b, *, name, ta=False, tb=False, tm=1024, tn=1024, tk=2048, out_dtypes=(F32,),
            a_fn=None, epi=None, epi_ins=(), col_shards=1):
    M, K = (a.shape[1], a.shape[0]) if ta else a.shape
    N = b.shape[0] if tb else b.shape[1]
    tm, tn, tk = min(tm, M), min(tn, N), min(tk, K)
    assert M % tm == 0 and N % tn == 0 and K % tk == 0, (name, M, N, K)
    nk = K // tk
    ne = len(epi_ins)
    dn = (((0 if ta else 1,), (1 if tb else 0,)), ((), ()))

    def body(a_ref, b_ref, *rest):
        e_refs, o_refs, acc = rest[:ne], rest[ne:-1], rest[-1]
        k = pl.program_id(2)

        @pl.when(k == 0)
        def _():
            acc[...] = jnp.zeros_like(acc)

        at = a_ref[...]
        if a_fn is not None:
            at = a_fn(at)
        acc[...] += lax.dot_general(at.astype(BF16), b_ref[...].astype(BF16), dn, preferred_element_type=F32)

        @pl.when(k == nk - 1)
        def _():
            outs = (acc[...],) if epi is None else epi(acc[...], *[r[...] for r in e_refs])
            for r, o in zip(o_refs, outs):
                r[...] = o.astype(r.dtype)

    a_spec = pl.BlockSpec((tk, tm), lambda i, j, k: (k, i)) if ta else pl.BlockSpec((tm, tk), lambda i, j, k: (i, k))
    b_spec = pl.BlockSpec((tn, tk), lambda i, j, k: (j, k)) if tb else pl.BlockSpec((tk, tn), lambda i, j, k: (k, j))
    o_spec = pl.BlockSpec((tm, tn), lambda i, j, k: (i, j))
    if col_shards > 1:
        per = N // col_shards // tn
        assert ne == 0 and per * tn * col_shards == N
        w_spec = pl.BlockSpec((None, tm, tn), lambda i, j, k: (j // per, i, j % per))
        o_shape = (col_shards, M, N // col_shards)
    else:
        w_spec, o_shape = o_spec, (M, N)
    outs = pl.pallas_call(
        body, name=name, grid=(M // tm, N // tn, nk),
        in_specs=[a_spec, b_spec] + [o_spec] * ne,
        out_specs=[w_spec] * len(out_dtypes),
        out_shape=[jax.ShapeDtypeStruct(o_shape, d) for d in out_dtypes],
        scratch_shapes=[pltpu.VMEM((tm, tn), F32)],
        compiler_params=_params("parallel", "parallel", "arbitrary"),
    )(a, b, *epi_ins)
    return outs[0] if len(outs) == 1 else outs


def _rows(fn, rows_in, vecs_in, rows_out, acc_out, *, name, tm=512):
    L = rows_in[0].shape[0]
    tm = min(tm, L)
    assert L % tm == 0
    nr, nv, no, na = len(rows_in), len(vecs_in), len(rows_out), len(acc_out)

    def body(*refs):
        rin, vin = refs[:nr], refs[nr:nr + nv]
        rout, aout = refs[nr + nv:nr + nv + no], refs[nr + nv + no:]
        outs = fn(*[r[...] for r in rin], *[v[...] for v in vin])
        for r, o in zip(rout, outs[:no]):
            r[...] = o.astype(r.dtype)
        if na:
            @pl.when(pl.program_id(0) == 0)
            def _():
                for r in aout:
                    r[...] = jnp.zeros_like(r)

            for r, o in zip(aout, outs[no:]):
                r[...] += o

    outs = pl.pallas_call(
        body, name=name, grid=(L // tm,),
        in_specs=[pl.BlockSpec((tm, r.shape[1]), lambda i: (i, 0)) for r in rows_in] + [_whole(v.shape) for v in vecs_in],
        out_specs=[pl.BlockSpec((tm, c), lambda i: (i, 0)) for c, _ in rows_out] + [_whole(s) for s in acc_out],
        out_shape=[jax.ShapeDtypeStruct((L, c), d) for c, d in rows_out] + [jax.ShapeDtypeStruct(s, F32) for s in acc_out],
        compiler_params=_params("arbitrary"),
    )(*rows_in, *vecs_in)
    return outs


def _rsum(t):
    return jnp.sum(t, axis=0, keepdims=True)


def _norm_mod(x, g, sc, sh):
    rs = lax.rsqrt(jnp.mean(x * x, axis=-1, keepdims=True) + EPS)
    return x * rs * g * (1.0 + sc) + sh


def _norm_mod_bwd(x, dh, g, sc):
    rs = lax.rsqrt(jnp.mean(x * x, axis=-1, keepdims=True) + EPS)
    xh = x * rs
    dn = dh * (1.0 + sc)
    dxh = dn * g
    dx = rs * (dxh - xh * jnp.mean(dxh * xh, axis=-1, keepdims=True))
    return dx, _rsum(dh), _rsum(dh * xh * g), _rsum(dn * xh)


def _sigmoid(x):
    return jax.nn.sigmoid(x)


def _gelu(y):
    return jax.nn.gelu(y, approximate=True)


def _gelu_grad(y):
    c = math.sqrt(2.0 / math.pi)
    t = jnp.tanh(c * (y + 0.044715 * y * y * y))
    return 0.5 * (1.0 + t) + 0.5 * y * (1.0 - t * t) * c * (1.0 + 3.0 * 0.044715 * y * y)


def _s5_tables(lb_re, lb_im, bb_re, bb_im, c_re, c_im, seg_len):
    G = lb_re.shape[0]
    nb = G // GPB
    eye = jnp.eye(GPB, dtype=F32)

    def bdiag(t):
        a, b = t.shape[1:]
        t = t.reshape(nb, GPB, a, b)
        return (t[:, :, :, None, :] * eye[None, :, None, :, None]).reshape(nb, GPB * a, GPB * b)

    bbd = jnp.concatenate([bdiag(bb_re.transpose(0, 2, 1)), bdiag(bb_im.transpose(0, 2, 1))], axis=2)
    cbd = jnp.concatenate([bdiag(c_re.transpose(0, 2, 1)), -bdiag(c_im.transpose(0, 2, 1))], axis=1)

    def lanes(re, im):
        t = jnp.concatenate([re.reshape(nb, GPB * S5_P), im.reshape(nb, GPB * S5_P)], axis=1)
        return jnp.repeat(t, NSEG, axis=0)

    tr, ti = lb_re, lb_im
    for _ in range(int(math.log2(seg_len))):
        tr, ti = tr * tr - ti * ti, 2.0 * tr * ti
    return dict(bbd=bbd.astype(BF16), bbdT=bbd.transpose(0, 2, 1).astype(BF16), cbd=cbd.astype(BF16),
                cbdT=cbd.transpose(0, 2, 1).astype(BF16), lam=lanes(lb_re, lb_im), lamT=lanes(tr, ti))


def _s5_untable(acc):
    nb = acc.shape[0]
    t = acc.reshape(nb, GPB, S5_H, 2, GPB, S5_P)
    d = jnp.diagonal(t, axis1=1, axis2=4)
    d = d.transpose(0, 4, 2, 1, 3).reshape(nb * GPB, 2, S5_H, S5_P)
    return d[:, 0], d[:, 1]


def _s5_fix(ends, lamT, *, reverse, name):
    nrow = ends.shape[0]
    half = ends.shape[1] // 2

    def body(e_ref, t_ref, o_ref):
        for gb in range(nrow // NSEG):
            r0 = gb * NSEG
            tr, ti = t_ref[r0:r0 + 1, :half], t_ref[r0:r0 + 1, half:]
            cr = jnp.zeros((1, half), F32)
            ci = jnp.zeros((1, half), F32)
            order = range(NSEG - 1, -1, -1) if reverse else range(NSEG)
            for n, s in enumerate(order):
                if n > 0:
                    p = s + 1 if reverse else s - 1
                    er, ei = e_ref[r0 + p:r0 + p + 1, :half], e_ref[r0 + p:r0 + p + 1, half:]
                    if reverse:
                        cr, ci = tr * cr + ti * ci + er, tr * ci - ti * cr + ei
                    else:
                        cr, ci = tr * cr - ti * ci + er, tr * ci + ti * cr + ei
                o_ref[r0 + s:r0 + s + 1, :half] = cr
                o_ref[r0 + s:r0 + s + 1, half:] = ci

    return pl.pallas_call(body, name=name, out_shape=jax.ShapeDtypeStruct(ends.shape, F32),
                          compiler_params=_params())(ends, lamT)


def _s5_fwd(up, carry_in, tb, dvec, *, emit, name, R=256):
    L, D = up.shape
    R = min(R, L)
    nb, ta, ngb = L // R, R // NSEG, D // 128
    SW = GPB * S5_P
    crows = ngb * NSEG

    def body(u_ref, cin_ref, lam_ref, b_ref, c_ref, d_ref, *rest):
        if emit:
            y_ref, z_ref, ck_ref, carry, xbuf = rest
        else:
            cout_ref, carry, xbuf = rest
        i = pl.program_id(0)

        @pl.when(i == 0)
        def _():
            carry[...] = cin_ref[...]

        if emit:
            ck_ref[0] = carry[...]
        for gb in range(ngb):
            cols = slice(gb * 128, (gb + 1) * 128)
            rws = slice(gb * NSEG, (gb + 1) * NSEG)
            ug = u_ref[:, cols]
            xbuf[...] = jnp.dot(ug.astype(BF16), b_ref[gb], preferred_element_type=F32)
            lr, li = lam_ref[rws, :SW], lam_ref[rws, SW:]

            def step(a, c, lr=lr, li=li):
                cr, ci = c
                o = pl.multiple_of(a * NSEG, NSEG)
                nr = lr * cr - li * ci + xbuf[pl.ds(o, NSEG), :SW]
                ni = lr * ci + li * cr + xbuf[pl.ds(o, NSEG), SW:]
                if emit:
                    xbuf[pl.ds(o, NSEG), :SW] = nr
                    xbuf[pl.ds(o, NSEG), SW:] = ni
                return nr, ni

            cr, ci = lax.fori_loop(0, ta, step, (carry[rws, :SW], carry[rws, SW:]), unroll=2)
            carry[rws, :SW] = cr
            carry[rws, SW:] = ci
            if emit:
                y = jnp.dot(xbuf[...].astype(BF16), c_ref[gb], preferred_element_type=F32) + d_ref[:, cols] * ug
                y_ref[:, cols] = y
                z_ref[:, cols] = _gelu(y).astype(BF16)
        if not emit:
            cout_ref[...] = carry[...]

    rowblk = pl.BlockSpec((R, D), lambda i: (i, 0))
    if emit:
        out_shape = [jax.ShapeDtypeStruct((L, D), F32), jax.ShapeDtypeStruct((L, D), BF16),
                     jax.ShapeDtypeStruct((nb, crows, 2 * SW), F32)]
        out_specs = [rowblk, rowblk, pl.BlockSpec((1, crows, 2 * SW), lambda i: (i, 0, 0))]
    else:
        out_shape = [jax.ShapeDtypeStruct((crows, 2 * SW), F32)]
        out_specs = [_whole((crows, 2 * SW))]
    return pl.pallas_call(
        body, name=name, grid=(nb,),
        in_specs=[rowblk, _whole(carry_in.shape), _whole(tb["lam"].shape), _whole(tb["bbd"].shape),
                  _whole(tb["cbd"].shape), _whole(dvec.shape)],
        out_specs=out_specs, out_shape=out_shape,
        scratch_shapes=[pltpu.VMEM((crows, 2 * SW), F32), pltpu.VMEM((R, 2 * SW), F32)],
        compiler_params=_params("arbitrary"),
    )(up, carry_in, tb["lam"], tb["bbd"], tb["cbd"], dvec)


def _s5_bwd(up, y, dz, ck, gcarry_in, tb, dvec, *, emit, name, R=256):
    L, D = up.shape
    R = min(R, L)
    nb, ta, ngb = L // R, R // NSEG, D // 128
    SW = GPB * S5_P
    crows = ngb * NSEG

    def body(u_ref, y_ref, dz_ref, ck_ref, gin_ref, lam_ref, b_ref, bt_ref, ct_ref, d_ref, *rest):
        if emit:
            du_ref, db_ref, dc_ref, dl_ref, dd_ref, gcarry, xbuf, gbuf, dybuf = rest
        else:
            gout_ref, gcarry, gbuf, dybuf = rest
        i = pl.program_id(0)

        @pl.when(i == 0)
        def _():
            gcarry[...] = gin_ref[...]
            if emit:
                db_ref[...] = jnp.zeros_like(db_ref)
                dc_ref[...] = jnp.zeros_like(dc_ref)
                dl_ref[...] = jnp.zeros_like(dl_ref)
                dd_ref[...] = jnp.zeros_like(dd_ref)

        dybuf[...] = dz_ref[...] * _gelu_grad(y_ref[...])
        for gb in range(ngb):
            cols = slice(gb * 128, (gb + 1) * 128)
            rws = slice(gb * NSEG, (gb + 1) * NSEG)
            dyg = dybuf[:, cols]
            lr, li = lam_ref[rws, :SW], lam_ref[rws, SW:]
            gbuf[...] = jnp.dot(dyg.astype(BF16), ct_ref[gb], preferred_element_type=F32)
            if emit:
                ug = u_ref[:, cols]
                xbuf[0:NSEG, :] = ck_ref[0, rws, :]
                xbuf[NSEG:, :] = jnp.dot(ug.astype(BF16), b_ref[gb], preferred_element_type=F32)

                def fstep(a, c, lr=lr, li=li):
                    cr, ci = c
                    o = pl.multiple_of(a * NSEG + NSEG, NSEG)
                    nr = lr * cr - li * ci + xbuf[pl.ds(o, NSEG), :SW]
                    ni = lr * ci + li * cr + xbuf[pl.ds(o, NSEG), SW:]
                    xbuf[pl.ds(o, NSEG), :SW] = nr
                    xbuf[pl.ds(o, NSEG), SW:] = ni
                    return nr, ni

                lax.fori_loop(0, ta, fstep, (xbuf[0:NSEG, :SW], xbuf[0:NSEG, SW:]), unroll=2)

            def rstep(k, c, lr=lr, li=li):
                o = pl.multiple_of((ta - 1 - k) * NSEG, NSEG)
                gr_n, gi_n = c[0], c[1]
                gr = gbuf[pl.ds(o, NSEG), :SW] + lr * gr_n + li * gi_n
                gi = gbuf[pl.ds(o, NSEG), SW:] - li * gr_n + lr * gi_n
                if not emit:
                    return gr, gi
                gbuf[pl.ds(o, NSEG), :SW] = gr
                gbuf[pl.ds(o, NSEG), SW:] = gi
                xr, xi = xbuf[pl.ds(o, NSEG), :SW], xbuf[pl.ds(o, NSEG), SW:]
                return gr, gi, c[2] + gr * xr + gi * xi, c[3] + gi * xr - gr * xi

            c0 = (gcarry[rws, :SW], gcarry[rws, SW:])
            if emit:
                c0 = c0 + (jnp.zeros((NSEG, SW), F32), jnp.zeros((NSEG, SW), F32))
            cf = lax.fori_loop(0, ta, rstep, c0, unroll=2)
            gcarry[rws, :SW] = cf[0]
            gcarry[rws, SW:] = cf[1]
            if emit:
                dl_ref[rws, :SW] += cf[2]
                dl_ref[rws, SW:] += cf[3]
                gb16 = gbuf[...].astype(BF16)
                du_ref[:, cols] = jnp.dot(gb16, bt_ref[gb], preferred_element_type=F32) + d_ref[:, cols] * dyg
                tn = (((0,), (0,)), ((), ()))
                db_ref[gb] += lax.dot_general(ug.astype(BF16), gb16, tn, preferred_element_type=F32)
                dc_ref[gb] += lax.dot_general(dyg.astype(BF16), xbuf[NSEG:, :].astype(BF16), tn,
                                              preferred_element_type=F32)
                dd_ref[:, cols] += _rsum(dyg * ug)
        if not emit:
            gout_ref[...] = gcarry[...]

    rev = pl.BlockSpec((R, D), lambda i: (nb - 1 - i, 0))
    acc3 = (ngb, 128, 2 * SW)
    if emit:
        out_shape = [jax.ShapeDtypeStruct((L, D), F32), jax.ShapeDtypeStruct(acc3, F32), jax.ShapeDtypeStruct(acc3, F32),
                     jax.ShapeDtypeStruct((crows, 2 * SW), F32), jax.ShapeDtypeStruct((1, D), F32)]
        out_specs = [rev, _whole(acc3), _whole(acc3), _whole((crows, 2 * SW)), _whole((1, D))]
        scratch = [pltpu.VMEM((crows, 2 * SW), F32), pltpu.VMEM((R + NSEG, 2 * SW), F32),
                   pltpu.VMEM((R, 2 * SW), F32), pltpu.VMEM((R, D), F32)]
    else:
        out_shape = [jax.ShapeDtypeStruct((crows, 2 * SW), F32)]
        out_specs = [_whole((crows, 2 * SW))]
        scratch = [pltpu.VMEM((crows, 2 * SW), F32), pltpu.VMEM((R, 2 * SW), F32), pltpu.VMEM((R, D), F32)]
    return pl.pallas_call(
        body, name=name, grid=(nb,),
        in_specs=[rev, rev, rev, pl.BlockSpec((1, crows, 2 * SW), lambda i: (nb - 1 - i, 0, 0)),
                  _whole(gcarry_in.shape), _whole(tb["lam"].shape), _whole(tb["bbd"].shape),
                  _whole(tb["bbdT"].shape), _whole(tb["cbdT"].shape), _whole(dvec.shape)],
        out_specs=out_specs, out_shape=out_shape, scratch_shapes=scratch,
        compiler_params=_params("arbitrary"),
    )(up, y, dz, ck, gcarry_in, tb["lam"], tb["bbd"], tb["bbdT"], tb["cbdT"], dvec)


S5_R = 256


def _s5_carries(ends, first, t_ref, rws, SW, cfx, *, reverse):
    er, ei = ends
    tr, ti = t_ref[rws, :SW][0:1], t_ref[rws, SW:][0:1]
    cr, ci = first
    order = range(NSEG - 1, -1, -1) if reverse else range(NSEG)
    for n, s in enumerate(order):
        if n > 0:
            p = s + 1 if reverse else s - 1
            if reverse:
                cr, ci = tr * cr + ti * ci + er[p:p + 1], tr * ci - ti * cr + ei[p:p + 1]
            else:
                cr, ci = tr * cr - ti * ci + er[p:p + 1], tr * ci + ti * cr + ei[p:p + 1]
        cfx[s:s + 1, :SW] = cr
        cfx[s:s + 1, SW:] = ci


def _s5_fwd2(up, tb, dvec, *, name):
    L, D = up.shape
    R = min(S5_R, L)
    nb, ta, ngb = L // R, R // NSEG, D // 128
    SW = GPB * S5_P
    crows = ngb * NSEG

    def body(u_ref, lam_ref, t_ref, b_ref, c_ref, d_ref, y_ref, z_ref, ck_ref, carry, xbuf, cfx):
        @pl.when(pl.program_id(0) == 0)
        def _():
            carry[...] = jnp.zeros_like(carry)

        zero = jnp.zeros((NSEG, SW), F32)
        for gb in range(ngb):
            cols = slice(gb * 128, (gb + 1) * 128)
            rws = slice(gb * NSEG, (gb + 1) * NSEG)
            ug = u_ref[:, cols]
            xbuf[...] = jnp.dot(ug.astype(BF16), b_ref[gb], preferred_element_type=F32)
            lr, li = lam_ref[rws, :SW], lam_ref[rws, SW:]

            def step(a, c, lr=lr, li=li, store=False):
                cr, ci = c
                o = pl.multiple_of(a * NSEG, NSEG)
                nr = lr * cr - li * ci + xbuf[pl.ds(o, NSEG), :SW]
                ni = lr * ci + li * cr + xbuf[pl.ds(o, NSEG), SW:]
                if store:
                    xbuf[pl.ds(o, NSEG), :SW] = nr
                    xbuf[pl.ds(o, NSEG), SW:] = ni
                return nr, ni

            ends = lax.fori_loop(0, ta, step, (zero, zero), unroll=2)
            prev = (carry[rws, :SW][NSEG - 1:NSEG], carry[rws, SW:][NSEG - 1:NSEG])
            _s5_carries(ends, prev, t_ref, rws, SW, cfx, reverse=False)
            ck_ref[0, rws, :] = cfx[...]
            cr, ci = lax.fori_loop(0, ta, lambda a, c, st=step: st(a, c, store=True), (cfx[:, :SW], cfx[:, SW:]), unroll=2)
            carry[rws, :SW] = cr
            carry[rws, SW:] = ci
            y = jnp.dot(xbuf[...].astype(BF16), c_ref[gb], preferred_element_type=F32) + d_ref[:, cols] * ug
            y_ref[:, cols] = y
            z_ref[:, cols] = _gelu(y).astype(BF16)

    rowblk = pl.BlockSpec((R, D), lambda i: (i, 0))
    return pl.pallas_call(
        body, name=name, grid=(nb,),
        in_specs=[rowblk, _whole(tb["lam"].shape), _whole(tb["lamT"].shape), _whole(tb["bbd"].shape),
                  _whole(tb["cbd"].shape), _whole(dvec.shape)],
        out_specs=[rowblk, rowblk, pl.BlockSpec((1, crows, 2 * SW), lambda i: (i, 0, 0))],
        out_shape=[jax.ShapeDtypeStruct((L, D), F32), jax.ShapeDtypeStruct((L, D), BF16),
                   jax.ShapeDtypeStruct((nb, crows, 2 * SW), F32)],
        scratch_shapes=[pltpu.VMEM((crows, 2 * SW), F32), pltpu.VMEM((R, 2 * SW), F32), pltpu.VMEM((NSEG, 2 * SW), F32)],
        compiler_params=_params("arbitrary"),
    )(up, tb["lam"], tb["lamT"], tb["bbd"], tb["cbd"], dvec)


def _s5_bwd2(up, y, dz, ck, tb, dvec, *, name):
    L, D = up.shape
    R = min(S5_R, L)
    nb, ta, ngb = L // R, R // NSEG, D // 128
    SW = GPB * S5_P
    crows = ngb * NSEG

    def body(u_ref, y_ref, dz_ref, ck_ref, lam_ref, t_ref, b_ref, bt_ref, ct_ref, d_ref,
             du_ref, db_ref, dc_ref, dl_ref, dd_ref, gcarry, xbuf, gbuf, dybuf, cfx):
        @pl.when(pl.program_id(0) == 0)
        def _():
            gcarry[...] = jnp.zeros_like(gcarry)
            db_ref[...] = jnp.zeros_like(db_ref)
            dc_ref[...] = jnp.zeros_like(dc_ref)
            dl_ref[...] = jnp.zeros_like(dl_ref)
            dd_ref[...] = jnp.zeros_like(dd_ref)

        zero = jnp.zeros((NSEG, SW), F32)
        dybuf[...] = dz_ref[...] * _gelu_grad(y_ref[...])
        for gb in range(ngb):
            cols = slice(gb * 128, (gb + 1) * 128)
            rws = slice(gb * NSEG, (gb + 1) * NSEG)
            dyg = dybuf[:, cols]
            ug = u_ref[:, cols]
            lr, li = lam_ref[rws, :SW], lam_ref[rws, SW:]
            gbuf[...] = jnp.dot(dyg.astype(BF16), ct_ref[gb], preferred_element_type=F32)
            xbuf[0:NSEG, :] = ck_ref[0, rws, :]
            xbuf[NSEG:, :] = jnp.dot(ug.astype(BF16), b_ref[gb], preferred_element_type=F32)

            def fstep(a, c, lr=lr, li=li):
                cr, ci = c
                o = pl.multiple_of(a * NSEG + NSEG, NSEG)
                nr = lr * cr - li * ci + xbuf[pl.ds(o, NSEG), :SW]
                ni = lr * ci + li * cr + xbuf[pl.ds(o, NSEG), SW:]
                xbuf[pl.ds(o, NSEG), :SW] = nr
                xbuf[pl.ds(o, NSEG), SW:] = ni
                return nr, ni

            lax.fori_loop(0, ta, fstep, (xbuf[0:NSEG, :SW], xbuf[0:NSEG, SW:]), unroll=2)

            def rstep(k, c, lr=lr, li=li, store=False):
                o = pl.multiple_of((ta - 1 - k) * NSEG, NSEG)
                gr_n, gi_n = c[0], c[1]
                gr = gbuf[pl.ds(o, NSEG), :SW] + lr * gr_n + li * gi_n
                gi = gbuf[pl.ds(o, NSEG), SW:] - li * gr_n + lr * gi_n
                if not store:
                    return gr, gi
                gbuf[pl.ds(o, NSEG), :SW] = gr
                gbuf[pl.ds(o, NSEG), SW:] = gi
                xr, xi = xbuf[pl.ds(o, NSEG), :SW], xbuf[pl.ds(o, NSEG), SW:]
                return gr, gi, c[2] + gr * xr + gi * xi, c[3] + gi * xr - gr * xi

            gends = lax.fori_loop(0, ta, rstep, (zero, zero), unroll=2)
            nxt = (gcarry[rws, :SW][0:1], gcarry[rws, SW:][0:1])
            _s5_carries(gends, nxt, t_ref, rws, SW, cfx, reverse=True)
            cf = lax.fori_loop(0, ta, lambda k, c, st=rstep: st(k, c, store=True),
                               (cfx[:, :SW], cfx[:, SW:], zero, zero), unroll=2)
            gcarry[rws, :SW] = cf[0]
            gcarry[rws, SW:] = cf[1]
            dl_ref[rws, :SW] += cf[2]
            dl_ref[rws, SW:] += cf[3]
            gb16 = gbuf[...].astype(BF16)
            du_ref[:, cols] = jnp.dot(gb16, bt_ref[gb], preferred_element_type=F32) + d_ref[:, cols] * dyg
            db_ref[gb] += lax.dot_general(ug.astype(BF16), gb16, TN, preferred_element_type=F32)
            dc_ref[gb] += lax.dot_general(dyg.astype(BF16), xbuf[NSEG:, :].astype(BF16), TN, preferred_element_type=F32)
            dd_ref[:, cols] += _rsum(dyg * ug)

    rev = pl.BlockSpec((R, D), lambda i: (nb - 1 - i, 0))
    acc3 = (ngb, 128, 2 * SW)
    return pl.pallas_call(
        body, name=name, grid=(nb,),
        in_specs=[rev, rev, rev, pl.BlockSpec((1, crows, 2 * SW), lambda i: (nb - 1 - i, 0, 0)),
                  _whole(tb["lam"].shape), _whole(tb["lamT"].shape), _whole(tb["bbd"].shape),
                  _whole(tb["bbdT"].shape), _whole(tb["cbdT"].shape), _whole(dvec.shape)],
        out_specs=[rev, _whole(acc3), _whole(acc3), _whole((crows, 2 * SW)), _whole((1, D))],
        out_shape=[jax.ShapeDtypeStruct((L, D), F32), jax.ShapeDtypeStruct(acc3, F32), jax.ShapeDtypeStruct(acc3, F32),
                   jax.ShapeDtypeStruct((crows, 2 * SW), F32), jax.ShapeDtypeStruct((1, D), F32)],
        scratch_shapes=[pltpu.VMEM((crows, 2 * SW), F32), pltpu.VMEM((R + NSEG, 2 * SW), F32),
                        pltpu.VMEM((R, 2 * SW), F32), pltpu.VMEM((R, D), F32), pltpu.VMEM((NSEG, 2 * SW), F32)],
        compiler_params=_params("arbitrary"),
    )(up, y, dz, ck, tb["lam"], tb["lamT"], tb["bbd"], tb["bbdT"], tb["cbdT"], dvec)


NN = (((1,), (0,)), ((), ()))
TN = (((0,), (0,)), ((), ()))
NT = (((1,), (1,)), ((), ()))


def _dot3(lhs, rhs, dn, split):
    x = rhs if split == "rhs" else lhs
    hi = x.astype(BF16)
    r1 = x - hi.astype(F32)
    mid = r1.astype(BF16)
    lo = (r1 - mid.astype(F32)).astype(BF16)
    out = None
    for part in (hi, mid, lo):
        ops = (lhs, part) if split == "rhs" else (part, rhs)
        t = lax.dot_general(ops[0], ops[1], dn, preferred_element_type=F32)
        out = t if out is None else out + t
    return out


def _log_sigmoid(x):
    return jnp.minimum(x, 0.0) - jnp.log(1.0 + jnp.exp(-jnp.abs(x)))


def _gla_gates(p, wg_ref, bg_ref, QK):
    C = p.shape[0]
    glr = p[:, 6 * QK:6 * QK + 128].astype(BF16)
    gpre = jnp.dot(glr, wg_ref[...], preferred_element_type=F32) + bg_ref[...]
    la = _log_sigmoid(gpre) * (1.0 / GATE_TAU)
    row = lax.broadcasted_iota(jnp.int32, (C, C), 0)
    col = lax.broadcasted_iota(jnp.int32, (C, C), 1)
    gc = _dot3((row >= col).astype(BF16), la, NN, "rhs")
    ge = gc[C - 1:C, :]
    w = jnp.exp(ge - gc)
    return glr, gpre, la, ge, w


def _gla_fwd(proj, wg2p, bg, gn, *, name):
    L = proj.shape[0]
    QK = wg2p.shape[1]
    DK, DV = QK // HEADS, 2 * QK // HEADS
    nC = L // CHUNK
    NB = min(GLA_NB, nC)
    assert nC % NB == 0
    scale = DK ** -0.5

    def body(p_ref, wg_ref, bg_ref, gn_ref, og_ref, s_ref, sst):
        @pl.when(pl.program_id(0) == 0)
        def _():
            sst[...] = jnp.zeros_like(sst)

        ones = jnp.ones((CHUNK, DV), BF16)
        for cc in range(NB):
            rows = slice(cc * CHUNK, (cc + 1) * CHUNK)
            p = p_ref[rows, :]
            _, _, la, _, w = _gla_gates(p, wg_ref, bg_ref, QK)
            for h in range(HEADS):
                ks, vs = slice(h * DK, (h + 1) * DK), slice(h * DV, (h + 1) * DV)
                q = p[:, h * DK:(h + 1) * DK] * scale
                kd = p[:, QK + h * DK:QK + (h + 1) * DK] * w[:, ks]
                v = p[:, 2 * QK + h * DV:2 * QK + (h + 1) * DV]
                r = p[:, 4 * QK + h * DV:4 * QK + (h + 1) * DV]
                dec = jnp.exp(_dot3(la[:, ks], ones, TN, "lhs"))
                kv = lax.dot_general(kd.astype(BF16), v.astype(BF16), TN, preferred_element_type=F32)
                S = dec * sst[ks, :] + kv
                sst[ks, :] = S
                s_ref[cc, ks, :] = S
                o = jnp.dot(q.astype(BF16), S.astype(BF16), preferred_element_type=F32)
                on = o * lax.rsqrt(jnp.mean(o * o, axis=-1, keepdims=True) + EPS)
                og_ref[rows, vs] = (on * gn_ref[:, vs] * (r * _sigmoid(r))).astype(BF16)

    RB = NB * CHUNK
    return pl.pallas_call(
        body, name=name, grid=(nC // NB,),
        in_specs=[pl.BlockSpec((RB, proj.shape[1]), lambda i: (i, 0)), _whole(wg2p.shape), _whole(bg.shape), _whole(gn.shape)],
        out_specs=[pl.BlockSpec((RB, 2 * QK), lambda i: (i, 0)), pl.BlockSpec((NB, QK, DV), lambda i: (i, 0, 0))],
        out_shape=[jax.ShapeDtypeStruct((L, 2 * QK), BF16), jax.ShapeDtypeStruct((nC, QK, DV), F32)],
        scratch_shapes=[pltpu.VMEM((QK, DV), F32)],
        compiler_params=_params("arbitrary"),
    )(proj, wg2p, bg, gn)


def _gla_bwd(proj, dog, states, wg2p, bg, gn, *, name):
    L, W = proj.shape
    QK = wg2p.shape[1]
    DK, DV = QK // HEADS, 2 * QK // HEADS
    nC = L // CHUNK
    NB = min(GLA_NB, nC)
    nB = nC // NB
    scale = DK ** -0.5

    def body(p_ref, dog_ref, sc_ref, sp_ref, wg_ref, bg_ref, gn_ref, dp_ref, dwg_ref, dbg_ref, dgn_ref, gst):
        i = pl.program_id(0)

        @pl.when(i == 0)
        def _():
            gst[...] = jnp.zeros_like(gst)
            dwg_ref[...] = jnp.zeros_like(dwg_ref)
            dbg_ref[...] = jnp.zeros_like(dbg_ref)
            dgn_ref[...] = jnp.zeros_like(dgn_ref)

        row = lax.broadcasted_iota(jnp.int32, (CHUNK, CHUNK), 0)
        col = lax.broadcasted_iota(jnp.int32, (CHUNK, CHUNK), 1)
        tri_u = (col >= row).astype(BF16)
        ones = jnp.ones((CHUNK, DV), BF16)
        ones8 = jnp.ones((8, DV), BF16)
        for cc in range(NB - 1, -1, -1):
            rows = slice(cc * CHUNK, (cc + 1) * CHUNK)
            p = p_ref[rows, :]
            glr, gpre, la, ge, w = _gla_gates(p, wg_ref, bg_ref, QK)
            dla_heads = []
            for h in range(HEADS):
                ks, vs = slice(h * DK, (h + 1) * DK), slice(h * DV, (h + 1) * DV)
                qs = p[:, h * DK:(h + 1) * DK] * scale
                k = p[:, QK + h * DK:QK + (h + 1) * DK]
                v = p[:, 2 * QK + h * DV:2 * QK + (h + 1) * DV]
                r = p[:, 4 * QK + h * DV:4 * QK + (h + 1) * DV]
                wh = w[:, ks]
                kd = k * wh
                S = sc_ref[cc, ks, :]
                if cc > 0:
                    Sp = sc_ref[cc - 1, ks, :]
                else:
                    Sp = jnp.where(i < nB - 1, sp_ref[0, ks, :], 0.0)
                o = jnp.dot(qs.astype(BF16), S.astype(BF16), preferred_element_type=F32)
                rs = lax.rsqrt(jnp.mean(o * o, axis=-1, keepdims=True) + EPS)
                on = o * rs
                sr = _sigmoid(r)
                dg = dog_ref[rows, vs]
                gnh = gn_ref[:, vs]
                dp_ref[rows, 4 * QK + h * DV:4 * QK + (h + 1) * DV] = (
                    dg * on * gnh * (sr * (1.0 + r * (1.0 - sr)))).astype(BF16)
                dt = dg * (r * sr)
                dgn_ref[:, vs] += _rsum(dt * on)
                don = dt * gnh
                do = (rs * (don - on * jnp.mean(don * on, axis=-1, keepdims=True))).astype(BF16)
                Gc = gst[ks, :] + lax.dot_general(qs.astype(BF16), do, TN, preferred_element_type=F32)
                G16 = Gc.astype(BF16)
                dp_ref[rows, h * DK:(h + 1) * DK] = (
                    lax.dot_general(do, S.astype(BF16), NT, preferred_element_type=F32) * scale).astype(BF16)
                dkd = lax.dot_general(v.astype(BF16), G16, NT, preferred_element_type=F32)
                dp_ref[rows, 2 * QK + h * DV:2 * QK + (h + 1) * DV] = jnp.dot(
                    kd.astype(BF16), G16, preferred_element_type=F32).astype(BF16)
                gst[ks, :] = jnp.exp(_dot3(la[:, ks], ones, TN, "lhs")) * Gc
                ddec = _dot3(ones8, Gc * Sp, NT, "rhs")[0:1, :]
                dp_ref[rows, QK + h * DK:QK + (h + 1) * DK] = (dkd * wh).astype(BF16)
                dww = dkd * kd
                dge = jnp.exp(ge[:, ks]) * ddec + _rsum(dww)
                dla_heads.append(dge - _dot3(tri_u, dww, NN, "rhs"))
            dla = jnp.concatenate(dla_heads, axis=1)
            dgpre = dla * (1.0 / GATE_TAU) * (1.0 - _sigmoid(gpre))
            d16 = dgpre.astype(BF16)
            dp_ref[rows, 6 * QK:6 * QK + 128] = lax.dot_general(d16, wg_ref[...], NT, preferred_element_type=F32).astype(BF16)
            dwg_ref[...] += lax.dot_general(glr, d16, TN, preferred_element_type=F32)
            dbg_ref[...] += _rsum(dgpre)

    RB = NB * CHUNK
    rev = lambda i: (nB - 1 - i, 0)
    return pl.pallas_call(
        body, name=name, grid=(nB,),
        in_specs=[pl.BlockSpec((RB, W), rev), pl.BlockSpec((RB, 2 * QK), rev),
                  pl.BlockSpec((NB, QK, DV), lambda i: (nB - 1 - i, 0, 0)),
                  pl.BlockSpec((1, QK, DV), lambda i: (jnp.maximum(NB * (nB - 1 - i) - 1, 0), 0, 0)),
                  _whole(wg2p.shape), _whole(bg.shape), _whole(gn.shape)],
        out_specs=[pl.BlockSpec((RB, W), rev), _whole((128, QK)), _whole((1, QK)), _whole((1, 2 * QK))],
        out_shape=[jax.ShapeDtypeStruct((L, W), BF16), jax.ShapeDtypeStruct((128, QK), F32),
                   jax.ShapeDtypeStruct((1, QK), F32), jax.ShapeDtypeStruct((1, 2 * QK), F32)],
        scratch_shapes=[pltpu.VMEM((QK, DV), F32)],
        compiler_params=_params("arbitrary"),
    )(proj, dog, states, states, wg2p, bg, gn)


def _chunk_tri(rows, upper):
    r = lax.broadcasted_iota(jnp.int32, (rows, rows), 0)
    c = lax.broadcasted_iota(jnp.int32, (rows, rows), 1)
    same = (r // CHUNK) == (c // CHUNK)
    return (same & ((c >= r) if upper else (r >= c))).astype(BF16)


def _gla_block_gates(p_ref, wg_ref, bg_ref, QK, wbuf, gebuf):
    RB = p_ref.shape[0]
    glr = p_ref[:, 6 * QK:6 * QK + 128].astype(BF16)
    gpre = jnp.dot(glr, wg_ref[...], preferred_element_type=F32) + bg_ref[...]
    la = _log_sigmoid(gpre) * (1.0 / GATE_TAU)
    gc = _dot3(_chunk_tri(RB, False), la, NN, "rhs")
    for cc in range(RB // CHUNK):
        rows = slice(cc * CHUNK, (cc + 1) * CHUNK)
        ge = gc[(cc + 1) * CHUNK - 1:(cc + 1) * CHUNK, :]
        gebuf[cc:cc + 1, :] = ge
        wbuf[rows, :] = jnp.exp(ge - gc[rows, :])
    return glr, gpre, la


def _as_column(row, lanes):
    t = jnp.transpose(jnp.broadcast_to(row, (row.shape[1], row.shape[1])))
    return jnp.concatenate([t] * (lanes // row.shape[1]), axis=1)


def _as_row(col):
    return jnp.transpose(jnp.broadcast_to(col, (col.shape[0], col.shape[0])))[0:1, :]


def _gla_fwd2(proj, wg2p, bg, gn, *, name):
    L = proj.shape[0]
    QK = wg2p.shape[1]
    DK, DV = QK // HEADS, 2 * QK // HEADS
    nC = L // CHUNK
    NB = min(GLA_NB, nC)
    assert nC % NB == 0
    scale = DK ** -0.5

    def body(p_ref, wg_ref, bg_ref, gn_ref, og_ref, s_ref, sst, wbuf, gebuf):
        @pl.when(pl.program_id(0) == 0)
        def _():
            sst[...] = jnp.zeros_like(sst)

        _gla_block_gates(p_ref, wg_ref, bg_ref, QK, wbuf, gebuf)
        for cc in range(NB):
            rows = slice(cc * CHUNK, (cc + 1) * CHUNK)
            for h in range(HEADS):
                ks, vs = slice(h * DK, (h + 1) * DK), slice(h * DV, (h + 1) * DV)
                q = p_ref[rows, h * DK:(h + 1) * DK] * scale
                kd = p_ref[rows, QK + h * DK:QK + (h + 1) * DK] * wbuf[rows, ks]
                v = p_ref[rows, 2 * QK + h * DV:2 * QK + (h + 1) * DV]
                r = p_ref[rows, 4 * QK + h * DV:4 * QK + (h + 1) * DV]
                dec = jnp.exp(_as_column(gebuf[cc:cc + 1, ks], DV))
                kv = lax.dot_general(kd.astype(BF16), v.astype(BF16), TN, preferred_element_type=F32)
                S = dec * sst[ks, :] + kv
                sst[ks, :] = S
                s_ref[cc, ks, :] = S
                o = jnp.dot(q.astype(BF16), S.astype(BF16), preferred_element_type=F32)
                on = o * lax.rsqrt(jnp.mean(o * o, axis=-1, keepdims=True) + EPS)
                og_ref[rows, vs] = (on * gn_ref[:, vs] * (r * _sigmoid(r))).astype(BF16)

    RB = NB * CHUNK
    return pl.pallas_call(
        body, name=name, grid=(nC // NB,),
        in_specs=[pl.BlockSpec((RB, proj.shape[1]), lambda i: (i, 0)), _whole(wg2p.shape), _whole(bg.shape), _whole(gn.shape)],
        out_specs=[pl.BlockSpec((RB, 2 * QK), lambda i: (i, 0)), pl.BlockSpec((NB, QK, DV), lambda i: (i, 0, 0))],
        out_shape=[jax.ShapeDtypeStruct((L, 2 * QK), BF16), jax.ShapeDtypeStruct((nC, QK, DV), F32)],
        scratch_shapes=[pltpu.VMEM((QK, DV), F32), pltpu.VMEM((RB, QK), F32), pltpu.VMEM((8, QK), F32)],
        compiler_params=_params("arbitrary"),
    )(proj, wg2p, bg, gn)


def _gla_bwd2(proj, dog, states, wg2p, bg, gn, *, name):
    L, W = proj.shape
    QK = wg2p.shape[1]
    DK, DV = QK // HEADS, 2 * QK // HEADS
    nC = L // CHUNK
    NB = min(GLA_NB, nC)
    nB = nC // NB
    scale = DK ** -0.5

    def body(p_ref, dog_ref, sc_ref, sp_ref, wg_ref, bg_ref, gn_ref, dp_ref, dwg_ref, dbg_ref, dgn_ref,
             gst, wbuf, gebuf, dwwbuf, dgebuf):
        i = pl.program_id(0)

        @pl.when(i == 0)
        def _():
            gst[...] = jnp.zeros_like(gst)
            dwg_ref[...] = jnp.zeros_like(dwg_ref)
            dbg_ref[...] = jnp.zeros_like(dbg_ref)
            dgn_ref[...] = jnp.zeros_like(dgn_ref)

        RB = NB * CHUNK
        glr, gpre, _ = _gla_block_gates(p_ref, wg_ref, bg_ref, QK, wbuf, gebuf)
        for cc in range(NB - 1, -1, -1):
            rows = slice(cc * CHUNK, (cc + 1) * CHUNK)
            for h in range(HEADS):
                ks, vs = slice(h * DK, (h + 1) * DK), slice(h * DV, (h + 1) * DV)
                qs = p_ref[rows, h * DK:(h + 1) * DK] * scale
                wh = wbuf[rows, ks]
                kd = p_ref[rows, QK + h * DK:QK + (h + 1) * DK] * wh
                v = p_ref[rows, 2 * QK + h * DV:2 * QK + (h + 1) * DV]
                r = p_ref[rows, 4 * QK + h * DV:4 * QK + (h + 1) * DV]
                S = sc_ref[cc, ks, :]
                if cc > 0:
                    Sp = sc_ref[cc - 1, ks, :]
                else:
                    Sp = jnp.where(i < nB - 1, sp_ref[0, ks, :], 0.0)
                o = jnp.dot(qs.astype(BF16), S.astype(BF16), preferred_element_type=F32)
                rs = lax.rsqrt(jnp.mean(o * o, axis=-1, keepdims=True) + EPS)
                on = o * rs
                sr = _sigmoid(r)
                dg = dog_ref[rows, vs]
                gnh = gn_ref[:, vs]
                dp_ref[rows, 4 * QK + h * DV:4 * QK + (h + 1) * DV] = (
                    dg * on * gnh * (sr * (1.0 + r * (1.0 - sr)))).astype(BF16)
                dt = dg * (r * sr)
                dgn_ref[:, vs] += _rsum(dt * on)
                don = dt * gnh
                do = (rs * (don - on * jnp.mean(don * on, axis=-1, keepdims=True))).astype(BF16)
                Gc = gst[ks, :] + lax.dot_general(qs.astype(BF16), do, TN, preferred_element_type=F32)
                G16 = Gc.astype(BF16)
                dp_ref[rows, h * DK:(h + 1) * DK] = (
                    lax.dot_general(do, S.astype(BF16), NT, preferred_element_type=F32) * scale).astype(BF16)
                dkd = lax.dot_general(v.astype(BF16), G16, NT, preferred_element_type=F32)
                dp_ref[rows, 2 * QK + h * DV:2 * QK + (h + 1) * DV] = jnp.dot(
                    kd.astype(BF16), G16, preferred_element_type=F32).astype(BF16)
                ge = gebuf[cc:cc + 1, ks]
                gst[ks, :] = jnp.exp(_as_column(ge, DV)) * Gc
                ddec = _as_row(jnp.sum(Gc * Sp, axis=1, keepdims=True))
                dp_ref[rows, QK + h * DK:QK + (h + 1) * DK] = (dkd * wh).astype(BF16)
                dww = dkd * kd
                dwwbuf[rows, ks] = dww
                dgebuf[cc:cc + 1, ks] = jnp.exp(ge) * ddec + _rsum(dww)
        rev = _dot3(_chunk_tri(RB, True), dwwbuf[...], NN, "rhs")
        for cc in range(NB):
            rows = slice(cc * CHUNK, (cc + 1) * CHUNK)
            wbuf[rows, :] = dgebuf[cc:cc + 1, :] - rev[rows, :]
        dgpre = wbuf[...] * (1.0 / GATE_TAU) * (1.0 - _sigmoid(gpre))
        d16 = dgpre.astype(BF16)
        dp_ref[:, 6 * QK:6 * QK + 128] = lax.dot_general(d16, wg_ref[...], NT, preferred_element_type=F32).astype(BF16)
        dwg_ref[...] += lax.dot_general(glr, d16, TN, preferred_element_type=F32)
        dbg_ref[...] += _rsum(dgpre)

    RB = NB * CHUNK
    rev_idx = lambda i: (nB - 1 - i, 0)
    return pl.pallas_call(
        body, name=name, grid=(nB,),
        in_specs=[pl.BlockSpec((RB, W), rev_idx), pl.BlockSpec((RB, 2 * QK), rev_idx),
                  pl.BlockSpec((NB, QK, DV), lambda i: (nB - 1 - i, 0, 0)),
                  pl.BlockSpec((1, QK, DV), lambda i: (jnp.maximum(NB * (nB - 1 - i) - 1, 0), 0, 0)),
                  _whole(wg2p.shape), _whole(bg.shape), _whole(gn.shape)],
        out_specs=[pl.BlockSpec((RB, W), rev_idx), _whole((128, QK)), _whole((1, QK)), _whole((1, 2 * QK))],
        out_shape=[jax.ShapeDtypeStruct((L, W), BF16), jax.ShapeDtypeStruct((128, QK), F32),
                   jax.ShapeDtypeStruct((1, QK), F32), jax.ShapeDtypeStruct((1, 2 * QK), F32)],
        scratch_shapes=[pltpu.VMEM((QK, DV), F32), pltpu.VMEM((RB, QK), F32), pltpu.VMEM((8, QK), F32),
                        pltpu.VMEM((RB, QK), F32), pltpu.VMEM((8, QK), F32)],
        compiler_params=_params("arbitrary"),
    )(proj, dog, states, states, wg2p, bg, gn)


def _coords():
    return lax.axis_index("x"), lax.axis_index("y"), lax.axis_index("c")


def _other_chips(x, y):
    return [(1 - x, y, 2 * (1 - x) + y), (x, 1 - y, 2 * x + 1 - y), (1 - x, 1 - y, 2 * (1 - x) + 1 - y)]


def _hbm_call(body, ins, out_shapes, n_sems, *, name, alias=False):
    any_spec = pl.BlockSpec(memory_space=pl.ANY)
    return pl.pallas_call(
        body, name=name, in_specs=[any_spec] * len(ins), out_specs=[any_spec] * len(out_shapes), out_shape=out_shapes,
        scratch_shapes=[pltpu.SemaphoreType.DMA((n,)) for n in n_sems],
        input_output_aliases={k: k for k in range(len(ins))} if alias else {},
    )(*ins)


def _exchange(src, masks, *, name):
    vary = [any(m[k] for m in masks) for k in range(3)]
    nslots = 2 ** sum(vary)
    n = len(masks)

    def slot(coords):
        s = 0
        for k in range(3):
            if vary[k]:
                s = s * 2 + coords[k]
        return s

    def body(src_ref, dst_ref, send_sems, recv_sems, loc_sem):
        me = _coords()
        mine = slot(me)
        loc = pltpu.make_async_copy(src_ref, dst_ref.at[mine], loc_sem.at[0])
        loc.start()
        copies = []
        for k, m in enumerate(masks):
            peer = tuple(1 - me[d] if m[d] else me[d] for d in range(3))
            cp = pltpu.make_async_remote_copy(src_ref=src_ref, dst_ref=dst_ref.at[mine], send_sem=send_sems.at[k],
                                              recv_sem=recv_sems.at[k], device_id=peer, device_id_type=MESH)
            cp.start()
            copies.append(cp)
        for cp in copies:
            cp.wait()
        loc.wait()

    return _hbm_call(body, [src], [jax.ShapeDtypeStruct((nslots,) + tuple(src.shape), src.dtype)], (n, n, 1), name=name)[0]


def _cast_into(t, lead, kind, chip, *, name, tm=256):
    r, cc = t.shape[-2:]
    tm = min(tm, r)
    nblk = r // tm
    if kind == "col":
        shp, o_spec = (r, NCH * cc), pl.BlockSpec((tm, cc), lambda i, s: (i, s[0]))
    elif kind == "row":
        shp, o_spec = (NCH * r, cc), pl.BlockSpec((tm, cc), lambda i, s: (s[0] * nblk + i, 0))
    else:
        shp, o_spec = (NCH, r, cc), pl.BlockSpec((None, tm, cc), lambda i, s: (s[0], i, 0))

    def body(s_ref, t_ref, o_ref):
        o_ref[...] = t_ref[...].astype(o_ref.dtype)

    return pl.pallas_call(
        body, name=name,
        grid_spec=pltpu.PrefetchScalarGridSpec(
            num_scalar_prefetch=1, grid=(nblk,),
            in_specs=[pl.BlockSpec((None, tm, cc), lambda i, s: (lead, i, 0))], out_specs=o_spec),
        out_shape=jax.ShapeDtypeStruct(shp, BF16), compiler_params=_params("parallel"),
    )(chip.reshape(1).astype(jnp.int32), t)


def _gather_weights(arrs, shard_shapes, kinds, *, name):
    n = len(arrs)

    def body(*refs):
        dst = refs[n:2 * n]
        send_sems, recv_sems = refs[2 * n:]
        x, y, c = _coords()
        chip = 2 * x + y
        others = _other_chips(x, y)
        sib = (x, y, 1 - c)

        def window(p, chip_id, cc):
            r, cols = shard_shapes[p]
            h = r // 2
            if kinds[p] == "col":
                return dst[p].at[pl.ds(cc * h, h), pl.ds(pl.multiple_of(chip_id * cols, 128), cols)]
            if kinds[p] == "row":
                return dst[p].at[pl.ds(chip_id * r + cc * h, h), :]
            return dst[p].at[chip_id, pl.ds(cc * h, h), :]

        def copy(p, k, win, to):
            return pltpu.make_async_remote_copy(src_ref=win, dst_ref=win, send_sem=send_sems.at[6 * p + k],
                                                recv_sem=recv_sems.at[6 * p + k], device_id=to, device_id_type=MESH)

        sends = []
        for p in range(n):
            for j, (ox, oy, _) in enumerate(others):
                cp = copy(p, j, window(p, chip, c), (ox, oy, c))
                cp.start()
                sends.append(cp)
        for j, (_, _, oc) in enumerate(others):
            for p in range(n):
                copy(p, j, window(p, oc, c), (x, y, c)).wait_recv()
                fw = copy(p, 3 + j, window(p, oc, c), sib)
                fw.start()
                sends.append(fw)
        for p in range(n):
            for j, (_, _, oc) in enumerate(others):
                copy(p, 3 + j, window(p, oc, 1 - c), sib).wait_recv()
        for cp in sends:
            cp.wait_send()

    outs = [jax.ShapeDtypeStruct(a.shape, a.dtype) for a in arrs]
    return _hbm_call(body, arrs, outs, (6 * n, 6 * n), name=name, alias=True)


HBM_SPEC = pl.BlockSpec(memory_space=pltpu.HBM)
SEM_SPEC = pl.BlockSpec(memory_space=pltpu.SEMAPHORE)
EFFECT = pltpu.SideEffectType.DATAFLOW_SIDE_EFFECTING


def _window(ref, shard_shape, kind, chip_id, cc):
    r, cols = shard_shape
    h = r // 2
    if kind == "col":
        return ref.at[pl.ds(cc * h, h), pl.ds(pl.multiple_of(chip_id * cols, 128), cols)]
    if kind == "row":
        return ref.at[pl.ds(chip_id * r + cc * h, h), :]
    return ref.at[chip_id, pl.ds(cc * h, h), :]


def _split_start(start, arrs, n_sems, *, name):
    n, ns = len(arrs), len(n_sems)

    def body(*refs):
        start(refs[:n], refs[n:n + ns])
        refs[-1][...] = jnp.zeros_like(refs[-1])

    outs = pl.pallas_call(
        body, name=name,
        out_shape=tuple([pltpu.SemaphoreType.DMA((k,)) for k in n_sems] + [pltpu.HBM(a.shape, a.dtype) for a in arrs]
                        + [jax.ShapeDtypeStruct((8, 128), F32)]),
        in_specs=[HBM_SPEC] * n, out_specs=tuple([SEM_SPEC] * ns + [HBM_SPEC] * n + [pl.BlockSpec(memory_space=pltpu.VMEM)]),
        input_output_aliases={k: ns + k for k in range(n)},
        compiler_params=pltpu.CompilerParams(has_side_effects=EFFECT),
    )(*[pltpu.with_memory_space_constraint(a, pltpu.HBM) for a in arrs])
    return list(outs[:ns]), list(outs[ns:ns + n]), outs[-1]


def _split_wait(wait, arrs, sems, after, *, name):
    n, ns = len(arrs), len(sems)

    def body(*refs):
        wait(refs[:n], refs[n:n + ns])

    return pl.pallas_call(
        body, name=name, out_shape=tuple(pltpu.HBM(a.shape, a.dtype) for a in arrs),
        in_specs=[HBM_SPEC] * n + [SEM_SPEC] * ns + [pl.BlockSpec(memory_space=pl.ANY)], out_specs=tuple([HBM_SPEC] * n),
        input_output_aliases={k: k for k in range(n)},
        compiler_params=pltpu.CompilerParams(has_side_effects=EFFECT),
    )(*arrs, *sems, after)


def _gw_copies(refs, send_sems, recv_sems, shard_shapes, kinds, outgoing):
    x, y, c = _coords()
    chip = 2 * x + y
    out = []
    for p in range(len(refs)):
        for j, (ox, oy, oc) in enumerate(_other_chips(x, y)):
            win = _window(refs[p], shard_shapes[p], kinds[p], chip if outgoing else oc, c)
            out.append(pltpu.make_async_remote_copy(
                src_ref=win, dst_ref=win, send_sem=send_sems.at[3 * p + j], recv_sem=recv_sems.at[3 * p + j],
                device_id=(ox, oy, c), device_id_type=MESH))
    return out


def _gw_start(arrs, shard_shapes, kinds, groups, *, name):
    def start(refs, sems):
        for g, idx in enumerate(groups):
            for cp in _gw_copies([refs[p] for p in idx], sems[2 * g], sems[2 * g + 1], [shard_shapes[p] for p in idx],
                                 [kinds[p] for p in idx], True):
                cp.start()

    n_sems = [3 * len(idx) for idx in groups for _ in range(2)]
    sems, thru, token = _split_start(start, arrs, n_sems, name=name)
    return [(sems[2 * g], sems[2 * g + 1]) for g in range(len(groups))], thru, token


def _gw_wait(arrs, shard_shapes, kinds, sem_pair, after, *, name):
    def wait(refs, sems):
        for cp in _gw_copies(refs, sems[0], sems[1], shard_shapes, kinds, True):
            cp.wait_send()
        for cp in _gw_copies(refs, sems[0], sems[1], shard_shapes, kinds, False):
            cp.wait_recv()

    return _split_wait(wait, arrs, list(sem_pair), after, name=name)


def _gw_forward(arrs, shard_shapes, kinds, *, name):
    n = len(arrs)

    def body(*refs):
        dst = refs[n:2 * n]
        send_sems, recv_sems = refs[2 * n:]
        x, y, c = _coords()
        sends = []
        for p in range(n):
            for j, (_, _, oc) in enumerate(_other_chips(x, y)):
                win = _window(dst[p], shard_shapes[p], kinds[p], oc, c)
                cp = pltpu.make_async_remote_copy(src_ref=win, dst_ref=win, send_sem=send_sems.at[3 * p + j],
                                                  recv_sem=recv_sems.at[3 * p + j], device_id=(x, y, 1 - c),
                                                  device_id_type=MESH)
                cp.start()
                sends.append(cp)
        for p in range(n):
            for j, (_, _, oc) in enumerate(_other_chips(x, y)):
                win = _window(dst[p], shard_shapes[p], kinds[p], oc, 1 - c)
                pltpu.make_async_remote_copy(src_ref=win, dst_ref=win, send_sem=send_sems.at[3 * p + j],
                                             recv_sem=recv_sems.at[3 * p + j], device_id=(x, y, 1 - c),
                                             device_id_type=MESH).wait_recv()
        for cp in sends:
            cp.wait_send()

    outs = [jax.ShapeDtypeStruct(a.shape, a.dtype) for a in arrs]
    return _hbm_call(body, arrs, outs, (3 * n, 3 * n), name=name, alias=True)


def _rs_chips_copies(parts, lands, send_sems, recv_sems):
    x, y, c = _coords()
    chip = 2 * x + y
    out = []
    for p in range(len(parts)):
        for j, (ox, oy, oc) in enumerate(_other_chips(x, y)):
            out.append(pltpu.make_async_remote_copy(
                src_ref=parts[p].at[oc], dst_ref=lands[p].at[chip], send_sem=send_sems.at[3 * p + j],
                recv_sem=recv_sems.at[3 * p + j], device_id=(ox, oy, c), device_id_type=MESH))
    return out


def _rs_chips_start(parts, *, name):
    n = len(parts)

    def start(refs, sems):
        for cp in _rs_chips_copies(refs[:n], refs[n:], sems[0], sems[1]):
            cp.start()

    lands = [lax.empty(t.shape, t.dtype) for t in parts]
    sems, thru, token = _split_start(start, list(parts) + lands, [3 * n, 3 * n], name=name)
    return (sems[0], sems[1]), thru[:n], thru[n:], token


def _rs_chips_wait(groups, after, *, name):
    sizes = [len(g[1]) for g in groups]
    arrs = [a for g in groups for a in list(g[1]) + list(g[2])]
    sems = [s for g in groups for s in g[0]]

    def wait(refs, sem_refs):
        o = 0
        for k, n in enumerate(sizes):
            for cp in _rs_chips_copies(refs[o:o + n], refs[o + n:o + 2 * n], sem_refs[2 * k], sem_refs[2 * k + 1]):
                cp.wait()
            o += 2 * n

    outs = _split_wait(wait, arrs, sems, after, name=name)
    res, o = [], 0
    for n in sizes:
        res.append((list(outs[o:o + n]), list(outs[o + n:o + 2 * n])))
        o += 2 * n
    return res


def _rs_cores_copies(grads, lands, send_sems, recv_sems):
    x, y, c = _coords()
    out, o = [], 0
    for p in range(len(grads)):
        nsh, h = lands[p].shape[0], lands[p].shape[1]
        for j in range(nsh):
            out.append(pltpu.make_async_remote_copy(
                src_ref=grads[p].at[j, pl.ds((1 - c) * h, h), :], dst_ref=lands[p].at[j],
                send_sem=send_sems.at[o + j], recv_sem=recv_sems.at[o + j], device_id=(x, y, 1 - c), device_id_type=MESH))
        o += nsh
    return out


def _rs_cores_start(grads, *, name):
    n = len(grads)
    tot = sum(g.shape[0] for g in grads)

    def start(refs, sems):
        for cp in _rs_cores_copies(refs[:n], refs[n:], sems[0], sems[1]):
            cp.start()

    lands = [lax.empty((g.shape[0], g.shape[1] // 2, g.shape[2]), g.dtype) for g in grads]
    sems, thru, token = _split_start(start, list(grads) + lands, [tot, tot], name=name)
    return (sems[0], sems[1]), thru[:n], thru[n:], token


def _rs_cores_wait(pair, grads, lands, after, *, name):
    n = len(grads)

    def wait(refs, sems):
        for cp in _rs_cores_copies(refs[:n], refs[n:], sems[0], sems[1]):
            cp.wait()

    outs = _split_wait(wait, list(grads) + list(lands), list(pair), after, name=name)
    return list(outs[:n]), list(outs[n:])


def _rs_cores(grads, *, name):
    n = len(grads)
    outs = [jax.ShapeDtypeStruct((g.shape[0], g.shape[1] // 2, g.shape[2]), g.dtype) for g in grads]

    def body(*refs):
        src, dst = refs[:n], refs[n:2 * n]
        send_sems, recv_sems = refs[2 * n:]
        x, y, c = _coords()
        copies = []
        for p in range(n):
            nsh, r, _ = grads[p].shape
            h = r // 2
            for j in range(nsh):
                cp = pltpu.make_async_remote_copy(
                    src_ref=src[p].at[j, pl.ds((1 - c) * h, h), :], dst_ref=dst[p].at[j],
                    send_sem=send_sems.at[nsh * p + j], recv_sem=recv_sems.at[nsh * p + j],
                    device_id=(x, y, 1 - c), device_id_type=MESH)
                cp.start()
                copies.append(cp)
        for cp in copies:
            cp.wait()

    tot = sum(g.shape[0] for g in grads)
    return _hbm_call(body, grads, outs, (tot, tot), name=name)


def _sum_own_half(full, recv, ci, out_dtype, *, name):
    nsh, h, cols = recv.shape
    tm = h if nsh * h * cols * 4 <= (2 << 20) else _tile_rows(h, 256)
    nblk = h // tm

    def body(c_ref, f_ref, r_ref, o_ref):
        o_ref[...] = (f_ref[...] + r_ref[...]).astype(o_ref.dtype)

    return pl.pallas_call(
        body, name=name,
        grid_spec=pltpu.PrefetchScalarGridSpec(
            num_scalar_prefetch=1, grid=(nsh, nblk),
            in_specs=[pl.BlockSpec((1, tm, cols), lambda j, i, c_ref: (j, c_ref[0] * nblk + i, 0)),
                      pl.BlockSpec((1, tm, cols), lambda j, i, c_ref: (j, i, 0))],
            out_specs=pl.BlockSpec((1, tm, cols), lambda j, i, c_ref: (j, i, 0))),
        out_shape=jax.ShapeDtypeStruct((nsh, h, cols), out_dtype), compiler_params=_params("parallel", "parallel"),
    )(ci.reshape(1).astype(jnp.int32), full, recv)


def _rs_chips(parts, *, name):
    n = len(parts)
    outs = [jax.ShapeDtypeStruct(t.shape, t.dtype) for t in parts]

    def body(*refs):
        src, dst = refs[:n], refs[n:2 * n]
        send_sems, recv_sems = refs[2 * n:]
        x, y, c = _coords()
        chip = 2 * x + y
        copies = []
        for p in range(n):
            for j, (ox, oy, oc) in enumerate(_other_chips(x, y)):
                cp = pltpu.make_async_remote_copy(
                    src_ref=src[p].at[oc], dst_ref=dst[p].at[chip], send_sem=send_sems.at[3 * p + j],
                    recv_sem=recv_sems.at[3 * p + j], device_id=(ox, oy, c), device_id_type=MESH)
                cp.start()
                copies.append(cp)
        for cp in copies:
            cp.wait()

    return _hbm_call(body, parts, outs, (3 * n, 3 * n), name=name)


def _sum_chips(recv, own, chip, ci, *, name, nlead=1, lead=0, prev=None, spread=False):
    nsh, h, cols = recv.shape
    tm = h if nsh * h * cols * 4 <= (2 << 20) else _tile_rows(h, 256)
    nblk = h // tm
    rows_out = 2 * h * (nsh if spread else 1)

    def body(s_ref, r_ref, o_ref, *rest):
        out_ref = rest[-1]
        t = None
        for s in range(nsh):
            v = jnp.where(s_ref[0] == s, o_ref[s], r_ref[s]).astype(F32)
            t = v if t is None else t + v
        out_ref[...] = t

    def out_idx(i, s):
        return (lead, (s[0] * 2 * nblk if spread else 0) + s[1] * nblk + i, 0)

    blk = pl.BlockSpec((nsh, tm, cols), lambda i, s: (0, i, 0))
    ins = [recv, own] + ([prev] if prev is not None else [])
    return pl.pallas_call(
        body, name=name,
        grid_spec=pltpu.PrefetchScalarGridSpec(
            num_scalar_prefetch=1, grid=(nblk,),
            in_specs=[blk, blk] + ([pl.BlockSpec(memory_space=pl.ANY)] if prev is not None else []),
            out_specs=pl.BlockSpec((None, tm, cols), out_idx)),
        out_shape=jax.ShapeDtypeStruct((nlead, rows_out, cols), F32),
        input_output_aliases={3: 0} if prev is not None else {},
        compiler_params=_params("arbitrary"),
    )(jnp.stack([chip, ci]).astype(jnp.int32), *ins)


def _rs_gather(arrs, halves, spread, *, name, nchunk=4):
    n = len(arrs)
    per = [a.shape[0] * nchunk for a in arrs]
    offs = [sum(per[:p]) for p in range(n)]

    def body(*refs):
        dst = refs[n:2 * n]
        send_sems, recv_sems = refs[2 * n:]
        x, y, c = _coords()
        chip = 2 * x + y
        copies = []
        for p in range(n):
            h = halves[p]
            q = h // nchunk
            base = chip * 2 * h if spread[p] else 0
            for l in range(arrs[p].shape[0]):
                for k in range(nchunk):
                    win = dst[p].at[l, pl.ds(base + c * h + k * q, q), :]
                    sem = offs[p] + l * nchunk + k
                    cp = pltpu.make_async_remote_copy(src_ref=win, dst_ref=win, send_sem=send_sems.at[sem],
                                                      recv_sem=recv_sems.at[sem], device_id=(x, y, 1 - c),
                                                      device_id_type=MESH)
                    cp.start()
                    copies.append(cp)
        for cp in copies:
            cp.wait_send()
        for p in range(n):
            h = halves[p]
            q = h // nchunk
            base = chip * 2 * h if spread[p] else 0
            for l in range(arrs[p].shape[0]):
                for k in range(nchunk):
                    win = dst[p].at[l, pl.ds(base + (1 - c) * h + k * q, q), :]
                    sem = offs[p] + l * nchunk + k
                    pltpu.make_async_remote_copy(src_ref=win, dst_ref=win, send_sem=send_sems.at[sem],
                                                 recv_sem=recv_sems.at[sem], device_id=(x, y, 1 - c),
                                                 device_id_type=MESH).wait_recv()

    outs = [jax.ShapeDtypeStruct(a.shape, a.dtype) for a in arrs]
    return _hbm_call(body, arrs, outs, (sum(per), sum(per)), name=name, alias=True)


def _adamw(w, g, m, v, *, name):
    nl, R, C = w.shape
    tm = _tile_rows(R, 256)

    blk = pl.BlockSpec((None, tm, C), lambda l, i: (l, i, 0))
    return pl.pallas_call(
        _adamw_body_copy(), name=name, grid=(nl, R // tm), in_specs=[blk] * 4, out_specs=[blk] * 3,
        out_shape=[jax.ShapeDtypeStruct((nl, R, C), F32)] * 3, compiler_params=_params("parallel", "parallel"),
    )(w, g, m, v)


def _adamw_body(w_ref, g_ref, m_ref, v_ref, d_ref, nm_ref, nv_ref):
    gg = g_ref[...]
    nm = B1 * m_ref[...] + (1.0 - B1) * gg
    nv = B2 * v_ref[...] + (1.0 - B2) * (gg * gg)
    m_hat = nm / (1.0 - B1 ** ASTEP)
    v_hat = nv / (1.0 - B2 ** ASTEP)
    d_ref[...] = -LR * (m_hat / (jnp.sqrt(v_hat) + AEPS) + WD * w_ref[...])
    nm_ref[...] = nm
    nv_ref[...] = nv


def _adamw_whole(w, g, m, v, *, name):
    return pl.pallas_call(_adamw_body_copy(), name=name, out_shape=[jax.ShapeDtypeStruct(w.shape, F32)] * 3,
                          compiler_params=_params())(w, g, m, v)


def _adamw_body_copy():
    def body(*refs):
        _adamw_body(*refs)
    return body


def _mod_cols(c_all, w_ada, b_cols, *, name):
    nl, D, cols = w_ada.shape
    B = c_all.shape[0]

    def body(c_ref, w_ref, b_ref, o_ref):
        cc = c_ref[...]
        cs = (cc * _sigmoid(cc)).astype(BF16)
        o_ref[0] = jnp.dot(cs, w_ref[0].astype(BF16), preferred_element_type=F32) + b_ref[0]

    return pl.pallas_call(
        body, name=name, grid=(nl,),
        in_specs=[_whole(c_all.shape), pl.BlockSpec((1, D, cols), lambda i: (i, 0, 0)), pl.BlockSpec((1, 1, cols), lambda i: (i, 0, 0))],
        out_specs=pl.BlockSpec((1, B, cols), lambda i: (i, 0, 0)),
        out_shape=jax.ShapeDtypeStruct((nl, B, cols), F32), compiler_params=_params("arbitrary"),
    )(c_all, w_ada, b_cols)


def _ada_grad(c_all, dmod_cols, *, name):
    nl, B, cols = dmod_cols.shape
    D = c_all.shape[1]

    def body(c_ref, d_ref, o_ref):
        cc = c_ref[...]
        cs = (cc * _sigmoid(cc)).astype(BF16)
        o_ref[0] = lax.dot_general(cs, d_ref[0].astype(BF16), TN, preferred_element_type=F32)

    return pl.pallas_call(
        body, name=name, grid=(nl,),
        in_specs=[_whole(c_all.shape), pl.BlockSpec((1, B, cols), lambda i: (i, 0, 0))],
        out_specs=pl.BlockSpec((1, D, cols), lambda i: (i, 0, 0)),
        out_shape=jax.ShapeDtypeStruct((nl, D, cols), F32), compiler_params=_params("arbitrary"),
    )(c_all, dmod_cols)


def _s5_disc(a_re, a_im, log_dt, b_re, b_im):
    dt = jnp.exp(log_dt)[:, None]
    mag = jnp.exp(a_re * dt)
    ph = a_im * dt
    lb_re = mag * jnp.cos(ph)
    lb_im = mag * jnp.sin(ph)
    den = a_re * a_re + a_im * a_im
    nr = lb_re - 1.0
    ni = lb_im
    f_re = (nr * a_re + ni * a_im) / den
    f_im = (ni * a_re - nr * a_im) / den
    bb_re = f_re[..., None] * b_re - f_im[..., None] * b_im
    bb_im = f_re[..., None] * b_im + f_im[..., None] * b_re
    return lb_re, lb_im, bb_re, bb_im


def _to_segments(t):
    L, D = t.shape
    R = min(S5_R, L)
    return t.reshape(L // R, NSEG, R // NSEG, D).transpose(0, 2, 1, 3).reshape(L, D)


def _from_segments(t):
    L, D = t.shape
    R = min(S5_R, L)
    return t.reshape(L // R, R // NSEG, NSEG, D).transpose(0, 2, 1, 3).reshape(L, D)


def _mlp_fwd(h2, w1, w2, tag):
    a = _matmul(h2, w1, name=f"ff1_{tag}", out_dtypes=(BF16,), epi=lambda acc: (jnp.maximum(acc, 0.0),))
    f = _matmul(a, w2, name=f"ff2_{tag}", a_fn=jnp.square)
    return a, f


def _mlp_bwd(df, h2, a, w1, w2, tag):
    da = _matmul(df, w2, tb=True, name=f"ff2_dx_{tag}", out_dtypes=(BF16,), epi_ins=(a,),
                 epi=lambda acc, at: (acc * (2.0 * at.astype(F32)),))
    dw2 = _matmul(a, df, ta=True, name=f"ff2_dw_{tag}", a_fn=jnp.square)
    dh2 = _matmul(da, w1, tb=True, name=f"ff1_dx_{tag}")
    dw1 = _matmul(h2, da, ta=True, name=f"ff1_dw_{tag}", col_shards=NCH)
    return dh2, dw1, dw2


def kernel(x, c, w_ada, b_ada, norm_mix, norm_mlp, s5_a_re, s5_a_im, s5_log_dt, s5_b_re, s5_b_im, s5_c_re, s5_c_im, s5_d, s5_w_glu, gla_w_in, gla_w_gate2, gla_b_gate, gla_g_norm, gla_w_out, w_ff1, w_ff2, norm_final, loss_target, m_w_ada, m_b_ada, m_norm_mix, m_norm_mlp, m_s5_a_re, m_s5_a_im, m_s5_log_dt, m_s5_b_re, m_s5_b_im, m_s5_c_re, m_s5_c_im, m_s5_d, m_s5_w_glu, m_gla_w_in, m_gla_w_gate2, m_gla_b_gate, m_gla_g_norm, m_gla_w_out, m_w_ff1, m_w_ff2, m_norm_final, v_w_ada, v_b_ada, v_norm_mix, v_norm_mlp, v_s5_a_re, v_s5_a_im, v_s5_log_dt, v_s5_b_re, v_s5_b_im, v_s5_c_re, v_s5_c_im, v_s5_d, v_s5_w_glu, v_gla_w_in, v_gla_w_gate2, v_gla_b_gate, v_gla_g_norm, v_gla_w_out, v_w_ff1, v_w_ff2, v_norm_final):
    args = dict(locals())
    L, D = x.shape[1], x.shape[2]
    QK = D // 2
    xi, yi, ci = _coords()
    chip = 2 * xi + yi
    dev = 2 * chip + ci

    cat = jnp.concatenate([gla_w_gate2[0].reshape(1, -1), gla_b_gate, gla_g_norm], axis=1)
    first = _exchange(jnp.concatenate([c.reshape(8, D // 8), jnp.tile(cat, (8, 1))], axis=1), MASK_ALL, name="gather_c")
    c_all = first[:, :, :D // 8].reshape(8, D)
    cat_all = first[0::2, 0, D // 8:]
    acols = w_ada.shape[2]
    b_cols = lax.dynamic_slice_in_dim(b_ada, chip * acols, acols, axis=1)[:, None, :]
    mod_cols = _mod_cols(c_all, w_ada, b_cols, name="ada_mod")
    mod_all = _exchange(mod_cols.reshape(16, acols), MASK_CHIPS, name="gather_mod")
    mod_all = mod_all.reshape(NCH, 2, 8, acols).transpose(1, 2, 0, 3).reshape(2, 8, NCH * acols)
    mod = lax.dynamic_index_in_dim(mod_all, dev, axis=1, keepdims=False).reshape(2, 6, 1, D)

    big = [("s5_w_glu", s5_w_glu, 0, "col"), ("gla_w_in", gla_w_in, 0, "slot"), ("gla_w_out", gla_w_out, 0, "row"),
           ("w_ff1_0", w_ff1, 0, "col"), ("w_ff1_1", w_ff1, 1, "col"), ("w_ff2_0", w_ff2, 0, "row"), ("w_ff2_1", w_ff2, 1, "row")]
    own16 = [_cast_into(t, lead, kind, chip, name=f"cast_{nm}") for nm, t, lead, kind in big]
    wshapes, wkinds = [b[1].shape[-2:] for b in big], [b[3] for b in big]
    wgroups = [[0, 3, 5], [1, 2, 4, 6]]
    wsems, wthru, wtoken = _gw_start(own16, wshapes, wkinds, wgroups, name="gather_w_start")
    W = {}

    def finish_weights(g, after):
        idx = wgroups[g]
        shp, knd = [wshapes[p] for p in idx], [wkinds[p] for p in idx]
        got = _gw_wait([wthru[p] for p in idx], shp, knd, wsems[g], after, name=f"gather_w_wait{g}")
        for p, w in zip(idx, _gw_forward(got, shp, knd, name=f"gather_w_cores{g}")):
            W[big[p][0]] = w

    qk4 = QK // NCH
    wg2 = cat_all[:, :GATE_RANK * qk4].reshape(NCH, GATE_RANK, qk4).transpose(1, 0, 2).reshape(GATE_RANK, QK)
    bg = cat_all[:, GATE_RANK * qk4:(GATE_RANK + 1) * qk4].reshape(1, QK)
    gn = cat_all[:, (GATE_RANK + 1) * qk4:].reshape(1, D)
    wg2p = jnp.concatenate([wg2, jnp.zeros((128 - GATE_RANK, QK), F32)], axis=0).astype(BF16)

    lb_re, lb_im, bb_re, bb_im = _s5_disc(s5_a_re[0], s5_a_im[0], s5_log_dt[0], s5_b_re[0], s5_b_im[0])
    tb = _s5_tables(lb_re, lb_im, bb_re, bb_im, s5_c_re[0], s5_c_im[0], min(S5_R, L) // NSEG)
    s5_dv = s5_d + wtoken[0, 0]

    def vec(t):
        return t.reshape(1, -1)

    m0, m1 = mod[0], mod[1]
    xp = _to_segments(x[0])
    (u0,) = _rows(lambda t, g, sc, sh: (_norm_mod(t, g, sc, sh),), [xp], [vec(norm_mix[0]), m0[1], m0[0]],
                  [(D, F32)], [], name="pre_mix0")
    y0, z0, ck0 = _s5_fwd2(u0, tb, s5_dv, name="s5_fwd")
    finish_weights(0, z0)
    vg0 = _matmul(z0, W["s5_w_glu"], name="glu")

    def res_glu_pre(xt, vgt, gt, g, sc, sh):
        xn = xt + gt * (vgt[:, :D] * _sigmoid(vgt[:, D:]))
        return xn, _norm_mod(xn, g, sc, sh)

    x2_0, h2_0 = _rows(res_glu_pre, [xp, vg0], [m0[2], vec(norm_mlp[0]), m0[4], m0[3]], [(D, F32), (D, BF16)], [],
                       name="res_mix0")
    a_0, f0 = _mlp_fwd(h2_0, W["w_ff1_0"], W["w_ff2_0"], "0")

    def res_pre(xt, bt, gt, g, sc, sh):
        xn = xt + gt * bt
        return xn, _norm_mod(xn, g, sc, sh)

    x3p, h1p = _rows(res_pre, [x2_0, f0], [m0[5], vec(norm_mix[1]), m1[1], m1[0]], [(D, F32), (D, BF16)], [],
                     name="res_mlp0")
    x3, h1 = _from_segments(x3p), _from_segments(h1p)
    finish_weights(1, f0)
    w_in = W["gla_w_in"].transpose(1, 0, 2).reshape(D, -1)
    w_in_r = jnp.concatenate([w_in[:, :4 * QK], w_in[:, 4 * QK + GATE_RANK:], w_in[:, 4 * QK:4 * QK + GATE_RANK],
                              jnp.zeros((D, 128 - GATE_RANK), BF16)], axis=1)
    proj = _matmul(h1, w_in_r, name="gla_in", tn=640)
    og, states = _gla_fwd2(proj, wg2p, bg, gn, name="gla_fwd")
    ymix = _matmul(og, W["gla_w_out"], name="gla_out")
    x2_1, h2_1 = _rows(res_pre, [x3, ymix], [m1[2], vec(norm_mlp[1]), m1[4], m1[3]], [(D, F32), (D, BF16)], [],
                       name="res_mix1")
    a_1, f1 = _mlp_fwd(h2_1, W["w_ff1_1"], W["w_ff2_1"], "1")

    def final(xt, ft, tgt, gt, g):
        xn = xt + gt * ft
        rs = lax.rsqrt(jnp.mean(xn * xn, axis=-1, keepdims=True) + EPS)
        xh = xn * rs
        e = xh * g - tgt
        dout = e * (1.0 / D)
        dxh = dout * g
        dx = rs * (dxh - xh * jnp.mean(dxh * xh, axis=-1, keepdims=True))
        lsum = 0.5 * jnp.sum(jnp.sum(e * e, axis=-1, keepdims=True), axis=0, keepdims=True) * (1.0 / D)
        return dx, dx * gt, jnp.broadcast_to(lsum, (1, 128)), _rsum(dout * xh), _rsum(dx * ft)

    dx, df1, loss_part, d_norm_final, dgt2_1 = _rows(
        final, [x2_1, f1, loss_target[0]], [m1[5], vec(norm_final)], [(D, F32), (D, BF16)],
        [(1, 128), (1, D), (1, D)], name="loss_head")
    loss = lax.psum(loss_part[0, 0], ("x", "y", "c"))

    def gate_bwd(dxt, bt, gt):
        return dxt * gt, _rsum(dxt * bt)

    def norm_bwd(xt, dht, drt, g, sc):
        dxn, dsh, dsc, dg = _norm_mod_bwd(xt, dht, g, sc)
        return drt + dxn, dsh, dsc, dg

    def norm_gate_bwd(xt, dht, drt, bt, g, sc, gt):
        dxn, dsh, dsc, dg = _norm_mod_bwd(xt, dht, g, sc)
        dxt = drt + dxn
        return dxt, dxt * gt, dsh, dsc, dg, _rsum(dxt * bt)

    vD = [(1, D)]
    dh2_1, dw_ff1_1, dw_ff2_1 = _mlp_bwd(df1, h2_1, a_1, W["w_ff1_1"], W["w_ff2_1"], "1")
    dx, dmix1, dsh2_1, dsc2_1, dg_mlp1, dgt1_1 = _rows(
        norm_gate_bwd, [x2_1, dh2_1, dx, ymix], [vec(norm_mlp[1]), m1[4], m1[2]], [(D, F32), (D, BF16)], vD * 4,
        name="norm_mlp1_bwd")
    dog = _matmul(dmix1, W["gla_w_out"], tb=True, name="gla_out_dx")
    dw_out = _matmul(og, dmix1, ta=True, name="gla_out_dw")
    dproj, dwg2p, dbg, dgn = _gla_bwd2(proj, dog, states, wg2p, bg, gn, name="gla_bwd")
    dh1 = _matmul(dproj, w_in_r, tb=True, name="gla_in_dx", tk=640)
    dw_in_r = _matmul(h1, dproj, ta=True, name="gla_in_dw", tn=640)
    dx, dsh1_1, dsc1_1, dg_mix1 = _rows(norm_bwd, [x3, dh1, dx], [vec(norm_mix[1]), m1[1]], [(D, F32)], vD * 3,
                                        name="norm_mix1_bwd")
    dxp = _to_segments(dx)
    tags = [b[0] for b in big] + ["small"]
    rs_groups = []

    def rs_chips_begin(idx, srcs, r1, gname):
        s1 = [_sum_own_half(g, r, ci, F32 if tags[k] == "small" else BF16, name=f"rs_sum_cores_{tags[k]}")
              for g, r, k in zip(srcs, r1, idx)]
        pair, parts, lands, token = _rs_chips_start(s1, name=f"rs_chips_start_{gname}")
        rs_groups.append((idx, pair, parts, lands))
        return token

    def rs_begin(idx, srcs, gname):
        return rs_chips_begin(idx, srcs, _rs_cores(srcs, name=f"rs_cores_{gname}"), gname)

    dw_in = jnp.concatenate([dw_in_r[:, :4 * QK], dw_in_r[:, 6 * QK:6 * QK + GATE_RANK], dw_in_r[:, 4 * QK:6 * QK]], axis=1)
    dw_in = dw_in.reshape(D, NCH, -1).transpose(1, 0, 2)
    idx1 = [1, 2, 4, 6]
    pair1, src1, land1, tok1 = _rs_cores_start(
        [dw_in, dw_out.reshape(NCH, -1, D), dw_ff1_1, dw_ff2_1.reshape(NCH, -1, D)], name="rs_cores_start_l1")

    df0, dgt2_0 = _rows(gate_bwd, [dxp, f0], [m0[5] + tok1[0, 0]], [(D, BF16)], vD, name="gate_mlp0")
    dh2_0, dw_ff1_0, dw_ff2_0 = _mlp_bwd(df0, h2_0, a_0, W["w_ff1_0"], W["w_ff2_0"], "0")
    src1, land1 = _rs_cores_wait(pair1, src1, land1, dh2_0, name="rs_cores_wait_l1")
    tok1b = rs_chips_begin(idx1, src1, land1, "l1")
    idx0 = [3, 5]
    pair0, src0, land0, tok0 = _rs_cores_start([dw_ff1_0, dw_ff2_0.reshape(NCH, -1, D)], name="rs_cores_start_l0")
    tok2 = tok1b + tok0

    def norm_glu_bwd(xt, dht, drt, vgt, g, sc, gt):
        dxn, dsh, dsc, dg = _norm_mod_bwd(xt, dht, g, sc)
        dxt = drt + dxn
        val, sg = vgt[:, :D], _sigmoid(vgt[:, D:])
        dbr = dxt * gt
        dvg = jnp.concatenate([dbr * sg, dbr * val * sg * (1.0 - sg)], axis=1)
        return dxt, dvg, dsh, dsc, dg, _rsum(dxt * val * sg)

    dxp, dvg0, dsh2_0, dsc2_0, dg_mlp0, dgt1_0 = _rows(
        norm_glu_bwd, [x2_0, dh2_0, dxp, vg0], [vec(norm_mlp[0]), m0[4] + tok2[0, 0], m0[2]], [(D, F32), (2 * D, BF16)],
        vD * 4, name="norm_mlp0_bwd")
    dz0 = _matmul(dvg0, W["s5_w_glu"], tb=True, name="glu_dx")
    dw_glu = _matmul(z0, dvg0, ta=True, name="glu_dw", tn=512, col_shards=NCH)
    src0, land0 = _rs_cores_wait(pair0, src0, land0, dw_glu, name="rs_cores_wait_l0")
    tok0b = rs_chips_begin(idx0 + [0], src0 + [dw_glu], land0 + list(_rs_cores([dw_glu], name="rs_cores_glu")), "l0")
    du0, db_acc, dc_acc, dl_acc, dd_s5 = _s5_bwd2(u0, y0, dz0, ck0, tb, s5_dv + tok0b[0, 0], name="s5_bwd")
    dxp, dsh1_0, dsc1_0, dg_mix0 = _rows(norm_bwd, [xp, du0, dxp], [vec(norm_mix[0]), m0[1]], [(D, F32)], vD * 3,
                                         name="norm_mix0_bwd")
    grad_x = _from_segments(dxp)[None]

    dmod = jnp.concatenate([dsh1_0, dsc1_0, dgt1_0, dsh2_0, dsc2_0, dgt2_0,
                            dsh1_1, dsc1_1, dgt1_1, dsh2_1, dsc2_1, dgt2_1], axis=1)
    dbb_re, dbb_im = _s5_untable(db_acc)
    dc_re, dc_im_neg = _s5_untable(dc_acc)
    nbk = D // 128
    dl = dl_acc.reshape(nbk, NSEG, 2, GPB * S5_P).sum(axis=1)
    smalls = [dmod, dg_mix0, dg_mix1, dg_mlp0, dg_mlp1, d_norm_final, dd_s5, dbg, dgn,
              dwg2p[:GATE_RANK].reshape(1, -1), dbb_re.reshape(1, -1), dbb_im.reshape(1, -1),
              dc_re.reshape(1, -1), dc_im_neg.reshape(1, -1), dl.reshape(1, -1)]
    ssz = [t.shape[1] for t in smalls]
    stot = sum(ssz)
    spad = -(-stot // 8192) * 8192
    svec = jnp.concatenate(smalls + [jnp.zeros((1, spad - stot), F32)], axis=1).reshape(NCH, spad // (128 * NCH), 128)

    dmod_all = _exchange(dmod.reshape(12 * D // 128, 128), MASK_ALL, name="gather_dmod").reshape(8, 2, 6 * D)
    dmod_cols = lax.dynamic_slice_in_dim(dmod_all, chip * acols, acols, axis=2).transpose(1, 0, 2)
    g_w_ada = _ada_grad(c_all, dmod_cols, name="ada_grad")

    rs_begin([7], [svec], "last")
    landed = _rs_chips_wait([(g[1], g[2], g[3]) for g in rs_groups], grad_x, name="rs_chips_wait")
    s1, r2 = {}, {}
    for (idx, _, _, _), (parts, lands) in zip(rs_groups, landed):
        for k, part, land in zip(idx, parts, lands):
            s1[k], r2[k] = part, land

    def fin(k, **kw):
        return _sum_chips(r2[k], s1[k], chip, ci, name=f"rs_sum_chips_{tags[k]}", **kw)

    f_ff1 = fin(4, nlead=2, lead=1, prev=fin(3, nlead=2, lead=0))
    f_ff2 = fin(6, nlead=2, lead=1, prev=fin(5, nlead=2, lead=0))
    finals = [fin(0), fin(1), fin(2), f_ff1, f_ff2, fin(7, spread=True)]
    halves = [t.shape[1] for t in (s1[0], s1[1], s1[2], s1[3], s1[5], s1[7])]
    g_glu, g_in, g_out, g_w_ff1, g_w_ff2, s_own = _rs_gather(finals, halves, [False] * 5 + [True], name="rs_gather_cores")
    srows = spad // (128 * NCH)
    (s_sum,) = _gather_weights([s_own.reshape(NCH * srows, 128)], [(srows, 128)], ["row"], name="gather_small_grads")
    s_sum = s_sum.reshape(-1)
    so = [sum(ssz[:k]) for k in range(len(ssz))]
    sm = [s_sum[o:o + n] for o, n in zip(so, ssz)]
    (dmod_s, g_mix0, g_mix1, g_mlp0, g_mlp1, g_nf, g_d, g_bg, g_gn, g_wg2, g_bbre, g_bbim, g_cre, g_cimn, g_dl) = sm
    g_b_ada = dmod_s.reshape(2, 6 * D)

    G = D // S5_H
    _, disc_vjp = jax.vjp(_s5_disc, s5_a_re[0], s5_a_im[0], s5_log_dt[0], s5_b_re[0], s5_b_im[0])
    g_dl = g_dl.reshape(nbk, 2, GPB, S5_P)
    ct = (g_dl[:, 0].reshape(G, S5_P), g_dl[:, 1].reshape(G, S5_P),
          g_bbre.reshape(G, S5_H, S5_P).transpose(0, 2, 1), g_bbim.reshape(G, S5_H, S5_P).transpose(0, 2, 1))
    g_a_re, g_a_im, g_log_dt, g_b_re, g_b_im = disc_vjp(ct)
    g_c_re = g_cre.reshape(G, S5_H, S5_P)
    g_c_im = -g_cimn.reshape(G, S5_H, S5_P)
    g_wg2_s = lax.dynamic_slice_in_dim(g_wg2.reshape(GATE_RANK, QK), chip * qk4, qk4, axis=1)
    g_bg_s = lax.dynamic_slice_in_dim(g_bg.reshape(1, QK), chip * qk4, qk4, axis=1)
    g_gn_s = lax.dynamic_slice_in_dim(g_gn.reshape(1, D), chip * (D // NCH), D // NCH, axis=1)

    grads = dict(
        w_ada=g_w_ada, b_ada=g_b_ada, norm_mix=jnp.stack([g_mix0, g_mix1]), norm_mlp=jnp.stack([g_mlp0, g_mlp1]),
        s5_a_re=g_a_re[None], s5_a_im=g_a_im[None], s5_log_dt=g_log_dt[None], s5_b_re=g_b_re[None], s5_b_im=g_b_im[None],
        s5_c_re=g_c_re[None], s5_c_im=g_c_im[None], s5_d=g_d[None], s5_w_glu=g_glu,
        gla_w_in=g_in, gla_w_gate2=g_wg2_s[None], gla_b_gate=g_bg_s, gla_g_norm=g_gn_s,
        gla_w_out=g_out, w_ff1=g_w_ff1, w_ff2=g_w_ff2, norm_final=g_nf)

    names = list(grads)
    large = ("w_ada", "s5_w_glu", "gla_w_in", "gla_w_out", "w_ff1", "w_ff2")
    delta, new_m, new_v = {}, {}, {}
    for nm in large:
        delta[nm], new_m[nm], new_v[nm] = _adamw(args[nm], grads[nm], args["m_" + nm], args["v_" + nm], name=f"adamw_{nm}")
    grads = {nm: grads[nm].reshape(args[nm].shape) for nm in names}
    for nm in names:
        if nm not in large:
            shp = args[nm].shape
            as2d = (1, -1) if len(shp) == 1 else shp
            outs = _adamw_whole(*[t.reshape(as2d) for t in (args[nm], grads[nm], args["m_" + nm], args["v_" + nm])],
                                name=f"adamw_{nm}")
            delta[nm], new_m[nm], new_v[nm] = (t.reshape(shp) for t in outs)
    return (loss, grad_x, *[grads[n] for n in names], *[delta[n] for n in names], *[new_m[n] for n in names],
            *[new_v[n] for n in names])
```

```python
import math

import jax
import jax.numpy as jnp
from jax import lax
from jax.experimental import pallas as pl
from jax.experimental.pallas import tpu as pltpu

F32 = jnp.float32
BF16 = jnp.bfloat16
MESH = pl.DeviceIdType.MESH

EPS = 1e-6
CHUNK = 64
GLA_NB = 4
S5_H = 16
S5_P = 64
GPB = 8
NSEG = 8
HEADS = 4
GATE_RANK = 16
GATE_TAU = 16.0
NCH = 4
LR, B1, B2, AEPS, WD, ASTEP = 0.001, 0.9, 0.999, 1e-08, 0.01, 10
VMEM_LIMIT = 56 << 20

MASK_CHIPS = ((1, 0, 0), (0, 1, 0), (1, 1, 0))
MASK_ALL = ((0, 0, 1), (0, 1, 0), (0, 1, 1), (1, 0, 0), (1, 0, 1), (1, 1, 0), (1, 1, 1))


def _params(*sem):
    return pltpu.CompilerParams(dimension_semantics=sem or None, vmem_limit_bytes=VMEM_LIMIT)


def _tile_rows(rows, cap=512):
    best = 8
    for t in range(8, cap + 1, 8):
        if rows % t == 0:
            best = t
    return best


def _whole(shape):
    return pl.BlockSpec(shape, lambda i, _n=len(shape): (0,) * _n)


def _matmul(a, b, *, name, ta=False, tb=False, tm=1024, tn=1024, tk=2048, out_dtypes=(F32,),
            a_fn=None, epi=None, epi_ins=(), col_shards=1):
    M, K = (a.shape[1], a.shape[0]) if ta else a.shape
    N = b.shape[0] if tb else b.shape[1]
    tm, tn, tk = min(tm, M), min(tn, N), min(tk, K)
    assert M % tm == 0 and N % tn == 0 and K % tk == 0, (name, M, N, K)
    nk = K // tk
    ne = len(epi_ins)
    dn = (((0 if ta else 1,), (1 if tb else 0,)), ((), ()))

    def body(a_ref, b_ref, *rest):
        e_refs, o_refs, acc = rest[:ne], rest[ne:-1], rest[-1]
        k = pl.program_id(2)

        @pl.when(k == 0)
        def _():
            acc[...] = jnp.zeros_like(acc)

        at = a_ref[...]
        if a_fn is not None:
            at = a_fn(at)
        acc[...] += lax.dot_general(at.astype(BF16), b_ref[...].astype(BF16), dn, preferred_element_type=F32)

        @pl.when(k == nk - 1)
        def _():
            outs = (acc[...],) if epi is None else epi(acc[...], *[r[...] for r in e_refs])
            for r, o in zip(o_refs, outs):
                r[...] = o.astype(r.dtype)

    a_spec = pl.BlockSpec((tk, tm), lambda i, j, k: (k, i)) if ta else pl.BlockSpec((tm, tk), lambda i, j, k: (i, k))
    b_spec = pl.BlockSpec((tn, tk), lambda i, j, k: (j, k)) if tb else pl.BlockSpec((tk, tn), lambda i, j, k: (k, j))
    o_spec = pl.BlockSpec((tm, tn), lambda i, j, k: (i, j))
    if col_shards > 1:
        per = N // col_shards // tn
        assert ne == 0 and per * tn * col_shards == N
        w_spec = pl.BlockSpec((None, tm, tn), lambda i, j, k: (j // per, i, j % per))
        o_shape = (col_shards, M, N // col_shards)
    else:
        w_spec, o_shape = o_spec, (M, N)
    outs = pl.pallas_call(
        body, name=name, grid=(M // tm, N // tn, nk),
        in_specs=[a_spec, b_spec] + [o_spec] * ne,
        out_specs=[w_spec] * len(out_dtypes),
        out_shape=[jax.ShapeDtypeStruct(o_shape, d) for d in out_dtypes],
        scratch_shapes=[pltpu.VMEM((tm, tn), F32)],
        compiler_params=_params("parallel", "parallel", "arbitrary"),
    )(a, b, *epi_ins)
    return outs[0] if len(outs) == 1 else outs


def _rows(fn, rows_in, vecs_in, rows_out, acc_out, *, name, tm=512):
    L = rows_in[0].shape[0]
    tm = min(tm, L)
    assert L % tm == 0
    nr, nv, no, na = len(rows_in), len(vecs_in), len(rows_out), len(acc_out)

    def body(*refs):
        rin, vin = refs[:nr], refs[nr:nr + nv]
        rout, aout = refs[nr + nv:nr + nv + no], refs[nr + nv + no:]
        outs = fn(*[r[...] for r in rin], *[v[...] for v in vin])
        for r, o in zip(rout, outs[:no]):
            r[...] = o.astype(r.dtype)
        if na:
            @pl.when(pl.program_id(0) == 0)
            def _():
                for r in aout:
                    r[...] = jnp.zeros_like(r)

            for r, o in zip(aout, outs[no:]):
                r[...] += o

    outs = pl.pallas_call(
        body, name=name, grid=(L // tm,),
        in_specs=[pl.BlockSpec((tm, r.shape[1]), lambda i: (i, 0)) for r in rows_in] + [_whole(v.shape) for v in vecs_in],
        out_specs=[pl.BlockSpec((tm, c), lambda i: (i, 0)) for c, _ in rows_out] + [_whole(s) for s in acc_out],
        out_shape=[jax.ShapeDtypeStruct((L, c), d) for c, d in rows_out] + [jax.ShapeDtypeStruct(s, F32) for s in acc_out],
        compiler_params=_params("arbitrary"),
    )(*rows_in, *vecs_in)
    return outs


def _rsum(t):
    return jnp.sum(t, axis=0, keepdims=True)


def _norm_mod(x, g, sc, sh):
    rs = lax.rsqrt(jnp.mean(x * x, axis=-1, keepdims=True) + EPS)
    return x * rs * g * (1.0 + sc) + sh


def _norm_mod_bwd(x, dh, g, sc):
    rs = lax.rsqrt(jnp.mean(x * x, axis=-1, keepdims=True) + EPS)
    xh = x * rs
    dn = dh * (1.0 + sc)
    dxh = dn * g
    dx = rs * (dxh - xh * jnp.mean(dxh * xh, axis=-1, keepdims=True))
    return dx, _rsum(dh), _rsum(dh * xh * g), _rsum(dn * xh)


def _sigmoid(x):
    return jax.nn.sigmoid(x)


def _gelu(y):
    return jax.nn.gelu(y, approximate=True)


def _gelu_grad(y):
    c = math.sqrt(2.0 / math.pi)
    t = jnp.tanh(c * (y + 0.044715 * y * y * y))
    return 0.5 * (1.0 + t) + 0.5 * y * (1.0 - t * t) * c * (1.0 + 3.0 * 0.044715 * y * y)


def _s5_tables(lb_re, lb_im, bb_re, bb_im, c_re, c_im, seg_len):
    G = lb_re.shape[0]
    nb = G // GPB
    eye = jnp.eye(GPB, dtype=F32)

    def bdiag(t):
        a, b = t.shape[1:]
        t = t.reshape(nb, GPB, a, b)
        return (t[:, :, :, None, :] * eye[None, :, None, :, None]).reshape(nb, GPB * a, GPB * b)

    bbd = jnp.concatenate([bdiag(bb_re.transpose(0, 2, 1)), bdiag(bb_im.transpose(0, 2, 1))], axis=2)
    cbd = jnp.concatenate([bdiag(c_re.transpose(0, 2, 1)), -bdiag(c_im.transpose(0, 2, 1))], axis=1)

    def lanes(re, im):
        t = jnp.concatenate([re.reshape(nb, GPB * S5_P), im.reshape(nb, GPB * S5_P)], axis=1)
        return jnp.repeat(t, NSEG, axis=0)

    tr, ti = lb_re, lb_im
    for _ in range(int(math.log2(seg_len))):
        tr, ti = tr * tr - ti * ti, 2.0 * tr * ti
    return dict(bbd=bbd.astype(BF16), bbdT=bbd.transpose(0, 2, 1).astype(BF16), cbd=cbd.astype(BF16),
                cbdT=cbd.transpose(0, 2, 1).astype(BF16), lam=lanes(lb_re, lb_im), lamT=lanes(tr, ti))


def _s5_untable(acc):
    nb = acc.shape[0]
    t = acc.reshape(nb, GPB, S5_H, 2, GPB, S5_P)
    d = jnp.diagonal(t, axis1=1, axis2=4)
    d = d.transpose(0, 4, 2, 1, 3).reshape(nb * GPB, 2, S5_H, S5_P)
    return d[:, 0], d[:, 1]


def _s5_fix(ends, lamT, *, reverse, name):
    nrow = ends.shape[0]
    half = ends.shape[1] // 2

    def body(e_ref, t_ref, o_ref):
        for gb in range(nrow // NSEG):
            r0 = gb * NSEG
            tr, ti = t_ref[r0:r0 + 1, :half], t_ref[r0:r0 + 1, half:]
            cr = jnp.zeros((1, half), F32)
            ci = jnp.zeros((1, half), F32)
            order = range(NSEG - 1, -1, -1) if reverse else range(NSEG)
            for n, s in enumerate(order):
                if n > 0:
                    p = s + 1 if reverse else s - 1
                    er, ei = e_ref[r0 + p:r0 + p + 1, :half], e_ref[r0 + p:r0 + p + 1, half:]
                    if reverse:
                        cr, ci = tr * cr + ti * ci + er, tr * ci - ti * cr + ei
                    else:
                        cr, ci = tr * cr - ti * ci + er, tr * ci + ti * cr + ei
                o_ref[r0 + s:r0 + s + 1, :half] = cr
                o_ref[r0 + s:r0 + s + 1, half:] = ci

    return pl.pallas_call(body, name=name, out_shape=jax.ShapeDtypeStruct(ends.shape, F32),
                          compiler_params=_params())(ends, lamT)


def _s5_fwd(up, carry_in, tb, dvec, *, emit, name, R=256):
    L, D = up.shape
    R = min(R, L)
    nb, ta, ngb = L // R, R // NSEG, D // 128
    SW = GPB * S5_P
    crows = ngb * NSEG

    def body(u_ref, cin_ref, lam_ref, b_ref, c_ref, d_ref, *rest):
        if emit:
            y_ref, z_ref, ck_ref, carry, xbuf = rest
        else:
            cout_ref, carry, xbuf = rest
        i = pl.program_id(0)

        @pl.when(i == 0)
        def _():
            carry[...] = cin_ref[...]

        if emit:
            ck_ref[0] = carry[...]
        for gb in range(ngb):
            cols = slice(gb * 128, (gb + 1) * 128)
            rws = slice(gb * NSEG, (gb + 1) * NSEG)
            ug = u_ref[:, cols]
            xbuf[...] = jnp.dot(ug.astype(BF16), b_ref[gb], preferred_element_type=F32)
            lr, li = lam_ref[rws, :SW], lam_ref[rws, SW:]

            def step(a, c, lr=lr, li=li):
                cr, ci = c
                o = pl.multiple_of(a * NSEG, NSEG)
                nr = lr * cr - li * ci + xbuf[pl.ds(o, NSEG), :SW]
                ni = lr * ci + li * cr + xbuf[pl.ds(o, NSEG), SW:]
                if emit:
                    xbuf[pl.ds(o, NSEG), :SW] = nr
                    xbuf[pl.ds(o, NSEG), SW:] = ni
                return nr, ni

            cr, ci = lax.fori_loop(0, ta, step, (carry[rws, :SW], carry[rws, SW:]), unroll=2)
            carry[rws, :SW] = cr
            carry[rws, SW:] = ci
            if emit:
                y = jnp.dot(xbuf[...].astype(BF16), c_ref[gb], preferred_element_type=F32) + d_ref[:, cols] * ug
                y_ref[:, cols] = y
                z_ref[:, cols] = _gelu(y).astype(BF16)
        if not emit:
            cout_ref[...] = carry[...]

    rowblk = pl.BlockSpec((R, D), lambda i: (i, 0))
    if emit:
        out_shape = [jax.ShapeDtypeStruct((L, D), F32), jax.ShapeDtypeStruct((L, D), BF16),
                     jax.ShapeDtypeStruct((nb, crows, 2 * SW), F32)]
        out_specs = [rowblk, rowblk, pl.BlockSpec((1, crows, 2 * SW), lambda i: (i, 0, 0))]
    else:
        out_shape = [jax.ShapeDtypeStruct((crows, 2 * SW), F32)]
        out_specs = [_whole((crows, 2 * SW))]
    return pl.pallas_call(
        body, name=name, grid=(nb,),
        in_specs=[rowblk, _whole(carry_in.shape), _whole(tb["lam"].shape), _whole(tb["bbd"].shape),
                  _whole(tb["cbd"].shape), _whole(dvec.shape)],
        out_specs=out_specs, out_shape=out_shape,
        scratch_shapes=[pltpu.VMEM((crows, 2 * SW), F32), pltpu.VMEM((R, 2 * SW), F32)],
        compiler_params=_params("arbitrary"),
    )(up, carry_in, tb["lam"], tb["bbd"], tb["cbd"], dvec)


def _s5_bwd(up, y, dz, ck, gcarry_in, tb, dvec, *, emit, name, R=256):
    L, D = up.shape
    R = min(R, L)
    nb, ta, ngb = L // R, R // NSEG, D // 128
    SW = GPB * S5_P
    crows = ngb * NSEG

    def body(u_ref, y_ref, dz_ref, ck_ref, gin_ref, lam_ref, b_ref, bt_ref, ct_ref, d_ref, *rest):
        if emit:
            du_ref, db_ref, dc_ref, dl_ref, dd_ref, gcarry, xbuf, gbuf, dybuf = rest
        else:
            gout_ref, gcarry, gbuf, dybuf = rest
        i = pl.program_id(0)

        @pl.when(i == 0)
        def _():
            gcarry[...] = gin_ref[...]
            if emit:
                db_ref[...] = jnp.zeros_like(db_ref)
                dc_ref[...] = jnp.zeros_like(dc_ref)
                dl_ref[...] = jnp.zeros_like(dl_ref)
                dd_ref[...] = jnp.zeros_like(dd_ref)

        dybuf[...] = dz_ref[...] * _gelu_grad(y_ref[...])
        for gb in range(ngb):
            cols = slice(gb * 128, (gb + 1) * 128)
            rws = slice(gb * NSEG, (gb + 1) * NSEG)
            dyg = dybuf[:, cols]
            lr, li = lam_ref[rws, :SW], lam_ref[rws, SW:]
            gbuf[...] = jnp.dot(dyg.astype(BF16), ct_ref[gb], preferred_element_type=F32)
            if emit:
                ug = u_ref[:, cols]
                xbuf[0:NSEG, :] = ck_ref[0, rws, :]
                xbuf[NSEG:, :] = jnp.dot(ug.astype(BF16), b_ref[gb], preferred_element_type=F32)

                def fstep(a, c, lr=lr, li=li):
                    cr, ci = c
                    o = pl.multiple_of(a * NSEG + NSEG, NSEG)
                    nr = lr * cr - li * ci + xbuf[pl.ds(o, NSEG), :SW]
                    ni = lr * ci + li * cr + xbuf[pl.ds(o, NSEG), SW:]
                    xbuf[pl.ds(o, NSEG), :SW] = nr
                    xbuf[pl.ds(o, NSEG), SW:] = ni
                    return nr, ni

                lax.fori_loop(0, ta, fstep, (xbuf[0:NSEG, :SW], xbuf[0:NSEG, SW:]), unroll=2)

            def rstep(k, c, lr=lr, li=li):
                o = pl.multiple_of((ta - 1 - k) * NSEG, NSEG)
                gr_n, gi_n = c[0], c[1]
                gr = gbuf[pl.ds(o, NSEG), :SW] + lr * gr_n + li * gi_n
                gi = gbuf[pl.ds(o, NSEG), SW:] - li * gr_n + lr * gi_n
                if not emit:
                    return gr, gi
                gbuf[pl.ds(o, NSEG), :SW] = gr
                gbuf[pl.ds(o, NSEG), SW:] = gi
                xr, xi = xbuf[pl.ds(o, NSEG), :SW], xbuf[pl.ds(o, NSEG), SW:]
                return gr, gi, c[2] + gr * xr + gi * xi, c[3] + gi * xr - gr * xi

            c0 = (gcarry[rws, :SW], gcarry[rws, SW:])
            if emit:
                c0 = c0 + (jnp.zeros((NSEG, SW), F32), jnp.zeros((NSEG, SW), F32))
            cf = lax.fori_loop(0, ta, rstep, c0, unroll=2)
            gcarry[rws, :SW] = cf[0]
            gcarry[rws, SW:] = cf[1]
            if emit:
                dl_ref[rws, :SW] += cf[2]
                dl_ref[rws, SW:] += cf[3]
                gb16 = gbuf[...].astype(BF16)
                du_ref[:, cols] = jnp.dot(gb16, bt_ref[gb], preferred_element_type=F32) + d_ref[:, cols] * dyg
                tn = (((0,), (0,)), ((), ()))
                db_ref[gb] += lax.dot_general(ug.astype(BF16), gb16, tn, preferred_element_type=F32)
                dc_ref[gb] += lax.dot_general(dyg.astype(BF16), xbuf[NSEG:, :].astype(BF16), tn,
                                              preferred_element_type=F32)
                dd_ref[:, cols] += _rsum(dyg * ug)
        if not emit:
            gout_ref[...] = gcarry[...]

    rev = pl.BlockSpec((R, D), lambda i: (nb - 1 - i, 0))
    acc3 = (ngb, 128, 2 * SW)
    if emit:
        out_shape = [jax.ShapeDtypeStruct((L, D), F32), jax.ShapeDtypeStruct(acc3, F32), jax.ShapeDtypeStruct(acc3, F32),
                     jax.ShapeDtypeStruct((crows, 2 * SW), F32), jax.ShapeDtypeStruct((1, D), F32)]
        out_specs = [rev, _whole(acc3), _whole(acc3), _whole((crows, 2 * SW)), _whole((1, D))]
        scratch = [pltpu.VMEM((crows, 2 * SW), F32), pltpu.VMEM((R + NSEG, 2 * SW), F32),
                   pltpu.VMEM((R, 2 * SW), F32), pltpu.VMEM((R, D), F32)]
    else:
        out_shape = [jax.ShapeDtypeStruct((crows, 2 * SW), F32)]
        out_specs = [_whole((crows, 2 * SW))]
        scratch = [pltpu.VMEM((crows, 2 * SW), F32), pltpu.VMEM((R, 2 * SW), F32), pltpu.VMEM((R, D), F32)]
    return pl.pallas_call(
        body, name=name, grid=(nb,),
        in_specs=[rev, rev, rev, pl.BlockSpec((1, crows, 2 * SW), lambda i: (nb - 1 - i, 0, 0)),
                  _whole(gcarry_in.shape), _whole(tb["lam"].shape), _whole(tb["bbd"].shape),
                  _whole(tb["bbdT"].shape), _whole(tb["cbdT"].shape), _whole(dvec.shape)],
        out_specs=out_specs, out_shape=out_shape, scratch_shapes=scratch,
        compiler_params=_params("arbitrary"),
    )(up, y, dz, ck, gcarry_in, tb["lam"], tb["bbd"], tb["bbdT"], tb["cbdT"], dvec)


S5_R = 256


def _s5_carries(ends, first, t_ref, rws, SW, cfx, *, reverse):
    er, ei = ends
    tr, ti = t_ref[rws, :SW][0:1], t_ref[rws, SW:][0:1]
    cr, ci = first
    order = range(NSEG - 1, -1, -1) if reverse else range(NSEG)
    for n, s in enumerate(order):
        if n > 0:
            p = s + 1 if reverse else s - 1
            if reverse:
                cr, ci = tr * cr + ti * ci + er[p:p + 1], tr * ci - ti * cr + ei[p:p + 1]
            else:
                cr, ci = tr * cr - ti * ci + er[p:p + 1], tr * ci + ti * cr + ei[p:p + 1]
        cfx[s:s + 1, :SW] = cr
        cfx[s:s + 1, SW:] = ci


def _s5_fwd2(up, tb, dvec, *, name):
    L, D = up.shape
    R = min(S5_R, L)
    nb, ta, ngb = L // R, R // NSEG, D // 128
    SW = GPB * S5_P
    crows = ngb * NSEG

    def body(u_ref, lam_ref, t_ref, b_ref, c_ref, d_ref, y_ref, z_ref, ck_ref, carry, xbuf2, cfx2):
        @pl.when(pl.program_id(0) == 0)
        def _():
            carry[...] = jnp.zeros_like(carry)

        zero = jnp.zeros((NSEG, SW), F32)
        for gb in range(ngb):
            xbuf, cfx = xbuf2.at[gb % 2], cfx2.at[gb % 2]
            cols = slice(gb * 128, (gb + 1) * 128)
            rws = slice(gb * NSEG, (gb + 1) * NSEG)
            ug = u_ref[:, cols]
            xbuf[...] = jnp.dot(ug.astype(BF16), b_ref[gb], preferred_element_type=F32)
            lr, li = lam_ref[rws, :SW], lam_ref[rws, SW:]

            def step(a, c, lr=lr, li=li, store=False):
                cr, ci = c
                o = pl.multiple_of(a * NSEG, NSEG)
                nr = lr * cr - li * ci + xbuf[pl.ds(o, NSEG), :SW]
                ni = lr * ci + li * cr + xbuf[pl.ds(o, NSEG), SW:]
                if store:
                    xbuf[pl.ds(o, NSEG), :SW] = nr
                    xbuf[pl.ds(o, NSEG), SW:] = ni
                return nr, ni

            ends = lax.fori_loop(0, ta, step, (zero, zero), unroll=True)
            prev = (carry[rws, :SW][NSEG - 1:NSEG], carry[rws, SW:][NSEG - 1:NSEG])
            _s5_carries(ends, prev, t_ref, rws, SW, cfx, reverse=False)
            ck_ref[0, rws, :] = cfx[...]
            cr, ci = lax.fori_loop(0, ta, lambda a, c, st=step: st(a, c, store=True), (cfx[:, :SW], cfx[:, SW:]),
                                   unroll=True)
            carry[rws, :SW] = cr
            carry[rws, SW:] = ci
            y = jnp.dot(xbuf[...].astype(BF16), c_ref[gb], preferred_element_type=F32) + d_ref[:, cols] * ug
            y_ref[:, cols] = y
            z_ref[:, cols] = _gelu(y).astype(BF16)

    rowblk = pl.BlockSpec((R, D), lambda i: (i, 0))
    return pl.pallas_call(
        body, name=name, grid=(nb,),
        in_specs=[rowblk, _whole(tb["lam"].shape), _whole(tb["lamT"].shape), _whole(tb["bbd"].shape),
                  _whole(tb["cbd"].shape), _whole(dvec.shape)],
        out_specs=[rowblk, rowblk, pl.BlockSpec((1, crows, 2 * SW), lambda i: (i, 0, 0))],
        out_shape=[jax.ShapeDtypeStruct((L, D), F32), jax.ShapeDtypeStruct((L, D), BF16),
                   jax.ShapeDtypeStruct((nb, crows, 2 * SW), F32)],
        scratch_shapes=[pltpu.VMEM((crows, 2 * SW), F32), pltpu.VMEM((2, R, 2 * SW), F32),
                        pltpu.VMEM((2, NSEG, 2 * SW), F32)],
        compiler_params=_params("arbitrary"),
    )(up, tb["lam"], tb["lamT"], tb["bbd"], tb["cbd"], dvec)


def _s5_bwd2(up, y, dz, ck, tb, dvec, *, name):
    L, D = up.shape
    R = min(S5_R, L)
    nb, ta, ngb = L // R, R // NSEG, D // 128
    SW = GPB * S5_P
    crows = ngb * NSEG

    def body(u_ref, y_ref, dz_ref, ck_ref, lam_ref, t_ref, b_ref, bt_ref, ct_ref, d_ref,
             du_ref, db_ref, dc_ref, dl_ref, dd_ref, gcarry, xbuf2, gbuf2, dybuf, cfx2):
        @pl.when(pl.program_id(0) == 0)
        def _():
            gcarry[...] = jnp.zeros_like(gcarry)
            db_ref[...] = jnp.zeros_like(db_ref)
            dc_ref[...] = jnp.zeros_like(dc_ref)
            dl_ref[...] = jnp.zeros_like(dl_ref)
            dd_ref[...] = jnp.zeros_like(dd_ref)

        zero = jnp.zeros((NSEG, SW), F32)
        dybuf[...] = dz_ref[...] * _gelu_grad(y_ref[...])
        for gb in range(ngb):
            xbuf, gbuf, cfx = xbuf2.at[gb % 2], gbuf2.at[gb % 2], cfx2.at[gb % 2]
            cols = slice(gb * 128, (gb + 1) * 128)
            rws = slice(gb * NSEG, (gb + 1) * NSEG)
            dyg = dybuf[:, cols]
            ug = u_ref[:, cols]
            lr, li = lam_ref[rws, :SW], lam_ref[rws, SW:]
            gbuf[...] = jnp.dot(dyg.astype(BF16), ct_ref[gb], preferred_element_type=F32)
            xbuf[0:NSEG, :] = ck_ref[0, rws, :]
            xbuf[NSEG:, :] = jnp.dot(ug.astype(BF16), b_ref[gb], preferred_element_type=F32)

            def fstep(a, c, lr=lr, li=li):
                cr, ci = c
                o = pl.multiple_of(a * NSEG + NSEG, NSEG)
                nr = lr * cr - li * ci + xbuf[pl.ds(o, NSEG), :SW]
                ni = lr * ci + li * cr + xbuf[pl.ds(o, NSEG), SW:]
                xbuf[pl.ds(o, NSEG), :SW] = nr
                xbuf[pl.ds(o, NSEG), SW:] = ni
                return nr, ni

            lax.fori_loop(0, ta, fstep, (xbuf[0:NSEG, :SW], xbuf[0:NSEG, SW:]), unroll=True)

            def rstep(k, c, lr=lr, li=li, store=False):
                o = pl.multiple_of((ta - 1 - k) * NSEG, NSEG)
                gr_n, gi_n = c[0], c[1]
                gr = gbuf[pl.ds(o, NSEG), :SW] + lr * gr_n + li * gi_n
                gi = gbuf[pl.ds(o, NSEG), SW:] - li * gr_n + lr * gi_n
                if not store:
                    return gr, gi
                gbuf[pl.ds(o, NSEG), :SW] = gr
                gbuf[pl.ds(o, NSEG), SW:] = gi
                xr, xi = xbuf[pl.ds(o, NSEG), :SW], xbuf[pl.ds(o, NSEG), SW:]
                return gr, gi, c[2] + gr * xr + gi * xi, c[3] + gi * xr - gr * xi

            gends = lax.fori_loop(0, ta, rstep, (zero, zero), unroll=True)
            nxt = (gcarry[rws, :SW][0:1], gcarry[rws, SW:][0:1])
            _s5_carries(gends, nxt, t_ref, rws, SW, cfx, reverse=True)
            cf = lax.fori_loop(0, ta, lambda k, c, st=rstep: st(k, c, store=True),
                               (cfx[:, :SW], cfx[:, SW:], zero, zero), unroll=True)
            gcarry[rws, :SW] = cf[0]
            gcarry[rws, SW:] = cf[1]
            dl_ref[rws, :SW] += cf[2]
            dl_ref[rws, SW:] += cf[3]
            gb16 = gbuf[...].astype(BF16)
            du_ref[:, cols] = jnp.dot(gb16, bt_ref[gb], preferred_element_type=F32) + d_ref[:, cols] * dyg
            db_ref[gb] += lax.dot_general(ug.astype(BF16), gb16, TN, preferred_element_type=F32)
            dc_ref[gb] += lax.dot_general(dyg.astype(BF16), xbuf[NSEG:, :].astype(BF16), TN, preferred_element_type=F32)
            dd_ref[:, cols] += _rsum(dyg * ug)

    rev = pl.BlockSpec((R, D), lambda i: (nb - 1 - i, 0))
    acc3 = (ngb, 128, 2 * SW)
    return pl.pallas_call(
        body, name=name, grid=(nb,),
        in_specs=[rev, rev, rev, pl.BlockSpec((1, crows, 2 * SW), lambda i: (nb - 1 - i, 0, 0)),
                  _whole(tb["lam"].shape), _whole(tb["lamT"].shape), _whole(tb["bbd"].shape),
                  _whole(tb["bbdT"].shape), _whole(tb["cbdT"].shape), _whole(dvec.shape)],
        out_specs=[rev, _whole(acc3), _whole(acc3), _whole((crows, 2 * SW)), _whole((1, D))],
        out_shape=[jax.ShapeDtypeStruct((L, D), F32), jax.ShapeDtypeStruct(acc3, F32), jax.ShapeDtypeStruct(acc3, F32),
                   jax.ShapeDtypeStruct((crows, 2 * SW), F32), jax.ShapeDtypeStruct((1, D), F32)],
        scratch_shapes=[pltpu.VMEM((crows, 2 * SW), F32), pltpu.VMEM((2, R + NSEG, 2 * SW), F32),
                        pltpu.VMEM((2, R, 2 * SW), F32), pltpu.VMEM((R, D), F32), pltpu.VMEM((2, NSEG, 2 * SW), F32)],
        compiler_params=_params("arbitrary"),
    )(up, y, dz, ck, tb["lam"], tb["lamT"], tb["bbd"], tb["bbdT"], tb["cbdT"], dvec)


NN = (((1,), (0,)), ((), ()))
TN = (((0,), (0,)), ((), ()))
NT = (((1,), (1,)), ((), ()))


def _dot3(lhs, rhs, dn, split):
    x = rhs if split == "rhs" else lhs
    hi = x.astype(BF16)
    r1 = x - hi.astype(F32)
    mid = r1.astype(BF16)
    lo = (r1 - mid.astype(F32)).astype(BF16)
    out = None
    for part in (hi, mid, lo):
        ops = (lhs, part) if split == "rhs" else (part, rhs)
        t = lax.dot_general(ops[0], ops[1], dn, preferred_element_type=F32)
        out = t if out is None else out + t
    return out


def _log_sigmoid(x):
    return jnp.minimum(x, 0.0) - jnp.log(1.0 + jnp.exp(-jnp.abs(x)))


def _gla_gates(p, wg_ref, bg_ref, QK):
    C = p.shape[0]
    glr = p[:, 6 * QK:6 * QK + 128].astype(BF16)
    gpre = jnp.dot(glr, wg_ref[...], preferred_element_type=F32) + bg_ref[...]
    la = _log_sigmoid(gpre) * (1.0 / GATE_TAU)
    row = lax.broadcasted_iota(jnp.int32, (C, C), 0)
    col = lax.broadcasted_iota(jnp.int32, (C, C), 1)
    gc = _dot3((row >= col).astype(BF16), la, NN, "rhs")
    ge = gc[C - 1:C, :]
    w = jnp.exp(ge - gc)
    return glr, gpre, la, ge, w


def _gla_fwd(proj, wg2p, bg, gn, *, name):
    L = proj.shape[0]
    QK = wg2p.shape[1]
    DK, DV = QK // HEADS, 2 * QK // HEADS
    nC = L // CHUNK
    NB = min(GLA_NB, nC)
    assert nC % NB == 0
    scale = DK ** -0.5

    def body(p_ref, wg_ref, bg_ref, gn_ref, og_ref, s_ref, sst):
        @pl.when(pl.program_id(0) == 0)
        def _():
            sst[...] = jnp.zeros_like(sst)

        ones = jnp.ones((CHUNK, DV), BF16)
        for cc in range(NB):
            rows = slice(cc * CHUNK, (cc + 1) * CHUNK)
            p = p_ref[rows, :]
            _, _, la, _, w = _gla_gates(p, wg_ref, bg_ref, QK)
            for h in range(HEADS):
                ks, vs = slice(h * DK, (h + 1) * DK), slice(h * DV, (h + 1) * DV)
                q = p[:, h * DK:(h + 1) * DK] * scale
                kd = p[:, QK + h * DK:QK + (h + 1) * DK] * w[:, ks]
                v = p[:, 2 * QK + h * DV:2 * QK + (h + 1) * DV]
                r = p[:, 4 * QK + h * DV:4 * QK + (h + 1) * DV]
                dec = jnp.exp(_dot3(la[:, ks], ones, TN, "lhs"))
                kv = lax.dot_general(kd.astype(BF16), v.astype(BF16), TN, preferred_element_type=F32)
                S = dec * sst[ks, :] + kv
                sst[ks, :] = S
                s_ref[cc, ks, :] = S
                o = jnp.dot(q.astype(BF16), S.astype(BF16), preferred_element_type=F32)
                on = o * lax.rsqrt(jnp.mean(o * o, axis=-1, keepdims=True) + EPS)
                og_ref[rows, vs] = (on * gn_ref[:, vs] * (r * _sigmoid(r))).astype(BF16)

    RB = NB * CHUNK
    return pl.pallas_call(
        body, name=name, grid=(nC // NB,),
        in_specs=[pl.BlockSpec((RB, proj.shape[1]), lambda i: (i, 0)), _whole(wg2p.shape), _whole(bg.shape), _whole(gn.shape)],
        out_specs=[pl.BlockSpec((RB, 2 * QK), lambda i: (i, 0)), pl.BlockSpec((NB, QK, DV), lambda i: (i, 0, 0))],
        out_shape=[jax.ShapeDtypeStruct((L, 2 * QK), BF16), jax.ShapeDtypeStruct((nC, QK, DV), F32)],
        scratch_shapes=[pltpu.VMEM((QK, DV), F32)],
        compiler_params=_params("arbitrary"),
    )(proj, wg2p, bg, gn)


def _gla_bwd(proj, dog, states, wg2p, bg, gn, *, name):
    L, W = proj.shape
    QK = wg2p.shape[1]
    DK, DV = QK // HEADS, 2 * QK // HEADS
    nC = L // CHUNK
    NB = min(GLA_NB, nC)
    nB = nC // NB
    scale = DK ** -0.5

    def body(p_ref, dog_ref, sc_ref, sp_ref, wg_ref, bg_ref, gn_ref, dp_ref, dwg_ref, dbg_ref, dgn_ref, gst):
        i = pl.program_id(0)

        @pl.when(i == 0)
        def _():
            gst[...] = jnp.zeros_like(gst)
            dwg_ref[...] = jnp.zeros_like(dwg_ref)
            dbg_ref[...] = jnp.zeros_like(dbg_ref)
            dgn_ref[...] = jnp.zeros_like(dgn_ref)

        row = lax.broadcasted_iota(jnp.int32, (CHUNK, CHUNK), 0)
        col = lax.broadcasted_iota(jnp.int32, (CHUNK, CHUNK), 1)
        tri_u = (col >= row).astype(BF16)
        ones = jnp.ones((CHUNK, DV), BF16)
        ones8 = jnp.ones((8, DV), BF16)
        for cc in range(NB - 1, -1, -1):
            rows = slice(cc * CHUNK, (cc + 1) * CHUNK)
            p = p_ref[rows, :]
            glr, gpre, la, ge, w = _gla_gates(p, wg_ref, bg_ref, QK)
            dla_heads = []
            for h in range(HEADS):
                ks, vs = slice(h * DK, (h + 1) * DK), slice(h * DV, (h + 1) * DV)
                qs = p[:, h * DK:(h + 1) * DK] * scale
                k = p[:, QK + h * DK:QK + (h + 1) * DK]
                v = p[:, 2 * QK + h * DV:2 * QK + (h + 1) * DV]
                r = p[:, 4 * QK + h * DV:4 * QK + (h + 1) * DV]
                wh = w[:, ks]
                kd = k * wh
                S = sc_ref[cc, ks, :]
                if cc > 0:
                    Sp = sc_ref[cc - 1, ks, :]
                else:
                    Sp = jnp.where(i < nB - 1, sp_ref[0, ks, :], 0.0)
                o = jnp.dot(qs.astype(BF16), S.astype(BF16), preferred_element_type=F32)
                rs = lax.rsqrt(jnp.mean(o * o, axis=-1, keepdims=True) + EPS)
                on = o * rs
                sr = _sigmoid(r)
                dg = dog_ref[rows, vs]
                gnh = gn_ref[:, vs]
                dp_ref[rows, 4 * QK + h * DV:4 * QK + (h + 1) * DV] = (
                    dg * on * gnh * (sr * (1.0 + r * (1.0 - sr)))).astype(BF16)
                dt = dg * (r * sr)
                dgn_ref[:, vs] += _rsum(dt * on)
                don = dt * gnh
                do = (rs * (don - on * jnp.mean(don * on, axis=-1, keepdims=True))).astype(BF16)
                Gc = gst[ks, :] + lax.dot_general(qs.astype(BF16), do, TN, preferred_element_type=F32)
                G16 = Gc.astype(BF16)
                dp_ref[rows, h * DK:(h + 1) * DK] = (
                    lax.dot_general(do, S.astype(BF16), NT, preferred_element_type=F32) * scale).astype(BF16)
                dkd = lax.dot_general(v.astype(BF16), G16, NT, preferred_element_type=F32)
                dp_ref[rows, 2 * QK + h * DV:2 * QK + (h + 1) * DV] = jnp.dot(
                    kd.astype(BF16), G16, preferred_element_type=F32).astype(BF16)
                gst[ks, :] = jnp.exp(_dot3(la[:, ks], ones, TN, "lhs")) * Gc
                ddec = _dot3(ones8, Gc * Sp, NT, "rhs")[0:1, :]
                dp_ref[rows, QK + h * DK:QK + (h + 1) * DK] = (dkd * wh).astype(BF16)
                dww = dkd * kd
                dge = jnp.exp(ge[:, ks]) * ddec + _rsum(dww)
                dla_heads.append(dge - _dot3(tri_u, dww, NN, "rhs"))
            dla = jnp.concatenate(dla_heads, axis=1)
            dgpre = dla * (1.0 / GATE_TAU) * (1.0 - _sigmoid(gpre))
            d16 = dgpre.astype(BF16)
            dp_ref[rows, 6 * QK:6 * QK + 128] = lax.dot_general(d16, wg_ref[...], NT, preferred_element_type=F32).astype(BF16)
            dwg_ref[...] += lax.dot_general(glr, d16, TN, preferred_element_type=F32)
            dbg_ref[...] += _rsum(dgpre)

    RB = NB * CHUNK
    rev = lambda i: (nB - 1 - i, 0)
    return pl.pallas_call(
        body, name=name, grid=(nB,),
        in_specs=[pl.BlockSpec((RB, W), rev), pl.BlockSpec((RB, 2 * QK), rev),
                  pl.BlockSpec((NB, QK, DV), lambda i: (nB - 1 - i, 0, 0)),
                  pl.BlockSpec((1, QK, DV), lambda i: (jnp.maximum(NB * (nB - 1 - i) - 1, 0), 0, 0)),
                  _whole(wg2p.shape), _whole(bg.shape), _whole(gn.shape)],
        out_specs=[pl.BlockSpec((RB, W), rev), _whole((128, QK)), _whole((1, QK)), _whole((1, 2 * QK))],
        out_shape=[jax.ShapeDtypeStruct((L, W), BF16), jax.ShapeDtypeStruct((128, QK), F32),
                   jax.ShapeDtypeStruct((1, QK), F32), jax.ShapeDtypeStruct((1, 2 * QK), F32)],
        scratch_shapes=[pltpu.VMEM((QK, DV), F32)],
        compiler_params=_params("arbitrary"),
    )(proj, dog, states, states, wg2p, bg, gn)


def _chunk_tri(rows, upper):
    r = lax.broadcasted_iota(jnp.int32, (rows, rows), 0)
    c = lax.broadcasted_iota(jnp.int32, (rows, rows), 1)
    same = (r // CHUNK) == (c // CHUNK)
    return (same & ((c >= r) if upper else (r >= c))).astype(BF16)


def _gla_block_gates(p_ref, wg_ref, bg_ref, QK, wbuf, gebuf):
    RB = p_ref.shape[0]
    glr = p_ref[:, 6 * QK:6 * QK + 128].astype(BF16)
    gpre = jnp.dot(glr, wg_ref[...], preferred_element_type=F32) + bg_ref[...]
    la = _log_sigmoid(gpre) * (1.0 / GATE_TAU)
    gc = _dot3(_chunk_tri(RB, False), la, NN, "rhs")
    for cc in range(RB // CHUNK):
        rows = slice(cc * CHUNK, (cc + 1) * CHUNK)
        ge = gc[(cc + 1) * CHUNK - 1:(cc + 1) * CHUNK, :]
        gebuf[cc:cc + 1, :] = ge
        wbuf[rows, :] = jnp.exp(ge - gc[rows, :])
    return glr, gpre, la


def _as_column(row, lanes):
    t = jnp.transpose(jnp.broadcast_to(row, (row.shape[1], row.shape[1])))
    return jnp.concatenate([t] * (lanes // row.shape[1]), axis=1)


def _as_row(col):
    return jnp.transpose(jnp.broadcast_to(col, (col.shape[0], col.shape[0])))[0:1, :]


def _gla_fwd2(proj, wg2p, bg, gn, *, name):
    L = proj.shape[0]
    QK = wg2p.shape[1]
    DK, DV = QK // HEADS, 2 * QK // HEADS
    nC = L // CHUNK
    NB = min(GLA_NB, nC)
    assert nC % NB == 0
    scale = DK ** -0.5

    def body(p_ref, wg_ref, bg_ref, gn_ref, og_ref, s_ref, sst, wbuf, gebuf):
        @pl.when(pl.program_id(0) == 0)
        def _():
            sst[...] = jnp.zeros_like(sst)

        _gla_block_gates(p_ref, wg_ref, bg_ref, QK, wbuf, gebuf)
        for cc in range(NB):
            rows = slice(cc * CHUNK, (cc + 1) * CHUNK)
            for h in range(HEADS):
                ks, vs = slice(h * DK, (h + 1) * DK), slice(h * DV, (h + 1) * DV)
                q = p_ref[rows, h * DK:(h + 1) * DK] * scale
                kd = p_ref[rows, QK + h * DK:QK + (h + 1) * DK] * wbuf[rows, ks]
                v = p_ref[rows, 2 * QK + h * DV:2 * QK + (h + 1) * DV]
                r = p_ref[rows, 4 * QK + h * DV:4 * QK + (h + 1) * DV]
                dec = jnp.exp(_as_column(gebuf[cc:cc + 1, ks], DV))
                kv = lax.dot_general(kd.astype(BF16), v.astype(BF16), TN, preferred_element_type=F32)
                S = dec * sst[ks, :] + kv
                sst[ks, :] = S
                s_ref[cc, ks, :] = S
                o = jnp.dot(q.astype(BF16), S.astype(BF16), preferred_element_type=F32)
                on = o * lax.rsqrt(jnp.mean(o * o, axis=-1, keepdims=True) + EPS)
                og_ref[rows, vs] = (on * gn_ref[:, vs] * (r * _sigmoid(r))).astype(BF16)

    RB = NB * CHUNK
    return pl.pallas_call(
        body, name=name, grid=(nC // NB,),
        in_specs=[pl.BlockSpec((RB, proj.shape[1]), lambda i: (i, 0)), _whole(wg2p.shape), _whole(bg.shape), _whole(gn.shape)],
        out_specs=[pl.BlockSpec((RB, 2 * QK), lambda i: (i, 0)), pl.BlockSpec((NB, QK, DV), lambda i: (i, 0, 0))],
        out_shape=[jax.ShapeDtypeStruct((L, 2 * QK), BF16), jax.ShapeDtypeStruct((nC, QK, DV), F32)],
        scratch_shapes=[pltpu.VMEM((QK, DV), F32), pltpu.VMEM((RB, QK), F32), pltpu.VMEM((8, QK), F32)],
        compiler_params=_params("arbitrary"),
    )(proj, wg2p, bg, gn)


def _gla_bwd2(proj, dog, states, wg2p, bg, gn, *, name):
    L, W = proj.shape
    QK = wg2p.shape[1]
    DK, DV = QK // HEADS, 2 * QK // HEADS
    nC = L // CHUNK
    NB = min(GLA_NB, nC)
    nB = nC // NB
    scale = DK ** -0.5

    def body(p_ref, dog_ref, sc_ref, sp_ref, wg_ref, bg_ref, gn_ref, dp_ref, dwg_ref, dbg_ref, dgn_ref,
             gst, wbuf, gebuf, dwwbuf, dgebuf):
        i = pl.program_id(0)

        @pl.when(i == 0)
        def _():
            gst[...] = jnp.zeros_like(gst)
            dwg_ref[...] = jnp.zeros_like(dwg_ref)
            dbg_ref[...] = jnp.zeros_like(dbg_ref)
            dgn_ref[...] = jnp.zeros_like(dgn_ref)

        RB = NB * CHUNK
        glr, gpre, _ = _gla_block_gates(p_ref, wg_ref, bg_ref, QK, wbuf, gebuf)
        for cc in range(NB - 1, -1, -1):
            rows = slice(cc * CHUNK, (cc + 1) * CHUNK)
            for h in range(HEADS):
                ks, vs = slice(h * DK, (h + 1) * DK), slice(h * DV, (h + 1) * DV)
                qs = p_ref[rows, h * DK:(h + 1) * DK] * scale
                wh = wbuf[rows, ks]
                kd = p_ref[rows, QK + h * DK:QK + (h + 1) * DK] * wh
                v = p_ref[rows, 2 * QK + h * DV:2 * QK + (h + 1) * DV]
                r = p_ref[rows, 4 * QK + h * DV:4 * QK + (h + 1) * DV]
                S = sc_ref[cc, ks, :]
                if cc > 0:
                    Sp = sc_ref[cc - 1, ks, :]
                else:
                    Sp = jnp.where(i < nB - 1, sp_ref[0, ks, :], 0.0)
                o = jnp.dot(qs.astype(BF16), S.astype(BF16), preferred_element_type=F32)
                rs = lax.rsqrt(jnp.mean(o * o, axis=-1, keepdims=True) + EPS)
                on = o * rs
                sr = _sigmoid(r)
                dg = dog_ref[rows, vs]
                gnh = gn_ref[:, vs]
                dp_ref[rows, 4 * QK + h * DV:4 * QK + (h + 1) * DV] = (
                    dg * on * gnh * (sr * (1.0 + r * (1.0 - sr)))).astype(BF16)
                dt = dg * (r * sr)
                dgn_ref[:, vs] += _rsum(dt * on)
                don = dt * gnh
                do = (rs * (don - on * jnp.mean(don * on, axis=-1, keepdims=True))).astype(BF16)
                Gc = gst[ks, :] + lax.dot_general(qs.astype(BF16), do, TN, preferred_element_type=F32)
                G16 = Gc.astype(BF16)
                dp_ref[rows, h * DK:(h + 1) * DK] = (
                    lax.dot_general(do, S.astype(BF16), NT, preferred_element_type=F32) * scale).astype(BF16)
                dkd = lax.dot_general(v.astype(BF16), G16, NT, preferred_element_type=F32)
                dp_ref[rows, 2 * QK + h * DV:2 * QK + (h + 1) * DV] = jnp.dot(
                    kd.astype(BF16), G16, preferred_element_type=F32).astype(BF16)
                ge = gebuf[cc:cc + 1, ks]
                gst[ks, :] = jnp.exp(_as_column(ge, DV)) * Gc
                ddec = _as_row(jnp.sum(Gc * Sp, axis=1, keepdims=True))
                dp_ref[rows, QK + h * DK:QK + (h + 1) * DK] = (dkd * wh).astype(BF16)
                dww = dkd * kd
                dwwbuf[rows, ks] = dww
                dgebuf[cc:cc + 1, ks] = jnp.exp(ge) * ddec + _rsum(dww)
        rev = _dot3(_chunk_tri(RB, True), dwwbuf[...], NN, "rhs")
        for cc in range(NB):
            rows = slice(cc * CHUNK, (cc + 1) * CHUNK)
            wbuf[rows, :] = dgebuf[cc:cc + 1, :] - rev[rows, :]
        dgpre = wbuf[...] * (1.0 / GATE_TAU) * (1.0 - _sigmoid(gpre))
        d16 = dgpre.astype(BF16)
        dp_ref[:, 6 * QK:6 * QK + 128] = lax.dot_general(d16, wg_ref[...], NT, preferred_element_type=F32).astype(BF16)
        dwg_ref[...] += lax.dot_general(glr, d16, TN, preferred_element_type=F32)
        dbg_ref[...] += _rsum(dgpre)

    RB = NB * CHUNK
    rev_idx = lambda i: (nB - 1 - i, 0)
    return pl.pallas_call(
        body, name=name, grid=(nB,),
        in_specs=[pl.BlockSpec((RB, W), rev_idx), pl.BlockSpec((RB, 2 * QK), rev_idx),
                  pl.BlockSpec((NB, QK, DV), lambda i: (nB - 1 - i, 0, 0)),
                  pl.BlockSpec((1, QK, DV), lambda i: (jnp.maximum(NB * (nB - 1 - i) - 1, 0), 0, 0)),
                  _whole(wg2p.shape), _whole(bg.shape), _whole(gn.shape)],
        out_specs=[pl.BlockSpec((RB, W), rev_idx), _whole((128, QK)), _whole((1, QK)), _whole((1, 2 * QK))],
        out_shape=[jax.ShapeDtypeStruct((L, W), BF16), jax.ShapeDtypeStruct((128, QK), F32),
                   jax.ShapeDtypeStruct((1, QK), F32), jax.ShapeDtypeStruct((1, 2 * QK), F32)],
        scratch_shapes=[pltpu.VMEM((QK, DV), F32), pltpu.VMEM((RB, QK), F32), pltpu.VMEM((8, QK), F32),
                        pltpu.VMEM((RB, QK), F32), pltpu.VMEM((8, QK), F32)],
        compiler_params=_params("arbitrary"),
    )(proj, dog, states, states, wg2p, bg, gn)


def _coords():
    return lax.axis_index("x"), lax.axis_index("y"), lax.axis_index("c")


def _other_chips(x, y):
    return [(1 - x, y, 2 * (1 - x) + y), (x, 1 - y, 2 * x + 1 - y), (1 - x, 1 - y, 2 * (1 - x) + 1 - y)]


def _hbm_call(body, ins, out_shapes, n_sems, *, name, alias=False):
    any_spec = pl.BlockSpec(memory_space=pl.ANY)
    return pl.pallas_call(
        body, name=name, in_specs=[any_spec] * len(ins), out_specs=[any_spec] * len(out_shapes), out_shape=out_shapes,
        scratch_shapes=[pltpu.SemaphoreType.DMA((n,)) for n in n_sems],
        input_output_aliases={k: k for k in range(len(ins))} if alias else {},
    )(*ins)


def _exchange(src, masks, *, name):
    vary = [any(m[k] for m in masks) for k in range(3)]
    nslots = 2 ** sum(vary)
    n = len(masks)

    def slot(coords):
        s = 0
        for k in range(3):
            if vary[k]:
                s = s * 2 + coords[k]
        return s

    def body(src_ref, dst_ref, send_sems, recv_sems, loc_sem):
        me = _coords()
        mine = slot(me)
        loc = pltpu.make_async_copy(src_ref, dst_ref.at[mine], loc_sem.at[0])
        loc.start()
        copies = []
        for k, m in enumerate(masks):
            peer = tuple(1 - me[d] if m[d] else me[d] for d in range(3))
            cp = pltpu.make_async_remote_copy(src_ref=src_ref, dst_ref=dst_ref.at[mine], send_sem=send_sems.at[k],
                                              recv_sem=recv_sems.at[k], device_id=peer, device_id_type=MESH)
            cp.start()
            copies.append(cp)
        for cp in copies:
            cp.wait()
        loc.wait()

    return _hbm_call(body, [src], [jax.ShapeDtypeStruct((nslots,) + tuple(src.shape), src.dtype)], (n, n, 1), name=name)[0]


def _cast_into(t, lead, kind, chip, *, name, tm=256):
    r, cc = t.shape[-2:]
    tm = min(tm, r)
    nblk = r // tm
    if kind == "col":
        shp, o_spec = (r, NCH * cc), pl.BlockSpec((tm, cc), lambda i, s: (i, s[0]))
    elif kind == "row":
        shp, o_spec = (NCH * r, cc), pl.BlockSpec((tm, cc), lambda i, s: (s[0] * nblk + i, 0))
    else:
        shp, o_spec = (NCH, r, cc), pl.BlockSpec((None, tm, cc), lambda i, s: (s[0], i, 0))

    def body(s_ref, t_ref, o_ref):
        o_ref[...] = t_ref[...].astype(o_ref.dtype)

    return pl.pallas_call(
        body, name=name,
        grid_spec=pltpu.PrefetchScalarGridSpec(
            num_scalar_prefetch=1, grid=(nblk,),
            in_specs=[pl.BlockSpec((None, tm, cc), lambda i, s: (lead, i, 0))], out_specs=o_spec),
        out_shape=jax.ShapeDtypeStruct(shp, BF16), compiler_params=_params("parallel"),
    )(chip.reshape(1).astype(jnp.int32), t)


def _gather_weights(arrs, shard_shapes, kinds, *, name):
    n = len(arrs)

    def body(*refs):
        dst = refs[n:2 * n]
        send_sems, recv_sems = refs[2 * n:]
        x, y, c = _coords()
        chip = 2 * x + y
        others = _other_chips(x, y)
        sib = (x, y, 1 - c)

        def window(p, chip_id, cc):
            r, cols = shard_shapes[p]
            h = r // 2
            if kinds[p] == "col":
                return dst[p].at[pl.ds(cc * h, h), pl.ds(pl.multiple_of(chip_id * cols, 128), cols)]
            if kinds[p] == "row":
                return dst[p].at[pl.ds(chip_id * r + cc * h, h), :]
            return dst[p].at[chip_id, pl.ds(cc * h, h), :]

        def copy(p, k, win, to):
            return pltpu.make_async_remote_copy(src_ref=win, dst_ref=win, send_sem=send_sems.at[6 * p + k],
                                                recv_sem=recv_sems.at[6 * p + k], device_id=to, device_id_type=MESH)

        sends = []
        for p in range(n):
            for j, (ox, oy, _) in enumerate(others):
                cp = copy(p, j, window(p, chip, c), (ox, oy, c))
                cp.start()
                sends.append(cp)
        for j, (_, _, oc) in enumerate(others):
            for p in range(n):
                copy(p, j, window(p, oc, c), (x, y, c)).wait_recv()
                fw = copy(p, 3 + j, window(p, oc, c), sib)
                fw.start()
                sends.append(fw)
        for p in range(n):
            for j, (_, _, oc) in enumerate(others):
                copy(p, 3 + j, window(p, oc, 1 - c), sib).wait_recv()
        for cp in sends:
            cp.wait_send()

    outs = [jax.ShapeDtypeStruct(a.shape, a.dtype) for a in arrs]
    return _hbm_call(body, arrs, outs, (6 * n, 6 * n), name=name, alias=True)


HBM_SPEC = pl.BlockSpec(memory_space=pltpu.HBM)
SEM_SPEC = pl.BlockSpec(memory_space=pltpu.SEMAPHORE)
EFFECT = pltpu.SideEffectType.DATAFLOW_SIDE_EFFECTING


def _window(ref, shard_shape, kind, chip_id, cc):
    r, cols = shard_shape
    h = r // 2
    if kind == "col":
        return ref.at[pl.ds(cc * h, h), pl.ds(pl.multiple_of(chip_id * cols, 128), cols)]
    if kind == "row":
        return ref.at[pl.ds(chip_id * r + cc * h, h), :]
    return ref.at[chip_id, pl.ds(cc * h, h), :]


def _split_start(start, arrs, n_sems, *, name):
    n, ns = len(arrs), len(n_sems)

    def body(*refs):
        start(refs[:n], refs[n:n + ns])
        refs[-1][...] = jnp.zeros_like(refs[-1])

    outs = pl.pallas_call(
        body, name=name,
        out_shape=tuple([pltpu.SemaphoreType.DMA((k,)) for k in n_sems] + [pltpu.HBM(a.shape, a.dtype) for a in arrs]
                        + [jax.ShapeDtypeStruct((8, 128), F32)]),
        in_specs=[HBM_SPEC] * n, out_specs=tuple([SEM_SPEC] * ns + [HBM_SPEC] * n + [pl.BlockSpec(memory_space=pltpu.VMEM)]),
        input_output_aliases={k: ns + k for k in range(n)},
        compiler_params=pltpu.CompilerParams(has_side_effects=EFFECT),
    )(*[pltpu.with_memory_space_constraint(a, pltpu.HBM) for a in arrs])
    return list(outs[:ns]), list(outs[ns:ns + n]), outs[-1]


def _split_wait(wait, arrs, sems, after, *, name):
    n, ns = len(arrs), len(sems)

    def body(*refs):
        wait(refs[:n], refs[n:n + ns])

    return pl.pallas_call(
        body, name=name, out_shape=tuple(pltpu.HBM(a.shape, a.dtype) for a in arrs),
        in_specs=[HBM_SPEC] * n + [SEM_SPEC] * ns + [pl.BlockSpec(memory_space=pl.ANY)], out_specs=tuple([HBM_SPEC] * n),
        input_output_aliases={k: k for k in range(n)},
        compiler_params=pltpu.CompilerParams(has_side_effects=EFFECT),
    )(*arrs, *sems, after)


def _gw_copies(refs, send_sems, recv_sems, shard_shapes, kinds, outgoing):
    x, y, c = _coords()
    chip = 2 * x + y
    out = []
    for p in range(len(refs)):
        for j, (ox, oy, oc) in enumerate(_other_chips(x, y)):
            win = _window(refs[p], shard_shapes[p], kinds[p], chip if outgoing else oc, c)
            out.append(pltpu.make_async_remote_copy(
                src_ref=win, dst_ref=win, send_sem=send_sems.at[3 * p + j], recv_sem=recv_sems.at[3 * p + j],
                device_id=(ox, oy, c), device_id_type=MESH))
    return out


def _gw_start(arrs, shard_shapes, kinds, groups, *, name):
    def start(refs, sems):
        for g, idx in enumerate(groups):
            for cp in _gw_copies([refs[p] for p in idx], sems[2 * g], sems[2 * g + 1], [shard_shapes[p] for p in idx],
                                 [kinds[p] for p in idx], True):
                cp.start()

    n_sems = [3 * len(idx) for idx in groups for _ in range(2)]
    sems, thru, token = _split_start(start, arrs, n_sems, name=name)
    return [(sems[2 * g], sems[2 * g + 1]) for g in range(len(groups))], thru, token


def _gw_wait(arrs, shard_shapes, kinds, sem_pair, after, *, name):
    def wait(refs, sems):
        for cp in _gw_copies(refs, sems[0], sems[1], shard_shapes, kinds, True):
            cp.wait_send()
        for cp in _gw_copies(refs, sems[0], sems[1], shard_shapes, kinds, False):
            cp.wait_recv()

    return _split_wait(wait, arrs, list(sem_pair), after, name=name)


def _gw_forward(arrs, shard_shapes, kinds, *, name):
    n = len(arrs)

    def body(*refs):
        dst = refs[n:2 * n]
        send_sems, recv_sems = refs[2 * n:]
        x, y, c = _coords()
        sends = []
        for p in range(n):
            for j, (_, _, oc) in enumerate(_other_chips(x, y)):
                win = _window(dst[p], shard_shapes[p], kinds[p], oc, c)
                cp = pltpu.make_async_remote_copy(src_ref=win, dst_ref=win, send_sem=send_sems.at[3 * p + j],
                                                  recv_sem=recv_sems.at[3 * p + j], device_id=(x, y, 1 - c),
                                                  device_id_type=MESH)
                cp.start()
                sends.append(cp)
        for p in range(n):
            for j, (_, _, oc) in enumerate(_other_chips(x, y)):
                win = _window(dst[p], shard_shapes[p], kinds[p], oc, 1 - c)
                pltpu.make_async_remote_copy(src_ref=win, dst_ref=win, send_sem=send_sems.at[3 * p + j],
                                             recv_sem=recv_sems.at[3 * p + j], device_id=(x, y, 1 - c),
                                             device_id_type=MESH).wait_recv()
        for cp in sends:
            cp.wait_send()

    outs = [jax.ShapeDtypeStruct(a.shape, a.dtype) for a in arrs]
    return _hbm_call(body, arrs, outs, (3 * n, 3 * n), name=name, alias=True)


def _rs_chips_copies(parts, lands, send_sems, recv_sems):
    x, y, c = _coords()
    chip = 2 * x + y
    out = []
    for p in range(len(parts)):
        for j, (ox, oy, oc) in enumerate(_other_chips(x, y)):
            out.append(pltpu.make_async_remote_copy(
                src_ref=parts[p].at[oc], dst_ref=lands[p].at[chip], send_sem=send_sems.at[3 * p + j],
                recv_sem=recv_sems.at[3 * p + j], device_id=(ox, oy, c), device_id_type=MESH))
    return out


def _rs_chips_start(parts, *, name):
    n = len(parts)

    def start(refs, sems):
        for cp in _rs_chips_copies(refs[:n], refs[n:], sems[0], sems[1]):
            cp.start()

    lands = [lax.empty(t.shape, t.dtype) for t in parts]
    sems, thru, token = _split_start(start, list(parts) + lands, [3 * n, 3 * n], name=name)
    return (sems[0], sems[1]), thru[:n], thru[n:], token


def _rs_chips_wait(groups, after, *, name):
    sizes = [len(g[1]) for g in groups]
    arrs = [a for g in groups for a in list(g[1]) + list(g[2])]
    sems = [s for g in groups for s in g[0]]

    def wait(refs, sem_refs):
        o = 0
        for k, n in enumerate(sizes):
            for cp in _rs_chips_copies(refs[o:o + n], refs[o + n:o + 2 * n], sem_refs[2 * k], sem_refs[2 * k + 1]):
                cp.wait()
            o += 2 * n

    outs = _split_wait(wait, arrs, sems, after, name=name)
    res, o = [], 0
    for n in sizes:
        res.append((list(outs[o:o + n]), list(outs[o + n:o + 2 * n])))
        o += 2 * n
    return res


def _rs_cores_copies(grads, lands, send_sems, recv_sems):
    x, y, c = _coords()
    out, o = [], 0
    for p in range(len(grads)):
        nsh, h = lands[p].shape[0], lands[p].shape[1]
        for j in range(nsh):
            out.append(pltpu.make_async_remote_copy(
                src_ref=grads[p].at[j, pl.ds((1 - c) * h, h), :], dst_ref=lands[p].at[j],
                send_sem=send_sems.at[o + j], recv_sem=recv_sems.at[o + j], device_id=(x, y, 1 - c), device_id_type=MESH))
        o += nsh
    return out


def _rs_cores_start(grads, *, name):
    n = len(grads)
    tot = sum(g.shape[0] for g in grads)

    def start(refs, sems):
        for cp in _rs_cores_copies(refs[:n], refs[n:], sems[0], sems[1]):
            cp.start()

    lands = [lax.empty((g.shape[0], g.shape[1] // 2, g.shape[2]), g.dtype) for g in grads]
    sems, thru, token = _split_start(start, list(grads) + lands, [tot, tot], name=name)
    return (sems[0], sems[1]), thru[:n], thru[n:], token


def _rs_cores_wait(pair, grads, lands, after, *, name):
    n = len(grads)

    def wait(refs, sems):
        for cp in _rs_cores_copies(refs[:n], refs[n:], sems[0], sems[1]):
            cp.wait()

    outs = _split_wait(wait, list(grads) + list(lands), list(pair), after, name=name)
    return list(outs[:n]), list(outs[n:])


def _rs_cores(grads, *, name):
    n = len(grads)
    outs = [jax.ShapeDtypeStruct((g.shape[0], g.shape[1] // 2, g.shape[2]), g.dtype) for g in grads]

    def body(*refs):
        src, dst = refs[:n], refs[n:2 * n]
        send_sems, recv_sems = refs[2 * n:]
        x, y, c = _coords()
        copies = []
        for p in range(n):
            nsh, r, _ = grads[p].shape
            h = r // 2
            for j in range(nsh):
                cp = pltpu.make_async_remote_copy(
                    src_ref=src[p].at[j, pl.ds((1 - c) * h, h), :], dst_ref=dst[p].at[j],
                    send_sem=send_sems.at[nsh * p + j], recv_sem=recv_sems.at[nsh * p + j],
                    device_id=(x, y, 1 - c), device_id_type=MESH)
                cp.start()
                copies.append(cp)
        for cp in copies:
            cp.wait()

    tot = sum(g.shape[0] for g in grads)
    return _hbm_call(body, grads, outs, (tot, tot), name=name)


def _sum_own_half(full, recv, ci, out_dtype, *, name):
    nsh, h, cols = recv.shape
    tm = h if nsh * h * cols * 4 <= (2 << 20) else _tile_rows(h, 256)
    nblk = h // tm

    def body(c_ref, f_ref, r_ref, o_ref):
        o_ref[...] = (f_ref[...] + r_ref[...]).astype(o_ref.dtype)

    return pl.pallas_call(
        body, name=name,
        grid_spec=pltpu.PrefetchScalarGridSpec(
            num_scalar_prefetch=1, grid=(nsh, nblk),
            in_specs=[pl.BlockSpec((1, tm, cols), lambda j, i, c_ref: (j, c_ref[0] * nblk + i, 0)),
                      pl.BlockSpec((1, tm, cols), lambda j, i, c_ref: (j, i, 0))],
            out_specs=pl.BlockSpec((1, tm, cols), lambda j, i, c_ref: (j, i, 0))),
        out_shape=jax.ShapeDtypeStruct((nsh, h, cols), out_dtype), compiler_params=_params("parallel", "parallel"),
    )(ci.reshape(1).astype(jnp.int32), full, recv)


def _rs_chips(parts, *, name):
    n = len(parts)
    outs = [jax.ShapeDtypeStruct(t.shape, t.dtype) for t in parts]

    def body(*refs):
        src, dst = refs[:n], refs[n:2 * n]
        send_sems, recv_sems = refs[2 * n:]
        x, y, c = _coords()
        chip = 2 * x + y
        copies = []
        for p in range(n):
            for j, (ox, oy, oc) in enumerate(_other_chips(x, y)):
                cp = pltpu.make_async_remote_copy(
                    src_ref=src[p].at[oc], dst_ref=dst[p].at[chip], send_sem=send_sems.at[3 * p + j],
                    recv_sem=recv_sems.at[3 * p + j], device_id=(ox, oy, c), device_id_type=MESH)
                cp.start()
                copies.append(cp)
        for cp in copies:
            cp.wait()

    return _hbm_call(body, parts, outs, (3 * n, 3 * n), name=name)


def _sum_chips(recv, own, chip, ci, *, name, nlead=1, lead=0, prev=None, spread=False):
    nsh, h, cols = recv.shape
    tm = h if nsh * h * cols * 4 <= (2 << 20) else _tile_rows(h, 256)
    nblk = h // tm
    rows_out = 2 * h * (nsh if spread else 1)

    def body(s_ref, r_ref, o_ref, *rest):
        out_ref = rest[-1]
        t = None
        for s in range(nsh):
            v = jnp.where(s_ref[0] == s, o_ref[s], r_ref[s]).astype(F32)
            t = v if t is None else t + v
        out_ref[...] = t

    def out_idx(i, s):
        return (lead, (s[0] * 2 * nblk if spread else 0) + s[1] * nblk + i, 0)

    blk = pl.BlockSpec((nsh, tm, cols), lambda i, s: (0, i, 0))
    ins = [recv, own] + ([prev] if prev is not None else [])
    return pl.pallas_call(
        body, name=name,
        grid_spec=pltpu.PrefetchScalarGridSpec(
            num_scalar_prefetch=1, grid=(nblk,),
            in_specs=[blk, blk] + ([pl.BlockSpec(memory_space=pl.ANY)] if prev is not None else []),
            out_specs=pl.BlockSpec((None, tm, cols), out_idx)),
        out_shape=jax.ShapeDtypeStruct((nlead, rows_out, cols), F32),
        input_output_aliases={3: 0} if prev is not None else {},
        compiler_params=_params("arbitrary"),
    )(jnp.stack([chip, ci]).astype(jnp.int32), *ins)


def _rs_gather(arrs, halves, spread, *, name, nchunk=4):
    n = len(arrs)
    per = [a.shape[0] * nchunk for a in arrs]
    offs = [sum(per[:p]) for p in range(n)]

    def body(*refs):
        dst = refs[n:2 * n]
        send_sems, recv_sems = refs[2 * n:]
        x, y, c = _coords()
        chip = 2 * x + y
        copies = []
        for p in range(n):
            h = halves[p]
            q = h // nchunk
            base = chip * 2 * h if spread[p] else 0
            for l in range(arrs[p].shape[0]):
                for k in range(nchunk):
                    win = dst[p].at[l, pl.ds(base + c * h + k * q, q), :]
                    sem = offs[p] + l * nchunk + k
                    cp = pltpu.make_async_remote_copy(src_ref=win, dst_ref=win, send_sem=send_sems.at[sem],
                                                      recv_sem=recv_sems.at[sem], device_id=(x, y, 1 - c),
                                                      device_id_type=MESH)
                    cp.start()
                    copies.append(cp)
        for cp in copies:
            cp.wait_send()
        for p in range(n):
            h = halves[p]
            q = h // nchunk
            base = chip * 2 * h if spread[p] else 0
            for l in range(arrs[p].shape[0]):
                for k in range(nchunk):
                    win = dst[p].at[l, pl.ds(base + (1 - c) * h + k * q, q), :]
                    sem = offs[p] + l * nchunk + k
                    pltpu.make_async_remote_copy(src_ref=win, dst_ref=win, send_sem=send_sems.at[sem],
                                                 recv_sem=recv_sems.at[sem], device_id=(x, y, 1 - c),
                                                 device_id_type=MESH).wait_recv()

    outs = [jax.ShapeDtypeStruct(a.shape, a.dtype) for a in arrs]
    return _hbm_call(body, arrs, outs, (sum(per), sum(per)), name=name, alias=True)


def _adamw(w, g, m, v, *, name):
    nl, R, C = w.shape
    tm = _tile_rows(R, 256)

    blk = pl.BlockSpec((None, tm, C), lambda l, i: (l, i, 0))
    return pl.pallas_call(
        _adamw_body_copy(), name=name, grid=(nl, R // tm), in_specs=[blk] * 4, out_specs=[blk] * 3,
        out_shape=[jax.ShapeDtypeStruct((nl, R, C), F32)] * 3, compiler_params=_params("parallel", "parallel"),
    )(w, g, m, v)


def _adamw_body(w_ref, g_ref, m_ref, v_ref, d_ref, nm_ref, nv_ref):
    gg = g_ref[...]
    nm = B1 * m_ref[...] + (1.0 - B1) * gg
    nv = B2 * v_ref[...] + (1.0 - B2) * (gg * gg)
    m_hat = nm / (1.0 - B1 ** ASTEP)
    v_hat = nv / (1.0 - B2 ** ASTEP)
    d_ref[...] = -LR * (m_hat / (jnp.sqrt(v_hat) + AEPS) + WD * w_ref[...])
    nm_ref[...] = nm
    nv_ref[...] = nv


def _adamw_whole(w, g, m, v, *, name):
    return pl.pallas_call(_adamw_body_copy(), name=name, out_shape=[jax.ShapeDtypeStruct(w.shape, F32)] * 3,
                          compiler_params=_params())(w, g, m, v)


def _adamw_body_copy():
    def body(*refs):
        _adamw_body(*refs)
    return body


def _mod_cols(c_all, w_ada, b_cols, *, name):
    nl, D, cols = w_ada.shape
    B = c_all.shape[0]

    def body(c_ref, w_ref, b_ref, o_ref):
        cc = c_ref[...]
        cs = (cc * _sigmoid(cc)).astype(BF16)
        o_ref[0] = jnp.dot(cs, w_ref[0].astype(BF16), preferred_element_type=F32) + b_ref[0]

    return pl.pallas_call(
        body, name=name, grid=(nl,),
        in_specs=[_whole(c_all.shape), pl.BlockSpec((1, D, cols), lambda i: (i, 0, 0)), pl.BlockSpec((1, 1, cols), lambda i: (i, 0, 0))],
        out_specs=pl.BlockSpec((1, B, cols), lambda i: (i, 0, 0)),
        out_shape=jax.ShapeDtypeStruct((nl, B, cols), F32), compiler_params=_params("arbitrary"),
    )(c_all, w_ada, b_cols)


def _ada_grad(c_all, dmod_cols, *, name):
    nl, B, cols = dmod_cols.shape
    D = c_all.shape[1]

    def body(c_ref, d_ref, o_ref):
        cc = c_ref[...]
        cs = (cc * _sigmoid(cc)).astype(BF16)
        o_ref[0] = lax.dot_general(cs, d_ref[0].astype(BF16), TN, preferred_element_type=F32)

    return pl.pallas_call(
        body, name=name, grid=(nl,),
        in_specs=[_whole(c_all.shape), pl.BlockSpec((1, B, cols), lambda i: (i, 0, 0))],
        out_specs=pl.BlockSpec((1, D, cols), lambda i: (i, 0, 0)),
        out_shape=jax.ShapeDtypeStruct((nl, D, cols), F32), compiler_params=_params("arbitrary"),
    )(c_all, dmod_cols)


def _s5_disc(a_re, a_im, log_dt, b_re, b_im):
    dt = jnp.exp(log_dt)[:, None]
    mag = jnp.exp(a_re * dt)
    ph = a_im * dt
    lb_re = mag * jnp.cos(ph)
    lb_im = mag * jnp.sin(ph)
    den = a_re * a_re + a_im * a_im
    nr = lb_re - 1.0
    ni = lb_im
    f_re = (nr * a_re + ni * a_im) / den
    f_im = (ni * a_re - nr * a_im) / den
    bb_re = f_re[..., None] * b_re - f_im[..., None] * b_im
    bb_im = f_re[..., None] * b_im + f_im[..., None] * b_re
    return lb_re, lb_im, bb_re, bb_im


def _to_segments(t):
    L, D = t.shape
    R = min(S5_R, L)
    return t.reshape(L // R, NSEG, R // NSEG, D).transpose(0, 2, 1, 3).reshape(L, D)


def _from_segments(t):
    L, D = t.shape
    R = min(S5_R, L)
    return t.reshape(L // R, R // NSEG, NSEG, D).transpose(0, 2, 1, 3).reshape(L, D)


def _mlp_fwd(h2, w1, w2, tag):
    a = _matmul(h2, w1, name=f"ff1_{tag}", out_dtypes=(BF16,), epi=lambda acc: (jnp.maximum(acc, 0.0),))
    f = _matmul(a, w2, name=f"ff2_{tag}", a_fn=jnp.square)
    return a, f


def _mlp_bwd(df, h2, a, w1, w2, tag):
    da = _matmul(df, w2, tb=True, name=f"ff2_dx_{tag}", out_dtypes=(BF16,), epi_ins=(a,),
                 epi=lambda acc, at: (acc * (2.0 * at.astype(F32)),))
    dw2 = _matmul(a, df, ta=True, name=f"ff2_dw_{tag}", a_fn=jnp.square)
    dh2 = _matmul(da, w1, tb=True, name=f"ff1_dx_{tag}")
    dw1 = _matmul(h2, da, ta=True, name=f"ff1_dw_{tag}", col_shards=NCH)
    return dh2, dw1, dw2


def kernel(x, c, w_ada, b_ada, norm_mix, norm_mlp, s5_a_re, s5_a_im, s5_log_dt, s5_b_re, s5_b_im, s5_c_re, s5_c_im, s5_d, s5_w_glu, gla_w_in, gla_w_gate2, gla_b_gate, gla_g_norm, gla_w_out, w_ff1, w_ff2, norm_final, loss_target, m_w_ada, m_b_ada, m_norm_mix, m_norm_mlp, m_s5_a_re, m_s5_a_im, m_s5_log_dt, m_s5_b_re, m_s5_b_im, m_s5_c_re, m_s5_c_im, m_s5_d, m_s5_w_glu, m_gla_w_in, m_gla_w_gate2, m_gla_b_gate, m_gla_g_norm, m_gla_w_out, m_w_ff1, m_w_ff2, m_norm_final, v_w_ada, v_b_ada, v_norm_mix, v_norm_mlp, v_s5_a_re, v_s5_a_im, v_s5_log_dt, v_s5_b_re, v_s5_b_im, v_s5_c_re, v_s5_c_im, v_s5_d, v_s5_w_glu, v_gla_w_in, v_gla_w_gate2, v_gla_b_gate, v_gla_g_norm, v_gla_w_out, v_w_ff1, v_w_ff2, v_norm_final):
    args = dict(locals())
    L, D = x.shape[1], x.shape[2]
    QK = D // 2
    xi, yi, ci = _coords()
    chip = 2 * xi + yi
    dev = 2 * chip + ci

    cat = jnp.concatenate([gla_w_gate2[0].reshape(1, -1), gla_b_gate, gla_g_norm], axis=1)
    first = _exchange(jnp.concatenate([c.reshape(8, D // 8), jnp.tile(cat, (8, 1))], axis=1), MASK_ALL, name="gather_c")
    c_all = first[:, :, :D // 8].reshape(8, D)
    cat_all = first[0::2, 0, D // 8:]
    acols = w_ada.shape[2]
    b_cols = lax.dynamic_slice_in_dim(b_ada, chip * acols, acols, axis=1)[:, None, :]
    mod_cols = _mod_cols(c_all, w_ada, b_cols, name="ada_mod")
    mod_all = _exchange(mod_cols.reshape(16, acols), MASK_CHIPS, name="gather_mod")
    mod_all = mod_all.reshape(NCH, 2, 8, acols).transpose(1, 2, 0, 3).reshape(2, 8, NCH * acols)
    mod = lax.dynamic_index_in_dim(mod_all, dev, axis=1, keepdims=False).reshape(2, 6, 1, D)

    big = [("s5_w_glu", s5_w_glu, 0, "col"), ("gla_w_in", gla_w_in, 0, "slot"), ("gla_w_out", gla_w_out, 0, "row"),
           ("w_ff1_0", w_ff1, 0, "col"), ("w_ff1_1", w_ff1, 1, "col"), ("w_ff2_0", w_ff2, 0, "row"), ("w_ff2_1", w_ff2, 1, "row")]
    own16 = [_cast_into(t, lead, kind, chip, name=f"cast_{nm}") for nm, t, lead, kind in big]
    wshapes, wkinds = [b[1].shape[-2:] for b in big], [b[3] for b in big]
    wgroups = [[0, 3, 5], [1, 2, 4, 6]]
    wsems, wthru, wtoken = _gw_start(own16, wshapes, wkinds, wgroups, name="gather_w_start")
    W = {}

    def finish_weights(g, after):
        idx = wgroups[g]
        shp, knd = [wshapes[p] for p in idx], [wkinds[p] for p in idx]
        got = _gw_wait([wthru[p] for p in idx], shp, knd, wsems[g], after, name=f"gather_w_wait{g}")
        for p, w in zip(idx, _gw_forward(got, shp, knd, name=f"gather_w_cores{g}")):
            W[big[p][0]] = w

    qk4 = QK // NCH
    wg2 = cat_all[:, :GATE_RANK * qk4].reshape(NCH, GATE_RANK, qk4).transpose(1, 0, 2).reshape(GATE_RANK, QK)
    bg = cat_all[:, GATE_RANK * qk4:(GATE_RANK + 1) * qk4].reshape(1, QK)
    gn = cat_all[:, (GATE_RANK + 1) * qk4:].reshape(1, D)
    wg2p = jnp.concatenate([wg2, jnp.zeros((128 - GATE_RANK, QK), F32)], axis=0).astype(BF16)

    lb_re, lb_im, bb_re, bb_im = _s5_disc(s5_a_re[0], s5_a_im[0], s5_log_dt[0], s5_b_re[0], s5_b_im[0])
    tb = _s5_tables(lb_re, lb_im, bb_re, bb_im, s5_c_re[0], s5_c_im[0], min(S5_R, L) // NSEG)
    s5_dv = s5_d + wtoken[0, 0]

    def vec(t):
        return t.reshape(1, -1)

    m0, m1 = mod[0], mod[1]
    xp = _to_segments(x[0])
    (u0,) = _rows(lambda t, g, sc, sh: (_norm_mod(t, g, sc, sh),), [xp], [vec(norm_mix[0]), m0[1], m0[0]],
                  [(D, F32)], [], name="pre_mix0")
    y0, z0, ck0 = _s5_fwd2(u0, tb, s5_dv, name="s5_fwd")
    finish_weights(0, z0)
    vg0 = _matmul(z0, W["s5_w_glu"], name="glu")

    def res_glu_pre(xt, vgt, gt, g, sc, sh):
        xn = xt + gt * (vgt[:, :D] * _sigmoid(vgt[:, D:]))
        return xn, _norm_mod(xn, g, sc, sh)

    x2_0, h2_0 = _rows(res_glu_pre, [xp, vg0], [m0[2], vec(norm_mlp[0]), m0[4], m0[3]], [(D, F32), (D, BF16)], [],
                       name="res_mix0")
    a_0, f0 = _mlp_fwd(h2_0, W["w_ff1_0"], W["w_ff2_0"], "0")

    def res_pre(xt, bt, gt, g, sc, sh):
        xn = xt + gt * bt
        return xn, _norm_mod(xn, g, sc, sh)

    x3p, h1p = _rows(res_pre, [x2_0, f0], [m0[5], vec(norm_mix[1]), m1[1], m1[0]], [(D, F32), (D, BF16)], [],
                     name="res_mlp0")
    x3, h1 = _from_segments(x3p), _from_segments(h1p)
    finish_weights(1, f0)
    w_in = W["gla_w_in"].transpose(1, 0, 2).reshape(D, -1)
    w_in_r = jnp.concatenate([w_in[:, :4 * QK], w_in[:, 4 * QK + GATE_RANK:], w_in[:, 4 * QK:4 * QK + GATE_RANK],
                              jnp.zeros((D, 128 - GATE_RANK), BF16)], axis=1)
    proj = _matmul(h1, w_in_r, name="gla_in", tn=640)
    og, states = _gla_fwd2(proj, wg2p, bg, gn, name="gla_fwd")
    ymix = _matmul(og, W["gla_w_out"], name="gla_out")
    x2_1, h2_1 = _rows(res_pre, [x3, ymix], [m1[2], vec(norm_mlp[1]), m1[4], m1[3]], [(D, F32), (D, BF16)], [],
                       name="res_mix1")
    a_1, f1 = _mlp_fwd(h2_1, W["w_ff1_1"], W["w_ff2_1"], "1")

    def final(xt, ft, tgt, gt, g):
        xn = xt + gt * ft
        rs = lax.rsqrt(jnp.mean(xn * xn, axis=-1, keepdims=True) + EPS)
        xh = xn * rs
        e = xh * g - tgt
        dout = e * (1.0 / D)
        dxh = dout * g
        dx = rs * (dxh - xh * jnp.mean(dxh * xh, axis=-1, keepdims=True))
        lsum = 0.5 * jnp.sum(jnp.sum(e * e, axis=-1, keepdims=True), axis=0, keepdims=True) * (1.0 / D)
        return dx, dx * gt, jnp.broadcast_to(lsum, (1, 128)), _rsum(dout * xh), _rsum(dx * ft)

    dx, df1, loss_part, d_norm_final, dgt2_1 = _rows(
        final, [x2_1, f1, loss_target[0]], [m1[5], vec(norm_final)], [(D, F32), (D, BF16)],
        [(1, 128), (1, D), (1, D)], name="loss_head")
    loss = lax.psum(loss_part[0, 0], ("x", "y", "c"))

    def gate_bwd(dxt, bt, gt):
        return dxt * gt, _rsum(dxt * bt)

    def norm_bwd(xt, dht, drt, g, sc):
        dxn, dsh, dsc, dg = _norm_mod_bwd(xt, dht, g, sc)
        return drt + dxn, dsh, dsc, dg

    def norm_gate_bwd(xt, dht, drt, bt, g, sc, gt):
        dxn, dsh, dsc, dg = _norm_mod_bwd(xt, dht, g, sc)
        dxt = drt + dxn
        return dxt, dxt * gt, dsh, dsc, dg, _rsum(dxt * bt)

    vD = [(1, D)]
    dh2_1, dw_ff1_1, dw_ff2_1 = _mlp_bwd(df1, h2_1, a_1, W["w_ff1_1"], W["w_ff2_1"], "1")
    dx, dmix1, dsh2_1, dsc2_1, dg_mlp1, dgt1_1 = _rows(
        norm_gate_bwd, [x2_1, dh2_1, dx, ymix], [vec(norm_mlp[1]), m1[4], m1[2]], [(D, F32), (D, BF16)], vD * 4,
        name="norm_mlp1_bwd")
    dog = _matmul(dmix1, W["gla_w_out"], tb=True, name="gla_out_dx")
    dw_out = _matmul(og, dmix1, ta=True, name="gla_out_dw")
    dproj, dwg2p, dbg, dgn = _gla_bwd2(proj, dog, states, wg2p, bg, gn, name="gla_bwd")
    dh1 = _matmul(dproj, w_in_r, tb=True, name="gla_in_dx", tk=640)
    dw_in_r = _matmul(h1, dproj, ta=True, name="gla_in_dw", tn=640)
    dx, dsh1_1, dsc1_1, dg_mix1 = _rows(norm_bwd, [x3, dh1, dx], [vec(norm_mix[1]), m1[1]], [(D, F32)], vD * 3,
                                        name="norm_mix1_bwd")
    dxp = _to_segments(dx)
    tags = [b[0] for b in big] + ["small"]
    rs_groups = []

    def rs_chips_begin(idx, srcs, r1, gname):
        s1 = [_sum_own_half(g, r, ci, F32 if tags[k] == "small" else BF16, name=f"rs_sum_cores_{tags[k]}")
              for g, r, k in zip(srcs, r1, idx)]
        pair, parts, lands, token = _rs_chips_start(s1, name=f"rs_chips_start_{gname}")
        rs_groups.append((idx, pair, parts, lands))
        return token

    def rs_begin(idx, srcs, gname):
        return rs_chips_begin(idx, srcs, _rs_cores(srcs, name=f"rs_cores_{gname}"), gname)

    dw_in = jnp.concatenate([dw_in_r[:, :4 * QK], dw_in_r[:, 6 * QK:6 * QK + GATE_RANK], dw_in_r[:, 4 * QK:6 * QK]], axis=1)
    dw_in = dw_in.reshape(D, NCH, -1).transpose(1, 0, 2)
    idx1 = [1, 2, 4, 6]
    pair1, src1, land1, tok1 = _rs_cores_start(
        [dw_in, dw_out.reshape(NCH, -1, D), dw_ff1_1, dw_ff2_1.reshape(NCH, -1, D)], name="rs_cores_start_l1")

    df0, dgt2_0 = _rows(gate_bwd, [dxp, f0], [m0[5] + tok1[0, 0]], [(D, BF16)], vD, name="gate_mlp0")
    dh2_0, dw_ff1_0, dw_ff2_0 = _mlp_bwd(df0, h2_0, a_0, W["w_ff1_0"], W["w_ff2_0"], "0")
    src1, land1 = _rs_cores_wait(pair1, src1, land1, dh2_0, name="rs_cores_wait_l1")
    tok1b = rs_chips_begin(idx1, src1, land1, "l1")
    idx0 = [3, 5]
    pair0, src0, land0, tok0 = _rs_cores_start([dw_ff1_0, dw_ff2_0.reshape(NCH, -1, D)], name="rs_cores_start_l0")
    tok2 = tok1b + tok0

    def norm_glu_bwd(xt, dht, drt, vgt, g, sc, gt):
        dxn, dsh, dsc, dg = _norm_mod_bwd(xt, dht, g, sc)
        dxt = drt + dxn
        val, sg = vgt[:, :D], _sigmoid(vgt[:, D:])
        dbr = dxt * gt
        dvg = jnp.concatenate([dbr * sg, dbr * val * sg * (1.0 - sg)], axis=1)
        return dxt, dvg, dsh, dsc, dg, _rsum(dxt * val * sg)

    dxp, dvg0, dsh2_0, dsc2_0, dg_mlp0, dgt1_0 = _rows(
        norm_glu_bwd, [x2_0, dh2_0, dxp, vg0], [vec(norm_mlp[0]), m0[4] + tok2[0, 0], m0[2]], [(D, F32), (2 * D, BF16)],
        vD * 4, name="norm_mlp0_bwd")
    dz0 = _matmul(dvg0, W["s5_w_glu"], tb=True, name="glu_dx")
    dw_glu = _matmul(z0, dvg0, ta=True, name="glu_dw", tn=512, col_shards=NCH)
    src0, land0 = _rs_cores_wait(pair0, src0, land0, dw_glu, name="rs_cores_wait_l0")
    tok0b = rs_chips_begin(idx0 + [0], src0 + [dw_glu], land0 + list(_rs_cores([dw_glu], name="rs_cores_glu")), "l0")
    du0, db_acc, dc_acc, dl_acc, dd_s5 = _s5_bwd2(u0, y0, dz0, ck0, tb, s5_dv + tok0b[0, 0], name="s5_bwd")
    dxp, dsh1_0, dsc1_0, dg_mix0 = _rows(norm_bwd, [xp, du0, dxp], [vec(norm_mix[0]), m0[1]], [(D, F32)], vD * 3,
                                         name="norm_mix0_bwd")
    grad_x = _from_segments(dxp)[None]

    dmod = jnp.concatenate([dsh1_0, dsc1_0, dgt1_0, dsh2_0, dsc2_0, dgt2_0,
                            dsh1_1, dsc1_1, dgt1_1, dsh2_1, dsc2_1, dgt2_1], axis=1)
    dbb_re, dbb_im = _s5_untable(db_acc)
    dc_re, dc_im_neg = _s5_untable(dc_acc)
    nbk = D // 128
    dl = dl_acc.reshape(nbk, NSEG, 2, GPB * S5_P).sum(axis=1)
    smalls = [dmod, dg_mix0, dg_mix1, dg_mlp0, dg_mlp1, d_norm_final, dd_s5, dbg, dgn,
              dwg2p[:GATE_RANK].reshape(1, -1), dbb_re.reshape(1, -1), dbb_im.reshape(1, -1),
              dc_re.reshape(1, -1), dc_im_neg.reshape(1, -1), dl.reshape(1, -1)]
    ssz = [t.shape[1] for t in smalls]
    stot = sum(ssz)
    spad = -(-stot // 8192) * 8192
    svec = jnp.concatenate(smalls + [jnp.zeros((1, spad - stot), F32)], axis=1).reshape(NCH, spad // (128 * NCH), 128)

    dmod_all = _exchange(dmod.reshape(12 * D // 128, 128), MASK_ALL, name="gather_dmod").reshape(8, 2, 6 * D)
    dmod_cols = lax.dynamic_slice_in_dim(dmod_all, chip * acols, acols, axis=2).transpose(1, 0, 2)
    g_w_ada = _ada_grad(c_all, dmod_cols, name="ada_grad")

    rs_begin([7], [svec], "last")
    landed = _rs_chips_wait([(g[1], g[2], g[3]) for g in rs_groups], grad_x, name="rs_chips_wait")
    s1, r2 = {}, {}
    for (idx, _, _, _), (parts, lands) in zip(rs_groups, landed):
        for k, part, land in zip(idx, parts, lands):
            s1[k], r2[k] = part, land

    def fin(k, **kw):
        return _sum_chips(r2[k], s1[k], chip, ci, name=f"rs_sum_chips_{tags[k]}", **kw)

    f_ff1 = fin(4, nlead=2, lead=1, prev=fin(3, nlead=2, lead=0))
    f_ff2 = fin(6, nlead=2, lead=1, prev=fin(5, nlead=2, lead=0))
    finals = [fin(0), fin(1), fin(2), f_ff1, f_ff2, fin(7, spread=True)]
    halves = [t.shape[1] for t in (s1[0], s1[1], s1[2], s1[3], s1[5], s1[7])]
    g_glu, g_in, g_out, g_w_ff1, g_w_ff2, s_own = _rs_gather(finals, halves, [False] * 5 + [True], name="rs_gather_cores")
    srows = spad // (128 * NCH)
    (s_sum,) = _gather_weights([s_own.reshape(NCH * srows, 128)], [(srows, 128)], ["row"], name="gather_small_grads")
    s_sum = s_sum.reshape(-1)
    so = [sum(ssz[:k]) for k in range(len(ssz))]
    sm = [s_sum[o:o + n] for o, n in zip(so, ssz)]
    (dmod_s, g_mix0, g_mix1, g_mlp0, g_mlp1, g_nf, g_d, g_bg, g_gn, g_wg2, g_bbre, g_bbim, g_cre, g_cimn, g_dl) = sm
    g_b_ada = dmod_s.reshape(2, 6 * D)

    G = D // S5_H
    _, disc_vjp = jax.vjp(_s5_disc, s5_a_re[0], s5_a_im[0], s5_log_dt[0], s5_b_re[0], s5_b_im[0])
    g_dl = g_dl.reshape(nbk, 2, GPB, S5_P)
    ct = (g_dl[:, 0].reshape(G, S5_P), g_dl[:, 1].reshape(G, S5_P),
          g_bbre.reshape(G, S5_H, S5_P).transpose(0, 2, 1), g_bbim.reshape(G, S5_H, S5_P).transpose(0, 2, 1))
    g_a_re, g_a_im, g_log_dt, g_b_re, g_b_im = disc_vjp(ct)
    g_c_re = g_cre.reshape(G, S5_H, S5_P)
    g_c_im = -g_cimn.reshape(G, S5_H, S5_P)
    g_wg2_s = lax.dynamic_slice_in_dim(g_wg2.reshape(GATE_RANK, QK), chip * qk4, qk4, axis=1)
    g_bg_s = lax.dynamic_slice_in_dim(g_bg.reshape(1, QK), chip * qk4, qk4, axis=1)
    g_gn_s = lax.dynamic_slice_in_dim(g_gn.reshape(1, D), chip * (D // NCH), D // NCH, axis=1)

    grads = dict(
        w_ada=g_w_ada, b_ada=g_b_ada, norm_mix=jnp.stack([g_mix0, g_mix1]), norm_mlp=jnp.stack([g_mlp0, g_mlp1]),
        s5_a_re=g_a_re[None], s5_a_im=g_a_im[None], s5_log_dt=g_log_dt[None], s5_b_re=g_b_re[None], s5_b_im=g_b_im[None],
        s5_c_re=g_c_re[None], s5_c_im=g_c_im[None], s5_d=g_d[None], s5_w_glu=g_glu,
        gla_w_in=g_in, gla_w_gate2=g_wg2_s[None], gla_b_gate=g_bg_s, gla_g_norm=g_gn_s,
        gla_w_out=g_out, w_ff1=g_w_ff1, w_ff2=g_w_ff2, norm_final=g_nf)

    names = list(grads)
    large = ("w_ada", "s5_w_glu", "gla_w_in", "gla_w_out", "w_ff1", "w_ff2")
    delta, new_m, new_v = {}, {}, {}
    for nm in large:
        delta[nm], new_m[nm], new_v[nm] = _adamw(args[nm], grads[nm], args["m_" + nm], args["v_" + nm], name=f"adamw_{nm}")
    grads = {nm: grads[nm].reshape(args[nm].shape) for nm in names}
    for nm in names:
        if nm not in large:
            shp = args[nm].shape
            as2d = (1, -1) if len(shp) == 1 else shp
            outs = _adamw_whole(*[t.reshape(as2d) for t in (args[nm], grads[nm], args["m_" + nm], args["v_" + nm])],
                                name=f"adamw_{nm}")
            delta[nm], new_m[nm], new_v[nm] = (t.reshape(shp) for t in outs)
    return (loss, grad_x, *[grads[n] for n in names], *[delta[n] for n in names], *[new_m[n] for n in names],
            *[new_v[n] for n in names])
```

```python
import math

import jax
import jax.numpy as jnp
from jax import lax
from jax.experimental import pallas as pl
from jax.experimental.pallas import tpu as pltpu

F32 = jnp.float32
BF16 = jnp.bfloat16
MESH = pl.DeviceIdType.MESH

EPS = 1e-6
CHUNK = 64
GLA_NB = 4
S5_H = 16
S5_P = 64
GPB = 8
NSEG = 8
HEADS = 4
GATE_RANK = 16
GATE_TAU = 16.0
NCH = 4
LR, B1, B2, AEPS, WD, ASTEP = 0.001, 0.9, 0.999, 1e-08, 0.01, 10
VMEM_LIMIT = 56 << 20

MASK_CHIPS = ((1, 0, 0), (0, 1, 0), (1, 1, 0))
MASK_ALL = ((0, 0, 1), (0, 1, 0), (0, 1, 1), (1, 0, 0), (1, 0, 1), (1, 1, 0), (1, 1, 1))


def _params(*sem):
    return pltpu.CompilerParams(dimension_semantics=sem or None, vmem_limit_bytes=VMEM_LIMIT)


def _tile_rows(rows, cap=512):
    best = 8
    for t in range(8, cap + 1, 8):
        if rows % t == 0:
            best = t
    return best


def _whole(shape):
    return pl.BlockSpec(shape, lambda i, _n=len(shape): (0,) * _n)


def _matmul(a, b, *, name, ta=False, tb=False, tm=1024, tn=1024, tk=2048, out_dtypes=(F32,),
            a_fn=None, epi=None, epi_ins=(), col_shards=1):
    M, K = (a.shape[1], a.shape[0]) if ta else a.shape
    N = b.shape[0] if tb else b.shape[1]
    tm, tn, tk = min(tm, M), min(tn, N), min(tk, K)
    assert M % tm == 0 and N % tn == 0 and K % tk == 0, (name, M, N, K)
    nk = K // tk
    ne = len(epi_ins)
    dn = (((0 if ta else 1,), (1 if tb else 0,)), ((), ()))

    def body(a_ref, b_ref, *rest):
        e_refs, o_refs, acc = rest[:ne], rest[ne:-1], rest[-1]
        k = pl.program_id(2)

        @pl.when(k == 0)
        def _():
            acc[...] = jnp.zeros_like(acc)

        at = a_ref[...]
        if a_fn is not None:
            at = a_fn(at)
        acc[...] += lax.dot_general(at.astype(BF16), b_ref[...].astype(BF16), dn, preferred_element_type=F32)

        @pl.when(k == nk - 1)
        def _():
            outs = (acc[...],) if epi is None else epi(acc[...], *[r[...] for r in e_refs])
            for r, o in zip(o_refs, outs):
                r[...] = o.astype(r.dtype)

    a_spec = pl.BlockSpec((tk, tm), lambda i, j, k: (k, i)) if ta else pl.BlockSpec((tm, tk), lambda i, j, k: (i, k))
    b_spec = pl.BlockSpec((tn, tk), lambda i, j, k: (j, k)) if tb else pl.BlockSpec((tk, tn), lambda i, j, k: (k, j))
    o_spec = pl.BlockSpec((tm, tn), lambda i, j, k: (i, j))
    if col_shards > 1:
        per = N // col_shards // tn
        assert ne == 0 and per * tn * col_shards == N
        w_spec = pl.BlockSpec((None, tm, tn), lambda i, j, k: (j // per, i, j % per))
        o_shape = (col_shards, M, N // col_shards)
    else:
        w_spec, o_shape = o_spec, (M, N)
    outs = pl.pallas_call(
        body, name=name, grid=(M // tm, N // tn, nk),
        in_specs=[a_spec, b_spec] + [o_spec] * ne,
        out_specs=[w_spec] * len(out_dtypes),
        out_shape=[jax.ShapeDtypeStruct(o_shape, d) for d in out_dtypes],
        scratch_shapes=[pltpu.VMEM((tm, tn), F32)],
        compiler_params=_params("parallel", "parallel", "arbitrary"),
    )(a, b, *epi_ins)
    return outs[0] if len(outs) == 1 else outs


def _rows(fn, rows_in, vecs_in, rows_out, acc_out, *, name, tm=512):
    L = rows_in[0].shape[0]
    tm = min(tm, L)
    assert L % tm == 0
    nr, nv, no, na = len(rows_in), len(vecs_in), len(rows_out), len(acc_out)

    def body(*refs):
        rin, vin = refs[:nr], refs[nr:nr + nv]
        rout, aout = refs[nr + nv:nr + nv + no], refs[nr + nv + no:]
        outs = fn(*[r[...] for r in rin], *[v[...] for v in vin])
        for r, o in zip(rout, outs[:no]):
            r[...] = o.astype(r.dtype)
        if na:
            @pl.when(pl.program_id(0) == 0)
            def _():
                for r in aout:
                    r[...] = jnp.zeros_like(r)

            for r, o in zip(aout, outs[no:]):
                r[...] += o

    outs = pl.pallas_call(
        body, name=name, grid=(L // tm,),
        in_specs=[pl.BlockSpec((tm, r.shape[1]), lambda i: (i, 0)) for r in rows_in] + [_whole(v.shape) for v in vecs_in],
        out_specs=[pl.BlockSpec((tm, c), lambda i: (i, 0)) for c, _ in rows_out] + [_whole(s) for s in acc_out],
        out_shape=[jax.ShapeDtypeStruct((L, c), d) for c, d in rows_out] + [jax.ShapeDtypeStruct(s, F32) for s in acc_out],
        compiler_params=_params("arbitrary"),
    )(*rows_in, *vecs_in)
    return outs


def _rsum(t):
    return jnp.sum(t, axis=0, keepdims=True)


def _norm_mod(x, g, sc, sh):
    rs = lax.rsqrt(jnp.mean(x * x, axis=-1, keepdims=True) + EPS)
    return x * rs * g * (1.0 + sc) + sh


def _norm_mod_bwd(x, dh, g, sc):
    rs = lax.rsqrt(jnp.mean(x * x, axis=-1, keepdims=True) + EPS)
    xh = x * rs
    dn = dh * (1.0 + sc)
    dxh = dn * g
    dx = rs * (dxh - xh * jnp.mean(dxh * xh, axis=-1, keepdims=True))
    return dx, _rsum(dh), _rsum(dh * xh * g), _rsum(dn * xh)


def _sigmoid(x):
    return jax.nn.sigmoid(x)


def _gelu(y):
    return jax.nn.gelu(y, approximate=True)


def _gelu_grad(y):
    c = math.sqrt(2.0 / math.pi)
    t = jnp.tanh(c * (y + 0.044715 * y * y * y))
    return 0.5 * (1.0 + t) + 0.5 * y * (1.0 - t * t) * c * (1.0 + 3.0 * 0.044715 * y * y)


def _s5_tables(lb_re, lb_im, bb_re, bb_im, c_re, c_im, seg_len):
    G = lb_re.shape[0]
    nb = G // GPB
    eye = jnp.eye(GPB, dtype=F32)

    def bdiag(t):
        a, b = t.shape[1:]
        t = t.reshape(nb, GPB, a, b)
        return (t[:, :, :, None, :] * eye[None, :, None, :, None]).reshape(nb, GPB * a, GPB * b)

    bbd = jnp.concatenate([bdiag(bb_re.transpose(0, 2, 1)), bdiag(bb_im.transpose(0, 2, 1))], axis=2)
    cbd = jnp.concatenate([bdiag(c_re.transpose(0, 2, 1)), -bdiag(c_im.transpose(0, 2, 1))], axis=1)

    def lanes(re, im):
        t = jnp.concatenate([re.reshape(nb, GPB * S5_P), im.reshape(nb, GPB * S5_P)], axis=1)
        return jnp.repeat(t, NSEG, axis=0)

    tr, ti = lb_re, lb_im
    for _ in range(int(math.log2(seg_len))):
        tr, ti = tr * tr - ti * ti, 2.0 * tr * ti
    return dict(bbd=bbd.astype(BF16), bbdT=bbd.transpose(0, 2, 1).astype(BF16), cbd=cbd.astype(BF16),
                cbdT=cbd.transpose(0, 2, 1).astype(BF16), lam=lanes(lb_re, lb_im), lamT=lanes(tr, ti))


def _s5_untable(acc):
    nb = acc.shape[0]
    t = acc.reshape(nb, GPB, S5_H, 2, GPB, S5_P)
    d = jnp.diagonal(t, axis1=1, axis2=4)
    d = d.transpose(0, 4, 2, 1, 3).reshape(nb * GPB, 2, S5_H, S5_P)
    return d[:, 0], d[:, 1]


def _s5_fix(ends, lamT, *, reverse, name):
    nrow = ends.shape[0]
    half = ends.shape[1] // 2

    def body(e_ref, t_ref, o_ref):
        for gb in range(nrow // NSEG):
            r0 = gb * NSEG
            tr, ti = t_ref[r0:r0 + 1, :half], t_ref[r0:r0 + 1, half:]
            cr = jnp.zeros((1, half), F32)
            ci = jnp.zeros((1, half), F32)
            order = range(NSEG - 1, -1, -1) if reverse else range(NSEG)
            for n, s in enumerate(order):
                if n > 0:
                    p = s + 1 if reverse else s - 1
                    er, ei = e_ref[r0 + p:r0 + p + 1, :half], e_ref[r0 + p:r0 + p + 1, half:]
                    if reverse:
                        cr, ci = tr * cr + ti * ci + er, tr * ci - ti * cr + ei
                    else:
                        cr, ci = tr * cr - ti * ci + er, tr * ci + ti * cr + ei
                o_ref[r0 + s:r0 + s + 1, :half] = cr
                o_ref[r0 + s:r0 + s + 1, half:] = ci

    return pl.pallas_call(body, name=name, out_shape=jax.ShapeDtypeStruct(ends.shape, F32),
                          compiler_params=_params())(ends, lamT)


def _s5_fwd(up, carry_in, tb, dvec, *, emit, name, R=256):
    L, D = up.shape
    R = min(R, L)
    nb, ta, ngb = L // R, R // NSEG, D // 128
    SW = GPB * S5_P
    crows = ngb * NSEG

    def body(u_ref, cin_ref, lam_ref, b_ref, c_ref, d_ref, *rest):
        if emit:
            y_ref, z_ref, ck_ref, carry, xbuf = rest
        else:
            cout_ref, carry, xbuf = rest
        i = pl.program_id(0)

        @pl.when(i == 0)
        def _():
            carry[...] = cin_ref[...]

        if emit:
            ck_ref[0] = carry[...]
        for gb in range(ngb):
            cols = slice(gb * 128, (gb + 1) * 128)
            rws = slice(gb * NSEG, (gb + 1) * NSEG)
            ug = u_ref[:, cols]
            xbuf[...] = jnp.dot(ug.astype(BF16), b_ref[gb], preferred_element_type=F32)
            lr, li = lam_ref[rws, :SW], lam_ref[rws, SW:]

            def step(a, c, lr=lr, li=li):
                cr, ci = c
                o = pl.multiple_of(a * NSEG, NSEG)
                nr = lr * cr - li * ci + xbuf[pl.ds(o, NSEG), :SW]
                ni = lr * ci + li * cr + xbuf[pl.ds(o, NSEG), SW:]
                if emit:
                    xbuf[pl.ds(o, NSEG), :SW] = nr
                    xbuf[pl.ds(o, NSEG), SW:] = ni
                return nr, ni

            cr, ci = lax.fori_loop(0, ta, step, (carry[rws, :SW], carry[rws, SW:]), unroll=2)
            carry[rws, :SW] = cr
            carry[rws, SW:] = ci
            if emit:
                y = jnp.dot(xbuf[...].astype(BF16), c_ref[gb], preferred_element_type=F32) + d_ref[:, cols] * ug
                y_ref[:, cols] = y
                z_ref[:, cols] = _gelu(y).astype(BF16)
        if not emit:
            cout_ref[...] = carry[...]

    rowblk = pl.BlockSpec((R, D), lambda i: (i, 0))
    if emit:
        out_shape = [jax.ShapeDtypeStruct((L, D), F32), jax.ShapeDtypeStruct((L, D), BF16),
                     jax.ShapeDtypeStruct((nb, crows, 2 * SW), F32)]
        out_specs = [rowblk, rowblk, pl.BlockSpec((1, crows, 2 * SW), lambda i: (i, 0, 0))]
    else:
        out_shape = [jax.ShapeDtypeStruct((crows, 2 * SW), F32)]
        out_specs = [_whole((crows, 2 * SW))]
    return pl.pallas_call(
        body, name=name, grid=(nb,),
        in_specs=[rowblk, _whole(carry_in.shape), _whole(tb["lam"].shape), _whole(tb["bbd"].shape),
                  _whole(tb["cbd"].shape), _whole(dvec.shape)],
        out_specs=out_specs, out_shape=out_shape,
        scratch_shapes=[pltpu.VMEM((crows, 2 * SW), F32), pltpu.VMEM((R, 2 * SW), F32)],
        compiler_params=_params("arbitrary"),
    )(up, carry_in, tb["lam"], tb["bbd"], tb["cbd"], dvec)


def _s5_bwd(up, y, dz, ck, gcarry_in, tb, dvec, *, emit, name, R=256):
    L, D = up.shape
    R = min(R, L)
    nb, ta, ngb = L // R, R // NSEG, D // 128
    SW = GPB * S5_P
    crows = ngb * NSEG

    def body(u_ref, y_ref, dz_ref, ck_ref, gin_ref, lam_ref, b_ref, bt_ref, ct_ref, d_ref, *rest):
        if emit:
            du_ref, db_ref, dc_ref, dl_ref, dd_ref, gcarry, xbuf, gbuf, dybuf = rest
        else:
            gout_ref, gcarry, gbuf, dybuf = rest
        i = pl.program_id(0)

        @pl.when(i == 0)
        def _():
            gcarry[...] = gin_ref[...]
            if emit:
                db_ref[...] = jnp.zeros_like(db_ref)
                dc_ref[...] = jnp.zeros_like(dc_ref)
                dl_ref[...] = jnp.zeros_like(dl_ref)
                dd_ref[...] = jnp.zeros_like(dd_ref)

        dybuf[...] = dz_ref[...] * _gelu_grad(y_ref[...])
        for gb in range(ngb):
            cols = slice(gb * 128, (gb + 1) * 128)
            rws = slice(gb * NSEG, (gb + 1) * NSEG)
            dyg = dybuf[:, cols]
            lr, li = lam_ref[rws, :SW], lam_ref[rws, SW:]
            gbuf[...] = jnp.dot(dyg.astype(BF16), ct_ref[gb], preferred_element_type=F32)
            if emit:
                ug = u_ref[:, cols]
                xbuf[0:NSEG, :] = ck_ref[0, rws, :]
                xbuf[NSEG:, :] = jnp.dot(ug.astype(BF16), b_ref[gb], preferred_element_type=F32)

                def fstep(a, c, lr=lr, li=li):
                    cr, ci = c
                    o = pl.multiple_of(a * NSEG + NSEG, NSEG)
                    nr = lr * cr - li * ci + xbuf[pl.ds(o, NSEG), :SW]
                    ni = lr * ci + li * cr + xbuf[pl.ds(o, NSEG), SW:]
                    xbuf[pl.ds(o, NSEG), :SW] = nr
                    xbuf[pl.ds(o, NSEG), SW:] = ni
                    return nr, ni

                lax.fori_loop(0, ta, fstep, (xbuf[0:NSEG, :SW], xbuf[0:NSEG, SW:]), unroll=2)

            def rstep(k, c, lr=lr, li=li):
                o = pl.multiple_of((ta - 1 - k) * NSEG, NSEG)
                gr_n, gi_n = c[0], c[1]
                gr = gbuf[pl.ds(o, NSEG), :SW] + lr * gr_n + li * gi_n
                gi = gbuf[pl.ds(o, NSEG), SW:] - li * gr_n + lr * gi_n
                if not emit:
                    return gr, gi
                gbuf[pl.ds(o, NSEG), :SW] = gr
                gbuf[pl.ds(o, NSEG), SW:] = gi
                xr, xi = xbuf[pl.ds(o, NSEG), :SW], xbuf[pl.ds(o, NSEG), SW:]
                return gr, gi, c[2] + gr * xr + gi * xi, c[3] + gi * xr - gr * xi

            c0 = (gcarry[rws, :SW], gcarry[rws, SW:])
            if emit:
                c0 = c0 + (jnp.zeros((NSEG, SW), F32), jnp.zeros((NSEG, SW), F32))
            cf = lax.fori_loop(0, ta, rstep, c0, unroll=2)
            gcarry[rws, :SW] = cf[0]
            gcarry[rws, SW:] = cf[1]
            if emit:
                dl_ref[rws, :SW] += cf[2]
                dl_ref[rws, SW:] += cf[3]
                gb16 = gbuf[...].astype(BF16)
                du_ref[:, cols] = jnp.dot(gb16, bt_ref[gb], preferred_element_type=F32) + d_ref[:, cols] * dyg
                tn = (((0,), (0,)), ((), ()))
                db_ref[gb] += lax.dot_general(ug.astype(BF16), gb16, tn, preferred_element_type=F32)
                dc_ref[gb] += lax.dot_general(dyg.astype(BF16), xbuf[NSEG:, :].astype(BF16), tn,
                                              preferred_element_type=F32)
                dd_ref[:, cols] += _rsum(dyg * ug)
        if not emit:
            gout_ref[...] = gcarry[...]

    rev = pl.BlockSpec((R, D), lambda i: (nb - 1 - i, 0))
    acc3 = (ngb, 128, 2 * SW)
    if emit:
        out_shape = [jax.ShapeDtypeStruct((L, D), F32), jax.ShapeDtypeStruct(acc3, F32), jax.ShapeDtypeStruct(acc3, F32),
                     jax.ShapeDtypeStruct((crows, 2 * SW), F32), jax.ShapeDtypeStruct((1, D), F32)]
        out_specs = [rev, _whole(acc3), _whole(acc3), _whole((crows, 2 * SW)), _whole((1, D))]
        scratch = [pltpu.VMEM((crows, 2 * SW), F32), pltpu.VMEM((R + NSEG, 2 * SW), F32),
                   pltpu.VMEM((R, 2 * SW), F32), pltpu.VMEM((R, D), F32)]
    else:
        out_shape = [jax.ShapeDtypeStruct((crows, 2 * SW), F32)]
        out_specs = [_whole((crows, 2 * SW))]
        scratch = [pltpu.VMEM((crows, 2 * SW), F32), pltpu.VMEM((R, 2 * SW), F32), pltpu.VMEM((R, D), F32)]
    return pl.pallas_call(
        body, name=name, grid=(nb,),
        in_specs=[rev, rev, rev, pl.BlockSpec((1, crows, 2 * SW), lambda i: (nb - 1 - i, 0, 0)),
                  _whole(gcarry_in.shape), _whole(tb["lam"].shape), _whole(tb["bbd"].shape),
                  _whole(tb["bbdT"].shape), _whole(tb["cbdT"].shape), _whole(dvec.shape)],
        out_specs=out_specs, out_shape=out_shape, scratch_shapes=scratch,
        compiler_params=_params("arbitrary"),
    )(up, y, dz, ck, gcarry_in, tb["lam"], tb["bbd"], tb["bbdT"], tb["cbdT"], dvec)


S5_R = 256


def _s5_carries(ends, first, t_ref, rws, SW, cfx, *, reverse):
    er, ei = ends
    tr, ti = t_ref[rws, :SW][0:1], t_ref[rws, SW:][0:1]
    cr, ci = first
    order = range(NSEG - 1, -1, -1) if reverse else range(NSEG)
    for n, s in enumerate(order):
        if n > 0:
            p = s + 1 if reverse else s - 1
            if reverse:
                cr, ci = tr * cr + ti * ci + er[p:p + 1], tr * ci - ti * cr + ei[p:p + 1]
            else:
                cr, ci = tr * cr - ti * ci + er[p:p + 1], tr * ci + ti * cr + ei[p:p + 1]
        cfx[s:s + 1, :SW] = cr
        cfx[s:s + 1, SW:] = ci


def _s5_fwd2(up, tb, dvec, *, name):
    L, D = up.shape
    R = min(S5_R, L)
    nb, ta, ngb = L // R, R // NSEG, D // 128
    SW = GPB * S5_P
    crows = ngb * NSEG

    def body(u_ref, lam_ref, t_ref, b_ref, c_ref, d_ref, y_ref, z_ref, ck_ref, carry, xbuf2, cfx2):
        @pl.when(pl.program_id(0) == 0)
        def _():
            carry[...] = jnp.zeros_like(carry)

        zero = jnp.zeros((NSEG, SW), F32)
        for gb in range(ngb):
            xbuf, cfx = xbuf2.at[gb % 2], cfx2.at[gb % 2]
            cols = slice(gb * 128, (gb + 1) * 128)
            rws = slice(gb * NSEG, (gb + 1) * NSEG)
            ug = u_ref[:, cols]
            xbuf[...] = jnp.dot(ug.astype(BF16), b_ref[gb], preferred_element_type=F32)
            lr, li = lam_ref[rws, :SW], lam_ref[rws, SW:]

            def step(a, c, lr=lr, li=li, store=False):
                cr, ci = c
                o = pl.multiple_of(a * NSEG, NSEG)
                nr = lr * cr - li * ci + xbuf[pl.ds(o, NSEG), :SW]
                ni = lr * ci + li * cr + xbuf[pl.ds(o, NSEG), SW:]
                if store:
                    xbuf[pl.ds(o, NSEG), :SW] = nr
                    xbuf[pl.ds(o, NSEG), SW:] = ni
                return nr, ni

            ends = lax.fori_loop(0, ta, step, (zero, zero), unroll=True)
            prev = (carry[rws, :SW][NSEG - 1:NSEG], carry[rws, SW:][NSEG - 1:NSEG])
            _s5_carries(ends, prev, t_ref, rws, SW, cfx, reverse=False)
            ck_ref[0, rws, :] = cfx[...]
            cr, ci = lax.fori_loop(0, ta, lambda a, c, st=step: st(a, c, store=True), (cfx[:, :SW], cfx[:, SW:]),
                                   unroll=True)
            carry[rws, :SW] = cr
            carry[rws, SW:] = ci
            y = jnp.dot(xbuf[...].astype(BF16), c_ref[gb], preferred_element_type=F32) + d_ref[:, cols] * ug
            y_ref[:, cols] = y
            z_ref[:, cols] = _gelu(y).astype(BF16)

    rowblk = pl.BlockSpec((R, D), lambda i: (i, 0))
    return pl.pallas_call(
        body, name=name, grid=(nb,),
        in_specs=[rowblk, _whole(tb["lam"].shape), _whole(tb["lamT"].shape), _whole(tb["bbd"].shape),
                  _whole(tb["cbd"].shape), _whole(dvec.shape)],
        out_specs=[rowblk, rowblk, pl.BlockSpec((1, crows, 2 * SW), lambda i: (i, 0, 0))],
        out_shape=[jax.ShapeDtypeStruct((L, D), F32), jax.ShapeDtypeStruct((L, D), BF16),
                   jax.ShapeDtypeStruct((nb, crows, 2 * SW), F32)],
        scratch_shapes=[pltpu.VMEM((crows, 2 * SW), F32), pltpu.VMEM((2, R, 2 * SW), F32),
                        pltpu.VMEM((2, NSEG, 2 * SW), F32)],
        compiler_params=_params("arbitrary"),
    )(up, tb["lam"], tb["lamT"], tb["bbd"], tb["cbd"], dvec)


def _s5_bwd2(up, y, dz, ck, tb, dvec, *, name):
    L, D = up.shape
    R = min(S5_R, L)
    nb, ta, ngb = L // R, R // NSEG, D // 128
    SW = GPB * S5_P
    crows = ngb * NSEG

    def body(u_ref, y_ref, dz_ref, ck_ref, lam_ref, t_ref, b_ref, bt_ref, ct_ref, d_ref,
             du_ref, db_ref, dc_ref, dl_ref, dd_ref, gcarry, xbuf2, gbuf2, dybuf, cfx2):
        @pl.when(pl.program_id(0) == 0)
        def _():
            gcarry[...] = jnp.zeros_like(gcarry)
            db_ref[...] = jnp.zeros_like(db_ref)
            dc_ref[...] = jnp.zeros_like(dc_ref)
            dl_ref[...] = jnp.zeros_like(dl_ref)
            dd_ref[...] = jnp.zeros_like(dd_ref)

        zero = jnp.zeros((NSEG, SW), F32)
        dybuf[...] = dz_ref[...] * _gelu_grad(y_ref[...])
        for gb in range(ngb):
            xbuf, gbuf, cfx = xbuf2.at[gb % 2], gbuf2.at[gb % 2], cfx2.at[gb % 2]
            cols = slice(gb * 128, (gb + 1) * 128)
            rws = slice(gb * NSEG, (gb + 1) * NSEG)
            dyg = dybuf[:, cols]
            ug = u_ref[:, cols]
            lr, li = lam_ref[rws, :SW], lam_ref[rws, SW:]
            gbuf[...] = jnp.dot(dyg.astype(BF16), ct_ref[gb], preferred_element_type=F32)
            xbuf[0:NSEG, :] = ck_ref[0, rws, :]
            xbuf[NSEG:, :] = jnp.dot(ug.astype(BF16), b_ref[gb], preferred_element_type=F32)

            def fstep(a, c, lr=lr, li=li):
                cr, ci = c
                o = pl.multiple_of(a * NSEG + NSEG, NSEG)
                nr = lr * cr - li * ci + xbuf[pl.ds(o, NSEG), :SW]
                ni = lr * ci + li * cr + xbuf[pl.ds(o, NSEG), SW:]
                xbuf[pl.ds(o, NSEG), :SW] = nr
                xbuf[pl.ds(o, NSEG), SW:] = ni
                return nr, ni

            lax.fori_loop(0, ta, fstep, (xbuf[0:NSEG, :SW], xbuf[0:NSEG, SW:]), unroll=True)

            def rstep(k, c, lr=lr, li=li, store=False):
                o = pl.multiple_of((ta - 1 - k) * NSEG, NSEG)
                gr_n, gi_n = c[0], c[1]
                gr = gbuf[pl.ds(o, NSEG), :SW] + lr * gr_n + li * gi_n
                gi = gbuf[pl.ds(o, NSEG), SW:] - li * gr_n + lr * gi_n
                if not store:
                    return gr, gi
                gbuf[pl.ds(o, NSEG), :SW] = gr
                gbuf[pl.ds(o, NSEG), SW:] = gi
                xr, xi = xbuf[pl.ds(o, NSEG), :SW], xbuf[pl.ds(o, NSEG), SW:]
                return gr, gi, c[2] + gr * xr + gi * xi, c[3] + gi * xr - gr * xi

            gends = lax.fori_loop(0, ta, rstep, (zero, zero), unroll=True)
            nxt = (gcarry[rws, :SW][0:1], gcarry[rws, SW:][0:1])
            _s5_carries(gends, nxt, t_ref, rws, SW, cfx, reverse=True)
            cf = lax.fori_loop(0, ta, lambda k, c, st=rstep: st(k, c, store=True),
                               (cfx[:, :SW], cfx[:, SW:], zero, zero), unroll=True)
            gcarry[rws, :SW] = cf[0]
            gcarry[rws, SW:] = cf[1]
            dl_ref[rws, :SW] += cf[2]
            dl_ref[rws, SW:] += cf[3]
            gb16 = gbuf[...].astype(BF16)
            du_ref[:, cols] = jnp.dot(gb16, bt_ref[gb], preferred_element_type=F32) + d_ref[:, cols] * dyg
            db_ref[gb] += lax.dot_general(ug.astype(BF16), gb16, TN, preferred_element_type=F32)
            dc_ref[gb] += lax.dot_general(dyg.astype(BF16), xbuf[NSEG:, :].astype(BF16), TN, preferred_element_type=F32)
            dd_ref[:, cols] += _rsum(dyg * ug)

    rev = pl.BlockSpec((R, D), lambda i: (nb - 1 - i, 0))
    acc3 = (ngb, 128, 2 * SW)
    return pl.pallas_call(
        body, name=name, grid=(nb,),
        in_specs=[rev, rev, rev, pl.BlockSpec((1, crows, 2 * SW), lambda i: (nb - 1 - i, 0, 0)),
                  _whole(tb["lam"].shape), _whole(tb["lamT"].shape), _whole(tb["bbd"].shape),
                  _whole(tb["bbdT"].shape), _whole(tb["cbdT"].shape), _whole(dvec.shape)],
        out_specs=[rev, _whole(acc3), _whole(acc3), _whole((crows, 2 * SW)), _whole((1, D))],
        out_shape=[jax.ShapeDtypeStruct((L, D), F32), jax.ShapeDtypeStruct(acc3, F32), jax.ShapeDtypeStruct(acc3, F32),
                   jax.ShapeDtypeStruct((crows, 2 * SW), F32), jax.ShapeDtypeStruct((1, D), F32)],
        scratch_shapes=[pltpu.VMEM((crows, 2 * SW), F32), pltpu.VMEM((2, R + NSEG, 2 * SW), F32),
                        pltpu.VMEM((2, R, 2 * SW), F32), pltpu.VMEM((R, D), F32), pltpu.VMEM((2, NSEG, 2 * SW), F32)],
        compiler_params=_params("arbitrary"),
    )(up, y, dz, ck, tb["lam"], tb["lamT"], tb["bbd"], tb["bbdT"], tb["cbdT"], dvec)


NN = (((1,), (0,)), ((), ()))
TN = (((0,), (0,)), ((), ()))
NT = (((1,), (1,)), ((), ()))


def _dot3(lhs, rhs, dn, split):
    x = rhs if split == "rhs" else lhs
    hi = x.astype(BF16)
    r1 = x - hi.astype(F32)
    mid = r1.astype(BF16)
    lo = (r1 - mid.astype(F32)).astype(BF16)
    out = None
    for part in (hi, mid, lo):
        ops = (lhs, part) if split == "rhs" else (part, rhs)
        t = lax.dot_general(ops[0], ops[1], dn, preferred_element_type=F32)
        out = t if out is None else out + t
    return out


def _log_sigmoid(x):
    return jnp.minimum(x, 0.0) - jnp.log(1.0 + jnp.exp(-jnp.abs(x)))


def _gla_gates(p, wg_ref, bg_ref, QK):
    C = p.shape[0]
    glr = p[:, 6 * QK:6 * QK + 128].astype(BF16)
    gpre = jnp.dot(glr, wg_ref[...], preferred_element_type=F32) + bg_ref[...]
    la = _log_sigmoid(gpre) * (1.0 / GATE_TAU)
    row = lax.broadcasted_iota(jnp.int32, (C, C), 0)
    col = lax.broadcasted_iota(jnp.int32, (C, C), 1)
    gc = _dot3((row >= col).astype(BF16), la, NN, "rhs")
    ge = gc[C - 1:C, :]
    w = jnp.exp(ge - gc)
    return glr, gpre, la, ge, w


def _gla_fwd(proj, wg2p, bg, gn, *, name):
    L = proj.shape[0]
    QK = wg2p.shape[1]
    DK, DV = QK // HEADS, 2 * QK // HEADS
    nC = L // CHUNK
    NB = min(GLA_NB, nC)
    assert nC % NB == 0
    scale = DK ** -0.5

    def body(p_ref, wg_ref, bg_ref, gn_ref, og_ref, s_ref, sst):
        @pl.when(pl.program_id(0) == 0)
        def _():
            sst[...] = jnp.zeros_like(sst)

        ones = jnp.ones((CHUNK, DV), BF16)
        for cc in range(NB):
            rows = slice(cc * CHUNK, (cc + 1) * CHUNK)
            p = p_ref[rows, :]
            _, _, la, _, w = _gla_gates(p, wg_ref, bg_ref, QK)
            for h in range(HEADS):
                ks, vs = slice(h * DK, (h + 1) * DK), slice(h * DV, (h + 1) * DV)
                q = p[:, h * DK:(h + 1) * DK] * scale
                kd = p[:, QK + h * DK:QK + (h + 1) * DK] * w[:, ks]
                v = p[:, 2 * QK + h * DV:2 * QK + (h + 1) * DV]
                r = p[:, 4 * QK + h * DV:4 * QK + (h + 1) * DV]
                dec = jnp.exp(_dot3(la[:, ks], ones, TN, "lhs"))
                kv = lax.dot_general(kd.astype(BF16), v.astype(BF16), TN, preferred_element_type=F32)
                S = dec * sst[ks, :] + kv
                sst[ks, :] = S
                s_ref[cc, ks, :] = S
                o = jnp.dot(q.astype(BF16), S.astype(BF16), preferred_element_type=F32)
                on = o * lax.rsqrt(jnp.mean(o * o, axis=-1, keepdims=True) + EPS)
                og_ref[rows, vs] = (on * gn_ref[:, vs] * (r * _sigmoid(r))).astype(BF16)

    RB = NB * CHUNK
    return pl.pallas_call(
        body, name=name, grid=(nC // NB,),
        in_specs=[pl.BlockSpec((RB, proj.shape[1]), lambda i: (i, 0)), _whole(wg2p.shape), _whole(bg.shape), _whole(gn.shape)],
        out_specs=[pl.BlockSpec((RB, 2 * QK), lambda i: (i, 0)), pl.BlockSpec((NB, QK, DV), lambda i: (i, 0, 0))],
        out_shape=[jax.ShapeDtypeStruct((L, 2 * QK), BF16), jax.ShapeDtypeStruct((nC, QK, DV), F32)],
        scratch_shapes=[pltpu.VMEM((QK, DV), F32)],
        compiler_params=_params("arbitrary"),
    )(proj, wg2p, bg, gn)


def _gla_bwd(proj, dog, states, wg2p, bg, gn, *, name):
    L, W = proj.shape
    QK = wg2p.shape[1]
    DK, DV = QK // HEADS, 2 * QK // HEADS
    nC = L // CHUNK
    NB = min(GLA_NB, nC)
    nB = nC // NB
    scale = DK ** -0.5

    def body(p_ref, dog_ref, sc_ref, sp_ref, wg_ref, bg_ref, gn_ref, dp_ref, dwg_ref, dbg_ref, dgn_ref, gst):
        i = pl.program_id(0)

        @pl.when(i == 0)
        def _():
            gst[...] = jnp.zeros_like(gst)
            dwg_ref[...] = jnp.zeros_like(dwg_ref)
            dbg_ref[...] = jnp.zeros_like(dbg_ref)
            dgn_ref[...] = jnp.zeros_like(dgn_ref)

        row = lax.broadcasted_iota(jnp.int32, (CHUNK, CHUNK), 0)
        col = lax.broadcasted_iota(jnp.int32, (CHUNK, CHUNK), 1)
        tri_u = (col >= row).astype(BF16)
        ones = jnp.ones((CHUNK, DV), BF16)
        ones8 = jnp.ones((8, DV), BF16)
        for cc in range(NB - 1, -1, -1):
            rows = slice(cc * CHUNK, (cc + 1) * CHUNK)
            p = p_ref[rows, :]
            glr, gpre, la, ge, w = _gla_gates(p, wg_ref, bg_ref, QK)
            dla_heads = []
            for h in range(HEADS):
                ks, vs = slice(h * DK, (h + 1) * DK), slice(h * DV, (h + 1) * DV)
                qs = p[:, h * DK:(h + 1) * DK] * scale
                k = p[:, QK + h * DK:QK + (h + 1) * DK]
                v = p[:, 2 * QK + h * DV:2 * QK + (h + 1) * DV]
                r = p[:, 4 * QK + h * DV:4 * QK + (h + 1) * DV]
                wh = w[:, ks]
                kd = k * wh
                S = sc_ref[cc, ks, :]
                if cc > 0:
                    Sp = sc_ref[cc - 1, ks, :]
                else:
                    Sp = jnp.where(i < nB - 1, sp_ref[0, ks, :], 0.0)
                o = jnp.dot(qs.astype(BF16), S.astype(BF16), preferred_element_type=F32)
                rs = lax.rsqrt(jnp.mean(o * o, axis=-1, keepdims=True) + EPS)
                on = o * rs
                sr = _sigmoid(r)
                dg = dog_ref[rows, vs]
                gnh = gn_ref[:, vs]
                dp_ref[rows, 4 * QK + h * DV:4 * QK + (h + 1) * DV] = (
                    dg * on * gnh * (sr * (1.0 + r * (1.0 - sr)))).astype(BF16)
                dt = dg * (r * sr)
                dgn_ref[:, vs] += _rsum(dt * on)
                don = dt * gnh
                do = (rs * (don - on * jnp.mean(don * on, axis=-1, keepdims=True))).astype(BF16)
                Gc = gst[ks, :] + lax.dot_general(qs.astype(BF16), do, TN, preferred_element_type=F32)
                G16 = Gc.astype(BF16)
                dp_ref[rows, h * DK:(h + 1) * DK] = (
                    lax.dot_general(do, S.astype(BF16), NT, preferred_element_type=F32) * scale).astype(BF16)
                dkd = lax.dot_general(v.astype(BF16), G16, NT, preferred_element_type=F32)
                dp_ref[rows, 2 * QK + h * DV:2 * QK + (h + 1) * DV] = jnp.dot(
                    kd.astype(BF16), G16, preferred_element_type=F32).astype(BF16)
                gst[ks, :] = jnp.exp(_dot3(la[:, ks], ones, TN, "lhs")) * Gc
                ddec = _dot3(ones8, Gc * Sp, NT, "rhs")[0:1, :]
                dp_ref[rows, QK + h * DK:QK + (h + 1) * DK] = (dkd * wh).astype(BF16)
                dww = dkd * kd
                dge = jnp.exp(ge[:, ks]) * ddec + _rsum(dww)
                dla_heads.append(dge - _dot3(tri_u, dww, NN, "rhs"))
            dla = jnp.concatenate(dla_heads, axis=1)
            dgpre = dla * (1.0 / GATE_TAU) * (1.0 - _sigmoid(gpre))
            d16 = dgpre.astype(BF16)
            dp_ref[rows, 6 * QK:6 * QK + 128] = lax.dot_general(d16, wg_ref[...], NT, preferred_element_type=F32).astype(BF16)
            dwg_ref[...] += lax.dot_general(glr, d16, TN, preferred_element_type=F32)
            dbg_ref[...] += _rsum(dgpre)

    RB = NB * CHUNK
    rev = lambda i: (nB - 1 - i, 0)
    return pl.pallas_call(
        body, name=name, grid=(nB,),
        in_specs=[pl.BlockSpec((RB, W), rev), pl.BlockSpec((RB, 2 * QK), rev),
                  pl.BlockSpec((NB, QK, DV), lambda i: (nB - 1 - i, 0, 0)),
                  pl.BlockSpec((1, QK, DV), lambda i: (jnp.maximum(NB * (nB - 1 - i) - 1, 0), 0, 0)),
                  _whole(wg2p.shape), _whole(bg.shape), _whole(gn.shape)],
        out_specs=[pl.BlockSpec((RB, W), rev), _whole((128, QK)), _whole((1, QK)), _whole((1, 2 * QK))],
        out_shape=[jax.ShapeDtypeStruct((L, W), BF16), jax.ShapeDtypeStruct((128, QK), F32),
                   jax.ShapeDtypeStruct((1, QK), F32), jax.ShapeDtypeStruct((1, 2 * QK), F32)],
        scratch_shapes=[pltpu.VMEM((QK, DV), F32)],
        compiler_params=_params("arbitrary"),
    )(proj, dog, states, states, wg2p, bg, gn)


def _chunk_tri(rows, upper):
    r = lax.broadcasted_iota(jnp.int32, (rows, rows), 0)
    c = lax.broadcasted_iota(jnp.int32, (rows, rows), 1)
    same = (r // CHUNK) == (c // CHUNK)
    return (same & ((c >= r) if upper else (r >= c))).astype(BF16)


def _gla_block_gates(p_ref, wg_ref, bg_ref, QK, wbuf, gebuf):
    RB = p_ref.shape[0]
    glr = p_ref[:, 6 * QK:6 * QK + 128].astype(BF16)
    gpre = jnp.dot(glr, wg_ref[...], preferred_element_type=F32) + bg_ref[...]
    la = _log_sigmoid(gpre) * (1.0 / GATE_TAU)
    gc = _dot3(_chunk_tri(RB, False), la, NN, "rhs")
    for cc in range(RB // CHUNK):
        rows = slice(cc * CHUNK, (cc + 1) * CHUNK)
        ge = gc[(cc + 1) * CHUNK - 1:(cc + 1) * CHUNK, :]
        gebuf[cc:cc + 1, :] = ge
        wbuf[rows, :] = jnp.exp(ge - gc[rows, :])
    return glr, gpre, la


def _as_column(row, lanes):
    t = jnp.transpose(jnp.broadcast_to(row, (row.shape[1], row.shape[1])))
    return jnp.concatenate([t] * (lanes // row.shape[1]), axis=1)


def _as_row(col):
    return jnp.transpose(jnp.broadcast_to(col, (col.shape[0], col.shape[0])))[0:1, :]


def _gla_fwd2(proj, wg2p, bg, gn, *, name):
    L = proj.shape[0]
    QK = wg2p.shape[1]
    DK, DV = QK // HEADS, 2 * QK // HEADS
    nC = L // CHUNK
    NB = min(GLA_NB, nC)
    assert nC % NB == 0
    scale = DK ** -0.5

    def body(p_ref, wg_ref, bg_ref, gn_ref, og_ref, s_ref, sst, wbuf, gebuf):
        @pl.when(pl.program_id(0) == 0)
        def _():
            sst[...] = jnp.zeros_like(sst)

        _gla_block_gates(p_ref, wg_ref, bg_ref, QK, wbuf, gebuf)
        heads = range(HEADS)
        ks = [slice(h * DK, (h + 1) * DK) for h in heads]
        vs = [slice(h * DV, (h + 1) * DV) for h in heads]
        for cc in range(NB):
            rows = slice(cc * CHUNK, (cc + 1) * CHUNK)
            kv = [lax.dot_general((p_ref[rows, QK + h * DK:QK + (h + 1) * DK] * wbuf[rows, ks[h]]).astype(BF16),
                                  p_ref[rows, 2 * QK + h * DV:2 * QK + (h + 1) * DV].astype(BF16), TN,
                                  preferred_element_type=F32) for h in heads]
            S = []
            for h in heads:
                S.append(jnp.exp(_as_column(gebuf[cc:cc + 1, ks[h]], DV)) * sst[ks[h], :] + kv[h])
                sst[ks[h], :] = S[h]
                s_ref[cc, ks[h], :] = S[h]
            o = [jnp.dot((p_ref[rows, h * DK:(h + 1) * DK] * scale).astype(BF16), S[h].astype(BF16),
                         preferred_element_type=F32) for h in heads]
            for h in heads:
                r = p_ref[rows, 4 * QK + h * DV:4 * QK + (h + 1) * DV]
                on = o[h] * lax.rsqrt(jnp.mean(o[h] * o[h], axis=-1, keepdims=True) + EPS)
                og_ref[rows, vs[h]] = (on * gn_ref[:, vs[h]] * (r * _sigmoid(r))).astype(BF16)

    RB = NB * CHUNK
    return pl.pallas_call(
        body, name=name, grid=(nC // NB,),
        in_specs=[pl.BlockSpec((RB, proj.shape[1]), lambda i: (i, 0)), _whole(wg2p.shape), _whole(bg.shape), _whole(gn.shape)],
        out_specs=[pl.BlockSpec((RB, 2 * QK), lambda i: (i, 0)), pl.BlockSpec((NB, QK, DV), lambda i: (i, 0, 0))],
        out_shape=[jax.ShapeDtypeStruct((L, 2 * QK), BF16), jax.ShapeDtypeStruct((nC, QK, DV), F32)],
        scratch_shapes=[pltpu.VMEM((QK, DV), F32), pltpu.VMEM((RB, QK), F32), pltpu.VMEM((8, QK), F32)],
        compiler_params=_params("arbitrary"),
    )(proj, wg2p, bg, gn)


def _gla_bwd2(proj, dog, states, wg2p, bg, gn, *, name):
    L, W = proj.shape
    QK = wg2p.shape[1]
    DK, DV = QK // HEADS, 2 * QK // HEADS
    nC = L // CHUNK
    NB = min(GLA_NB, nC)
    nB = nC // NB
    scale = DK ** -0.5

    def body(p_ref, dog_ref, sc_ref, sp_ref, wg_ref, bg_ref, gn_ref, dp_ref, dwg_ref, dbg_ref, dgn_ref,
             gst, wbuf, gebuf, dwwbuf, dgebuf):
        i = pl.program_id(0)

        @pl.when(i == 0)
        def _():
            gst[...] = jnp.zeros_like(gst)
            dwg_ref[...] = jnp.zeros_like(dwg_ref)
            dbg_ref[...] = jnp.zeros_like(dbg_ref)
            dgn_ref[...] = jnp.zeros_like(dgn_ref)

        RB = NB * CHUNK
        glr, gpre, _ = _gla_block_gates(p_ref, wg_ref, bg_ref, QK, wbuf, gebuf)
        heads = range(HEADS)
        ks = [slice(h * DK, (h + 1) * DK) for h in heads]
        vs = [slice(h * DV, (h + 1) * DV) for h in heads]
        for cc in range(NB - 1, -1, -1):
            rows = slice(cc * CHUNK, (cc + 1) * CHUNK)
            qs16 = [(p_ref[rows, h * DK:(h + 1) * DK] * scale).astype(BF16) for h in heads]
            S16 = [sc_ref[cc, ks[h], :].astype(BF16) for h in heads]
            o = [jnp.dot(qs16[h], S16[h], preferred_element_type=F32) for h in heads]
            do = []
            for h in heads:
                r = p_ref[rows, 4 * QK + h * DV:4 * QK + (h + 1) * DV]
                rs = lax.rsqrt(jnp.mean(o[h] * o[h], axis=-1, keepdims=True) + EPS)
                on = o[h] * rs
                sr = _sigmoid(r)
                dg = dog_ref[rows, vs[h]]
                gnh = gn_ref[:, vs[h]]
                dp_ref[rows, 4 * QK + h * DV:4 * QK + (h + 1) * DV] = (
                    dg * on * gnh * (sr * (1.0 + r * (1.0 - sr)))).astype(BF16)
                dt = dg * (r * sr)
                dgn_ref[:, vs[h]] += _rsum(dt * on)
                don = dt * gnh
                do.append((rs * (don - on * jnp.mean(don * on, axis=-1, keepdims=True))).astype(BF16))
            Gc = [gst[ks[h], :] + lax.dot_general(qs16[h], do[h], TN, preferred_element_type=F32) for h in heads]
            G16 = [g.astype(BF16) for g in Gc]
            dq = [lax.dot_general(do[h], S16[h], NT, preferred_element_type=F32) for h in heads]
            kd = [p_ref[rows, QK + h * DK:QK + (h + 1) * DK] * wbuf[rows, ks[h]] for h in heads]
            dkd = [lax.dot_general(p_ref[rows, 2 * QK + h * DV:2 * QK + (h + 1) * DV].astype(BF16), G16[h], NT,
                                   preferred_element_type=F32) for h in heads]
            dv = [jnp.dot(kd[h].astype(BF16), G16[h], preferred_element_type=F32) for h in heads]
            for h in heads:
                if cc > 0:
                    Sp = sc_ref[cc - 1, ks[h], :]
                else:
                    Sp = jnp.where(i < nB - 1, sp_ref[0, ks[h], :], 0.0)
                dp_ref[rows, h * DK:(h + 1) * DK] = (dq[h] * scale).astype(BF16)
                dp_ref[rows, 2 * QK + h * DV:2 * QK + (h + 1) * DV] = dv[h].astype(BF16)
                ge = gebuf[cc:cc + 1, ks[h]]
                gst[ks[h], :] = jnp.exp(_as_column(ge, DV)) * Gc[h]
                ddec = _as_row(jnp.sum(Gc[h] * Sp, axis=1, keepdims=True))
                dp_ref[rows, QK + h * DK:QK + (h + 1) * DK] = (dkd[h] * wbuf[rows, ks[h]]).astype(BF16)
                dww = dkd[h] * kd[h]
                dwwbuf[rows, ks[h]] = dww
                dgebuf[cc:cc + 1, ks[h]] = jnp.exp(ge) * ddec + _rsum(dww)
        rev = _dot3(_chunk_tri(RB, True), dwwbuf[...], NN, "rhs")
        for cc in range(NB):
            rows = slice(cc * CHUNK, (cc + 1) * CHUNK)
            wbuf[rows, :] = dgebuf[cc:cc + 1, :] - rev[rows, :]
        dgpre = wbuf[...] * (1.0 / GATE_TAU) * (1.0 - _sigmoid(gpre))
        d16 = dgpre.astype(BF16)
        dp_ref[:, 6 * QK:6 * QK + 128] = lax.dot_general(d16, wg_ref[...], NT, preferred_element_type=F32).astype(BF16)
        dwg_ref[...] += lax.dot_general(glr, d16, TN, preferred_element_type=F32)
        dbg_ref[...] += _rsum(dgpre)

    RB = NB * CHUNK
    rev_idx = lambda i: (nB - 1 - i, 0)
    return pl.pallas_call(
        body, name=name, grid=(nB,),
        in_specs=[pl.BlockSpec((RB, W), rev_idx), pl.BlockSpec((RB, 2 * QK), rev_idx),
                  pl.BlockSpec((NB, QK, DV), lambda i: (nB - 1 - i, 0, 0)),
                  pl.BlockSpec((1, QK, DV), lambda i: (jnp.maximum(NB * (nB - 1 - i) - 1, 0), 0, 0)),
                  _whole(wg2p.shape), _whole(bg.shape), _whole(gn.shape)],
        out_specs=[pl.BlockSpec((RB, W), rev_idx), _whole((128, QK)), _whole((1, QK)), _whole((1, 2 * QK))],
        out_shape=[jax.ShapeDtypeStruct((L, W), BF16), jax.ShapeDtypeStruct((128, QK), F32),
                   jax.ShapeDtypeStruct((1, QK), F32), jax.ShapeDtypeStruct((1, 2 * QK), F32)],
        scratch_shapes=[pltpu.VMEM((QK, DV), F32), pltpu.VMEM((RB, QK), F32), pltpu.VMEM((8, QK), F32),
                        pltpu.VMEM((RB, QK), F32), pltpu.VMEM((8, QK), F32)],
        compiler_params=_params("arbitrary"),
    )(proj, dog, states, states, wg2p, bg, gn)


def _coords():
    return lax.axis_index("x"), lax.axis_index("y"), lax.axis_index("c")


def _other_chips(x, y):
    return [(1 - x, y, 2 * (1 - x) + y), (x, 1 - y, 2 * x + 1 - y), (1 - x, 1 - y, 2 * (1 - x) + 1 - y)]


def _hbm_call(body, ins, out_shapes, n_sems, *, name, alias=False):
    any_spec = pl.BlockSpec(memory_space=pl.ANY)
    return pl.pallas_call(
        body, name=name, in_specs=[any_spec] * len(ins), out_specs=[any_spec] * len(out_shapes), out_shape=out_shapes,
        scratch_shapes=[pltpu.SemaphoreType.DMA((n,)) for n in n_sems],
        input_output_aliases={k: k for k in range(len(ins))} if alias else {},
    )(*ins)


def _exchange(src, masks, *, name):
    vary = [any(m[k] for m in masks) for k in range(3)]
    nslots = 2 ** sum(vary)
    n = len(masks)

    def slot(coords):
        s = 0
        for k in range(3):
            if vary[k]:
                s = s * 2 + coords[k]
        return s

    def body(src_ref, dst_ref, send_sems, recv_sems, loc_sem):
        me = _coords()
        mine = slot(me)
        loc = pltpu.make_async_copy(src_ref, dst_ref.at[mine], loc_sem.at[0])
        loc.start()
        copies = []
        for k, m in enumerate(masks):
            peer = tuple(1 - me[d] if m[d] else me[d] for d in range(3))
            cp = pltpu.make_async_remote_copy(src_ref=src_ref, dst_ref=dst_ref.at[mine], send_sem=send_sems.at[k],
                                              recv_sem=recv_sems.at[k], device_id=peer, device_id_type=MESH)
            cp.start()
            copies.append(cp)
        for cp in copies:
            cp.wait()
        loc.wait()

    return _hbm_call(body, [src], [jax.ShapeDtypeStruct((nslots,) + tuple(src.shape), src.dtype)], (n, n, 1), name=name)[0]


def _cast_into(t, lead, kind, chip, *, name, tm=256):
    r, cc = t.shape[-2:]
    tm = min(tm, r)
    nblk = r // tm
    if kind == "col":
        shp, o_spec = (r, NCH * cc), pl.BlockSpec((tm, cc), lambda i, s: (i, s[0]))
    elif kind == "row":
        shp, o_spec = (NCH * r, cc), pl.BlockSpec((tm, cc), lambda i, s: (s[0] * nblk + i, 0))
    else:
        shp, o_spec = (NCH, r, cc), pl.BlockSpec((None, tm, cc), lambda i, s: (s[0], i, 0))

    def body(s_ref, t_ref, o_ref):
        o_ref[...] = t_ref[...].astype(o_ref.dtype)

    return pl.pallas_call(
        body, name=name,
        grid_spec=pltpu.PrefetchScalarGridSpec(
            num_scalar_prefetch=1, grid=(nblk,),
            in_specs=[pl.BlockSpec((None, tm, cc), lambda i, s: (lead, i, 0))], out_specs=o_spec),
        out_shape=jax.ShapeDtypeStruct(shp, BF16), compiler_params=_params("parallel"),
    )(chip.reshape(1).astype(jnp.int32), t)


def _gather_weights(arrs, shard_shapes, kinds, *, name):
    n = len(arrs)

    def body(*refs):
        dst = refs[n:2 * n]
        send_sems, recv_sems = refs[2 * n:]
        x, y, c = _coords()
        chip = 2 * x + y
        others = _other_chips(x, y)
        sib = (x, y, 1 - c)

        def window(p, chip_id, cc):
            r, cols = shard_shapes[p]
            h = r // 2
            if kinds[p] == "col":
                return dst[p].at[pl.ds(cc * h, h), pl.ds(pl.multiple_of(chip_id * cols, 128), cols)]
            if kinds[p] == "row":
                return dst[p].at[pl.ds(chip_id * r + cc * h, h), :]
            return dst[p].at[chip_id, pl.ds(cc * h, h), :]

        def copy(p, k, win, to):
            return pltpu.make_async_remote_copy(src_ref=win, dst_ref=win, send_sem=send_sems.at[6 * p + k],
                                                recv_sem=recv_sems.at[6 * p + k], device_id=to, device_id_type=MESH)

        sends = []
        for p in range(n):
            for j, (ox, oy, _) in enumerate(others):
                cp = copy(p, j, window(p, chip, c), (ox, oy, c))
                cp.start()
                sends.append(cp)
        for j, (_, _, oc) in enumerate(others):
            for p in range(n):
                copy(p, j, window(p, oc, c), (x, y, c)).wait_recv()
                fw = copy(p, 3 + j, window(p, oc, c), sib)
                fw.start()
                sends.append(fw)
        for p in range(n):
            for j, (_, _, oc) in enumerate(others):
                copy(p, 3 + j, window(p, oc, 1 - c), sib).wait_recv()
        for cp in sends:
            cp.wait_send()

    outs = [jax.ShapeDtypeStruct(a.shape, a.dtype) for a in arrs]
    return _hbm_call(body, arrs, outs, (6 * n, 6 * n), name=name, alias=True)


HBM_SPEC = pl.BlockSpec(memory_space=pltpu.HBM)
SEM_SPEC = pl.BlockSpec(memory_space=pltpu.SEMAPHORE)
EFFECT = pltpu.SideEffectType.DATAFLOW_SIDE_EFFECTING


def _window(ref, shard_shape, kind, chip_id, cc):
    r, cols = shard_shape
    h = r // 2
    if kind == "col":
        return ref.at[pl.ds(cc * h, h), pl.ds(pl.multiple_of(chip_id * cols, 128), cols)]
    if kind == "row":
        return ref.at[pl.ds(chip_id * r + cc * h, h), :]
    return ref.at[chip_id, pl.ds(cc * h, h), :]


def _split_start(start, arrs, n_sems, *, name):
    n, ns = len(arrs), len(n_sems)

    def body(*refs):
        start(refs[:n], refs[n:n + ns])
        refs[-1][...] = jnp.zeros_like(refs[-1])

    outs = pl.pallas_call(
        body, name=name,
        out_shape=tuple([pltpu.SemaphoreType.DMA((k,)) for k in n_sems] + [pltpu.HBM(a.shape, a.dtype) for a in arrs]
                        + [jax.ShapeDtypeStruct((8, 128), F32)]),
        in_specs=[HBM_SPEC] * n, out_specs=tuple([SEM_SPEC] * ns + [HBM_SPEC] * n + [pl.BlockSpec(memory_space=pltpu.VMEM)]),
        input_output_aliases={k: ns + k for k in range(n)},
        compiler_params=pltpu.CompilerParams(has_side_effects=EFFECT),
    )(*[pltpu.with_memory_space_constraint(a, pltpu.HBM) for a in arrs])
    return list(outs[:ns]), list(outs[ns:ns + n]), outs[-1]


def _split_wait(wait, arrs, sems, after, *, name):
    n, ns = len(arrs), len(sems)

    def body(*refs):
        wait(refs[:n], refs[n:n + ns])

    return pl.pallas_call(
        body, name=name, out_shape=tuple(pltpu.HBM(a.shape, a.dtype) for a in arrs),
        in_specs=[HBM_SPEC] * n + [SEM_SPEC] * ns + [pl.BlockSpec(memory_space=pl.ANY)], out_specs=tuple([HBM_SPEC] * n),
        input_output_aliases={k: k for k in range(n)},
        compiler_params=pltpu.CompilerParams(has_side_effects=EFFECT),
    )(*arrs, *sems, after)


def _gw_copies(refs, send_sems, recv_sems, shard_shapes, kinds, outgoing):
    x, y, c = _coords()
    chip = 2 * x + y
    out = []
    for p in range(len(refs)):
        for j, (ox, oy, oc) in enumerate(_other_chips(x, y)):
            win = _window(refs[p], shard_shapes[p], kinds[p], chip if outgoing else oc, c)
            out.append(pltpu.make_async_remote_copy(
                src_ref=win, dst_ref=win, send_sem=send_sems.at[3 * p + j], recv_sem=recv_sems.at[3 * p + j],
                device_id=(ox, oy, c), device_id_type=MESH))
    return out


def _gw_start(arrs, shard_shapes, kinds, groups, *, name):
    def start(refs, sems):
        for g, idx in enumerate(groups):
            for cp in _gw_copies([refs[p] for p in idx], sems[2 * g], sems[2 * g + 1], [shard_shapes[p] for p in idx],
                                 [kinds[p] for p in idx], True):
                cp.start()

    n_sems = [3 * len(idx) for idx in groups for _ in range(2)]
    sems, thru, token = _split_start(start, arrs, n_sems, name=name)
    return [(sems[2 * g], sems[2 * g + 1]) for g in range(len(groups))], thru, token


def _gw_wait(arrs, shard_shapes, kinds, sem_pair, after, *, name):
    def wait(refs, sems):
        for cp in _gw_copies(refs, sems[0], sems[1], shard_shapes, kinds, True):
            cp.wait_send()
        for cp in _gw_copies(refs, sems[0], sems[1], shard_shapes, kinds, False):
            cp.wait_recv()

    return _split_wait(wait, arrs, list(sem_pair), after, name=name)


def _gw_forward(arrs, shard_shapes, kinds, *, name):
    n = len(arrs)

    def body(*refs):
        dst = refs[n:2 * n]
        send_sems, recv_sems = refs[2 * n:]
        x, y, c = _coords()
        sends = []
        for p in range(n):
            for j, (_, _, oc) in enumerate(_other_chips(x, y)):
                win = _window(dst[p], shard_shapes[p], kinds[p], oc, c)
                cp = pltpu.make_async_remote_copy(src_ref=win, dst_ref=win, send_sem=send_sems.at[3 * p + j],
                                                  recv_sem=recv_sems.at[3 * p + j], device_id=(x, y, 1 - c),
                                                  device_id_type=MESH)
                cp.start()
                sends.append(cp)
        for p in range(n):
            for j, (_, _, oc) in enumerate(_other_chips(x, y)):
                win = _window(dst[p], shard_shapes[p], kinds[p], oc, 1 - c)
                pltpu.make_async_remote_copy(src_ref=win, dst_ref=win, send_sem=send_sems.at[3 * p + j],
                                             recv_sem=recv_sems.at[3 * p + j], device_id=(x, y, 1 - c),
                                             device_id_type=MESH).wait_recv()
        for cp in sends:
            cp.wait_send()

    outs = [jax.ShapeDtypeStruct(a.shape, a.dtype) for a in arrs]
    return _hbm_call(body, arrs, outs, (3 * n, 3 * n), name=name, alias=True)


def _rs_chips_copies(parts, lands, send_sems, recv_sems):
    x, y, c = _coords()
    chip = 2 * x + y
    out = []
    for p in range(len(parts)):
        for j, (ox, oy, oc) in enumerate(_other_chips(x, y)):
            out.append(pltpu.make_async_remote_copy(
                src_ref=parts[p].at[oc], dst_ref=lands[p].at[chip], send_sem=send_sems.at[3 * p + j],
                recv_sem=recv_sems.at[3 * p + j], device_id=(ox, oy, c), device_id_type=MESH))
    return out


def _rs_chips_start(parts, *, name):
    n = len(parts)

    def start(refs, sems):
        for cp in _rs_chips_copies(refs[:n], refs[n:], sems[0], sems[1]):
            cp.start()

    lands = [lax.empty(t.shape, t.dtype) for t in parts]
    sems, thru, token = _split_start(start, list(parts) + lands, [3 * n, 3 * n], name=name)
    return (sems[0], sems[1]), thru[:n], thru[n:], token


def _rs_chips_wait(groups, after, *, name):
    sizes = [len(g[1]) for g in groups]
    arrs = [a for g in groups for a in list(g[1]) + list(g[2])]
    sems = [s for g in groups for s in g[0]]

    def wait(refs, sem_refs):
        o = 0
        for k, n in enumerate(sizes):
            for cp in _rs_chips_copies(refs[o:o + n], refs[o + n:o + 2 * n], sem_refs[2 * k], sem_refs[2 * k + 1]):
                cp.wait()
            o += 2 * n

    outs = _split_wait(wait, arrs, sems, after, name=name)
    res, o = [], 0
    for n in sizes:
        res.append((list(outs[o:o + n]), list(outs[o + n:o + 2 * n])))
        o += 2 * n
    return res


def _rs_cores_copies(grads, lands, send_sems, recv_sems):
    x, y, c = _coords()
    out, o = [], 0
    for p in range(len(grads)):
        nsh, h = lands[p].shape[0], lands[p].shape[1]
        for j in range(nsh):
            out.append(pltpu.make_async_remote_copy(
                src_ref=grads[p].at[j, pl.ds((1 - c) * h, h), :], dst_ref=lands[p].at[j],
                send_sem=send_sems.at[o + j], recv_sem=recv_sems.at[o + j], device_id=(x, y, 1 - c), device_id_type=MESH))
        o += nsh
    return out


def _rs_cores_start(grads, *, name):
    n = len(grads)
    tot = sum(g.shape[0] for g in grads)

    def start(refs, sems):
        for cp in _rs_cores_copies(refs[:n], refs[n:], sems[0], sems[1]):
            cp.start()

    lands = [lax.empty((g.shape[0], g.shape[1] // 2, g.shape[2]), g.dtype) for g in grads]
    sems, thru, token = _split_start(start, list(grads) + lands, [tot, tot], name=name)
    return (sems[0], sems[1]), thru[:n], thru[n:], token


def _rs_cores_wait(pair, grads, lands, after, *, name):
    n = len(grads)

    def wait(refs, sems):
        for cp in _rs_cores_copies(refs[:n], refs[n:], sems[0], sems[1]):
            cp.wait()

    outs = _split_wait(wait, list(grads) + list(lands), list(pair), after, name=name)
    return list(outs[:n]), list(outs[n:])


def _rs_cores(grads, *, name):
    n = len(grads)
    outs = [jax.ShapeDtypeStruct((g.shape[0], g.shape[1] // 2, g.shape[2]), g.dtype) for g in grads]

    def body(*refs):
        src, dst = refs[:n], refs[n:2 * n]
        send_sems, recv_sems = refs[2 * n:]
        x, y, c = _coords()
        copies = []
        for p in range(n):
            nsh, r, _ = grads[p].shape
            h = r // 2
            for j in range(nsh):
                cp = pltpu.make_async_remote_copy(
                    src_ref=src[p].at[j, pl.ds((1 - c) * h, h), :], dst_ref=dst[p].at[j],
                    send_sem=send_sems.at[nsh * p + j], recv_sem=recv_sems.at[nsh * p + j],
                    device_id=(x, y, 1 - c), device_id_type=MESH)
                cp.start()
                copies.append(cp)
        for cp in copies:
            cp.wait()

    tot = sum(g.shape[0] for g in grads)
    return _hbm_call(body, grads, outs, (tot, tot), name=name)


def _sum_own_half(full, recv, ci, out_dtype, *, name):
    nsh, h, cols = recv.shape
    tm = h if nsh * h * cols * 4 <= (2 << 20) else _tile_rows(h, 256)
    nblk = h // tm

    def body(c_ref, f_ref, r_ref, o_ref):
        o_ref[...] = (f_ref[...] + r_ref[...]).astype(o_ref.dtype)

    return pl.pallas_call(
        body, name=name,
        grid_spec=pltpu.PrefetchScalarGridSpec(
            num_scalar_prefetch=1, grid=(nsh, nblk),
            in_specs=[pl.BlockSpec((1, tm, cols), lambda j, i, c_ref: (j, c_ref[0] * nblk + i, 0)),
                      pl.BlockSpec((1, tm, cols), lambda j, i, c_ref: (j, i, 0))],
            out_specs=pl.BlockSpec((1, tm, cols), lambda j, i, c_ref: (j, i, 0))),
        out_shape=jax.ShapeDtypeStruct((nsh, h, cols), out_dtype), compiler_params=_params("parallel", "parallel"),
    )(ci.reshape(1).astype(jnp.int32), full, recv)


def _rs_chips(parts, *, name):
    n = len(parts)
    outs = [jax.ShapeDtypeStruct(t.shape, t.dtype) for t in parts]

    def body(*refs):
        src, dst = refs[:n], refs[n:2 * n]
        send_sems, recv_sems = refs[2 * n:]
        x, y, c = _coords()
        chip = 2 * x + y
        copies = []
        for p in range(n):
            for j, (ox, oy, oc) in enumerate(_other_chips(x, y)):
                cp = pltpu.make_async_remote_copy(
                    src_ref=src[p].at[oc], dst_ref=dst[p].at[chip], send_sem=send_sems.at[3 * p + j],
                    recv_sem=recv_sems.at[3 * p + j], device_id=(ox, oy, c), device_id_type=MESH)
                cp.start()
                copies.append(cp)
        for cp in copies:
            cp.wait()

    return _hbm_call(body, parts, outs, (3 * n, 3 * n), name=name)


def _sum_chips(recv, own, chip, ci, *, name, nlead=1, lead=0, prev=None, spread=False):
    nsh, h, cols = recv.shape
    tm = h if nsh * h * cols * 4 <= (2 << 20) else _tile_rows(h, 256)
    nblk = h // tm
    rows_out = 2 * h * (nsh if spread else 1)

    def body(s_ref, r_ref, o_ref, *rest):
        out_ref = rest[-1]
        t = None
        for s in range(nsh):
            v = jnp.where(s_ref[0] == s, o_ref[s], r_ref[s]).astype(F32)
            t = v if t is None else t + v
        out_ref[...] = t

    def out_idx(i, s):
        return (lead, (s[0] * 2 * nblk if spread else 0) + s[1] * nblk + i, 0)

    blk = pl.BlockSpec((nsh, tm, cols), lambda i, s: (0, i, 0))
    ins = [recv, own] + ([prev] if prev is not None else [])
    return pl.pallas_call(
        body, name=name,
        grid_spec=pltpu.PrefetchScalarGridSpec(
            num_scalar_prefetch=1, grid=(nblk,),
            in_specs=[blk, blk] + ([pl.BlockSpec(memory_space=pl.ANY)] if prev is not None else []),
            out_specs=pl.BlockSpec((None, tm, cols), out_idx)),
        out_shape=jax.ShapeDtypeStruct((nlead, rows_out, cols), F32),
        input_output_aliases={3: 0} if prev is not None else {},
        compiler_params=_params("arbitrary"),
    )(jnp.stack([chip, ci]).astype(jnp.int32), *ins)


def _rs_gather(arrs, halves, spread, *, name, nchunk=4):
    n = len(arrs)
    per = [a.shape[0] * nchunk for a in arrs]
    offs = [sum(per[:p]) for p in range(n)]

    def body(*refs):
        dst = refs[n:2 * n]
        send_sems, recv_sems = refs[2 * n:]
        x, y, c = _coords()
        chip = 2 * x + y
        copies = []
        for p in range(n):
            h = halves[p]
            q = h // nchunk
            base = chip * 2 * h if spread[p] else 0
            for l in range(arrs[p].shape[0]):
                for k in range(nchunk):
                    win = dst[p].at[l, pl.ds(base + c * h + k * q, q), :]
                    sem = offs[p] + l * nchunk + k
                    cp = pltpu.make_async_remote_copy(src_ref=win, dst_ref=win, send_sem=send_sems.at[sem],
                                                      recv_sem=recv_sems.at[sem], device_id=(x, y, 1 - c),
                                                      device_id_type=MESH)
                    cp.start()
                    copies.append(cp)
        for cp in copies:
            cp.wait_send()
        for p in range(n):
            h = halves[p]
            q = h // nchunk
            base = chip * 2 * h if spread[p] else 0
            for l in range(arrs[p].shape[0]):
                for k in range(nchunk):
                    win = dst[p].at[l, pl.ds(base + (1 - c) * h + k * q, q), :]
                    sem = offs[p] + l * nchunk + k
                    pltpu.make_async_remote_copy(src_ref=win, dst_ref=win, send_sem=send_sems.at[sem],
                                                 recv_sem=recv_sems.at[sem], device_id=(x, y, 1 - c),
                                                 device_id_type=MESH).wait_recv()

    outs = [jax.ShapeDtypeStruct(a.shape, a.dtype) for a in arrs]
    return _hbm_call(body, arrs, outs, (sum(per), sum(per)), name=name, alias=True)


def _adamw(w, g, m, v, *, name):
    nl, R, C = w.shape
    tm = _tile_rows(R, 256)

    blk = pl.BlockSpec((None, tm, C), lambda l, i: (l, i, 0))
    return pl.pallas_call(
        _adamw_body_copy(), name=name, grid=(nl, R // tm), in_specs=[blk] * 4, out_specs=[blk] * 3,
        out_shape=[jax.ShapeDtypeStruct((nl, R, C), F32)] * 3, compiler_params=_params("parallel", "parallel"),
    )(w, g, m, v)


def _adamw_body(w_ref, g_ref, m_ref, v_ref, d_ref, nm_ref, nv_ref):
    gg = g_ref[...]
    nm = B1 * m_ref[...] + (1.0 - B1) * gg
    nv = B2 * v_ref[...] + (1.0 - B2) * (gg * gg)
    m_hat = nm / (1.0 - B1 ** ASTEP)
    v_hat = nv / (1.0 - B2 ** ASTEP)
    d_ref[...] = -LR * (m_hat / (jnp.sqrt(v_hat) + AEPS) + WD * w_ref[...])
    nm_ref[...] = nm
    nv_ref[...] = nv


def _adamw_whole(w, g, m, v, *, name):
    return pl.pallas_call(_adamw_body_copy(), name=name, out_shape=[jax.ShapeDtypeStruct(w.shape, F32)] * 3,
                          compiler_params=_params())(w, g, m, v)


def _adamw_body_copy():
    def body(*refs):
        _adamw_body(*refs)
    return body


def _mod_cols(c_all, w_ada, b_cols, *, name):
    nl, D, cols = w_ada.shape
    B = c_all.shape[0]

    def body(c_ref, w_ref, b_ref, o_ref):
        cc = c_ref[...]
        cs = (cc * _sigmoid(cc)).astype(BF16)
        o_ref[0] = jnp.dot(cs, w_ref[0].astype(BF16), preferred_element_type=F32) + b_ref[0]

    return pl.pallas_call(
        body, name=name, grid=(nl,),
        in_specs=[_whole(c_all.shape), pl.BlockSpec((1, D, cols), lambda i: (i, 0, 0)), pl.BlockSpec((1, 1, cols), lambda i: (i, 0, 0))],
        out_specs=pl.BlockSpec((1, B, cols), lambda i: (i, 0, 0)),
        out_shape=jax.ShapeDtypeStruct((nl, B, cols), F32), compiler_params=_params("arbitrary"),
    )(c_all, w_ada, b_cols)


def _ada_grad(c_all, dmod_cols, *, name):
    nl, B, cols = dmod_cols.shape
    D = c_all.shape[1]

    def body(c_ref, d_ref, o_ref):
        cc = c_ref[...]
        cs = (cc * _sigmoid(cc)).astype(BF16)
        o_ref[0] = lax.dot_general(cs, d_ref[0].astype(BF16), TN, preferred_element_type=F32)

    return pl.pallas_call(
        body, name=name, grid=(nl,),
        in_specs=[_whole(c_all.shape), pl.BlockSpec((1, B, cols), lambda i: (i, 0, 0))],
        out_specs=pl.BlockSpec((1, D, cols), lambda i: (i, 0, 0)),
        out_shape=jax.ShapeDtypeStruct((nl, D, cols), F32), compiler_params=_params("arbitrary"),
    )(c_all, dmod_cols)


def _s5_disc(a_re, a_im, log_dt, b_re, b_im):
    dt = jnp.exp(log_dt)[:, None]
    mag = jnp.exp(a_re * dt)
    ph = a_im * dt
    lb_re = mag * jnp.cos(ph)
    lb_im = mag * jnp.sin(ph)
    den = a_re * a_re + a_im * a_im
    nr = lb_re - 1.0
    ni = lb_im
    f_re = (nr * a_re + ni * a_im) / den
    f_im = (ni * a_re - nr * a_im) / den
    bb_re = f_re[..., None] * b_re - f_im[..., None] * b_im
    bb_im = f_re[..., None] * b_im + f_im[..., None] * b_re
    return lb_re, lb_im, bb_re, bb_im


def _to_segments(t):
    L, D = t.shape
    R = min(S5_R, L)
    return t.reshape(L // R, NSEG, R // NSEG, D).transpose(0, 2, 1, 3).reshape(L, D)


def _from_segments(t):
    L, D = t.shape
    R = min(S5_R, L)
    return t.reshape(L // R, R // NSEG, NSEG, D).transpose(0, 2, 1, 3).reshape(L, D)


def _mlp_fwd(h2, w1, w2, tag):
    a = _matmul(h2, w1, name=f"ff1_{tag}", out_dtypes=(BF16,), epi=lambda acc: (jnp.maximum(acc, 0.0),))
    f = _matmul(a, w2, name=f"ff2_{tag}", a_fn=jnp.square)
    return a, f


def _mlp_bwd(df, h2, a, w1, w2, tag):
    da = _matmul(df, w2, tb=True, name=f"ff2_dx_{tag}", out_dtypes=(BF16,), epi_ins=(a,),
                 epi=lambda acc, at: (acc * (2.0 * at.astype(F32)),))
    dw2 = _matmul(a, df, ta=True, name=f"ff2_dw_{tag}", a_fn=jnp.square)
    dh2 = _matmul(da, w1, tb=True, name=f"ff1_dx_{tag}")
    dw1 = _matmul(h2, da, ta=True, name=f"ff1_dw_{tag}", col_shards=NCH)
    return dh2, dw1, dw2


def kernel(x, c, w_ada, b_ada, norm_mix, norm_mlp, s5_a_re, s5_a_im, s5_log_dt, s5_b_re, s5_b_im, s5_c_re, s5_c_im, s5_d, s5_w_glu, gla_w_in, gla_w_gate2, gla_b_gate, gla_g_norm, gla_w_out, w_ff1, w_ff2, norm_final, loss_target, m_w_ada, m_b_ada, m_norm_mix, m_norm_mlp, m_s5_a_re, m_s5_a_im, m_s5_log_dt, m_s5_b_re, m_s5_b_im, m_s5_c_re, m_s5_c_im, m_s5_d, m_s5_w_glu, m_gla_w_in, m_gla_w_gate2, m_gla_b_gate, m_gla_g_norm, m_gla_w_out, m_w_ff1, m_w_ff2, m_norm_final, v_w_ada, v_b_ada, v_norm_mix, v_norm_mlp, v_s5_a_re, v_s5_a_im, v_s5_log_dt, v_s5_b_re, v_s5_b_im, v_s5_c_re, v_s5_c_im, v_s5_d, v_s5_w_glu, v_gla_w_in, v_gla_w_gate2, v_gla_b_gate, v_gla_g_norm, v_gla_w_out, v_w_ff1, v_w_ff2, v_norm_final):
    args = dict(locals())
    L, D = x.shape[1], x.shape[2]
    QK = D // 2
    xi, yi, ci = _coords()
    chip = 2 * xi + yi
    dev = 2 * chip + ci

    cat = jnp.concatenate([gla_w_gate2[0].reshape(1, -1), gla_b_gate, gla_g_norm], axis=1)
    first = _exchange(jnp.concatenate([c.reshape(8, D // 8), jnp.tile(cat, (8, 1))], axis=1), MASK_ALL, name="gather_c")
    c_all = first[:, :, :D // 8].reshape(8, D)
    cat_all = first[0::2, 0, D // 8:]
    acols = w_ada.shape[2]
    b_cols = lax.dynamic_slice_in_dim(b_ada, chip * acols, acols, axis=1)[:, None, :]
    mod_cols = _mod_cols(c_all, w_ada, b_cols, name="ada_mod")
    mod_all = _exchange(mod_cols.reshape(16, acols), MASK_CHIPS, name="gather_mod")
    mod_all = mod_all.reshape(NCH, 2, 8, acols).transpose(1, 2, 0, 3).reshape(2, 8, NCH * acols)
    mod = lax.dynamic_index_in_dim(mod_all, dev, axis=1, keepdims=False).reshape(2, 6, 1, D)

    big = [("s5_w_glu", s5_w_glu, 0, "col"), ("gla_w_in", gla_w_in, 0, "slot"), ("gla_w_out", gla_w_out, 0, "row"),
           ("w_ff1_0", w_ff1, 0, "col"), ("w_ff1_1", w_ff1, 1, "col"), ("w_ff2_0", w_ff2, 0, "row"), ("w_ff2_1", w_ff2, 1, "row")]
    own16 = [_cast_into(t, lead, kind, chip, name=f"cast_{nm}") for nm, t, lead, kind in big]
    wshapes, wkinds = [b[1].shape[-2:] for b in big], [b[3] for b in big]
    wgroups = [[0, 3, 5], [1, 2, 4, 6]]
    wsems, wthru, wtoken = _gw_start(own16, wshapes, wkinds, wgroups, name="gather_w_start")
    W = {}

    def finish_weights(g, after):
        idx = wgroups[g]
        shp, knd = [wshapes[p] for p in idx], [wkinds[p] for p in idx]
        got = _gw_wait([wthru[p] for p in idx], shp, knd, wsems[g], after, name=f"gather_w_wait{g}")
        for p, w in zip(idx, _gw_forward(got, shp, knd, name=f"gather_w_cores{g}")):
            W[big[p][0]] = w

    qk4 = QK // NCH
    wg2 = cat_all[:, :GATE_RANK * qk4].reshape(NCH, GATE_RANK, qk4).transpose(1, 0, 2).reshape(GATE_RANK, QK)
    bg = cat_all[:, GATE_RANK * qk4:(GATE_RANK + 1) * qk4].reshape(1, QK)
    gn = cat_all[:, (GATE_RANK + 1) * qk4:].reshape(1, D)
    wg2p = jnp.concatenate([wg2, jnp.zeros((128 - GATE_RANK, QK), F32)], axis=0).astype(BF16)

    lb_re, lb_im, bb_re, bb_im = _s5_disc(s5_a_re[0], s5_a_im[0], s5_log_dt[0], s5_b_re[0], s5_b_im[0])
    tb = _s5_tables(lb_re, lb_im, bb_re, bb_im, s5_c_re[0], s5_c_im[0], min(S5_R, L) // NSEG)
    s5_dv = s5_d + wtoken[0, 0]

    def vec(t):
        return t.reshape(1, -1)

    m0, m1 = mod[0], mod[1]
    xp = _to_segments(x[0])
    (u0,) = _rows(lambda t, g, sc, sh: (_norm_mod(t, g, sc, sh),), [xp], [vec(norm_mix[0]), m0[1], m0[0]],
                  [(D, F32)], [], name="pre_mix0")
    y0, z0, ck0 = _s5_fwd2(u0, tb, s5_dv, name="s5_fwd")
    finish_weights(0, z0)
    vg0 = _matmul(z0, W["s5_w_glu"], name="glu")

    def res_glu_pre(xt, vgt, gt, g, sc, sh):
        xn = xt + gt * (vgt[:, :D] * _sigmoid(vgt[:, D:]))
        return xn, _norm_mod(xn, g, sc, sh)

    x2_0, h2_0 = _rows(res_glu_pre, [xp, vg0], [m0[2], vec(norm_mlp[0]), m0[4], m0[3]], [(D, F32), (D, BF16)], [],
                       name="res_mix0")
    a_0, f0 = _mlp_fwd(h2_0, W["w_ff1_0"], W["w_ff2_0"], "0")

    def res_pre(xt, bt, gt, g, sc, sh):
        xn = xt + gt * bt
        return xn, _norm_mod(xn, g, sc, sh)

    x3p, h1p = _rows(res_pre, [x2_0, f0], [m0[5], vec(norm_mix[1]), m1[1], m1[0]], [(D, F32), (D, BF16)], [],
                     name="res_mlp0")
    x3, h1 = _from_segments(x3p), _from_segments(h1p)
    finish_weights(1, f0)
    w_in = W["gla_w_in"].transpose(1, 0, 2).reshape(D, -1)
    w_in_r = jnp.concatenate([w_in[:, :4 * QK], w_in[:, 4 * QK + GATE_RANK:], w_in[:, 4 * QK:4 * QK + GATE_RANK],
                              jnp.zeros((D, 128 - GATE_RANK), BF16)], axis=1)
    proj = _matmul(h1, w_in_r, name="gla_in", tn=640)
    og, states = _gla_fwd2(proj, wg2p, bg, gn, name="gla_fwd")
    ymix = _matmul(og, W["gla_w_out"], name="gla_out")
    x2_1, h2_1 = _rows(res_pre, [x3, ymix], [m1[2], vec(norm_mlp[1]), m1[4], m1[3]], [(D, F32), (D, BF16)], [],
                       name="res_mix1")
    a_1, f1 = _mlp_fwd(h2_1, W["w_ff1_1"], W["w_ff2_1"], "1")

    def final(xt, ft, tgt, gt, g):
        xn = xt + gt * ft
        rs = lax.rsqrt(jnp.mean(xn * xn, axis=-1, keepdims=True) + EPS)
        xh = xn * rs
        e = xh * g - tgt
        dout = e * (1.0 / D)
        dxh = dout * g
        dx = rs * (dxh - xh * jnp.mean(dxh * xh, axis=-1, keepdims=True))
        lsum = 0.5 * jnp.sum(jnp.sum(e * e, axis=-1, keepdims=True), axis=0, keepdims=True) * (1.0 / D)
        return dx, dx * gt, jnp.broadcast_to(lsum, (1, 128)), _rsum(dout * xh), _rsum(dx * ft)

    dx, df1, loss_part, d_norm_final, dgt2_1 = _rows(
        final, [x2_1, f1, loss_target[0]], [m1[5], vec(norm_final)], [(D, F32), (D, BF16)],
        [(1, 128), (1, D), (1, D)], name="loss_head")
    loss = lax.psum(loss_part[0, 0], ("x", "y", "c"))

    def gate_bwd(dxt, bt, gt):
        return dxt * gt, _rsum(dxt * bt)

    def norm_bwd(xt, dht, drt, g, sc):
        dxn, dsh, dsc, dg = _norm_mod_bwd(xt, dht, g, sc)
        return drt + dxn, dsh, dsc, dg

    def norm_gate_bwd(xt, dht, drt, bt, g, sc, gt):
        dxn, dsh, dsc, dg = _norm_mod_bwd(xt, dht, g, sc)
        dxt = drt + dxn
        return dxt, dxt * gt, dsh, dsc, dg, _rsum(dxt * bt)

    vD = [(1, D)]
    dh2_1, dw_ff1_1, dw_ff2_1 = _mlp_bwd(df1, h2_1, a_1, W["w_ff1_1"], W["w_ff2_1"], "1")
    dx, dmix1, dsh2_1, dsc2_1, dg_mlp1, dgt1_1 = _rows(
        norm_gate_bwd, [x2_1, dh2_1, dx, ymix], [vec(norm_mlp[1]), m1[4], m1[2]], [(D, F32), (D, BF16)], vD * 4,
        name="norm_mlp1_bwd")
    dog = _matmul(dmix1, W["gla_w_out"], tb=True, name="gla_out_dx")
    dw_out = _matmul(og, dmix1, ta=True, name="gla_out_dw")
    dproj, dwg2p, dbg, dgn = _gla_bwd2(proj, dog, states, wg2p, bg, gn, name="gla_bwd")
    dh1 = _matmul(dproj, w_in_r, tb=True, name="gla_in_dx", tk=3200)
    dw_in_r = _matmul(h1, dproj, ta=True, name="gla_in_dw", tn=640)
    dx, dsh1_1, dsc1_1, dg_mix1 = _rows(norm_bwd, [x3, dh1, dx], [vec(norm_mix[1]), m1[1]], [(D, F32)], vD * 3,
                                        name="norm_mix1_bwd")
    dxp = _to_segments(dx)
    tags = [b[0] for b in big] + ["small"]
    rs_groups = []

    def rs_chips_begin(idx, srcs, r1, gname):
        s1 = [_sum_own_half(g, r, ci, F32 if tags[k] == "small" else BF16, name=f"rs_sum_cores_{tags[k]}")
              for g, r, k in zip(srcs, r1, idx)]
        pair, parts, lands, token = _rs_chips_start(s1, name=f"rs_chips_start_{gname}")
        rs_groups.append((idx, pair, parts, lands))
        return token

    def rs_begin(idx, srcs, gname):
        return rs_chips_begin(idx, srcs, _rs_cores(srcs, name=f"rs_cores_{gname}"), gname)

    dw_in = jnp.concatenate([dw_in_r[:, :4 * QK], dw_in_r[:, 6 * QK:6 * QK + GATE_RANK], dw_in_r[:, 4 * QK:6 * QK]], axis=1)
    dw_in = dw_in.reshape(D, NCH, -1).transpose(1, 0, 2)
    idx1 = [1, 2, 4, 6]
    pair1, src1, land1, tok1 = _rs_cores_start(
        [dw_in, dw_out.reshape(NCH, -1, D), dw_ff1_1, dw_ff2_1.reshape(NCH, -1, D)], name="rs_cores_start_l1")

    df0, dgt2_0 = _rows(gate_bwd, [dxp, f0], [m0[5] + tok1[0, 0]], [(D, BF16)], vD, name="gate_mlp0")
    dh2_0, dw_ff1_0, dw_ff2_0 = _mlp_bwd(df0, h2_0, a_0, W["w_ff1_0"], W["w_ff2_0"], "0")
    src1, land1 = _rs_cores_wait(pair1, src1, land1, dh2_0, name="rs_cores_wait_l1")
    tok1b = rs_chips_begin(idx1, src1, land1, "l1")
    idx0 = [3, 5]
    pair0, src0, land0, tok0 = _rs_cores_start([dw_ff1_0, dw_ff2_0.reshape(NCH, -1, D)], name="rs_cores_start_l0")
    tok2 = tok1b + tok0

    def norm_glu_bwd(xt, dht, drt, vgt, g, sc, gt):
        dxn, dsh, dsc, dg = _norm_mod_bwd(xt, dht, g, sc)
        dxt = drt + dxn
        val, sg = vgt[:, :D], _sigmoid(vgt[:, D:])
        dbr = dxt * gt
        dvg = jnp.concatenate([dbr * sg, dbr * val * sg * (1.0 - sg)], axis=1)
        return dxt, dvg, dsh, dsc, dg, _rsum(dxt * val * sg)

    dxp, dvg0, dsh2_0, dsc2_0, dg_mlp0, dgt1_0 = _rows(
        norm_glu_bwd, [x2_0, dh2_0, dxp, vg0], [vec(norm_mlp[0]), m0[4] + tok2[0, 0], m0[2]], [(D, F32), (2 * D, BF16)],
        vD * 4, name="norm_mlp0_bwd")
    dz0 = _matmul(dvg0, W["s5_w_glu"], tb=True, name="glu_dx")
    dw_glu = _matmul(z0, dvg0, ta=True, name="glu_dw", tn=512, col_shards=NCH)
    src0, land0 = _rs_cores_wait(pair0, src0, land0, dw_glu, name="rs_cores_wait_l0")
    tok0b = rs_chips_begin(idx0 + [0], src0 + [dw_glu], land0 + list(_rs_cores([dw_glu], name="rs_cores_glu")), "l0")
    du0, db_acc, dc_acc, dl_acc, dd_s5 = _s5_bwd2(u0, y0, dz0, ck0, tb, s5_dv + tok0b[0, 0], name="s5_bwd")
    dxp, dsh1_0, dsc1_0, dg_mix0 = _rows(norm_bwd, [xp, du0, dxp], [vec(norm_mix[0]), m0[1]], [(D, F32)], vD * 3,
                                         name="norm_mix0_bwd")
    grad_x = _from_segments(dxp)[None]

    dmod = jnp.concatenate([dsh1_0, dsc1_0, dgt1_0, dsh2_0, dsc2_0, dgt2_0,
                            dsh1_1, dsc1_1, dgt1_1, dsh2_1, dsc2_1, dgt2_1], axis=1)
    dbb_re, dbb_im = _s5_untable(db_acc)
    dc_re, dc_im_neg = _s5_untable(dc_acc)
    nbk = D // 128
    dl = dl_acc.reshape(nbk, NSEG, 2, GPB * S5_P).sum(axis=1)
    smalls = [dmod, dg_mix0, dg_mix1, dg_mlp0, dg_mlp1, d_norm_final, dd_s5, dbg, dgn,
              dwg2p[:GATE_RANK].reshape(1, -1), dbb_re.reshape(1, -1), dbb_im.reshape(1, -1),
              dc_re.reshape(1, -1), dc_im_neg.reshape(1, -1), dl.reshape(1, -1)]
    ssz = [t.shape[1] for t in smalls]
    stot = sum(ssz)
    spad = -(-stot // 8192) * 8192
    svec = jnp.concatenate(smalls + [jnp.zeros((1, spad - stot), F32)], axis=1).reshape(NCH, spad // (128 * NCH), 128)

    dmod_all = _exchange(dmod.reshape(12 * D // 128, 128), MASK_ALL, name="gather_dmod").reshape(8, 2, 6 * D)
    dmod_cols = lax.dynamic_slice_in_dim(dmod_all, chip * acols, acols, axis=2).transpose(1, 0, 2)
    g_w_ada = _ada_grad(c_all, dmod_cols, name="ada_grad")

    rs_begin([7], [svec], "last")
    landed = _rs_chips_wait([(g[1], g[2], g[3]) for g in rs_groups], grad_x, name="rs_chips_wait")
    s1, r2 = {}, {}
    for (idx, _, _, _), (parts, lands) in zip(rs_groups, landed):
        for k, part, land in zip(idx, parts, lands):
            s1[k], r2[k] = part, land

    def fin(k, **kw):
        return _sum_chips(r2[k], s1[k], chip, ci, name=f"rs_sum_chips_{tags[k]}", **kw)

    f_ff1 = fin(4, nlead=2, lead=1, prev=fin(3, nlead=2, lead=0))
    f_ff2 = fin(6, nlead=2, lead=1, prev=fin(5, nlead=2, lead=0))
    finals = [fin(0), fin(1), fin(2), f_ff1, f_ff2, fin(7, spread=True)]
    halves = [t.shape[1] for t in (s1[0], s1[1], s1[2], s1[3], s1[5], s1[7])]
    g_glu, g_in, g_out, g_w_ff1, g_w_ff2, s_own = _rs_gather(finals, halves, [False] * 5 + [True], name="rs_gather_cores")
    srows = spad // (128 * NCH)
    (s_sum,) = _gather_weights([s_own.reshape(NCH * srows, 128)], [(srows, 128)], ["row"], name="gather_small_grads")
    s_sum = s_sum.reshape(-1)
    so = [sum(ssz[:k]) for k in range(len(ssz))]
    sm = [s_sum[o:o + n] for o, n in zip(so, ssz)]
    (dmod_s, g_mix0, g_mix1, g_mlp0, g_mlp1, g_nf, g_d, g_bg, g_gn, g_wg2, g_bbre, g_bbim, g_cre, g_cimn, g_dl) = sm
    g_b_ada = dmod_s.reshape(2, 6 * D)

    G = D // S5_H
    _, disc_vjp = jax.vjp(_s5_disc, s5_a_re[0], s5_a_im[0], s5_log_dt[0], s5_b_re[0], s5_b_im[0])
    g_dl = g_dl.reshape(nbk, 2, GPB, S5_P)
    ct = (g_dl[:, 0].reshape(G, S5_P), g_dl[:, 1].reshape(G, S5_P),
          g_bbre.reshape(G, S5_H, S5_P).transpose(0, 2, 1), g_bbim.reshape(G, S5_H, S5_P).transpose(0, 2, 1))
    g_a_re, g_a_im, g_log_dt, g_b_re, g_b_im = disc_vjp(ct)
    g_c_re = g_cre.reshape(G, S5_H, S5_P)
    g_c_im = -g_cimn.reshape(G, S5_H, S5_P)
    g_wg2_s = lax.dynamic_slice_in_dim(g_wg2.reshape(GATE_RANK, QK), chip * qk4, qk4, axis=1)
    g_bg_s = lax.dynamic_slice_in_dim(g_bg.reshape(1, QK), chip * qk4, qk4, axis=1)
    g_gn_s = lax.dynamic_slice_in_dim(g_gn.reshape(1, D), chip * (D // NCH), D // NCH, axis=1)

    grads = dict(
        w_ada=g_w_ada, b_ada=g_b_ada, norm_mix=jnp.stack([g_mix0, g_mix1]), norm_mlp=jnp.stack([g_mlp0, g_mlp1]),
        s5_a_re=g_a_re[None], s5_a_im=g_a_im[None], s5_log_dt=g_log_dt[None], s5_b_re=g_b_re[None], s5_b_im=g_b_im[None],
        s5_c_re=g_c_re[None], s5_c_im=g_c_im[None], s5_d=g_d[None], s5_w_glu=g_glu,
        gla_w_in=g_in, gla_w_gate2=g_wg2_s[None], gla_b_gate=g_bg_s, gla_g_norm=g_gn_s,
        gla_w_out=g_out, w_ff1=g_w_ff1, w_ff2=g_w_ff2, norm_final=g_nf)

    names = list(grads)
    large = ("w_ada", "s5_w_glu", "gla_w_in", "gla_w_out", "w_ff1", "w_ff2")
    delta, new_m, new_v = {}, {}, {}
    for nm in large:
        delta[nm], new_m[nm], new_v[nm] = _adamw(args[nm], grads[nm], args["m_" + nm], args["v_" + nm], name=f"adamw_{nm}")
    grads = {nm: grads[nm].reshape(args[nm].shape) for nm in names}
    for nm in names:
        if nm not in large:
            shp = args[nm].shape
            as2d = (1, -1) if len(shp) == 1 else shp
            outs = _adamw_whole(*[t.reshape(as2d) for t in (args[nm], grads[nm], args["m_" + nm], args["v_" + nm])],
                                name=f"adamw_{nm}")
            delta[nm], new_m[nm], new_v[nm] = (t.reshape(shp) for t in outs)
    return (loss, grad_x, *[grads[n] for n in names], *[delta[n] for n in names], *[new_m[n] for n in names],
            *[new_v[n] for n in names])
```

```python
import math

import jax
import jax.numpy as jnp
from jax import lax
from jax.experimental import pallas as pl
from jax.experimental.pallas import tpu as pltpu

F32 = jnp.float32
BF16 = jnp.bfloat16
MESH = pl.DeviceIdType.MESH

EPS = 1e-6
CHUNK = 64
GLA_NB = 4
S5_H = 16
S5_P = 64
GPB = 8
NSEG = 8
HEADS = 4
GATE_RANK = 16
GATE_TAU = 16.0
NCH = 4
LR, B1, B2, AEPS, WD, ASTEP = 0.001, 0.9, 0.999, 1e-08, 0.01, 10
VMEM_LIMIT = 56 << 20

MASK_CHIPS = ((1, 0, 0), (0, 1, 0), (1, 1, 0))
MASK_ALL = ((0, 0, 1), (0, 1, 0), (0, 1, 1), (1, 0, 0), (1, 0, 1), (1, 1, 0), (1, 1, 1))


def _params(*sem):
    return pltpu.CompilerParams(dimension_semantics=sem or None, vmem_limit_bytes=VMEM_LIMIT)


def _tile_rows(rows, cap=512):
    best = 8
    for t in range(8, cap + 1, 8):
        if rows % t == 0:
            best = t
    return best


def _whole(shape):
    return pl.BlockSpec(shape, lambda i, _n=len(shape): (0,) * _n)


def _matmul(a, b, *, name, ta=False, tb=False, tm=1024, tn=1024, tk=4096, out_dtypes=(F32,),
            a_fn=None, epi=None, epi_ins=(), col_shards=1):
    M, K = (a.shape[1], a.shape[0]) if ta else a.shape
    N = b.shape[0] if tb else b.shape[1]
    tm, tn, tk = min(tm, M), min(tn, N), min(tk, K)
    assert M % tm == 0 and N % tn == 0 and K % tk == 0, (name, M, N, K)
    nk = K // tk
    ne = len(epi_ins)
    dn = (((0 if ta else 1,), (1 if tb else 0,)), ((), ()))

    def body(a_ref, b_ref, *rest):
        e_refs, o_refs = rest[:ne], rest[ne:ne + len(out_dtypes)]
        at = a_ref[...]
        if a_fn is not None:
            at = a_fn(at)
        part = lax.dot_general(at.astype(BF16), b_ref[...].astype(BF16), dn, preferred_element_type=F32)

        def finish(total):
            outs = (total,) if epi is None else epi(total, *[r[...] for r in e_refs])
            for r, o in zip(o_refs, outs):
                r[...] = o.astype(r.dtype)

        if nk == 1:
            finish(part)
            return
        acc = rest[-1]
        k = pl.program_id(2)

        @pl.when(k == 0)
        def _():
            acc[...] = part

        @pl.when(k > 0)
        def _():
            acc[...] += part

        @pl.when(k == nk - 1)
        def _():
            finish(acc[...])

    a_spec = pl.BlockSpec((tk, tm), lambda i, j, k: (k, i)) if ta else pl.BlockSpec((tm, tk), lambda i, j, k: (i, k))
    b_spec = pl.BlockSpec((tn, tk), lambda i, j, k: (j, k)) if tb else pl.BlockSpec((tk, tn), lambda i, j, k: (k, j))
    o_spec = pl.BlockSpec((tm, tn), lambda i, j, k: (i, j))
    if col_shards > 1:
        per = N // col_shards // tn
        assert ne == 0 and per * tn * col_shards == N
        w_spec = pl.BlockSpec((None, tm, tn), lambda i, j, k: (j // per, i, j % per))
        o_shape = (col_shards, M, N // col_shards)
    else:
        w_spec, o_shape = o_spec, (M, N)
    outs = pl.pallas_call(
        body, name=name, grid=(M // tm, N // tn, nk),
        in_specs=[a_spec, b_spec] + [o_spec] * ne,
        out_specs=[w_spec] * len(out_dtypes),
        out_shape=[jax.ShapeDtypeStruct(o_shape, d) for d in out_dtypes],
        scratch_shapes=[pltpu.VMEM((tm, tn), F32)] if nk > 1 else [],
        compiler_params=_params("parallel", "parallel", "arbitrary"),
    )(a, b, *epi_ins)
    return outs[0] if len(outs) == 1 else outs


def _rows(fn, rows_in, vecs_in, rows_out, acc_out, *, name, tm=512):
    L = rows_in[0].shape[0]
    tm = min(tm, L)
    assert L % tm == 0
    nr, nv, no, na = len(rows_in), len(vecs_in), len(rows_out), len(acc_out)

    def body(*refs):
        rin, vin = refs[:nr], refs[nr:nr + nv]
        rout, aout = refs[nr + nv:nr + nv + no], refs[nr + nv + no:]
        outs = fn(*[r[...] for r in rin], *[v[...] for v in vin])
        for r, o in zip(rout, outs[:no]):
            r[...] = o.astype(r.dtype)
        if na:
            @pl.when(pl.program_id(0) == 0)
            def _():
                for r in aout:
                    r[...] = jnp.zeros_like(r)

            for r, o in zip(aout, outs[no:]):
                r[...] += o

    outs = pl.pallas_call(
        body, name=name, grid=(L // tm,),
        in_specs=[pl.BlockSpec((tm, r.shape[1]), lambda i: (i, 0)) for r in rows_in] + [_whole(v.shape) for v in vecs_in],
        out_specs=[pl.BlockSpec((tm, c), lambda i: (i, 0)) for c, _ in rows_out] + [_whole(s) for s in acc_out],
        out_shape=[jax.ShapeDtypeStruct((L, c), d) for c, d in rows_out] + [jax.ShapeDtypeStruct(s, F32) for s in acc_out],
        compiler_params=_params("arbitrary"),
    )(*rows_in, *vecs_in)
    return outs


def _rsum(t):
    return jnp.sum(t, axis=0, keepdims=True)


def _norm_mod(x, g, sc, sh):
    rs = lax.rsqrt(jnp.mean(x * x, axis=-1, keepdims=True) + EPS)
    return x * rs * g * (1.0 + sc) + sh


def _norm_mod_bwd(x, dh, g, sc):
    rs = lax.rsqrt(jnp.mean(x * x, axis=-1, keepdims=True) + EPS)
    xh = x * rs
    dn = dh * (1.0 + sc)
    dxh = dn * g
    dx = rs * (dxh - xh * jnp.mean(dxh * xh, axis=-1, keepdims=True))
    return dx, _rsum(dh), _rsum(dh * xh * g), _rsum(dn * xh)


def _sigmoid(x):
    return jax.nn.sigmoid(x)


def _gelu(y):
    return jax.nn.gelu(y, approximate=True)


def _gelu_grad(y):
    c = math.sqrt(2.0 / math.pi)
    t = jnp.tanh(c * (y + 0.044715 * y * y * y))
    return 0.5 * (1.0 + t) + 0.5 * y * (1.0 - t * t) * c * (1.0 + 3.0 * 0.044715 * y * y)


def _s5_tables(lb_re, lb_im, bb_re, bb_im, c_re, c_im, seg_len):
    G = lb_re.shape[0]
    nb = G // GPB
    eye = jnp.eye(GPB, dtype=F32)

    def bdiag(t):
        a, b = t.shape[1:]
        t = t.reshape(nb, GPB, a, b)
        return (t[:, :, :, None, :] * eye[None, :, None, :, None]).reshape(nb, GPB * a, GPB * b)

    bbd = jnp.concatenate([bdiag(bb_re.transpose(0, 2, 1)), bdiag(bb_im.transpose(0, 2, 1))], axis=2)
    cbd = jnp.concatenate([bdiag(c_re.transpose(0, 2, 1)), -bdiag(c_im.transpose(0, 2, 1))], axis=1)

    def lanes(re, im):
        t = jnp.concatenate([re.reshape(nb, GPB * S5_P), im.reshape(nb, GPB * S5_P)], axis=1)
        return jnp.repeat(t, NSEG, axis=0)

    tr, ti = lb_re, lb_im
    for _ in range(int(math.log2(seg_len))):
        tr, ti = tr * tr - ti * ti, 2.0 * tr * ti
    return dict(bbd=bbd.astype(BF16), bbdT=bbd.transpose(0, 2, 1).astype(BF16), cbd=cbd.astype(BF16),
                cbdT=cbd.transpose(0, 2, 1).astype(BF16), lam=lanes(lb_re, lb_im), lamT=lanes(tr, ti))


def _s5_untable(acc):
    nb = acc.shape[0]
    t = acc.reshape(nb, GPB, S5_H, 2, GPB, S5_P)
    d = jnp.diagonal(t, axis1=1, axis2=4)
    d = d.transpose(0, 4, 2, 1, 3).reshape(nb * GPB, 2, S5_H, S5_P)
    return d[:, 0], d[:, 1]


S5_R = 256


def _s5_carries(ends, first, t_ref, rws, SW, cfx, *, reverse):
    er, ei = ends
    tr, ti = t_ref[rws, :SW][0:1], t_ref[rws, SW:][0:1]
    cr, ci = first
    order = range(NSEG - 1, -1, -1) if reverse else range(NSEG)
    for n, s in enumerate(order):
        if n > 0:
            p = s + 1 if reverse else s - 1
            if reverse:
                cr, ci = tr * cr + ti * ci + er[p:p + 1], tr * ci - ti * cr + ei[p:p + 1]
            else:
                cr, ci = tr * cr - ti * ci + er[p:p + 1], tr * ci + ti * cr + ei[p:p + 1]
        cfx[s:s + 1, :SW] = cr
        cfx[s:s + 1, SW:] = ci


def _s5_fwd(up, tb, dvec, *, name):
    L, D = up.shape
    R = min(S5_R, L)
    nb, ta, ngb = L // R, R // NSEG, D // 128
    SW = GPB * S5_P
    crows = ngb * NSEG

    def body(u_ref, lam_ref, t_ref, b_ref, c_ref, d_ref, y_ref, z_ref, ck_ref, carry, xbuf2, cfx2):
        @pl.when(pl.program_id(0) == 0)
        def _():
            carry[...] = jnp.zeros_like(carry)

        zero = jnp.zeros((NSEG, SW), F32)
        for gb in range(ngb):
            xbuf, cfx = xbuf2.at[gb % 2], cfx2.at[gb % 2]
            cols = slice(gb * 128, (gb + 1) * 128)
            rws = slice(gb * NSEG, (gb + 1) * NSEG)
            ug = u_ref[:, cols]
            xbuf[...] = jnp.dot(ug.astype(BF16), b_ref[gb], preferred_element_type=F32)
            lr, li = lam_ref[rws, :SW], lam_ref[rws, SW:]

            def step(a, c, lr=lr, li=li, store=False):
                cr, ci = c
                o = pl.multiple_of(a * NSEG, NSEG)
                nr = lr * cr - li * ci + xbuf[pl.ds(o, NSEG), :SW]
                ni = lr * ci + li * cr + xbuf[pl.ds(o, NSEG), SW:]
                if store:
                    xbuf[pl.ds(o, NSEG), :SW] = nr
                    xbuf[pl.ds(o, NSEG), SW:] = ni
                return nr, ni

            ends = lax.fori_loop(0, ta, step, (zero, zero), unroll=True)
            prev = (carry[rws, :SW][NSEG - 1:NSEG], carry[rws, SW:][NSEG - 1:NSEG])
            _s5_carries(ends, prev, t_ref, rws, SW, cfx, reverse=False)
            ck_ref[0, rws, :] = cfx[...]
            cr, ci = lax.fori_loop(0, ta, lambda a, c, st=step: st(a, c, store=True), (cfx[:, :SW], cfx[:, SW:]),
                                   unroll=True)
            carry[rws, :SW] = cr
            carry[rws, SW:] = ci
            y = jnp.dot(xbuf[...].astype(BF16), c_ref[gb], preferred_element_type=F32) + d_ref[:, cols] * ug
            y_ref[:, cols] = y
            z_ref[:, cols] = _gelu(y).astype(BF16)

    rowblk = pl.BlockSpec((R, D), lambda i: (i, 0))
    return pl.pallas_call(
        body, name=name, grid=(nb,),
        in_specs=[rowblk, _whole(tb["lam"].shape), _whole(tb["lamT"].shape), _whole(tb["bbd"].shape),
                  _whole(tb["cbd"].shape), _whole(dvec.shape)],
        out_specs=[rowblk, rowblk, pl.BlockSpec((1, crows, 2 * SW), lambda i: (i, 0, 0))],
        out_shape=[jax.ShapeDtypeStruct((L, D), F32), jax.ShapeDtypeStruct((L, D), BF16),
                   jax.ShapeDtypeStruct((nb, crows, 2 * SW), F32)],
        scratch_shapes=[pltpu.VMEM((crows, 2 * SW), F32), pltpu.VMEM((2, R, 2 * SW), F32),
                        pltpu.VMEM((2, NSEG, 2 * SW), F32)],
        compiler_params=_params("arbitrary"),
    )(up, tb["lam"], tb["lamT"], tb["bbd"], tb["cbd"], dvec)


def _s5_bwd(up, y, dz, ck, tb, dvec, *, name):
    L, D = up.shape
    R = min(S5_R, L)
    nb, ta, ngb = L // R, R // NSEG, D // 128
    SW = GPB * S5_P
    crows = ngb * NSEG

    def body(u_ref, y_ref, dz_ref, ck_ref, lam_ref, t_ref, b_ref, bt_ref, ct_ref, d_ref,
             du_ref, db_ref, dc_ref, dl_ref, dd_ref, gcarry, xbuf2, gbuf2, dybuf, cfx2):
        @pl.when(pl.program_id(0) == 0)
        def _():
            gcarry[...] = jnp.zeros_like(gcarry)
            db_ref[...] = jnp.zeros_like(db_ref)
            dc_ref[...] = jnp.zeros_like(dc_ref)
            dl_ref[...] = jnp.zeros_like(dl_ref)
            dd_ref[...] = jnp.zeros_like(dd_ref)

        zero = jnp.zeros((NSEG, SW), F32)
        dybuf[...] = dz_ref[...] * _gelu_grad(y_ref[...])
        for gb in range(ngb):
            xbuf, gbuf, cfx = xbuf2.at[gb % 2], gbuf2.at[gb % 2], cfx2.at[gb % 2]
            cols = slice(gb * 128, (gb + 1) * 128)
            rws = slice(gb * NSEG, (gb + 1) * NSEG)
            dyg = dybuf[:, cols]
            ug = u_ref[:, cols]
            lr, li = lam_ref[rws, :SW], lam_ref[rws, SW:]
            gbuf[...] = jnp.dot(dyg.astype(BF16), ct_ref[gb], preferred_element_type=F32)
            xbuf[0:NSEG, :] = ck_ref[0, rws, :]
            xbuf[NSEG:, :] = jnp.dot(ug.astype(BF16), b_ref[gb], preferred_element_type=F32)

            def fstep(a, c, lr=lr, li=li):
                cr, ci = c
                o = pl.multiple_of(a * NSEG + NSEG, NSEG)
                nr = lr * cr - li * ci + xbuf[pl.ds(o, NSEG), :SW]
                ni = lr * ci + li * cr + xbuf[pl.ds(o, NSEG), SW:]
                xbuf[pl.ds(o, NSEG), :SW] = nr
                xbuf[pl.ds(o, NSEG), SW:] = ni
                return nr, ni

            lax.fori_loop(0, ta, fstep, (xbuf[0:NSEG, :SW], xbuf[0:NSEG, SW:]), unroll=True)

            def rstep(k, c, lr=lr, li=li, store=False):
                o = pl.multiple_of((ta - 1 - k) * NSEG, NSEG)
                gr_n, gi_n = c[0], c[1]
                gr = gbuf[pl.ds(o, NSEG), :SW] + lr * gr_n + li * gi_n
                gi = gbuf[pl.ds(o, NSEG), SW:] - li * gr_n + lr * gi_n
                if not store:
                    return gr, gi
                gbuf[pl.ds(o, NSEG), :SW] = gr
                gbuf[pl.ds(o, NSEG), SW:] = gi
                xr, xi = xbuf[pl.ds(o, NSEG), :SW], xbuf[pl.ds(o, NSEG), SW:]
                return gr, gi, c[2] + gr * xr + gi * xi, c[3] + gi * xr - gr * xi

            gends = lax.fori_loop(0, ta, rstep, (zero, zero), unroll=True)
            nxt = (gcarry[rws, :SW][0:1], gcarry[rws, SW:][0:1])
            _s5_carries(gends, nxt, t_ref, rws, SW, cfx, reverse=True)
            cf = lax.fori_loop(0, ta, lambda k, c, st=rstep: st(k, c, store=True),
                               (cfx[:, :SW], cfx[:, SW:], zero, zero), unroll=True)
            gcarry[rws, :SW] = cf[0]
            gcarry[rws, SW:] = cf[1]
            dl_ref[rws, :SW] += cf[2]
            dl_ref[rws, SW:] += cf[3]
            gb16 = gbuf[...].astype(BF16)
            du_ref[:, cols] = jnp.dot(gb16, bt_ref[gb], preferred_element_type=F32) + d_ref[:, cols] * dyg
            db_ref[gb] += lax.dot_general(ug.astype(BF16), gb16, TN, preferred_element_type=F32)
            dc_ref[gb] += lax.dot_general(dyg.astype(BF16), xbuf[NSEG:, :].astype(BF16), TN, preferred_element_type=F32)
            dd_ref[:, cols] += _rsum(dyg * ug)

    rev = pl.BlockSpec((R, D), lambda i: (nb - 1 - i, 0))
    acc3 = (ngb, 128, 2 * SW)
    return pl.pallas_call(
        body, name=name, grid=(nb,),
        in_specs=[rev, rev, rev, pl.BlockSpec((1, crows, 2 * SW), lambda i: (nb - 1 - i, 0, 0)),
                  _whole(tb["lam"].shape), _whole(tb["lamT"].shape), _whole(tb["bbd"].shape),
                  _whole(tb["bbdT"].shape), _whole(tb["cbdT"].shape), _whole(dvec.shape)],
        out_specs=[rev, _whole(acc3), _whole(acc3), _whole((crows, 2 * SW)), _whole((1, D))],
        out_shape=[jax.ShapeDtypeStruct((L, D), F32), jax.ShapeDtypeStruct(acc3, F32), jax.ShapeDtypeStruct(acc3, F32),
                   jax.ShapeDtypeStruct((crows, 2 * SW), F32), jax.ShapeDtypeStruct((1, D), F32)],
        scratch_shapes=[pltpu.VMEM((crows, 2 * SW), F32), pltpu.VMEM((2, R + NSEG, 2 * SW), F32),
                        pltpu.VMEM((2, R, 2 * SW), F32), pltpu.VMEM((R, D), F32), pltpu.VMEM((2, NSEG, 2 * SW), F32)],
        compiler_params=_params("arbitrary"),
    )(up, y, dz, ck, tb["lam"], tb["lamT"], tb["bbd"], tb["bbdT"], tb["cbdT"], dvec)


NN = (((1,), (0,)), ((), ()))
TN = (((0,), (0,)), ((), ()))
NT = (((1,), (1,)), ((), ()))


def _dot3(lhs, rhs, dn, split):
    x = rhs if split == "rhs" else lhs
    hi = x.astype(BF16)
    r1 = x - hi.astype(F32)
    mid = r1.astype(BF16)
    lo = (r1 - mid.astype(F32)).astype(BF16)
    out = None
    for part in (hi, mid, lo):
        ops = (lhs, part) if split == "rhs" else (part, rhs)
        t = lax.dot_general(ops[0], ops[1], dn, preferred_element_type=F32)
        out = t if out is None else out + t
    return out


def _log_sigmoid(x):
    return jnp.minimum(x, 0.0) - jnp.log(1.0 + jnp.exp(-jnp.abs(x)))


def _chunk_tri(rows, upper):
    r = lax.broadcasted_iota(jnp.int32, (rows, rows), 0)
    c = lax.broadcasted_iota(jnp.int32, (rows, rows), 1)
    same = (r // CHUNK) == (c // CHUNK)
    return (same & ((c >= r) if upper else (r >= c))).astype(BF16)


def _gla_block_gates(p_ref, wg_ref, bg_ref, QK, wbuf, gebuf):
    RB = p_ref.shape[0]
    glr = p_ref[:, 6 * QK:6 * QK + 128].astype(BF16)
    gpre = jnp.dot(glr, wg_ref[...], preferred_element_type=F32) + bg_ref[...]
    la = _log_sigmoid(gpre) * (1.0 / GATE_TAU)
    gc = _dot3(_chunk_tri(RB, False), la, NN, "rhs")
    for cc in range(RB // CHUNK):
        rows = slice(cc * CHUNK, (cc + 1) * CHUNK)
        ge = gc[(cc + 1) * CHUNK - 1:(cc + 1) * CHUNK, :]
        gebuf[cc:cc + 1, :] = ge
        wbuf[rows, :] = jnp.exp(ge - gc[rows, :])
    return glr, gpre, la


def _as_column(row, lanes):
    t = jnp.transpose(jnp.broadcast_to(row, (row.shape[1], row.shape[1])))
    return jnp.concatenate([t] * (lanes // row.shape[1]), axis=1)


def _as_row(col):
    return jnp.transpose(jnp.broadcast_to(col, (col.shape[0], col.shape[0])))[0:1, :]


def _gla_fwd(proj, wg2p, bg, gn, *, name):
    L = proj.shape[0]
    QK = wg2p.shape[1]
    DK, DV = QK // HEADS, 2 * QK // HEADS
    nC = L // CHUNK
    NB = min(GLA_NB, nC)
    assert nC % NB == 0
    scale = DK ** -0.5

    def body(p_ref, wg_ref, bg_ref, gn_ref, og_ref, s_ref, sst, wbuf, gebuf):
        @pl.when(pl.program_id(0) == 0)
        def _():
            sst[...] = jnp.zeros_like(sst)

        _gla_block_gates(p_ref, wg_ref, bg_ref, QK, wbuf, gebuf)
        heads = range(HEADS)
        ks = [slice(h * DK, (h + 1) * DK) for h in heads]
        vs = [slice(h * DV, (h + 1) * DV) for h in heads]
        for cc in range(NB):
            rows = slice(cc * CHUNK, (cc + 1) * CHUNK)
            kv = [lax.dot_general((p_ref[rows, QK + h * DK:QK + (h + 1) * DK] * wbuf[rows, ks[h]]).astype(BF16),
                                  p_ref[rows, 2 * QK + h * DV:2 * QK + (h + 1) * DV].astype(BF16), TN,
                                  preferred_element_type=F32) for h in heads]
            S = []
            for h in heads:
                S.append(jnp.exp(_as_column(gebuf[cc:cc + 1, ks[h]], DV)) * sst[ks[h], :] + kv[h])
                sst[ks[h], :] = S[h]
                s_ref[cc, ks[h], :] = S[h]
            o = [jnp.dot((p_ref[rows, h * DK:(h + 1) * DK] * scale).astype(BF16), S[h].astype(BF16),
                         preferred_element_type=F32) for h in heads]
            for h in heads:
                r = p_ref[rows, 4 * QK + h * DV:4 * QK + (h + 1) * DV]
                on = o[h] * lax.rsqrt(jnp.mean(o[h] * o[h], axis=-1, keepdims=True) + EPS)
                og_ref[rows, vs[h]] = (on * gn_ref[:, vs[h]] * (r * _sigmoid(r))).astype(BF16)

    RB = NB * CHUNK
    return pl.pallas_call(
        body, name=name, grid=(nC // NB,),
        in_specs=[pl.BlockSpec((RB, proj.shape[1]), lambda i: (i, 0)), _whole(wg2p.shape), _whole(bg.shape), _whole(gn.shape)],
        out_specs=[pl.BlockSpec((RB, 2 * QK), lambda i: (i, 0)), pl.BlockSpec((NB, QK, DV), lambda i: (i, 0, 0))],
        out_shape=[jax.ShapeDtypeStruct((L, 2 * QK), BF16), jax.ShapeDtypeStruct((nC, QK, DV), F32)],
        scratch_shapes=[pltpu.VMEM((QK, DV), F32), pltpu.VMEM((RB, QK), F32), pltpu.VMEM((8, QK), F32)],
        compiler_params=_params("arbitrary"),
    )(proj, wg2p, bg, gn)


def _gla_bwd(proj, dog, states, wg2p, bg, gn, *, name):
    L, W = proj.shape
    QK = wg2p.shape[1]
    DK, DV = QK // HEADS, 2 * QK // HEADS
    nC = L // CHUNK
    NB = min(GLA_NB, nC)
    nB = nC // NB
    scale = DK ** -0.5

    def body(p_ref, dog_ref, sc_ref, sp_ref, wg_ref, bg_ref, gn_ref, dp_ref, dwg_ref, dbg_ref, dgn_ref,
             gst, wbuf, gebuf, dwwbuf, dgebuf):
        i = pl.program_id(0)

        @pl.when(i == 0)
        def _():
            gst[...] = jnp.zeros_like(gst)
            dwg_ref[...] = jnp.zeros_like(dwg_ref)
            dbg_ref[...] = jnp.zeros_like(dbg_ref)
            dgn_ref[...] = jnp.zeros_like(dgn_ref)

        RB = NB * CHUNK
        glr, gpre, _ = _gla_block_gates(p_ref, wg_ref, bg_ref, QK, wbuf, gebuf)
        heads = range(HEADS)
        ks = [slice(h * DK, (h + 1) * DK) for h in heads]
        vs = [slice(h * DV, (h + 1) * DV) for h in heads]
        for cc in range(NB - 1, -1, -1):
            rows = slice(cc * CHUNK, (cc + 1) * CHUNK)
            qs16 = [(p_ref[rows, h * DK:(h + 1) * DK] * scale).astype(BF16) for h in heads]
            S16 = [sc_ref[cc, ks[h], :].astype(BF16) for h in heads]
            o = [jnp.dot(qs16[h], S16[h], preferred_element_type=F32) for h in heads]
            do = []
            for h in heads:
                r = p_ref[rows, 4 * QK + h * DV:4 * QK + (h + 1) * DV]
                rs = lax.rsqrt(jnp.mean(o[h] * o[h], axis=-1, keepdims=True) + EPS)
                on = o[h] * rs
                sr = _sigmoid(r)
                dg = dog_ref[rows, vs[h]]
                gnh = gn_ref[:, vs[h]]
                dp_ref[rows, 4 * QK + h * DV:4 * QK + (h + 1) * DV] = (
                    dg * on * gnh * (sr * (1.0 + r * (1.0 - sr)))).astype(BF16)
                dt = dg * (r * sr)
                dgn_ref[:, vs[h]] += _rsum(dt * on)
                don = dt * gnh
                do.append((rs * (don - on * jnp.mean(don * on, axis=-1, keepdims=True))).astype(BF16))
            Gc = [gst[ks[h], :] + lax.dot_general(qs16[h], do[h], TN, preferred_element_type=F32) for h in heads]
            G16 = [g.astype(BF16) for g in Gc]
            dq = [lax.dot_general(do[h], S16[h], NT, preferred_element_type=F32) for h in heads]
            kd = [p_ref[rows, QK + h * DK:QK + (h + 1) * DK] * wbuf[rows, ks[h]] for h in heads]
            dkd = [lax.dot_general(p_ref[rows, 2 * QK + h * DV:2 * QK + (h + 1) * DV].astype(BF16), G16[h], NT,
                                   preferred_element_type=F32) for h in heads]
            dv = [jnp.dot(kd[h].astype(BF16), G16[h], preferred_element_type=F32) for h in heads]
            for h in heads:
                if cc > 0:
                    Sp = sc_ref[cc - 1, ks[h], :]
                else:
                    Sp = jnp.where(i < nB - 1, sp_ref[0, ks[h], :], 0.0)
                dp_ref[rows, h * DK:(h + 1) * DK] = (dq[h] * scale).astype(BF16)
                dp_ref[rows, 2 * QK + h * DV:2 * QK + (h + 1) * DV] = dv[h].astype(BF16)
                ge = gebuf[cc:cc + 1, ks[h]]
                gst[ks[h], :] = jnp.exp(_as_column(ge, DV)) * Gc[h]
                ddec = _as_row(jnp.sum(Gc[h] * Sp, axis=1, keepdims=True))
                dp_ref[rows, QK + h * DK:QK + (h + 1) * DK] = (dkd[h] * wbuf[rows, ks[h]]).astype(BF16)
                dww = dkd[h] * kd[h]
                dwwbuf[rows, ks[h]] = dww
                dgebuf[cc:cc + 1, ks[h]] = jnp.exp(ge) * ddec + _rsum(dww)
        rev = _dot3(_chunk_tri(RB, True), dwwbuf[...], NN, "rhs")
        for cc in range(NB):
            rows = slice(cc * CHUNK, (cc + 1) * CHUNK)
            wbuf[rows, :] = dgebuf[cc:cc + 1, :] - rev[rows, :]
        dgpre = wbuf[...] * (1.0 / GATE_TAU) * (1.0 - _sigmoid(gpre))
        d16 = dgpre.astype(BF16)
        dp_ref[:, 6 * QK:6 * QK + 128] = lax.dot_general(d16, wg_ref[...], NT, preferred_element_type=F32).astype(BF16)
        dwg_ref[...] += lax.dot_general(glr, d16, TN, preferred_element_type=F32)
        dbg_ref[...] += _rsum(dgpre)

    RB = NB * CHUNK
    rev_idx = lambda i: (nB - 1 - i, 0)
    return pl.pallas_call(
        body, name=name, grid=(nB,),
        in_specs=[pl.BlockSpec((RB, W), rev_idx), pl.BlockSpec((RB, 2 * QK), rev_idx),
                  pl.BlockSpec((NB, QK, DV), lambda i: (nB - 1 - i, 0, 0)),
                  pl.BlockSpec((1, QK, DV), lambda i: (jnp.maximum(NB * (nB - 1 - i) - 1, 0), 0, 0)),
                  _whole(wg2p.shape), _whole(bg.shape), _whole(gn.shape)],
        out_specs=[pl.BlockSpec((RB, W), rev_idx), _whole((128, QK)), _whole((1, QK)), _whole((1, 2 * QK))],
        out_shape=[jax.ShapeDtypeStruct((L, W), BF16), jax.ShapeDtypeStruct((128, QK), F32),
                   jax.ShapeDtypeStruct((1, QK), F32), jax.ShapeDtypeStruct((1, 2 * QK), F32)],
        scratch_shapes=[pltpu.VMEM((QK, DV), F32), pltpu.VMEM((RB, QK), F32), pltpu.VMEM((8, QK), F32),
                        pltpu.VMEM((RB, QK), F32), pltpu.VMEM((8, QK), F32)],
        compiler_params=_params("arbitrary"),
    )(proj, dog, states, states, wg2p, bg, gn)


def _coords():
    return lax.axis_index("x"), lax.axis_index("y"), lax.axis_index("c")


def _other_chips(x, y):
    return [(1 - x, y, 2 * (1 - x) + y), (x, 1 - y, 2 * x + 1 - y), (1 - x, 1 - y, 2 * (1 - x) + 1 - y)]


def _hbm_call(body, ins, out_shapes, n_sems, *, name, alias=False):
    any_spec = pl.BlockSpec(memory_space=pl.ANY)
    return pl.pallas_call(
        body, name=name, in_specs=[any_spec] * len(ins), out_specs=[any_spec] * len(out_shapes), out_shape=out_shapes,
        scratch_shapes=[pltpu.SemaphoreType.DMA((n,)) for n in n_sems],
        input_output_aliases={k: k for k in range(len(ins))} if alias else {},
    )(*ins)


def _exchange(src, masks, *, name):
    vary = [any(m[k] for m in masks) for k in range(3)]
    nslots = 2 ** sum(vary)
    n = len(masks)

    def slot(coords):
        s = 0
        for k in range(3):
            if vary[k]:
                s = s * 2 + coords[k]
        return s

    def body(src_ref, dst_ref, send_sems, recv_sems, loc_sem):
        me = _coords()
        mine = slot(me)
        loc = pltpu.make_async_copy(src_ref, dst_ref.at[mine], loc_sem.at[0])
        loc.start()
        copies = []
        for k, m in enumerate(masks):
            peer = tuple(1 - me[d] if m[d] else me[d] for d in range(3))
            cp = pltpu.make_async_remote_copy(src_ref=src_ref, dst_ref=dst_ref.at[mine], send_sem=send_sems.at[k],
                                              recv_sem=recv_sems.at[k], device_id=peer, device_id_type=MESH)
            cp.start()
            copies.append(cp)
        for cp in copies:
            cp.wait()
        loc.wait()

    return _hbm_call(body, [src], [jax.ShapeDtypeStruct((nslots,) + tuple(src.shape), src.dtype)], (n, n, 1), name=name)[0]


def _cast_into(t, lead, kind, chip, *, name, tm=256):
    r, cc = t.shape[-2:]
    tm = min(tm, r)
    nblk = r // tm
    if kind == "col":
        shp, o_spec = (r, NCH * cc), pl.BlockSpec((tm, cc), lambda i, s: (i, s[0]))
    elif kind == "row":
        shp, o_spec = (NCH * r, cc), pl.BlockSpec((tm, cc), lambda i, s: (s[0] * nblk + i, 0))
    else:
        shp, o_spec = (NCH, r, cc), pl.BlockSpec((None, tm, cc), lambda i, s: (s[0], i, 0))

    def body(s_ref, t_ref, o_ref):
        o_ref[...] = t_ref[...].astype(o_ref.dtype)

    return pl.pallas_call(
        body, name=name,
        grid_spec=pltpu.PrefetchScalarGridSpec(
            num_scalar_prefetch=1, grid=(nblk,),
            in_specs=[pl.BlockSpec((None, tm, cc), lambda i, s: (lead, i, 0))], out_specs=o_spec),
        out_shape=jax.ShapeDtypeStruct(shp, BF16), compiler_params=_params("parallel"),
    )(chip.reshape(1).astype(jnp.int32), t)


def _gather_weights(arrs, shard_shapes, kinds, *, name):
    n = len(arrs)

    def body(*refs):
        dst = refs[n:2 * n]
        send_sems, recv_sems = refs[2 * n:]
        x, y, c = _coords()
        chip = 2 * x + y
        others = _other_chips(x, y)
        sib = (x, y, 1 - c)

        def window(p, chip_id, cc):
            r, cols = shard_shapes[p]
            h = r // 2
            if kinds[p] == "col":
                return dst[p].at[pl.ds(cc * h, h), pl.ds(pl.multiple_of(chip_id * cols, 128), cols)]
            if kinds[p] == "row":
                return dst[p].at[pl.ds(chip_id * r + cc * h, h), :]
            return dst[p].at[chip_id, pl.ds(cc * h, h), :]

        def copy(p, k, win, to):
            return pltpu.make_async_remote_copy(src_ref=win, dst_ref=win, send_sem=send_sems.at[6 * p + k],
                                                recv_sem=recv_sems.at[6 * p + k], device_id=to, device_id_type=MESH)

        sends = []
        for p in range(n):
            for j, (ox, oy, _) in enumerate(others):
                cp = copy(p, j, window(p, chip, c), (ox, oy, c))
                cp.start()
                sends.append(cp)
        for j, (_, _, oc) in enumerate(others):
            for p in range(n):
                copy(p, j, window(p, oc, c), (x, y, c)).wait_recv()
                fw = copy(p, 3 + j, window(p, oc, c), sib)
                fw.start()
                sends.append(fw)
        for p in range(n):
            for j, (_, _, oc) in enumerate(others):
                copy(p, 3 + j, window(p, oc, 1 - c), sib).wait_recv()
        for cp in sends:
            cp.wait_send()

    outs = [jax.ShapeDtypeStruct(a.shape, a.dtype) for a in arrs]
    return _hbm_call(body, arrs, outs, (6 * n, 6 * n), name=name, alias=True)


HBM_SPEC = pl.BlockSpec(memory_space=pltpu.HBM)
SEM_SPEC = pl.BlockSpec(memory_space=pltpu.SEMAPHORE)
EFFECT = pltpu.SideEffectType.DATAFLOW_SIDE_EFFECTING


def _window(ref, shard_shape, kind, chip_id, cc):
    r, cols = shard_shape
    h = r // 2
    if kind == "col":
        return ref.at[pl.ds(cc * h, h), pl.ds(pl.multiple_of(chip_id * cols, 128), cols)]
    if kind == "row":
        return ref.at[pl.ds(chip_id * r + cc * h, h), :]
    return ref.at[chip_id, pl.ds(cc * h, h), :]


def _split_start(start, arrs, n_sems, *, name):
    n, ns = len(arrs), len(n_sems)

    def body(*refs):
        start(refs[:n], refs[n:n + ns])
        refs[-1][...] = jnp.zeros_like(refs[-1])

    outs = pl.pallas_call(
        body, name=name,
        out_shape=tuple([pltpu.SemaphoreType.DMA((k,)) for k in n_sems] + [pltpu.HBM(a.shape, a.dtype) for a in arrs]
                        + [jax.ShapeDtypeStruct((8, 128), F32)]),
        in_specs=[HBM_SPEC] * n, out_specs=tuple([SEM_SPEC] * ns + [HBM_SPEC] * n + [pl.BlockSpec(memory_space=pltpu.VMEM)]),
        input_output_aliases={k: ns + k for k in range(n)},
        compiler_params=pltpu.CompilerParams(has_side_effects=EFFECT),
    )(*[pltpu.with_memory_space_constraint(a, pltpu.HBM) for a in arrs])
    return list(outs[:ns]), list(outs[ns:ns + n]), outs[-1]


def _split_wait(wait, arrs, sems, after, *, name):
    n, ns = len(arrs), len(sems)

    def body(*refs):
        wait(refs[:n], refs[n:n + ns])

    return pl.pallas_call(
        body, name=name, out_shape=tuple(pltpu.HBM(a.shape, a.dtype) for a in arrs),
        in_specs=[HBM_SPEC] * n + [SEM_SPEC] * ns + [pl.BlockSpec(memory_space=pl.ANY)], out_specs=tuple([HBM_SPEC] * n),
        input_output_aliases={k: k for k in range(n)},
        compiler_params=pltpu.CompilerParams(has_side_effects=EFFECT),
    )(*arrs, *sems, after)


def _gw_copies(refs, send_sems, recv_sems, shard_shapes, kinds, outgoing):
    x, y, c = _coords()
    chip = 2 * x + y
    out = []
    for p in range(len(refs)):
        for j, (ox, oy, oc) in enumerate(_other_chips(x, y)):
            win = _window(refs[p], shard_shapes[p], kinds[p], chip if outgoing else oc, c)
            out.append(pltpu.make_async_remote_copy(
                src_ref=win, dst_ref=win, send_sem=send_sems.at[3 * p + j], recv_sem=recv_sems.at[3 * p + j],
                device_id=(ox, oy, c), device_id_type=MESH))
    return out


def _gw_start(arrs, shard_shapes, kinds, groups, *, name):
    def start(refs, sems):
        for g, idx in enumerate(groups):
            for cp in _gw_copies([refs[p] for p in idx], sems[2 * g], sems[2 * g + 1], [shard_shapes[p] for p in idx],
                                 [kinds[p] for p in idx], True):
                cp.start()

    n_sems = [3 * len(idx) for idx in groups for _ in range(2)]
    sems, thru, token = _split_start(start, arrs, n_sems, name=name)
    return [(sems[2 * g], sems[2 * g + 1]) for g in range(len(groups))], thru, token


def _gw_wait(arrs, shard_shapes, kinds, sem_pair, after, *, name):
    def wait(refs, sems):
        for cp in _gw_copies(refs, sems[0], sems[1], shard_shapes, kinds, True):
            cp.wait_send()
        for cp in _gw_copies(refs, sems[0], sems[1], shard_shapes, kinds, False):
            cp.wait_recv()

    return _split_wait(wait, arrs, list(sem_pair), after, name=name)


def _gw_forward(arrs, shard_shapes, kinds, *, name):
    n = len(arrs)

    def body(*refs):
        dst = refs[n:2 * n]
        send_sems, recv_sems = refs[2 * n:]
        x, y, c = _coords()
        sends = []
        for p in range(n):
            for j, (_, _, oc) in enumerate(_other_chips(x, y)):
                win = _window(dst[p], shard_shapes[p], kinds[p], oc, c)
                cp = pltpu.make_async_remote_copy(src_ref=win, dst_ref=win, send_sem=send_sems.at[3 * p + j],
                                                  recv_sem=recv_sems.at[3 * p + j], device_id=(x, y, 1 - c),
                                                  device_id_type=MESH)
                cp.start()
                sends.append(cp)
        for p in range(n):
            for j, (_, _, oc) in enumerate(_other_chips(x, y)):
                win = _window(dst[p], shard_shapes[p], kinds[p], oc, 1 - c)
                pltpu.make_async_remote_copy(src_ref=win, dst_ref=win, send_sem=send_sems.at[3 * p + j],
                                             recv_sem=recv_sems.at[3 * p + j], device_id=(x, y, 1 - c),
                                             device_id_type=MESH).wait_recv()
        for cp in sends:
            cp.wait_send()

    outs = [jax.ShapeDtypeStruct(a.shape, a.dtype) for a in arrs]
    return _hbm_call(body, arrs, outs, (3 * n, 3 * n), name=name, alias=True)


def _rs_chips_copies(parts, lands, send_sems, recv_sems):
    x, y, c = _coords()
    chip = 2 * x + y
    out = []
    for p in range(len(parts)):
        for j, (ox, oy, oc) in enumerate(_other_chips(x, y)):
            out.append(pltpu.make_async_remote_copy(
                src_ref=parts[p].at[oc], dst_ref=lands[p].at[chip], send_sem=send_sems.at[3 * p + j],
                recv_sem=recv_sems.at[3 * p + j], device_id=(ox, oy, c), device_id_type=MESH))
    return out


def _rs_chips_start(parts, *, name):
    n = len(parts)

    def start(refs, sems):
        for cp in _rs_chips_copies(refs[:n], refs[n:], sems[0], sems[1]):
            cp.start()

    lands = [lax.empty(t.shape, t.dtype) for t in parts]
    sems, thru, token = _split_start(start, list(parts) + lands, [3 * n, 3 * n], name=name)
    return (sems[0], sems[1]), thru[:n], thru[n:], token


def _rs_chips_wait(groups, after, *, name):
    sizes = [len(g[1]) for g in groups]
    arrs = [a for g in groups for a in list(g[1]) + list(g[2])]
    sems = [s for g in groups for s in g[0]]

    def wait(refs, sem_refs):
        o = 0
        for k, n in enumerate(sizes):
            for cp in _rs_chips_copies(refs[o:o + n], refs[o + n:o + 2 * n], sem_refs[2 * k], sem_refs[2 * k + 1]):
                cp.wait()
            o += 2 * n

    outs = _split_wait(wait, arrs, sems, after, name=name)
    res, o = [], 0
    for n in sizes:
        res.append((list(outs[o:o + n]), list(outs[o + n:o + 2 * n])))
        o += 2 * n
    return res


def _rs_cores_copies(grads, lands, send_sems, recv_sems):
    x, y, c = _coords()
    out, o = [], 0
    for p in range(len(grads)):
        nsh, h = lands[p].shape[0], lands[p].shape[1]
        for j in range(nsh):
            out.append(pltpu.make_async_remote_copy(
                src_ref=grads[p].at[j, pl.ds((1 - c) * h, h), :], dst_ref=lands[p].at[j],
                send_sem=send_sems.at[o + j], recv_sem=recv_sems.at[o + j], device_id=(x, y, 1 - c), device_id_type=MESH))
        o += nsh
    return out


def _rs_cores_start(grads, *, name):
    n = len(grads)
    tot = sum(g.shape[0] for g in grads)

    def start(refs, sems):
        for cp in _rs_cores_copies(refs[:n], refs[n:], sems[0], sems[1]):
            cp.start()

    lands = [lax.empty((g.shape[0], g.shape[1] // 2, g.shape[2]), g.dtype) for g in grads]
    sems, thru, token = _split_start(start, list(grads) + lands, [tot, tot], name=name)
    return (sems[0], sems[1]), thru[:n], thru[n:], token


def _rs_cores_wait(pair, grads, lands, after, *, name):
    n = len(grads)

    def wait(refs, sems):
        for cp in _rs_cores_copies(refs[:n], refs[n:], sems[0], sems[1]):
            cp.wait()

    outs = _split_wait(wait, list(grads) + list(lands), list(pair), after, name=name)
    return list(outs[:n]), list(outs[n:])


def _rs_cores(grads, *, name):
    n = len(grads)
    outs = [jax.ShapeDtypeStruct((g.shape[0], g.shape[1] // 2, g.shape[2]), g.dtype) for g in grads]

    def body(*refs):
        src, dst = refs[:n], refs[n:2 * n]
        send_sems, recv_sems = refs[2 * n:]
        x, y, c = _coords()
        copies = []
        for p in range(n):
            nsh, r, _ = grads[p].shape
            h = r // 2
            for j in range(nsh):
                cp = pltpu.make_async_remote_copy(
                    src_ref=src[p].at[j, pl.ds((1 - c) * h, h), :], dst_ref=dst[p].at[j],
                    send_sem=send_sems.at[nsh * p + j], recv_sem=recv_sems.at[nsh * p + j],
                    device_id=(x, y, 1 - c), device_id_type=MESH)
                cp.start()
                copies.append(cp)
        for cp in copies:
            cp.wait()

    tot = sum(g.shape[0] for g in grads)
    return _hbm_call(body, grads, outs, (tot, tot), name=name)


def _sum_own_half(full, recv, ci, out_dtype, *, name):
    nsh, h, cols = recv.shape
    tm = h if nsh * h * cols * 4 <= (2 << 20) else _tile_rows(h, 256)
    nblk = h // tm

    def body(c_ref, f_ref, r_ref, o_ref):
        o_ref[...] = (f_ref[...] + r_ref[...]).astype(o_ref.dtype)

    return pl.pallas_call(
        body, name=name,
        grid_spec=pltpu.PrefetchScalarGridSpec(
            num_scalar_prefetch=1, grid=(nsh, nblk),
            in_specs=[pl.BlockSpec((1, tm, cols), lambda j, i, c_ref: (j, c_ref[0] * nblk + i, 0)),
                      pl.BlockSpec((1, tm, cols), lambda j, i, c_ref: (j, i, 0))],
            out_specs=pl.BlockSpec((1, tm, cols), lambda j, i, c_ref: (j, i, 0))),
        out_shape=jax.ShapeDtypeStruct((nsh, h, cols), out_dtype), compiler_params=_params("parallel", "parallel"),
    )(ci.reshape(1).astype(jnp.int32), full, recv)


def _sum_chips(recv, own, chip, ci, *, name, nlead=1, lead=0, prev=None, spread=False):
    nsh, h, cols = recv.shape
    tm = h if nsh * h * cols * 4 <= (2 << 20) else _tile_rows(h, 256)
    nblk = h // tm
    rows_out = 2 * h * (nsh if spread else 1)

    def body(s_ref, r_ref, o_ref, *rest):
        out_ref = rest[-1]
        t = None
        for s in range(nsh):
            v = jnp.where(s_ref[0] == s, o_ref[s], r_ref[s]).astype(F32)
            t = v if t is None else t + v
        out_ref[...] = t

    def out_idx(i, s):
        return (lead, (s[0] * 2 * nblk if spread else 0) + s[1] * nblk + i, 0)

    blk = pl.BlockSpec((nsh, tm, cols), lambda i, s: (0, i, 0))
    ins = [recv, own] + ([prev] if prev is not None else [])
    return pl.pallas_call(
        body, name=name,
        grid_spec=pltpu.PrefetchScalarGridSpec(
            num_scalar_prefetch=1, grid=(nblk,),
            in_specs=[blk, blk] + ([pl.BlockSpec(memory_space=pl.ANY)] if prev is not None else []),
            out_specs=pl.BlockSpec((None, tm, cols), out_idx)),
        out_shape=jax.ShapeDtypeStruct((nlead, rows_out, cols), F32),
        input_output_aliases={3: 0} if prev is not None else {},
        compiler_params=_params("arbitrary"),
    )(jnp.stack([chip, ci]).astype(jnp.int32), *ins)


def _rs_gather(arrs, halves, spread, *, name, nchunk=4):
    n = len(arrs)
    per = [a.shape[0] * nchunk for a in arrs]
    offs = [sum(per[:p]) for p in range(n)]

    def body(*refs):
        dst = refs[n:2 * n]
        send_sems, recv_sems = refs[2 * n:]
        x, y, c = _coords()
        chip = 2 * x + y
        copies = []
        for p in range(n):
            h = halves[p]
            q = h // nchunk
            base = chip * 2 * h if spread[p] else 0
            for l in range(arrs[p].shape[0]):
                for k in range(nchunk):
                    win = dst[p].at[l, pl.ds(base + c * h + k * q, q), :]
                    sem = offs[p] + l * nchunk + k
                    cp = pltpu.make_async_remote_copy(src_ref=win, dst_ref=win, send_sem=send_sems.at[sem],
                                                      recv_sem=recv_sems.at[sem], device_id=(x, y, 1 - c),
                                                      device_id_type=MESH)
                    cp.start()
                    copies.append(cp)
        for cp in copies:
            cp.wait_send()
        for p in range(n):
            h = halves[p]
            q = h // nchunk
            base = chip * 2 * h if spread[p] else 0
            for l in range(arrs[p].shape[0]):
                for k in range(nchunk):
                    win = dst[p].at[l, pl.ds(base + (1 - c) * h + k * q, q), :]
                    sem = offs[p] + l * nchunk + k
                    pltpu.make_async_remote_copy(src_ref=win, dst_ref=win, send_sem=send_sems.at[sem],
                                                 recv_sem=recv_sems.at[sem], device_id=(x, y, 1 - c),
                                                 device_id_type=MESH).wait_recv()

    outs = [jax.ShapeDtypeStruct(a.shape, a.dtype) for a in arrs]
    return _hbm_call(body, arrs, outs, (sum(per), sum(per)), name=name, alias=True)


def _adamw(w, g, m, v, *, name):
    nl, R, C = w.shape
    tm = _tile_rows(R, 256)

    blk = pl.BlockSpec((None, tm, C), lambda l, i: (l, i, 0))
    return pl.pallas_call(
        _adamw_body_copy(), name=name, grid=(nl, R // tm), in_specs=[blk] * 4, out_specs=[blk] * 3,
        out_shape=[jax.ShapeDtypeStruct((nl, R, C), F32)] * 3, compiler_params=_params("parallel", "parallel"),
    )(w, g, m, v)


def _adamw_body(w_ref, g_ref, m_ref, v_ref, d_ref, nm_ref, nv_ref):
    gg = g_ref[...]
    nm = B1 * m_ref[...] + (1.0 - B1) * gg
    nv = B2 * v_ref[...] + (1.0 - B2) * (gg * gg)
    m_hat = nm / (1.0 - B1 ** ASTEP)
    v_hat = nv / (1.0 - B2 ** ASTEP)
    d_ref[...] = -LR * (m_hat / (jnp.sqrt(v_hat) + AEPS) + WD * w_ref[...])
    nm_ref[...] = nm
    nv_ref[...] = nv


def _adamw_whole(w, g, m, v, *, name):
    return pl.pallas_call(_adamw_body_copy(), name=name, out_shape=[jax.ShapeDtypeStruct(w.shape, F32)] * 3,
                          compiler_params=_params())(w, g, m, v)


def _adamw_body_copy():
    def body(*refs):
        _adamw_body(*refs)
    return body


def _mod_cols(c_all, w_ada, b_cols, *, name):
    nl, D, cols = w_ada.shape
    B = c_all.shape[0]

    def body(c_ref, w_ref, b_ref, o_ref):
        cc = c_ref[...]
        cs = (cc * _sigmoid(cc)).astype(BF16)
        o_ref[0] = jnp.dot(cs, w_ref[0].astype(BF16), preferred_element_type=F32) + b_ref[0]

    return pl.pallas_call(
        body, name=name, grid=(nl,),
        in_specs=[_whole(c_all.shape), pl.BlockSpec((1, D, cols), lambda i: (i, 0, 0)), pl.BlockSpec((1, 1, cols), lambda i: (i, 0, 0))],
        out_specs=pl.BlockSpec((1, B, cols), lambda i: (i, 0, 0)),
        out_shape=jax.ShapeDtypeStruct((nl, B, cols), F32), compiler_params=_params("arbitrary"),
    )(c_all, w_ada, b_cols)


def _ada_grad(c_all, dmod_cols, *, name):
    nl, B, cols = dmod_cols.shape
    D = c_all.shape[1]

    def body(c_ref, d_ref, o_ref):
        cc = c_ref[...]
        cs = (cc * _sigmoid(cc)).astype(BF16)
        o_ref[0] = lax.dot_general(cs, d_ref[0].astype(BF16), TN, preferred_element_type=F32)

    return pl.pallas_call(
        body, name=name, grid=(nl,),
        in_specs=[_whole(c_all.shape), pl.BlockSpec((1, B, cols), lambda i: (i, 0, 0))],
        out_specs=pl.BlockSpec((1, D, cols), lambda i: (i, 0, 0)),
        out_shape=jax.ShapeDtypeStruct((nl, D, cols), F32), compiler_params=_params("arbitrary"),
    )(c_all, dmod_cols)


def _s5_disc(a_re, a_im, log_dt, b_re, b_im):
    dt = jnp.exp(log_dt)[:, None]
    mag = jnp.exp(a_re * dt)
    ph = a_im * dt
    lb_re = mag * jnp.cos(ph)
    lb_im = mag * jnp.sin(ph)
    den = a_re * a_re + a_im * a_im
    nr = lb_re - 1.0
    ni = lb_im
    f_re = (nr * a_re + ni * a_im) / den
    f_im = (ni * a_re - nr * a_im) / den
    bb_re = f_re[..., None] * b_re - f_im[..., None] * b_im
    bb_im = f_re[..., None] * b_im + f_im[..., None] * b_re
    return lb_re, lb_im, bb_re, bb_im


def _to_segments(t):
    L, D = t.shape
    R = min(S5_R, L)
    return t.reshape(L // R, NSEG, R // NSEG, D).transpose(0, 2, 1, 3).reshape(L, D)


def _from_segments(t):
    L, D = t.shape
    R = min(S5_R, L)
    return t.reshape(L // R, R // NSEG, NSEG, D).transpose(0, 2, 1, 3).reshape(L, D)


def _mlp_fwd(h2, w1, w2, tag):
    a = _matmul(h2, w1, name=f"ff1_{tag}", tn=2048, out_dtypes=(BF16,), epi=lambda acc: (jnp.maximum(acc, 0.0),))
    f = _matmul(a, w2, name=f"ff2_{tag}", a_fn=jnp.square)
    return a, f


def _mlp_bwd(df, h2, a, w1, w2, tag):
    da = _matmul(df, w2, tb=True, name=f"ff2_dx_{tag}", tn=2048, out_dtypes=(BF16,), epi_ins=(a,),
                 epi=lambda acc, at: (acc * (2.0 * at.astype(F32)),))
    dw2 = _matmul(a, df, ta=True, name=f"ff2_dw_{tag}", a_fn=jnp.square)
    dh2 = _matmul(da, w1, tb=True, name=f"ff1_dx_{tag}")
    dw1 = _matmul(h2, da, ta=True, name=f"ff1_dw_{tag}", col_shards=NCH)
    return dh2, dw1, dw2


def kernel(x, c, w_ada, b_ada, norm_mix, norm_mlp, s5_a_re, s5_a_im, s5_log_dt, s5_b_re, s5_b_im, s5_c_re, s5_c_im, s5_d, s5_w_glu, gla_w_in, gla_w_gate2, gla_b_gate, gla_g_norm, gla_w_out, w_ff1, w_ff2, norm_final, loss_target, m_w_ada, m_b_ada, m_norm_mix, m_norm_mlp, m_s5_a_re, m_s5_a_im, m_s5_log_dt, m_s5_b_re, m_s5_b_im, m_s5_c_re, m_s5_c_im, m_s5_d, m_s5_w_glu, m_gla_w_in, m_gla_w_gate2, m_gla_b_gate, m_gla_g_norm, m_gla_w_out, m_w_ff1, m_w_ff2, m_norm_final, v_w_ada, v_b_ada, v_norm_mix, v_norm_mlp, v_s5_a_re, v_s5_a_im, v_s5_log_dt, v_s5_b_re, v_s5_b_im, v_s5_c_re, v_s5_c_im, v_s5_d, v_s5_w_glu, v_gla_w_in, v_gla_w_gate2, v_gla_b_gate, v_gla_g_norm, v_gla_w_out, v_w_ff1, v_w_ff2, v_norm_final):
    args = dict(locals())
    L, D = x.shape[1], x.shape[2]
    QK = D // 2
    xi, yi, ci = _coords()
    chip = 2 * xi + yi
    dev = 2 * chip + ci

    cat = jnp.concatenate([gla_w_gate2[0].reshape(1, -1), gla_b_gate, gla_g_norm], axis=1)
    first = _exchange(jnp.concatenate([c.reshape(8, D // 8), jnp.tile(cat, (8, 1))], axis=1), MASK_ALL, name="gather_c")
    c_all = first[:, :, :D // 8].reshape(8, D)
    cat_all = first[0::2, 0, D // 8:]
    acols = w_ada.shape[2]
    b_cols = lax.dynamic_slice_in_dim(b_ada, chip * acols, acols, axis=1)[:, None, :]
    mod_cols = _mod_cols(c_all, w_ada, b_cols, name="ada_mod")
    mod_all = _exchange(mod_cols.reshape(16, acols), MASK_CHIPS, name="gather_mod")
    mod_all = mod_all.reshape(NCH, 2, 8, acols).transpose(1, 2, 0, 3).reshape(2, 8, NCH * acols)
    mod = lax.dynamic_index_in_dim(mod_all, dev, axis=1, keepdims=False).reshape(2, 6, 1, D)

    big = [("s5_w_glu", s5_w_glu, 0, "col"), ("gla_w_in", gla_w_in, 0, "slot"), ("gla_w_out", gla_w_out, 0, "row"),
           ("w_ff1_0", w_ff1, 0, "col"), ("w_ff1_1", w_ff1, 1, "col"), ("w_ff2_0", w_ff2, 0, "row"), ("w_ff2_1", w_ff2, 1, "row")]
    own16 = [_cast_into(t, lead, kind, chip, name=f"cast_{nm}") for nm, t, lead, kind in big]
    wshapes, wkinds = [b[1].shape[-2:] for b in big], [b[3] for b in big]
    wgroups = [[0, 3, 5], [1, 2, 4, 6]]
    wsems, wthru, wtoken = _gw_start(own16, wshapes, wkinds, wgroups, name="gather_w_start")
    W = {}

    def finish_weights(g, after):
        idx = wgroups[g]
        shp, knd = [wshapes[p] for p in idx], [wkinds[p] for p in idx]
        got = _gw_wait([wthru[p] for p in idx], shp, knd, wsems[g], after, name=f"gather_w_wait{g}")
        for p, w in zip(idx, _gw_forward(got, shp, knd, name=f"gather_w_cores{g}")):
            W[big[p][0]] = w

    qk4 = QK // NCH
    wg2 = cat_all[:, :GATE_RANK * qk4].reshape(NCH, GATE_RANK, qk4).transpose(1, 0, 2).reshape(GATE_RANK, QK)
    bg = cat_all[:, GATE_RANK * qk4:(GATE_RANK + 1) * qk4].reshape(1, QK)
    gn = cat_all[:, (GATE_RANK + 1) * qk4:].reshape(1, D)
    wg2p = jnp.concatenate([wg2, jnp.zeros((128 - GATE_RANK, QK), F32)], axis=0).astype(BF16)

    lb_re, lb_im, bb_re, bb_im = _s5_disc(s5_a_re[0], s5_a_im[0], s5_log_dt[0], s5_b_re[0], s5_b_im[0])
    tb = _s5_tables(lb_re, lb_im, bb_re, bb_im, s5_c_re[0], s5_c_im[0], min(S5_R, L) // NSEG)
    s5_dv = s5_d + wtoken[0, 0]

    def vec(t):
        return t.reshape(1, -1)

    m0, m1 = mod[0], mod[1]
    xp = _to_segments(x[0])
    (u0,) = _rows(lambda t, g, sc, sh: (_norm_mod(t, g, sc, sh),), [xp], [vec(norm_mix[0]), m0[1], m0[0]],
                  [(D, F32)], [], name="pre_mix0")
    y0, z0, ck0 = _s5_fwd(u0, tb, s5_dv, name="s5_fwd")
    finish_weights(0, z0)
    vg0 = _matmul(z0, W["s5_w_glu"], name="glu", tn=2048)

    def res_glu_pre(xt, vgt, gt, g, sc, sh):
        xn = xt + gt * (vgt[:, :D] * _sigmoid(vgt[:, D:]))
        return xn, _norm_mod(xn, g, sc, sh)

    x2_0, h2_0 = _rows(res_glu_pre, [xp, vg0], [m0[2], vec(norm_mlp[0]), m0[4], m0[3]], [(D, F32), (D, BF16)], [],
                       name="res_mix0")
    a_0, f0 = _mlp_fwd(h2_0, W["w_ff1_0"], W["w_ff2_0"], "0")

    def res_pre(xt, bt, gt, g, sc, sh):
        xn = xt + gt * bt
        return xn, _norm_mod(xn, g, sc, sh)

    x3p, h1p = _rows(res_pre, [x2_0, f0], [m0[5], vec(norm_mix[1]), m1[1], m1[0]], [(D, F32), (D, BF16)], [],
                     name="res_mlp0")
    x3, h1 = _from_segments(x3p), _from_segments(h1p)
    finish_weights(1, f0)
    w_in = W["gla_w_in"].transpose(1, 0, 2).reshape(D, -1)
    w_in_r = jnp.concatenate([w_in[:, :4 * QK], w_in[:, 4 * QK + GATE_RANK:], w_in[:, 4 * QK:4 * QK + GATE_RANK],
                              jnp.zeros((D, 128 - GATE_RANK), BF16)], axis=1)
    proj = _matmul(h1, w_in_r, name="gla_in", tn=640)
    og, states = _gla_fwd(proj, wg2p, bg, gn, name="gla_fwd")
    ymix = _matmul(og, W["gla_w_out"], name="gla_out")
    x2_1, h2_1 = _rows(res_pre, [x3, ymix], [m1[2], vec(norm_mlp[1]), m1[4], m1[3]], [(D, F32), (D, BF16)], [],
                       name="res_mix1")
    a_1, f1 = _mlp_fwd(h2_1, W["w_ff1_1"], W["w_ff2_1"], "1")

    def final(xt, ft, tgt, gt, g):
        xn = xt + gt * ft
        rs = lax.rsqrt(jnp.mean(xn * xn, axis=-1, keepdims=True) + EPS)
        xh = xn * rs
        e = xh * g - tgt
        dout = e * (1.0 / D)
        dxh = dout * g
        dx = rs * (dxh - xh * jnp.mean(dxh * xh, axis=-1, keepdims=True))
        lsum = 0.5 * jnp.sum(jnp.sum(e * e, axis=-1, keepdims=True), axis=0, keepdims=True) * (1.0 / D)
        return dx, dx * gt, jnp.broadcast_to(lsum, (1, 128)), _rsum(dout * xh), _rsum(dx * ft)

    dx, df1, loss_part, d_norm_final, dgt2_1 = _rows(
        final, [x2_1, f1, loss_target[0]], [m1[5], vec(norm_final)], [(D, F32), (D, BF16)],
        [(1, 128), (1, D), (1, D)], name="loss_head")
    loss = lax.psum(loss_part[0, 0], ("x", "y", "c"))

    def gate_bwd(dxt, bt, gt):
        return dxt * gt, _rsum(dxt * bt)

    def norm_bwd(xt, dht, drt, g, sc):
        dxn, dsh, dsc, dg = _norm_mod_bwd(xt, dht, g, sc)
        return drt + dxn, dsh, dsc, dg

    def norm_gate_bwd(xt, dht, drt, bt, g, sc, gt):
        dxn, dsh, dsc, dg = _norm_mod_bwd(xt, dht, g, sc)
        dxt = drt + dxn
        return dxt, dxt * gt, dsh, dsc, dg, _rsum(dxt * bt)

    vD = [(1, D)]
    dh2_1, dw_ff1_1, dw_ff2_1 = _mlp_bwd(df1, h2_1, a_1, W["w_ff1_1"], W["w_ff2_1"], "1")
    dx, dmix1, dsh2_1, dsc2_1, dg_mlp1, dgt1_1 = _rows(
        norm_gate_bwd, [x2_1, dh2_1, dx, ymix], [vec(norm_mlp[1]), m1[4], m1[2]], [(D, F32), (D, BF16)], vD * 4,
        name="norm_mlp1_bwd")
    dog = _matmul(dmix1, W["gla_w_out"], tb=True, name="gla_out_dx")
    dw_out = _matmul(og, dmix1, ta=True, name="gla_out_dw")
    dproj, dwg2p, dbg, dgn = _gla_bwd(proj, dog, states, wg2p, bg, gn, name="gla_bwd")
    dh1 = _matmul(dproj, w_in_r, tb=True, name="gla_in_dx", tk=3200)
    dw_in_r = _matmul(h1, dproj, ta=True, name="gla_in_dw", tn=640)
    dx, dsh1_1, dsc1_1, dg_mix1 = _rows(norm_bwd, [x3, dh1, dx], [vec(norm_mix[1]), m1[1]], [(D, F32)], vD * 3,
                                        name="norm_mix1_bwd")
    dxp = _to_segments(dx)
    tags = [b[0] for b in big] + ["small"]
    rs_groups = []

    def rs_chips_begin(idx, srcs, r1, gname):
        s1 = [_sum_own_half(g, r, ci, F32 if tags[k] == "small" else BF16, name=f"rs_sum_cores_{tags[k]}")
              for g, r, k in zip(srcs, r1, idx)]
        pair, parts, lands, token = _rs_chips_start(s1, name=f"rs_chips_start_{gname}")
        rs_groups.append((idx, pair, parts, lands))
        return token

    def rs_begin(idx, srcs, gname):
        return rs_chips_begin(idx, srcs, _rs_cores(srcs, name=f"rs_cores_{gname}"), gname)

    dw_in = jnp.concatenate([dw_in_r[:, :4 * QK], dw_in_r[:, 6 * QK:6 * QK + GATE_RANK], dw_in_r[:, 4 * QK:6 * QK]], axis=1)
    dw_in = dw_in.reshape(D, NCH, -1).transpose(1, 0, 2)
    idx1 = [1, 2, 4, 6]
    pair1, src1, land1, tok1 = _rs_cores_start(
        [dw_in, dw_out.reshape(NCH, -1, D), dw_ff1_1, dw_ff2_1.reshape(NCH, -1, D)], name="rs_cores_start_l1")

    df0, dgt2_0 = _rows(gate_bwd, [dxp, f0], [m0[5] + tok1[0, 0]], [(D, BF16)], vD, name="gate_mlp0")
    dh2_0, dw_ff1_0, dw_ff2_0 = _mlp_bwd(df0, h2_0, a_0, W["w_ff1_0"], W["w_ff2_0"], "0")
    src1, land1 = _rs_cores_wait(pair1, src1, land1, dh2_0, name="rs_cores_wait_l1")
    tok1b = rs_chips_begin(idx1, src1, land1, "l1")
    idx0 = [3, 5]
    pair0, src0, land0, tok0 = _rs_cores_start([dw_ff1_0, dw_ff2_0.reshape(NCH, -1, D)], name="rs_cores_start_l0")
    tok2 = tok1b + tok0

    def norm_glu_bwd(xt, dht, drt, vgt, g, sc, gt):
        dxn, dsh, dsc, dg = _norm_mod_bwd(xt, dht, g, sc)
        dxt = drt + dxn
        val, sg = vgt[:, :D], _sigmoid(vgt[:, D:])
        dbr = dxt * gt
        dvg = jnp.concatenate([dbr * sg, dbr * val * sg * (1.0 - sg)], axis=1)
        return dxt, dvg, dsh, dsc, dg, _rsum(dxt * val * sg)

    dxp, dvg0, dsh2_0, dsc2_0, dg_mlp0, dgt1_0 = _rows(
        norm_glu_bwd, [x2_0, dh2_0, dxp, vg0], [vec(norm_mlp[0]), m0[4] + tok2[0, 0], m0[2]], [(D, F32), (2 * D, BF16)],
        vD * 4, name="norm_mlp0_bwd")
    dz0 = _matmul(dvg0, W["s5_w_glu"], tb=True, name="glu_dx")
    dw_glu = _matmul(z0, dvg0, ta=True, name="glu_dw", tn=512, col_shards=NCH)
    src0, land0 = _rs_cores_wait(pair0, src0, land0, dw_glu, name="rs_cores_wait_l0")
    tok0b = rs_chips_begin(idx0 + [0], src0 + [dw_glu], land0 + list(_rs_cores([dw_glu], name="rs_cores_glu")), "l0")
    du0, db_acc, dc_acc, dl_acc, dd_s5 = _s5_bwd(u0, y0, dz0, ck0, tb, s5_dv + tok0b[0, 0], name="s5_bwd")
    dxp, dsh1_0, dsc1_0, dg_mix0 = _rows(norm_bwd, [xp, du0, dxp], [vec(norm_mix[0]), m0[1]], [(D, F32)], vD * 3,
                                         name="norm_mix0_bwd")
    grad_x = _from_segments(dxp)[None]

    dmod = jnp.concatenate([dsh1_0, dsc1_0, dgt1_0, dsh2_0, dsc2_0, dgt2_0,
                            dsh1_1, dsc1_1, dgt1_1, dsh2_1, dsc2_1, dgt2_1], axis=1)
    dbb_re, dbb_im = _s5_untable(db_acc)
    dc_re, dc_im_neg = _s5_untable(dc_acc)
    nbk = D // 128
    dl = dl_acc.reshape(nbk, NSEG, 2, GPB * S5_P).sum(axis=1)
    smalls = [dmod, dg_mix0, dg_mix1, dg_mlp0, dg_mlp1, d_norm_final, dd_s5, dbg, dgn,
              dwg2p[:GATE_RANK].reshape(1, -1), dbb_re.reshape(1, -1), dbb_im.reshape(1, -1),
              dc_re.reshape(1, -1), dc_im_neg.reshape(1, -1), dl.reshape(1, -1)]
    ssz = [t.shape[1] for t in smalls]
    stot = sum(ssz)
    spad = -(-stot // 8192) * 8192
    svec = jnp.concatenate(smalls + [jnp.zeros((1, spad - stot), F32)], axis=1).reshape(NCH, spad // (128 * NCH), 128)

    dmod_all = _exchange(dmod.reshape(12 * D // 128, 128), MASK_ALL, name="gather_dmod").reshape(8, 2, 6 * D)
    dmod_cols = lax.dynamic_slice_in_dim(dmod_all, chip * acols, acols, axis=2).transpose(1, 0, 2)
    g_w_ada = _ada_grad(c_all, dmod_cols, name="ada_grad")

    rs_begin([7], [svec], "last")
    landed = _rs_chips_wait([(g[1], g[2], g[3]) for g in rs_groups], grad_x, name="rs_chips_wait")
    s1, r2 = {}, {}
    for (idx, _, _, _), (parts, lands) in zip(rs_groups, landed):
        for k, part, land in zip(idx, parts, lands):
            s1[k], r2[k] = part, land

    def fin(k, **kw):
        return _sum_chips(r2[k], s1[k], chip, ci, name=f"rs_sum_chips_{tags[k]}", **kw)

    f_ff1 = fin(4, nlead=2, lead=1, prev=fin(3, nlead=2, lead=0))
    f_ff2 = fin(6, nlead=2, lead=1, prev=fin(5, nlead=2, lead=0))
    finals = [fin(0), fin(1), fin(2), f_ff1, f_ff2, fin(7, spread=True)]
    halves = [t.shape[1] for t in (s1[0], s1[1], s1[2], s1[3], s1[5], s1[7])]
    g_glu, g_in, g_out, g_w_ff1, g_w_ff2, s_own = _rs_gather(finals, halves, [False] * 5 + [True], name="rs_gather_cores")
    srows = spad // (128 * NCH)
    (s_sum,) = _gather_weights([s_own.reshape(NCH * srows, 128)], [(srows, 128)], ["row"], name="gather_small_grads")
    s_sum = s_sum.reshape(-1)
    so = [sum(ssz[:k]) for k in range(len(ssz))]
    sm = [s_sum[o:o + n] for o, n in zip(so, ssz)]
    (dmod_s, g_mix0, g_mix1, g_mlp0, g_mlp1, g_nf, g_d, g_bg, g_gn, g_wg2, g_bbre, g_bbim, g_cre, g_cimn, g_dl) = sm
    g_b_ada = dmod_s.reshape(2, 6 * D)

    G = D // S5_H
    _, disc_vjp = jax.vjp(_s5_disc, s5_a_re[0], s5_a_im[0], s5_log_dt[0], s5_b_re[0], s5_b_im[0])
    g_dl = g_dl.reshape(nbk, 2, GPB, S5_P)
    ct = (g_dl[:, 0].reshape(G, S5_P), g_dl[:, 1].reshape(G, S5_P),
          g_bbre.reshape(G, S5_H, S5_P).transpose(0, 2, 1), g_bbim.reshape(G, S5_H, S5_P).transpose(0, 2, 1))
    g_a_re, g_a_im, g_log_dt, g_b_re, g_b_im = disc_vjp(ct)
    g_c_re = g_cre.reshape(G, S5_H, S5_P)
    g_c_im = -g_cimn.reshape(G, S5_H, S5_P)
    g_wg2_s = lax.dynamic_slice_in_dim(g_wg2.reshape(GATE_RANK, QK), chip * qk4, qk4, axis=1)
    g_bg_s = lax.dynamic_slice_in_dim(g_bg.reshape(1, QK), chip * qk4, qk4, axis=1)
    g_gn_s = lax.dynamic_slice_in_dim(g_gn.reshape(1, D), chip * (D // NCH), D // NCH, axis=1)

    grads = dict(
        w_ada=g_w_ada, b_ada=g_b_ada, norm_mix=jnp.stack([g_mix0, g_mix1]), norm_mlp=jnp.stack([g_mlp0, g_mlp1]),
        s5_a_re=g_a_re[None], s5_a_im=g_a_im[None], s5_log_dt=g_log_dt[None], s5_b_re=g_b_re[None], s5_b_im=g_b_im[None],
        s5_c_re=g_c_re[None], s5_c_im=g_c_im[None], s5_d=g_d[None], s5_w_glu=g_glu,
        gla_w_in=g_in, gla_w_gate2=g_wg2_s[None], gla_b_gate=g_bg_s, gla_g_norm=g_gn_s,
        gla_w_out=g_out, w_ff1=g_w_ff1, w_ff2=g_w_ff2, norm_final=g_nf)

    names = list(grads)
    large = ("w_ada", "s5_w_glu", "gla_w_in", "gla_w_out", "w_ff1", "w_ff2")
    delta, new_m, new_v = {}, {}, {}
    for nm in large:
        delta[nm], new_m[nm], new_v[nm] = _adamw(args[nm], grads[nm], args["m_" + nm], args["v_" + nm], name=f"adamw_{nm}")
    grads = {nm: grads[nm].reshape(args[nm].shape) for nm in names}
    for nm in names:
        if nm not in large:
            shp = args[nm].shape
            as2d = (1, -1) if len(shp) == 1 else shp
            outs = _adamw_whole(*[t.reshape(as2d) for t in (args[nm], grads[nm], args["m_" + nm], args["v_" + nm])],
                                name=f"adamw_{nm}")
            delta[nm], new_m[nm], new_v[nm] = (t.reshape(shp) for t in outs)
    return (loss, grad_x, *[grads[n] for n in names], *[delta[n] for n in names], *[new_m[n] for n in names],
            *[new_v[n] for n in names])
```

```python
import math

import jax
import jax.numpy as jnp
from jax import lax
from jax.experimental import pallas as pl
from jax.experimental.pallas import tpu as pltpu

F32 = jnp.float32
BF16 = jnp.bfloat16
MESH = pl.DeviceIdType.MESH

EPS = 1e-6
CHUNK = 64
GLA_NB = 4
S5_H = 16
S5_P = 64
GPB = 8
NSEG = 8
HEADS = 4
GATE_RANK = 16
GATE_TAU = 16.0
NCH = 4
LR, B1, B2, AEPS, WD, ASTEP = 0.001, 0.9, 0.999, 1e-08, 0.01, 10
VMEM_LIMIT = 56 << 20
ROW_SUB = 64

MASK_CHIPS = ((1, 0, 0), (0, 1, 0), (1, 1, 0))
MASK_ALL = ((0, 0, 1), (0, 1, 0), (0, 1, 1), (1, 0, 0), (1, 0, 1), (1, 1, 0), (1, 1, 1))


def _params(*sem):
    return pltpu.CompilerParams(dimension_semantics=sem or None, vmem_limit_bytes=VMEM_LIMIT)


def _tile_rows(rows, cap=512):
    best = 8
    for t in range(8, cap + 1, 8):
        if rows % t == 0:
            best = t
    return best


def _whole(shape):
    return pl.BlockSpec(shape, lambda i, _n=len(shape): (0,) * _n)


def _matmul(a, b, *, name, ta=False, tb=False, tm=1024, tn=1024, tk=4096, out_dtypes=(F32,),
            a_fn=None, epi=None, epi_ins=(), col_shards=1):
    M, K = (a.shape[1], a.shape[0]) if ta else a.shape
    N = b.shape[0] if tb else b.shape[1]
    tm, tn, tk = min(tm, M), min(tn, N), min(tk, K)
    assert M % tm == 0 and N % tn == 0 and K % tk == 0, (name, M, N, K)
    nk = K // tk
    ne = len(epi_ins)
    dn = (((0 if ta else 1,), (1 if tb else 0,)), ((), ()))

    def body(a_ref, b_ref, *rest):
        e_refs, o_refs = rest[:ne], rest[ne:ne + len(out_dtypes)]
        at = a_ref[...]
        if a_fn is not None:
            at = a_fn(at)
        part = lax.dot_general(at.astype(BF16), b_ref[...].astype(BF16), dn, preferred_element_type=F32)

        def finish(total):
            outs = (total,) if epi is None else epi(total, *[r[...] for r in e_refs])
            for r, o in zip(o_refs, outs):
                r[...] = o.astype(r.dtype)

        if nk == 1:
            finish(part)
            return
        acc = rest[-1]
        k = pl.program_id(2)

        @pl.when(k == 0)
        def _():
            acc[...] = part

        @pl.when(k > 0)
        def _():
            acc[...] += part

        @pl.when(k == nk - 1)
        def _():
            finish(acc[...])

    a_spec = pl.BlockSpec((tk, tm), lambda i, j, k: (k, i)) if ta else pl.BlockSpec((tm, tk), lambda i, j, k: (i, k))
    b_spec = pl.BlockSpec((tn, tk), lambda i, j, k: (j, k)) if tb else pl.BlockSpec((tk, tn), lambda i, j, k: (k, j))
    o_spec = pl.BlockSpec((tm, tn), lambda i, j, k: (i, j))
    if col_shards > 1:
        per = N // col_shards // tn
        assert ne == 0 and per * tn * col_shards == N
        w_spec = pl.BlockSpec((None, tm, tn), lambda i, j, k: (j // per, i, j % per))
        o_shape = (col_shards, M, N // col_shards)
    else:
        w_spec, o_shape = o_spec, (M, N)
    outs = pl.pallas_call(
        body, name=name, grid=(M // tm, N // tn, nk),
        in_specs=[a_spec, b_spec] + [o_spec] * ne,
        out_specs=[w_spec] * len(out_dtypes),
        out_shape=[jax.ShapeDtypeStruct(o_shape, d) for d in out_dtypes],
        scratch_shapes=[pltpu.VMEM((tm, tn), F32)] if nk > 1 else [],
        compiler_params=_params("parallel", "parallel", "arbitrary"),
    )(a, b, *epi_ins)
    return outs[0] if len(outs) == 1 else outs


def _rows(fn, rows_in, vecs_in, rows_out, acc_out, *, name, tm=512):
    L = rows_in[0].shape[0]
    tm = min(tm, L)
    assert L % tm == 0
    nr, nv, no, na = len(rows_in), len(vecs_in), len(rows_out), len(acc_out)

    sub = ROW_SUB if tm % ROW_SUB == 0 else tm

    def body(*refs):
        rin, vin = refs[:nr], refs[nr:nr + nv]
        rout, aout = refs[nr + nv:nr + nv + no], refs[nr + nv + no:]
        if na:
            @pl.when(pl.program_id(0) == 0)
            def _():
                for r in aout:
                    r[...] = jnp.zeros_like(r)

        vecs = [v[...] for v in vin]
        sums = None
        for s in range(tm // sub):
            rows = pl.ds(s * sub, sub)
            outs = fn(*[r[rows, :] for r in rin], *vecs)
            for r, o in zip(rout, outs[:no]):
                r[rows, :] = o.astype(r.dtype)
            sums = list(outs[no:]) if sums is None else [t + o for t, o in zip(sums, outs[no:])]
        for r, t in zip(aout, sums):
            r[...] += t

    outs = pl.pallas_call(
        body, name=name, grid=(L // tm,),
        in_specs=[pl.BlockSpec((tm, r.shape[1]), lambda i: (i, 0)) for r in rows_in] + [_whole(v.shape) for v in vecs_in],
        out_specs=[pl.BlockSpec((tm, c), lambda i: (i, 0)) for c, _ in rows_out] + [_whole(s) for s in acc_out],
        out_shape=[jax.ShapeDtypeStruct((L, c), d) for c, d in rows_out] + [jax.ShapeDtypeStruct(s, F32) for s in acc_out],
        compiler_params=_params("arbitrary"),
    )(*rows_in, *vecs_in)
    return outs


def _rsum(t):
    return jnp.sum(t, axis=0, keepdims=True)


def _norm_mod(x, g, sc, sh):
    rs = lax.rsqrt(jnp.mean(x * x, axis=-1, keepdims=True) + EPS)
    return x * rs * g * (1.0 + sc) + sh


def _norm_mod_bwd(x, dh, g, sc):
    rs = lax.rsqrt(jnp.mean(x * x, axis=-1, keepdims=True) + EPS)
    xh = x * rs
    dn = dh * (1.0 + sc)
    dxh = dn * g
    dx = rs * (dxh - xh * jnp.mean(dxh * xh, axis=-1, keepdims=True))
    return dx, _rsum(dh), _rsum(dh * xh * g), _rsum(dn * xh)


def _sigmoid(x):
    return jax.nn.sigmoid(x)


def _gelu(y):
    return jax.nn.gelu(y, approximate=True)


def _gelu_grad(y):
    c = math.sqrt(2.0 / math.pi)
    t = jnp.tanh(c * (y + 0.044715 * y * y * y))
    return 0.5 * (1.0 + t) + 0.5 * y * (1.0 - t * t) * c * (1.0 + 3.0 * 0.044715 * y * y)


def _s5_tables(lb_re, lb_im, bb_re, bb_im, c_re, c_im, seg_len):
    G = lb_re.shape[0]
    nb = G // GPB
    eye = jnp.eye(GPB, dtype=F32)

    def bdiag(t):
        a, b = t.shape[1:]
        t = t.reshape(nb, GPB, a, b)
        return (t[:, :, :, None, :] * eye[None, :, None, :, None]).reshape(nb, GPB * a, GPB * b)

    bbd = jnp.concatenate([bdiag(bb_re.transpose(0, 2, 1)), bdiag(bb_im.transpose(0, 2, 1))], axis=2)
    cbd = jnp.concatenate([bdiag(c_re.transpose(0, 2, 1)), -bdiag(c_im.transpose(0, 2, 1))], axis=1)

    def lanes(re, im):
        t = jnp.concatenate([re.reshape(nb, GPB * S5_P), im.reshape(nb, GPB * S5_P)], axis=1)
        return jnp.repeat(t, NSEG, axis=0)

    tr, ti = lb_re, lb_im
    for _ in range(int(math.log2(seg_len))):
        tr, ti = tr * tr - ti * ti, 2.0 * tr * ti
    return dict(bbd=bbd.astype(BF16), bbdT=bbd.transpose(0, 2, 1).astype(BF16), cbd=cbd.astype(BF16),
                cbdT=cbd.transpose(0, 2, 1).astype(BF16), lam=lanes(lb_re, lb_im), lamT=lanes(tr, ti))


def _s5_untable(acc):
    nb = acc.shape[0]
    t = acc.reshape(nb, GPB, S5_H, 2, GPB, S5_P)
    d = jnp.diagonal(t, axis1=1, axis2=4)
    d = d.transpose(0, 4, 2, 1, 3).reshape(nb * GPB, 2, S5_H, S5_P)
    return d[:, 0], d[:, 1]


S5_R = 256


def _s5_carries(ends, first, t_ref, rws, SW, cfx, *, reverse):
    er, ei = ends
    tr, ti = t_ref[rws, :SW][0:1], t_ref[rws, SW:][0:1]
    cr, ci = first
    order = range(NSEG - 1, -1, -1) if reverse else range(NSEG)
    for n, s in enumerate(order):
        if n > 0:
            p = s + 1 if reverse else s - 1
            if reverse:
                cr, ci = tr * cr + ti * ci + er[p:p + 1], tr * ci - ti * cr + ei[p:p + 1]
            else:
                cr, ci = tr * cr - ti * ci + er[p:p + 1], tr * ci + ti * cr + ei[p:p + 1]
        cfx[s:s + 1, :SW] = cr
        cfx[s:s + 1, SW:] = ci


def _s5_fwd(up, tb, dvec, *, name):
    L, D = up.shape
    R = min(S5_R, L)
    nb, ta, ngb = L // R, R // NSEG, D // 128
    SW = GPB * S5_P
    crows = ngb * NSEG

    def body(u_ref, lam_ref, t_ref, b_ref, c_ref, d_ref, y_ref, z_ref, ck_ref, carry, xbuf2, cfx2):
        @pl.when(pl.program_id(0) == 0)
        def _():
            carry[...] = jnp.zeros_like(carry)

        zero = jnp.zeros((NSEG, SW), F32)
        for g0 in range(0, ngb, 2):
            pair = (g0, g0 + 1)
            xb = [xbuf2.at[g % 4] for g in pair]
            cf = [cfx2.at[g % 4] for g in pair]
            cols = [slice(g * 128, (g + 1) * 128) for g in pair]
            rws = [slice(g * NSEG, (g + 1) * NSEG) for g in pair]
            ug = [u_ref[:, cols[q]] for q in range(2)]
            for q in range(2):
                xb[q][...] = jnp.dot(ug[q].astype(BF16), b_ref[pair[q]], preferred_element_type=F32)
            lam = [(lam_ref[rws[q], :SW], lam_ref[rws[q], SW:]) for q in range(2)]

            def scan(c, store, xb=xb, lam=lam):
                c = list(c)
                for a in range(ta):
                    o = slice(a * NSEG, (a + 1) * NSEG)
                    for q in range(2):
                        (lr, li), (cr, ci) = lam[q], c[q]
                        nr = lr * cr - li * ci + xb[q][o, :SW]
                        ni = lr * ci + li * cr + xb[q][o, SW:]
                        if store:
                            xb[q][o, :SW] = nr
                            xb[q][o, SW:] = ni
                        c[q] = (nr, ni)
                return c

            ends = scan([(zero, zero)] * 2, False)
            for q in range(2):
                prev = (carry[rws[q], :SW][NSEG - 1:NSEG], carry[rws[q], SW:][NSEG - 1:NSEG])
                _s5_carries(ends[q], prev, t_ref, rws[q], SW, cf[q], reverse=False)
                ck_ref[0, rws[q], :] = cf[q][...]
            fin = scan([(cf[q][:, :SW], cf[q][:, SW:]) for q in range(2)], True)
            for q in range(2):
                carry[rws[q], :SW] = fin[q][0]
                carry[rws[q], SW:] = fin[q][1]
            for q in range(2):
                y = (jnp.dot(xb[q][...].astype(BF16), c_ref[pair[q]], preferred_element_type=F32)
                     + d_ref[:, cols[q]] * ug[q])
                y_ref[:, cols[q]] = y
                z_ref[:, cols[q]] = _gelu(y).astype(BF16)

    rowblk = pl.BlockSpec((R, D), lambda i: (i, 0))
    return pl.pallas_call(
        body, name=name, grid=(nb,),
        in_specs=[rowblk, _whole(tb["lam"].shape), _whole(tb["lamT"].shape), _whole(tb["bbd"].shape),
                  _whole(tb["cbd"].shape), _whole(dvec.shape)],
        out_specs=[rowblk, rowblk, pl.BlockSpec((1, crows, 2 * SW), lambda i: (i, 0, 0))],
        out_shape=[jax.ShapeDtypeStruct((L, D), F32), jax.ShapeDtypeStruct((L, D), BF16),
                   jax.ShapeDtypeStruct((nb, crows, 2 * SW), F32)],
        scratch_shapes=[pltpu.VMEM((crows, 2 * SW), F32), pltpu.VMEM((4, R, 2 * SW), F32),
                        pltpu.VMEM((4, NSEG, 2 * SW), F32)],
        compiler_params=_params("arbitrary"),
    )(up, tb["lam"], tb["lamT"], tb["bbd"], tb["cbd"], dvec)


def _s5_bwd(up, y, dz, ck, tb, dvec, *, name):
    L, D = up.shape
    R = min(S5_R, L)
    nb, ta, ngb = L // R, R // NSEG, D // 128
    SW = GPB * S5_P
    crows = ngb * NSEG

    def body(u_ref, y_ref, dz_ref, ck_ref, lam_ref, t_ref, b_ref, bt_ref, ct_ref, d_ref,
             du_ref, db_ref, dc_ref, dl_ref, dd_ref, gcarry, xbuf2, gbuf2, dybuf, cfx2):
        @pl.when(pl.program_id(0) == 0)
        def _():
            gcarry[...] = jnp.zeros_like(gcarry)
            db_ref[...] = jnp.zeros_like(db_ref)
            dc_ref[...] = jnp.zeros_like(dc_ref)
            dl_ref[...] = jnp.zeros_like(dl_ref)
            dd_ref[...] = jnp.zeros_like(dd_ref)

        zero = jnp.zeros((NSEG, SW), F32)
        dybuf[...] = dz_ref[...] * _gelu_grad(y_ref[...])
        for g0 in range(0, ngb, 2):
            pair = (g0, g0 + 1)
            xb = [xbuf2.at[g % 4] for g in pair]
            gbf = [gbuf2.at[g % 4] for g in pair]
            cf = [cfx2.at[g % 4] for g in pair]
            cols = [slice(g * 128, (g + 1) * 128) for g in pair]
            rws = [slice(g * NSEG, (g + 1) * NSEG) for g in pair]
            dyg = [dybuf[:, cols[q]] for q in range(2)]
            ug = [u_ref[:, cols[q]] for q in range(2)]
            lam = [(lam_ref[rws[q], :SW], lam_ref[rws[q], SW:]) for q in range(2)]
            for q in range(2):
                gbf[q][...] = jnp.dot(dyg[q].astype(BF16), ct_ref[pair[q]], preferred_element_type=F32)
                xb[q][0:NSEG, :] = ck_ref[0, rws[q], :]
                xb[q][NSEG:, :] = jnp.dot(ug[q].astype(BF16), b_ref[pair[q]], preferred_element_type=F32)

            c = [(xb[q][0:NSEG, :SW], xb[q][0:NSEG, SW:]) for q in range(2)]
            for a in range(ta):
                o = slice((a + 1) * NSEG, (a + 2) * NSEG)
                for q in range(2):
                    (lr, li), (cr, ci) = lam[q], c[q]
                    nr = lr * cr - li * ci + xb[q][o, :SW]
                    ni = lr * ci + li * cr + xb[q][o, SW:]
                    xb[q][o, :SW] = nr
                    xb[q][o, SW:] = ni
                    c[q] = (nr, ni)

            def rscan(c, store, qs=(0, 1), xb=xb, gbf=gbf, lam=lam):
                c = list(c)
                for a in range(ta - 1, -1, -1):
                    o = slice(a * NSEG, (a + 1) * NSEG)
                    for q in qs:
                        lr, li = lam[q]
                        gr = gbf[q][o, :SW] + lr * c[q][0] + li * c[q][1]
                        gi = gbf[q][o, SW:] - li * c[q][0] + lr * c[q][1]
                        if store:
                            gbf[q][o, :SW] = gr
                            gbf[q][o, SW:] = gi
                            xr, xi = xb[q][o, :SW], xb[q][o, SW:]
                            c[q] = (gr, gi, c[q][2] + gr * xr + gi * xi, c[q][3] + gi * xr - gr * xi)
                        else:
                            c[q] = (gr, gi)
                return c

            gends = rscan([(zero, zero)] * 2, False)
            for q in range(2):
                nxt = (gcarry[rws[q], :SW][0:1], gcarry[rws[q], SW:][0:1])
                _s5_carries(gends[q], nxt, t_ref, rws[q], SW, cf[q], reverse=True)
            fin = [(cf[q][:, :SW], cf[q][:, SW:], zero, zero) for q in range(2)]
            for q in range(2):
                fin = rscan(fin, True, qs=(q,))
            for q in range(2):
                gcarry[rws[q], :SW] = fin[q][0]
                gcarry[rws[q], SW:] = fin[q][1]
                dl_ref[rws[q], :SW] += fin[q][2]
                dl_ref[rws[q], SW:] += fin[q][3]
            gb16 = [gbf[q][...].astype(BF16) for q in range(2)]
            for q in range(2):
                du_ref[:, cols[q]] = (jnp.dot(gb16[q], bt_ref[pair[q]], preferred_element_type=F32)
                                      + d_ref[:, cols[q]] * dyg[q])
            for q in range(2):
                db_ref[pair[q]] += lax.dot_general(ug[q].astype(BF16), gb16[q], TN, preferred_element_type=F32)
            for q in range(2):
                dc_ref[pair[q]] += lax.dot_general(dyg[q].astype(BF16), xb[q][NSEG:, :].astype(BF16), TN,
                                                   preferred_element_type=F32)
                dd_ref[:, cols[q]] += _rsum(dyg[q] * ug[q])

    rev = pl.BlockSpec((R, D), lambda i: (nb - 1 - i, 0))
    acc3 = (ngb, 128, 2 * SW)
    return pl.pallas_call(
        body, name=name, grid=(nb,),
        in_specs=[rev, rev, rev, pl.BlockSpec((1, crows, 2 * SW), lambda i: (nb - 1 - i, 0, 0)),
                  _whole(tb["lam"].shape), _whole(tb["lamT"].shape), _whole(tb["bbd"].shape),
                  _whole(tb["bbdT"].shape), _whole(tb["cbdT"].shape), _whole(dvec.shape)],
        out_specs=[rev, _whole(acc3), _whole(acc3), _whole((crows, 2 * SW)), _whole((1, D))],
        out_shape=[jax.ShapeDtypeStruct((L, D), F32), jax.ShapeDtypeStruct(acc3, F32), jax.ShapeDtypeStruct(acc3, F32),
                   jax.ShapeDtypeStruct((crows, 2 * SW), F32), jax.ShapeDtypeStruct((1, D), F32)],
        scratch_shapes=[pltpu.VMEM((crows, 2 * SW), F32), pltpu.VMEM((4, R + NSEG, 2 * SW), F32),
                        pltpu.VMEM((4, R, 2 * SW), F32), pltpu.VMEM((R, D), F32), pltpu.VMEM((4, NSEG, 2 * SW), F32)],
        compiler_params=_params("arbitrary"),
    )(up, y, dz, ck, tb["lam"], tb["lamT"], tb["bbd"], tb["bbdT"], tb["cbdT"], dvec)


NN = (((1,), (0,)), ((), ()))
TN = (((0,), (0,)), ((), ()))
NT = (((1,), (1,)), ((), ()))


def _dot3(lhs, rhs, dn, split):
    x = rhs if split == "rhs" else lhs
    hi = x.astype(BF16)
    r1 = x - hi.astype(F32)
    mid = r1.astype(BF16)
    lo = (r1 - mid.astype(F32)).astype(BF16)
    out = None
    for part in (hi, mid, lo):
        ops = (lhs, part) if split == "rhs" else (part, rhs)
        t = lax.dot_general(ops[0], ops[1], dn, preferred_element_type=F32)
        out = t if out is None else out + t
    return out


def _log_sigmoid(x):
    return jnp.minimum(x, 0.0) - jnp.log(1.0 + jnp.exp(-jnp.abs(x)))


def _chunk_tri(rows, upper):
    r = lax.broadcasted_iota(jnp.int32, (rows, rows), 0)
    c = lax.broadcasted_iota(jnp.int32, (rows, rows), 1)
    same = (r // CHUNK) == (c // CHUNK)
    return (same & ((c >= r) if upper else (r >= c))).astype(BF16)


def _gla_block_gates(p_ref, wg_ref, bg_ref, QK, wbuf, gebuf):
    RB = p_ref.shape[0]
    glr = p_ref[:, 6 * QK:6 * QK + 128].astype(BF16)
    gpre = jnp.dot(glr, wg_ref[...], preferred_element_type=F32) + bg_ref[...]
    la = _log_sigmoid(gpre) * (1.0 / GATE_TAU)
    gc = _dot3(_chunk_tri(RB, False), la, NN, "rhs")
    for cc in range(RB // CHUNK):
        rows = slice(cc * CHUNK, (cc + 1) * CHUNK)
        ge = gc[(cc + 1) * CHUNK - 1:(cc + 1) * CHUNK, :]
        gebuf[cc:cc + 1, :] = ge
        wbuf[rows, :] = jnp.exp(ge - gc[rows, :])
    return glr, gpre, la


def _as_column(row, lanes):
    t = jnp.transpose(jnp.broadcast_to(row, (row.shape[1], row.shape[1])))
    return jnp.concatenate([t] * (lanes // row.shape[1]), axis=1)


def _as_row(col):
    return jnp.transpose(jnp.broadcast_to(col, (col.shape[0], col.shape[0])))[0:1, :]


def _gla_fwd(proj, wg2p, bg, gn, *, name):
    L = proj.shape[0]
    QK = wg2p.shape[1]
    DK, DV = QK // HEADS, 2 * QK // HEADS
    nC = L // CHUNK
    NB = min(GLA_NB, nC)
    assert nC % NB == 0
    scale = DK ** -0.5

    def body(p_ref, wg_ref, bg_ref, gn_ref, og_ref, s_ref, sst, wbuf, gebuf):
        @pl.when(pl.program_id(0) == 0)
        def _():
            sst[...] = jnp.zeros_like(sst)

        _gla_block_gates(p_ref, wg_ref, bg_ref, QK, wbuf, gebuf)
        heads = range(HEADS)
        ks = [slice(h * DK, (h + 1) * DK) for h in heads]
        vs = [slice(h * DV, (h + 1) * DV) for h in heads]
        for cc in range(NB):
            rows = slice(cc * CHUNK, (cc + 1) * CHUNK)
            kv = [lax.dot_general((p_ref[rows, QK + h * DK:QK + (h + 1) * DK] * wbuf[rows, ks[h]]).astype(BF16),
                                  p_ref[rows, 2 * QK + h * DV:2 * QK + (h + 1) * DV].astype(BF16), TN,
                                  preferred_element_type=F32) for h in heads]
            S = []
            for h in heads:
                S.append(jnp.exp(_as_column(gebuf[cc:cc + 1, ks[h]], DV)) * sst[ks[h], :] + kv[h])
                sst[ks[h], :] = S[h]
                s_ref[cc, ks[h], :] = S[h]
            o = [jnp.dot((p_ref[rows, h * DK:(h + 1) * DK] * scale).astype(BF16), S[h].astype(BF16),
                         preferred_element_type=F32) for h in heads]
            for h in heads:
                r = p_ref[rows, 4 * QK + h * DV:4 * QK + (h + 1) * DV]
                on = o[h] * lax.rsqrt(jnp.mean(o[h] * o[h], axis=-1, keepdims=True) + EPS)
                og_ref[rows, vs[h]] = (on * gn_ref[:, vs[h]] * (r * _sigmoid(r))).astype(BF16)

    RB = NB * CHUNK
    return pl.pallas_call(
        body, name=name, grid=(nC // NB,),
        in_specs=[pl.BlockSpec((RB, proj.shape[1]), lambda i: (i, 0)), _whole(wg2p.shape), _whole(bg.shape), _whole(gn.shape)],
        out_specs=[pl.BlockSpec((RB, 2 * QK), lambda i: (i, 0)), pl.BlockSpec((NB, QK, DV), lambda i: (i, 0, 0))],
        out_shape=[jax.ShapeDtypeStruct((L, 2 * QK), BF16), jax.ShapeDtypeStruct((nC, QK, DV), F32)],
        scratch_shapes=[pltpu.VMEM((QK, DV), F32), pltpu.VMEM((RB, QK), F32), pltpu.VMEM((8, QK), F32)],
        compiler_params=_params("arbitrary"),
    )(proj, wg2p, bg, gn)


def _gla_bwd(proj, dog, states, wg2p, bg, gn, *, name):
    L, W = proj.shape
    QK = wg2p.shape[1]
    DK, DV = QK // HEADS, 2 * QK // HEADS
    nC = L // CHUNK
    NB = min(GLA_NB, nC)
    nB = nC // NB
    scale = DK ** -0.5

    def body(p_ref, dog_ref, sc_ref, sp_ref, wg_ref, bg_ref, gn_ref, dp_ref, dwg_ref, dbg_ref, dgn_ref,
             gst, wbuf, gebuf, dwwbuf, dgebuf):
        i = pl.program_id(0)

        @pl.when(i == 0)
        def _():
            gst[...] = jnp.zeros_like(gst)
            dwg_ref[...] = jnp.zeros_like(dwg_ref)
            dbg_ref[...] = jnp.zeros_like(dbg_ref)
            dgn_ref[...] = jnp.zeros_like(dgn_ref)

        RB = NB * CHUNK
        glr, gpre, _ = _gla_block_gates(p_ref, wg_ref, bg_ref, QK, wbuf, gebuf)
        heads = range(HEADS)
        ks = [slice(h * DK, (h + 1) * DK) for h in heads]
        vs = [slice(h * DV, (h + 1) * DV) for h in heads]
        for cc in range(NB - 1, -1, -1):
            rows = slice(cc * CHUNK, (cc + 1) * CHUNK)
            qs16 = [(p_ref[rows, h * DK:(h + 1) * DK] * scale).astype(BF16) for h in heads]
            S16 = [sc_ref[cc, ks[h], :].astype(BF16) for h in heads]
            o = [jnp.dot(qs16[h], S16[h], preferred_element_type=F32) for h in heads]
            do = []
            for h in heads:
                r = p_ref[rows, 4 * QK + h * DV:4 * QK + (h + 1) * DV]
                rs = lax.rsqrt(jnp.mean(o[h] * o[h], axis=-1, keepdims=True) + EPS)
                on = o[h] * rs
                sr = _sigmoid(r)
                dg = dog_ref[rows, vs[h]]
                gnh = gn_ref[:, vs[h]]
                dp_ref[rows, 4 * QK + h * DV:4 * QK + (h + 1) * DV] = (
                    dg * on * gnh * (sr * (1.0 + r * (1.0 - sr)))).astype(BF16)
                dt = dg * (r * sr)
                dgn_ref[:, vs[h]] += _rsum(dt * on)
                don = dt * gnh
                do.append((rs * (don - on * jnp.mean(don * on, axis=-1, keepdims=True))).astype(BF16))
            Gc = [gst[ks[h], :] + lax.dot_general(qs16[h], do[h], TN, preferred_element_type=F32) for h in heads]
            G16 = [g.astype(BF16) for g in Gc]
            dq = [lax.dot_general(do[h], S16[h], NT, preferred_element_type=F32) for h in heads]
            kd = [p_ref[rows, QK + h * DK:QK + (h + 1) * DK] * wbuf[rows, ks[h]] for h in heads]
            dkd = [lax.dot_general(p_ref[rows, 2 * QK + h * DV:2 * QK + (h + 1) * DV].astype(BF16), G16[h], NT,
                                   preferred_element_type=F32) for h in heads]
            dv = [jnp.dot(kd[h].astype(BF16), G16[h], preferred_element_type=F32) for h in heads]
            for h in heads:
                if cc > 0:
                    Sp = sc_ref[cc - 1, ks[h], :]
                else:
                    Sp = jnp.where(i < nB - 1, sp_ref[0, ks[h], :], 0.0)
                dp_ref[rows, h * DK:(h + 1) * DK] = (dq[h] * scale).astype(BF16)
                dp_ref[rows, 2 * QK + h * DV:2 * QK + (h + 1) * DV] = dv[h].astype(BF16)
                ge = gebuf[cc:cc + 1, ks[h]]
                gst[ks[h], :] = jnp.exp(_as_column(ge, DV)) * Gc[h]
                ddec = _as_row(jnp.sum(Gc[h] * Sp, axis=1, keepdims=True))
                dp_ref[rows, QK + h * DK:QK + (h + 1) * DK] = (dkd[h] * wbuf[rows, ks[h]]).astype(BF16)
                dww = dkd[h] * kd[h]
                dwwbuf[rows, ks[h]] = dww
                dgebuf[cc:cc + 1, ks[h]] = jnp.exp(ge) * ddec + _rsum(dww)
        rev = _dot3(_chunk_tri(RB, True), dwwbuf[...], NN, "rhs")
        for cc in range(NB):
            rows = slice(cc * CHUNK, (cc + 1) * CHUNK)
            wbuf[rows, :] = dgebuf[cc:cc + 1, :] - rev[rows, :]
        dgpre = wbuf[...] * (1.0 / GATE_TAU) * (1.0 - _sigmoid(gpre))
        d16 = dgpre.astype(BF16)
        dp_ref[:, 6 * QK:6 * QK + 128] = lax.dot_general(d16, wg_ref[...], NT, preferred_element_type=F32).astype(BF16)
        dwg_ref[...] += lax.dot_general(glr, d16, TN, preferred_element_type=F32)
        dbg_ref[...] += _rsum(dgpre)

    RB = NB * CHUNK
    rev_idx = lambda i: (nB - 1 - i, 0)
    return pl.pallas_call(
        body, name=name, grid=(nB,),
        in_specs=[pl.BlockSpec((RB, W), rev_idx), pl.BlockSpec((RB, 2 * QK), rev_idx),
                  pl.BlockSpec((NB, QK, DV), lambda i: (nB - 1 - i, 0, 0)),
                  pl.BlockSpec((1, QK, DV), lambda i: (jnp.maximum(NB * (nB - 1 - i) - 1, 0), 0, 0)),
                  _whole(wg2p.shape), _whole(bg.shape), _whole(gn.shape)],
        out_specs=[pl.BlockSpec((RB, W), rev_idx), _whole((128, QK)), _whole((1, QK)), _whole((1, 2 * QK))],
        out_shape=[jax.ShapeDtypeStruct((L, W), BF16), jax.ShapeDtypeStruct((128, QK), F32),
                   jax.ShapeDtypeStruct((1, QK), F32), jax.ShapeDtypeStruct((1, 2 * QK), F32)],
        scratch_shapes=[pltpu.VMEM((QK, DV), F32), pltpu.VMEM((RB, QK), F32), pltpu.VMEM((8, QK), F32),
                        pltpu.VMEM((RB, QK), F32), pltpu.VMEM((8, QK), F32)],
        compiler_params=_params("arbitrary"),
    )(proj, dog, states, states, wg2p, bg, gn)


def _coords():
    return lax.axis_index("x"), lax.axis_index("y"), lax.axis_index("c")


def _other_chips(x, y):
    return [(1 - x, y, 2 * (1 - x) + y), (x, 1 - y, 2 * x + 1 - y), (1 - x, 1 - y, 2 * (1 - x) + 1 - y)]


def _hbm_call(body, ins, out_shapes, n_sems, *, name, alias=False):
    any_spec = pl.BlockSpec(memory_space=pl.ANY)
    return pl.pallas_call(
        body, name=name, in_specs=[any_spec] * len(ins), out_specs=[any_spec] * len(out_shapes), out_shape=out_shapes,
        scratch_shapes=[pltpu.SemaphoreType.DMA((n,)) for n in n_sems],
        input_output_aliases={k: k for k in range(len(ins))} if alias else {},
    )(*ins)


def _exchange(src, masks, *, name):
    vary = [any(m[k] for m in masks) for k in range(3)]
    nslots = 2 ** sum(vary)
    n = len(masks)

    def slot(coords):
        s = 0
        for k in range(3):
            if vary[k]:
                s = s * 2 + coords[k]
        return s

    def body(src_ref, dst_ref, send_sems, recv_sems, loc_sem):
        me = _coords()
        mine = slot(me)
        loc = pltpu.make_async_copy(src_ref, dst_ref.at[mine], loc_sem.at[0])
        loc.start()
        copies = []
        for k, m in enumerate(masks):
            peer = tuple(1 - me[d] if m[d] else me[d] for d in range(3))
            cp = pltpu.make_async_remote_copy(src_ref=src_ref, dst_ref=dst_ref.at[mine], send_sem=send_sems.at[k],
                                              recv_sem=recv_sems.at[k], device_id=peer, device_id_type=MESH)
            cp.start()
            copies.append(cp)
        for cp in copies:
            cp.wait()
        loc.wait()

    return _hbm_call(body, [src], [jax.ShapeDtypeStruct((nslots,) + tuple(src.shape), src.dtype)], (n, n, 1), name=name)[0]


def _cast_into(t, lead, kind, chip, *, name, tm=256):
    r, cc = t.shape[-2:]
    tm = min(tm, r)
    nblk = r // tm
    if kind == "col":
        shp, o_spec = (r, NCH * cc), pl.BlockSpec((tm, cc), lambda i, s: (i, s[0]))
    elif kind == "row":
        shp, o_spec = (NCH * r, cc), pl.BlockSpec((tm, cc), lambda i, s: (s[0] * nblk + i, 0))
    else:
        shp, o_spec = (NCH, r, cc), pl.BlockSpec((None, tm, cc), lambda i, s: (s[0], i, 0))

    def body(s_ref, t_ref, o_ref):
        o_ref[...] = t_ref[...].astype(o_ref.dtype)

    return pl.pallas_call(
        body, name=name,
        grid_spec=pltpu.PrefetchScalarGridSpec(
            num_scalar_prefetch=1, grid=(nblk,),
            in_specs=[pl.BlockSpec((None, tm, cc), lambda i, s: (lead, i, 0))], out_specs=o_spec),
        out_shape=jax.ShapeDtypeStruct(shp, BF16), compiler_params=_params("parallel"),
    )(chip.reshape(1).astype(jnp.int32), t)


def _gather_weights(arrs, shard_shapes, kinds, *, name):
    n = len(arrs)

    def body(*refs):
        dst = refs[n:2 * n]
        send_sems, recv_sems = refs[2 * n:]
        x, y, c = _coords()
        chip = 2 * x + y
        others = _other_chips(x, y)
        sib = (x, y, 1 - c)

        def window(p, chip_id, cc):
            r, cols = shard_shapes[p]
            h = r // 2
            if kinds[p] == "col":
                return dst[p].at[pl.ds(cc * h, h), pl.ds(pl.multiple_of(chip_id * cols, 128), cols)]
            if kinds[p] == "row":
                return dst[p].at[pl.ds(chip_id * r + cc * h, h), :]
            return dst[p].at[chip_id, pl.ds(cc * h, h), :]

        def copy(p, k, win, to):
            return pltpu.make_async_remote_copy(src_ref=win, dst_ref=win, send_sem=send_sems.at[6 * p + k],
                                                recv_sem=recv_sems.at[6 * p + k], device_id=to, device_id_type=MESH)

        sends = []
        for p in range(n):
            for j, (ox, oy, _) in enumerate(others):
                cp = copy(p, j, window(p, chip, c), (ox, oy, c))
                cp.start()
                sends.append(cp)
        for j, (_, _, oc) in enumerate(others):
            for p in range(n):
                copy(p, j, window(p, oc, c), (x, y, c)).wait_recv()
                fw = copy(p, 3 + j, window(p, oc, c), sib)
                fw.start()
                sends.append(fw)
        for p in range(n):
            for j, (_, _, oc) in enumerate(others):
                copy(p, 3 + j, window(p, oc, 1 - c), sib).wait_recv()
        for cp in sends:
            cp.wait_send()

    outs = [jax.ShapeDtypeStruct(a.shape, a.dtype) for a in arrs]
    return _hbm_call(body, arrs, outs, (6 * n, 6 * n), name=name, alias=True)


HBM_SPEC = pl.BlockSpec(memory_space=pltpu.HBM)
SEM_SPEC = pl.BlockSpec(memory_space=pltpu.SEMAPHORE)
EFFECT = pltpu.SideEffectType.DATAFLOW_SIDE_EFFECTING


def _window(ref, shard_shape, kind, chip_id, cc):
    r, cols = shard_shape
    h = r // 2
    if kind == "col":
        return ref.at[pl.ds(cc * h, h), pl.ds(pl.multiple_of(chip_id * cols, 128), cols)]
    if kind == "row":
        return ref.at[pl.ds(chip_id * r + cc * h, h), :]
    return ref.at[chip_id, pl.ds(cc * h, h), :]


def _split_start(start, arrs, n_sems, *, name):
    n, ns = len(arrs), len(n_sems)

    def body(*refs):
        start(refs[:n], refs[n:n + ns])
        refs[-1][...] = jnp.zeros_like(refs[-1])

    outs = pl.pallas_call(
        body, name=name,
        out_shape=tuple([pltpu.SemaphoreType.DMA((k,)) for k in n_sems] + [pltpu.HBM(a.shape, a.dtype) for a in arrs]
                        + [jax.ShapeDtypeStruct((8, 128), F32)]),
        in_specs=[HBM_SPEC] * n, out_specs=tuple([SEM_SPEC] * ns + [HBM_SPEC] * n + [pl.BlockSpec(memory_space=pltpu.VMEM)]),
        input_output_aliases={k: ns + k for k in range(n)},
        compiler_params=pltpu.CompilerParams(has_side_effects=EFFECT),
    )(*[pltpu.with_memory_space_constraint(a, pltpu.HBM) for a in arrs])
    return list(outs[:ns]), list(outs[ns:ns + n]), outs[-1]


def _split_wait(wait, arrs, sems, after, *, name):
    n, ns = len(arrs), len(sems)

    def body(*refs):
        wait(refs[:n], refs[n:n + ns])

    return pl.pallas_call(
        body, name=name, out_shape=tuple(pltpu.HBM(a.shape, a.dtype) for a in arrs),
        in_specs=[HBM_SPEC] * n + [SEM_SPEC] * ns + [pl.BlockSpec(memory_space=pl.ANY)], out_specs=tuple([HBM_SPEC] * n),
        input_output_aliases={k: k for k in range(n)},
        compiler_params=pltpu.CompilerParams(has_side_effects=EFFECT),
    )(*arrs, *sems, after)


def _gw_copies(refs, send_sems, recv_sems, shard_shapes, kinds, outgoing):
    x, y, c = _coords()
    chip = 2 * x + y
    out = []
    for p in range(len(refs)):
        for j, (ox, oy, oc) in enumerate(_other_chips(x, y)):
            win = _window(refs[p], shard_shapes[p], kinds[p], chip if outgoing else oc, c)
            out.append(pltpu.make_async_remote_copy(
                src_ref=win, dst_ref=win, send_sem=send_sems.at[3 * p + j], recv_sem=recv_sems.at[3 * p + j],
                device_id=(ox, oy, c), device_id_type=MESH))
    return out


def _gw_start(arrs, shard_shapes, kinds, groups, *, name):
    def start(refs, sems):
        for g, idx in enumerate(groups):
            for cp in _gw_copies([refs[p] for p in idx], sems[2 * g], sems[2 * g + 1], [shard_shapes[p] for p in idx],
                                 [kinds[p] for p in idx], True):
                cp.start()

    n_sems = [3 * len(idx) for idx in groups for _ in range(2)]
    sems, thru, token = _split_start(start, arrs, n_sems, name=name)
    return [(sems[2 * g], sems[2 * g + 1]) for g in range(len(groups))], thru, token


def _gw_wait(arrs, shard_shapes, kinds, sem_pair, after, *, name):
    def wait(refs, sems):
        for cp in _gw_copies(refs, sems[0], sems[1], shard_shapes, kinds, True):
            cp.wait_send()
        for cp in _gw_copies(refs, sems[0], sems[1], shard_shapes, kinds, False):
            cp.wait_recv()

    return _split_wait(wait, arrs, list(sem_pair), after, name=name)


def _gw_forward(arrs, shard_shapes, kinds, *, name):
    n = len(arrs)

    def body(*refs):
        dst = refs[n:2 * n]
        send_sems, recv_sems = refs[2 * n:]
        x, y, c = _coords()
        sends = []
        for p in range(n):
            for j, (_, _, oc) in enumerate(_other_chips(x, y)):
                win = _window(dst[p], shard_shapes[p], kinds[p], oc, c)
                cp = pltpu.make_async_remote_copy(src_ref=win, dst_ref=win, send_sem=send_sems.at[3 * p + j],
                                                  recv_sem=recv_sems.at[3 * p + j], device_id=(x, y, 1 - c),
                                                  device_id_type=MESH)
                cp.start()
                sends.append(cp)
        for p in range(n):
            for j, (_, _, oc) in enumerate(_other_chips(x, y)):
                win = _window(dst[p], shard_shapes[p], kinds[p], oc, 1 - c)
                pltpu.make_async_remote_copy(src_ref=win, dst_ref=win, send_sem=send_sems.at[3 * p + j],
                                             recv_sem=recv_sems.at[3 * p + j], device_id=(x, y, 1 - c),
                                             device_id_type=MESH).wait_recv()
        for cp in sends:
            cp.wait_send()

    outs = [jax.ShapeDtypeStruct(a.shape, a.dtype) for a in arrs]
    return _hbm_call(body, arrs, outs, (3 * n, 3 * n), name=name, alias=True)


def _rs_chips_copies(parts, lands, send_sems, recv_sems):
    x, y, c = _coords()
    chip = 2 * x + y
    out = []
    for p in range(len(parts)):
        for j, (ox, oy, oc) in enumerate(_other_chips(x, y)):
            out.append(pltpu.make_async_remote_copy(
                src_ref=parts[p].at[oc], dst_ref=lands[p].at[chip], send_sem=send_sems.at[3 * p + j],
                recv_sem=recv_sems.at[3 * p + j], device_id=(ox, oy, c), device_id_type=MESH))
    return out


def _rs_chips_start(parts, *, name):
    n = len(parts)

    def start(refs, sems):
        for cp in _rs_chips_copies(refs[:n], refs[n:], sems[0], sems[1]):
            cp.start()

    lands = [lax.empty(t.shape, t.dtype) for t in parts]
    sems, thru, token = _split_start(start, list(parts) + lands, [3 * n, 3 * n], name=name)
    return (sems[0], sems[1]), thru[:n], thru[n:], token


def _rs_chips_wait(groups, after, *, name):
    sizes = [len(g[1]) for g in groups]
    arrs = [a for g in groups for a in list(g[1]) + list(g[2])]
    sems = [s for g in groups for s in g[0]]

    def wait(refs, sem_refs):
        o = 0
        for k, n in enumerate(sizes):
            for cp in _rs_chips_copies(refs[o:o + n], refs[o + n:o + 2 * n], sem_refs[2 * k], sem_refs[2 * k + 1]):
                cp.wait()
            o += 2 * n

    outs = _split_wait(wait, arrs, sems, after, name=name)
    res, o = [], 0
    for n in sizes:
        res.append((list(outs[o:o + n]), list(outs[o + n:o + 2 * n])))
        o += 2 * n
    return res


def _rs_cores_copies(grads, lands, send_sems, recv_sems):
    x, y, c = _coords()
    out, o = [], 0
    for p in range(len(grads)):
        nsh, h = lands[p].shape[0], lands[p].shape[1]
        for j in range(nsh):
            out.append(pltpu.make_async_remote_copy(
                src_ref=grads[p].at[j, pl.ds((1 - c) * h, h), :], dst_ref=lands[p].at[j],
                send_sem=send_sems.at[o + j], recv_sem=recv_sems.at[o + j], device_id=(x, y, 1 - c), device_id_type=MESH))
        o += nsh
    return out


def _rs_cores_start(grads, *, name):
    n = len(grads)
    tot = sum(g.shape[0] for g in grads)

    def start(refs, sems):
        for cp in _rs_cores_copies(refs[:n], refs[n:], sems[0], sems[1]):
            cp.start()

    lands = [lax.empty((g.shape[0], g.shape[1] // 2, g.shape[2]), g.dtype) for g in grads]
    sems, thru, token = _split_start(start, list(grads) + lands, [tot, tot], name=name)
    return (sems[0], sems[1]), thru[:n], thru[n:], token


def _rs_cores_wait(pair, grads, lands, after, *, name):
    n = len(grads)

    def wait(refs, sems):
        for cp in _rs_cores_copies(refs[:n], refs[n:], sems[0], sems[1]):
            cp.wait()

    outs = _split_wait(wait, list(grads) + list(lands), list(pair), after, name=name)
    return list(outs[:n]), list(outs[n:])


def _rs_cores(grads, *, name):
    n = len(grads)
    outs = [jax.ShapeDtypeStruct((g.shape[0], g.shape[1] // 2, g.shape[2]), g.dtype) for g in grads]

    def body(*refs):
        src, dst = refs[:n], refs[n:2 * n]
        send_sems, recv_sems = refs[2 * n:]
        x, y, c = _coords()
        copies = []
        for p in range(n):
            nsh, r, _ = grads[p].shape
            h = r // 2
            for j in range(nsh):
                cp = pltpu.make_async_remote_copy(
                    src_ref=src[p].at[j, pl.ds((1 - c) * h, h), :], dst_ref=dst[p].at[j],
                    send_sem=send_sems.at[nsh * p + j], recv_sem=recv_sems.at[nsh * p + j],
                    device_id=(x, y, 1 - c), device_id_type=MESH)
                cp.start()
                copies.append(cp)
        for cp in copies:
            cp.wait()

    tot = sum(g.shape[0] for g in grads)
    return _hbm_call(body, grads, outs, (tot, tot), name=name)


def _sum_own_half(full, recv, ci, out_dtype, *, name):
    nsh, h, cols = recv.shape
    tm = h if nsh * h * cols * 4 <= (2 << 20) else _tile_rows(h, 256)
    nblk = h // tm

    def body(c_ref, f_ref, r_ref, o_ref):
        o_ref[...] = (f_ref[...] + r_ref[...]).astype(o_ref.dtype)

    return pl.pallas_call(
        body, name=name,
        grid_spec=pltpu.PrefetchScalarGridSpec(
            num_scalar_prefetch=1, grid=(nsh, nblk),
            in_specs=[pl.BlockSpec((1, tm, cols), lambda j, i, c_ref: (j, c_ref[0] * nblk + i, 0)),
                      pl.BlockSpec((1, tm, cols), lambda j, i, c_ref: (j, i, 0))],
            out_specs=pl.BlockSpec((1, tm, cols), lambda j, i, c_ref: (j, i, 0))),
        out_shape=jax.ShapeDtypeStruct((nsh, h, cols), out_dtype), compiler_params=_params("parallel", "parallel"),
    )(ci.reshape(1).astype(jnp.int32), full, recv)


def _sum_chips(recv, own, chip, ci, *, name, nlead=1, lead=0, prev=None, spread=False):
    nsh, h, cols = recv.shape
    tm = h if nsh * h * cols * 4 <= (2 << 20) else _tile_rows(h, 256)
    nblk = h // tm
    rows_out = 2 * h * (nsh if spread else 1)

    def body(s_ref, r_ref, o_ref, *rest):
        out_ref = rest[-1]
        t = None
        for s in range(nsh):
            v = jnp.where(s_ref[0] == s, o_ref[s], r_ref[s]).astype(F32)
            t = v if t is None else t + v
        out_ref[...] = t

    def out_idx(i, s):
        return (lead, (s[0] * 2 * nblk if spread else 0) + s[1] * nblk + i, 0)

    blk = pl.BlockSpec((nsh, tm, cols), lambda i, s: (0, i, 0))
    ins = [recv, own] + ([prev] if prev is not None else [])
    return pl.pallas_call(
        body, name=name,
        grid_spec=pltpu.PrefetchScalarGridSpec(
            num_scalar_prefetch=1, grid=(nblk,),
            in_specs=[blk, blk] + ([pl.BlockSpec(memory_space=pl.ANY)] if prev is not None else []),
            out_specs=pl.BlockSpec((None, tm, cols), out_idx)),
        out_shape=jax.ShapeDtypeStruct((nlead, rows_out, cols), F32),
        input_output_aliases={3: 0} if prev is not None else {},
        compiler_params=_params("arbitrary"),
    )(jnp.stack([chip, ci]).astype(jnp.int32), *ins)


def _rs_gather(arrs, halves, spread, *, name, nchunk=4):
    n = len(arrs)
    per = [a.shape[0] * nchunk for a in arrs]
    offs = [sum(per[:p]) for p in range(n)]

    def body(*refs):
        dst = refs[n:2 * n]
        send_sems, recv_sems = refs[2 * n:]
        x, y, c = _coords()
        chip = 2 * x + y
        copies = []
        for p in range(n):
            h = halves[p]
            q = h // nchunk
            base = chip * 2 * h if spread[p] else 0
            for l in range(arrs[p].shape[0]):
                for k in range(nchunk):
                    win = dst[p].at[l, pl.ds(base + c * h + k * q, q), :]
                    sem = offs[p] + l * nchunk + k
                    cp = pltpu.make_async_remote_copy(src_ref=win, dst_ref=win, send_sem=send_sems.at[sem],
                                                      recv_sem=recv_sems.at[sem], device_id=(x, y, 1 - c),
                                                      device_id_type=MESH)
                    cp.start()
                    copies.append(cp)
        for cp in copies:
            cp.wait_send()
        for p in range(n):
            h = halves[p]
            q = h // nchunk
            base = chip * 2 * h if spread[p] else 0
            for l in range(arrs[p].shape[0]):
                for k in range(nchunk):
                    win = dst[p].at[l, pl.ds(base + (1 - c) * h + k * q, q), :]
                    sem = offs[p] + l * nchunk + k
                    pltpu.make_async_remote_copy(src_ref=win, dst_ref=win, send_sem=send_sems.at[sem],
                                                 recv_sem=recv_sems.at[sem], device_id=(x, y, 1 - c),
                                                 device_id_type=MESH).wait_recv()

    outs = [jax.ShapeDtypeStruct(a.shape, a.dtype) for a in arrs]
    return _hbm_call(body, arrs, outs, (sum(per), sum(per)), name=name, alias=True)


def _adamw(w, g, m, v, *, name):
    nl, R, C = w.shape
    tm = _tile_rows(R, 256)

    blk = pl.BlockSpec((None, tm, C), lambda l, i: (l, i, 0))
    return pl.pallas_call(
        _adamw_body_copy(), name=name, grid=(nl, R // tm), in_specs=[blk] * 4, out_specs=[blk] * 3,
        out_shape=[jax.ShapeDtypeStruct((nl, R, C), F32)] * 3, compiler_params=_params("parallel", "parallel"),
    )(w, g, m, v)


def _adamw_body(w_ref, g_ref, m_ref, v_ref, d_ref, nm_ref, nv_ref):
    gg = g_ref[...]
    nm = B1 * m_ref[...] + (1.0 - B1) * gg
    nv = B2 * v_ref[...] + (1.0 - B2) * (gg * gg)
    m_hat = nm / (1.0 - B1 ** ASTEP)
    v_hat = nv / (1.0 - B2 ** ASTEP)
    d_ref[...] = -LR * (m_hat / (jnp.sqrt(v_hat) + AEPS) + WD * w_ref[...])
    nm_ref[...] = nm
    nv_ref[...] = nv


def _adamw_whole(w, g, m, v, *, name):
    return pl.pallas_call(_adamw_body_copy(), name=name, out_shape=[jax.ShapeDtypeStruct(w.shape, F32)] * 3,
                          compiler_params=_params())(w, g, m, v)


def _adamw_body_copy():
    def body(*refs):
        _adamw_body(*refs)
    return body


def _mod_cols(c_all, w_ada, b_cols, *, name):
    nl, D, cols = w_ada.shape
    B = c_all.shape[0]

    def body(c_ref, w_ref, b_ref, o_ref):
        cc = c_ref[...]
        cs = (cc * _sigmoid(cc)).astype(BF16)
        o_ref[0] = jnp.dot(cs, w_ref[0].astype(BF16), preferred_element_type=F32) + b_ref[0]

    return pl.pallas_call(
        body, name=name, grid=(nl,),
        in_specs=[_whole(c_all.shape), pl.BlockSpec((1, D, cols), lambda i: (i, 0, 0)), pl.BlockSpec((1, 1, cols), lambda i: (i, 0, 0))],
        out_specs=pl.BlockSpec((1, B, cols), lambda i: (i, 0, 0)),
        out_shape=jax.ShapeDtypeStruct((nl, B, cols), F32), compiler_params=_params("arbitrary"),
    )(c_all, w_ada, b_cols)


def _ada_grad(c_all, dmod_cols, *, name):
    nl, B, cols = dmod_cols.shape
    D = c_all.shape[1]

    def body(c_ref, d_ref, o_ref):
        cc = c_ref[...]
        cs = (cc * _sigmoid(cc)).astype(BF16)
        o_ref[0] = lax.dot_general(cs, d_ref[0].astype(BF16), TN, preferred_element_type=F32)

    return pl.pallas_call(
        body, name=name, grid=(nl,),
        in_specs=[_whole(c_all.shape), pl.BlockSpec((1, B, cols), lambda i: (i, 0, 0))],
        out_specs=pl.BlockSpec((1, D, cols), lambda i: (i, 0, 0)),
        out_shape=jax.ShapeDtypeStruct((nl, D, cols), F32), compiler_params=_params("arbitrary"),
    )(c_all, dmod_cols)


def _s5_disc(a_re, a_im, log_dt, b_re, b_im):
    dt = jnp.exp(log_dt)[:, None]
    mag = jnp.exp(a_re * dt)
    ph = a_im * dt
    lb_re = mag * jnp.cos(ph)
    lb_im = mag * jnp.sin(ph)
    den = a_re * a_re + a_im * a_im
    nr = lb_re - 1.0
    ni = lb_im
    f_re = (nr * a_re + ni * a_im) / den
    f_im = (ni * a_re - nr * a_im) / den
    bb_re = f_re[..., None] * b_re - f_im[..., None] * b_im
    bb_im = f_re[..., None] * b_im + f_im[..., None] * b_re
    return lb_re, lb_im, bb_re, bb_im


def _to_segments(t):
    L, D = t.shape
    R = min(S5_R, L)
    return t.reshape(L // R, NSEG, R // NSEG, D).transpose(0, 2, 1, 3).reshape(L, D)


def _from_segments(t):
    L, D = t.shape
    R = min(S5_R, L)
    return t.reshape(L // R, R // NSEG, NSEG, D).transpose(0, 2, 1, 3).reshape(L, D)


def _mlp_fwd(h2, w1, w2, tag):
    a = _matmul(h2, w1, name=f"ff1_{tag}", tn=2048, out_dtypes=(BF16,), epi=lambda acc: (jnp.maximum(acc, 0.0),))
    f = _matmul(a, w2, name=f"ff2_{tag}", a_fn=jnp.square)
    return a, f


def _mlp_bwd(df, h2, a, w1, w2, tag):
    da = _matmul(df, w2, tb=True, name=f"ff2_dx_{tag}", tn=2048, out_dtypes=(BF16,), epi_ins=(a,),
                 epi=lambda acc, at: (acc * (2.0 * at.astype(F32)),))
    dw2 = _matmul(a, df, ta=True, name=f"ff2_dw_{tag}", a_fn=jnp.square)
    dh2 = _matmul(da, w1, tb=True, name=f"ff1_dx_{tag}")
    dw1 = _matmul(h2, da, ta=True, name=f"ff1_dw_{tag}", col_shards=NCH)
    return dh2, dw1, dw2


def kernel(x, c, w_ada, b_ada, norm_mix, norm_mlp, s5_a_re, s5_a_im, s5_log_dt, s5_b_re, s5_b_im, s5_c_re, s5_c_im, s5_d, s5_w_glu, gla_w_in, gla_w_gate2, gla_b_gate, gla_g_norm, gla_w_out, w_ff1, w_ff2, norm_final, loss_target, m_w_ada, m_b_ada, m_norm_mix, m_norm_mlp, m_s5_a_re, m_s5_a_im, m_s5_log_dt, m_s5_b_re, m_s5_b_im, m_s5_c_re, m_s5_c_im, m_s5_d, m_s5_w_glu, m_gla_w_in, m_gla_w_gate2, m_gla_b_gate, m_gla_g_norm, m_gla_w_out, m_w_ff1, m_w_ff2, m_norm_final, v_w_ada, v_b_ada, v_norm_mix, v_norm_mlp, v_s5_a_re, v_s5_a_im, v_s5_log_dt, v_s5_b_re, v_s5_b_im, v_s5_c_re, v_s5_c_im, v_s5_d, v_s5_w_glu, v_gla_w_in, v_gla_w_gate2, v_gla_b_gate, v_gla_g_norm, v_gla_w_out, v_w_ff1, v_w_ff2, v_norm_final):
    args = dict(locals())
    L, D = x.shape[1], x.shape[2]
    QK = D // 2
    xi, yi, ci = _coords()
    chip = 2 * xi + yi
    dev = 2 * chip + ci

    cat = jnp.concatenate([gla_w_gate2[0].reshape(1, -1), gla_b_gate, gla_g_norm], axis=1)
    first = _exchange(jnp.concatenate([c.reshape(8, D // 8), jnp.tile(cat, (8, 1))], axis=1), MASK_ALL, name="gather_c")
    c_all = first[:, :, :D // 8].reshape(8, D)
    cat_all = first[0::2, 0, D // 8:]
    acols = w_ada.shape[2]
    b_cols = lax.dynamic_slice_in_dim(b_ada, chip * acols, acols, axis=1)[:, None, :]
    mod_cols = _mod_cols(c_all, w_ada, b_cols, name="ada_mod")
    mod_all = _exchange(mod_cols.reshape(16, acols), MASK_CHIPS, name="gather_mod")
    mod_all = mod_all.reshape(NCH, 2, 8, acols).transpose(1, 2, 0, 3).reshape(2, 8, NCH * acols)
    mod = lax.dynamic_index_in_dim(mod_all, dev, axis=1, keepdims=False).reshape(2, 6, 1, D)

    big = [("s5_w_glu", s5_w_glu, 0, "col"), ("gla_w_in", gla_w_in, 0, "slot"), ("gla_w_out", gla_w_out, 0, "row"),
           ("w_ff1_0", w_ff1, 0, "col"), ("w_ff1_1", w_ff1, 1, "col"), ("w_ff2_0", w_ff2, 0, "row"), ("w_ff2_1", w_ff2, 1, "row")]
    own16 = [_cast_into(t, lead, kind, chip, name=f"cast_{nm}") for nm, t, lead, kind in big]
    wshapes, wkinds = [b[1].shape[-2:] for b in big], [b[3] for b in big]
    wgroups = [[0, 3, 5], [1, 2, 4, 6]]
    wsems, wthru, wtoken = _gw_start(own16, wshapes, wkinds, wgroups, name="gather_w_start")
    W = {}

    def finish_weights(g, after):
        idx = wgroups[g]
        shp, knd = [wshapes[p] for p in idx], [wkinds[p] for p in idx]
        got = _gw_wait([wthru[p] for p in idx], shp, knd, wsems[g], after, name=f"gather_w_wait{g}")
        for p, w in zip(idx, _gw_forward(got, shp, knd, name=f"gather_w_cores{g}")):
            W[big[p][0]] = w

    qk4 = QK // NCH
    wg2 = cat_all[:, :GATE_RANK * qk4].reshape(NCH, GATE_RANK, qk4).transpose(1, 0, 2).reshape(GATE_RANK, QK)
    bg = cat_all[:, GATE_RANK * qk4:(GATE_RANK + 1) * qk4].reshape(1, QK)
    gn = cat_all[:, (GATE_RANK + 1) * qk4:].reshape(1, D)
    wg2p = jnp.concatenate([wg2, jnp.zeros((128 - GATE_RANK, QK), F32)], axis=0).astype(BF16)

    lb_re, lb_im, bb_re, bb_im = _s5_disc(s5_a_re[0], s5_a_im[0], s5_log_dt[0], s5_b_re[0], s5_b_im[0])
    tb = _s5_tables(lb_re, lb_im, bb_re, bb_im, s5_c_re[0], s5_c_im[0], min(S5_R, L) // NSEG)
    s5_dv = s5_d + wtoken[0, 0]

    def vec(t):
        return t.reshape(1, -1)

    m0, m1 = mod[0], mod[1]
    xp = _to_segments(x[0])
    (u0,) = _rows(lambda t, g, sc, sh: (_norm_mod(t, g, sc, sh),), [xp], [vec(norm_mix[0]), m0[1], m0[0]],
                  [(D, F32)], [], name="pre_mix0")
    y0, z0, ck0 = _s5_fwd(u0, tb, s5_dv, name="s5_fwd")
    finish_weights(0, z0)
    vg0 = _matmul(z0, W["s5_w_glu"], name="glu", tn=2048)

    def res_glu_pre(xt, vgt, gt, g, sc, sh):
        xn = xt + gt * (vgt[:, :D] * _sigmoid(vgt[:, D:]))
        return xn, _norm_mod(xn, g, sc, sh)

    x2_0, h2_0 = _rows(res_glu_pre, [xp, vg0], [m0[2], vec(norm_mlp[0]), m0[4], m0[3]], [(D, F32), (D, BF16)], [],
                       name="res_mix0")
    a_0, f0 = _mlp_fwd(h2_0, W["w_ff1_0"], W["w_ff2_0"], "0")

    def res_pre(xt, bt, gt, g, sc, sh):
        xn = xt + gt * bt
        return xn, _norm_mod(xn, g, sc, sh)

    x3p, h1p = _rows(res_pre, [x2_0, f0], [m0[5], vec(norm_mix[1]), m1[1], m1[0]], [(D, F32), (D, BF16)], [],
                     name="res_mlp0")
    x3, h1 = _from_segments(x3p), _from_segments(h1p)
    finish_weights(1, f0)
    w_in = W["gla_w_in"].transpose(1, 0, 2).reshape(D, -1)
    w_in_r = jnp.concatenate([w_in[:, :4 * QK], w_in[:, 4 * QK + GATE_RANK:], w_in[:, 4 * QK:4 * QK + GATE_RANK],
                              jnp.zeros((D, 128 - GATE_RANK), BF16)], axis=1)
    proj = _matmul(h1, w_in_r, name="gla_in", tn=640)
    og, states = _gla_fwd(proj, wg2p, bg, gn, name="gla_fwd")
    ymix = _matmul(og, W["gla_w_out"], name="gla_out")
    x2_1, h2_1 = _rows(res_pre, [x3, ymix], [m1[2], vec(norm_mlp[1]), m1[4], m1[3]], [(D, F32), (D, BF16)], [],
                       name="res_mix1")
    a_1, f1 = _mlp_fwd(h2_1, W["w_ff1_1"], W["w_ff2_1"], "1")

    def final(xt, ft, tgt, gt, g):
        xn = xt + gt * ft
        rs = lax.rsqrt(jnp.mean(xn * xn, axis=-1, keepdims=True) + EPS)
        xh = xn * rs
        e = xh * g - tgt
        dout = e * (1.0 / D)
        dxh = dout * g
        dx = rs * (dxh - xh * jnp.mean(dxh * xh, axis=-1, keepdims=True))
        lsum = 0.5 * jnp.sum(jnp.sum(e * e, axis=-1, keepdims=True), axis=0, keepdims=True) * (1.0 / D)
        return dx, dx * gt, jnp.broadcast_to(lsum, (1, 128)), _rsum(dout * xh), _rsum(dx * ft)

    dx, df1, loss_part, d_norm_final, dgt2_1 = _rows(
        final, [x2_1, f1, loss_target[0]], [m1[5], vec(norm_final)], [(D, F32), (D, BF16)],
        [(1, 128), (1, D), (1, D)], name="loss_head")
    loss = lax.psum(loss_part[0, 0], ("x", "y", "c"))

    def gate_bwd(dxt, bt, gt):
        return dxt * gt, _rsum(dxt * bt)

    def norm_bwd(xt, dht, drt, g, sc):
        dxn, dsh, dsc, dg = _norm_mod_bwd(xt, dht, g, sc)
        return drt + dxn, dsh, dsc, dg

    def norm_gate_bwd(xt, dht, drt, bt, g, sc, gt):
        dxn, dsh, dsc, dg = _norm_mod_bwd(xt, dht, g, sc)
        dxt = drt + dxn
        return dxt, dxt * gt, dsh, dsc, dg, _rsum(dxt * bt)

    vD = [(1, D)]
    dh2_1, dw_ff1_1, dw_ff2_1 = _mlp_bwd(df1, h2_1, a_1, W["w_ff1_1"], W["w_ff2_1"], "1")
    dx, dmix1, dsh2_1, dsc2_1, dg_mlp1, dgt1_1 = _rows(
        norm_gate_bwd, [x2_1, dh2_1, dx, ymix], [vec(norm_mlp[1]), m1[4], m1[2]], [(D, F32), (D, BF16)], vD * 4,
        name="norm_mlp1_bwd")
    dog = _matmul(dmix1, W["gla_w_out"], tb=True, name="gla_out_dx")
    dw_out = _matmul(og, dmix1, ta=True, name="gla_out_dw")
    dproj, dwg2p, dbg, dgn = _gla_bwd(proj, dog, states, wg2p, bg, gn, name="gla_bwd")
    dh1 = _matmul(dproj, w_in_r, tb=True, name="gla_in_dx", tk=3200)
    dw_in_r = _matmul(h1, dproj, ta=True, name="gla_in_dw", tn=640)
    dx, dsh1_1, dsc1_1, dg_mix1 = _rows(norm_bwd, [x3, dh1, dx], [vec(norm_mix[1]), m1[1]], [(D, F32)], vD * 3,
                                        name="norm_mix1_bwd")
    dxp = _to_segments(dx)
    tags = [b[0] for b in big] + ["small"]
    rs_groups = []

    def rs_chips_begin(idx, srcs, r1, gname):
        s1 = [_sum_own_half(g, r, ci, F32 if tags[k] == "small" else BF16, name=f"rs_sum_cores_{tags[k]}")
              for g, r, k in zip(srcs, r1, idx)]
        pair, parts, lands, token = _rs_chips_start(s1, name=f"rs_chips_start_{gname}")
        rs_groups.append((idx, pair, parts, lands))
        return token

    def rs_begin(idx, srcs, gname):
        return rs_chips_begin(idx, srcs, _rs_cores(srcs, name=f"rs_cores_{gname}"), gname)

    dw_in = jnp.concatenate([dw_in_r[:, :4 * QK], dw_in_r[:, 6 * QK:6 * QK + GATE_RANK], dw_in_r[:, 4 * QK:6 * QK]], axis=1)
    dw_in = dw_in.reshape(D, NCH, -1).transpose(1, 0, 2)
    idx1 = [1, 2, 4, 6]
    pair1, src1, land1, tok1 = _rs_cores_start(
        [dw_in, dw_out.reshape(NCH, -1, D), dw_ff1_1, dw_ff2_1.reshape(NCH, -1, D)], name="rs_cores_start_l1")

    df0, dgt2_0 = _rows(gate_bwd, [dxp, f0], [m0[5] + tok1[0, 0]], [(D, BF16)], vD, name="gate_mlp0")
    dh2_0, dw_ff1_0, dw_ff2_0 = _mlp_bwd(df0, h2_0, a_0, W["w_ff1_0"], W["w_ff2_0"], "0")
    src1, land1 = _rs_cores_wait(pair1, src1, land1, dh2_0, name="rs_cores_wait_l1")
    tok1b = rs_chips_begin(idx1, src1, land1, "l1")
    idx0 = [3, 5]
    pair0, src0, land0, tok0 = _rs_cores_start([dw_ff1_0, dw_ff2_0.reshape(NCH, -1, D)], name="rs_cores_start_l0")
    tok2 = tok1b + tok0

    def norm_glu_bwd(xt, dht, drt, vgt, g, sc, gt):
        dxn, dsh, dsc, dg = _norm_mod_bwd(xt, dht, g, sc)
        dxt = drt + dxn
        val, sg = vgt[:, :D], _sigmoid(vgt[:, D:])
        dbr = dxt * gt
        dvg = jnp.concatenate([dbr * sg, dbr * val * sg * (1.0 - sg)], axis=1)
        return dxt, dvg, dsh, dsc, dg, _rsum(dxt * val * sg)

    dxp, dvg0, dsh2_0, dsc2_0, dg_mlp0, dgt1_0 = _rows(
        norm_glu_bwd, [x2_0, dh2_0, dxp, vg0], [vec(norm_mlp[0]), m0[4] + tok2[0, 0], m0[2]], [(D, F32), (2 * D, BF16)],
        vD * 4, name="norm_mlp0_bwd")
    dz0 = _matmul(dvg0, W["s5_w_glu"], tb=True, name="glu_dx")
    dw_glu = _matmul(z0, dvg0, ta=True, name="glu_dw", tn=512, col_shards=NCH)
    src0, land0 = _rs_cores_wait(pair0, src0, land0, dw_glu, name="rs_cores_wait_l0")
    tok0b = rs_chips_begin(idx0 + [0], src0 + [dw_glu], land0 + list(_rs_cores([dw_glu], name="rs_cores_glu")), "l0")
    du0, db_acc, dc_acc, dl_acc, dd_s5 = _s5_bwd(u0, y0, dz0, ck0, tb, s5_dv + tok0b[0, 0], name="s5_bwd")
    dxp, dsh1_0, dsc1_0, dg_mix0 = _rows(norm_bwd, [xp, du0, dxp], [vec(norm_mix[0]), m0[1]], [(D, F32)], vD * 3,
                                         name="norm_mix0_bwd")
    grad_x = _from_segments(dxp)[None]

    dmod = jnp.concatenate([dsh1_0, dsc1_0, dgt1_0, dsh2_0, dsc2_0, dgt2_0,
                            dsh1_1, dsc1_1, dgt1_1, dsh2_1, dsc2_1, dgt2_1], axis=1)
    dbb_re, dbb_im = _s5_untable(db_acc)
    dc_re, dc_im_neg = _s5_untable(dc_acc)
    nbk = D // 128
    dl = dl_acc.reshape(nbk, NSEG, 2, GPB * S5_P).sum(axis=1)
    smalls = [dmod, dg_mix0, dg_mix1, dg_mlp0, dg_mlp1, d_norm_final, dd_s5, dbg, dgn,
              dwg2p[:GATE_RANK].reshape(1, -1), dbb_re.reshape(1, -1), dbb_im.reshape(1, -1),
              dc_re.reshape(1, -1), dc_im_neg.reshape(1, -1), dl.reshape(1, -1)]
    ssz = [t.shape[1] for t in smalls]
    stot = sum(ssz)
    spad = -(-stot // 8192) * 8192
    svec = jnp.concatenate(smalls + [jnp.zeros((1, spad - stot), F32)], axis=1).reshape(NCH, spad // (128 * NCH), 128)

    dmod_all = _exchange(dmod.reshape(12 * D // 128, 128), MASK_ALL, name="gather_dmod").reshape(8, 2, 6 * D)
    dmod_cols = lax.dynamic_slice_in_dim(dmod_all, chip * acols, acols, axis=2).transpose(1, 0, 2)
    g_w_ada = _ada_grad(c_all, dmod_cols, name="ada_grad")

    rs_begin([7], [svec], "last")
    landed = _rs_chips_wait([(g[1], g[2], g[3]) for g in rs_groups], grad_x, name="rs_chips_wait")
    s1, r2 = {}, {}
    for (idx, _, _, _), (parts, lands) in zip(rs_groups, landed):
        for k, part, land in zip(idx, parts, lands):
            s1[k], r2[k] = part, land

    def fin(k, **kw):
        return _sum_chips(r2[k], s1[k], chip, ci, name=f"rs_sum_chips_{tags[k]}", **kw)

    f_ff1 = fin(4, nlead=2, lead=1, prev=fin(3, nlead=2, lead=0))
    f_ff2 = fin(6, nlead=2, lead=1, prev=fin(5, nlead=2, lead=0))
    finals = [fin(0), fin(1), fin(2), f_ff1, f_ff2, fin(7, spread=True)]
    halves = [t.shape[1] for t in (s1[0], s1[1], s1[2], s1[3], s1[5], s1[7])]
    g_glu, g_in, g_out, g_w_ff1, g_w_ff2, s_own = _rs_gather(finals, halves, [False] * 5 + [True], name="rs_gather_cores")
    srows = spad // (128 * NCH)
    (s_sum,) = _gather_weights([s_own.reshape(NCH * srows, 128)], [(srows, 128)], ["row"], name="gather_small_grads")
    s_sum = s_sum.reshape(-1)
    so = [sum(ssz[:k]) for k in range(len(ssz))]
    sm = [s_sum[o:o + n] for o, n in zip(so, ssz)]
    (dmod_s, g_mix0, g_mix1, g_mlp0, g_mlp1, g_nf, g_d, g_bg, g_gn, g_wg2, g_bbre, g_bbim, g_cre, g_cimn, g_dl) = sm
    g_b_ada = dmod_s.reshape(2, 6 * D)

    G = D // S5_H
    _, disc_vjp = jax.vjp(_s5_disc, s5_a_re[0], s5_a_im[0], s5_log_dt[0], s5_b_re[0], s5_b_im[0])
    g_dl = g_dl.reshape(nbk, 2, GPB, S5_P)
    ct = (g_dl[:, 0].reshape(G, S5_P), g_dl[:, 1].reshape(G, S5_P),
          g_bbre.reshape(G, S5_H, S5_P).transpose(0, 2, 1), g_bbim.reshape(G, S5_H, S5_P).transpose(0, 2, 1))
    g_a_re, g_a_im, g_log_dt, g_b_re, g_b_im = disc_vjp(ct)
    g_c_re = g_cre.reshape(G, S5_H, S5_P)
    g_c_im = -g_cimn.reshape(G, S5_H, S5_P)
    g_wg2_s = lax.dynamic_slice_in_dim(g_wg2.reshape(GATE_RANK, QK), chip * qk4, qk4, axis=1)
    g_bg_s = lax.dynamic_slice_in_dim(g_bg.reshape(1, QK), chip * qk4, qk4, axis=1)
    g_gn_s = lax.dynamic_slice_in_dim(g_gn.reshape(1, D), chip * (D // NCH), D // NCH, axis=1)

    grads = dict(
        w_ada=g_w_ada, b_ada=g_b_ada, norm_mix=jnp.stack([g_mix0, g_mix1]), norm_mlp=jnp.stack([g_mlp0, g_mlp1]),
        s5_a_re=g_a_re[None], s5_a_im=g_a_im[None], s5_log_dt=g_log_dt[None], s5_b_re=g_b_re[None], s5_b_im=g_b_im[None],
        s5_c_re=g_c_re[None], s5_c_im=g_c_im[None], s5_d=g_d[None], s5_w_glu=g_glu,
        gla_w_in=g_in, gla_w_gate2=g_wg2_s[None], gla_b_gate=g_bg_s, gla_g_norm=g_gn_s,
        gla_w_out=g_out, w_ff1=g_w_ff1, w_ff2=g_w_ff2, norm_final=g_nf)

    names = list(grads)
    large = ("w_ada", "s5_w_glu", "gla_w_in", "gla_w_out", "w_ff1", "w_ff2")
    delta, new_m, new_v = {}, {}, {}
    for nm in large:
        delta[nm], new_m[nm], new_v[nm] = _adamw(args[nm], grads[nm], args["m_" + nm], args["v_" + nm], name=f"adamw_{nm}")
    grads = {nm: grads[nm].reshape(args[nm].shape) for nm in names}
    for nm in names:
        if nm not in large:
            shp = args[nm].shape
            as2d = (1, -1) if len(shp) == 1 else shp
            outs = _adamw_whole(*[t.reshape(as2d) for t in (args[nm], grads[nm], args["m_" + nm], args["v_" + nm])],
                                name=f"adamw_{nm}")
            delta[nm], new_m[nm], new_v[nm] = (t.reshape(shp) for t in outs)
    return (loss, grad_x, *[grads[n] for n in names], *[delta[n] for n in names], *[new_m[n] for n in names],
            *[new_v[n] for n in names])
```

```python
import math

import jax
import jax.numpy as jnp
from jax import lax
from jax.experimental import pallas as pl
from jax.experimental.pallas import tpu as pltpu

F32 = jnp.float32
BF16 = jnp.bfloat16
MESH = pl.DeviceIdType.MESH

EPS = 1e-6
CHUNK = 64
GLA_NB = 4
S5_H = 16
S5_P = 64
GPB = 8
NSEG = 8
HEADS = 4
GATE_RANK = 16
GATE_TAU = 16.0
NCH = 4
LR, B1, B2, AEPS, WD, ASTEP = 0.001, 0.9, 0.999, 1e-08, 0.01, 10
VMEM_LIMIT = 56 << 20
ROW_SUB = 64

MASK_CHIPS = ((1, 0, 0), (0, 1, 0), (1, 1, 0))
MASK_ALL = ((0, 0, 1), (0, 1, 0), (0, 1, 1), (1, 0, 0), (1, 0, 1), (1, 1, 0), (1, 1, 1))


def _params(*sem):
    return pltpu.CompilerParams(dimension_semantics=sem or None, vmem_limit_bytes=VMEM_LIMIT)


def _tile_rows(rows, cap=512):
    best = 8
    for t in range(8, cap + 1, 8):
        if rows % t == 0:
            best = t
    return best


def _whole(shape):
    return pl.BlockSpec(shape, lambda i, _n=len(shape): (0,) * _n)


def _matmul(a, b, *, name, ta=False, tb=False, tm=1024, tn=1024, tk=4096, out_dtypes=(F32,),
            a_fn=None, epi=None, epi_ins=(), col_shards=1):
    M, K = (a.shape[1], a.shape[0]) if ta else a.shape
    N = b.shape[0] if tb else b.shape[1]
    tm, tn, tk = min(tm, M), min(tn, N), min(tk, K)
    assert M % tm == 0 and N % tn == 0 and K % tk == 0, (name, M, N, K)
    nk = K // tk
    ne = len(epi_ins)
    dn = (((0 if ta else 1,), (1 if tb else 0,)), ((), ()))

    def body(a_ref, b_ref, *rest):
        e_refs, o_refs = rest[:ne], rest[ne:ne + len(out_dtypes)]
        at = a_ref[...]
        if a_fn is not None:
            at = a_fn(at)
        part = lax.dot_general(at.astype(BF16), b_ref[...].astype(BF16), dn, preferred_element_type=F32)

        def finish(total):
            outs = (total,) if epi is None else epi(total, *[r[...] for r in e_refs])
            for r, o in zip(o_refs, outs):
                r[...] = o.astype(r.dtype)

        if nk == 1:
            finish(part)
            return
        acc = rest[-1]
        k = pl.program_id(2)

        @pl.when(k == 0)
        def _():
            acc[...] = part

        @pl.when(k > 0)
        def _():
            acc[...] += part

        @pl.when(k == nk - 1)
        def _():
            finish(acc[...])

    a_spec = pl.BlockSpec((tk, tm), lambda i, j, k: (k, i)) if ta else pl.BlockSpec((tm, tk), lambda i, j, k: (i, k))
    b_spec = pl.BlockSpec((tn, tk), lambda i, j, k: (j, k)) if tb else pl.BlockSpec((tk, tn), lambda i, j, k: (k, j))
    o_spec = pl.BlockSpec((tm, tn), lambda i, j, k: (i, j))
    if col_shards > 1:
        per = N // col_shards // tn
        assert ne == 0 and per * tn * col_shards == N
        w_spec = pl.BlockSpec((None, tm, tn), lambda i, j, k: (j // per, i, j % per))
        o_shape = (col_shards, M, N // col_shards)
    else:
        w_spec, o_shape = o_spec, (M, N)
    outs = pl.pallas_call(
        body, name=name, grid=(M // tm, N // tn, nk),
        in_specs=[a_spec, b_spec] + [o_spec] * ne,
        out_specs=[w_spec] * len(out_dtypes),
        out_shape=[jax.ShapeDtypeStruct(o_shape, d) for d in out_dtypes],
        scratch_shapes=[pltpu.VMEM((tm, tn), F32)] if nk > 1 else [],
        compiler_params=_params("parallel", "parallel", "arbitrary"),
    )(a, b, *epi_ins)
    return outs[0] if len(outs) == 1 else outs


def _rows(fn, rows_in, vecs_in, rows_out, acc_out, *, name, tm=512):
    L = rows_in[0].shape[0]
    tm = min(tm, L)
    assert L % tm == 0
    nr, nv, no, na = len(rows_in), len(vecs_in), len(rows_out), len(acc_out)

    sub = ROW_SUB if tm % ROW_SUB == 0 else tm

    def body(*refs):
        rin, vin = refs[:nr], refs[nr:nr + nv]
        rout, aout = refs[nr + nv:nr + nv + no], refs[nr + nv + no:]
        if na:
            @pl.when(pl.program_id(0) == 0)
            def _():
                for r in aout:
                    r[...] = jnp.zeros_like(r)

        vecs = [v[...] for v in vin]
        sums = None
        for s in range(tm // sub):
            rows = pl.ds(s * sub, sub)
            outs = fn(*[r[rows, :] for r in rin], *vecs)
            for r, o in zip(rout, outs[:no]):
                r[rows, :] = o.astype(r.dtype)
            sums = list(outs[no:]) if sums is None else [t + o for t, o in zip(sums, outs[no:])]
        for r, t in zip(aout, sums):
            r[...] += t

    outs = pl.pallas_call(
        body, name=name, grid=(L // tm,),
        in_specs=[pl.BlockSpec((tm, r.shape[1]), lambda i: (i, 0)) for r in rows_in] + [_whole(v.shape) for v in vecs_in],
        out_specs=[pl.BlockSpec((tm, c), lambda i: (i, 0)) for c, _ in rows_out] + [_whole(s) for s in acc_out],
        out_shape=[jax.ShapeDtypeStruct((L, c), d) for c, d in rows_out] + [jax.ShapeDtypeStruct(s, F32) for s in acc_out],
        compiler_params=_params("arbitrary"),
    )(*rows_in, *vecs_in)
    return outs


def _rsum(t):
    return jnp.sum(t, axis=0, keepdims=True)


def _norm_mod(x, g, sc, sh):
    rs = lax.rsqrt(jnp.mean(x * x, axis=-1, keepdims=True) + EPS)
    return x * rs * g * (1.0 + sc) + sh


def _norm_mod_bwd(x, dh, g, sc):
    rs = lax.rsqrt(jnp.mean(x * x, axis=-1, keepdims=True) + EPS)
    xh = x * rs
    dn = dh * (1.0 + sc)
    dxh = dn * g
    dx = rs * (dxh - xh * jnp.mean(dxh * xh, axis=-1, keepdims=True))
    return dx, _rsum(dh), _rsum(dh * xh * g), _rsum(dn * xh)


def _sigmoid(x):
    return jax.nn.sigmoid(x)


def _gelu(y):
    return jax.nn.gelu(y, approximate=True)


def _gelu_grad(y):
    c = math.sqrt(2.0 / math.pi)
    t = jnp.tanh(c * (y + 0.044715 * y * y * y))
    return 0.5 * (1.0 + t) + 0.5 * y * (1.0 - t * t) * c * (1.0 + 3.0 * 0.044715 * y * y)


def _s5_tables(lb_re, lb_im, bb_re, bb_im, c_re, c_im, seg_len):
    G = lb_re.shape[0]
    nb = G // GPB
    eye = jnp.eye(GPB, dtype=F32)

    def bdiag(t):
        a, b = t.shape[1:]
        t = t.reshape(nb, GPB, a, b)
        return (t[:, :, :, None, :] * eye[None, :, None, :, None]).reshape(nb, GPB * a, GPB * b)

    bbd = jnp.concatenate([bdiag(bb_re.transpose(0, 2, 1)), bdiag(bb_im.transpose(0, 2, 1))], axis=2)
    cbd = jnp.concatenate([bdiag(c_re.transpose(0, 2, 1)), -bdiag(c_im.transpose(0, 2, 1))], axis=1)

    def lanes(re, im):
        t = jnp.concatenate([re.reshape(nb, GPB * S5_P), im.reshape(nb, GPB * S5_P)], axis=1)
        return jnp.repeat(t, NSEG, axis=0)

    tr, ti = lb_re, lb_im
    for _ in range(int(math.log2(seg_len))):
        tr, ti = tr * tr - ti * ti, 2.0 * tr * ti
    return dict(bbd=bbd.astype(BF16), bbdT=bbd.transpose(0, 2, 1).astype(BF16), cbd=cbd.astype(BF16),
                cbdT=cbd.transpose(0, 2, 1).astype(BF16), lam=lanes(lb_re, lb_im), lamT=lanes(tr, ti))


def _s5_untable(acc):
    nb = acc.shape[0]
    t = acc.reshape(nb, GPB, S5_H, 2, GPB, S5_P)
    d = jnp.diagonal(t, axis1=1, axis2=4)
    d = d.transpose(0, 4, 2, 1, 3).reshape(nb * GPB, 2, S5_H, S5_P)
    return d[:, 0], d[:, 1]


S5_R = 256


def _s5_carries(ends, first, t_ref, rws, SW, cfx, *, reverse):
    er, ei = ends
    tr, ti = t_ref[rws, :SW][0:1], t_ref[rws, SW:][0:1]
    cr, ci = first
    order = range(NSEG - 1, -1, -1) if reverse else range(NSEG)
    for n, s in enumerate(order):
        if n > 0:
            p = s + 1 if reverse else s - 1
            if reverse:
                cr, ci = tr * cr + ti * ci + er[p:p + 1], tr * ci - ti * cr + ei[p:p + 1]
            else:
                cr, ci = tr * cr - ti * ci + er[p:p + 1], tr * ci + ti * cr + ei[p:p + 1]
        cfx[s:s + 1, :SW] = cr
        cfx[s:s + 1, SW:] = ci


def _s5_fwd(up, tb, dvec, *, name):
    L, D = up.shape
    R = min(S5_R, L)
    nb, ta, ngb = L // R, R // NSEG, D // 128
    SW = GPB * S5_P
    crows = ngb * NSEG

    def body(u_ref, lam_ref, t_ref, b_ref, c_ref, d_ref, y_ref, z_ref, ck_ref, carry, xbuf2, cfx2):
        @pl.when(pl.program_id(0) == 0)
        def _():
            carry[...] = jnp.zeros_like(carry)

        zero = jnp.zeros((NSEG, SW), F32)
        for g0 in range(0, ngb, 2):
            pair = (g0, g0 + 1)
            xb = [xbuf2.at[g % 4] for g in pair]
            cf = [cfx2.at[g % 4] for g in pair]
            cols = [slice(g * 128, (g + 1) * 128) for g in pair]
            rws = [slice(g * NSEG, (g + 1) * NSEG) for g in pair]
            ug = [u_ref[:, cols[q]] for q in range(2)]
            for q in range(2):
                xb[q][...] = jnp.dot(ug[q].astype(BF16), b_ref[pair[q]], preferred_element_type=F32)
            lam = [(lam_ref[rws[q], :SW], lam_ref[rws[q], SW:]) for q in range(2)]

            def scan(c, store, xb=xb, lam=lam):
                c = list(c)
                for a in range(ta):
                    o = slice(a * NSEG, (a + 1) * NSEG)
                    for q in range(2):
                        (lr, li), (cr, ci) = lam[q], c[q]
                        nr = lr * cr - li * ci + xb[q][o, :SW]
                        ni = lr * ci + li * cr + xb[q][o, SW:]
                        if store:
                            xb[q][o, :SW] = nr
                            xb[q][o, SW:] = ni
                        c[q] = (nr, ni)
                return c

            ends = scan([(zero, zero)] * 2, False)
            for q in range(2):
                prev = (carry[rws[q], :SW][NSEG - 1:NSEG], carry[rws[q], SW:][NSEG - 1:NSEG])
                _s5_carries(ends[q], prev, t_ref, rws[q], SW, cf[q], reverse=False)
                ck_ref[0, rws[q], :] = cf[q][...]
            fin = scan([(cf[q][:, :SW], cf[q][:, SW:]) for q in range(2)], True)
            for q in range(2):
                carry[rws[q], :SW] = fin[q][0]
                carry[rws[q], SW:] = fin[q][1]
            for q in range(2):
                y = (jnp.dot(xb[q][...].astype(BF16), c_ref[pair[q]], preferred_element_type=F32)
                     + d_ref[:, cols[q]] * ug[q])
                y_ref[:, cols[q]] = y
                z_ref[:, cols[q]] = _gelu(y).astype(BF16)

    rowblk = pl.BlockSpec((R, D), lambda i: (i, 0))
    return pl.pallas_call(
        body, name=name, grid=(nb,),
        in_specs=[rowblk, _whole(tb["lam"].shape), _whole(tb["lamT"].shape), _whole(tb["bbd"].shape),
                  _whole(tb["cbd"].shape), _whole(dvec.shape)],
        out_specs=[rowblk, rowblk, pl.BlockSpec((1, crows, 2 * SW), lambda i: (i, 0, 0))],
        out_shape=[jax.ShapeDtypeStruct((L, D), F32), jax.ShapeDtypeStruct((L, D), BF16),
                   jax.ShapeDtypeStruct((nb, crows, 2 * SW), F32)],
        scratch_shapes=[pltpu.VMEM((crows, 2 * SW), F32), pltpu.VMEM((4, R, 2 * SW), F32),
                        pltpu.VMEM((4, NSEG, 2 * SW), F32)],
        compiler_params=_params("arbitrary"),
    )(up, tb["lam"], tb["lamT"], tb["bbd"], tb["cbd"], dvec)


def _s5_bwd(up, y, dz, ck, tb, dvec, *, name):
    L, D = up.shape
    R = min(S5_R, L)
    nb, ta, ngb = L // R, R // NSEG, D // 128
    SW = GPB * S5_P
    crows = ngb * NSEG

    def body(u_ref, y_ref, dz_ref, ck_ref, lam_ref, t_ref, b_ref, bt_ref, ct_ref, d_ref,
             du_ref, db_ref, dc_ref, dl_ref, dd_ref, gcarry, xbuf2, gbuf2, dybuf, cfx2):
        @pl.when(pl.program_id(0) == 0)
        def _():
            gcarry[...] = jnp.zeros_like(gcarry)
            db_ref[...] = jnp.zeros_like(db_ref)
            dc_ref[...] = jnp.zeros_like(dc_ref)
            dl_ref[...] = jnp.zeros_like(dl_ref)
            dd_ref[...] = jnp.zeros_like(dd_ref)

        zero = jnp.zeros((NSEG, SW), F32)
        dybuf[...] = dz_ref[...] * _gelu_grad(y_ref[...])
        for g0 in range(0, ngb, 2):
            pair = (g0, g0 + 1)
            xb = [xbuf2.at[g % 4] for g in pair]
            gbf = [gbuf2.at[g % 4] for g in pair]
            cf = [cfx2.at[g % 4] for g in pair]
            cols = [slice(g * 128, (g + 1) * 128) for g in pair]
            rws = [slice(g * NSEG, (g + 1) * NSEG) for g in pair]
            dyg = [dybuf[:, cols[q]] for q in range(2)]
            ug = [u_ref[:, cols[q]] for q in range(2)]
            lam = [(lam_ref[rws[q], :SW], lam_ref[rws[q], SW:]) for q in range(2)]
            for q in range(2):
                gbf[q][...] = jnp.dot(dyg[q].astype(BF16), ct_ref[pair[q]], preferred_element_type=F32)
                xb[q][0:NSEG, :] = ck_ref[0, rws[q], :]
                xb[q][NSEG:, :] = jnp.dot(ug[q].astype(BF16), b_ref[pair[q]], preferred_element_type=F32)

            c = [(xb[q][0:NSEG, :SW], xb[q][0:NSEG, SW:]) for q in range(2)]
            for a in range(ta):
                o = slice((a + 1) * NSEG, (a + 2) * NSEG)
                for q in range(2):
                    (lr, li), (cr, ci) = lam[q], c[q]
                    nr = lr * cr - li * ci + xb[q][o, :SW]
                    ni = lr * ci + li * cr + xb[q][o, SW:]
                    xb[q][o, :SW] = nr
                    xb[q][o, SW:] = ni
                    c[q] = (nr, ni)

            def rscan(c, store, qs=(0, 1), xb=xb, gbf=gbf, lam=lam):
                c = list(c)
                for a in range(ta - 1, -1, -1):
                    o = slice(a * NSEG, (a + 1) * NSEG)
                    for q in qs:
                        lr, li = lam[q]
                        gr = gbf[q][o, :SW] + lr * c[q][0] + li * c[q][1]
                        gi = gbf[q][o, SW:] - li * c[q][0] + lr * c[q][1]
                        if store:
                            gbf[q][o, :SW] = gr
                            gbf[q][o, SW:] = gi
                            xr, xi = xb[q][o, :SW], xb[q][o, SW:]
                            c[q] = (gr, gi, c[q][2] + gr * xr + gi * xi, c[q][3] + gi * xr - gr * xi)
                        else:
                            c[q] = (gr, gi)
                return c

            gends = rscan([(zero, zero)] * 2, False)
            for q in range(2):
                nxt = (gcarry[rws[q], :SW][0:1], gcarry[rws[q], SW:][0:1])
                _s5_carries(gends[q], nxt, t_ref, rws[q], SW, cf[q], reverse=True)
            fin = [(cf[q][:, :SW], cf[q][:, SW:], zero, zero) for q in range(2)]
            for q in range(2):
                fin = rscan(fin, True, qs=(q,))
            for q in range(2):
                gcarry[rws[q], :SW] = fin[q][0]
                gcarry[rws[q], SW:] = fin[q][1]
                dl_ref[rws[q], :SW] += fin[q][2]
                dl_ref[rws[q], SW:] += fin[q][3]
            gb16 = [gbf[q][...].astype(BF16) for q in range(2)]
            for q in range(2):
                du_ref[:, cols[q]] = (jnp.dot(gb16[q], bt_ref[pair[q]], preferred_element_type=F32)
                                      + d_ref[:, cols[q]] * dyg[q])
            for q in range(2):
                db_ref[pair[q]] += lax.dot_general(ug[q].astype(BF16), gb16[q], TN, preferred_element_type=F32)
            for q in range(2):
                dc_ref[pair[q]] += lax.dot_general(dyg[q].astype(BF16), xb[q][NSEG:, :].astype(BF16), TN,
                                                   preferred_element_type=F32)
                dd_ref[:, cols[q]] += _rsum(dyg[q] * ug[q])

    rev = pl.BlockSpec((R, D), lambda i: (nb - 1 - i, 0))
    acc3 = (ngb, 128, 2 * SW)
    return pl.pallas_call(
        body, name=name, grid=(nb,),
        in_specs=[rev, rev, rev, pl.BlockSpec((1, crows, 2 * SW), lambda i: (nb - 1 - i, 0, 0)),
                  _whole(tb["lam"].shape), _whole(tb["lamT"].shape), _whole(tb["bbd"].shape),
                  _whole(tb["bbdT"].shape), _whole(tb["cbdT"].shape), _whole(dvec.shape)],
        out_specs=[rev, _whole(acc3), _whole(acc3), _whole((crows, 2 * SW)), _whole((1, D))],
        out_shape=[jax.ShapeDtypeStruct((L, D), F32), jax.ShapeDtypeStruct(acc3, F32), jax.ShapeDtypeStruct(acc3, F32),
                   jax.ShapeDtypeStruct((crows, 2 * SW), F32), jax.ShapeDtypeStruct((1, D), F32)],
        scratch_shapes=[pltpu.VMEM((crows, 2 * SW), F32), pltpu.VMEM((4, R + NSEG, 2 * SW), F32),
                        pltpu.VMEM((4, R, 2 * SW), F32), pltpu.VMEM((R, D), F32), pltpu.VMEM((4, NSEG, 2 * SW), F32)],
        compiler_params=_params("arbitrary"),
    )(up, y, dz, ck, tb["lam"], tb["lamT"], tb["bbd"], tb["bbdT"], tb["cbdT"], dvec)


NN = (((1,), (0,)), ((), ()))
TN = (((0,), (0,)), ((), ()))
NT = (((1,), (1,)), ((), ()))


def _dot3(lhs, rhs, dn, split):
    x = rhs if split == "rhs" else lhs
    hi = x.astype(BF16)
    r1 = x - hi.astype(F32)
    mid = r1.astype(BF16)
    lo = (r1 - mid.astype(F32)).astype(BF16)
    out = None
    for part in (hi, mid, lo):
        ops = (lhs, part) if split == "rhs" else (part, rhs)
        t = lax.dot_general(ops[0], ops[1], dn, preferred_element_type=F32)
        out = t if out is None else out + t
    return out


def _log_sigmoid(x):
    return jnp.minimum(x, 0.0) - jnp.log(1.0 + jnp.exp(-jnp.abs(x)))


def _chunk_tri(rows, upper):
    r = lax.broadcasted_iota(jnp.int32, (rows, rows), 0)
    c = lax.broadcasted_iota(jnp.int32, (rows, rows), 1)
    same = (r // CHUNK) == (c // CHUNK)
    return (same & ((c >= r) if upper else (r >= c))).astype(BF16)


def _gla_block_gates(p_ref, wg_ref, bg_ref, QK, wbuf, gebuf):
    RB = p_ref.shape[0]
    glr = p_ref[:, 6 * QK:6 * QK + 128].astype(BF16)
    gpre = jnp.dot(glr, wg_ref[...], preferred_element_type=F32) + bg_ref[...]
    la = _log_sigmoid(gpre) * (1.0 / GATE_TAU)
    gc = _dot3(_chunk_tri(RB, False), la, NN, "rhs")
    for cc in range(RB // CHUNK):
        rows = slice(cc * CHUNK, (cc + 1) * CHUNK)
        ge = gc[(cc + 1) * CHUNK - 1:(cc + 1) * CHUNK, :]
        gebuf[cc:cc + 1, :] = ge
        wbuf[rows, :] = jnp.exp(ge - gc[rows, :])
    return glr, gpre, la


def _as_column(row, lanes):
    t = jnp.transpose(jnp.broadcast_to(row, (row.shape[1], row.shape[1])))
    return jnp.concatenate([t] * (lanes // row.shape[1]), axis=1)


def _as_row(col):
    return jnp.transpose(jnp.broadcast_to(col, (col.shape[0], col.shape[0])))[0:1, :]


def _gla_fwd(proj, wg2p, bg, gn, *, name):
    L = proj.shape[0]
    QK = wg2p.shape[1]
    DK, DV = QK // HEADS, 2 * QK // HEADS
    nC = L // CHUNK
    NB = min(GLA_NB, nC)
    assert nC % NB == 0
    scale = DK ** -0.5

    def body(p_ref, wg_ref, bg_ref, gn_ref, og_ref, s_ref, sst, wbuf, gebuf):
        @pl.when(pl.program_id(0) == 0)
        def _():
            sst[...] = jnp.zeros_like(sst)

        _gla_block_gates(p_ref, wg_ref, bg_ref, QK, wbuf, gebuf)
        heads = range(HEADS)
        ks = [slice(h * DK, (h + 1) * DK) for h in heads]
        vs = [slice(h * DV, (h + 1) * DV) for h in heads]
        units = [(cc, h) for cc in range(NB) for h in heads]
        rows = [slice(cc * CHUNK, (cc + 1) * CHUNK) for cc in range(NB)]
        kv = {(cc, h): lax.dot_general(
            (p_ref[rows[cc], QK + h * DK:QK + (h + 1) * DK] * wbuf[rows[cc], ks[h]]).astype(BF16),
            p_ref[rows[cc], 2 * QK + h * DV:2 * QK + (h + 1) * DV].astype(BF16), TN, preferred_element_type=F32)
            for cc, h in units}
        S16 = {}
        for cc, h in units:
            S = jnp.exp(_as_column(gebuf[cc:cc + 1, ks[h]], DV)) * sst[ks[h], :] + kv[cc, h]
            sst[ks[h], :] = S
            s_ref[cc, ks[h], :] = S
            S16[cc, h] = S.astype(BF16)
        o = {(cc, h): jnp.dot((p_ref[rows[cc], h * DK:(h + 1) * DK] * scale).astype(BF16), S16[cc, h],
                              preferred_element_type=F32) for cc, h in units}
        for cc, h in units:
            r = p_ref[rows[cc], 4 * QK + h * DV:4 * QK + (h + 1) * DV]
            on = o[cc, h] * lax.rsqrt(jnp.mean(o[cc, h] * o[cc, h], axis=-1, keepdims=True) + EPS)
            og_ref[rows[cc], vs[h]] = (on * gn_ref[:, vs[h]] * (r * _sigmoid(r))).astype(BF16)

    RB = NB * CHUNK
    return pl.pallas_call(
        body, name=name, grid=(nC // NB,),
        in_specs=[pl.BlockSpec((RB, proj.shape[1]), lambda i: (i, 0)), _whole(wg2p.shape), _whole(bg.shape), _whole(gn.shape)],
        out_specs=[pl.BlockSpec((RB, 2 * QK), lambda i: (i, 0)), pl.BlockSpec((NB, QK, DV), lambda i: (i, 0, 0))],
        out_shape=[jax.ShapeDtypeStruct((L, 2 * QK), BF16), jax.ShapeDtypeStruct((nC, QK, DV), F32)],
        scratch_shapes=[pltpu.VMEM((QK, DV), F32), pltpu.VMEM((RB, QK), F32), pltpu.VMEM((8, QK), F32)],
        compiler_params=_params("arbitrary"),
    )(proj, wg2p, bg, gn)


def _gla_bwd(proj, dog, states, wg2p, bg, gn, *, name):
    L, W = proj.shape
    QK = wg2p.shape[1]
    DK, DV = QK // HEADS, 2 * QK // HEADS
    nC = L // CHUNK
    NB = min(GLA_NB, nC)
    nB = nC // NB
    scale = DK ** -0.5

    def body(p_ref, dog_ref, sc_ref, sp_ref, wg_ref, bg_ref, gn_ref, dp_ref, dwg_ref, dbg_ref, dgn_ref,
             gst, wbuf, gebuf, dwwbuf, dgebuf):
        i = pl.program_id(0)

        @pl.when(i == 0)
        def _():
            gst[...] = jnp.zeros_like(gst)
            dwg_ref[...] = jnp.zeros_like(dwg_ref)
            dbg_ref[...] = jnp.zeros_like(dbg_ref)
            dgn_ref[...] = jnp.zeros_like(dgn_ref)

        RB = NB * CHUNK
        glr, gpre, _ = _gla_block_gates(p_ref, wg_ref, bg_ref, QK, wbuf, gebuf)
        heads = range(HEADS)
        ks = [slice(h * DK, (h + 1) * DK) for h in heads]
        vs = [slice(h * DV, (h + 1) * DV) for h in heads]
        units = [(cc, h) for cc in range(NB) for h in heads]
        rws = [slice(cc * CHUNK, (cc + 1) * CHUNK) for cc in range(NB)]
        qs16 = {(cc, h): (p_ref[rws[cc], h * DK:(h + 1) * DK] * scale).astype(BF16) for cc, h in units}
        S16a = {(cc, h): sc_ref[cc, ks[h], :].astype(BF16) for cc, h in units}
        oa = {u: jnp.dot(qs16[u], S16a[u], preferred_element_type=F32) for u in units}
        doa = {}
        for cc, h in units:
            r = p_ref[rws[cc], 4 * QK + h * DV:4 * QK + (h + 1) * DV]
            o = oa[cc, h]
            rs = lax.rsqrt(jnp.mean(o * o, axis=-1, keepdims=True) + EPS)
            on = o * rs
            sr = _sigmoid(r)
            dg = dog_ref[rws[cc], vs[h]]
            gnh = gn_ref[:, vs[h]]
            dp_ref[rws[cc], 4 * QK + h * DV:4 * QK + (h + 1) * DV] = (
                dg * on * gnh * (sr * (1.0 + r * (1.0 - sr)))).astype(BF16)
            dt = dg * (r * sr)
            dgn_ref[:, vs[h]] += _rsum(dt * on)
            don = dt * gnh
            doa[cc, h] = (rs * (don - on * jnp.mean(don * on, axis=-1, keepdims=True))).astype(BF16)
        dqa = {u: lax.dot_general(doa[u], S16a[u], NT, preferred_element_type=F32) for u in units}
        for cc, h in units:
            dp_ref[rws[cc], h * DK:(h + 1) * DK] = (dqa[cc, h] * scale).astype(BF16)
        for cc in range(NB - 1, -1, -1):
            rows = rws[cc]
            do = [doa[cc, h] for h in heads]
            Gc = [gst[ks[h], :] + lax.dot_general(qs16[cc, h], do[h], TN, preferred_element_type=F32) for h in heads]
            G16 = [g.astype(BF16) for g in Gc]
            kd = [p_ref[rows, QK + h * DK:QK + (h + 1) * DK] * wbuf[rows, ks[h]] for h in heads]
            dkd = [lax.dot_general(p_ref[rows, 2 * QK + h * DV:2 * QK + (h + 1) * DV].astype(BF16), G16[h], NT,
                                   preferred_element_type=F32) for h in heads]
            dv = [jnp.dot(kd[h].astype(BF16), G16[h], preferred_element_type=F32) for h in heads]
            for h in heads:
                if cc > 0:
                    Sp = sc_ref[cc - 1, ks[h], :]
                else:
                    Sp = jnp.where(i < nB - 1, sp_ref[0, ks[h], :], 0.0)
                dp_ref[rows, 2 * QK + h * DV:2 * QK + (h + 1) * DV] = dv[h].astype(BF16)
                ge = gebuf[cc:cc + 1, ks[h]]
                gst[ks[h], :] = jnp.exp(_as_column(ge, DV)) * Gc[h]
                ddec = _as_row(jnp.sum(Gc[h] * Sp, axis=1, keepdims=True))
                dp_ref[rows, QK + h * DK:QK + (h + 1) * DK] = (dkd[h] * wbuf[rows, ks[h]]).astype(BF16)
                dww = dkd[h] * kd[h]
                dwwbuf[rows, ks[h]] = dww
                dgebuf[cc:cc + 1, ks[h]] = jnp.exp(ge) * ddec + _rsum(dww)
        rev = _dot3(_chunk_tri(RB, True), dwwbuf[...], NN, "rhs")
        for cc in range(NB):
            rows = slice(cc * CHUNK, (cc + 1) * CHUNK)
            wbuf[rows, :] = dgebuf[cc:cc + 1, :] - rev[rows, :]
        dgpre = wbuf[...] * (1.0 / GATE_TAU) * (1.0 - _sigmoid(gpre))
        d16 = dgpre.astype(BF16)
        dp_ref[:, 6 * QK:6 * QK + 128] = lax.dot_general(d16, wg_ref[...], NT, preferred_element_type=F32).astype(BF16)
        dwg_ref[...] += lax.dot_general(glr, d16, TN, preferred_element_type=F32)
        dbg_ref[...] += _rsum(dgpre)

    RB = NB * CHUNK
    rev_idx = lambda i: (nB - 1 - i, 0)
    return pl.pallas_call(
        body, name=name, grid=(nB,),
        in_specs=[pl.BlockSpec((RB, W), rev_idx), pl.BlockSpec((RB, 2 * QK), rev_idx),
                  pl.BlockSpec((NB, QK, DV), lambda i: (nB - 1 - i, 0, 0)),
                  pl.BlockSpec((1, QK, DV), lambda i: (jnp.maximum(NB * (nB - 1 - i) - 1, 0), 0, 0)),
                  _whole(wg2p.shape), _whole(bg.shape), _whole(gn.shape)],
        out_specs=[pl.BlockSpec((RB, W), rev_idx), _whole((128, QK)), _whole((1, QK)), _whole((1, 2 * QK))],
        out_shape=[jax.ShapeDtypeStruct((L, W), BF16), jax.ShapeDtypeStruct((128, QK), F32),
                   jax.ShapeDtypeStruct((1, QK), F32), jax.ShapeDtypeStruct((1, 2 * QK), F32)],
        scratch_shapes=[pltpu.VMEM((QK, DV), F32), pltpu.VMEM((RB, QK), F32), pltpu.VMEM((8, QK), F32),
                        pltpu.VMEM((RB, QK), F32), pltpu.VMEM((8, QK), F32)],
        compiler_params=_params("arbitrary"),
    )(proj, dog, states, states, wg2p, bg, gn)


def _coords():
    return lax.axis_index("x"), lax.axis_index("y"), lax.axis_index("c")


def _other_chips(x, y):
    return [(1 - x, y, 2 * (1 - x) + y), (x, 1 - y, 2 * x + 1 - y), (1 - x, 1 - y, 2 * (1 - x) + 1 - y)]


def _hbm_call(body, ins, out_shapes, n_sems, *, name, alias=False):
    any_spec = pl.BlockSpec(memory_space=pl.ANY)
    return pl.pallas_call(
        body, name=name, in_specs=[any_spec] * len(ins), out_specs=[any_spec] * len(out_shapes), out_shape=out_shapes,
        scratch_shapes=[pltpu.SemaphoreType.DMA((n,)) for n in n_sems],
        input_output_aliases={k: k for k in range(len(ins))} if alias else {},
    )(*ins)


def _exchange(src, masks, *, name):
    vary = [any(m[k] for m in masks) for k in range(3)]
    nslots = 2 ** sum(vary)
    n = len(masks)

    def slot(coords):
        s = 0
        for k in range(3):
            if vary[k]:
                s = s * 2 + coords[k]
        return s

    def body(src_ref, dst_ref, send_sems, recv_sems, loc_sem):
        me = _coords()
        mine = slot(me)
        loc = pltpu.make_async_copy(src_ref, dst_ref.at[mine], loc_sem.at[0])
        loc.start()
        copies = []
        for k, m in enumerate(masks):
            peer = tuple(1 - me[d] if m[d] else me[d] for d in range(3))
            cp = pltpu.make_async_remote_copy(src_ref=src_ref, dst_ref=dst_ref.at[mine], send_sem=send_sems.at[k],
                                              recv_sem=recv_sems.at[k], device_id=peer, device_id_type=MESH)
            cp.start()
            copies.append(cp)
        for cp in copies:
            cp.wait()
        loc.wait()

    return _hbm_call(body, [src], [jax.ShapeDtypeStruct((nslots,) + tuple(src.shape), src.dtype)], (n, n, 1), name=name)[0]


def _cast_into(t, lead, kind, chip, *, name, tm=256):
    r, cc = t.shape[-2:]
    tm = min(tm, r)
    nblk = r // tm
    if kind == "col":
        shp, o_spec = (r, NCH * cc), pl.BlockSpec((tm, cc), lambda i, s: (i, s[0]))
    elif kind == "row":
        shp, o_spec = (NCH * r, cc), pl.BlockSpec((tm, cc), lambda i, s: (s[0] * nblk + i, 0))
    else:
        shp, o_spec = (NCH, r, cc), pl.BlockSpec((None, tm, cc), lambda i, s: (s[0], i, 0))

    def body(s_ref, t_ref, o_ref):
        o_ref[...] = t_ref[...].astype(o_ref.dtype)

    return pl.pallas_call(
        body, name=name,
        grid_spec=pltpu.PrefetchScalarGridSpec(
            num_scalar_prefetch=1, grid=(nblk,),
            in_specs=[pl.BlockSpec((None, tm, cc), lambda i, s: (lead, i, 0))], out_specs=o_spec),
        out_shape=jax.ShapeDtypeStruct(shp, BF16), compiler_params=_params("parallel"),
    )(chip.reshape(1).astype(jnp.int32), t)


def _gather_weights(arrs, shard_shapes, kinds, *, name):
    n = len(arrs)

    def body(*refs):
        dst = refs[n:2 * n]
        send_sems, recv_sems = refs[2 * n:]
        x, y, c = _coords()
        chip = 2 * x + y
        others = _other_chips(x, y)
        sib = (x, y, 1 - c)

        def window(p, chip_id, cc):
            r, cols = shard_shapes[p]
            h = r // 2
            if kinds[p] == "col":
                return dst[p].at[pl.ds(cc * h, h), pl.ds(pl.multiple_of(chip_id * cols, 128), cols)]
            if kinds[p] == "row":
                return dst[p].at[pl.ds(chip_id * r + cc * h, h), :]
            return dst[p].at[chip_id, pl.ds(cc * h, h), :]

        def copy(p, k, win, to):
            return pltpu.make_async_remote_copy(src_ref=win, dst_ref=win, send_sem=send_sems.at[6 * p + k],
                                                recv_sem=recv_sems.at[6 * p + k], device_id=to, device_id_type=MESH)

        sends = []
        for p in range(n):
            for j, (ox, oy, _) in enumerate(others):
                cp = copy(p, j, window(p, chip, c), (ox, oy, c))
                cp.start()
                sends.append(cp)
        for j, (_, _, oc) in enumerate(others):
            for p in range(n):
                copy(p, j, window(p, oc, c), (x, y, c)).wait_recv()
                fw = copy(p, 3 + j, window(p, oc, c), sib)
                fw.start()
                sends.append(fw)
        for p in range(n):
            for j, (_, _, oc) in enumerate(others):
                copy(p, 3 + j, window(p, oc, 1 - c), sib).wait_recv()
        for cp in sends:
            cp.wait_send()

    outs = [jax.ShapeDtypeStruct(a.shape, a.dtype) for a in arrs]
    return _hbm_call(body, arrs, outs, (6 * n, 6 * n), name=name, alias=True)


HBM_SPEC = pl.BlockSpec(memory_space=pltpu.HBM)
SEM_SPEC = pl.BlockSpec(memory_space=pltpu.SEMAPHORE)
EFFECT = pltpu.SideEffectType.DATAFLOW_SIDE_EFFECTING


def _window(ref, shard_shape, kind, chip_id, cc):
    r, cols = shard_shape
    h = r // 2
    if kind == "col":
        return ref.at[pl.ds(cc * h, h), pl.ds(pl.multiple_of(chip_id * cols, 128), cols)]
    if kind == "row":
        return ref.at[pl.ds(chip_id * r + cc * h, h), :]
    return ref.at[chip_id, pl.ds(cc * h, h), :]


def _split_start(start, arrs, n_sems, *, name):
    n, ns = len(arrs), len(n_sems)

    def body(*refs):
        start(refs[:n], refs[n:n + ns])
        refs[-1][...] = jnp.zeros_like(refs[-1])

    outs = pl.pallas_call(
        body, name=name,
        out_shape=tuple([pltpu.SemaphoreType.DMA((k,)) for k in n_sems] + [pltpu.HBM(a.shape, a.dtype) for a in arrs]
                        + [jax.ShapeDtypeStruct((8, 128), F32)]),
        in_specs=[HBM_SPEC] * n, out_specs=tuple([SEM_SPEC] * ns + [HBM_SPEC] * n + [pl.BlockSpec(memory_space=pltpu.VMEM)]),
        input_output_aliases={k: ns + k for k in range(n)},
        compiler_params=pltpu.CompilerParams(has_side_effects=EFFECT),
    )(*[pltpu.with_memory_space_constraint(a, pltpu.HBM) for a in arrs])
    return list(outs[:ns]), list(outs[ns:ns + n]), outs[-1]


def _split_wait(wait, arrs, sems, after, *, name):
    n, ns = len(arrs), len(sems)

    def body(*refs):
        wait(refs[:n], refs[n:n + ns])

    return pl.pallas_call(
        body, name=name, out_shape=tuple(pltpu.HBM(a.shape, a.dtype) for a in arrs),
        in_specs=[HBM_SPEC] * n + [SEM_SPEC] * ns + [pl.BlockSpec(memory_space=pl.ANY)], out_specs=tuple([HBM_SPEC] * n),
        input_output_aliases={k: k for k in range(n)},
        compiler_params=pltpu.CompilerParams(has_side_effects=EFFECT),
    )(*arrs, *sems, after)


def _gw_copies(refs, send_sems, recv_sems, shard_shapes, kinds, outgoing):
    x, y, c = _coords()
    chip = 2 * x + y
    out = []
    for p in range(len(refs)):
        for j, (ox, oy, oc) in enumerate(_other_chips(x, y)):
            win = _window(refs[p], shard_shapes[p], kinds[p], chip if outgoing else oc, c)
            out.append(pltpu.make_async_remote_copy(
                src_ref=win, dst_ref=win, send_sem=send_sems.at[3 * p + j], recv_sem=recv_sems.at[3 * p + j],
                device_id=(ox, oy, c), device_id_type=MESH))
    return out


def _gw_start(arrs, shard_shapes, kinds, groups, *, name):
    def start(refs, sems):
        for g, idx in enumerate(groups):
            for cp in _gw_copies([refs[p] for p in idx], sems[2 * g], sems[2 * g + 1], [shard_shapes[p] for p in idx],
                                 [kinds[p] for p in idx], True):
                cp.start()

    n_sems = [3 * len(idx) for idx in groups for _ in range(2)]
    sems, thru, token = _split_start(start, arrs, n_sems, name=name)
    return [(sems[2 * g], sems[2 * g + 1]) for g in range(len(groups))], thru, token


def _gw_wait(arrs, shard_shapes, kinds, sem_pair, after, *, name):
    def wait(refs, sems):
        for cp in _gw_copies(refs, sems[0], sems[1], shard_shapes, kinds, True):
            cp.wait_send()
        for cp in _gw_copies(refs, sems[0], sems[1], shard_shapes, kinds, False):
            cp.wait_recv()

    return _split_wait(wait, arrs, list(sem_pair), after, name=name)


def _gw_forward_copies(refs, shard_shapes, kinds, send_sems, recv_sems, incoming):
    x, y, c = _coords()
    out = []
    for p in range(len(refs)):
        for j, (_, _, oc) in enumerate(_other_chips(x, y)):
            win = _window(refs[p], shard_shapes[p], kinds[p], oc, 1 - c if incoming else c)
            out.append(pltpu.make_async_remote_copy(
                src_ref=win, dst_ref=win, send_sem=send_sems.at[3 * p + j], recv_sem=recv_sems.at[3 * p + j],
                device_id=(x, y, 1 - c), device_id_type=MESH))
    return out


def _gw_forward(arrs, shard_shapes, kinds, *, name):
    n = len(arrs)

    def body(*refs):
        dst, (send_sems, recv_sems) = refs[n:2 * n], refs[2 * n:]
        sends = _gw_forward_copies(dst, shard_shapes, kinds, send_sems, recv_sems, False)
        for cp in sends:
            cp.start()
        for cp in _gw_forward_copies(dst, shard_shapes, kinds, send_sems, recv_sems, True):
            cp.wait_recv()
        for cp in sends:
            cp.wait_send()

    outs = [jax.ShapeDtypeStruct(a.shape, a.dtype) for a in arrs]
    return _hbm_call(body, arrs, outs, (3 * n, 3 * n), name=name, alias=True)


def _gw_forward_start(arrs, shard_shapes, kinds, *, name):
    def start(refs, sems):
        for cp in _gw_forward_copies(refs, shard_shapes, kinds, sems[0], sems[1], False):
            cp.start()

    sems, thru, token = _split_start(start, arrs, [3 * len(arrs)] * 2, name=name)
    return (sems[0], sems[1]), thru, token


def _gw_forward_wait(pair, arrs, shard_shapes, kinds, after, *, name):
    def wait(refs, sems):
        for cp in _gw_forward_copies(refs, shard_shapes, kinds, sems[0], sems[1], False):
            cp.wait_send()
        for cp in _gw_forward_copies(refs, shard_shapes, kinds, sems[0], sems[1], True):
            cp.wait_recv()

    return _split_wait(wait, arrs, list(pair), after, name=name)


def _rs_chips_copies(parts, lands, send_sems, recv_sems):
    x, y, c = _coords()
    chip = 2 * x + y
    out = []
    for p in range(len(parts)):
        for j, (ox, oy, oc) in enumerate(_other_chips(x, y)):
            out.append(pltpu.make_async_remote_copy(
                src_ref=parts[p].at[oc], dst_ref=lands[p].at[chip], send_sem=send_sems.at[3 * p + j],
                recv_sem=recv_sems.at[3 * p + j], device_id=(ox, oy, c), device_id_type=MESH))
    return out


def _rs_chips_start(parts, *, name):
    n = len(parts)

    def start(refs, sems):
        for cp in _rs_chips_copies(refs[:n], refs[n:], sems[0], sems[1]):
            cp.start()

    lands = [lax.empty(t.shape, t.dtype) for t in parts]
    sems, thru, token = _split_start(start, list(parts) + lands, [3 * n, 3 * n], name=name)
    return (sems[0], sems[1]), thru[:n], thru[n:], token


def _rs_chips_wait(groups, after, *, name):
    sizes = [len(g[1]) for g in groups]
    arrs = [a for g in groups for a in list(g[1]) + list(g[2])]
    sems = [s for g in groups for s in g[0]]

    def wait(refs, sem_refs):
        o = 0
        for k, n in enumerate(sizes):
            for cp in _rs_chips_copies(refs[o:o + n], refs[o + n:o + 2 * n], sem_refs[2 * k], sem_refs[2 * k + 1]):
                cp.wait()
            o += 2 * n

    outs = _split_wait(wait, arrs, sems, after, name=name)
    res, o = [], 0
    for n in sizes:
        res.append((list(outs[o:o + n]), list(outs[o + n:o + 2 * n])))
        o += 2 * n
    return res


def _rs_cores_copies(grads, lands, send_sems, recv_sems):
    x, y, c = _coords()
    out, o = [], 0
    for p in range(len(grads)):
        nsh, h = lands[p].shape[0], lands[p].shape[1]
        for j in range(nsh):
            out.append(pltpu.make_async_remote_copy(
                src_ref=grads[p].at[j, pl.ds((1 - c) * h, h), :], dst_ref=lands[p].at[j],
                send_sem=send_sems.at[o + j], recv_sem=recv_sems.at[o + j], device_id=(x, y, 1 - c), device_id_type=MESH))
        o += nsh
    return out


def _rs_cores_start(grads, *, name):
    n = len(grads)
    tot = sum(g.shape[0] for g in grads)

    def start(refs, sems):
        for cp in _rs_cores_copies(refs[:n], refs[n:], sems[0], sems[1]):
            cp.start()

    lands = [lax.empty((g.shape[0], g.shape[1] // 2, g.shape[2]), g.dtype) for g in grads]
    sems, thru, token = _split_start(start, list(grads) + lands, [tot, tot], name=name)
    return (sems[0], sems[1]), thru[:n], thru[n:], token


def _rs_cores_wait(pair, grads, lands, after, *, name):
    n = len(grads)

    def wait(refs, sems):
        for cp in _rs_cores_copies(refs[:n], refs[n:], sems[0], sems[1]):
            cp.wait()

    outs = _split_wait(wait, list(grads) + list(lands), list(pair), after, name=name)
    return list(outs[:n]), list(outs[n:])


def _rs_cores(grads, *, name):
    n = len(grads)
    outs = [jax.ShapeDtypeStruct((g.shape[0], g.shape[1] // 2, g.shape[2]), g.dtype) for g in grads]

    def body(*refs):
        src, dst = refs[:n], refs[n:2 * n]
        send_sems, recv_sems = refs[2 * n:]
        x, y, c = _coords()
        copies = []
        for p in range(n):
            nsh, r, _ = grads[p].shape
            h = r // 2
            for j in range(nsh):
                cp = pltpu.make_async_remote_copy(
                    src_ref=src[p].at[j, pl.ds((1 - c) * h, h), :], dst_ref=dst[p].at[j],
                    send_sem=send_sems.at[nsh * p + j], recv_sem=recv_sems.at[nsh * p + j],
                    device_id=(x, y, 1 - c), device_id_type=MESH)
                cp.start()
                copies.append(cp)
        for cp in copies:
            cp.wait()

    tot = sum(g.shape[0] for g in grads)
    return _hbm_call(body, grads, outs, (tot, tot), name=name)


def _sum_own_half(full, recv, ci, out_dtype, *, name):
    nsh, h, cols = recv.shape
    tm = h if nsh * h * cols * 4 <= (2 << 20) else _tile_rows(h, 256)
    nblk = h // tm

    def body(c_ref, f_ref, r_ref, o_ref):
        o_ref[...] = (f_ref[...] + r_ref[...]).astype(o_ref.dtype)

    return pl.pallas_call(
        body, name=name,
        grid_spec=pltpu.PrefetchScalarGridSpec(
            num_scalar_prefetch=1, grid=(nsh, nblk),
            in_specs=[pl.BlockSpec((1, tm, cols), lambda j, i, c_ref: (j, c_ref[0] * nblk + i, 0)),
                      pl.BlockSpec((1, tm, cols), lambda j, i, c_ref: (j, i, 0))],
            out_specs=pl.BlockSpec((1, tm, cols), lambda j, i, c_ref: (j, i, 0))),
        out_shape=jax.ShapeDtypeStruct((nsh, h, cols), out_dtype), compiler_params=_params("parallel", "parallel"),
    )(ci.reshape(1).astype(jnp.int32), full, recv)


def _sum_chips(recv, own, chip, ci, *, name, nlead=1, lead=0, prev=None, spread=False):
    nsh, h, cols = recv.shape
    tm = h if nsh * h * cols * 4 <= (2 << 20) else _tile_rows(h, 256)
    nblk = h // tm
    rows_out = 2 * h * (nsh if spread else 1)

    def body(s_ref, r_ref, o_ref, *rest):
        out_ref = rest[-1]
        t = None
        for s in range(nsh):
            v = jnp.where(s_ref[0] == s, o_ref[s], r_ref[s]).astype(F32)
            t = v if t is None else t + v
        out_ref[...] = t

    def out_idx(i, s):
        return (lead, (s[0] * 2 * nblk if spread else 0) + s[1] * nblk + i, 0)

    blk = pl.BlockSpec((nsh, tm, cols), lambda i, s: (0, i, 0))
    ins = [recv, own] + ([prev] if prev is not None else [])
    return pl.pallas_call(
        body, name=name,
        grid_spec=pltpu.PrefetchScalarGridSpec(
            num_scalar_prefetch=1, grid=(nblk,),
            in_specs=[blk, blk] + ([pl.BlockSpec(memory_space=pl.ANY)] if prev is not None else []),
            out_specs=pl.BlockSpec((None, tm, cols), out_idx)),
        out_shape=jax.ShapeDtypeStruct((nlead, rows_out, cols), F32),
        input_output_aliases={3: 0} if prev is not None else {},
        compiler_params=_params("arbitrary"),
    )(jnp.stack([chip, ci]).astype(jnp.int32), *ins)


RS_GATHER_CHUNKS = 4


def _rs_gather_copies(refs, nleads, halves, spread, send_sems, recv_sems, incoming):
    x, y, c = _coords()
    chip = 2 * x + y
    out, sem = [], 0
    for p in range(len(refs)):
        h = halves[p]
        q = h // RS_GATHER_CHUNKS
        base = (chip * 2 * h if spread[p] else 0) + (1 - c if incoming else c) * h
        for l in range(nleads[p]):
            for k in range(RS_GATHER_CHUNKS):
                win = refs[p].at[l, pl.ds(base + k * q, q), :]
                out.append(pltpu.make_async_remote_copy(src_ref=win, dst_ref=win, send_sem=send_sems.at[sem],
                                                        recv_sem=recv_sems.at[sem], device_id=(x, y, 1 - c),
                                                        device_id_type=MESH))
                sem += 1
    return out


def _rs_gather_start(arrs, halves, spread, *, name):
    nleads = [a.shape[0] for a in arrs]
    tot = sum(nleads) * RS_GATHER_CHUNKS

    def start(refs, sems):
        for cp in _rs_gather_copies(refs, nleads, halves, spread, sems[0], sems[1], False):
            cp.start()

    sems, thru, token = _split_start(start, arrs, [tot, tot], name=name)
    return (sems[0], sems[1]), thru, token


def _rs_gather_wait(pair, arrs, halves, spread, after, *, name):
    nleads = [a.shape[0] for a in arrs]

    def wait(refs, sems):
        for cp in _rs_gather_copies(refs, nleads, halves, spread, sems[0], sems[1], False):
            cp.wait_send()
        for cp in _rs_gather_copies(refs, nleads, halves, spread, sems[0], sems[1], True):
            cp.wait_recv()

    return _split_wait(wait, arrs, list(pair), after, name=name)


def _adamw(w, g, m, v, *, name):
    nl, R, C = w.shape
    tm = _tile_rows(R, 256)

    blk = pl.BlockSpec((None, tm, C), lambda l, i: (l, i, 0))
    return pl.pallas_call(
        _adamw_body_copy(), name=name, grid=(nl, R // tm), in_specs=[blk] * 4, out_specs=[blk] * 3,
        out_shape=[jax.ShapeDtypeStruct((nl, R, C), F32)] * 3, compiler_params=_params("parallel", "parallel"),
    )(w, g, m, v)


def _adamw_body(w_ref, g_ref, m_ref, v_ref, d_ref, nm_ref, nv_ref):
    gg = g_ref[...]
    nm = B1 * m_ref[...] + (1.0 - B1) * gg
    nv = B2 * v_ref[...] + (1.0 - B2) * (gg * gg)
    m_hat = nm / (1.0 - B1 ** ASTEP)
    v_hat = nv / (1.0 - B2 ** ASTEP)
    d_ref[...] = -LR * (m_hat / (jnp.sqrt(v_hat) + AEPS) + WD * w_ref[...])
    nm_ref[...] = nm
    nv_ref[...] = nv


def _adamw_whole(w, g, m, v, *, name):
    return pl.pallas_call(_adamw_body_copy(), name=name, out_shape=[jax.ShapeDtypeStruct(w.shape, F32)] * 3,
                          compiler_params=_params())(w, g, m, v)


def _adamw_body_copy():
    def body(*refs):
        _adamw_body(*refs)
    return body


def _mod_cols(c_all, w_ada, b_cols, *, name):
    nl, D, cols = w_ada.shape
    B = c_all.shape[0]

    def body(c_ref, w_ref, b_ref, o_ref):
        cc = c_ref[...]
        cs = (cc * _sigmoid(cc)).astype(BF16)
        o_ref[0] = jnp.dot(cs, w_ref[0].astype(BF16), preferred_element_type=F32) + b_ref[0]

    return pl.pallas_call(
        body, name=name, grid=(nl,),
        in_specs=[_whole(c_all.shape), pl.BlockSpec((1, D, cols), lambda i: (i, 0, 0)), pl.BlockSpec((1, 1, cols), lambda i: (i, 0, 0))],
        out_specs=pl.BlockSpec((1, B, cols), lambda i: (i, 0, 0)),
        out_shape=jax.ShapeDtypeStruct((nl, B, cols), F32), compiler_params=_params("arbitrary"),
    )(c_all, w_ada, b_cols)


def _ada_grad(c_all, dmod_cols, *, name):
    nl, B, cols = dmod_cols.shape
    D = c_all.shape[1]

    def body(c_ref, d_ref, o_ref):
        cc = c_ref[...]
        cs = (cc * _sigmoid(cc)).astype(BF16)
        o_ref[0] = lax.dot_general(cs, d_ref[0].astype(BF16), TN, preferred_element_type=F32)

    return pl.pallas_call(
        body, name=name, grid=(nl,),
        in_specs=[_whole(c_all.shape), pl.BlockSpec((1, B, cols), lambda i: (i, 0, 0))],
        out_specs=pl.BlockSpec((1, D, cols), lambda i: (i, 0, 0)),
        out_shape=jax.ShapeDtypeStruct((nl, D, cols), F32), compiler_params=_params("arbitrary"),
    )(c_all, dmod_cols)


def _s5_disc(a_re, a_im, log_dt, b_re, b_im):
    dt = jnp.exp(log_dt)[:, None]
    mag = jnp.exp(a_re * dt)
    ph = a_im * dt
    lb_re = mag * jnp.cos(ph)
    lb_im = mag * jnp.sin(ph)
    den = a_re * a_re + a_im * a_im
    nr = lb_re - 1.0
    ni = lb_im
    f_re = (nr * a_re + ni * a_im) / den
    f_im = (ni * a_re - nr * a_im) / den
    bb_re = f_re[..., None] * b_re - f_im[..., None] * b_im
    bb_im = f_re[..., None] * b_im + f_im[..., None] * b_re
    return lb_re, lb_im, bb_re, bb_im


def _to_segments(t):
    L, D = t.shape
    R = min(S5_R, L)
    return t.reshape(L // R, NSEG, R // NSEG, D).transpose(0, 2, 1, 3).reshape(L, D)


def _from_segments(t):
    L, D = t.shape
    R = min(S5_R, L)
    return t.reshape(L // R, R // NSEG, NSEG, D).transpose(0, 2, 1, 3).reshape(L, D)


def _mlp_fwd(h2, w1, w2, tag):
    a = _matmul(h2, w1, name=f"ff1_{tag}", tn=2048, out_dtypes=(BF16,), epi=lambda acc: (jnp.maximum(acc, 0.0),))
    f = _matmul(a, w2, name=f"ff2_{tag}", a_fn=jnp.square)
    return a, f


def _mlp_bwd(df, h2, a, w1, w2, tag):
    da = _matmul(df, w2, tb=True, name=f"ff2_dx_{tag}", tn=2048, out_dtypes=(BF16,), epi_ins=(a,),
                 epi=lambda acc, at: (acc * (2.0 * at.astype(F32)),))
    dw2 = _matmul(a, df, ta=True, name=f"ff2_dw_{tag}", a_fn=jnp.square)
    dh2 = _matmul(da, w1, tb=True, name=f"ff1_dx_{tag}")
    dw1 = _matmul(h2, da, ta=True, name=f"ff1_dw_{tag}", col_shards=NCH)
    return dh2, dw1, dw2


def kernel(x, c, w_ada, b_ada, norm_mix, norm_mlp, s5_a_re, s5_a_im, s5_log_dt, s5_b_re, s5_b_im, s5_c_re, s5_c_im, s5_d, s5_w_glu, gla_w_in, gla_w_gate2, gla_b_gate, gla_g_norm, gla_w_out, w_ff1, w_ff2, norm_final, loss_target, m_w_ada, m_b_ada, m_norm_mix, m_norm_mlp, m_s5_a_re, m_s5_a_im, m_s5_log_dt, m_s5_b_re, m_s5_b_im, m_s5_c_re, m_s5_c_im, m_s5_d, m_s5_w_glu, m_gla_w_in, m_gla_w_gate2, m_gla_b_gate, m_gla_g_norm, m_gla_w_out, m_w_ff1, m_w_ff2, m_norm_final, v_w_ada, v_b_ada, v_norm_mix, v_norm_mlp, v_s5_a_re, v_s5_a_im, v_s5_log_dt, v_s5_b_re, v_s5_b_im, v_s5_c_re, v_s5_c_im, v_s5_d, v_s5_w_glu, v_gla_w_in, v_gla_w_gate2, v_gla_b_gate, v_gla_g_norm, v_gla_w_out, v_w_ff1, v_w_ff2, v_norm_final):
    args = dict(locals())
    L, D = x.shape[1], x.shape[2]
    QK = D // 2
    xi, yi, ci = _coords()
    chip = 2 * xi + yi
    dev = 2 * chip + ci

    cat = jnp.concatenate([gla_w_gate2[0].reshape(1, -1), gla_b_gate, gla_g_norm], axis=1)
    first = _exchange(jnp.concatenate([c.reshape(8, D // 8), jnp.tile(cat, (8, 1))], axis=1), MASK_ALL, name="gather_c")
    c_all = first[:, :, :D // 8].reshape(8, D)
    cat_all = first[0::2, 0, D // 8:]
    acols = w_ada.shape[2]
    b_cols = lax.dynamic_slice_in_dim(b_ada, chip * acols, acols, axis=1)[:, None, :]
    mod_cols = _mod_cols(c_all, w_ada, b_cols, name="ada_mod")
    mod_all = _exchange(mod_cols.reshape(16, acols), MASK_CHIPS, name="gather_mod")
    mod_all = mod_all.reshape(NCH, 2, 8, acols).transpose(1, 2, 0, 3).reshape(2, 8, NCH * acols)
    mod = lax.dynamic_index_in_dim(mod_all, dev, axis=1, keepdims=False).reshape(2, 6, 1, D)

    big = [("s5_w_glu", s5_w_glu, 0, "col"), ("gla_w_in", gla_w_in, 0, "slot"), ("gla_w_out", gla_w_out, 0, "row"),
           ("w_ff1_0", w_ff1, 0, "col"), ("w_ff1_1", w_ff1, 1, "col"), ("w_ff2_0", w_ff2, 0, "row"), ("w_ff2_1", w_ff2, 1, "row")]
    own16 = [_cast_into(t, lead, kind, chip, name=f"cast_{nm}") for nm, t, lead, kind in big]
    wshapes, wkinds = [b[1].shape[-2:] for b in big], [b[3] for b in big]
    wgroups = [[0], [3, 5], [1, 2, 4, 6]]
    wsems, wthru, wtoken = _gw_start(own16, wshapes, wkinds, wgroups, name="gather_w_start")
    W = {}

    def weights_landed(g, after):
        idx = wgroups[g]
        shp, knd = [wshapes[p] for p in idx], [wkinds[p] for p in idx]
        return _gw_wait([wthru[p] for p in idx], shp, knd, wsems[g], after, name=f"gather_w_wait{g}"), shp, knd

    def weights_ready(g, arrs):
        for p, w in zip(wgroups[g], arrs):
            W[big[p][0]] = w

    def forward_start(g, after):
        got, shp, knd = weights_landed(g, after)
        pair, thru, token = _gw_forward_start(got, shp, knd, name=f"gather_w_cores_start{g}")
        return (pair, thru, shp, knd), token

    def forward_finish(g, state, after):
        pair, thru, shp, knd = state
        weights_ready(g, _gw_forward_wait(pair, thru, shp, knd, after, name=f"gather_w_cores_wait{g}"))

    qk4 = QK // NCH
    wg2 = cat_all[:, :GATE_RANK * qk4].reshape(NCH, GATE_RANK, qk4).transpose(1, 0, 2).reshape(GATE_RANK, QK)
    bg = cat_all[:, GATE_RANK * qk4:(GATE_RANK + 1) * qk4].reshape(1, QK)
    gn = cat_all[:, (GATE_RANK + 1) * qk4:].reshape(1, D)
    wg2p = jnp.concatenate([wg2, jnp.zeros((128 - GATE_RANK, QK), F32)], axis=0).astype(BF16)

    lb_re, lb_im, bb_re, bb_im = _s5_disc(s5_a_re[0], s5_a_im[0], s5_log_dt[0], s5_b_re[0], s5_b_im[0])
    tb = _s5_tables(lb_re, lb_im, bb_re, bb_im, s5_c_re[0], s5_c_im[0], min(S5_R, L) // NSEG)
    s5_dv = s5_d + wtoken[0, 0]

    def vec(t):
        return t.reshape(1, -1)

    m0, m1 = mod[0], mod[1]
    xp = _to_segments(x[0])
    (u0,) = _rows(lambda t, g, sc, sh: (_norm_mod(t, g, sc, sh),), [xp], [vec(norm_mix[0]), m0[1], m0[0]],
                  [(D, F32)], [], name="pre_mix0")
    y0, z0, ck0 = _s5_fwd(u0, tb, s5_dv, name="s5_fwd")
    got, shp, knd = weights_landed(0, z0)
    weights_ready(0, _gw_forward(got, shp, knd, name="gather_w_cores0"))
    vg0 = _matmul(z0, W["s5_w_glu"], name="glu", tn=2048)
    fwd1, ftok1 = forward_start(1, vg0)

    def res_glu_pre(xt, vgt, gt, g, sc, sh):
        xn = xt + gt * (vgt[:, :D] * _sigmoid(vgt[:, D:]))
        return xn, _norm_mod(xn, g, sc, sh)

    x2_0, h2_0 = _rows(res_glu_pre, [xp, vg0], [m0[2] + ftok1[0, 0], vec(norm_mlp[0]), m0[4], m0[3]],
                       [(D, F32), (D, BF16)], [], name="res_mix0")
    forward_finish(1, fwd1, h2_0)
    a_0, f0 = _mlp_fwd(h2_0, W["w_ff1_0"], W["w_ff2_0"], "0")
    fwd2, ftok2 = forward_start(2, f0)

    def res_pre(xt, bt, gt, g, sc, sh):
        xn = xt + gt * bt
        return xn, _norm_mod(xn, g, sc, sh)

    x3p, h1p = _rows(res_pre, [x2_0, f0], [m0[5] + ftok2[0, 0], vec(norm_mix[1]), m1[1], m1[0]],
                     [(D, F32), (D, BF16)], [], name="res_mlp0")
    x3, h1 = _from_segments(x3p), _from_segments(h1p)
    forward_finish(2, fwd2, h1)
    w_in = W["gla_w_in"].transpose(1, 0, 2).reshape(D, -1)
    w_in_r = jnp.concatenate([w_in[:, :4 * QK], w_in[:, 4 * QK + GATE_RANK:], w_in[:, 4 * QK:4 * QK + GATE_RANK],
                              jnp.zeros((D, 128 - GATE_RANK), BF16)], axis=1)
    proj = _matmul(h1, w_in_r, name="gla_in", tn=640)
    og, states = _gla_fwd(proj, wg2p, bg, gn, name="gla_fwd")
    ymix = _matmul(og, W["gla_w_out"], name="gla_out")
    x2_1, h2_1 = _rows(res_pre, [x3, ymix], [m1[2], vec(norm_mlp[1]), m1[4], m1[3]], [(D, F32), (D, BF16)], [],
                       name="res_mix1")
    a_1, f1 = _mlp_fwd(h2_1, W["w_ff1_1"], W["w_ff2_1"], "1")

    def final(xt, ft, tgt, gt, g):
        xn = xt + gt * ft
        rs = lax.rsqrt(jnp.mean(xn * xn, axis=-1, keepdims=True) + EPS)
        xh = xn * rs
        e = xh * g - tgt
        dout = e * (1.0 / D)
        dxh = dout * g
        dx = rs * (dxh - xh * jnp.mean(dxh * xh, axis=-1, keepdims=True))
        lsum = 0.5 * jnp.sum(jnp.sum(e * e, axis=-1, keepdims=True), axis=0, keepdims=True) * (1.0 / D)
        return dx, dx * gt, jnp.broadcast_to(lsum, (1, 128)), _rsum(dout * xh), _rsum(dx * ft)

    dx, df1, loss_part, d_norm_final, dgt2_1 = _rows(
        final, [x2_1, f1, loss_target[0]], [m1[5], vec(norm_final)], [(D, F32), (D, BF16)],
        [(1, 128), (1, D), (1, D)], name="loss_head")
    loss = lax.psum(loss_part[0, 0], ("x", "y", "c"))

    def gate_bwd(dxt, bt, gt):
        return dxt * gt, _rsum(dxt * bt)

    def norm_bwd(xt, dht, drt, g, sc):
        dxn, dsh, dsc, dg = _norm_mod_bwd(xt, dht, g, sc)
        return drt + dxn, dsh, dsc, dg

    def norm_gate_bwd(xt, dht, drt, bt, g, sc, gt):
        dxn, dsh, dsc, dg = _norm_mod_bwd(xt, dht, g, sc)
        dxt = drt + dxn
        return dxt, dxt * gt, dsh, dsc, dg, _rsum(dxt * bt)

    vD = [(1, D)]
    dh2_1, dw_ff1_1, dw_ff2_1 = _mlp_bwd(df1, h2_1, a_1, W["w_ff1_1"], W["w_ff2_1"], "1")
    dx, dmix1, dsh2_1, dsc2_1, dg_mlp1, dgt1_1 = _rows(
        norm_gate_bwd, [x2_1, dh2_1, dx, ymix], [vec(norm_mlp[1]), m1[4], m1[2]], [(D, F32), (D, BF16)], vD * 4,
        name="norm_mlp1_bwd")
    dog = _matmul(dmix1, W["gla_w_out"], tb=True, name="gla_out_dx")
    dw_out = _matmul(og, dmix1, ta=True, name="gla_out_dw")
    dproj, dwg2p, dbg, dgn = _gla_bwd(proj, dog, states, wg2p, bg, gn, name="gla_bwd")
    dh1 = _matmul(dproj, w_in_r, tb=True, name="gla_in_dx", tk=3200)
    dw_in_r = _matmul(h1, dproj, ta=True, name="gla_in_dw", tn=640)
    dx, dsh1_1, dsc1_1, dg_mix1 = _rows(norm_bwd, [x3, dh1, dx], [vec(norm_mix[1]), m1[1]], [(D, F32)], vD * 3,
                                        name="norm_mix1_bwd")
    dxp = _to_segments(dx)
    tags = [b[0] for b in big] + ["small"]
    rs_groups = []

    def rs_chips_begin(idx, srcs, r1, gname):
        s1 = [_sum_own_half(g, r, ci, F32 if tags[k] == "small" else BF16, name=f"rs_sum_cores_{tags[k]}")
              for g, r, k in zip(srcs, r1, idx)]
        pair, parts, lands, token = _rs_chips_start(s1, name=f"rs_chips_start_{gname}")
        rs_groups.append((idx, pair, parts, lands))
        return token

    def rs_begin(idx, srcs, gname):
        return rs_chips_begin(idx, srcs, _rs_cores(srcs, name=f"rs_cores_{gname}"), gname)

    dw_in = jnp.concatenate([dw_in_r[:, :4 * QK], dw_in_r[:, 6 * QK:6 * QK + GATE_RANK], dw_in_r[:, 4 * QK:6 * QK]], axis=1)
    dw_in = dw_in.reshape(D, NCH, -1).transpose(1, 0, 2)
    idx1 = [1, 2, 4, 6]
    pair1, src1, land1, tok1 = _rs_cores_start(
        [dw_in, dw_out.reshape(NCH, -1, D), dw_ff1_1, dw_ff2_1.reshape(NCH, -1, D)], name="rs_cores_start_l1")

    df0, dgt2_0 = _rows(gate_bwd, [dxp, f0], [m0[5] + tok1[0, 0]], [(D, BF16)], vD, name="gate_mlp0")
    dh2_0, dw_ff1_0, dw_ff2_0 = _mlp_bwd(df0, h2_0, a_0, W["w_ff1_0"], W["w_ff2_0"], "0")
    src1, land1 = _rs_cores_wait(pair1, src1, land1, dh2_0, name="rs_cores_wait_l1")
    tok1b = rs_chips_begin(idx1, src1, land1, "l1")
    idx0 = [3, 5]
    pair0, src0, land0, tok0 = _rs_cores_start([dw_ff1_0, dw_ff2_0.reshape(NCH, -1, D)], name="rs_cores_start_l0")
    tok2 = tok1b + tok0

    def norm_glu_bwd(xt, dht, drt, vgt, g, sc, gt):
        dxn, dsh, dsc, dg = _norm_mod_bwd(xt, dht, g, sc)
        dxt = drt + dxn
        val, sg = vgt[:, :D], _sigmoid(vgt[:, D:])
        dbr = dxt * gt
        dvg = jnp.concatenate([dbr * sg, dbr * val * sg * (1.0 - sg)], axis=1)
        return dxt, dvg, dsh, dsc, dg, _rsum(dxt * val * sg)

    dxp, dvg0, dsh2_0, dsc2_0, dg_mlp0, dgt1_0 = _rows(
        norm_glu_bwd, [x2_0, dh2_0, dxp, vg0], [vec(norm_mlp[0]), m0[4] + tok2[0, 0], m0[2]], [(D, F32), (2 * D, BF16)],
        vD * 4, name="norm_mlp0_bwd")
    dz0 = _matmul(dvg0, W["s5_w_glu"], tb=True, name="glu_dx")
    dw_glu = _matmul(z0, dvg0, ta=True, name="glu_dw", tn=512, col_shards=NCH)
    src0, land0 = _rs_cores_wait(pair0, src0, land0, dw_glu, name="rs_cores_wait_l0")
    tok0b = rs_chips_begin(idx0 + [0], src0 + [dw_glu], land0 + list(_rs_cores([dw_glu], name="rs_cores_glu")), "l0")
    du0, db_acc, dc_acc, dl_acc, dd_s5 = _s5_bwd(u0, y0, dz0, ck0, tb, s5_dv + tok0b[0, 0], name="s5_bwd")
    dxp, dsh1_0, dsc1_0, dg_mix0 = _rows(norm_bwd, [xp, du0, dxp], [vec(norm_mix[0]), m0[1]], [(D, F32)], vD * 3,
                                         name="norm_mix0_bwd")
    grad_x = _from_segments(dxp)[None]

    dmod = jnp.concatenate([dsh1_0, dsc1_0, dgt1_0, dsh2_0, dsc2_0, dgt2_0,
                            dsh1_1, dsc1_1, dgt1_1, dsh2_1, dsc2_1, dgt2_1], axis=1)
    dbb_re, dbb_im = _s5_untable(db_acc)
    dc_re, dc_im_neg = _s5_untable(dc_acc)
    nbk = D // 128
    dl = dl_acc.reshape(nbk, NSEG, 2, GPB * S5_P).sum(axis=1)
    smalls = [dmod, dg_mix0, dg_mix1, dg_mlp0, dg_mlp1, d_norm_final, dd_s5, dbg, dgn,
              dwg2p[:GATE_RANK].reshape(1, -1), dbb_re.reshape(1, -1), dbb_im.reshape(1, -1),
              dc_re.reshape(1, -1), dc_im_neg.reshape(1, -1), dl.reshape(1, -1)]
    ssz = [t.shape[1] for t in smalls]
    stot = sum(ssz)
    spad = -(-stot // 8192) * 8192
    svec = jnp.concatenate(smalls + [jnp.zeros((1, spad - stot), F32)], axis=1).reshape(NCH, spad // (128 * NCH), 128)


    rs_begin([7], [svec], "last")
    landed = _rs_chips_wait([(g[1], g[2], g[3]) for g in rs_groups], grad_x, name="rs_chips_wait")
    s1, r2 = {}, {}
    for (idx, _, _, _), (parts, lands) in zip(rs_groups, landed):
        for k, part, land in zip(idx, parts, lands):
            s1[k], r2[k] = part, land

    def fin(k, **kw):
        return _sum_chips(r2[k], s1[k], chip, ci, name=f"rs_sum_chips_{tags[k]}", **kw)

    f_ff1 = fin(4, nlead=2, lead=1, prev=fin(3, nlead=2, lead=0))
    f_ff2 = fin(6, nlead=2, lead=1, prev=fin(5, nlead=2, lead=0))
    finals = [fin(0), fin(1), fin(2), f_ff1, f_ff2, fin(7, spread=True)]
    halves = [t.shape[1] for t in (s1[0], s1[1], s1[2], s1[3], s1[5], s1[7])]
    gpair, gthru, gtok = _rs_gather_start(finals, halves, [False] * 5 + [True], name="rs_gather_cores_start")

    dmod_all = _exchange((dmod + gtok[0, 0]).reshape(12 * D // 128, 128), MASK_ALL, name="gather_dmod").reshape(8, 2, 6 * D)
    dmod_cols = lax.dynamic_slice_in_dim(dmod_all, chip * acols, acols, axis=2).transpose(1, 0, 2)
    g_w_ada = _ada_grad(c_all, dmod_cols, name="ada_grad")
    upd_w_ada = _adamw(w_ada, g_w_ada, m_w_ada, v_w_ada, name="adamw_w_ada")

    g_glu, g_in, g_out, g_w_ff1, g_w_ff2, s_own = _rs_gather_wait(
        gpair, gthru, halves, [False] * 5 + [True], upd_w_ada[0], name="rs_gather_cores_wait")
    srows = spad // (128 * NCH)
    (s_sum,) = _gather_weights([s_own.reshape(NCH * srows, 128)], [(srows, 128)], ["row"], name="gather_small_grads")
    s_sum = s_sum.reshape(-1)
    so = [sum(ssz[:k]) for k in range(len(ssz))]
    sm = [s_sum[o:o + n] for o, n in zip(so, ssz)]
    (dmod_s, g_mix0, g_mix1, g_mlp0, g_mlp1, g_nf, g_d, g_bg, g_gn, g_wg2, g_bbre, g_bbim, g_cre, g_cimn, g_dl) = sm
    g_b_ada = dmod_s.reshape(2, 6 * D)

    G = D // S5_H
    _, disc_vjp = jax.vjp(_s5_disc, s5_a_re[0], s5_a_im[0], s5_log_dt[0], s5_b_re[0], s5_b_im[0])
    g_dl = g_dl.reshape(nbk, 2, GPB, S5_P)
    ct = (g_dl[:, 0].reshape(G, S5_P), g_dl[:, 1].reshape(G, S5_P),
          g_bbre.reshape(G, S5_H, S5_P).transpose(0, 2, 1), g_bbim.reshape(G, S5_H, S5_P).transpose(0, 2, 1))
    g_a_re, g_a_im, g_log_dt, g_b_re, g_b_im = disc_vjp(ct)
    g_c_re = g_cre.reshape(G, S5_H, S5_P)
    g_c_im = -g_cimn.reshape(G, S5_H, S5_P)
    g_wg2_s = lax.dynamic_slice_in_dim(g_wg2.reshape(GATE_RANK, QK), chip * qk4, qk4, axis=1)
    g_bg_s = lax.dynamic_slice_in_dim(g_bg.reshape(1, QK), chip * qk4, qk4, axis=1)
    g_gn_s = lax.dynamic_slice_in_dim(g_gn.reshape(1, D), chip * (D // NCH), D // NCH, axis=1)

    grads = dict(
        w_ada=g_w_ada, b_ada=g_b_ada, norm_mix=jnp.stack([g_mix0, g_mix1]), norm_mlp=jnp.stack([g_mlp0, g_mlp1]),
        s5_a_re=g_a_re[None], s5_a_im=g_a_im[None], s5_log_dt=g_log_dt[None], s5_b_re=g_b_re[None], s5_b_im=g_b_im[None],
        s5_c_re=g_c_re[None], s5_c_im=g_c_im[None], s5_d=g_d[None], s5_w_glu=g_glu,
        gla_w_in=g_in, gla_w_gate2=g_wg2_s[None], gla_b_gate=g_bg_s, gla_g_norm=g_gn_s,
        gla_w_out=g_out, w_ff1=g_w_ff1, w_ff2=g_w_ff2, norm_final=g_nf)

    names = list(grads)
    large = ("w_ada", "s5_w_glu", "gla_w_in", "gla_w_out", "w_ff1", "w_ff2")
    delta, new_m, new_v = {}, {}, {}
    delta["w_ada"], new_m["w_ada"], new_v["w_ada"] = upd_w_ada
    for nm in large[1:]:
        delta[nm], new_m[nm], new_v[nm] = _adamw(args[nm], grads[nm], args["m_" + nm], args["v_" + nm], name=f"adamw_{nm}")
    grads = {nm: grads[nm].reshape(args[nm].shape) for nm in names}
    for nm in names:
        if nm not in large:
            shp = args[nm].shape
            as2d = (1, -1) if len(shp) == 1 else shp
            outs = _adamw_whole(*[t.reshape(as2d) for t in (args[nm], grads[nm], args["m_" + nm], args["v_" + nm])],
                                name=f"adamw_{nm}")
            delta[nm], new_m[nm], new_v[nm] = (t.reshape(shp) for t in outs)
    return (loss, grad_x, *[grads[n] for n in names], *[delta[n] for n in names], *[new_m[n] for n in names],
            *[new_v[n] for n in names])
```

```python
import math

import jax
import jax.numpy as jnp
from jax import lax
from jax.experimental import pallas as pl
from jax.experimental.pallas import tpu as pltpu

F32 = jnp.float32
BF16 = jnp.bfloat16
MESH = pl.DeviceIdType.MESH

EPS = 1e-6
CHUNK = 64
GLA_NB = 4
S5_H = 16
S5_P = 64
GPB = 8
NSEG = 8
HEADS = 4
GATE_RANK = 16
GATE_TAU = 16.0
NCH = 4
LR, B1, B2, AEPS, WD, ASTEP = 0.001, 0.9, 0.999, 1e-08, 0.01, 10
VMEM_LIMIT = 56 << 20
ROW_SUB = 64

MASK_CHIPS = ((1, 0, 0), (0, 1, 0), (1, 1, 0))
MASK_ALL = ((0, 0, 1), (0, 1, 0), (0, 1, 1), (1, 0, 0), (1, 0, 1), (1, 1, 0), (1, 1, 1))


def _params(*sem):
    return pltpu.CompilerParams(dimension_semantics=sem or None, vmem_limit_bytes=VMEM_LIMIT)


def _tile_rows(rows, cap=512):
    best = 8
    for t in range(8, cap + 1, 8):
        if rows % t == 0:
            best = t
    return best


def _whole(shape):
    return pl.BlockSpec(shape, lambda i, _n=len(shape): (0,) * _n)


def _matmul(a, b, *, name, ta=False, tb=False, tm=1024, tn=1024, tk=4096, out_dtypes=(F32,),
            a_fn=None, epi=None, epi_ins=(), col_shards=1):
    M, K = (a.shape[1], a.shape[0]) if ta else a.shape
    N = b.shape[0] if tb else b.shape[1]
    tm, tn, tk = min(tm, M), min(tn, N), min(tk, K)
    assert M % tm == 0 and N % tn == 0 and K % tk == 0, (name, M, N, K)
    nk = K // tk
    ne = len(epi_ins)
    dn = (((0 if ta else 1,), (1 if tb else 0,)), ((), ()))

    def body(a_ref, b_ref, *rest):
        e_refs, o_refs = rest[:ne], rest[ne:ne + len(out_dtypes)]
        at = a_ref[...]
        if a_fn is not None:
            at = a_fn(at)
        part = lax.dot_general(at.astype(BF16), b_ref[...].astype(BF16), dn, preferred_element_type=F32)

        def finish(total):
            outs = (total,) if epi is None else epi(total, *[r[...] for r in e_refs])
            for r, o in zip(o_refs, outs):
                r[...] = o.astype(r.dtype)

        if nk == 1:
            finish(part)
            return
        acc = rest[-1]
        k = pl.program_id(2)

        @pl.when(k == 0)
        def _():
            acc[...] = part

        @pl.when(k > 0)
        def _():
            acc[...] += part

        @pl.when(k == nk - 1)
        def _():
            finish(acc[...])

    a_spec = pl.BlockSpec((tk, tm), lambda i, j, k: (k, i)) if ta else pl.BlockSpec((tm, tk), lambda i, j, k: (i, k))
    b_spec = pl.BlockSpec((tn, tk), lambda i, j, k: (j, k)) if tb else pl.BlockSpec((tk, tn), lambda i, j, k: (k, j))
    o_spec = pl.BlockSpec((tm, tn), lambda i, j, k: (i, j))
    if col_shards > 1:
        per = N // col_shards // tn
        assert ne == 0 and per * tn * col_shards == N
        w_spec = pl.BlockSpec((None, tm, tn), lambda i, j, k: (j // per, i, j % per))
        o_shape = (col_shards, M, N // col_shards)
    else:
        w_spec, o_shape = o_spec, (M, N)
    outs = pl.pallas_call(
        body, name=name, grid=(M // tm, N // tn, nk),
        in_specs=[a_spec, b_spec] + [o_spec] * ne,
        out_specs=[w_spec] * len(out_dtypes),
        out_shape=[jax.ShapeDtypeStruct(o_shape, d) for d in out_dtypes],
        scratch_shapes=[pltpu.VMEM((tm, tn), F32)] if nk > 1 else [],
        compiler_params=_params("parallel", "parallel", "arbitrary"),
    )(a, b, *epi_ins)
    return outs[0] if len(outs) == 1 else outs


def _rows(fn, rows_in, vecs_in, rows_out, acc_out, *, name, tm=512):
    L = rows_in[0].shape[0]
    tm = min(tm, L)
    assert L % tm == 0
    nr, nv, no, na = len(rows_in), len(vecs_in), len(rows_out), len(acc_out)

    sub = ROW_SUB if tm % ROW_SUB == 0 else tm

    def body(*refs):
        rin, vin = refs[:nr], refs[nr:nr + nv]
        rout, aout = refs[nr + nv:nr + nv + no], refs[nr + nv + no:]
        if na:
            @pl.when(pl.program_id(0) == 0)
            def _():
                for r in aout:
                    r[...] = jnp.zeros_like(r)

        vecs = [v[...] for v in vin]
        sums = None
        for s in range(tm // sub):
            rows = pl.ds(s * sub, sub)
            outs = fn(*[r[rows, :] for r in rin], *vecs)
            for r, o in zip(rout, outs[:no]):
                r[rows, :] = o.astype(r.dtype)
            sums = list(outs[no:]) if sums is None else [t + o for t, o in zip(sums, outs[no:])]
        for r, t in zip(aout, sums):
            r[...] += t

    outs = pl.pallas_call(
        body, name=name, grid=(L // tm,),
        in_specs=[pl.BlockSpec((tm, r.shape[1]), lambda i: (i, 0)) for r in rows_in] + [_whole(v.shape) for v in vecs_in],
        out_specs=[pl.BlockSpec((tm, c), lambda i: (i, 0)) for c, _ in rows_out] + [_whole(s) for s in acc_out],
        out_shape=[jax.ShapeDtypeStruct((L, c), d) for c, d in rows_out] + [jax.ShapeDtypeStruct(s, F32) for s in acc_out],
        compiler_params=_params("arbitrary"),
    )(*rows_in, *vecs_in)
    return outs


def _rsum(t):
    return jnp.sum(t, axis=0, keepdims=True)


def _norm_mod(x, g, sc, sh):
    rs = lax.rsqrt(jnp.mean(x * x, axis=-1, keepdims=True) + EPS)
    return x * rs * g * (1.0 + sc) + sh


def _norm_mod_bwd(x, dh, g, sc):
    rs = lax.rsqrt(jnp.mean(x * x, axis=-1, keepdims=True) + EPS)
    xh = x * rs
    dn = dh * (1.0 + sc)
    dxh = dn * g
    dx = rs * (dxh - xh * jnp.mean(dxh * xh, axis=-1, keepdims=True))
    return dx, _rsum(dh), _rsum(dh * xh * g), _rsum(dn * xh)


def _sigmoid(x):
    return jax.nn.sigmoid(x)


def _gelu(y):
    return jax.nn.gelu(y, approximate=True)


def _gelu_grad(y):
    c = math.sqrt(2.0 / math.pi)
    t = jnp.tanh(c * (y + 0.044715 * y * y * y))
    return 0.5 * (1.0 + t) + 0.5 * y * (1.0 - t * t) * c * (1.0 + 3.0 * 0.044715 * y * y)


def _s5_tables(lb_re, lb_im, bb_re, bb_im, c_re, c_im, seg_len):
    G = lb_re.shape[0]
    nb = G // GPB
    eye = jnp.eye(GPB, dtype=F32)

    def bdiag(t):
        a, b = t.shape[1:]
        t = t.reshape(nb, GPB, a, b)
        return (t[:, :, :, None, :] * eye[None, :, None, :, None]).reshape(nb, GPB * a, GPB * b)

    bbd = jnp.concatenate([bdiag(bb_re.transpose(0, 2, 1)), bdiag(bb_im.transpose(0, 2, 1))], axis=2)
    cbd = jnp.concatenate([bdiag(c_re.transpose(0, 2, 1)), -bdiag(c_im.transpose(0, 2, 1))], axis=1)

    def lanes(re, im):
        t = jnp.concatenate([re.reshape(nb, GPB * S5_P), im.reshape(nb, GPB * S5_P)], axis=1)
        return jnp.repeat(t, NSEG, axis=0)

    tr, ti = lb_re, lb_im
    for _ in range(int(math.log2(seg_len))):
        tr, ti = tr * tr - ti * ti, 2.0 * tr * ti
    return dict(bbd=bbd.astype(BF16), bbdT=bbd.transpose(0, 2, 1).astype(BF16), cbd=cbd.astype(BF16),
                cbdT=cbd.transpose(0, 2, 1).astype(BF16), lam=lanes(lb_re, lb_im), lamT=lanes(tr, ti))


def _s5_untable(acc):
    nb = acc.shape[0]
    t = acc.reshape(nb, GPB, S5_H, 2, GPB, S5_P)
    d = jnp.diagonal(t, axis1=1, axis2=4)
    d = d.transpose(0, 4, 2, 1, 3).reshape(nb * GPB, 2, S5_H, S5_P)
    return d[:, 0], d[:, 1]


S5_R = 256


def _s5_carries(ends, first, t_ref, rws, SW, cfx, *, reverse):
    er, ei = ends
    tr, ti = t_ref[rws, :SW][0:1], t_ref[rws, SW:][0:1]
    cr, ci = first
    order = range(NSEG - 1, -1, -1) if reverse else range(NSEG)
    for n, s in enumerate(order):
        if n > 0:
            p = s + 1 if reverse else s - 1
            if reverse:
                cr, ci = tr * cr + ti * ci + er[p:p + 1], tr * ci - ti * cr + ei[p:p + 1]
            else:
                cr, ci = tr * cr - ti * ci + er[p:p + 1], tr * ci + ti * cr + ei[p:p + 1]
        cfx[s:s + 1, :SW] = cr
        cfx[s:s + 1, SW:] = ci


def _s5_fwd(up, tb, dvec, *, name):
    L, D = up.shape
    R = min(S5_R, L)
    nb, ta, ngb = L // R, R // NSEG, D // 128
    SW = GPB * S5_P
    crows = ngb * NSEG

    def body(u_ref, lam_ref, t_ref, b_ref, c_ref, d_ref, y_ref, z_ref, ck_ref, carry, xbuf2, cfx2):
        @pl.when(pl.program_id(0) == 0)
        def _():
            carry[...] = jnp.zeros_like(carry)

        zero = jnp.zeros((NSEG, SW), F32)
        for g0 in range(0, ngb, 2):
            pair = (g0, g0 + 1)
            xb = [xbuf2.at[g % 4] for g in pair]
            cf = [cfx2.at[g % 4] for g in pair]
            cols = [slice(g * 128, (g + 1) * 128) for g in pair]
            rws = [slice(g * NSEG, (g + 1) * NSEG) for g in pair]
            ug = [u_ref[:, cols[q]] for q in range(2)]
            for q in range(2):
                xb[q][...] = jnp.dot(ug[q].astype(BF16), b_ref[pair[q]], preferred_element_type=F32)
            lam = [(lam_ref[rws[q], :SW], lam_ref[rws[q], SW:]) for q in range(2)]

            def scan(c, store, xb=xb, lam=lam):
                c = list(c)
                for a in range(ta):
                    o = slice(a * NSEG, (a + 1) * NSEG)
                    for q in range(2):
                        (lr, li), (cr, ci) = lam[q], c[q]
                        nr = lr * cr - li * ci + xb[q][o, :SW]
                        ni = lr * ci + li * cr + xb[q][o, SW:]
                        if store:
                            xb[q][o, :SW] = nr
                            xb[q][o, SW:] = ni
                        c[q] = (nr, ni)
                return c

            ends = scan([(zero, zero)] * 2, False)
            for q in range(2):
                prev = (carry[rws[q], :SW][NSEG - 1:NSEG], carry[rws[q], SW:][NSEG - 1:NSEG])
                _s5_carries(ends[q], prev, t_ref, rws[q], SW, cf[q], reverse=False)
                ck_ref[0, rws[q], :] = cf[q][...]
            fin = scan([(cf[q][:, :SW], cf[q][:, SW:]) for q in range(2)], True)
            for q in range(2):
                carry[rws[q], :SW] = fin[q][0]
                carry[rws[q], SW:] = fin[q][1]
            for q in range(2):
                y = (jnp.dot(xb[q][...].astype(BF16), c_ref[pair[q]], preferred_element_type=F32)
                     + d_ref[:, cols[q]] * ug[q])
                y_ref[:, cols[q]] = y
                z_ref[:, cols[q]] = _gelu(y).astype(BF16)

    rowblk = pl.BlockSpec((R, D), lambda i: (i, 0))
    return pl.pallas_call(
        body, name=name, grid=(nb,),
        in_specs=[rowblk, _whole(tb["lam"].shape), _whole(tb["lamT"].shape), _whole(tb["bbd"].shape),
                  _whole(tb["cbd"].shape), _whole(dvec.shape)],
        out_specs=[rowblk, rowblk, pl.BlockSpec((1, crows, 2 * SW), lambda i: (i, 0, 0))],
        out_shape=[jax.ShapeDtypeStruct((L, D), F32), jax.ShapeDtypeStruct((L, D), BF16),
                   jax.ShapeDtypeStruct((nb, crows, 2 * SW), F32)],
        scratch_shapes=[pltpu.VMEM((crows, 2 * SW), F32), pltpu.VMEM((4, R, 2 * SW), F32),
                        pltpu.VMEM((4, NSEG, 2 * SW), F32)],
        compiler_params=_params("arbitrary"),
    )(up, tb["lam"], tb["lamT"], tb["bbd"], tb["cbd"], dvec)


def _s5_bwd(up, y, dz, ck, tb, dvec, *, name):
    L, D = up.shape
    R = min(S5_R, L)
    nb, ta, ngb = L // R, R // NSEG, D // 128
    SW = GPB * S5_P
    crows = ngb * NSEG

    def body(u_ref, y_ref, dz_ref, ck_ref, lam_ref, t_ref, b_ref, bt_ref, ct_ref, d_ref,
             du_ref, db_ref, dc_ref, dl_ref, dd_ref, gcarry, xbuf2, gbuf2, dybuf, cfx2):
        @pl.when(pl.program_id(0) == 0)
        def _():
            gcarry[...] = jnp.zeros_like(gcarry)
            db_ref[...] = jnp.zeros_like(db_ref)
            dc_ref[...] = jnp.zeros_like(dc_ref)
            dl_ref[...] = jnp.zeros_like(dl_ref)
            dd_ref[...] = jnp.zeros_like(dd_ref)

        zero = jnp.zeros((NSEG, SW), F32)
        dybuf[...] = dz_ref[...] * _gelu_grad(y_ref[...])
        for g0 in range(0, ngb, 2):
            pair = (g0, g0 + 1)
            xb = [xbuf2.at[g % 4] for g in pair]
            gbf = [gbuf2.at[g % 4] for g in pair]
            cf = [cfx2.at[g % 4] for g in pair]
            cols = [slice(g * 128, (g + 1) * 128) for g in pair]
            rws = [slice(g * NSEG, (g + 1) * NSEG) for g in pair]
            dyg = [dybuf[:, cols[q]] for q in range(2)]
            ug = [u_ref[:, cols[q]] for q in range(2)]
            lam = [(lam_ref[rws[q], :SW], lam_ref[rws[q], SW:]) for q in range(2)]
            for q in range(2):
                gbf[q][...] = jnp.dot(dyg[q].astype(BF16), ct_ref[pair[q]], preferred_element_type=F32)
                xb[q][0:NSEG, :] = ck_ref[0, rws[q], :]
                xb[q][NSEG:, :] = jnp.dot(ug[q].astype(BF16), b_ref[pair[q]], preferred_element_type=F32)

            c = [(xb[q][0:NSEG, :SW], xb[q][0:NSEG, SW:]) for q in range(2)]
            for a in range(ta):
                o = slice((a + 1) * NSEG, (a + 2) * NSEG)
                for q in range(2):
                    (lr, li), (cr, ci) = lam[q], c[q]
                    nr = lr * cr - li * ci + xb[q][o, :SW]
                    ni = lr * ci + li * cr + xb[q][o, SW:]
                    xb[q][o, :SW] = nr
                    xb[q][o, SW:] = ni
                    c[q] = (nr, ni)

            def rscan(c, store, qs=(0, 1), xb=xb, gbf=gbf, lam=lam):
                c = list(c)
                for a in range(ta - 1, -1, -1):
                    o = slice(a * NSEG, (a + 1) * NSEG)
                    for q in qs:
                        lr, li = lam[q]
                        gr = gbf[q][o, :SW] + lr * c[q][0] + li * c[q][1]
                        gi = gbf[q][o, SW:] - li * c[q][0] + lr * c[q][1]
                        if store:
                            gbf[q][o, :SW] = gr
                            gbf[q][o, SW:] = gi
                            xr, xi = xb[q][o, :SW], xb[q][o, SW:]
                            c[q] = (gr, gi, c[q][2] + gr * xr + gi * xi, c[q][3] + gi * xr - gr * xi)
                        else:
                            c[q] = (gr, gi)
                return c

            gends = rscan([(zero, zero)] * 2, False)
            for q in range(2):
                nxt = (gcarry[rws[q], :SW][0:1], gcarry[rws[q], SW:][0:1])
                _s5_carries(gends[q], nxt, t_ref, rws[q], SW, cf[q], reverse=True)
            fin = [(cf[q][:, :SW], cf[q][:, SW:], zero, zero) for q in range(2)]
            for q in range(2):
                fin = rscan(fin, True, qs=(q,))
            for q in range(2):
                gcarry[rws[q], :SW] = fin[q][0]
                gcarry[rws[q], SW:] = fin[q][1]
                dl_ref[rws[q], :SW] += fin[q][2]
                dl_ref[rws[q], SW:] += fin[q][3]
            gb16 = [gbf[q][...].astype(BF16) for q in range(2)]
            for q in range(2):
                du_ref[:, cols[q]] = (jnp.dot(gb16[q], bt_ref[pair[q]], preferred_element_type=F32)
                                      + d_ref[:, cols[q]] * dyg[q])
            for q in range(2):
                db_ref[pair[q]] += lax.dot_general(ug[q].astype(BF16), gb16[q], TN, preferred_element_type=F32)
            for q in range(2):
                dc_ref[pair[q]] += lax.dot_general(dyg[q].astype(BF16), xb[q][NSEG:, :].astype(BF16), TN,
                                                   preferred_element_type=F32)
                dd_ref[:, cols[q]] += _rsum(dyg[q] * ug[q])

    rev = pl.BlockSpec((R, D), lambda i: (nb - 1 - i, 0))
    acc3 = (ngb, 128, 2 * SW)
    return pl.pallas_call(
        body, name=name, grid=(nb,),
        in_specs=[rev, rev, rev, pl.BlockSpec((1, crows, 2 * SW), lambda i: (nb - 1 - i, 0, 0)),
                  _whole(tb["lam"].shape), _whole(tb["lamT"].shape), _whole(tb["bbd"].shape),
                  _whole(tb["bbdT"].shape), _whole(tb["cbdT"].shape), _whole(dvec.shape)],
        out_specs=[rev, _whole(acc3), _whole(acc3), _whole((crows, 2 * SW)), _whole((1, D))],
        out_shape=[jax.ShapeDtypeStruct((L, D), F32), jax.ShapeDtypeStruct(acc3, F32), jax.ShapeDtypeStruct(acc3, F32),
                   jax.ShapeDtypeStruct((crows, 2 * SW), F32), jax.ShapeDtypeStruct((1, D), F32)],
        scratch_shapes=[pltpu.VMEM((crows, 2 * SW), F32), pltpu.VMEM((4, R + NSEG, 2 * SW), F32),
                        pltpu.VMEM((4, R, 2 * SW), F32), pltpu.VMEM((R, D), F32), pltpu.VMEM((4, NSEG, 2 * SW), F32)],
        compiler_params=_params("arbitrary"),
    )(up, y, dz, ck, tb["lam"], tb["lamT"], tb["bbd"], tb["bbdT"], tb["cbdT"], dvec)


NN = (((1,), (0,)), ((), ()))
TN = (((0,), (0,)), ((), ()))
NT = (((1,), (1,)), ((), ()))


def _dot3(lhs, rhs, dn, split):
    x = rhs if split == "rhs" else lhs
    hi = x.astype(BF16)
    r1 = x - hi.astype(F32)
    mid = r1.astype(BF16)
    lo = (r1 - mid.astype(F32)).astype(BF16)
    out = None
    for part in (hi, mid, lo):
        ops = (lhs, part) if split == "rhs" else (part, rhs)
        t = lax.dot_general(ops[0], ops[1], dn, preferred_element_type=F32)
        out = t if out is None else out + t
    return out


def _log_sigmoid(x):
    return jnp.minimum(x, 0.0) - jnp.log(1.0 + jnp.exp(-jnp.abs(x)))


def _chunk_tri(rows, upper):
    r = lax.broadcasted_iota(jnp.int32, (rows, rows), 0)
    c = lax.broadcasted_iota(jnp.int32, (rows, rows), 1)
    same = (r // CHUNK) == (c // CHUNK)
    return (same & ((c >= r) if upper else (r >= c))).astype(BF16)


def _gla_block_gates(p_ref, wg_ref, bg_ref, QK, wbuf, gebuf):
    RB = p_ref.shape[0]
    glr = p_ref[:, 6 * QK:6 * QK + 128].astype(BF16)
    gpre = jnp.dot(glr, wg_ref[...], preferred_element_type=F32) + bg_ref[...]
    la = _log_sigmoid(gpre) * (1.0 / GATE_TAU)
    gc = _dot3(_chunk_tri(RB, False), la, NN, "rhs")
    for cc in range(RB // CHUNK):
        rows = slice(cc * CHUNK, (cc + 1) * CHUNK)
        ge = gc[(cc + 1) * CHUNK - 1:(cc + 1) * CHUNK, :]
        gebuf[cc:cc + 1, :] = ge
        wbuf[rows, :] = jnp.exp(ge - gc[rows, :])
    return glr, gpre, la


def _as_column(row, lanes):
    t = jnp.transpose(jnp.broadcast_to(row, (row.shape[1], row.shape[1])))
    return jnp.concatenate([t] * (lanes // row.shape[1]), axis=1)


def _as_row(col):
    return jnp.transpose(jnp.broadcast_to(col, (col.shape[0], col.shape[0])))[0:1, :]


def _gla_fwd(proj, wg2p, bg, gn, *, name):
    L = proj.shape[0]
    QK = wg2p.shape[1]
    DK, DV = QK // HEADS, 2 * QK // HEADS
    nC = L // CHUNK
    NB = min(GLA_NB, nC)
    assert nC % NB == 0
    scale = DK ** -0.5

    def body(p_ref, wg_ref, bg_ref, gn_ref, og_ref, s_ref, sst, wbuf, gebuf):
        @pl.when(pl.program_id(0) == 0)
        def _():
            sst[...] = jnp.zeros_like(sst)

        _gla_block_gates(p_ref, wg_ref, bg_ref, QK, wbuf, gebuf)
        heads = range(HEADS)
        ks = [slice(h * DK, (h + 1) * DK) for h in heads]
        vs = [slice(h * DV, (h + 1) * DV) for h in heads]
        units = [(cc, h) for cc in range(NB) for h in heads]
        rows = [slice(cc * CHUNK, (cc + 1) * CHUNK) for cc in range(NB)]
        kv = {(cc, h): lax.dot_general(
            (p_ref[rows[cc], QK + h * DK:QK + (h + 1) * DK] * wbuf[rows[cc], ks[h]]).astype(BF16),
            p_ref[rows[cc], 2 * QK + h * DV:2 * QK + (h + 1) * DV].astype(BF16), TN, preferred_element_type=F32)
            for cc, h in units}
        S16 = {}
        for cc, h in units:
            S = jnp.exp(_as_column(gebuf[cc:cc + 1, ks[h]], DV)) * sst[ks[h], :] + kv[cc, h]
            sst[ks[h], :] = S
            s_ref[cc, ks[h], :] = S
            S16[cc, h] = S.astype(BF16)
        o = {(cc, h): jnp.dot((p_ref[rows[cc], h * DK:(h + 1) * DK] * scale).astype(BF16), S16[cc, h],
                              preferred_element_type=F32) for cc, h in units}
        for cc, h in units:
            r = p_ref[rows[cc], 4 * QK + h * DV:4 * QK + (h + 1) * DV]
            on = o[cc, h] * lax.rsqrt(jnp.mean(o[cc, h] * o[cc, h], axis=-1, keepdims=True) + EPS)
            og_ref[rows[cc], vs[h]] = (on * gn_ref[:, vs[h]] * (r * _sigmoid(r))).astype(BF16)

    RB = NB * CHUNK
    return pl.pallas_call(
        body, name=name, grid=(nC // NB,),
        in_specs=[pl.BlockSpec((RB, proj.shape[1]), lambda i: (i, 0)), _whole(wg2p.shape), _whole(bg.shape), _whole(gn.shape)],
        out_specs=[pl.BlockSpec((RB, 2 * QK), lambda i: (i, 0)), pl.BlockSpec((NB, QK, DV), lambda i: (i, 0, 0))],
        out_shape=[jax.ShapeDtypeStruct((L, 2 * QK), BF16), jax.ShapeDtypeStruct((nC, QK, DV), F32)],
        scratch_shapes=[pltpu.VMEM((QK, DV), F32), pltpu.VMEM((RB, QK), F32), pltpu.VMEM((8, QK), F32)],
        compiler_params=_params("arbitrary"),
    )(proj, wg2p, bg, gn)


def _gla_bwd(proj, dog, states, wg2p, bg, gn, *, name):
    L, W = proj.shape
    QK = wg2p.shape[1]
    DK, DV = QK // HEADS, 2 * QK // HEADS
    nC = L // CHUNK
    NB = min(GLA_NB, nC)
    nB = nC // NB
    scale = DK ** -0.5

    def body(p_ref, dog_ref, sc_ref, sp_ref, wg_ref, bg_ref, gn_ref, dp_ref, dwg_ref, dbg_ref, dgn_ref,
             gst, wbuf, gebuf, dwwbuf, dgebuf):
        i = pl.program_id(0)

        @pl.when(i == 0)
        def _():
            gst[...] = jnp.zeros_like(gst)
            dwg_ref[...] = jnp.zeros_like(dwg_ref)
            dbg_ref[...] = jnp.zeros_like(dbg_ref)
            dgn_ref[...] = jnp.zeros_like(dgn_ref)

        RB = NB * CHUNK
        glr, gpre, _ = _gla_block_gates(p_ref, wg_ref, bg_ref, QK, wbuf, gebuf)
        heads = range(HEADS)
        ks = [slice(h * DK, (h + 1) * DK) for h in heads]
        vs = [slice(h * DV, (h + 1) * DV) for h in heads]
        units = [(cc, h) for cc in range(NB) for h in heads]
        rws = [slice(cc * CHUNK, (cc + 1) * CHUNK) for cc in range(NB)]
        qs16 = {(cc, h): (p_ref[rws[cc], h * DK:(h + 1) * DK] * scale).astype(BF16) for cc, h in units}
        S16a = {(cc, h): sc_ref[cc, ks[h], :].astype(BF16) for cc, h in units}
        oa = {u: jnp.dot(qs16[u], S16a[u], preferred_element_type=F32) for u in units}
        doa = {}
        for cc, h in units:
            r = p_ref[rws[cc], 4 * QK + h * DV:4 * QK + (h + 1) * DV]
            o = oa[cc, h]
            rs = lax.rsqrt(jnp.mean(o * o, axis=-1, keepdims=True) + EPS)
            on = o * rs
            sr = _sigmoid(r)
            dg = dog_ref[rws[cc], vs[h]]
            gnh = gn_ref[:, vs[h]]
            dp_ref[rws[cc], 4 * QK + h * DV:4 * QK + (h + 1) * DV] = (
                dg * on * gnh * (sr * (1.0 + r * (1.0 - sr)))).astype(BF16)
            dt = dg * (r * sr)
            dgn_ref[:, vs[h]] += _rsum(dt * on)
            don = dt * gnh
            doa[cc, h] = (rs * (don - on * jnp.mean(don * on, axis=-1, keepdims=True))).astype(BF16)
        dqa = {u: lax.dot_general(doa[u], S16a[u], NT, preferred_element_type=F32) for u in units}
        for cc, h in units:
            dp_ref[rws[cc], h * DK:(h + 1) * DK] = (dqa[cc, h] * scale).astype(BF16)
        for cc in range(NB - 1, -1, -1):
            rows = rws[cc]
            do = [doa[cc, h] for h in heads]
            Gc = [gst[ks[h], :] + lax.dot_general(qs16[cc, h], do[h], TN, preferred_element_type=F32) for h in heads]
            G16 = [g.astype(BF16) for g in Gc]
            kd = [p_ref[rows, QK + h * DK:QK + (h + 1) * DK] * wbuf[rows, ks[h]] for h in heads]
            dkd = [lax.dot_general(p_ref[rows, 2 * QK + h * DV:2 * QK + (h + 1) * DV].astype(BF16), G16[h], NT,
                                   preferred_element_type=F32) for h in heads]
            dv = [jnp.dot(kd[h].astype(BF16), G16[h], preferred_element_type=F32) for h in heads]
            for h in heads:
                if cc > 0:
                    Sp = sc_ref[cc - 1, ks[h], :]
                else:
                    Sp = jnp.where(i < nB - 1, sp_ref[0, ks[h], :], 0.0)
                dp_ref[rows, 2 * QK + h * DV:2 * QK + (h + 1) * DV] = dv[h].astype(BF16)
                ge = gebuf[cc:cc + 1, ks[h]]
                gst[ks[h], :] = jnp.exp(_as_column(ge, DV)) * Gc[h]
                ddec = _as_row(jnp.sum(Gc[h] * Sp, axis=1, keepdims=True))
                dp_ref[rows, QK + h * DK:QK + (h + 1) * DK] = (dkd[h] * wbuf[rows, ks[h]]).astype(BF16)
                dww = dkd[h] * kd[h]
                dwwbuf[rows, ks[h]] = dww
                dgebuf[cc:cc + 1, ks[h]] = jnp.exp(ge) * ddec + _rsum(dww)
        rev = _dot3(_chunk_tri(RB, True), dwwbuf[...], NN, "rhs")
        for cc in range(NB):
            rows = slice(cc * CHUNK, (cc + 1) * CHUNK)
            wbuf[rows, :] = dgebuf[cc:cc + 1, :] - rev[rows, :]
        dgpre = wbuf[...] * (1.0 / GATE_TAU) * (1.0 - _sigmoid(gpre))
        d16 = dgpre.astype(BF16)
        dp_ref[:, 6 * QK:6 * QK + 128] = lax.dot_general(d16, wg_ref[...], NT, preferred_element_type=F32).astype(BF16)
        dwg_ref[...] += lax.dot_general(glr, d16, TN, preferred_element_type=F32)
        dbg_ref[...] += _rsum(dgpre)

    RB = NB * CHUNK
    rev_idx = lambda i: (nB - 1 - i, 0)
    return pl.pallas_call(
        body, name=name, grid=(nB,),
        in_specs=[pl.BlockSpec((RB, W), rev_idx), pl.BlockSpec((RB, 2 * QK), rev_idx),
                  pl.BlockSpec((NB, QK, DV), lambda i: (nB - 1 - i, 0, 0)),
                  pl.BlockSpec((1, QK, DV), lambda i: (jnp.maximum(NB * (nB - 1 - i) - 1, 0), 0, 0)),
                  _whole(wg2p.shape), _whole(bg.shape), _whole(gn.shape)],
        out_specs=[pl.BlockSpec((RB, W), rev_idx), _whole((128, QK)), _whole((1, QK)), _whole((1, 2 * QK))],
        out_shape=[jax.ShapeDtypeStruct((L, W), BF16), jax.ShapeDtypeStruct((128, QK), F32),
                   jax.ShapeDtypeStruct((1, QK), F32), jax.ShapeDtypeStruct((1, 2 * QK), F32)],
        scratch_shapes=[pltpu.VMEM((QK, DV), F32), pltpu.VMEM((RB, QK), F32), pltpu.VMEM((8, QK), F32),
                        pltpu.VMEM((RB, QK), F32), pltpu.VMEM((8, QK), F32)],
        compiler_params=_params("arbitrary"),
    )(proj, dog, states, states, wg2p, bg, gn)


def _coords():
    return lax.axis_index("x"), lax.axis_index("y"), lax.axis_index("c")


def _other_chips(x, y):
    return [(1 - x, y, 2 * (1 - x) + y), (x, 1 - y, 2 * x + 1 - y), (1 - x, 1 - y, 2 * (1 - x) + 1 - y)]


def _hbm_call(body, ins, out_shapes, n_sems, *, name, alias=False):
    any_spec = pl.BlockSpec(memory_space=pl.ANY)
    return pl.pallas_call(
        body, name=name, in_specs=[any_spec] * len(ins), out_specs=[any_spec] * len(out_shapes), out_shape=out_shapes,
        scratch_shapes=[pltpu.SemaphoreType.DMA((n,)) for n in n_sems],
        input_output_aliases={k: k for k in range(len(ins))} if alias else {},
    )(*ins)


def _exchange(src, masks, *, name):
    vary = [any(m[k] for m in masks) for k in range(3)]
    nslots = 2 ** sum(vary)
    n = len(masks)

    def slot(coords):
        s = 0
        for k in range(3):
            if vary[k]:
                s = s * 2 + coords[k]
        return s

    def body(src_ref, dst_ref, send_sems, recv_sems, loc_sem):
        me = _coords()
        mine = slot(me)
        loc = pltpu.make_async_copy(src_ref, dst_ref.at[mine], loc_sem.at[0])
        loc.start()
        copies = []
        for k, m in enumerate(masks):
            peer = tuple(1 - me[d] if m[d] else me[d] for d in range(3))
            cp = pltpu.make_async_remote_copy(src_ref=src_ref, dst_ref=dst_ref.at[mine], send_sem=send_sems.at[k],
                                              recv_sem=recv_sems.at[k], device_id=peer, device_id_type=MESH)
            cp.start()
            copies.append(cp)
        for cp in copies:
            cp.wait()
        loc.wait()

    return _hbm_call(body, [src], [jax.ShapeDtypeStruct((nslots,) + tuple(src.shape), src.dtype)], (n, n, 1), name=name)[0]


def _cast_into(t, lead, kind, chip, *, name, tm=256):
    r, cc = t.shape[-2:]
    tm = min(tm, r)
    nblk = r // tm
    if kind == "col":
        shp, o_spec = (r, NCH * cc), pl.BlockSpec((tm, cc), lambda i, s: (i, s[0]))
    elif kind == "row":
        shp, o_spec = (NCH * r, cc), pl.BlockSpec((tm, cc), lambda i, s: (s[0] * nblk + i, 0))
    else:
        shp, o_spec = (NCH, r, cc), pl.BlockSpec((None, tm, cc), lambda i, s: (s[0], i, 0))

    def body(s_ref, t_ref, o_ref):
        o_ref[...] = t_ref[...].astype(o_ref.dtype)

    return pl.pallas_call(
        body, name=name,
        grid_spec=pltpu.PrefetchScalarGridSpec(
            num_scalar_prefetch=1, grid=(nblk,),
            in_specs=[pl.BlockSpec((None, tm, cc), lambda i, s: (lead, i, 0))], out_specs=o_spec),
        out_shape=jax.ShapeDtypeStruct(shp, BF16), compiler_params=_params("parallel"),
    )(chip.reshape(1).astype(jnp.int32), t)


def _gather_weights(arrs, shard_shapes, kinds, *, name):
    n = len(arrs)

    def body(*refs):
        dst = refs[n:2 * n]
        send_sems, recv_sems = refs[2 * n:]
        x, y, c = _coords()
        chip = 2 * x + y
        others = _other_chips(x, y)
        sib = (x, y, 1 - c)

        def window(p, chip_id, cc):
            r, cols = shard_shapes[p]
            h = r // 2
            if kinds[p] == "col":
                return dst[p].at[pl.ds(cc * h, h), pl.ds(pl.multiple_of(chip_id * cols, 128), cols)]
            if kinds[p] == "row":
                return dst[p].at[pl.ds(chip_id * r + cc * h, h), :]
            return dst[p].at[chip_id, pl.ds(cc * h, h), :]

        def copy(p, k, win, to):
            return pltpu.make_async_remote_copy(src_ref=win, dst_ref=win, send_sem=send_sems.at[6 * p + k],
                                                recv_sem=recv_sems.at[6 * p + k], device_id=to, device_id_type=MESH)

        sends = []
        for p in range(n):
            for j, (ox, oy, _) in enumerate(others):
                cp = copy(p, j, window(p, chip, c), (ox, oy, c))
                cp.start()
                sends.append(cp)
        for j, (_, _, oc) in enumerate(others):
            for p in range(n):
                copy(p, j, window(p, oc, c), (x, y, c)).wait_recv()
                fw = copy(p, 3 + j, window(p, oc, c), sib)
                fw.start()
                sends.append(fw)
        for p in range(n):
            for j, (_, _, oc) in enumerate(others):
                copy(p, 3 + j, window(p, oc, 1 - c), sib).wait_recv()
        for cp in sends:
            cp.wait_send()

    outs = [jax.ShapeDtypeStruct(a.shape, a.dtype) for a in arrs]
    return _hbm_call(body, arrs, outs, (6 * n, 6 * n), name=name, alias=True)


HBM_SPEC = pl.BlockSpec(memory_space=pltpu.HBM)
SEM_SPEC = pl.BlockSpec(memory_space=pltpu.SEMAPHORE)
EFFECT = pltpu.SideEffectType.DATAFLOW_SIDE_EFFECTING


def _window(ref, shard_shape, kind, chip_id, cc):
    r, cols = shard_shape
    h = r // 2
    if kind == "col":
        return ref.at[pl.ds(cc * h, h), pl.ds(pl.multiple_of(chip_id * cols, 128), cols)]
    if kind == "row":
        return ref.at[pl.ds(chip_id * r + cc * h, h), :]
    return ref.at[chip_id, pl.ds(cc * h, h), :]


def _split_start(start, arrs, n_sems, *, name):
    n, ns = len(arrs), len(n_sems)

    def body(*refs):
        start(refs[:n], refs[n:n + ns])
        refs[-1][...] = jnp.zeros_like(refs[-1])

    outs = pl.pallas_call(
        body, name=name,
        out_shape=tuple([pltpu.SemaphoreType.DMA((k,)) for k in n_sems] + [pltpu.HBM(a.shape, a.dtype) for a in arrs]
                        + [jax.ShapeDtypeStruct((8, 128), F32)]),
        in_specs=[HBM_SPEC] * n, out_specs=tuple([SEM_SPEC] * ns + [HBM_SPEC] * n + [pl.BlockSpec(memory_space=pltpu.VMEM)]),
        input_output_aliases={k: ns + k for k in range(n)},
        compiler_params=pltpu.CompilerParams(has_side_effects=EFFECT),
    )(*[pltpu.with_memory_space_constraint(a, pltpu.HBM) for a in arrs])
    return list(outs[:ns]), list(outs[ns:ns + n]), outs[-1]


def _split_wait(wait, arrs, sems, after, *, name):
    n, ns = len(arrs), len(sems)

    def body(*refs):
        wait(refs[:n], refs[n:n + ns])

    return pl.pallas_call(
        body, name=name, out_shape=tuple(pltpu.HBM(a.shape, a.dtype) for a in arrs),
        in_specs=[HBM_SPEC] * n + [SEM_SPEC] * ns + [pl.BlockSpec(memory_space=pl.ANY)], out_specs=tuple([HBM_SPEC] * n),
        input_output_aliases={k: k for k in range(n)},
        compiler_params=pltpu.CompilerParams(has_side_effects=EFFECT),
    )(*arrs, *sems, after)


def _gw_copies(refs, send_sems, recv_sems, shard_shapes, kinds, outgoing):
    x, y, c = _coords()
    chip = 2 * x + y
    out = []
    for p in range(len(refs)):
        for j, (ox, oy, oc) in enumerate(_other_chips(x, y)):
            win = _window(refs[p], shard_shapes[p], kinds[p], chip if outgoing else oc, c)
            out.append(pltpu.make_async_remote_copy(
                src_ref=win, dst_ref=win, send_sem=send_sems.at[3 * p + j], recv_sem=recv_sems.at[3 * p + j],
                device_id=(ox, oy, c), device_id_type=MESH))
    return out


def _gw_start(arrs, shard_shapes, kinds, groups, *, name):
    def start(refs, sems):
        for g, idx in enumerate(groups):
            for cp in _gw_copies([refs[p] for p in idx], sems[2 * g], sems[2 * g + 1], [shard_shapes[p] for p in idx],
                                 [kinds[p] for p in idx], True):
                cp.start()

    n_sems = [3 * len(idx) for idx in groups for _ in range(2)]
    sems, thru, token = _split_start(start, arrs, n_sems, name=name)
    return [(sems[2 * g], sems[2 * g + 1]) for g in range(len(groups))], thru, token


def _gw_wait(arrs, shard_shapes, kinds, sem_pair, after, *, name):
    def wait(refs, sems):
        for cp in _gw_copies(refs, sems[0], sems[1], shard_shapes, kinds, True):
            cp.wait_send()
        for cp in _gw_copies(refs, sems[0], sems[1], shard_shapes, kinds, False):
            cp.wait_recv()

    return _split_wait(wait, arrs, list(sem_pair), after, name=name)


def _gw_forward_copies(refs, shard_shapes, kinds, send_sems, recv_sems, incoming):
    x, y, c = _coords()
    out = []
    for p in range(len(refs)):
        for j, (_, _, oc) in enumerate(_other_chips(x, y)):
            win = _window(refs[p], shard_shapes[p], kinds[p], oc, 1 - c if incoming else c)
            out.append(pltpu.make_async_remote_copy(
                src_ref=win, dst_ref=win, send_sem=send_sems.at[3 * p + j], recv_sem=recv_sems.at[3 * p + j],
                device_id=(x, y, 1 - c), device_id_type=MESH))
    return out


def _gw_forward(arrs, shard_shapes, kinds, *, name):
    n = len(arrs)

    def body(*refs):
        dst, (send_sems, recv_sems) = refs[n:2 * n], refs[2 * n:]
        sends = _gw_forward_copies(dst, shard_shapes, kinds, send_sems, recv_sems, False)
        for cp in sends:
            cp.start()
        for cp in _gw_forward_copies(dst, shard_shapes, kinds, send_sems, recv_sems, True):
            cp.wait_recv()
        for cp in sends:
            cp.wait_send()

    outs = [jax.ShapeDtypeStruct(a.shape, a.dtype) for a in arrs]
    return _hbm_call(body, arrs, outs, (3 * n, 3 * n), name=name, alias=True)


def _gw_forward_start(arrs, shard_shapes, kinds, *, name):
    def start(refs, sems):
        for cp in _gw_forward_copies(refs, shard_shapes, kinds, sems[0], sems[1], False):
            cp.start()

    sems, thru, token = _split_start(start, arrs, [3 * len(arrs)] * 2, name=name)
    return (sems[0], sems[1]), thru, token


def _gw_forward_wait(pair, arrs, shard_shapes, kinds, after, *, name):
    def wait(refs, sems):
        for cp in _gw_forward_copies(refs, shard_shapes, kinds, sems[0], sems[1], False):
            cp.wait_send()
        for cp in _gw_forward_copies(refs, shard_shapes, kinds, sems[0], sems[1], True):
            cp.wait_recv()

    return _split_wait(wait, arrs, list(pair), after, name=name)


def _rs_chips_copies(parts, lands, send_sems, recv_sems):
    x, y, c = _coords()
    chip = 2 * x + y
    out = []
    for p in range(len(parts)):
        for j, (ox, oy, oc) in enumerate(_other_chips(x, y)):
            out.append(pltpu.make_async_remote_copy(
                src_ref=parts[p].at[oc], dst_ref=lands[p].at[chip], send_sem=send_sems.at[3 * p + j],
                recv_sem=recv_sems.at[3 * p + j], device_id=(ox, oy, c), device_id_type=MESH))
    return out


def _rs_chips_start(parts, *, name):
    n = len(parts)

    def start(refs, sems):
        for cp in _rs_chips_copies(refs[:n], refs[n:], sems[0], sems[1]):
            cp.start()

    lands = [lax.empty(t.shape, t.dtype) for t in parts]
    sems, thru, token = _split_start(start, list(parts) + lands, [3 * n, 3 * n], name=name)
    return (sems[0], sems[1]), thru[:n], thru[n:], token


def _rs_chips_wait(groups, after, *, name):
    sizes = [len(g[1]) for g in groups]
    arrs = [a for g in groups for a in list(g[1]) + list(g[2])]
    sems = [s for g in groups for s in g[0]]

    def wait(refs, sem_refs):
        o = 0
        for k, n in enumerate(sizes):
            for cp in _rs_chips_copies(refs[o:o + n], refs[o + n:o + 2 * n], sem_refs[2 * k], sem_refs[2 * k + 1]):
                cp.wait()
            o += 2 * n

    outs = _split_wait(wait, arrs, sems, after, name=name)
    res, o = [], 0
    for n in sizes:
        res.append((list(outs[o:o + n]), list(outs[o + n:o + 2 * n])))
        o += 2 * n
    return res


def _rs_cores_copies(grads, lands, send_sems, recv_sems):
    x, y, c = _coords()
    out, o = [], 0
    for p in range(len(grads)):
        nsh, h = lands[p].shape[0], lands[p].shape[1]
        for j in range(nsh):
            out.append(pltpu.make_async_remote_copy(
                src_ref=grads[p].at[j, pl.ds((1 - c) * h, h), :], dst_ref=lands[p].at[j],
                send_sem=send_sems.at[o + j], recv_sem=recv_sems.at[o + j], device_id=(x, y, 1 - c), device_id_type=MESH))
        o += nsh
    return out


def _rs_cores_start(grads, *, name):
    n = len(grads)
    tot = sum(g.shape[0] for g in grads)

    def start(refs, sems):
        for cp in _rs_cores_copies(refs[:n], refs[n:], sems[0], sems[1]):
            cp.start()

    lands = [lax.empty((g.shape[0], g.shape[1] // 2, g.shape[2]), g.dtype) for g in grads]
    sems, thru, token = _split_start(start, list(grads) + lands, [tot, tot], name=name)
    return (sems[0], sems[1]), thru[:n], thru[n:], token


def _rs_cores_wait(pair, grads, lands, after, *, name):
    n = len(grads)

    def wait(refs, sems):
        for cp in _rs_cores_copies(refs[:n], refs[n:], sems[0], sems[1]):
            cp.wait()

    outs = _split_wait(wait, list(grads) + list(lands), list(pair), after, name=name)
    return list(outs[:n]), list(outs[n:])


def _rs_cores(grads, *, name):
    n = len(grads)
    outs = [jax.ShapeDtypeStruct((g.shape[0], g.shape[1] // 2, g.shape[2]), g.dtype) for g in grads]

    def body(*refs):
        src, dst = refs[:n], refs[n:2 * n]
        send_sems, recv_sems = refs[2 * n:]
        x, y, c = _coords()
        copies = []
        for p in range(n):
            nsh, r, _ = grads[p].shape
            h = r // 2
            for j in range(nsh):
                cp = pltpu.make_async_remote_copy(
                    src_ref=src[p].at[j, pl.ds((1 - c) * h, h), :], dst_ref=dst[p].at[j],
                    send_sem=send_sems.at[nsh * p + j], recv_sem=recv_sems.at[nsh * p + j],
                    device_id=(x, y, 1 - c), device_id_type=MESH)
                cp.start()
                copies.append(cp)
        for cp in copies:
            cp.wait()

    tot = sum(g.shape[0] for g in grads)
    return _hbm_call(body, grads, outs, (tot, tot), name=name)


def _sum_own_half(full, recv, ci, out_dtype, *, name):
    nsh, h, cols = recv.shape
    tm = h if nsh * h * cols * 4 <= (2 << 20) else _tile_rows(h, 256)
    nblk = h // tm

    def body(c_ref, f_ref, r_ref, o_ref):
        o_ref[...] = (f_ref[...] + r_ref[...]).astype(o_ref.dtype)

    return pl.pallas_call(
        body, name=name,
        grid_spec=pltpu.PrefetchScalarGridSpec(
            num_scalar_prefetch=1, grid=(nsh, nblk),
            in_specs=[pl.BlockSpec((1, tm, cols), lambda j, i, c_ref: (j, c_ref[0] * nblk + i, 0)),
                      pl.BlockSpec((1, tm, cols), lambda j, i, c_ref: (j, i, 0))],
            out_specs=pl.BlockSpec((1, tm, cols), lambda j, i, c_ref: (j, i, 0))),
        out_shape=jax.ShapeDtypeStruct((nsh, h, cols), out_dtype), compiler_params=_params("parallel", "parallel"),
    )(ci.reshape(1).astype(jnp.int32), full, recv)


def _sum_chips(recv, own, chip, ci, *, name, nlead=1, lead=0, prev=None, spread=False):
    nsh, h, cols = recv.shape
    tm = h if nsh * h * cols * 4 <= (2 << 20) else _tile_rows(h, 256)
    nblk = h // tm
    rows_out = 2 * h * (nsh if spread else 1)

    def body(s_ref, r_ref, o_ref, *rest):
        out_ref = rest[-1]
        t = None
        for s in range(nsh):
            v = jnp.where(s_ref[0] == s, o_ref[s], r_ref[s]).astype(F32)
            t = v if t is None else t + v
        out_ref[...] = t

    def out_idx(i, s):
        return (lead, (s[0] * 2 * nblk if spread else 0) + s[1] * nblk + i, 0)

    blk = pl.BlockSpec((nsh, tm, cols), lambda i, s: (0, i, 0))
    ins = [recv, own] + ([prev] if prev is not None else [])
    return pl.pallas_call(
        body, name=name,
        grid_spec=pltpu.PrefetchScalarGridSpec(
            num_scalar_prefetch=1, grid=(nblk,),
            in_specs=[blk, blk] + ([pl.BlockSpec(memory_space=pl.ANY)] if prev is not None else []),
            out_specs=pl.BlockSpec((None, tm, cols), out_idx)),
        out_shape=jax.ShapeDtypeStruct((nlead, rows_out, cols), F32),
        input_output_aliases={3: 0} if prev is not None else {},
        compiler_params=_params("arbitrary"),
    )(jnp.stack([chip, ci]).astype(jnp.int32), *ins)


RS_GATHER_CHUNKS = 4


def _rs_gather_copies(refs, nleads, halves, spread, send_sems, recv_sems, incoming):
    x, y, c = _coords()
    chip = 2 * x + y
    out, sem = [], 0
    for p in range(len(refs)):
        h = halves[p]
        q = h // RS_GATHER_CHUNKS
        base = (chip * 2 * h if spread[p] else 0) + (1 - c if incoming else c) * h
        for l in range(nleads[p]):
            for k in range(RS_GATHER_CHUNKS):
                win = refs[p].at[l, pl.ds(base + k * q, q), :]
                out.append(pltpu.make_async_remote_copy(src_ref=win, dst_ref=win, send_sem=send_sems.at[sem],
                                                        recv_sem=recv_sems.at[sem], device_id=(x, y, 1 - c),
                                                        device_id_type=MESH))
                sem += 1
    return out


def _rs_gather_start(arrs, halves, spread, *, name):
    nleads = [a.shape[0] for a in arrs]
    tot = sum(nleads) * RS_GATHER_CHUNKS

    def start(refs, sems):
        for cp in _rs_gather_copies(refs, nleads, halves, spread, sems[0], sems[1], False):
            cp.start()

    sems, thru, token = _split_start(start, arrs, [tot, tot], name=name)
    return (sems[0], sems[1]), thru, token


def _rs_gather_wait(pair, arrs, halves, spread, after, *, name):
    nleads = [a.shape[0] for a in arrs]

    def wait(refs, sems):
        for cp in _rs_gather_copies(refs, nleads, halves, spread, sems[0], sems[1], False):
            cp.wait_send()
        for cp in _rs_gather_copies(refs, nleads, halves, spread, sems[0], sems[1], True):
            cp.wait_recv()

    return _split_wait(wait, arrs, list(pair), after, name=name)


def _adamw(w, g, m, v, *, name):
    nl, R, C = w.shape
    tm = _tile_rows(R, 256)

    blk = pl.BlockSpec((None, tm, C), lambda l, i: (l, i, 0))
    return pl.pallas_call(
        _adamw_body_copy(), name=name, grid=(nl, R // tm), in_specs=[blk] * 4, out_specs=[blk] * 3,
        out_shape=[jax.ShapeDtypeStruct((nl, R, C), F32)] * 3, compiler_params=_params("parallel", "parallel"),
    )(w, g, m, v)


def _adamw_body(w_ref, g_ref, m_ref, v_ref, d_ref, nm_ref, nv_ref):
    gg = g_ref[...]
    nm = B1 * m_ref[...] + (1.0 - B1) * gg
    nv = B2 * v_ref[...] + (1.0 - B2) * (gg * gg)
    m_hat = nm / (1.0 - B1 ** ASTEP)
    v_hat = nv / (1.0 - B2 ** ASTEP)
    d_ref[...] = -LR * (m_hat / (jnp.sqrt(v_hat) + AEPS) + WD * w_ref[...])
    nm_ref[...] = nm
    nv_ref[...] = nv


def _adamw_whole(w, g, m, v, *, name):
    return pl.pallas_call(_adamw_body_copy(), name=name, out_shape=[jax.ShapeDtypeStruct(w.shape, F32)] * 3,
                          compiler_params=_params())(w, g, m, v)


def _adamw_body_copy():
    def body(*refs):
        _adamw_body(*refs)
    return body


def _mod_cols(c_all, w_ada, b_cols, *, name):
    nl, D, cols = w_ada.shape
    B = c_all.shape[0]

    def body(c_ref, w_ref, b_ref, o_ref):
        cc = c_ref[...]
        cs = (cc * _sigmoid(cc)).astype(BF16)
        o_ref[0] = jnp.dot(cs, w_ref[0].astype(BF16), preferred_element_type=F32) + b_ref[0]

    return pl.pallas_call(
        body, name=name, grid=(nl,),
        in_specs=[_whole(c_all.shape), pl.BlockSpec((1, D, cols), lambda i: (i, 0, 0)), pl.BlockSpec((1, 1, cols), lambda i: (i, 0, 0))],
        out_specs=pl.BlockSpec((1, B, cols), lambda i: (i, 0, 0)),
        out_shape=jax.ShapeDtypeStruct((nl, B, cols), F32), compiler_params=_params("arbitrary"),
    )(c_all, w_ada, b_cols)


def _ada_grad(c_all, dmod_cols, *, name):
    nl, B, cols = dmod_cols.shape
    D = c_all.shape[1]

    def body(c_ref, d_ref, o_ref):
        cc = c_ref[...]
        cs = (cc * _sigmoid(cc)).astype(BF16)
        o_ref[0] = lax.dot_general(cs, d_ref[0].astype(BF16), TN, preferred_element_type=F32)

    return pl.pallas_call(
        body, name=name, grid=(nl,),
        in_specs=[_whole(c_all.shape), pl.BlockSpec((1, B, cols), lambda i: (i, 0, 0))],
        out_specs=pl.BlockSpec((1, D, cols), lambda i: (i, 0, 0)),
        out_shape=jax.ShapeDtypeStruct((nl, D, cols), F32), compiler_params=_params("arbitrary"),
    )(c_all, dmod_cols)


def _s5_disc(a_re, a_im, log_dt, b_re, b_im):
    dt = jnp.exp(log_dt)[:, None]
    mag = jnp.exp(a_re * dt)
    ph = a_im * dt
    lb_re = mag * jnp.cos(ph)
    lb_im = mag * jnp.sin(ph)
    den = a_re * a_re + a_im * a_im
    nr = lb_re - 1.0
    ni = lb_im
    f_re = (nr * a_re + ni * a_im) / den
    f_im = (ni * a_re - nr * a_im) / den
    bb_re = f_re[..., None] * b_re - f_im[..., None] * b_im
    bb_im = f_re[..., None] * b_im + f_im[..., None] * b_re
    return lb_re, lb_im, bb_re, bb_im


def _to_segments(t):
    L, D = t.shape
    R = min(S5_R, L)
    return t.reshape(L // R, NSEG, R // NSEG, D).transpose(0, 2, 1, 3).reshape(L, D)


def _from_segments(t):
    L, D = t.shape
    R = min(S5_R, L)
    return t.reshape(L // R, R // NSEG, NSEG, D).transpose(0, 2, 1, 3).reshape(L, D)


def _w_in_layout(QK, D, ncols):
    segs = [(0, 4 * QK, 0), (4 * QK + GATE_RANK, 4 * QK + GATE_RANK + D, 4 * QK), (4 * QK, 4 * QK + GATE_RANK, 4 * QK + D)]
    fwd = []
    for lo, hi, _ in segs:
        col = lo
        while col < hi:
            j = col // ncols
            end = min(hi, (j + 1) * ncols)
            fwd.append((j, col - j * ncols, end - j * ncols))
            col = end
    bwd = []
    for j in range(NCH):
        ranges, col = [], j * ncols
        while col < (j + 1) * ncols:
            lo, hi, rlo = next(sg for sg in segs if sg[0] <= col < sg[1])
            end = min((j + 1) * ncols, hi)
            ranges.append((rlo + col - lo, rlo + end - lo))
            col = end
        bwd.append(ranges)
    return fwd, bwd


def _mlp_fwd(h2, w1, w2, tag):
    a = _matmul(h2, w1, name=f"ff1_{tag}", tn=2048, out_dtypes=(BF16,), epi=lambda acc: (jnp.maximum(acc, 0.0),))
    f = _matmul(a, w2, name=f"ff2_{tag}", a_fn=jnp.square)
    return a, f


def _mlp_bwd(df, h2, a, w1, w2, tag):
    da = _matmul(df, w2, tb=True, name=f"ff2_dx_{tag}", tn=2048, out_dtypes=(BF16,), epi_ins=(a,),
                 epi=lambda acc, at: (acc * (2.0 * at.astype(F32)),))
    dw2 = _matmul(a, df, ta=True, name=f"ff2_dw_{tag}", a_fn=jnp.square)
    dh2 = _matmul(da, w1, tb=True, name=f"ff1_dx_{tag}")
    dw1 = _matmul(h2, da, ta=True, name=f"ff1_dw_{tag}", col_shards=NCH)
    return dh2, dw1, dw2


def kernel(x, c, w_ada, b_ada, norm_mix, norm_mlp, s5_a_re, s5_a_im, s5_log_dt, s5_b_re, s5_b_im, s5_c_re, s5_c_im, s5_d, s5_w_glu, gla_w_in, gla_w_gate2, gla_b_gate, gla_g_norm, gla_w_out, w_ff1, w_ff2, norm_final, loss_target, m_w_ada, m_b_ada, m_norm_mix, m_norm_mlp, m_s5_a_re, m_s5_a_im, m_s5_log_dt, m_s5_b_re, m_s5_b_im, m_s5_c_re, m_s5_c_im, m_s5_d, m_s5_w_glu, m_gla_w_in, m_gla_w_gate2, m_gla_b_gate, m_gla_g_norm, m_gla_w_out, m_w_ff1, m_w_ff2, m_norm_final, v_w_ada, v_b_ada, v_norm_mix, v_norm_mlp, v_s5_a_re, v_s5_a_im, v_s5_log_dt, v_s5_b_re, v_s5_b_im, v_s5_c_re, v_s5_c_im, v_s5_d, v_s5_w_glu, v_gla_w_in, v_gla_w_gate2, v_gla_b_gate, v_gla_g_norm, v_gla_w_out, v_w_ff1, v_w_ff2, v_norm_final):
    args = dict(locals())
    L, D = x.shape[1], x.shape[2]
    QK = D // 2
    xi, yi, ci = _coords()
    chip = 2 * xi + yi
    dev = 2 * chip + ci

    cat = jnp.concatenate([gla_w_gate2[0].reshape(1, -1), gla_b_gate, gla_g_norm], axis=1)
    first = _exchange(jnp.concatenate([c.reshape(8, D // 8), jnp.tile(cat, (8, 1))], axis=1), MASK_ALL, name="gather_c")
    c_all = first[:, :, :D // 8].reshape(8, D)
    cat_all = first[0::2, 0, D // 8:]
    acols = w_ada.shape[2]
    b_cols = lax.dynamic_slice_in_dim(b_ada, chip * acols, acols, axis=1)[:, None, :]
    mod_cols = _mod_cols(c_all, w_ada, b_cols, name="ada_mod")
    mod_all = _exchange(mod_cols.reshape(16, acols), MASK_CHIPS, name="gather_mod")
    mod_all = mod_all.reshape(NCH, 2, 8, acols).transpose(1, 2, 0, 3).reshape(2, 8, NCH * acols)
    mod = lax.dynamic_index_in_dim(mod_all, dev, axis=1, keepdims=False).reshape(2, 6, 1, D)

    big = [("s5_w_glu", s5_w_glu, 0, "col"), ("gla_w_in", gla_w_in, 0, "slot"), ("gla_w_out", gla_w_out, 0, "row"),
           ("w_ff1_0", w_ff1, 0, "col"), ("w_ff1_1", w_ff1, 1, "col"), ("w_ff2_0", w_ff2, 0, "row"), ("w_ff2_1", w_ff2, 1, "row")]
    own16 = [_cast_into(t, lead, kind, chip, name=f"cast_{nm}") for nm, t, lead, kind in big]
    wshapes, wkinds = [b[1].shape[-2:] for b in big], [b[3] for b in big]
    wgroups = [[0], [3, 5], [1, 2, 4, 6]]
    wsems, wthru, wtoken = _gw_start(own16, wshapes, wkinds, wgroups, name="gather_w_start")
    W = {}

    def weights_landed(g, after):
        idx = wgroups[g]
        shp, knd = [wshapes[p] for p in idx], [wkinds[p] for p in idx]
        return _gw_wait([wthru[p] for p in idx], shp, knd, wsems[g], after, name=f"gather_w_wait{g}"), shp, knd

    def weights_ready(g, arrs):
        for p, w in zip(wgroups[g], arrs):
            W[big[p][0]] = w

    def forward_start(g, after):
        got, shp, knd = weights_landed(g, after)
        pair, thru, token = _gw_forward_start(got, shp, knd, name=f"gather_w_cores_start{g}")
        return (pair, thru, shp, knd), token

    def forward_finish(g, state, after):
        pair, thru, shp, knd = state
        weights_ready(g, _gw_forward_wait(pair, thru, shp, knd, after, name=f"gather_w_cores_wait{g}"))

    qk4 = QK // NCH
    wg2 = cat_all[:, :GATE_RANK * qk4].reshape(NCH, GATE_RANK, qk4).transpose(1, 0, 2).reshape(GATE_RANK, QK)
    bg = cat_all[:, GATE_RANK * qk4:(GATE_RANK + 1) * qk4].reshape(1, QK)
    gn = cat_all[:, (GATE_RANK + 1) * qk4:].reshape(1, D)
    wg2p = jnp.concatenate([wg2, jnp.zeros((128 - GATE_RANK, QK), F32)], axis=0).astype(BF16)

    lb_re, lb_im, bb_re, bb_im = _s5_disc(s5_a_re[0], s5_a_im[0], s5_log_dt[0], s5_b_re[0], s5_b_im[0])
    tb = _s5_tables(lb_re, lb_im, bb_re, bb_im, s5_c_re[0], s5_c_im[0], min(S5_R, L) // NSEG)
    s5_dv = s5_d + wtoken[0, 0]

    def vec(t):
        return t.reshape(1, -1)

    m0, m1 = mod[0], mod[1]
    xp = _to_segments(x[0])
    (u0,) = _rows(lambda t, g, sc, sh: (_norm_mod(t, g, sc, sh),), [xp], [vec(norm_mix[0]), m0[1], m0[0]],
                  [(D, F32)], [], name="pre_mix0")
    y0, z0, ck0 = _s5_fwd(u0, tb, s5_dv, name="s5_fwd")
    got, shp, knd = weights_landed(0, z0)
    weights_ready(0, _gw_forward(got, shp, knd, name="gather_w_cores0"))
    vg0 = _matmul(z0, W["s5_w_glu"], name="glu", tn=2048)
    fwd1, ftok1 = forward_start(1, vg0)

    def res_glu_pre(xt, vgt, gt, g, sc, sh):
        xn = xt + gt * (vgt[:, :D] * _sigmoid(vgt[:, D:]))
        return xn, _norm_mod(xn, g, sc, sh)

    x2_0, h2_0 = _rows(res_glu_pre, [xp, vg0], [m0[2] + ftok1[0, 0], vec(norm_mlp[0]), m0[4], m0[3]],
                       [(D, F32), (D, BF16)], [], name="res_mix0")
    forward_finish(1, fwd1, h2_0)
    a_0, f0 = _mlp_fwd(h2_0, W["w_ff1_0"], W["w_ff2_0"], "0")
    fwd2, ftok2 = forward_start(2, f0)

    def res_pre(xt, bt, gt, g, sc, sh):
        xn = xt + gt * bt
        return xn, _norm_mod(xn, g, sc, sh)

    x3p, h1p = _rows(res_pre, [x2_0, f0], [m0[5] + ftok2[0, 0], vec(norm_mix[1]), m1[1], m1[0]],
                     [(D, F32), (D, BF16)], [], name="res_mlp0")
    x3, h1 = _from_segments(x3p), _from_segments(h1p)
    forward_finish(2, fwd2, h1)
    in_fwd, in_bwd = _w_in_layout(QK, D, gla_w_in.shape[2])
    w_in_r = jnp.concatenate([W["gla_w_in"][j, :, lo:hi] for j, lo, hi in in_fwd]
                             + [jnp.zeros((D, 128 - GATE_RANK), BF16)], axis=1)
    proj = _matmul(h1, w_in_r, name="gla_in", tn=640)
    og, states = _gla_fwd(proj, wg2p, bg, gn, name="gla_fwd")
    ymix = _matmul(og, W["gla_w_out"], name="gla_out")
    x2_1, h2_1 = _rows(res_pre, [x3, ymix], [m1[2], vec(norm_mlp[1]), m1[4], m1[3]], [(D, F32), (D, BF16)], [],
                       name="res_mix1")
    a_1, f1 = _mlp_fwd(h2_1, W["w_ff1_1"], W["w_ff2_1"], "1")

    def final(xt, ft, tgt, gt, g):
        xn = xt + gt * ft
        rs = lax.rsqrt(jnp.mean(xn * xn, axis=-1, keepdims=True) + EPS)
        xh = xn * rs
        e = xh * g - tgt
        dout = e * (1.0 / D)
        dxh = dout * g
        dx = rs * (dxh - xh * jnp.mean(dxh * xh, axis=-1, keepdims=True))
        lsum = 0.5 * jnp.sum(jnp.sum(e * e, axis=-1, keepdims=True), axis=0, keepdims=True) * (1.0 / D)
        return dx, dx * gt, jnp.broadcast_to(lsum, (1, 128)), _rsum(dout * xh), _rsum(dx * ft)

    dx, df1, loss_part, d_norm_final, dgt2_1 = _rows(
        final, [x2_1, f1, loss_target[0]], [m1[5], vec(norm_final)], [(D, F32), (D, BF16)],
        [(1, 128), (1, D), (1, D)], name="loss_head")
    loss = lax.psum(loss_part[0, 0], ("x", "y", "c"))

    def gate_bwd(dxt, bt, gt):
        return dxt * gt, _rsum(dxt * bt)

    def norm_bwd(xt, dht, drt, g, sc):
        dxn, dsh, dsc, dg = _norm_mod_bwd(xt, dht, g, sc)
        return drt + dxn, dsh, dsc, dg

    def norm_gate_bwd(xt, dht, drt, bt, g, sc, gt):
        dxn, dsh, dsc, dg = _norm_mod_bwd(xt, dht, g, sc)
        dxt = drt + dxn
        return dxt, dxt * gt, dsh, dsc, dg, _rsum(dxt * bt)

    vD = [(1, D)]
    dh2_1, dw_ff1_1, dw_ff2_1 = _mlp_bwd(df1, h2_1, a_1, W["w_ff1_1"], W["w_ff2_1"], "1")
    dx, dmix1, dsh2_1, dsc2_1, dg_mlp1, dgt1_1 = _rows(
        norm_gate_bwd, [x2_1, dh2_1, dx, ymix], [vec(norm_mlp[1]), m1[4], m1[2]], [(D, F32), (D, BF16)], vD * 4,
        name="norm_mlp1_bwd")
    dog = _matmul(dmix1, W["gla_w_out"], tb=True, name="gla_out_dx")
    dw_out = _matmul(og, dmix1, ta=True, name="gla_out_dw")
    dproj, dwg2p, dbg, dgn = _gla_bwd(proj, dog, states, wg2p, bg, gn, name="gla_bwd")
    dh1 = _matmul(dproj, w_in_r, tb=True, name="gla_in_dx", tk=3200)
    dw_in_r = _matmul(h1, dproj, ta=True, name="gla_in_dw", tn=640)
    dx, dsh1_1, dsc1_1, dg_mix1 = _rows(norm_bwd, [x3, dh1, dx], [vec(norm_mix[1]), m1[1]], [(D, F32)], vD * 3,
                                        name="norm_mix1_bwd")
    dxp = _to_segments(dx)
    tags = [b[0] for b in big] + ["small"]
    rs_groups = []

    def rs_chips_begin(idx, srcs, r1, gname):
        s1 = [_sum_own_half(g, r, ci, F32 if tags[k] == "small" else BF16, name=f"rs_sum_cores_{tags[k]}")
              for g, r, k in zip(srcs, r1, idx)]
        pair, parts, lands, token = _rs_chips_start(s1, name=f"rs_chips_start_{gname}")
        rs_groups.append((idx, pair, parts, lands))
        return token

    def rs_begin(idx, srcs, gname):
        return rs_chips_begin(idx, srcs, _rs_cores(srcs, name=f"rs_cores_{gname}"), gname)

    dw_in = jnp.stack([jnp.concatenate([dw_in_r[:, lo:hi] for lo, hi in in_bwd[j]], axis=1) for j in range(NCH)])
    idx1 = [1, 2, 4, 6]
    pair1, src1, land1, tok1 = _rs_cores_start(
        [dw_in, dw_out.reshape(NCH, -1, D), dw_ff1_1, dw_ff2_1.reshape(NCH, -1, D)], name="rs_cores_start_l1")

    df0, dgt2_0 = _rows(gate_bwd, [dxp, f0], [m0[5] + tok1[0, 0]], [(D, BF16)], vD, name="gate_mlp0")
    dh2_0, dw_ff1_0, dw_ff2_0 = _mlp_bwd(df0, h2_0, a_0, W["w_ff1_0"], W["w_ff2_0"], "0")
    src1, land1 = _rs_cores_wait(pair1, src1, land1, dh2_0, name="rs_cores_wait_l1")
    tok1b = rs_chips_begin(idx1, src1, land1, "l1")
    idx0 = [3, 5]
    pair0, src0, land0, tok0 = _rs_cores_start([dw_ff1_0, dw_ff2_0.reshape(NCH, -1, D)], name="rs_cores_start_l0")
    tok2 = tok1b + tok0

    def norm_glu_bwd(xt, dht, drt, vgt, g, sc, gt):
        dxn, dsh, dsc, dg = _norm_mod_bwd(xt, dht, g, sc)
        dxt = drt + dxn
        val, sg = vgt[:, :D], _sigmoid(vgt[:, D:])
        dbr = dxt * gt
        dvg = jnp.concatenate([dbr * sg, dbr * val * sg * (1.0 - sg)], axis=1)
        return dxt, dvg, dsh, dsc, dg, _rsum(dxt * val * sg)

    dxp, dvg0, dsh2_0, dsc2_0, dg_mlp0, dgt1_0 = _rows(
        norm_glu_bwd, [x2_0, dh2_0, dxp, vg0], [vec(norm_mlp[0]), m0[4] + tok2[0, 0], m0[2]], [(D, F32), (2 * D, BF16)],
        vD * 4, name="norm_mlp0_bwd")
    dz0 = _matmul(dvg0, W["s5_w_glu"], tb=True, name="glu_dx")
    dw_glu = _matmul(z0, dvg0, ta=True, name="glu_dw", tn=512, col_shards=NCH)
    src0, land0 = _rs_cores_wait(pair0, src0, land0, dw_glu, name="rs_cores_wait_l0")
    tok0b = rs_chips_begin(idx0 + [0], src0 + [dw_glu], land0 + list(_rs_cores([dw_glu], name="rs_cores_glu")), "l0")
    du0, db_acc, dc_acc, dl_acc, dd_s5 = _s5_bwd(u0, y0, dz0, ck0, tb, s5_dv + tok0b[0, 0], name="s5_bwd")
    dxp, dsh1_0, dsc1_0, dg_mix0 = _rows(norm_bwd, [xp, du0, dxp], [vec(norm_mix[0]), m0[1]], [(D, F32)], vD * 3,
                                         name="norm_mix0_bwd")
    grad_x = _from_segments(dxp)[None]

    dmod = jnp.concatenate([dsh1_0, dsc1_0, dgt1_0, dsh2_0, dsc2_0, dgt2_0,
                            dsh1_1, dsc1_1, dgt1_1, dsh2_1, dsc2_1, dgt2_1], axis=1)
    dbb_re, dbb_im = _s5_untable(db_acc)
    dc_re, dc_im_neg = _s5_untable(dc_acc)
    nbk = D // 128
    dl = dl_acc.reshape(nbk, NSEG, 2, GPB * S5_P).sum(axis=1)
    smalls = [dmod, dg_mix0, dg_mix1, dg_mlp0, dg_mlp1, d_norm_final, dd_s5, dbg, dgn,
              dwg2p[:GATE_RANK].reshape(1, -1), dbb_re.reshape(1, -1), dbb_im.reshape(1, -1),
              dc_re.reshape(1, -1), dc_im_neg.reshape(1, -1), dl.reshape(1, -1)]
    ssz = [t.shape[1] for t in smalls]
    stot = sum(ssz)
    spad = -(-stot // 8192) * 8192
    svec = jnp.concatenate(smalls + [jnp.zeros((1, spad - stot), F32)], axis=1).reshape(NCH, spad // (128 * NCH), 128)


    rs_begin([7], [svec], "last")
    landed = _rs_chips_wait([(g[1], g[2], g[3]) for g in rs_groups], grad_x, name="rs_chips_wait")
    s1, r2 = {}, {}
    for (idx, _, _, _), (parts, lands) in zip(rs_groups, landed):
        for k, part, land in zip(idx, parts, lands):
            s1[k], r2[k] = part, land

    def fin(k, **kw):
        return _sum_chips(r2[k], s1[k], chip, ci, name=f"rs_sum_chips_{tags[k]}", **kw)

    f_ff1 = fin(4, nlead=2, lead=1, prev=fin(3, nlead=2, lead=0))
    f_ff2 = fin(6, nlead=2, lead=1, prev=fin(5, nlead=2, lead=0))
    finals = [fin(0), fin(1), fin(2), f_ff1, f_ff2, fin(7, spread=True)]
    halves = [t.shape[1] for t in (s1[0], s1[1], s1[2], s1[3], s1[5], s1[7])]
    gpair, gthru, gtok = _rs_gather_start(finals, halves, [False] * 5 + [True], name="rs_gather_cores_start")

    dmod_all = _exchange((dmod + gtok[0, 0]).reshape(12 * D // 128, 128), MASK_ALL, name="gather_dmod").reshape(8, 2, 6 * D)
    dmod_cols = lax.dynamic_slice_in_dim(dmod_all, chip * acols, acols, axis=2).transpose(1, 0, 2)
    g_w_ada = _ada_grad(c_all, dmod_cols, name="ada_grad")
    upd_w_ada = _adamw(w_ada, g_w_ada, m_w_ada, v_w_ada, name="adamw_w_ada")

    g_glu, g_in, g_out, g_w_ff1, g_w_ff2, s_own = _rs_gather_wait(
        gpair, gthru, halves, [False] * 5 + [True], upd_w_ada[0], name="rs_gather_cores_wait")
    srows = spad // (128 * NCH)
    (s_sum,) = _gather_weights([s_own.reshape(NCH * srows, 128)], [(srows, 128)], ["row"], name="gather_small_grads")
    s_sum = s_sum.reshape(-1)
    so = [sum(ssz[:k]) for k in range(len(ssz))]
    sm = [s_sum[o:o + n] for o, n in zip(so, ssz)]
    (dmod_s, g_mix0, g_mix1, g_mlp0, g_mlp1, g_nf, g_d, g_bg, g_gn, g_wg2, g_bbre, g_bbim, g_cre, g_cimn, g_dl) = sm
    g_b_ada = dmod_s.reshape(2, 6 * D)

    G = D // S5_H
    _, disc_vjp = jax.vjp(_s5_disc, s5_a_re[0], s5_a_im[0], s5_log_dt[0], s5_b_re[0], s5_b_im[0])
    g_dl = g_dl.reshape(nbk, 2, GPB, S5_P)
    ct = (g_dl[:, 0].reshape(G, S5_P), g_dl[:, 1].reshape(G, S5_P),
          g_bbre.reshape(G, S5_H, S5_P).transpose(0, 2, 1), g_bbim.reshape(G, S5_H, S5_P).transpose(0, 2, 1))
    g_a_re, g_a_im, g_log_dt, g_b_re, g_b_im = disc_vjp(ct)
    g_c_re = g_cre.reshape(G, S5_H, S5_P)
    g_c_im = -g_cimn.reshape(G, S5_H, S5_P)
    g_wg2_s = lax.dynamic_slice_in_dim(g_wg2.reshape(GATE_RANK, QK), chip * qk4, qk4, axis=1)
    g_bg_s = lax.dynamic_slice_in_dim(g_bg.reshape(1, QK), chip * qk4, qk4, axis=1)
    g_gn_s = lax.dynamic_slice_in_dim(g_gn.reshape(1, D), chip * (D // NCH), D // NCH, axis=1)

    grads = dict(
        w_ada=g_w_ada, b_ada=g_b_ada, norm_mix=jnp.stack([g_mix0, g_mix1]), norm_mlp=jnp.stack([g_mlp0, g_mlp1]),
        s5_a_re=g_a_re[None], s5_a_im=g_a_im[None], s5_log_dt=g_log_dt[None], s5_b_re=g_b_re[None], s5_b_im=g_b_im[None],
        s5_c_re=g_c_re[None], s5_c_im=g_c_im[None], s5_d=g_d[None], s5_w_glu=g_glu,
        gla_w_in=g_in, gla_w_gate2=g_wg2_s[None], gla_b_gate=g_bg_s, gla_g_norm=g_gn_s,
        gla_w_out=g_out, w_ff1=g_w_ff1, w_ff2=g_w_ff2, norm_final=g_nf)

    names = list(grads)
    large = ("w_ada", "s5_w_glu", "gla_w_in", "gla_w_out", "w_ff1", "w_ff2")
    delta, new_m, new_v = {}, {}, {}
    delta["w_ada"], new_m["w_ada"], new_v["w_ada"] = upd_w_ada
    for nm in large[1:]:
        delta[nm], new_m[nm], new_v[nm] = _adamw(args[nm], grads[nm], args["m_" + nm], args["v_" + nm], name=f"adamw_{nm}")
    grads = {nm: grads[nm].reshape(args[nm].shape) for nm in names}
    for nm in names:
        if nm not in large:
            shp = args[nm].shape
            as2d = (1, -1) if len(shp) == 1 else shp
            outs = _adamw_whole(*[t.reshape(as2d) for t in (args[nm], grads[nm], args["m_" + nm], args["v_" + nm])],
                                name=f"adamw_{nm}")
            delta[nm], new_m[nm], new_v[nm] = (t.reshape(shp) for t in outs)
    return (loss, grad_x, *[grads[n] for n in names], *[delta[n] for n in names], *[new_m[n] for n in names],
            *[new_v[n] for n in names])
```

```python
import math

import jax
import jax.numpy as jnp
from jax import lax
from jax.experimental import pallas as pl
from jax.experimental.pallas import tpu as pltpu

F32 = jnp.float32
BF16 = jnp.bfloat16
MESH = pl.DeviceIdType.MESH

EPS = 1e-6
CHUNK = 64
GLA_NB = 4
S5_H = 16
S5_P = 64
GPB = 8
NSEG = 8
HEADS = 4
GATE_RANK = 16
GATE_TAU = 16.0
NCH = 4
LR, B1, B2, AEPS, WD, ASTEP = 0.001, 0.9, 0.999, 1e-08, 0.01, 10
VMEM_LIMIT = 56 << 20
ROW_SUB = 64

MASK_CHIPS = ((1, 0, 0), (0, 1, 0), (1, 1, 0))
MASK_ALL = ((0, 0, 1), (0, 1, 0), (0, 1, 1), (1, 0, 0), (1, 0, 1), (1, 1, 0), (1, 1, 1))


def _params(*sem):
    return pltpu.CompilerParams(dimension_semantics=sem or None, vmem_limit_bytes=VMEM_LIMIT)


def _tile_rows(rows, cap=512):
    best = 8
    for t in range(8, cap + 1, 8):
        if rows % t == 0:
            best = t
    return best


def _whole(shape):
    return pl.BlockSpec(shape, lambda i, _n=len(shape): (0,) * _n)


def _matmul(a, b, *, name, ta=False, tb=False, tm=1024, tn=1024, tk=4096, out_dtypes=(F32,),
            a_fn=None, epi=None, epi_ins=(), col_shards=1):
    M, K = (a.shape[1], a.shape[0]) if ta else a.shape
    N = b.shape[0] if tb else b.shape[1]
    tm, tn, tk = min(tm, M), min(tn, N), min(tk, K)
    assert M % tm == 0 and N % tn == 0 and K % tk == 0, (name, M, N, K)
    nk = K // tk
    ne = len(epi_ins)
    dn = (((0 if ta else 1,), (1 if tb else 0,)), ((), ()))

    def body(a_ref, b_ref, *rest):
        e_refs, o_refs = rest[:ne], rest[ne:ne + len(out_dtypes)]
        at = a_ref[...]
        if a_fn is not None:
            at = a_fn(at)
        part = lax.dot_general(at.astype(BF16), b_ref[...].astype(BF16), dn, preferred_element_type=F32)

        def finish(total):
            outs = (total,) if epi is None else epi(total, *[r[...] for r in e_refs])
            for r, o in zip(o_refs, outs):
                r[...] = o.astype(r.dtype)

        if nk == 1:
            finish(part)
            return
        acc = rest[-1]
        k = pl.program_id(2)

        @pl.when(k == 0)
        def _():
            acc[...] = part

        @pl.when(k > 0)
        def _():
            acc[...] += part

        @pl.when(k == nk - 1)
        def _():
            finish(acc[...])

    a_spec = pl.BlockSpec((tk, tm), lambda i, j, k: (k, i)) if ta else pl.BlockSpec((tm, tk), lambda i, j, k: (i, k))
    b_spec = pl.BlockSpec((tn, tk), lambda i, j, k: (j, k)) if tb else pl.BlockSpec((tk, tn), lambda i, j, k: (k, j))
    o_spec = pl.BlockSpec((tm, tn), lambda i, j, k: (i, j))
    if col_shards > 1:
        per = N // col_shards // tn
        assert ne == 0 and per * tn * col_shards == N
        w_spec = pl.BlockSpec((None, tm, tn), lambda i, j, k: (j // per, i, j % per))
        o_shape = (col_shards, M, N // col_shards)
    else:
        w_spec, o_shape = o_spec, (M, N)
    outs = pl.pallas_call(
        body, name=name, grid=(M // tm, N // tn, nk),
        in_specs=[a_spec, b_spec] + [o_spec] * ne,
        out_specs=[w_spec] * len(out_dtypes),
        out_shape=[jax.ShapeDtypeStruct(o_shape, d) for d in out_dtypes],
        scratch_shapes=[pltpu.VMEM((tm, tn), F32)] if nk > 1 else [],
        compiler_params=_params("parallel", "parallel", "arbitrary"),
    )(a, b, *epi_ins)
    return outs[0] if len(outs) == 1 else outs


def _rows(fn, rows_in, vecs_in, rows_out, acc_out, *, name, tm=512):
    L = rows_in[0].shape[0]
    tm = min(tm, L)
    assert L % tm == 0
    nr, nv, no, na = len(rows_in), len(vecs_in), len(rows_out), len(acc_out)

    sub = ROW_SUB if tm % ROW_SUB == 0 else tm

    def body(*refs):
        rin, vin = refs[:nr], refs[nr:nr + nv]
        rout, aout = refs[nr + nv:nr + nv + no], refs[nr + nv + no:]
        if na:
            @pl.when(pl.program_id(0) == 0)
            def _():
                for r in aout:
                    r[...] = jnp.zeros_like(r)

        vecs = [v[...] for v in vin]
        sums = None
        for s in range(tm // sub):
            rows = pl.ds(s * sub, sub)
            outs = fn(*[r[rows, :] for r in rin], *vecs)
            for r, o in zip(rout, outs[:no]):
                r[rows, :] = o.astype(r.dtype)
            sums = list(outs[no:]) if sums is None else [t + o for t, o in zip(sums, outs[no:])]
        for r, t in zip(aout, sums):
            r[...] += t

    outs = pl.pallas_call(
        body, name=name, grid=(L // tm,),
        in_specs=[pl.BlockSpec((tm, r.shape[1]), lambda i: (i, 0)) for r in rows_in] + [_whole(v.shape) for v in vecs_in],
        out_specs=[pl.BlockSpec((tm, c), lambda i: (i, 0)) for c, _ in rows_out] + [_whole(s) for s in acc_out],
        out_shape=[jax.ShapeDtypeStruct((L, c), d) for c, d in rows_out] + [jax.ShapeDtypeStruct(s, F32) for s in acc_out],
        compiler_params=_params("arbitrary"),
    )(*rows_in, *vecs_in)
    return outs


def _rsum(t):
    return jnp.sum(t, axis=0, keepdims=True)


def _norm_mod(x, g, sc, sh):
    rs = lax.rsqrt(jnp.mean(x * x, axis=-1, keepdims=True) + EPS)
    return x * rs * g * (1.0 + sc) + sh


def _norm_mod_bwd(x, dh, g, sc):
    rs = lax.rsqrt(jnp.mean(x * x, axis=-1, keepdims=True) + EPS)
    xh = x * rs
    dn = dh * (1.0 + sc)
    dxh = dn * g
    dx = rs * (dxh - xh * jnp.mean(dxh * xh, axis=-1, keepdims=True))
    return dx, _rsum(dh), _rsum(dh * xh * g), _rsum(dn * xh)


def _sigmoid(x):
    return jax.nn.sigmoid(x)


def _gelu(y):
    return jax.nn.gelu(y, approximate=True)


def _gelu_grad(y):
    c = math.sqrt(2.0 / math.pi)
    t = jnp.tanh(c * (y + 0.044715 * y * y * y))
    return 0.5 * (1.0 + t) + 0.5 * y * (1.0 - t * t) * c * (1.0 + 3.0 * 0.044715 * y * y)


def _s5_tables(lb_re, lb_im, bb_re, bb_im, c_re, c_im, seg_len):
    G = lb_re.shape[0]
    nb = G // GPB
    eye = jnp.eye(GPB, dtype=F32)

    def bdiag(t):
        a, b = t.shape[1:]
        t = t.reshape(nb, GPB, a, b)
        return (t[:, :, :, None, :] * eye[None, :, None, :, None]).reshape(nb, GPB * a, GPB * b)

    bbd = jnp.concatenate([bdiag(bb_re.transpose(0, 2, 1)), bdiag(bb_im.transpose(0, 2, 1))], axis=2)
    cbd = jnp.concatenate([bdiag(c_re.transpose(0, 2, 1)), -bdiag(c_im.transpose(0, 2, 1))], axis=1)

    def lanes(re, im):
        t = jnp.concatenate([re.reshape(nb, GPB * S5_P), im.reshape(nb, GPB * S5_P)], axis=1)
        return jnp.repeat(t, NSEG, axis=0)

    tr, ti = lb_re, lb_im
    for _ in range(int(math.log2(seg_len))):
        tr, ti = tr * tr - ti * ti, 2.0 * tr * ti
    return dict(bbd=bbd.astype(BF16), bbdT=bbd.transpose(0, 2, 1).astype(BF16), cbd=cbd.astype(BF16),
                cbdT=cbd.transpose(0, 2, 1).astype(BF16), lam=lanes(lb_re, lb_im), lamT=lanes(tr, ti))


def _s5_untable(acc):
    nb = acc.shape[0]
    t = acc.reshape(nb, GPB, S5_H, 2, GPB, S5_P)
    d = jnp.diagonal(t, axis1=1, axis2=4)
    d = d.transpose(0, 4, 2, 1, 3).reshape(nb * GPB, 2, S5_H, S5_P)
    return d[:, 0], d[:, 1]


S5_R = 256


def _s5_carries(ends, first, t_ref, rws, SW, cfx, *, reverse):
    er, ei = ends
    tr, ti = t_ref[rws, :SW][0:1], t_ref[rws, SW:][0:1]
    cr, ci = first
    order = range(NSEG - 1, -1, -1) if reverse else range(NSEG)
    for n, s in enumerate(order):
        if n > 0:
            p = s + 1 if reverse else s - 1
            if reverse:
                cr, ci = tr * cr + ti * ci + er[p:p + 1], tr * ci - ti * cr + ei[p:p + 1]
            else:
                cr, ci = tr * cr - ti * ci + er[p:p + 1], tr * ci + ti * cr + ei[p:p + 1]
        cfx[s:s + 1, :SW] = cr
        cfx[s:s + 1, SW:] = ci


def _s5_fwd(up, tb, dvec, *, name):
    L, D = up.shape
    R = min(S5_R, L)
    nb, ta, ngb = L // R, R // NSEG, D // 128
    SW = GPB * S5_P
    crows = ngb * NSEG

    def body(u_ref, lam_ref, t_ref, b_ref, c_ref, d_ref, y_ref, z_ref, ck_ref, carry, xbuf2, cfx2):
        @pl.when(pl.program_id(0) == 0)
        def _():
            carry[...] = jnp.zeros_like(carry)

        zero = jnp.zeros((NSEG, SW), F32)
        for g0 in range(0, ngb, 2):
            pair = (g0, g0 + 1)
            xb = [xbuf2.at[g % 4] for g in pair]
            cf = [cfx2.at[g % 4] for g in pair]
            cols = [slice(g * 128, (g + 1) * 128) for g in pair]
            rws = [slice(g * NSEG, (g + 1) * NSEG) for g in pair]
            ug = [u_ref[:, cols[q]] for q in range(2)]
            for q in range(2):
                xb[q][...] = jnp.dot(ug[q].astype(BF16), b_ref[pair[q]], preferred_element_type=F32)
            lam = [(lam_ref[rws[q], :SW], lam_ref[rws[q], SW:]) for q in range(2)]

            def scan(c, store, xb=xb, lam=lam):
                c = list(c)
                for a in range(ta):
                    o = slice(a * NSEG, (a + 1) * NSEG)
                    for q in range(2):
                        (lr, li), (cr, ci) = lam[q], c[q]
                        nr = lr * cr - li * ci + xb[q][o, :SW]
                        ni = lr * ci + li * cr + xb[q][o, SW:]
                        if store:
                            xb[q][o, :SW] = nr
                            xb[q][o, SW:] = ni
                        c[q] = (nr, ni)
                return c

            ends = scan([(zero, zero)] * 2, False)
            for q in range(2):
                prev = (carry[rws[q], :SW][NSEG - 1:NSEG], carry[rws[q], SW:][NSEG - 1:NSEG])
                _s5_carries(ends[q], prev, t_ref, rws[q], SW, cf[q], reverse=False)
                ck_ref[0, rws[q], :] = cf[q][...]
            fin = scan([(cf[q][:, :SW], cf[q][:, SW:]) for q in range(2)], True)
            for q in range(2):
                carry[rws[q], :SW] = fin[q][0]
                carry[rws[q], SW:] = fin[q][1]
            for q in range(2):
                y = (jnp.dot(xb[q][...].astype(BF16), c_ref[pair[q]], preferred_element_type=F32)
                     + d_ref[:, cols[q]] * ug[q])
                y_ref[:, cols[q]] = y
                z_ref[:, cols[q]] = _gelu(y).astype(BF16)

    rowblk = pl.BlockSpec((R, D), lambda i: (i, 0))
    return pl.pallas_call(
        body, name=name, grid=(nb,),
        in_specs=[rowblk, _whole(tb["lam"].shape), _whole(tb["lamT"].shape), _whole(tb["bbd"].shape),
                  _whole(tb["cbd"].shape), _whole(dvec.shape)],
        out_specs=[rowblk, rowblk, pl.BlockSpec((1, crows, 2 * SW), lambda i: (i, 0, 0))],
        out_shape=[jax.ShapeDtypeStruct((L, D), F32), jax.ShapeDtypeStruct((L, D), BF16),
                   jax.ShapeDtypeStruct((nb, crows, 2 * SW), F32)],
        scratch_shapes=[pltpu.VMEM((crows, 2 * SW), F32), pltpu.VMEM((4, R, 2 * SW), F32),
                        pltpu.VMEM((4, NSEG, 2 * SW), F32)],
        compiler_params=_params("arbitrary"),
    )(up, tb["lam"], tb["lamT"], tb["bbd"], tb["cbd"], dvec)


def _s5_bwd(up, y, dz, ck, tb, dvec, *, name):
    L, D = up.shape
    R = min(S5_R, L)
    nb, ta, ngb = L // R, R // NSEG, D // 128
    SW = GPB * S5_P
    crows = ngb * NSEG

    def body(u_ref, y_ref, dz_ref, ck_ref, lam_ref, t_ref, b_ref, bt_ref, ct_ref, d_ref,
             du_ref, db_ref, dc_ref, dl_ref, dd_ref, gcarry, xbuf2, gbuf2, dybuf, cfx2):
        @pl.when(pl.program_id(0) == 0)
        def _():
            gcarry[...] = jnp.zeros_like(gcarry)
            db_ref[...] = jnp.zeros_like(db_ref)
            dc_ref[...] = jnp.zeros_like(dc_ref)
            dl_ref[...] = jnp.zeros_like(dl_ref)
            dd_ref[...] = jnp.zeros_like(dd_ref)

        zero = jnp.zeros((NSEG, SW), F32)
        dybuf[...] = dz_ref[...] * _gelu_grad(y_ref[...])
        for g0 in range(0, ngb, 2):
            pair = (g0, g0 + 1)
            xb = [xbuf2.at[g % 4] for g in pair]
            gbf = [gbuf2.at[g % 4] for g in pair]
            cf = [cfx2.at[g % 4] for g in pair]
            cols = [slice(g * 128, (g + 1) * 128) for g in pair]
            rws = [slice(g * NSEG, (g + 1) * NSEG) for g in pair]
            dyg = [dybuf[:, cols[q]] for q in range(2)]
            ug = [u_ref[:, cols[q]] for q in range(2)]
            lam = [(lam_ref[rws[q], :SW], lam_ref[rws[q], SW:]) for q in range(2)]
            for q in range(2):
                gbf[q][...] = jnp.dot(dyg[q].astype(BF16), ct_ref[pair[q]], preferred_element_type=F32)
                xb[q][0:NSEG, :] = ck_ref[0, rws[q], :]
                xb[q][NSEG:, :] = jnp.dot(ug[q].astype(BF16), b_ref[pair[q]], preferred_element_type=F32)

            c = [(xb[q][0:NSEG, :SW], xb[q][0:NSEG, SW:]) for q in range(2)]
            for a in range(ta):
                o = slice((a + 1) * NSEG, (a + 2) * NSEG)
                for q in range(2):
                    (lr, li), (cr, ci) = lam[q], c[q]
                    nr = lr * cr - li * ci + xb[q][o, :SW]
                    ni = lr * ci + li * cr + xb[q][o, SW:]
                    xb[q][o, :SW] = nr
                    xb[q][o, SW:] = ni
                    c[q] = (nr, ni)

            def rscan(c, store, qs=(0, 1), xb=xb, gbf=gbf, lam=lam):
                c = list(c)
                for a in range(ta - 1, -1, -1):
                    o = slice(a * NSEG, (a + 1) * NSEG)
                    for q in qs:
                        lr, li = lam[q]
                        gr = gbf[q][o, :SW] + lr * c[q][0] + li * c[q][1]
                        gi = gbf[q][o, SW:] - li * c[q][0] + lr * c[q][1]
                        if store:
                            gbf[q][o, :SW] = gr
                            gbf[q][o, SW:] = gi
                            xr, xi = xb[q][o, :SW], xb[q][o, SW:]
                            c[q] = (gr, gi, c[q][2] + gr * xr + gi * xi, c[q][3] + gi * xr - gr * xi)
                        else:
                            c[q] = (gr, gi)
                return c

            gends = rscan([(zero, zero)] * 2, False)
            for q in range(2):
                nxt = (gcarry[rws[q], :SW][0:1], gcarry[rws[q], SW:][0:1])
                _s5_carries(gends[q], nxt, t_ref, rws[q], SW, cf[q], reverse=True)
            fin = [(cf[q][:, :SW], cf[q][:, SW:], zero, zero) for q in range(2)]
            for q in range(2):
                fin = rscan(fin, True, qs=(q,))
            for q in range(2):
                gcarry[rws[q], :SW] = fin[q][0]
                gcarry[rws[q], SW:] = fin[q][1]
                dl_ref[rws[q], :SW] += fin[q][2]
                dl_ref[rws[q], SW:] += fin[q][3]
            gb16 = [gbf[q][...].astype(BF16) for q in range(2)]
            for q in range(2):
                du_ref[:, cols[q]] = (jnp.dot(gb16[q], bt_ref[pair[q]], preferred_element_type=F32)
                                      + d_ref[:, cols[q]] * dyg[q])
            for q in range(2):
                db_ref[pair[q]] += lax.dot_general(ug[q].astype(BF16), gb16[q], TN, preferred_element_type=F32)
            for q in range(2):
                dc_ref[pair[q]] += lax.dot_general(dyg[q].astype(BF16), xb[q][NSEG:, :].astype(BF16), TN,
                                                   preferred_element_type=F32)
                dd_ref[:, cols[q]] += _rsum(dyg[q] * ug[q])

    rev = pl.BlockSpec((R, D), lambda i: (nb - 1 - i, 0))
    acc3 = (ngb, 128, 2 * SW)
    return pl.pallas_call(
        body, name=name, grid=(nb,),
        in_specs=[rev, rev, rev, pl.BlockSpec((1, crows, 2 * SW), lambda i: (nb - 1 - i, 0, 0)),
                  _whole(tb["lam"].shape), _whole(tb["lamT"].shape), _whole(tb["bbd"].shape),
                  _whole(tb["bbdT"].shape), _whole(tb["cbdT"].shape), _whole(dvec.shape)],
        out_specs=[rev, _whole(acc3), _whole(acc3), _whole((crows, 2 * SW)), _whole((1, D))],
        out_shape=[jax.ShapeDtypeStruct((L, D), F32), jax.ShapeDtypeStruct(acc3, F32), jax.ShapeDtypeStruct(acc3, F32),
                   jax.ShapeDtypeStruct((crows, 2 * SW), F32), jax.ShapeDtypeStruct((1, D), F32)],
        scratch_shapes=[pltpu.VMEM((crows, 2 * SW), F32), pltpu.VMEM((4, R + NSEG, 2 * SW), F32),
                        pltpu.VMEM((4, R, 2 * SW), F32), pltpu.VMEM((R, D), F32), pltpu.VMEM((4, NSEG, 2 * SW), F32)],
        compiler_params=_params("arbitrary"),
    )(up, y, dz, ck, tb["lam"], tb["lamT"], tb["bbd"], tb["bbdT"], tb["cbdT"], dvec)


NN = (((1,), (0,)), ((), ()))
TN = (((0,), (0,)), ((), ()))
NT = (((1,), (1,)), ((), ()))


def _dot3(lhs, rhs, dn, split):
    x = rhs if split == "rhs" else lhs
    hi = x.astype(BF16)
    r1 = x - hi.astype(F32)
    mid = r1.astype(BF16)
    lo = (r1 - mid.astype(F32)).astype(BF16)
    out = None
    for part in (hi, mid, lo):
        ops = (lhs, part) if split == "rhs" else (part, rhs)
        t = lax.dot_general(ops[0], ops[1], dn, preferred_element_type=F32)
        out = t if out is None else out + t
    return out


def _log_sigmoid(x):
    return jnp.minimum(x, 0.0) - jnp.log(1.0 + jnp.exp(-jnp.abs(x)))


def _chunk_tri(rows, upper):
    r = lax.broadcasted_iota(jnp.int32, (rows, rows), 0)
    c = lax.broadcasted_iota(jnp.int32, (rows, rows), 1)
    same = (r // CHUNK) == (c // CHUNK)
    return (same & ((c >= r) if upper else (r >= c))).astype(BF16)


def _gla_block_gates(p_ref, wg_ref, bg_ref, QK, wbuf, gebuf):
    RB = p_ref.shape[0]
    glr = p_ref[:, 6 * QK:6 * QK + 128].astype(BF16)
    gpre = jnp.dot(glr, wg_ref[...], preferred_element_type=F32) + bg_ref[...]
    la = _log_sigmoid(gpre) * (1.0 / GATE_TAU)
    gc = _dot3(_chunk_tri(RB, False), la, NN, "rhs")
    for cc in range(RB // CHUNK):
        rows = slice(cc * CHUNK, (cc + 1) * CHUNK)
        ge = gc[(cc + 1) * CHUNK - 1:(cc + 1) * CHUNK, :]
        gebuf[cc:cc + 1, :] = ge
        wbuf[rows, :] = jnp.exp(ge - gc[rows, :])
    return glr, gpre, la


def _as_column(row, lanes):
    t = jnp.transpose(jnp.broadcast_to(row, (row.shape[1], row.shape[1])))
    return jnp.concatenate([t] * (lanes // row.shape[1]), axis=1)


def _as_row(col):
    return jnp.transpose(jnp.broadcast_to(col, (col.shape[0], col.shape[0])))[0:1, :]


def _gla_fwd(proj, wg2p, bg, gn, *, name):
    L = proj.shape[0]
    QK = wg2p.shape[1]
    DK, DV = QK // HEADS, 2 * QK // HEADS
    nC = L // CHUNK
    NB = min(GLA_NB, nC)
    assert nC % NB == 0
    scale = DK ** -0.5

    def body(p_ref, wg_ref, bg_ref, gn_ref, og_ref, s_ref, sst, wbuf, gebuf):
        @pl.when(pl.program_id(0) == 0)
        def _():
            sst[...] = jnp.zeros_like(sst)

        _gla_block_gates(p_ref, wg_ref, bg_ref, QK, wbuf, gebuf)
        heads = range(HEADS)
        ks = [slice(h * DK, (h + 1) * DK) for h in heads]
        vs = [slice(h * DV, (h + 1) * DV) for h in heads]
        units = [(cc, h) for cc in range(NB) for h in heads]
        rows = [slice(cc * CHUNK, (cc + 1) * CHUNK) for cc in range(NB)]
        kv = {(cc, h): lax.dot_general(
            (p_ref[rows[cc], QK + h * DK:QK + (h + 1) * DK] * wbuf[rows[cc], ks[h]]).astype(BF16),
            p_ref[rows[cc], 2 * QK + h * DV:2 * QK + (h + 1) * DV].astype(BF16), TN, preferred_element_type=F32)
            for cc, h in units}
        S16 = {}
        for cc, h in units:
            S = jnp.exp(_as_column(gebuf[cc:cc + 1, ks[h]], DV)) * sst[ks[h], :] + kv[cc, h]
            sst[ks[h], :] = S
            s_ref[cc, ks[h], :] = S
            S16[cc, h] = S.astype(BF16)
        o = {(cc, h): jnp.dot((p_ref[rows[cc], h * DK:(h + 1) * DK] * scale).astype(BF16), S16[cc, h],
                              preferred_element_type=F32) for cc, h in units}
        for cc, h in units:
            r = p_ref[rows[cc], 4 * QK + h * DV:4 * QK + (h + 1) * DV]
            on = o[cc, h] * lax.rsqrt(jnp.mean(o[cc, h] * o[cc, h], axis=-1, keepdims=True) + EPS)
            og_ref[rows[cc], vs[h]] = (on * gn_ref[:, vs[h]] * (r * _sigmoid(r))).astype(BF16)

    RB = NB * CHUNK
    return pl.pallas_call(
        body, name=name, grid=(nC // NB,),
        in_specs=[pl.BlockSpec((RB, proj.shape[1]), lambda i: (i, 0)), _whole(wg2p.shape), _whole(bg.shape), _whole(gn.shape)],
        out_specs=[pl.BlockSpec((RB, 2 * QK), lambda i: (i, 0)), pl.BlockSpec((NB, QK, DV), lambda i: (i, 0, 0))],
        out_shape=[jax.ShapeDtypeStruct((L, 2 * QK), BF16), jax.ShapeDtypeStruct((nC, QK, DV), F32)],
        scratch_shapes=[pltpu.VMEM((QK, DV), F32), pltpu.VMEM((RB, QK), F32), pltpu.VMEM((8, QK), F32)],
        compiler_params=_params("arbitrary"),
    )(proj, wg2p, bg, gn)


def _gla_bwd(proj, dog, states, wg2p, bg, gn, *, name):
    L, W = proj.shape
    QK = wg2p.shape[1]
    DK, DV = QK // HEADS, 2 * QK // HEADS
    nC = L // CHUNK
    NB = min(GLA_NB, nC)
    nB = nC // NB
    scale = DK ** -0.5

    def body(p_ref, dog_ref, sc_ref, sp_ref, wg_ref, bg_ref, gn_ref, dp_ref, dwg_ref, dbg_ref, dgn_ref,
             gst, wbuf, gebuf, dwwbuf, dgebuf):
        i = pl.program_id(0)

        @pl.when(i == 0)
        def _():
            gst[...] = jnp.zeros_like(gst)
            dwg_ref[...] = jnp.zeros_like(dwg_ref)
            dbg_ref[...] = jnp.zeros_like(dbg_ref)
            dgn_ref[...] = jnp.zeros_like(dgn_ref)

        RB = NB * CHUNK
        glr, gpre, _ = _gla_block_gates(p_ref, wg_ref, bg_ref, QK, wbuf, gebuf)
        heads = range(HEADS)
        ks = [slice(h * DK, (h + 1) * DK) for h in heads]
        vs = [slice(h * DV, (h + 1) * DV) for h in heads]
        units = [(cc, h) for cc in range(NB) for h in heads]
        rws = [slice(cc * CHUNK, (cc + 1) * CHUNK) for cc in range(NB)]
        qs16 = {(cc, h): (p_ref[rws[cc], h * DK:(h + 1) * DK] * scale).astype(BF16) for cc, h in units}
        S16a = {(cc, h): sc_ref[cc, ks[h], :].astype(BF16) for cc, h in units}
        oa = {u: jnp.dot(qs16[u], S16a[u], preferred_element_type=F32) for u in units}
        doa = {}
        for cc, h in units:
            r = p_ref[rws[cc], 4 * QK + h * DV:4 * QK + (h + 1) * DV]
            o = oa[cc, h]
            rs = lax.rsqrt(jnp.mean(o * o, axis=-1, keepdims=True) + EPS)
            on = o * rs
            sr = _sigmoid(r)
            dg = dog_ref[rws[cc], vs[h]]
            gnh = gn_ref[:, vs[h]]
            dp_ref[rws[cc], 4 * QK + h * DV:4 * QK + (h + 1) * DV] = (
                dg * on * gnh * (sr * (1.0 + r * (1.0 - sr)))).astype(BF16)
            dt = dg * (r * sr)
            dgn_ref[:, vs[h]] += _rsum(dt * on)
            don = dt * gnh
            doa[cc, h] = (rs * (don - on * jnp.mean(don * on, axis=-1, keepdims=True))).astype(BF16)
        dqa = {u: lax.dot_general(doa[u], S16a[u], NT, preferred_element_type=F32) for u in units}
        for cc, h in units:
            dp_ref[rws[cc], h * DK:(h + 1) * DK] = (dqa[cc, h] * scale).astype(BF16)
        for cc in range(NB - 1, -1, -1):
            rows = rws[cc]
            do = [doa[cc, h] for h in heads]
            Gc = [gst[ks[h], :] + lax.dot_general(qs16[cc, h], do[h], TN, preferred_element_type=F32) for h in heads]
            G16 = [g.astype(BF16) for g in Gc]
            kd = [p_ref[rows, QK + h * DK:QK + (h + 1) * DK] * wbuf[rows, ks[h]] for h in heads]
            dkd = [lax.dot_general(p_ref[rows, 2 * QK + h * DV:2 * QK + (h + 1) * DV].astype(BF16), G16[h], NT,
                                   preferred_element_type=F32) for h in heads]
            dv = [jnp.dot(kd[h].astype(BF16), G16[h], preferred_element_type=F32) for h in heads]
            for h in heads:
                if cc > 0:
                    Sp = sc_ref[cc - 1, ks[h], :]
                else:
                    Sp = jnp.where(i < nB - 1, sp_ref[0, ks[h], :], 0.0)
                dp_ref[rows, 2 * QK + h * DV:2 * QK + (h + 1) * DV] = dv[h].astype(BF16)
                ge = gebuf[cc:cc + 1, ks[h]]
                gst[ks[h], :] = jnp.exp(_as_column(ge, DV)) * Gc[h]
                ddec = _as_row(jnp.sum(Gc[h] * Sp, axis=1, keepdims=True))
                dp_ref[rows, QK + h * DK:QK + (h + 1) * DK] = (dkd[h] * wbuf[rows, ks[h]]).astype(BF16)
                dww = dkd[h] * kd[h]
                dwwbuf[rows, ks[h]] = dww
                dgebuf[cc:cc + 1, ks[h]] = jnp.exp(ge) * ddec + _rsum(dww)
        rev = _dot3(_chunk_tri(RB, True), dwwbuf[...], NN, "rhs")
        for cc in range(NB):
            rows = slice(cc * CHUNK, (cc + 1) * CHUNK)
            wbuf[rows, :] = dgebuf[cc:cc + 1, :] - rev[rows, :]
        dgpre = wbuf[...] * (1.0 / GATE_TAU) * (1.0 - _sigmoid(gpre))
        d16 = dgpre.astype(BF16)
        dp_ref[:, 6 * QK:6 * QK + 128] = lax.dot_general(d16, wg_ref[...], NT, preferred_element_type=F32).astype(BF16)
        dwg_ref[...] += lax.dot_general(glr, d16, TN, preferred_element_type=F32)
        dbg_ref[...] += _rsum(dgpre)

    RB = NB * CHUNK
    rev_idx = lambda i: (nB - 1 - i, 0)
    return pl.pallas_call(
        body, name=name, grid=(nB,),
        in_specs=[pl.BlockSpec((RB, W), rev_idx), pl.BlockSpec((RB, 2 * QK), rev_idx),
                  pl.BlockSpec((NB, QK, DV), lambda i: (nB - 1 - i, 0, 0)),
                  pl.BlockSpec((1, QK, DV), lambda i: (jnp.maximum(NB * (nB - 1 - i) - 1, 0), 0, 0)),
                  _whole(wg2p.shape), _whole(bg.shape), _whole(gn.shape)],
        out_specs=[pl.BlockSpec((RB, W), rev_idx), _whole((128, QK)), _whole((1, QK)), _whole((1, 2 * QK))],
        out_shape=[jax.ShapeDtypeStruct((L, W), BF16), jax.ShapeDtypeStruct((128, QK), F32),
                   jax.ShapeDtypeStruct((1, QK), F32), jax.ShapeDtypeStruct((1, 2 * QK), F32)],
        scratch_shapes=[pltpu.VMEM((QK, DV), F32), pltpu.VMEM((RB, QK), F32), pltpu.VMEM((8, QK), F32),
                        pltpu.VMEM((RB, QK), F32), pltpu.VMEM((8, QK), F32)],
        compiler_params=_params("arbitrary"),
    )(proj, dog, states, states, wg2p, bg, gn)


def _coords():
    return lax.axis_index("x"), lax.axis_index("y"), lax.axis_index("c")


def _other_chips(x, y):
    return [(1 - x, y, 2 * (1 - x) + y), (x, 1 - y, 2 * x + 1 - y), (1 - x, 1 - y, 2 * (1 - x) + 1 - y)]


def _hbm_call(body, ins, out_shapes, n_sems, *, name, alias=False):
    any_spec = pl.BlockSpec(memory_space=pl.ANY)
    return pl.pallas_call(
        body, name=name, in_specs=[any_spec] * len(ins), out_specs=[any_spec] * len(out_shapes), out_shape=out_shapes,
        scratch_shapes=[pltpu.SemaphoreType.DMA((n,)) for n in n_sems],
        input_output_aliases={k: k for k in range(len(ins))} if alias else {},
    )(*ins)


def _exchange(src, masks, *, name):
    vary = [any(m[k] for m in masks) for k in range(3)]
    nslots = 2 ** sum(vary)
    n = len(masks)

    def slot(coords):
        s = 0
        for k in range(3):
            if vary[k]:
                s = s * 2 + coords[k]
        return s

    def body(src_ref, dst_ref, send_sems, recv_sems, loc_sem):
        me = _coords()
        mine = slot(me)
        loc = pltpu.make_async_copy(src_ref, dst_ref.at[mine], loc_sem.at[0])
        loc.start()
        copies = []
        for k, m in enumerate(masks):
            peer = tuple(1 - me[d] if m[d] else me[d] for d in range(3))
            cp = pltpu.make_async_remote_copy(src_ref=src_ref, dst_ref=dst_ref.at[mine], send_sem=send_sems.at[k],
                                              recv_sem=recv_sems.at[k], device_id=peer, device_id_type=MESH)
            cp.start()
            copies.append(cp)
        for cp in copies:
            cp.wait()
        loc.wait()

    return _hbm_call(body, [src], [jax.ShapeDtypeStruct((nslots,) + tuple(src.shape), src.dtype)], (n, n, 1), name=name)[0]


def _cast_into(t, lead, kind, chip, *, name, tm=256):
    r, cc = t.shape[-2:]
    tm = min(tm, r)
    nblk = r // tm
    if kind == "col":
        shp, o_spec = (r, NCH * cc), pl.BlockSpec((tm, cc), lambda i, s: (i, s[0]))
    elif kind == "row":
        shp, o_spec = (NCH * r, cc), pl.BlockSpec((tm, cc), lambda i, s: (s[0] * nblk + i, 0))
    else:
        shp, o_spec = (NCH, r, cc), pl.BlockSpec((None, tm, cc), lambda i, s: (s[0], i, 0))

    def body(s_ref, t_ref, o_ref):
        o_ref[...] = t_ref[...].astype(o_ref.dtype)

    return pl.pallas_call(
        body, name=name,
        grid_spec=pltpu.PrefetchScalarGridSpec(
            num_scalar_prefetch=1, grid=(nblk,),
            in_specs=[pl.BlockSpec((None, tm, cc), lambda i, s: (lead, i, 0))], out_specs=o_spec),
        out_shape=jax.ShapeDtypeStruct(shp, BF16), compiler_params=_params("parallel"),
    )(chip.reshape(1).astype(jnp.int32), t)


def _gather_weights(arrs, shard_shapes, kinds, *, name):
    n = len(arrs)

    def body(*refs):
        dst = refs[n:2 * n]
        send_sems, recv_sems = refs[2 * n:]
        x, y, c = _coords()
        chip = 2 * x + y
        others = _other_chips(x, y)
        sib = (x, y, 1 - c)

        def window(p, chip_id, cc):
            r, cols = shard_shapes[p]
            h = r // 2
            if kinds[p] == "col":
                return dst[p].at[pl.ds(cc * h, h), pl.ds(pl.multiple_of(chip_id * cols, 128), cols)]
            if kinds[p] == "row":
                return dst[p].at[pl.ds(chip_id * r + cc * h, h), :]
            return dst[p].at[chip_id, pl.ds(cc * h, h), :]

        def copy(p, k, win, to):
            return pltpu.make_async_remote_copy(src_ref=win, dst_ref=win, send_sem=send_sems.at[6 * p + k],
                                                recv_sem=recv_sems.at[6 * p + k], device_id=to, device_id_type=MESH)

        sends = []
        for p in range(n):
            for j, (ox, oy, _) in enumerate(others):
                cp = copy(p, j, window(p, chip, c), (ox, oy, c))
                cp.start()
                sends.append(cp)
        for j, (_, _, oc) in enumerate(others):
            for p in range(n):
                copy(p, j, window(p, oc, c), (x, y, c)).wait_recv()
                fw = copy(p, 3 + j, window(p, oc, c), sib)
                fw.start()
                sends.append(fw)
        for p in range(n):
            for j, (_, _, oc) in enumerate(others):
                copy(p, 3 + j, window(p, oc, 1 - c), sib).wait_recv()
        for cp in sends:
            cp.wait_send()

    outs = [jax.ShapeDtypeStruct(a.shape, a.dtype) for a in arrs]
    return _hbm_call(body, arrs, outs, (6 * n, 6 * n), name=name, alias=True)


HBM_SPEC = pl.BlockSpec(memory_space=pltpu.HBM)
SEM_SPEC = pl.BlockSpec(memory_space=pltpu.SEMAPHORE)
EFFECT = pltpu.SideEffectType.DATAFLOW_SIDE_EFFECTING


def _window(ref, shard_shape, kind, chip_id, cc):
    r, cols = shard_shape
    h = r // 2
    if kind == "col":
        return ref.at[pl.ds(cc * h, h), pl.ds(pl.multiple_of(chip_id * cols, 128), cols)]
    if kind == "row":
        return ref.at[pl.ds(chip_id * r + cc * h, h), :]
    return ref.at[chip_id, pl.ds(cc * h, h), :]


def _split_start(start, arrs, n_sems, *, name):
    n, ns = len(arrs), len(n_sems)

    def body(*refs):
        start(refs[:n], refs[n:n + ns])
        refs[-1][...] = jnp.zeros_like(refs[-1])

    outs = pl.pallas_call(
        body, name=name,
        out_shape=tuple([pltpu.SemaphoreType.DMA((k,)) for k in n_sems] + [pltpu.HBM(a.shape, a.dtype) for a in arrs]
                        + [jax.ShapeDtypeStruct((8, 128), F32)]),
        in_specs=[HBM_SPEC] * n, out_specs=tuple([SEM_SPEC] * ns + [HBM_SPEC] * n + [pl.BlockSpec(memory_space=pltpu.VMEM)]),
        input_output_aliases={k: ns + k for k in range(n)},
        compiler_params=pltpu.CompilerParams(has_side_effects=EFFECT),
    )(*[pltpu.with_memory_space_constraint(a, pltpu.HBM) for a in arrs])
    return list(outs[:ns]), list(outs[ns:ns + n]), outs[-1]


def _split_wait(wait, arrs, sems, after, *, name):
    n, ns = len(arrs), len(sems)

    def body(*refs):
        wait(refs[:n], refs[n:n + ns])

    return pl.pallas_call(
        body, name=name, out_shape=tuple(pltpu.HBM(a.shape, a.dtype) for a in arrs),
        in_specs=[HBM_SPEC] * n + [SEM_SPEC] * ns + [pl.BlockSpec(memory_space=pl.ANY)], out_specs=tuple([HBM_SPEC] * n),
        input_output_aliases={k: k for k in range(n)},
        compiler_params=pltpu.CompilerParams(has_side_effects=EFFECT),
    )(*arrs, *sems, after)


def _gw_copies(refs, send_sems, recv_sems, shard_shapes, kinds, outgoing):
    x, y, c = _coords()
    chip = 2 * x + y
    out = []
    for p in range(len(refs)):
        for j, (ox, oy, oc) in enumerate(_other_chips(x, y)):
            win = _window(refs[p], shard_shapes[p], kinds[p], chip if outgoing else oc, c)
            out.append(pltpu.make_async_remote_copy(
                src_ref=win, dst_ref=win, send_sem=send_sems.at[3 * p + j], recv_sem=recv_sems.at[3 * p + j],
                device_id=(ox, oy, c), device_id_type=MESH))
    return out


def _gw_start(arrs, shard_shapes, kinds, groups, *, name):
    def start(refs, sems):
        for g, idx in enumerate(groups):
            for cp in _gw_copies([refs[p] for p in idx], sems[2 * g], sems[2 * g + 1], [shard_shapes[p] for p in idx],
                                 [kinds[p] for p in idx], True):
                cp.start()

    n_sems = [3 * len(idx) for idx in groups for _ in range(2)]
    sems, thru, token = _split_start(start, arrs, n_sems, name=name)
    return [(sems[2 * g], sems[2 * g + 1]) for g in range(len(groups))], thru, token


def _gw_wait(arrs, shard_shapes, kinds, sem_pair, after, *, name):
    def wait(refs, sems):
        for cp in _gw_copies(refs, sems[0], sems[1], shard_shapes, kinds, True):
            cp.wait_send()
        for cp in _gw_copies(refs, sems[0], sems[1], shard_shapes, kinds, False):
            cp.wait_recv()

    return _split_wait(wait, arrs, list(sem_pair), after, name=name)


def _gw_forward_copies(refs, shard_shapes, kinds, send_sems, recv_sems, incoming):
    x, y, c = _coords()
    out = []
    for p in range(len(refs)):
        for j, (_, _, oc) in enumerate(_other_chips(x, y)):
            win = _window(refs[p], shard_shapes[p], kinds[p], oc, 1 - c if incoming else c)
            out.append(pltpu.make_async_remote_copy(
                src_ref=win, dst_ref=win, send_sem=send_sems.at[3 * p + j], recv_sem=recv_sems.at[3 * p + j],
                device_id=(x, y, 1 - c), device_id_type=MESH))
    return out


def _gw_forward(arrs, shard_shapes, kinds, *, name):
    n = len(arrs)

    def body(*refs):
        dst, (send_sems, recv_sems) = refs[n:2 * n], refs[2 * n:]
        sends = _gw_forward_copies(dst, shard_shapes, kinds, send_sems, recv_sems, False)
        for cp in sends:
            cp.start()
        for cp in _gw_forward_copies(dst, shard_shapes, kinds, send_sems, recv_sems, True):
            cp.wait_recv()
        for cp in sends:
            cp.wait_send()

    outs = [jax.ShapeDtypeStruct(a.shape, a.dtype) for a in arrs]
    return _hbm_call(body, arrs, outs, (3 * n, 3 * n), name=name, alias=True)


def _gw_forward_start(arrs, shard_shapes, kinds, *, name):
    def start(refs, sems):
        for cp in _gw_forward_copies(refs, shard_shapes, kinds, sems[0], sems[1], False):
            cp.start()

    sems, thru, token = _split_start(start, arrs, [3 * len(arrs)] * 2, name=name)
    return (sems[0], sems[1]), thru, token


def _gw_forward_wait(pair, arrs, shard_shapes, kinds, after, *, name):
    def wait(refs, sems):
        for cp in _gw_forward_copies(refs, shard_shapes, kinds, sems[0], sems[1], False):
            cp.wait_send()
        for cp in _gw_forward_copies(refs, shard_shapes, kinds, sems[0], sems[1], True):
            cp.wait_recv()

    return _split_wait(wait, arrs, list(pair), after, name=name)


def _rs_chips_copies(parts, lands, send_sems, recv_sems):
    x, y, c = _coords()
    chip = 2 * x + y
    out = []
    for p in range(len(parts)):
        for j, (ox, oy, oc) in enumerate(_other_chips(x, y)):
            out.append(pltpu.make_async_remote_copy(
                src_ref=parts[p].at[oc], dst_ref=lands[p].at[chip], send_sem=send_sems.at[3 * p + j],
                recv_sem=recv_sems.at[3 * p + j], device_id=(ox, oy, c), device_id_type=MESH))
    return out


def _rs_chips_start(parts, *, name):
    n = len(parts)

    def start(refs, sems):
        for cp in _rs_chips_copies(refs[:n], refs[n:], sems[0], sems[1]):
            cp.start()

    lands = [lax.empty(t.shape, t.dtype) for t in parts]
    sems, thru, token = _split_start(start, list(parts) + lands, [3 * n, 3 * n], name=name)
    return (sems[0], sems[1]), thru[:n], thru[n:], token


def _rs_chips_wait(groups, after, *, name):
    sizes = [len(g[1]) for g in groups]
    arrs = [a for g in groups for a in list(g[1]) + list(g[2])]
    sems = [s for g in groups for s in g[0]]

    def wait(refs, sem_refs):
        o = 0
        for k, n in enumerate(sizes):
            for cp in _rs_chips_copies(refs[o:o + n], refs[o + n:o + 2 * n], sem_refs[2 * k], sem_refs[2 * k + 1]):
                cp.wait()
            o += 2 * n

    outs = _split_wait(wait, arrs, sems, after, name=name)
    res, o = [], 0
    for n in sizes:
        res.append((list(outs[o:o + n]), list(outs[o + n:o + 2 * n])))
        o += 2 * n
    return res


def _rs_cores_copies(grads, lands, send_sems, recv_sems):
    x, y, c = _coords()
    out, o = [], 0
    for p in range(len(grads)):
        nsh, h = lands[p].shape[0], lands[p].shape[1]
        for j in range(nsh):
            out.append(pltpu.make_async_remote_copy(
                src_ref=grads[p].at[j, pl.ds((1 - c) * h, h), :], dst_ref=lands[p].at[j],
                send_sem=send_sems.at[o + j], recv_sem=recv_sems.at[o + j], device_id=(x, y, 1 - c), device_id_type=MESH))
        o += nsh
    return out


def _rs_cores_start(grads, *, name):
    n = len(grads)
    tot = sum(g.shape[0] for g in grads)

    def start(refs, sems):
        for cp in _rs_cores_copies(refs[:n], refs[n:], sems[0], sems[1]):
            cp.start()

    lands = [lax.empty((g.shape[0], g.shape[1] // 2, g.shape[2]), g.dtype) for g in grads]
    sems, thru, token = _split_start(start, list(grads) + lands, [tot, tot], name=name)
    return (sems[0], sems[1]), thru[:n], thru[n:], token


def _rs_cores_wait(pair, grads, lands, after, *, name):
    n = len(grads)

    def wait(refs, sems):
        for cp in _rs_cores_copies(refs[:n], refs[n:], sems[0], sems[1]):
            cp.wait()

    outs = _split_wait(wait, list(grads) + list(lands), list(pair), after, name=name)
    return list(outs[:n]), list(outs[n:])


def _rs_cores(grads, *, name):
    n = len(grads)
    outs = [jax.ShapeDtypeStruct((g.shape[0], g.shape[1] // 2, g.shape[2]), g.dtype) for g in grads]

    def body(*refs):
        src, dst = refs[:n], refs[n:2 * n]
        send_sems, recv_sems = refs[2 * n:]
        x, y, c = _coords()
        copies = []
        for p in range(n):
            nsh, r, _ = grads[p].shape
            h = r // 2
            for j in range(nsh):
                cp = pltpu.make_async_remote_copy(
                    src_ref=src[p].at[j, pl.ds((1 - c) * h, h), :], dst_ref=dst[p].at[j],
                    send_sem=send_sems.at[nsh * p + j], recv_sem=recv_sems.at[nsh * p + j],
                    device_id=(x, y, 1 - c), device_id_type=MESH)
                cp.start()
                copies.append(cp)
        for cp in copies:
            cp.wait()

    tot = sum(g.shape[0] for g in grads)
    return _hbm_call(body, grads, outs, (tot, tot), name=name)


def _sum_own_half(full, recv, ci, out_dtype, *, name):
    nsh, h, cols = recv.shape
    tm = h if nsh * h * cols * 4 <= (2 << 20) else _tile_rows(h, 256)
    nblk = h // tm

    def body(c_ref, f_ref, r_ref, o_ref):
        o_ref[...] = (f_ref[...] + r_ref[...]).astype(o_ref.dtype)

    return pl.pallas_call(
        body, name=name,
        grid_spec=pltpu.PrefetchScalarGridSpec(
            num_scalar_prefetch=1, grid=(nsh, nblk),
            in_specs=[pl.BlockSpec((1, tm, cols), lambda j, i, c_ref: (j, c_ref[0] * nblk + i, 0)),
                      pl.BlockSpec((1, tm, cols), lambda j, i, c_ref: (j, i, 0))],
            out_specs=pl.BlockSpec((1, tm, cols), lambda j, i, c_ref: (j, i, 0))),
        out_shape=jax.ShapeDtypeStruct((nsh, h, cols), out_dtype), compiler_params=_params("parallel", "parallel"),
    )(ci.reshape(1).astype(jnp.int32), full, recv)


def _sum_chips(recv, own, chip, ci, *, name, nlead=1, lead=0, prev=None, spread=False):
    nsh, h, cols = recv.shape
    tm = h if nsh * h * cols * 4 <= (2 << 20) else _tile_rows(h, 256)
    nblk = h // tm
    rows_out = 2 * h * (nsh if spread else 1)

    def body(s_ref, r_ref, o_ref, *rest):
        out_ref = rest[-1]
        t = None
        for s in range(nsh):
            v = jnp.where(s_ref[0] == s, o_ref[s], r_ref[s]).astype(F32)
            t = v if t is None else t + v
        out_ref[...] = t

    def out_idx(i, s):
        return (lead, (s[0] * 2 * nblk if spread else 0) + s[1] * nblk + i, 0)

    blk = pl.BlockSpec((nsh, tm, cols), lambda i, s: (0, i, 0))
    ins = [recv, own] + ([prev] if prev is not None else [])
    return pl.pallas_call(
        body, name=name,
        grid_spec=pltpu.PrefetchScalarGridSpec(
            num_scalar_prefetch=1, grid=(nblk,),
            in_specs=[blk, blk] + ([pl.BlockSpec(memory_space=pl.ANY)] if prev is not None else []),
            out_specs=pl.BlockSpec((None, tm, cols), out_idx)),
        out_shape=jax.ShapeDtypeStruct((nlead, rows_out, cols), F32),
        input_output_aliases={3: 0} if prev is not None else {},
        compiler_params=_params("arbitrary"),
    )(jnp.stack([chip, ci]).astype(jnp.int32), *ins)


RS_GATHER_CHUNKS = 4


def _rs_gather_copies(refs, nleads, halves, spread, send_sems, recv_sems, incoming):
    x, y, c = _coords()
    chip = 2 * x + y
    out, sem = [], 0
    for p in range(len(refs)):
        h = halves[p]
        q = h // RS_GATHER_CHUNKS
        base = (chip * 2 * h if spread[p] else 0) + (1 - c if incoming else c) * h
        for l in range(nleads[p]):
            for k in range(RS_GATHER_CHUNKS):
                win = refs[p].at[l, pl.ds(base + k * q, q), :]
                out.append(pltpu.make_async_remote_copy(src_ref=win, dst_ref=win, send_sem=send_sems.at[sem],
                                                        recv_sem=recv_sems.at[sem], device_id=(x, y, 1 - c),
                                                        device_id_type=MESH))
                sem += 1
    return out


def _rs_gather_start(arrs, halves, spread, *, name):
    nleads = [a.shape[0] for a in arrs]
    tot = sum(nleads) * RS_GATHER_CHUNKS

    def start(refs, sems):
        for cp in _rs_gather_copies(refs, nleads, halves, spread, sems[0], sems[1], False):
            cp.start()

    sems, thru, token = _split_start(start, arrs, [tot, tot], name=name)
    return (sems[0], sems[1]), thru, token


def _rs_gather_wait(pair, arrs, halves, spread, after, *, name):
    nleads = [a.shape[0] for a in arrs]

    def wait(refs, sems):
        for cp in _rs_gather_copies(refs, nleads, halves, spread, sems[0], sems[1], False):
            cp.wait_send()
        for cp in _rs_gather_copies(refs, nleads, halves, spread, sems[0], sems[1], True):
            cp.wait_recv()

    return _split_wait(wait, arrs, list(pair), after, name=name)


def _adamw(w, g, m, v, *, name):
    nl, R, C = w.shape
    tm = _tile_rows(R, 256)

    blk = pl.BlockSpec((None, tm, C), lambda l, i: (l, i, 0))
    return pl.pallas_call(
        _adamw_body_copy(), name=name, grid=(nl, R // tm), in_specs=[blk] * 4, out_specs=[blk] * 3,
        out_shape=[jax.ShapeDtypeStruct((nl, R, C), F32)] * 3, compiler_params=_params("parallel", "parallel"),
    )(w, g, m, v)


def _adamw_body(w_ref, g_ref, m_ref, v_ref, d_ref, nm_ref, nv_ref):
    gg = g_ref[...]
    nm = B1 * m_ref[...] + (1.0 - B1) * gg
    nv = B2 * v_ref[...] + (1.0 - B2) * (gg * gg)
    m_hat = nm / (1.0 - B1 ** ASTEP)
    v_hat = nv / (1.0 - B2 ** ASTEP)
    d_ref[...] = -LR * (m_hat / (jnp.sqrt(v_hat) + AEPS) + WD * w_ref[...])
    nm_ref[...] = nm
    nv_ref[...] = nv


def _adamw_whole(w, g, m, v, *, name):
    return pl.pallas_call(_adamw_body_copy(), name=name, out_shape=[jax.ShapeDtypeStruct(w.shape, F32)] * 3,
                          compiler_params=_params())(w, g, m, v)


def _adamw_body_copy():
    def body(*refs):
        _adamw_body(*refs)
    return body


def _mod_cols(c_all, w_ada, b_cols, *, name):
    nl, D, cols = w_ada.shape
    B = c_all.shape[0]

    def body(c_ref, w_ref, b_ref, o_ref):
        cc = c_ref[...]
        cs = (cc * _sigmoid(cc)).astype(BF16)
        o_ref[0] = jnp.dot(cs, w_ref[0].astype(BF16), preferred_element_type=F32) + b_ref[0]

    return pl.pallas_call(
        body, name=name, grid=(nl,),
        in_specs=[_whole(c_all.shape), pl.BlockSpec((1, D, cols), lambda i: (i, 0, 0)), pl.BlockSpec((1, 1, cols), lambda i: (i, 0, 0))],
        out_specs=pl.BlockSpec((1, B, cols), lambda i: (i, 0, 0)),
        out_shape=jax.ShapeDtypeStruct((nl, B, cols), F32), compiler_params=_params("arbitrary"),
    )(c_all, w_ada, b_cols)


def _ada_grad(c_all, dmod_cols, *, name):
    nl, B, cols = dmod_cols.shape
    D = c_all.shape[1]

    def body(c_ref, d_ref, o_ref):
        cc = c_ref[...]
        cs = (cc * _sigmoid(cc)).astype(BF16)
        o_ref[0] = lax.dot_general(cs, d_ref[0].astype(BF16), TN, preferred_element_type=F32)

    return pl.pallas_call(
        body, name=name, grid=(nl,),
        in_specs=[_whole(c_all.shape), pl.BlockSpec((1, B, cols), lambda i: (i, 0, 0))],
        out_specs=pl.BlockSpec((1, D, cols), lambda i: (i, 0, 0)),
        out_shape=jax.ShapeDtypeStruct((nl, D, cols), F32), compiler_params=_params("arbitrary"),
    )(c_all, dmod_cols)


def _s5_disc(a_re, a_im, log_dt, b_re, b_im):
    dt = jnp.exp(log_dt)[:, None]
    mag = jnp.exp(a_re * dt)
    ph = a_im * dt
    lb_re = mag * jnp.cos(ph)
    lb_im = mag * jnp.sin(ph)
    den = a_re * a_re + a_im * a_im
    nr = lb_re - 1.0
    ni = lb_im
    f_re = (nr * a_re + ni * a_im) / den
    f_im = (ni * a_re - nr * a_im) / den
    bb_re = f_re[..., None] * b_re - f_im[..., None] * b_im
    bb_im = f_re[..., None] * b_im + f_im[..., None] * b_re
    return lb_re, lb_im, bb_re, bb_im


def _to_segments(t):
    L, D = t.shape
    R = min(S5_R, L)
    return t.reshape(L // R, NSEG, R // NSEG, D).transpose(0, 2, 1, 3).reshape(L, D)


def _from_segments(t):
    L, D = t.shape
    R = min(S5_R, L)
    return t.reshape(L // R, R // NSEG, NSEG, D).transpose(0, 2, 1, 3).reshape(L, D)


def _w_in_layout(QK, D, ncols):
    segs = [(0, 4 * QK, 0), (4 * QK + GATE_RANK, 4 * QK + GATE_RANK + D, 4 * QK), (4 * QK, 4 * QK + GATE_RANK, 4 * QK + D)]
    fwd = []
    for lo, hi, _ in segs:
        col = lo
        while col < hi:
            j = col // ncols
            end = min(hi, (j + 1) * ncols)
            fwd.append((j, col - j * ncols, end - j * ncols))
            col = end
    bwd = []
    for j in range(NCH):
        ranges, col = [], j * ncols
        while col < (j + 1) * ncols:
            lo, hi, rlo = next(sg for sg in segs if sg[0] <= col < sg[1])
            end = min((j + 1) * ncols, hi)
            ranges.append((rlo + col - lo, rlo + end - lo))
            col = end
        bwd.append(ranges)
    return fwd, bwd


def _mlp_fwd(h2, w1, w2, tag):
    a = _matmul(h2, w1, name=f"ff1_{tag}", tn=2048, out_dtypes=(BF16,), epi=lambda acc: (jnp.maximum(acc, 0.0),))
    f = _matmul(a, w2, name=f"ff2_{tag}", a_fn=jnp.square)
    return a, f


def _mlp_bwd(df, h2, a, w1, w2, tag):
    da = _matmul(df, w2, tb=True, name=f"ff2_dx_{tag}", tn=2048, out_dtypes=(BF16,), epi_ins=(a,),
                 epi=lambda acc, at: (acc * (2.0 * at.astype(F32)),))
    dw2 = _matmul(a, df, ta=True, name=f"ff2_dw_{tag}", a_fn=jnp.square)
    dh2 = _matmul(da, w1, tb=True, name=f"ff1_dx_{tag}")
    dw1 = _matmul(h2, da, ta=True, name=f"ff1_dw_{tag}", col_shards=NCH)
    return dh2, dw1, dw2


def kernel(x, c, w_ada, b_ada, norm_mix, norm_mlp, s5_a_re, s5_a_im, s5_log_dt, s5_b_re, s5_b_im, s5_c_re, s5_c_im, s5_d, s5_w_glu, gla_w_in, gla_w_gate2, gla_b_gate, gla_g_norm, gla_w_out, w_ff1, w_ff2, norm_final, loss_target, m_w_ada, m_b_ada, m_norm_mix, m_norm_mlp, m_s5_a_re, m_s5_a_im, m_s5_log_dt, m_s5_b_re, m_s5_b_im, m_s5_c_re, m_s5_c_im, m_s5_d, m_s5_w_glu, m_gla_w_in, m_gla_w_gate2, m_gla_b_gate, m_gla_g_norm, m_gla_w_out, m_w_ff1, m_w_ff2, m_norm_final, v_w_ada, v_b_ada, v_norm_mix, v_norm_mlp, v_s5_a_re, v_s5_a_im, v_s5_log_dt, v_s5_b_re, v_s5_b_im, v_s5_c_re, v_s5_c_im, v_s5_d, v_s5_w_glu, v_gla_w_in, v_gla_w_gate2, v_gla_b_gate, v_gla_g_norm, v_gla_w_out, v_w_ff1, v_w_ff2, v_norm_final):
    args = dict(locals())
    L, D = x.shape[1], x.shape[2]
    QK = D // 2
    xi, yi, ci = _coords()
    chip = 2 * xi + yi
    dev = 2 * chip + ci

    cat = jnp.concatenate([gla_w_gate2[0].reshape(1, -1), gla_b_gate, gla_g_norm], axis=1)
    first = _exchange(jnp.concatenate([c.reshape(8, D // 8), jnp.tile(cat, (8, 1))], axis=1), MASK_ALL, name="gather_c")
    c_all = first[:, :, :D // 8].reshape(8, D)
    cat_all = first[0::2, 0, D // 8:]
    acols = w_ada.shape[2]
    b_cols = lax.dynamic_slice_in_dim(b_ada, chip * acols, acols, axis=1)[:, None, :]
    mod_cols = _mod_cols(c_all, w_ada, b_cols, name="ada_mod")
    mod_all = _exchange(mod_cols.reshape(16, acols), MASK_CHIPS, name="gather_mod")
    mod_all = mod_all.reshape(NCH, 2, 8, acols).transpose(1, 2, 0, 3).reshape(2, 8, NCH * acols)
    mod = lax.dynamic_index_in_dim(mod_all, dev, axis=1, keepdims=False).reshape(2, 6, 1, D)

    big = [("s5_w_glu", s5_w_glu, 0, "col"), ("gla_w_in", gla_w_in, 0, "slot"), ("gla_w_out", gla_w_out, 0, "row"),
           ("w_ff1_0", w_ff1, 0, "col"), ("w_ff1_1", w_ff1, 1, "col"), ("w_ff2_0", w_ff2, 0, "row"), ("w_ff2_1", w_ff2, 1, "row")]
    own16 = [_cast_into(t, lead, kind, chip, name=f"cast_{nm}") for nm, t, lead, kind in big]
    wshapes, wkinds = [b[1].shape[-2:] for b in big], [b[3] for b in big]
    wgroups = [[0], [3, 5], [1, 2, 4, 6]]
    wsems, wthru, wtoken = _gw_start(own16, wshapes, wkinds, wgroups, name="gather_w_start")
    W = {}

    def weights_landed(g, after):
        idx = wgroups[g]
        shp, knd = [wshapes[p] for p in idx], [wkinds[p] for p in idx]
        return _gw_wait([wthru[p] for p in idx], shp, knd, wsems[g], after, name=f"gather_w_wait{g}"), shp, knd

    def weights_ready(g, arrs):
        for p, w in zip(wgroups[g], arrs):
            W[big[p][0]] = w

    def forward_start(g, after):
        got, shp, knd = weights_landed(g, after)
        pair, thru, token = _gw_forward_start(got, shp, knd, name=f"gather_w_cores_start{g}")
        return (pair, thru, shp, knd), token

    def forward_finish(g, state, after):
        pair, thru, shp, knd = state
        weights_ready(g, _gw_forward_wait(pair, thru, shp, knd, after, name=f"gather_w_cores_wait{g}"))

    qk4 = QK // NCH
    wg2 = cat_all[:, :GATE_RANK * qk4].reshape(NCH, GATE_RANK, qk4).transpose(1, 0, 2).reshape(GATE_RANK, QK)
    bg = cat_all[:, GATE_RANK * qk4:(GATE_RANK + 1) * qk4].reshape(1, QK)
    gn = cat_all[:, (GATE_RANK + 1) * qk4:].reshape(1, D)
    wg2p = jnp.concatenate([wg2, jnp.zeros((128 - GATE_RANK, QK), F32)], axis=0).astype(BF16)

    lb_re, lb_im, bb_re, bb_im = _s5_disc(s5_a_re[0], s5_a_im[0], s5_log_dt[0], s5_b_re[0], s5_b_im[0])
    tb = _s5_tables(lb_re, lb_im, bb_re, bb_im, s5_c_re[0], s5_c_im[0], min(S5_R, L) // NSEG)
    s5_dv = s5_d + wtoken[0, 0]

    def vec(t):
        return t.reshape(1, -1)

    m0, m1 = mod[0], mod[1]
    xp = _to_segments(x[0])
    (u0,) = _rows(lambda t, g, sc, sh: (_norm_mod(t, g, sc, sh),), [xp], [vec(norm_mix[0]), m0[1], m0[0]],
                  [(D, F32)], [], name="pre_mix0")
    y0, z0, ck0 = _s5_fwd(u0, tb, s5_dv, name="s5_fwd")
    got, shp, knd = weights_landed(0, z0)
    weights_ready(0, _gw_forward(got, shp, knd, name="gather_w_cores0"))
    vg0 = _matmul(z0, W["s5_w_glu"], name="glu", tn=2048)
    fwd1, ftok1 = forward_start(1, vg0)

    def res_glu_pre(xt, vgt, gt, g, sc, sh):
        xn = xt + gt * (vgt[:, :D] * _sigmoid(vgt[:, D:]))
        return xn, _norm_mod(xn, g, sc, sh)

    x2_0, h2_0 = _rows(res_glu_pre, [xp, vg0], [m0[2] + ftok1[0, 0], vec(norm_mlp[0]), m0[4], m0[3]],
                       [(D, F32), (D, BF16)], [], name="res_mix0")
    forward_finish(1, fwd1, h2_0)
    a_0, f0 = _mlp_fwd(h2_0, W["w_ff1_0"], W["w_ff2_0"], "0")
    fwd2, ftok2 = forward_start(2, f0)

    def res_pre(xt, bt, gt, g, sc, sh):
        xn = xt + gt * bt
        return xn, _norm_mod(xn, g, sc, sh)

    x3p, h1p = _rows(res_pre, [x2_0, f0], [m0[5] + ftok2[0, 0], vec(norm_mix[1]), m1[1], m1[0]],
                     [(D, F32), (D, BF16)], [], name="res_mlp0")
    x3, h1 = _from_segments(x3p), _from_segments(h1p)
    forward_finish(2, fwd2, h1)
    in_fwd, in_bwd = _w_in_layout(QK, D, gla_w_in.shape[2])
    w_in_r = jnp.concatenate([W["gla_w_in"][j, :, lo:hi] for j, lo, hi in in_fwd]
                             + [jnp.zeros((D, 128 - GATE_RANK), BF16)], axis=1)
    proj = _matmul(h1, w_in_r, name="gla_in", tn=640)
    og, states = _gla_fwd(proj, wg2p, bg, gn, name="gla_fwd")
    ymix = _matmul(og, W["gla_w_out"], name="gla_out")
    x2_1, h2_1 = _rows(res_pre, [x3, ymix], [m1[2], vec(norm_mlp[1]), m1[4], m1[3]], [(D, F32), (D, BF16)], [],
                       name="res_mix1")
    a_1, f1 = _mlp_fwd(h2_1, W["w_ff1_1"], W["w_ff2_1"], "1")

    def final(xt, ft, tgt, gt, g):
        xn = xt + gt * ft
        rs = lax.rsqrt(jnp.mean(xn * xn, axis=-1, keepdims=True) + EPS)
        xh = xn * rs
        e = xh * g - tgt
        dout = e * (1.0 / D)
        dxh = dout * g
        dx = rs * (dxh - xh * jnp.mean(dxh * xh, axis=-1, keepdims=True))
        lsum = 0.5 * jnp.sum(jnp.sum(e * e, axis=-1, keepdims=True), axis=0, keepdims=True) * (1.0 / D)
        return dx, dx * gt, jnp.broadcast_to(lsum, (1, 128)), _rsum(dout * xh), _rsum(dx * ft)

    dx, df1, loss_part, d_norm_final, dgt2_1 = _rows(
        final, [x2_1, f1, loss_target[0]], [m1[5], vec(norm_final)], [(D, F32), (D, BF16)],
        [(1, 128), (1, D), (1, D)], name="loss_head")

    def gate_bwd(dxt, bt, gt):
        return dxt * gt, _rsum(dxt * bt)

    def norm_bwd(xt, dht, drt, g, sc):
        dxn, dsh, dsc, dg = _norm_mod_bwd(xt, dht, g, sc)
        return drt + dxn, dsh, dsc, dg

    def norm_gate_bwd(xt, dht, drt, bt, g, sc, gt):
        dxn, dsh, dsc, dg = _norm_mod_bwd(xt, dht, g, sc)
        dxt = drt + dxn
        return dxt, dxt * gt, dsh, dsc, dg, _rsum(dxt * bt)

    vD = [(1, D)]
    dh2_1, dw_ff1_1, dw_ff2_1 = _mlp_bwd(df1, h2_1, a_1, W["w_ff1_1"], W["w_ff2_1"], "1")
    dx, dmix1, dsh2_1, dsc2_1, dg_mlp1, dgt1_1 = _rows(
        norm_gate_bwd, [x2_1, dh2_1, dx, ymix], [vec(norm_mlp[1]), m1[4], m1[2]], [(D, F32), (D, BF16)], vD * 4,
        name="norm_mlp1_bwd")
    dog = _matmul(dmix1, W["gla_w_out"], tb=True, name="gla_out_dx")
    dw_out = _matmul(og, dmix1, ta=True, name="gla_out_dw")
    dproj, dwg2p, dbg, dgn = _gla_bwd(proj, dog, states, wg2p, bg, gn, name="gla_bwd")
    dh1 = _matmul(dproj, w_in_r, tb=True, name="gla_in_dx", tk=3200)
    dw_in_r = _matmul(h1, dproj, ta=True, name="gla_in_dw", tn=640)
    dx, dsh1_1, dsc1_1, dg_mix1 = _rows(norm_bwd, [x3, dh1, dx], [vec(norm_mix[1]), m1[1]], [(D, F32)], vD * 3,
                                        name="norm_mix1_bwd")
    dxp = _to_segments(dx)
    tags = [b[0] for b in big] + ["small"]
    rs_groups = []

    def rs_chips_begin(idx, srcs, r1, gname):
        s1 = [_sum_own_half(g, r, ci, F32 if tags[k] == "small" else BF16, name=f"rs_sum_cores_{tags[k]}")
              for g, r, k in zip(srcs, r1, idx)]
        pair, parts, lands, token = _rs_chips_start(s1, name=f"rs_chips_start_{gname}")
        rs_groups.append((idx, pair, parts, lands))
        return token

    def rs_begin(idx, srcs, gname):
        return rs_chips_begin(idx, srcs, _rs_cores(srcs, name=f"rs_cores_{gname}"), gname)

    dw_in = jnp.stack([jnp.concatenate([dw_in_r[:, lo:hi] for lo, hi in in_bwd[j]], axis=1) for j in range(NCH)])
    idx1 = [1, 2, 4, 6]
    pair1, src1, land1, tok1 = _rs_cores_start(
        [dw_in, dw_out.reshape(NCH, -1, D), dw_ff1_1, dw_ff2_1.reshape(NCH, -1, D)], name="rs_cores_start_l1")

    df0, dgt2_0 = _rows(gate_bwd, [dxp, f0], [m0[5] + tok1[0, 0]], [(D, BF16)], vD, name="gate_mlp0")
    dh2_0, dw_ff1_0, dw_ff2_0 = _mlp_bwd(df0, h2_0, a_0, W["w_ff1_0"], W["w_ff2_0"], "0")
    src1, land1 = _rs_cores_wait(pair1, src1, land1, dh2_0, name="rs_cores_wait_l1")
    tok1b = rs_chips_begin(idx1, src1, land1, "l1")
    idx0 = [3, 5]
    pair0, src0, land0, tok0 = _rs_cores_start([dw_ff1_0, dw_ff2_0.reshape(NCH, -1, D)], name="rs_cores_start_l0")
    tok2 = tok1b + tok0

    def norm_glu_bwd(xt, dht, drt, vgt, g, sc, gt):
        dxn, dsh, dsc, dg = _norm_mod_bwd(xt, dht, g, sc)
        dxt = drt + dxn
        val, sg = vgt[:, :D], _sigmoid(vgt[:, D:])
        dbr = dxt * gt
        dvg = jnp.concatenate([dbr * sg, dbr * val * sg * (1.0 - sg)], axis=1)
        return dxt, dvg, dsh, dsc, dg, _rsum(dxt * val * sg)

    dxp, dvg0, dsh2_0, dsc2_0, dg_mlp0, dgt1_0 = _rows(
        norm_glu_bwd, [x2_0, dh2_0, dxp, vg0], [vec(norm_mlp[0]), m0[4] + tok2[0, 0], m0[2]], [(D, F32), (2 * D, BF16)],
        vD * 4, name="norm_mlp0_bwd")
    dz0 = _matmul(dvg0, W["s5_w_glu"], tb=True, name="glu_dx")
    dw_glu = _matmul(z0, dvg0, ta=True, name="glu_dw", tn=512, col_shards=NCH)
    src0, land0 = _rs_cores_wait(pair0, src0, land0, dw_glu, name="rs_cores_wait_l0")
    tok0b = rs_chips_begin(idx0 + [0], src0 + [dw_glu], land0 + list(_rs_cores([dw_glu], name="rs_cores_glu")), "l0")
    du0, db_acc, dc_acc, dl_acc, dd_s5 = _s5_bwd(u0, y0, dz0, ck0, tb, s5_dv + tok0b[0, 0], name="s5_bwd")
    dxp, dsh1_0, dsc1_0, dg_mix0 = _rows(norm_bwd, [xp, du0, dxp], [vec(norm_mix[0]), m0[1]], [(D, F32)], vD * 3,
                                         name="norm_mix0_bwd")
    grad_x = _from_segments(dxp)[None]

    dmod = jnp.concatenate([dsh1_0, dsc1_0, dgt1_0, dsh2_0, dsc2_0, dgt2_0,
                            dsh1_1, dsc1_1, dgt1_1, dsh2_1, dsc2_1, dgt2_1], axis=1)
    dbb_re, dbb_im = _s5_untable(db_acc)
    dc_re, dc_im_neg = _s5_untable(dc_acc)
    nbk = D // 128
    dl = dl_acc.reshape(nbk, NSEG, 2, GPB * S5_P).sum(axis=1)
    smalls = [dmod, dg_mix0, dg_mix1, dg_mlp0, dg_mlp1, d_norm_final, dd_s5, dbg, dgn,
              dwg2p[:GATE_RANK].reshape(1, -1), dbb_re.reshape(1, -1), dbb_im.reshape(1, -1),
              dc_re.reshape(1, -1), dc_im_neg.reshape(1, -1), dl.reshape(1, -1), loss_part]
    ssz = [t.shape[1] for t in smalls]
    stot = sum(ssz)
    spad = -(-stot // 8192) * 8192
    svec = jnp.concatenate(smalls + [jnp.zeros((1, spad - stot), F32)], axis=1).reshape(NCH, spad // (128 * NCH), 128)


    rs_begin([7], [svec], "last")
    landed = _rs_chips_wait([(g[1], g[2], g[3]) for g in rs_groups], grad_x, name="rs_chips_wait")
    s1, r2 = {}, {}
    for (idx, _, _, _), (parts, lands) in zip(rs_groups, landed):
        for k, part, land in zip(idx, parts, lands):
            s1[k], r2[k] = part, land

    def fin(k, **kw):
        return _sum_chips(r2[k], s1[k], chip, ci, name=f"rs_sum_chips_{tags[k]}", **kw)

    f_ff1 = fin(4, nlead=2, lead=1, prev=fin(3, nlead=2, lead=0))
    f_ff2 = fin(6, nlead=2, lead=1, prev=fin(5, nlead=2, lead=0))
    finals = [fin(0), fin(1), fin(2), f_ff1, f_ff2, fin(7, spread=True)]
    halves = [t.shape[1] for t in (s1[0], s1[1], s1[2], s1[3], s1[5], s1[7])]
    gpair, gthru, gtok = _rs_gather_start(finals, halves, [False] * 5 + [True], name="rs_gather_cores_start")

    dmod_all = _exchange((dmod + gtok[0, 0]).reshape(12 * D // 128, 128), MASK_ALL, name="gather_dmod").reshape(8, 2, 6 * D)
    dmod_cols = lax.dynamic_slice_in_dim(dmod_all, chip * acols, acols, axis=2).transpose(1, 0, 2)
    g_w_ada = _ada_grad(c_all, dmod_cols, name="ada_grad")
    upd_w_ada = _adamw(w_ada, g_w_ada, m_w_ada, v_w_ada, name="adamw_w_ada")

    g_glu, g_in, g_out, g_w_ff1, g_w_ff2, s_own = _rs_gather_wait(
        gpair, gthru, halves, [False] * 5 + [True], upd_w_ada[0], name="rs_gather_cores_wait")
    srows = spad // (128 * NCH)
    (s_sum,) = _gather_weights([s_own.reshape(NCH * srows, 128)], [(srows, 128)], ["row"], name="gather_small_grads")
    s_sum = s_sum.reshape(-1)
    so = [sum(ssz[:k]) for k in range(len(ssz))]
    sm = [s_sum[o:o + n] for o, n in zip(so, ssz)]
    (dmod_s, g_mix0, g_mix1, g_mlp0, g_mlp1, g_nf, g_d, g_bg, g_gn, g_wg2, g_bbre, g_bbim, g_cre, g_cimn, g_dl, loss_s) = sm
    loss = loss_s[0]
    g_b_ada = dmod_s.reshape(2, 6 * D)

    G = D // S5_H
    _, disc_vjp = jax.vjp(_s5_disc, s5_a_re[0], s5_a_im[0], s5_log_dt[0], s5_b_re[0], s5_b_im[0])
    g_dl = g_dl.reshape(nbk, 2, GPB, S5_P)
    ct = (g_dl[:, 0].reshape(G, S5_P), g_dl[:, 1].reshape(G, S5_P),
          g_bbre.reshape(G, S5_H, S5_P).transpose(0, 2, 1), g_bbim.reshape(G, S5_H, S5_P).transpose(0, 2, 1))
    g_a_re, g_a_im, g_log_dt, g_b_re, g_b_im = disc_vjp(ct)
    g_c_re = g_cre.reshape(G, S5_H, S5_P)
    g_c_im = -g_cimn.reshape(G, S5_H, S5_P)
    g_wg2_s = lax.dynamic_slice_in_dim(g_wg2.reshape(GATE_RANK, QK), chip * qk4, qk4, axis=1)
    g_bg_s = lax.dynamic_slice_in_dim(g_bg.reshape(1, QK), chip * qk4, qk4, axis=1)
    g_gn_s = lax.dynamic_slice_in_dim(g_gn.reshape(1, D), chip * (D // NCH), D // NCH, axis=1)

    grads = dict(
        w_ada=g_w_ada, b_ada=g_b_ada, norm_mix=jnp.stack([g_mix0, g_mix1]), norm_mlp=jnp.stack([g_mlp0, g_mlp1]),
        s5_a_re=g_a_re[None], s5_a_im=g_a_im[None], s5_log_dt=g_log_dt[None], s5_b_re=g_b_re[None], s5_b_im=g_b_im[None],
        s5_c_re=g_c_re[None], s5_c_im=g_c_im[None], s5_d=g_d[None], s5_w_glu=g_glu,
        gla_w_in=g_in, gla_w_gate2=g_wg2_s[None], gla_b_gate=g_bg_s, gla_g_norm=g_gn_s,
        gla_w_out=g_out, w_ff1=g_w_ff1, w_ff2=g_w_ff2, norm_final=g_nf)

    names = list(grads)
    large = ("w_ada", "s5_w_glu", "gla_w_in", "gla_w_out", "w_ff1", "w_ff2")
    delta, new_m, new_v = {}, {}, {}
    delta["w_ada"], new_m["w_ada"], new_v["w_ada"] = upd_w_ada
    for nm in large[1:]:
        delta[nm], new_m[nm], new_v[nm] = _adamw(args[nm], grads[nm], args["m_" + nm], args["v_" + nm], name=f"adamw_{nm}")
    grads = {nm: grads[nm].reshape(args[nm].shape) for nm in names}
    for nm in names:
        if nm not in large:
            shp = args[nm].shape
            as2d = (1, -1) if len(shp) == 1 else shp
            outs = _adamw_whole(*[t.reshape(as2d) for t in (args[nm], grads[nm], args["m_" + nm], args["v_" + nm])],
                                name=f"adamw_{nm}")
            delta[nm], new_m[nm], new_v[nm] = (t.reshape(shp) for t in outs)
    return (loss, grad_x, *[grads[n] for n in names], *[delta[n] for n in names], *[new_m[n] for n in names],
            *[new_v[n] for n in names])
```

```python
import math

import jax
import jax.numpy as jnp
from jax import lax
from jax.experimental import pallas as pl
from jax.experimental.pallas import tpu as pltpu

F32 = jnp.float32
BF16 = jnp.bfloat16
MESH = pl.DeviceIdType.MESH

EPS = 1e-6
CHUNK = 64
GLA_NB = 4
S5_H = 16
S5_P = 64
GPB = 8
NSEG = 8
HEADS = 4
GATE_RANK = 16
GATE_TAU = 16.0
NCH = 4
LR, B1, B2, AEPS, WD, ASTEP = 0.001, 0.9, 0.999, 1e-08, 0.01, 10
VMEM_LIMIT = 56 << 20
ROW_SUB = 64

MASK_CHIPS = ((1, 0, 0), (0, 1, 0), (1, 1, 0))
MASK_ALL = ((0, 0, 1), (0, 1, 0), (0, 1, 1), (1, 0, 0), (1, 0, 1), (1, 1, 0), (1, 1, 1))


def _params(*sem):
    return pltpu.CompilerParams(dimension_semantics=sem or None, vmem_limit_bytes=VMEM_LIMIT)


def _tile_rows(rows, cap=512):
    best = 8
    for t in range(8, cap + 1, 8):
        if rows % t == 0:
            best = t
    return best


def _whole(shape):
    return pl.BlockSpec(shape, lambda i, _n=len(shape): (0,) * _n)


def _matmul(a, b, *, name, ta=False, tb=False, tm=1024, tn=1024, tk=4096, out_dtypes=(F32,),
            a_fn=None, epi=None, epi_ins=(), col_shards=1):
    M, K = (a.shape[1], a.shape[0]) if ta else a.shape
    N = b.shape[0] if tb else b.shape[1]
    tm, tn, tk = min(tm, M), min(tn, N), min(tk, K)
    assert M % tm == 0 and N % tn == 0 and K % tk == 0, (name, M, N, K)
    nk = K // tk
    ne = len(epi_ins)
    dn = (((0 if ta else 1,), (1 if tb else 0,)), ((), ()))

    def body(a_ref, b_ref, *rest):
        e_refs, o_refs = rest[:ne], rest[ne:ne + len(out_dtypes)]
        at = a_ref[...]
        if a_fn is not None:
            at = a_fn(at)
        part = lax.dot_general(at.astype(BF16), b_ref[...].astype(BF16), dn, preferred_element_type=F32)

        def finish(total):
            outs = (total,) if epi is None else epi(total, *[r[...] for r in e_refs])
            for r, o in zip(o_refs, outs):
                r[...] = o.astype(r.dtype)

        if nk == 1:
            finish(part)
            return
        acc = rest[-1]
        k = pl.program_id(2)

        @pl.when(k == 0)
        def _():
            acc[...] = part

        @pl.when(k > 0)
        def _():
            acc[...] += part

        @pl.when(k == nk - 1)
        def _():
            finish(acc[...])

    a_spec = pl.BlockSpec((tk, tm), lambda i, j, k: (k, i)) if ta else pl.BlockSpec((tm, tk), lambda i, j, k: (i, k))
    b_spec = pl.BlockSpec((tn, tk), lambda i, j, k: (j, k)) if tb else pl.BlockSpec((tk, tn), lambda i, j, k: (k, j))
    o_spec = pl.BlockSpec((tm, tn), lambda i, j, k: (i, j))
    if col_shards > 1:
        per = N // col_shards // tn
        assert ne == 0 and per * tn * col_shards == N
        w_spec = pl.BlockSpec((None, tm, tn), lambda i, j, k: (j // per, i, j % per))
        o_shape = (col_shards, M, N // col_shards)
    else:
        w_spec, o_shape = o_spec, (M, N)
    outs = pl.pallas_call(
        body, name=name, grid=(M // tm, N // tn, nk),
        in_specs=[a_spec, b_spec] + [o_spec] * ne,
        out_specs=[w_spec] * len(out_dtypes),
        out_shape=[jax.ShapeDtypeStruct(o_shape, d) for d in out_dtypes],
        scratch_shapes=[pltpu.VMEM((tm, tn), F32)] if nk > 1 else [],
        compiler_params=_params("parallel", "parallel", "arbitrary"),
    )(a, b, *epi_ins)
    return outs[0] if len(outs) == 1 else outs


def _rows(fn, rows_in, vecs_in, rows_out, acc_out, *, name, tm=512):
    L = rows_in[0].shape[0]
    tm = min(tm, L)
    assert L % tm == 0
    nr, nv, no, na = len(rows_in), len(vecs_in), len(rows_out), len(acc_out)

    sub = ROW_SUB if tm % ROW_SUB == 0 else tm

    def body(*refs):
        rin, vin = refs[:nr], refs[nr:nr + nv]
        rout, aout = refs[nr + nv:nr + nv + no], refs[nr + nv + no:]
        if na:
            @pl.when(pl.program_id(0) == 0)
            def _():
                for r in aout:
                    r[...] = jnp.zeros_like(r)

        vecs = [v[...] for v in vin]
        sums = None
        for s in range(tm // sub):
            rows = pl.ds(s * sub, sub)
            outs = fn(*[r[rows, :] for r in rin], *vecs)
            for r, o in zip(rout, outs[:no]):
                r[rows, :] = o.astype(r.dtype)
            sums = list(outs[no:]) if sums is None else [t + o for t, o in zip(sums, outs[no:])]
        for r, t in zip(aout, sums):
            r[...] += t

    outs = pl.pallas_call(
        body, name=name, grid=(L // tm,),
        in_specs=[pl.BlockSpec((tm, r.shape[1]), lambda i: (i, 0)) for r in rows_in] + [_whole(v.shape) for v in vecs_in],
        out_specs=[pl.BlockSpec((tm, c), lambda i: (i, 0)) for c, _ in rows_out] + [_whole(s) for s in acc_out],
        out_shape=[jax.ShapeDtypeStruct((L, c), d) for c, d in rows_out] + [jax.ShapeDtypeStruct(s, F32) for s in acc_out],
        compiler_params=_params("arbitrary"),
    )(*rows_in, *vecs_in)
    return outs


def _rsum(t):
    return jnp.sum(t, axis=0, keepdims=True)


def _norm_mod(x, g, sc, sh):
    rs = lax.rsqrt(jnp.mean(x * x, axis=-1, keepdims=True) + EPS)
    return x * rs * g * (1.0 + sc) + sh


def _norm_mod_bwd(x, dh, g, sc):
    rs = lax.rsqrt(jnp.mean(x * x, axis=-1, keepdims=True) + EPS)
    xh = x * rs
    dn = dh * (1.0 + sc)
    dxh = dn * g
    dx = rs * (dxh - xh * jnp.mean(dxh * xh, axis=-1, keepdims=True))
    return dx, _rsum(dh), _rsum(dh * xh * g), _rsum(dn * xh)


def _sigmoid(x):
    return jax.nn.sigmoid(x)


def _gelu(y):
    return jax.nn.gelu(y, approximate=True)


def _gelu_grad(y):
    c = math.sqrt(2.0 / math.pi)
    t = jnp.tanh(c * (y + 0.044715 * y * y * y))
    return 0.5 * (1.0 + t) + 0.5 * y * (1.0 - t * t) * c * (1.0 + 3.0 * 0.044715 * y * y)


def _s5_tables(lb_re, lb_im, bb_re, bb_im, c_re, c_im, seg_len):
    G = lb_re.shape[0]
    nb = G // GPB
    eye = jnp.eye(GPB, dtype=F32)

    def bdiag(t):
        a, b = t.shape[1:]
        t = t.reshape(nb, GPB, a, b)
        return (t[:, :, :, None, :] * eye[None, :, None, :, None]).reshape(nb, GPB * a, GPB * b)

    bbd = jnp.concatenate([bdiag(bb_re.transpose(0, 2, 1)), bdiag(bb_im.transpose(0, 2, 1))], axis=2)
    cbd = jnp.concatenate([bdiag(c_re.transpose(0, 2, 1)), -bdiag(c_im.transpose(0, 2, 1))], axis=1)

    def lanes(re, im):
        t = jnp.concatenate([re.reshape(nb, GPB * S5_P), im.reshape(nb, GPB * S5_P)], axis=1)
        return jnp.repeat(t, NSEG, axis=0)

    tr, ti = lb_re, lb_im
    for _ in range(int(math.log2(seg_len))):
        tr, ti = tr * tr - ti * ti, 2.0 * tr * ti
    return dict(bbd=bbd.astype(BF16), bbdT=bbd.transpose(0, 2, 1).astype(BF16), cbd=cbd.astype(BF16),
                cbdT=cbd.transpose(0, 2, 1).astype(BF16), lam=lanes(lb_re, lb_im), lamT=lanes(tr, ti))


def _s5_untable(acc):
    nb = acc.shape[0]
    t = acc.reshape(nb, GPB, S5_H, 2, GPB, S5_P)
    d = jnp.diagonal(t, axis1=1, axis2=4)
    d = d.transpose(0, 4, 2, 1, 3).reshape(nb * GPB, 2, S5_H, S5_P)
    return d[:, 0], d[:, 1]


S5_R = 256


def _s5_carries(ends, first, t_ref, rws, SW, cfx, *, reverse):
    er, ei = ends
    tr, ti = t_ref[rws, :SW][0:1], t_ref[rws, SW:][0:1]
    cr, ci = first
    order = range(NSEG - 1, -1, -1) if reverse else range(NSEG)
    for n, s in enumerate(order):
        if n > 0:
            p = s + 1 if reverse else s - 1
            if reverse:
                cr, ci = tr * cr + ti * ci + er[p:p + 1], tr * ci - ti * cr + ei[p:p + 1]
            else:
                cr, ci = tr * cr - ti * ci + er[p:p + 1], tr * ci + ti * cr + ei[p:p + 1]
        cfx[s:s + 1, :SW] = cr
        cfx[s:s + 1, SW:] = ci


def _s5_fwd(up, tb, dvec, *, name):
    L, D = up.shape
    R = min(S5_R, L)
    nb, ta, ngb = L // R, R // NSEG, D // 128
    SW = GPB * S5_P
    crows = ngb * NSEG

    def body(u_ref, lam_ref, t_ref, b_ref, c_ref, d_ref, y_ref, z_ref, ck_ref, carry, xbuf2, cfx2):
        @pl.when(pl.program_id(0) == 0)
        def _():
            carry[...] = jnp.zeros_like(carry)

        zero = jnp.zeros((NSEG, SW), F32)
        for g0 in range(0, ngb, 2):
            pair = (g0, g0 + 1)
            xb = [xbuf2.at[g % 4] for g in pair]
            cf = [cfx2.at[g % 4] for g in pair]
            cols = [slice(g * 128, (g + 1) * 128) for g in pair]
            rws = [slice(g * NSEG, (g + 1) * NSEG) for g in pair]
            ug = [u_ref[:, cols[q]] for q in range(2)]
            for q in range(2):
                xb[q][...] = jnp.dot(ug[q].astype(BF16), b_ref[pair[q]], preferred_element_type=F32)
            lam = [(lam_ref[rws[q], :SW], lam_ref[rws[q], SW:]) for q in range(2)]

            def scan(c, store, xb=xb, lam=lam):
                c = list(c)
                for a in range(ta):
                    o = slice(a * NSEG, (a + 1) * NSEG)
                    for q in range(2):
                        (lr, li), (cr, ci) = lam[q], c[q]
                        nr = lr * cr - li * ci + xb[q][o, :SW]
                        ni = lr * ci + li * cr + xb[q][o, SW:]
                        if store:
                            xb[q][o, :SW] = nr
                            xb[q][o, SW:] = ni
                        c[q] = (nr, ni)
                return c

            ends = scan([(zero, zero)] * 2, False)
            for q in range(2):
                prev = (carry[rws[q], :SW][NSEG - 1:NSEG], carry[rws[q], SW:][NSEG - 1:NSEG])
                _s5_carries(ends[q], prev, t_ref, rws[q], SW, cf[q], reverse=False)
                ck_ref[0, rws[q], :] = cf[q][...]
            fin = scan([(cf[q][:, :SW], cf[q][:, SW:]) for q in range(2)], True)
            for q in range(2):
                carry[rws[q], :SW] = fin[q][0]
                carry[rws[q], SW:] = fin[q][1]
            for q in range(2):
                y = (jnp.dot(xb[q][...].astype(BF16), c_ref[pair[q]], preferred_element_type=F32)
                     + d_ref[:, cols[q]] * ug[q])
                y_ref[:, cols[q]] = y
                z_ref[:, cols[q]] = _gelu(y).astype(BF16)

    rowblk = pl.BlockSpec((R, D), lambda i: (i, 0))
    return pl.pallas_call(
        body, name=name, grid=(nb,),
        in_specs=[rowblk, _whole(tb["lam"].shape), _whole(tb["lamT"].shape), _whole(tb["bbd"].shape),
                  _whole(tb["cbd"].shape), _whole(dvec.shape)],
        out_specs=[rowblk, rowblk, pl.BlockSpec((1, crows, 2 * SW), lambda i: (i, 0, 0))],
        out_shape=[jax.ShapeDtypeStruct((L, D), F32), jax.ShapeDtypeStruct((L, D), BF16),
                   jax.ShapeDtypeStruct((nb, crows, 2 * SW), F32)],
        scratch_shapes=[pltpu.VMEM((crows, 2 * SW), F32), pltpu.VMEM((4, R, 2 * SW), F32),
                        pltpu.VMEM((4, NSEG, 2 * SW), F32)],
        compiler_params=_params("arbitrary"),
    )(up, tb["lam"], tb["lamT"], tb["bbd"], tb["cbd"], dvec)


def _s5_bwd(up, y, dz, ck, tb, dvec, *, name):
    L, D = up.shape
    R = min(S5_R, L)
    nb, ta, ngb = L // R, R // NSEG, D // 128
    SW = GPB * S5_P
    crows = ngb * NSEG

    def body(u_ref, y_ref, dz_ref, ck_ref, lam_ref, t_ref, b_ref, bt_ref, ct_ref, d_ref,
             du_ref, db_ref, dc_ref, dl_ref, dd_ref, gcarry, xbuf2, gbuf2, dybuf, cfx2):
        @pl.when(pl.program_id(0) == 0)
        def _():
            gcarry[...] = jnp.zeros_like(gcarry)
            db_ref[...] = jnp.zeros_like(db_ref)
            dc_ref[...] = jnp.zeros_like(dc_ref)
            dl_ref[...] = jnp.zeros_like(dl_ref)
            dd_ref[...] = jnp.zeros_like(dd_ref)

        zero = jnp.zeros((NSEG, SW), F32)
        dybuf[...] = dz_ref[...] * _gelu_grad(y_ref[...])
        for g0 in range(0, ngb, 2):
            pair = (g0, g0 + 1)
            xb = [xbuf2.at[g % 4] for g in pair]
            gbf = [gbuf2.at[g % 4] for g in pair]
            cf = [cfx2.at[g % 4] for g in pair]
            cols = [slice(g * 128, (g + 1) * 128) for g in pair]
            rws = [slice(g * NSEG, (g + 1) * NSEG) for g in pair]
            dyg = [dybuf[:, cols[q]] for q in range(2)]
            ug = [u_ref[:, cols[q]] for q in range(2)]
            lam = [(lam_ref[rws[q], :SW], lam_ref[rws[q], SW:]) for q in range(2)]
            for q in range(2):
                gbf[q][...] = jnp.dot(dyg[q].astype(BF16), ct_ref[pair[q]], preferred_element_type=F32)
                xb[q][0:NSEG, :] = ck_ref[0, rws[q], :]
                xb[q][NSEG:, :] = jnp.dot(ug[q].astype(BF16), b_ref[pair[q]], preferred_element_type=F32)

            c = [(xb[q][0:NSEG, :SW], xb[q][0:NSEG, SW:]) for q in range(2)]
            for a in range(ta):
                o = slice((a + 1) * NSEG, (a + 2) * NSEG)
                for q in range(2):
                    (lr, li), (cr, ci) = lam[q], c[q]
                    nr = lr * cr - li * ci + xb[q][o, :SW]
                    ni = lr * ci + li * cr + xb[q][o, SW:]
                    xb[q][o, :SW] = nr
                    xb[q][o, SW:] = ni
                    c[q] = (nr, ni)

            def rscan(c, store, qs=(0, 1), xb=xb, gbf=gbf, lam=lam):
                c = list(c)
                for a in range(ta - 1, -1, -1):
                    o = slice(a * NSEG, (a + 1) * NSEG)
                    for q in qs:
                        lr, li = lam[q]
                        gr = gbf[q][o, :SW] + lr * c[q][0] + li * c[q][1]
                        gi = gbf[q][o, SW:] - li * c[q][0] + lr * c[q][1]
                        if store:
                            gbf[q][o, :SW] = gr
                            gbf[q][o, SW:] = gi
                            xr, xi = xb[q][o, :SW], xb[q][o, SW:]
                            c[q] = (gr, gi, c[q][2] + gr * xr + gi * xi, c[q][3] + gi * xr - gr * xi)
                        else:
                            c[q] = (gr, gi)
                return c

            gends = rscan([(zero, zero)] * 2, False)
            for q in range(2):
                nxt = (gcarry[rws[q], :SW][0:1], gcarry[rws[q], SW:][0:1])
                _s5_carries(gends[q], nxt, t_ref, rws[q], SW, cf[q], reverse=True)
            fin = [(cf[q][:, :SW], cf[q][:, SW:], zero, zero) for q in range(2)]
            for q in range(2):
                fin = rscan(fin, True, qs=(q,))
            for q in range(2):
                gcarry[rws[q], :SW] = fin[q][0]
                gcarry[rws[q], SW:] = fin[q][1]
                dl_ref[rws[q], :SW] += fin[q][2]
                dl_ref[rws[q], SW:] += fin[q][3]
            gb16 = [gbf[q][...].astype(BF16) for q in range(2)]
            for q in range(2):
                du_ref[:, cols[q]] = (jnp.dot(gb16[q], bt_ref[pair[q]], preferred_element_type=F32)
                                      + d_ref[:, cols[q]] * dyg[q])
            for q in range(2):
                db_ref[pair[q]] += lax.dot_general(ug[q].astype(BF16), gb16[q], TN, preferred_element_type=F32)
            for q in range(2):
                dc_ref[pair[q]] += lax.dot_general(dyg[q].astype(BF16), xb[q][NSEG:, :].astype(BF16), TN,
                                                   preferred_element_type=F32)
                dd_ref[:, cols[q]] += _rsum(dyg[q] * ug[q])

    rev = pl.BlockSpec((R, D), lambda i: (nb - 1 - i, 0))
    acc3 = (ngb, 128, 2 * SW)
    return pl.pallas_call(
        body, name=name, grid=(nb,),
        in_specs=[rev, rev, rev, pl.BlockSpec((1, crows, 2 * SW), lambda i: (nb - 1 - i, 0, 0)),
                  _whole(tb["lam"].shape), _whole(tb["lamT"].shape), _whole(tb["bbd"].shape),
                  _whole(tb["bbdT"].shape), _whole(tb["cbdT"].shape), _whole(dvec.shape)],
        out_specs=[rev, _whole(acc3), _whole(acc3), _whole((crows, 2 * SW)), _whole((1, D))],
        out_shape=[jax.ShapeDtypeStruct((L, D), F32), jax.ShapeDtypeStruct(acc3, F32), jax.ShapeDtypeStruct(acc3, F32),
                   jax.ShapeDtypeStruct((crows, 2 * SW), F32), jax.ShapeDtypeStruct((1, D), F32)],
        scratch_shapes=[pltpu.VMEM((crows, 2 * SW), F32), pltpu.VMEM((4, R + NSEG, 2 * SW), F32),
                        pltpu.VMEM((4, R, 2 * SW), F32), pltpu.VMEM((R, D), F32), pltpu.VMEM((4, NSEG, 2 * SW), F32)],
        compiler_params=_params("arbitrary"),
    )(up, y, dz, ck, tb["lam"], tb["lamT"], tb["bbd"], tb["bbdT"], tb["cbdT"], dvec)


NN = (((1,), (0,)), ((), ()))
TN = (((0,), (0,)), ((), ()))
NT = (((1,), (1,)), ((), ()))


def _dot3(lhs, rhs, dn, split):
    x = rhs if split == "rhs" else lhs
    hi = x.astype(BF16)
    r1 = x - hi.astype(F32)
    mid = r1.astype(BF16)
    lo = (r1 - mid.astype(F32)).astype(BF16)
    out = None
    for part in (hi, mid, lo):
        ops = (lhs, part) if split == "rhs" else (part, rhs)
        t = lax.dot_general(ops[0], ops[1], dn, preferred_element_type=F32)
        out = t if out is None else out + t
    return out


def _log_sigmoid(x):
    return jnp.minimum(x, 0.0) - jnp.log(1.0 + jnp.exp(-jnp.abs(x)))


def _chunk_tri(rows, upper):
    r = lax.broadcasted_iota(jnp.int32, (rows, rows), 0)
    c = lax.broadcasted_iota(jnp.int32, (rows, rows), 1)
    same = (r // CHUNK) == (c // CHUNK)
    return (same & ((c >= r) if upper else (r >= c))).astype(BF16)


def _gla_block_gates(p_ref, wg_ref, bg_ref, QK, wbuf, gebuf):
    RB = p_ref.shape[0]
    glr = p_ref[:, 6 * QK:6 * QK + 128].astype(BF16)
    gpre = jnp.dot(glr, wg_ref[...], preferred_element_type=F32) + bg_ref[...]
    la = _log_sigmoid(gpre) * (1.0 / GATE_TAU)
    gc = _dot3(_chunk_tri(RB, False), la, NN, "rhs")
    for cc in range(RB // CHUNK):
        rows = slice(cc * CHUNK, (cc + 1) * CHUNK)
        ge = gc[(cc + 1) * CHUNK - 1:(cc + 1) * CHUNK, :]
        gebuf[cc:cc + 1, :] = ge
        wbuf[rows, :] = jnp.exp(ge - gc[rows, :])
    return glr, gpre, la


def _as_column(row, lanes):
    t = jnp.transpose(jnp.broadcast_to(row, (row.shape[1], row.shape[1])))
    return jnp.concatenate([t] * (lanes // row.shape[1]), axis=1)


def _as_row(col):
    return jnp.transpose(jnp.broadcast_to(col, (col.shape[0], col.shape[0])))[0:1, :]


def _gla_fwd(proj, wg2p, bg, gn, *, name):
    L = proj.shape[0]
    QK = wg2p.shape[1]
    DK, DV = QK // HEADS, 2 * QK // HEADS
    nC = L // CHUNK
    NB = min(GLA_NB, nC)
    assert nC % NB == 0
    scale = DK ** -0.5

    def body(p_ref, wg_ref, bg_ref, gn_ref, og_ref, s_ref, sst, wbuf, gebuf):
        @pl.when(pl.program_id(0) == 0)
        def _():
            sst[...] = jnp.zeros_like(sst)

        _gla_block_gates(p_ref, wg_ref, bg_ref, QK, wbuf, gebuf)
        heads = range(HEADS)
        ks = [slice(h * DK, (h + 1) * DK) for h in heads]
        vs = [slice(h * DV, (h + 1) * DV) for h in heads]
        units = [(cc, h) for cc in range(NB) for h in heads]
        rows = [slice(cc * CHUNK, (cc + 1) * CHUNK) for cc in range(NB)]
        kv = {(cc, h): lax.dot_general(
            (p_ref[rows[cc], QK + h * DK:QK + (h + 1) * DK] * wbuf[rows[cc], ks[h]]).astype(BF16),
            p_ref[rows[cc], 2 * QK + h * DV:2 * QK + (h + 1) * DV].astype(BF16), TN, preferred_element_type=F32)
            for cc, h in units}
        S16 = {}
        for cc, h in units:
            S = jnp.exp(_as_column(gebuf[cc:cc + 1, ks[h]], DV)) * sst[ks[h], :] + kv[cc, h]
            sst[ks[h], :] = S
            s_ref[cc, ks[h], :] = S
            S16[cc, h] = S.astype(BF16)
        o = {(cc, h): jnp.dot((p_ref[rows[cc], h * DK:(h + 1) * DK] * scale).astype(BF16), S16[cc, h],
                              preferred_element_type=F32) for cc, h in units}
        for cc, h in units:
            r = p_ref[rows[cc], 4 * QK + h * DV:4 * QK + (h + 1) * DV]
            on = o[cc, h] * lax.rsqrt(jnp.mean(o[cc, h] * o[cc, h], axis=-1, keepdims=True) + EPS)
            og_ref[rows[cc], vs[h]] = (on * gn_ref[:, vs[h]] * (r * _sigmoid(r))).astype(BF16)

    RB = NB * CHUNK
    return pl.pallas_call(
        body, name=name, grid=(nC // NB,),
        in_specs=[pl.BlockSpec((RB, proj.shape[1]), lambda i: (i, 0)), _whole(wg2p.shape), _whole(bg.shape), _whole(gn.shape)],
        out_specs=[pl.BlockSpec((RB, 2 * QK), lambda i: (i, 0)), pl.BlockSpec((NB, QK, DV), lambda i: (i, 0, 0))],
        out_shape=[jax.ShapeDtypeStruct((L, 2 * QK), BF16), jax.ShapeDtypeStruct((nC, QK, DV), F32)],
        scratch_shapes=[pltpu.VMEM((QK, DV), F32), pltpu.VMEM((RB, QK), F32), pltpu.VMEM((8, QK), F32)],
        compiler_params=_params("arbitrary"),
    )(proj, wg2p, bg, gn)


def _gla_bwd(proj, dog, states, wg2p, bg, gn, *, name):
    L, W = proj.shape
    QK = wg2p.shape[1]
    DK, DV = QK // HEADS, 2 * QK // HEADS
    nC = L // CHUNK
    NB = min(GLA_NB, nC)
    nB = nC // NB
    scale = DK ** -0.5

    def body(p_ref, dog_ref, sc_ref, sp_ref, wg_ref, bg_ref, gn_ref, dp_ref, dwg_ref, dbg_ref, dgn_ref,
             gst, wbuf, gebuf, dwwbuf, dgebuf):
        i = pl.program_id(0)

        @pl.when(i == 0)
        def _():
            gst[...] = jnp.zeros_like(gst)
            dwg_ref[...] = jnp.zeros_like(dwg_ref)
            dbg_ref[...] = jnp.zeros_like(dbg_ref)
            dgn_ref[...] = jnp.zeros_like(dgn_ref)

        RB = NB * CHUNK
        glr, gpre, _ = _gla_block_gates(p_ref, wg_ref, bg_ref, QK, wbuf, gebuf)
        heads = range(HEADS)
        ks = [slice(h * DK, (h + 1) * DK) for h in heads]
        vs = [slice(h * DV, (h + 1) * DV) for h in heads]
        units = [(cc, h) for cc in range(NB) for h in heads]
        rws = [slice(cc * CHUNK, (cc + 1) * CHUNK) for cc in range(NB)]
        qs16 = {(cc, h): (p_ref[rws[cc], h * DK:(h + 1) * DK] * scale).astype(BF16) for cc, h in units}
        S16a = {(cc, h): sc_ref[cc, ks[h], :].astype(BF16) for cc, h in units}
        oa = {u: jnp.dot(qs16[u], S16a[u], preferred_element_type=F32) for u in units}
        doa = {}
        for cc, h in units:
            r = p_ref[rws[cc], 4 * QK + h * DV:4 * QK + (h + 1) * DV]
            o = oa[cc, h]
            rs = lax.rsqrt(jnp.mean(o * o, axis=-1, keepdims=True) + EPS)
            on = o * rs
            sr = _sigmoid(r)
            dg = dog_ref[rws[cc], vs[h]]
            gnh = gn_ref[:, vs[h]]
            dp_ref[rws[cc], 4 * QK + h * DV:4 * QK + (h + 1) * DV] = (
                dg * on * gnh * (sr * (1.0 + r * (1.0 - sr)))).astype(BF16)
            dt = dg * (r * sr)
            dgn_ref[:, vs[h]] += _rsum(dt * on)
            don = dt * gnh
            doa[cc, h] = (rs * (don - on * jnp.mean(don * on, axis=-1, keepdims=True))).astype(BF16)
        dqa = {u: lax.dot_general(doa[u], S16a[u], NT, preferred_element_type=F32) for u in units}
        for cc, h in units:
            dp_ref[rws[cc], h * DK:(h + 1) * DK] = (dqa[cc, h] * scale).astype(BF16)
        for cc in range(NB - 1, -1, -1):
            rows = rws[cc]
            do = [doa[cc, h] for h in heads]
            Gc = [gst[ks[h], :] + lax.dot_general(qs16[cc, h], do[h], TN, preferred_element_type=F32) for h in heads]
            G16 = [g.astype(BF16) for g in Gc]
            kd = [p_ref[rows, QK + h * DK:QK + (h + 1) * DK] * wbuf[rows, ks[h]] for h in heads]
            dkd = [lax.dot_general(p_ref[rows, 2 * QK + h * DV:2 * QK + (h + 1) * DV].astype(BF16), G16[h], NT,
                                   preferred_element_type=F32) for h in heads]
            dv = [jnp.dot(kd[h].astype(BF16), G16[h], preferred_element_type=F32) for h in heads]
            for h in heads:
                if cc > 0:
                    Sp = sc_ref[cc - 1, ks[h], :]
                else:
                    Sp = jnp.where(i < nB - 1, sp_ref[0, ks[h], :], 0.0)
                dp_ref[rows, 2 * QK + h * DV:2 * QK + (h + 1) * DV] = dv[h].astype(BF16)
                ge = gebuf[cc:cc + 1, ks[h]]
                gst[ks[h], :] = jnp.exp(_as_column(ge, DV)) * Gc[h]
                ddec = _as_row(jnp.sum(Gc[h] * Sp, axis=1, keepdims=True))
                dp_ref[rows, QK + h * DK:QK + (h + 1) * DK] = (dkd[h] * wbuf[rows, ks[h]]).astype(BF16)
                dww = dkd[h] * kd[h]
                dwwbuf[rows, ks[h]] = dww
                dgebuf[cc:cc + 1, ks[h]] = jnp.exp(ge) * ddec + _rsum(dww)
        rev = _dot3(_chunk_tri(RB, True), dwwbuf[...], NN, "rhs")
        for cc in range(NB):
            rows = slice(cc * CHUNK, (cc + 1) * CHUNK)
            wbuf[rows, :] = dgebuf[cc:cc + 1, :] - rev[rows, :]
        dgpre = wbuf[...] * (1.0 / GATE_TAU) * (1.0 - _sigmoid(gpre))
        d16 = dgpre.astype(BF16)
        dp_ref[:, 6 * QK:6 * QK + 128] = lax.dot_general(d16, wg_ref[...], NT, preferred_element_type=F32).astype(BF16)
        dwg_ref[...] += lax.dot_general(glr, d16, TN, preferred_element_type=F32)
        dbg_ref[...] += _rsum(dgpre)

    RB = NB * CHUNK
    rev_idx = lambda i: (nB - 1 - i, 0)
    return pl.pallas_call(
        body, name=name, grid=(nB,),
        in_specs=[pl.BlockSpec((RB, W), rev_idx), pl.BlockSpec((RB, 2 * QK), rev_idx),
                  pl.BlockSpec((NB, QK, DV), lambda i: (nB - 1 - i, 0, 0)),
                  pl.BlockSpec((1, QK, DV), lambda i: (jnp.maximum(NB * (nB - 1 - i) - 1, 0), 0, 0)),
                  _whole(wg2p.shape), _whole(bg.shape), _whole(gn.shape)],
        out_specs=[pl.BlockSpec((RB, W), rev_idx), _whole((128, QK)), _whole((1, QK)), _whole((1, 2 * QK))],
        out_shape=[jax.ShapeDtypeStruct((L, W), BF16), jax.ShapeDtypeStruct((128, QK), F32),
                   jax.ShapeDtypeStruct((1, QK), F32), jax.ShapeDtypeStruct((1, 2 * QK), F32)],
        scratch_shapes=[pltpu.VMEM((QK, DV), F32), pltpu.VMEM((RB, QK), F32), pltpu.VMEM((8, QK), F32),
                        pltpu.VMEM((RB, QK), F32), pltpu.VMEM((8, QK), F32)],
        compiler_params=_params("arbitrary"),
    )(proj, dog, states, states, wg2p, bg, gn)


def _coords():
    return lax.axis_index("x"), lax.axis_index("y"), lax.axis_index("c")


def _other_chips(x, y):
    return [(1 - x, y, 2 * (1 - x) + y), (x, 1 - y, 2 * x + 1 - y), (1 - x, 1 - y, 2 * (1 - x) + 1 - y)]


def _hbm_call(body, ins, out_shapes, n_sems, *, name, alias=False):
    any_spec = pl.BlockSpec(memory_space=pl.ANY)
    return pl.pallas_call(
        body, name=name, in_specs=[any_spec] * len(ins), out_specs=[any_spec] * len(out_shapes), out_shape=out_shapes,
        scratch_shapes=[pltpu.SemaphoreType.DMA((n,)) for n in n_sems],
        input_output_aliases={k: k for k in range(len(ins))} if alias else {},
    )(*ins)


def _exchange(src, masks, *, name):
    vary = [any(m[k] for m in masks) for k in range(3)]
    nslots = 2 ** sum(vary)
    n = len(masks)

    def slot(coords):
        s = 0
        for k in range(3):
            if vary[k]:
                s = s * 2 + coords[k]
        return s

    def body(src_ref, dst_ref, send_sems, recv_sems, loc_sem):
        me = _coords()
        mine = slot(me)
        loc = pltpu.make_async_copy(src_ref, dst_ref.at[mine], loc_sem.at[0])
        loc.start()
        copies = []
        for k, m in enumerate(masks):
            peer = tuple(1 - me[d] if m[d] else me[d] for d in range(3))
            cp = pltpu.make_async_remote_copy(src_ref=src_ref, dst_ref=dst_ref.at[mine], send_sem=send_sems.at[k],
                                              recv_sem=recv_sems.at[k], device_id=peer, device_id_type=MESH)
            cp.start()
            copies.append(cp)
        for cp in copies:
            cp.wait()
        loc.wait()

    return _hbm_call(body, [src], [jax.ShapeDtypeStruct((nslots,) + tuple(src.shape), src.dtype)], (n, n, 1), name=name)[0]


def _cast_into(t, lead, kind, chip, *, name, tm=256):
    r, cc = t.shape[-2:]
    tm = min(tm, r)
    nblk = r // tm
    if kind == "col":
        shp, o_spec = (r, NCH * cc), pl.BlockSpec((tm, cc), lambda i, s: (i, s[0]))
    elif kind == "row":
        shp, o_spec = (NCH * r, cc), pl.BlockSpec((tm, cc), lambda i, s: (s[0] * nblk + i, 0))
    else:
        shp, o_spec = (NCH, r, cc), pl.BlockSpec((None, tm, cc), lambda i, s: (s[0], i, 0))

    def body(s_ref, t_ref, o_ref):
        o_ref[...] = t_ref[...].astype(o_ref.dtype)

    return pl.pallas_call(
        body, name=name,
        grid_spec=pltpu.PrefetchScalarGridSpec(
            num_scalar_prefetch=1, grid=(nblk,),
            in_specs=[pl.BlockSpec((None, tm, cc), lambda i, s: (lead, i, 0))], out_specs=o_spec),
        out_shape=jax.ShapeDtypeStruct(shp, BF16), compiler_params=_params("parallel"),
    )(chip.reshape(1).astype(jnp.int32), t)


def _gather_weights(arrs, shard_shapes, kinds, *, name):
    n = len(arrs)

    def body(*refs):
        dst = refs[n:2 * n]
        send_sems, recv_sems = refs[2 * n:]
        x, y, c = _coords()
        chip = 2 * x + y
        others = _other_chips(x, y)
        sib = (x, y, 1 - c)

        def window(p, chip_id, cc):
            r, cols = shard_shapes[p]
            h = r // 2
            if kinds[p] == "col":
                return dst[p].at[pl.ds(cc * h, h), pl.ds(pl.multiple_of(chip_id * cols, 128), cols)]
            if kinds[p] == "row":
                return dst[p].at[pl.ds(chip_id * r + cc * h, h), :]
            return dst[p].at[chip_id, pl.ds(cc * h, h), :]

        def copy(p, k, win, to):
            return pltpu.make_async_remote_copy(src_ref=win, dst_ref=win, send_sem=send_sems.at[6 * p + k],
                                                recv_sem=recv_sems.at[6 * p + k], device_id=to, device_id_type=MESH)

        sends = []
        for p in range(n):
            for j, (ox, oy, _) in enumerate(others):
                cp = copy(p, j, window(p, chip, c), (ox, oy, c))
                cp.start()
                sends.append(cp)
        for j, (_, _, oc) in enumerate(others):
            for p in range(n):
                copy(p, j, window(p, oc, c), (x, y, c)).wait_recv()
                fw = copy(p, 3 + j, window(p, oc, c), sib)
                fw.start()
                sends.append(fw)
        for p in range(n):
            for j, (_, _, oc) in enumerate(others):
                copy(p, 3 + j, window(p, oc, 1 - c), sib).wait_recv()
        for cp in sends:
            cp.wait_send()

    outs = [jax.ShapeDtypeStruct(a.shape, a.dtype) for a in arrs]
    return _hbm_call(body, arrs, outs, (6 * n, 6 * n), name=name, alias=True)


HBM_SPEC = pl.BlockSpec(memory_space=pltpu.HBM)
SEM_SPEC = pl.BlockSpec(memory_space=pltpu.SEMAPHORE)
EFFECT = pltpu.SideEffectType.DATAFLOW_SIDE_EFFECTING


def _window(ref, shard_shape, kind, chip_id, cc):
    r, cols = shard_shape
    h = r // 2
    if kind == "col":
        return ref.at[pl.ds(cc * h, h), pl.ds(pl.multiple_of(chip_id * cols, 128), cols)]
    if kind == "row":
        return ref.at[pl.ds(chip_id * r + cc * h, h), :]
    return ref.at[chip_id, pl.ds(cc * h, h), :]


def _split_start(start, arrs, n_sems, *, name):
    n, ns = len(arrs), len(n_sems)

    def body(*refs):
        start(refs[:n], refs[n:n + ns])
        refs[-1][...] = jnp.zeros_like(refs[-1])

    outs = pl.pallas_call(
        body, name=name,
        out_shape=tuple([pltpu.SemaphoreType.DMA((k,)) for k in n_sems] + [pltpu.HBM(a.shape, a.dtype) for a in arrs]
                        + [jax.ShapeDtypeStruct((8, 128), F32)]),
        in_specs=[HBM_SPEC] * n, out_specs=tuple([SEM_SPEC] * ns + [HBM_SPEC] * n + [pl.BlockSpec(memory_space=pltpu.VMEM)]),
        input_output_aliases={k: ns + k for k in range(n)},
        compiler_params=pltpu.CompilerParams(has_side_effects=EFFECT),
    )(*[pltpu.with_memory_space_constraint(a, pltpu.HBM) for a in arrs])
    return list(outs[:ns]), list(outs[ns:ns + n]), outs[-1]


def _split_wait(wait, arrs, sems, after, *, name):
    n, ns = len(arrs), len(sems)

    def body(*refs):
        wait(refs[:n], refs[n:n + ns])

    return pl.pallas_call(
        body, name=name, out_shape=tuple(pltpu.HBM(a.shape, a.dtype) for a in arrs),
        in_specs=[HBM_SPEC] * n + [SEM_SPEC] * ns + [pl.BlockSpec(memory_space=pl.ANY)], out_specs=tuple([HBM_SPEC] * n),
        input_output_aliases={k: k for k in range(n)},
        compiler_params=pltpu.CompilerParams(has_side_effects=EFFECT),
    )(*arrs, *sems, after)


def _gw_copies(refs, send_sems, recv_sems, shard_shapes, kinds, outgoing):
    x, y, c = _coords()
    chip = 2 * x + y
    out = []
    for p in range(len(refs)):
        for j, (ox, oy, oc) in enumerate(_other_chips(x, y)):
            win = _window(refs[p], shard_shapes[p], kinds[p], chip if outgoing else oc, c)
            out.append(pltpu.make_async_remote_copy(
                src_ref=win, dst_ref=win, send_sem=send_sems.at[3 * p + j], recv_sem=recv_sems.at[3 * p + j],
                device_id=(ox, oy, c), device_id_type=MESH))
    return out


def _gw_start(arrs, shard_shapes, kinds, groups, *, name):
    def start(refs, sems):
        for g, idx in enumerate(groups):
            for cp in _gw_copies([refs[p] for p in idx], sems[2 * g], sems[2 * g + 1], [shard_shapes[p] for p in idx],
                                 [kinds[p] for p in idx], True):
                cp.start()

    n_sems = [3 * len(idx) for idx in groups for _ in range(2)]
    sems, thru, token = _split_start(start, arrs, n_sems, name=name)
    return [(sems[2 * g], sems[2 * g + 1]) for g in range(len(groups))], thru, token


def _gw_wait(arrs, shard_shapes, kinds, sem_pair, after, *, name):
    def wait(refs, sems):
        for cp in _gw_copies(refs, sems[0], sems[1], shard_shapes, kinds, True):
            cp.wait_send()
        for cp in _gw_copies(refs, sems[0], sems[1], shard_shapes, kinds, False):
            cp.wait_recv()

    return _split_wait(wait, arrs, list(sem_pair), after, name=name)


def _gw_forward_copies(refs, shard_shapes, kinds, send_sems, recv_sems, incoming):
    x, y, c = _coords()
    out = []
    for p in range(len(refs)):
        for j, (_, _, oc) in enumerate(_other_chips(x, y)):
            win = _window(refs[p], shard_shapes[p], kinds[p], oc, 1 - c if incoming else c)
            out.append(pltpu.make_async_remote_copy(
                src_ref=win, dst_ref=win, send_sem=send_sems.at[3 * p + j], recv_sem=recv_sems.at[3 * p + j],
                device_id=(x, y, 1 - c), device_id_type=MESH))
    return out


def _gw_forward(arrs, shard_shapes, kinds, *, name):
    n = len(arrs)

    def body(*refs):
        dst, (send_sems, recv_sems) = refs[n:2 * n], refs[2 * n:]
        sends = _gw_forward_copies(dst, shard_shapes, kinds, send_sems, recv_sems, False)
        for cp in sends:
            cp.start()
        for cp in _gw_forward_copies(dst, shard_shapes, kinds, send_sems, recv_sems, True):
            cp.wait_recv()
        for cp in sends:
            cp.wait_send()

    outs = [jax.ShapeDtypeStruct(a.shape, a.dtype) for a in arrs]
    return _hbm_call(body, arrs, outs, (3 * n, 3 * n), name=name, alias=True)


def _gw_forward_start(arrs, shard_shapes, kinds, *, name):
    def start(refs, sems):
        for cp in _gw_forward_copies(refs, shard_shapes, kinds, sems[0], sems[1], False):
            cp.start()

    sems, thru, token = _split_start(start, arrs, [3 * len(arrs)] * 2, name=name)
    return (sems[0], sems[1]), thru, token


def _gw_forward_wait(pair, arrs, shard_shapes, kinds, after, *, name):
    def wait(refs, sems):
        for cp in _gw_forward_copies(refs, shard_shapes, kinds, sems[0], sems[1], False):
            cp.wait_send()
        for cp in _gw_forward_copies(refs, shard_shapes, kinds, sems[0], sems[1], True):
            cp.wait_recv()

    return _split_wait(wait, arrs, list(pair), after, name=name)


def _rs_chips_copies(parts, lands, send_sems, recv_sems):
    x, y, c = _coords()
    chip = 2 * x + y
    out = []
    for p in range(len(parts)):
        for j, (ox, oy, oc) in enumerate(_other_chips(x, y)):
            out.append(pltpu.make_async_remote_copy(
                src_ref=parts[p].at[oc], dst_ref=lands[p].at[chip], send_sem=send_sems.at[3 * p + j],
                recv_sem=recv_sems.at[3 * p + j], device_id=(ox, oy, c), device_id_type=MESH))
    return out


def _rs_chips_start(parts, *, name):
    n = len(parts)

    def start(refs, sems):
        for cp in _rs_chips_copies(refs[:n], refs[n:], sems[0], sems[1]):
            cp.start()

    lands = [lax.empty(t.shape, t.dtype) for t in parts]
    sems, thru, token = _split_start(start, list(parts) + lands, [3 * n, 3 * n], name=name)
    return (sems[0], sems[1]), thru[:n], thru[n:], token


def _rs_chips_wait(groups, after, *, name):
    sizes = [len(g[1]) for g in groups]
    arrs = [a for g in groups for a in list(g[1]) + list(g[2])]
    sems = [s for g in groups for s in g[0]]

    def wait(refs, sem_refs):
        o = 0
        for k, n in enumerate(sizes):
            for cp in _rs_chips_copies(refs[o:o + n], refs[o + n:o + 2 * n], sem_refs[2 * k], sem_refs[2 * k + 1]):
                cp.wait()
            o += 2 * n

    outs = _split_wait(wait, arrs, sems, after, name=name)
    res, o = [], 0
    for n in sizes:
        res.append((list(outs[o:o + n]), list(outs[o + n:o + 2 * n])))
        o += 2 * n
    return res


def _rs_cores_copies(grads, lands, send_sems, recv_sems):
    x, y, c = _coords()
    out, o = [], 0
    for p in range(len(grads)):
        nsh, h = lands[p].shape[0], lands[p].shape[1]
        for j in range(nsh):
            out.append(pltpu.make_async_remote_copy(
                src_ref=grads[p].at[j, pl.ds((1 - c) * h, h), :], dst_ref=lands[p].at[j],
                send_sem=send_sems.at[o + j], recv_sem=recv_sems.at[o + j], device_id=(x, y, 1 - c), device_id_type=MESH))
        o += nsh
    return out


def _rs_cores_start(grads, *, name):
    n = len(grads)
    tot = sum(g.shape[0] for g in grads)

    def start(refs, sems):
        for cp in _rs_cores_copies(refs[:n], refs[n:], sems[0], sems[1]):
            cp.start()

    lands = [lax.empty((g.shape[0], g.shape[1] // 2, g.shape[2]), g.dtype) for g in grads]
    sems, thru, token = _split_start(start, list(grads) + lands, [tot, tot], name=name)
    return (sems[0], sems[1]), thru[:n], thru[n:], token


def _rs_cores_wait(pair, grads, lands, after, *, name):
    n = len(grads)

    def wait(refs, sems):
        for cp in _rs_cores_copies(refs[:n], refs[n:], sems[0], sems[1]):
            cp.wait()

    outs = _split_wait(wait, list(grads) + list(lands), list(pair), after, name=name)
    return list(outs[:n]), list(outs[n:])


def _rs_cores(grads, *, name):
    n = len(grads)
    outs = [jax.ShapeDtypeStruct((g.shape[0], g.shape[1] // 2, g.shape[2]), g.dtype) for g in grads]

    def body(*refs):
        src, dst = refs[:n], refs[n:2 * n]
        send_sems, recv_sems = refs[2 * n:]
        x, y, c = _coords()
        copies = []
        for p in range(n):
            nsh, r, _ = grads[p].shape
            h = r // 2
            for j in range(nsh):
                cp = pltpu.make_async_remote_copy(
                    src_ref=src[p].at[j, pl.ds((1 - c) * h, h), :], dst_ref=dst[p].at[j],
                    send_sem=send_sems.at[nsh * p + j], recv_sem=recv_sems.at[nsh * p + j],
                    device_id=(x, y, 1 - c), device_id_type=MESH)
                cp.start()
                copies.append(cp)
        for cp in copies:
            cp.wait()

    tot = sum(g.shape[0] for g in grads)
    return _hbm_call(body, grads, outs, (tot, tot), name=name)


def _sum_own_half(full, recv, ci, out_dtype, *, name):
    nsh, h, cols = recv.shape
    tm = h if nsh * h * cols * 4 <= (2 << 20) else _tile_rows(h, 256)
    nblk = h // tm

    def body(c_ref, f_ref, r_ref, o_ref):
        o_ref[...] = (f_ref[...] + r_ref[...]).astype(o_ref.dtype)

    return pl.pallas_call(
        body, name=name,
        grid_spec=pltpu.PrefetchScalarGridSpec(
            num_scalar_prefetch=1, grid=(nsh, nblk),
            in_specs=[pl.BlockSpec((1, tm, cols), lambda j, i, c_ref: (j, c_ref[0] * nblk + i, 0)),
                      pl.BlockSpec((1, tm, cols), lambda j, i, c_ref: (j, i, 0))],
            out_specs=pl.BlockSpec((1, tm, cols), lambda j, i, c_ref: (j, i, 0))),
        out_shape=jax.ShapeDtypeStruct((nsh, h, cols), out_dtype), compiler_params=_params("parallel", "parallel"),
    )(ci.reshape(1).astype(jnp.int32), full, recv)


def _sum_chips(recv, own, chip, ci, *, name, nlead=1, lead=0, prev=None, spread=False):
    nsh, h, cols = recv.shape
    tm = h if nsh * h * cols * 4 <= (2 << 20) else _tile_rows(h, 256)
    nblk = h // tm
    rows_out = 2 * h * (nsh if spread else 1)

    def body(s_ref, r_ref, o_ref, *rest):
        out_ref = rest[-1]
        t = None
        for s in range(nsh):
            v = jnp.where(s_ref[0] == s, o_ref[s], r_ref[s]).astype(F32)
            t = v if t is None else t + v
        out_ref[...] = t

    def out_idx(i, s):
        return (lead, (s[0] * 2 * nblk if spread else 0) + s[1] * nblk + i, 0)

    blk = pl.BlockSpec((nsh, tm, cols), lambda i, s: (0, i, 0))
    ins = [recv, own] + ([prev] if prev is not None else [])
    return pl.pallas_call(
        body, name=name,
        grid_spec=pltpu.PrefetchScalarGridSpec(
            num_scalar_prefetch=1, grid=(nblk,),
            in_specs=[blk, blk] + ([pl.BlockSpec(memory_space=pl.ANY)] if prev is not None else []),
            out_specs=pl.BlockSpec((None, tm, cols), out_idx)),
        out_shape=jax.ShapeDtypeStruct((nlead, rows_out, cols), F32),
        input_output_aliases={3: 0} if prev is not None else {},
        compiler_params=_params("arbitrary"),
    )(jnp.stack([chip, ci]).astype(jnp.int32), *ins)


RS_GATHER_CHUNKS = 4


def _rs_gather_copies(refs, nleads, halves, spread, send_sems, recv_sems, incoming):
    x, y, c = _coords()
    chip = 2 * x + y
    out, sem = [], 0
    for p in range(len(refs)):
        h = halves[p]
        q = h // RS_GATHER_CHUNKS
        base = (chip * 2 * h if spread[p] else 0) + (1 - c if incoming else c) * h
        for l in range(nleads[p]):
            for k in range(RS_GATHER_CHUNKS):
                win = refs[p].at[l, pl.ds(base + k * q, q), :]
                out.append(pltpu.make_async_remote_copy(src_ref=win, dst_ref=win, send_sem=send_sems.at[sem],
                                                        recv_sem=recv_sems.at[sem], device_id=(x, y, 1 - c),
                                                        device_id_type=MESH))
                sem += 1
    return out


def _rs_gather_start(arrs, halves, spread, *, name):
    nleads = [a.shape[0] for a in arrs]
    tot = sum(nleads) * RS_GATHER_CHUNKS

    def start(refs, sems):
        for cp in _rs_gather_copies(refs, nleads, halves, spread, sems[0], sems[1], False):
            cp.start()

    sems, thru, token = _split_start(start, arrs, [tot, tot], name=name)
    return (sems[0], sems[1]), thru, token


def _rs_gather_wait(pair, arrs, halves, spread, after, *, name):
    nleads = [a.shape[0] for a in arrs]

    def wait(refs, sems):
        for cp in _rs_gather_copies(refs, nleads, halves, spread, sems[0], sems[1], False):
            cp.wait_send()
        for cp in _rs_gather_copies(refs, nleads, halves, spread, sems[0], sems[1], True):
            cp.wait_recv()

    return _split_wait(wait, arrs, list(pair), after, name=name)


def _adamw(w, g, m, v, *, name):
    nl, R, C = w.shape
    tm = _tile_rows(R, 256)

    blk = pl.BlockSpec((None, tm, C), lambda l, i: (l, i, 0))
    return pl.pallas_call(
        _adamw_body_copy(), name=name, grid=(nl, R // tm), in_specs=[blk] * 4, out_specs=[blk] * 3,
        out_shape=[jax.ShapeDtypeStruct((nl, R, C), F32)] * 3, compiler_params=_params("parallel", "parallel"),
    )(w, g, m, v)


def _adamw_body(w_ref, g_ref, m_ref, v_ref, d_ref, nm_ref, nv_ref):
    gg = g_ref[...]
    nm = B1 * m_ref[...] + (1.0 - B1) * gg
    nv = B2 * v_ref[...] + (1.0 - B2) * (gg * gg)
    m_hat = nm / (1.0 - B1 ** ASTEP)
    v_hat = nv / (1.0 - B2 ** ASTEP)
    d_ref[...] = -LR * (m_hat / (jnp.sqrt(v_hat) + AEPS) + WD * w_ref[...])
    nm_ref[...] = nm
    nv_ref[...] = nv


def _adamw_whole(w, g, m, v, *, name):
    return pl.pallas_call(_adamw_body_copy(), name=name, out_shape=[jax.ShapeDtypeStruct(w.shape, F32)] * 3,
                          compiler_params=_params())(w, g, m, v)


def _adamw_body_copy():
    def body(*refs):
        _adamw_body(*refs)
    return body


def _mod_cols(c_all, w_ada, b_cols, *, name):
    nl, D, cols = w_ada.shape
    B = c_all.shape[0]

    def body(c_ref, w_ref, b_ref, o_ref):
        cc = c_ref[...]
        cs = (cc * _sigmoid(cc)).astype(BF16)
        o_ref[0] = jnp.dot(cs, w_ref[0].astype(BF16), preferred_element_type=F32) + b_ref[0]

    return pl.pallas_call(
        body, name=name, grid=(nl,),
        in_specs=[_whole(c_all.shape), pl.BlockSpec((1, D, cols), lambda i: (i, 0, 0)), pl.BlockSpec((1, 1, cols), lambda i: (i, 0, 0))],
        out_specs=pl.BlockSpec((1, B, cols), lambda i: (i, 0, 0)),
        out_shape=jax.ShapeDtypeStruct((nl, B, cols), F32), compiler_params=_params("arbitrary"),
    )(c_all, w_ada, b_cols)


def _ada_grad(c_all, dmod_cols, *, name):
    nl, B, cols = dmod_cols.shape
    D = c_all.shape[1]

    def body(c_ref, d_ref, o_ref):
        cc = c_ref[...]
        cs = (cc * _sigmoid(cc)).astype(BF16)
        o_ref[0] = lax.dot_general(cs, d_ref[0].astype(BF16), TN, preferred_element_type=F32)

    return pl.pallas_call(
        body, name=name, grid=(nl,),
        in_specs=[_whole(c_all.shape), pl.BlockSpec((1, B, cols), lambda i: (i, 0, 0))],
        out_specs=pl.BlockSpec((1, D, cols), lambda i: (i, 0, 0)),
        out_shape=jax.ShapeDtypeStruct((nl, D, cols), F32), compiler_params=_params("arbitrary"),
    )(c_all, dmod_cols)


def _s5_disc(a_re, a_im, log_dt, b_re, b_im):
    dt = jnp.exp(log_dt)[:, None]
    mag = jnp.exp(a_re * dt)
    ph = a_im * dt
    lb_re = mag * jnp.cos(ph)
    lb_im = mag * jnp.sin(ph)
    den = a_re * a_re + a_im * a_im
    nr = lb_re - 1.0
    ni = lb_im
    f_re = (nr * a_re + ni * a_im) / den
    f_im = (ni * a_re - nr * a_im) / den
    bb_re = f_re[..., None] * b_re - f_im[..., None] * b_im
    bb_im = f_re[..., None] * b_im + f_im[..., None] * b_re
    return lb_re, lb_im, bb_re, bb_im


def _to_segments(t):
    L, D = t.shape
    R = min(S5_R, L)
    return t.reshape(L // R, NSEG, R // NSEG, D).transpose(0, 2, 1, 3).reshape(L, D)


def _from_segments(t):
    L, D = t.shape
    R = min(S5_R, L)
    return t.reshape(L // R, R // NSEG, NSEG, D).transpose(0, 2, 1, 3).reshape(L, D)


def _w_in_layout(QK, D, ncols):
    segs = [(0, 4 * QK, 0), (4 * QK + GATE_RANK, 4 * QK + GATE_RANK + D, 4 * QK), (4 * QK, 4 * QK + GATE_RANK, 4 * QK + D)]
    fwd = []
    for lo, hi, _ in segs:
        col = lo
        while col < hi:
            j = col // ncols
            end = min(hi, (j + 1) * ncols)
            fwd.append((j, col - j * ncols, end - j * ncols))
            col = end
    bwd = []
    for j in range(NCH):
        ranges, col = [], j * ncols
        while col < (j + 1) * ncols:
            lo, hi, rlo = next(sg for sg in segs if sg[0] <= col < sg[1])
            end = min((j + 1) * ncols, hi)
            ranges.append((rlo + col - lo, rlo + end - lo))
            col = end
        bwd.append(ranges)
    return fwd, bwd


def _mlp_fwd(h2, w1, w2, tag):
    a = _matmul(h2, w1, name=f"ff1_{tag}", tn=2048, out_dtypes=(BF16,), epi=lambda acc: (jnp.maximum(acc, 0.0),))
    f = _matmul(a, w2, name=f"ff2_{tag}", a_fn=jnp.square)
    return a, f


def _mlp_bwd(df, h2, a, w1, w2, tag):
    da = _matmul(df, w2, tb=True, name=f"ff2_dx_{tag}", tn=2048, out_dtypes=(BF16,), epi_ins=(a,),
                 epi=lambda acc, at: (acc * (2.0 * at.astype(F32)),))
    dw2 = _matmul(a, df, ta=True, name=f"ff2_dw_{tag}", a_fn=jnp.square)
    dh2 = _matmul(da, w1, tb=True, name=f"ff1_dx_{tag}")
    dw1 = _matmul(h2, da, ta=True, name=f"ff1_dw_{tag}", col_shards=NCH)
    return dh2, dw1, dw2


def kernel(x, c, w_ada, b_ada, norm_mix, norm_mlp, s5_a_re, s5_a_im, s5_log_dt, s5_b_re, s5_b_im, s5_c_re, s5_c_im, s5_d, s5_w_glu, gla_w_in, gla_w_gate2, gla_b_gate, gla_g_norm, gla_w_out, w_ff1, w_ff2, norm_final, loss_target, m_w_ada, m_b_ada, m_norm_mix, m_norm_mlp, m_s5_a_re, m_s5_a_im, m_s5_log_dt, m_s5_b_re, m_s5_b_im, m_s5_c_re, m_s5_c_im, m_s5_d, m_s5_w_glu, m_gla_w_in, m_gla_w_gate2, m_gla_b_gate, m_gla_g_norm, m_gla_w_out, m_w_ff1, m_w_ff2, m_norm_final, v_w_ada, v_b_ada, v_norm_mix, v_norm_mlp, v_s5_a_re, v_s5_a_im, v_s5_log_dt, v_s5_b_re, v_s5_b_im, v_s5_c_re, v_s5_c_im, v_s5_d, v_s5_w_glu, v_gla_w_in, v_gla_w_gate2, v_gla_b_gate, v_gla_g_norm, v_gla_w_out, v_w_ff1, v_w_ff2, v_norm_final):
    args = dict(locals())
    L, D = x.shape[1], x.shape[2]
    QK = D // 2
    xi, yi, ci = _coords()
    chip = 2 * xi + yi
    dev = 2 * chip + ci

    cat = jnp.concatenate([gla_w_gate2[0].reshape(1, -1), gla_b_gate, gla_g_norm], axis=1)
    first = _exchange(jnp.concatenate([c.reshape(8, D // 8), jnp.tile(cat, (8, 1))], axis=1), MASK_ALL, name="gather_c")
    c_all = first[:, :, :D // 8].reshape(8, D)
    cat_all = first[0::2, 0, D // 8:]
    acols = w_ada.shape[2]
    b_cols = lax.dynamic_slice_in_dim(b_ada, chip * acols, acols, axis=1)[:, None, :]
    mod_cols = _mod_cols(c_all, w_ada, b_cols, name="ada_mod")
    mod_all = _exchange(mod_cols.reshape(16, acols), MASK_CHIPS, name="gather_mod")
    mod_all = mod_all.reshape(NCH, 2, 8, acols).transpose(1, 2, 0, 3).reshape(2, 8, NCH * acols)
    mod = lax.dynamic_index_in_dim(mod_all, dev, axis=1, keepdims=False).reshape(2, 6, 1, D)

    big = [("s5_w_glu", s5_w_glu, 0, "col"), ("gla_w_in", gla_w_in, 0, "slot"), ("gla_w_out", gla_w_out, 0, "row"),
           ("w_ff1_0", w_ff1, 0, "col"), ("w_ff1_1", w_ff1, 1, "col"), ("w_ff2_0", w_ff2, 0, "row"), ("w_ff2_1", w_ff2, 1, "row")]
    own16 = [_cast_into(t, lead, kind, chip, name=f"cast_{nm}") for nm, t, lead, kind in big]
    wshapes, wkinds = [b[1].shape[-2:] for b in big], [b[3] for b in big]
    wgroups = [[0], [3, 5], [1, 2, 4, 6]]
    wsems, wthru, wtoken = _gw_start(own16, wshapes, wkinds, wgroups, name="gather_w_start")
    W = {}

    def weights_landed(g, after):
        idx = wgroups[g]
        shp, knd = [wshapes[p] for p in idx], [wkinds[p] for p in idx]
        return _gw_wait([wthru[p] for p in idx], shp, knd, wsems[g], after, name=f"gather_w_wait{g}"), shp, knd

    def weights_ready(g, arrs):
        for p, w in zip(wgroups[g], arrs):
            W[big[p][0]] = w

    def forward_start(g, after):
        got, shp, knd = weights_landed(g, after)
        pair, thru, token = _gw_forward_start(got, shp, knd, name=f"gather_w_cores_start{g}")
        return (pair, thru, shp, knd), token

    def forward_finish(g, state, after):
        pair, thru, shp, knd = state
        weights_ready(g, _gw_forward_wait(pair, thru, shp, knd, after, name=f"gather_w_cores_wait{g}"))

    qk4 = QK // NCH
    wg2 = cat_all[:, :GATE_RANK * qk4].reshape(NCH, GATE_RANK, qk4).transpose(1, 0, 2).reshape(GATE_RANK, QK)
    bg = cat_all[:, GATE_RANK * qk4:(GATE_RANK + 1) * qk4].reshape(1, QK)
    gn = cat_all[:, (GATE_RANK + 1) * qk4:].reshape(1, D)
    wg2p = jnp.concatenate([wg2, jnp.zeros((128 - GATE_RANK, QK), F32)], axis=0).astype(BF16)

    lb_re, lb_im, bb_re, bb_im = _s5_disc(s5_a_re[0], s5_a_im[0], s5_log_dt[0], s5_b_re[0], s5_b_im[0])
    tb = _s5_tables(lb_re, lb_im, bb_re, bb_im, s5_c_re[0], s5_c_im[0], min(S5_R, L) // NSEG)
    s5_dv = s5_d + wtoken[0, 0]

    def vec(t):
        return t.reshape(1, -1)

    m0, m1 = mod[0], mod[1]
    xp = _to_segments(x[0])
    (u0,) = _rows(lambda t, g, sc, sh: (_norm_mod(t, g, sc, sh),), [xp], [vec(norm_mix[0]), m0[1], m0[0]],
                  [(D, F32)], [], name="pre_mix0")
    y0, z0, ck0 = _s5_fwd(u0, tb, s5_dv, name="s5_fwd")
    got, shp, knd = weights_landed(0, z0)
    weights_ready(0, _gw_forward(got, shp, knd, name="gather_w_cores0"))
    vg0 = _matmul(z0, W["s5_w_glu"], name="glu", tn=2048)
    fwd1, ftok1 = forward_start(1, vg0)

    def res_glu_pre(xt, vgt, gt, g, sc, sh):
        xn = xt + gt * (vgt[:, :D] * _sigmoid(vgt[:, D:]))
        return xn, _norm_mod(xn, g, sc, sh)

    x2_0, h2_0 = _rows(res_glu_pre, [xp, vg0], [m0[2] + ftok1[0, 0], vec(norm_mlp[0]), m0[4], m0[3]],
                       [(D, F32), (D, BF16)], [], name="res_mix0")
    forward_finish(1, fwd1, h2_0)
    a_0, f0 = _mlp_fwd(h2_0, W["w_ff1_0"], W["w_ff2_0"], "0")
    fwd2, ftok2 = forward_start(2, f0)

    def res_pre(xt, bt, gt, g, sc, sh):
        xn = xt + gt * bt
        return xn, _norm_mod(xn, g, sc, sh)

    x3p, h1p = _rows(res_pre, [x2_0, f0], [m0[5] + ftok2[0, 0], vec(norm_mix[1]), m1[1], m1[0]],
                     [(D, F32), (D, BF16)], [], name="res_mlp0")
    x3, h1 = _from_segments(x3p), _from_segments(h1p)
    forward_finish(2, fwd2, h1)
    in_fwd, in_bwd = _w_in_layout(QK, D, gla_w_in.shape[2])
    w_in_r = jnp.concatenate([W["gla_w_in"][j, :, lo:hi] for j, lo, hi in in_fwd]
                             + [jnp.zeros((D, 128 - GATE_RANK), BF16)], axis=1)
    proj = _matmul(h1, w_in_r, name="gla_in", tm=2048, tn=640)
    og, states = _gla_fwd(proj, wg2p, bg, gn, name="gla_fwd")
    ymix = _matmul(og, W["gla_w_out"], name="gla_out")
    x2_1, h2_1 = _rows(res_pre, [x3, ymix], [m1[2], vec(norm_mlp[1]), m1[4], m1[3]], [(D, F32), (D, BF16)], [],
                       name="res_mix1")
    a_1, f1 = _mlp_fwd(h2_1, W["w_ff1_1"], W["w_ff2_1"], "1")

    def final(xt, ft, tgt, gt, g):
        xn = xt + gt * ft
        rs = lax.rsqrt(jnp.mean(xn * xn, axis=-1, keepdims=True) + EPS)
        xh = xn * rs
        e = xh * g - tgt
        dout = e * (1.0 / D)
        dxh = dout * g
        dx = rs * (dxh - xh * jnp.mean(dxh * xh, axis=-1, keepdims=True))
        lsum = 0.5 * jnp.sum(jnp.sum(e * e, axis=-1, keepdims=True), axis=0, keepdims=True) * (1.0 / D)
        return dx, dx * gt, jnp.broadcast_to(lsum, (1, 128)), _rsum(dout * xh), _rsum(dx * ft)

    dx, df1, loss_part, d_norm_final, dgt2_1 = _rows(
        final, [x2_1, f1, loss_target[0]], [m1[5], vec(norm_final)], [(D, F32), (D, BF16)],
        [(1, 128), (1, D), (1, D)], name="loss_head")

    def gate_bwd(dxt, bt, gt):
        return dxt * gt, _rsum(dxt * bt)

    def norm_bwd(xt, dht, drt, g, sc):
        dxn, dsh, dsc, dg = _norm_mod_bwd(xt, dht, g, sc)
        return drt + dxn, dsh, dsc, dg

    def norm_gate_bwd(xt, dht, drt, bt, g, sc, gt):
        dxn, dsh, dsc, dg = _norm_mod_bwd(xt, dht, g, sc)
        dxt = drt + dxn
        return dxt, dxt * gt, dsh, dsc, dg, _rsum(dxt * bt)

    vD = [(1, D)]
    dh2_1, dw_ff1_1, dw_ff2_1 = _mlp_bwd(df1, h2_1, a_1, W["w_ff1_1"], W["w_ff2_1"], "1")
    dx, dmix1, dsh2_1, dsc2_1, dg_mlp1, dgt1_1 = _rows(
        norm_gate_bwd, [x2_1, dh2_1, dx, ymix], [vec(norm_mlp[1]), m1[4], m1[2]], [(D, F32), (D, BF16)], vD * 4,
        name="norm_mlp1_bwd")
    dog = _matmul(dmix1, W["gla_w_out"], tb=True, name="gla_out_dx")
    dw_out = _matmul(og, dmix1, ta=True, name="gla_out_dw")
    dproj, dwg2p, dbg, dgn = _gla_bwd(proj, dog, states, wg2p, bg, gn, name="gla_bwd")
    dh1 = _matmul(dproj, w_in_r, tb=True, name="gla_in_dx", tk=3200)
    dw_in_r = _matmul(h1, dproj, ta=True, name="gla_in_dw", tn=640)
    dx, dsh1_1, dsc1_1, dg_mix1 = _rows(norm_bwd, [x3, dh1, dx], [vec(norm_mix[1]), m1[1]], [(D, F32)], vD * 3,
                                        name="norm_mix1_bwd")
    dxp = _to_segments(dx)
    tags = [b[0] for b in big] + ["small"]
    rs_groups = []

    def rs_chips_begin(idx, srcs, r1, gname):
        s1 = [_sum_own_half(g, r, ci, F32 if tags[k] == "small" else BF16, name=f"rs_sum_cores_{tags[k]}")
              for g, r, k in zip(srcs, r1, idx)]
        pair, parts, lands, token = _rs_chips_start(s1, name=f"rs_chips_start_{gname}")
        rs_groups.append((idx, pair, parts, lands))
        return token

    def rs_begin(idx, srcs, gname):
        return rs_chips_begin(idx, srcs, _rs_cores(srcs, name=f"rs_cores_{gname}"), gname)

    dw_in = jnp.stack([jnp.concatenate([dw_in_r[:, lo:hi] for lo, hi in in_bwd[j]], axis=1) for j in range(NCH)])
    idx1 = [1, 2, 4, 6]
    pair1, src1, land1, tok1 = _rs_cores_start(
        [dw_in, dw_out.reshape(NCH, -1, D), dw_ff1_1, dw_ff2_1.reshape(NCH, -1, D)], name="rs_cores_start_l1")

    df0, dgt2_0 = _rows(gate_bwd, [dxp, f0], [m0[5] + tok1[0, 0]], [(D, BF16)], vD, name="gate_mlp0")
    dh2_0, dw_ff1_0, dw_ff2_0 = _mlp_bwd(df0, h2_0, a_0, W["w_ff1_0"], W["w_ff2_0"], "0")
    src1, land1 = _rs_cores_wait(pair1, src1, land1, dh2_0, name="rs_cores_wait_l1")
    tok1b = rs_chips_begin(idx1, src1, land1, "l1")
    idx0 = [3, 5]
    pair0, src0, land0, tok0 = _rs_cores_start([dw_ff1_0, dw_ff2_0.reshape(NCH, -1, D)], name="rs_cores_start_l0")
    tok2 = tok1b + tok0

    def norm_glu_bwd(xt, dht, drt, vgt, g, sc, gt):
        dxn, dsh, dsc, dg = _norm_mod_bwd(xt, dht, g, sc)
        dxt = drt + dxn
        val, sg = vgt[:, :D], _sigmoid(vgt[:, D:])
        dbr = dxt * gt
        dvg = jnp.concatenate([dbr * sg, dbr * val * sg * (1.0 - sg)], axis=1)
        return dxt, dvg, dsh, dsc, dg, _rsum(dxt * val * sg)

    dxp, dvg0, dsh2_0, dsc2_0, dg_mlp0, dgt1_0 = _rows(
        norm_glu_bwd, [x2_0, dh2_0, dxp, vg0], [vec(norm_mlp[0]), m0[4] + tok2[0, 0], m0[2]], [(D, F32), (2 * D, BF16)],
        vD * 4, name="norm_mlp0_bwd")
    dz0 = _matmul(dvg0, W["s5_w_glu"], tb=True, name="glu_dx")
    dw_glu = _matmul(z0, dvg0, ta=True, name="glu_dw", tn=512, col_shards=NCH)
    src0, land0 = _rs_cores_wait(pair0, src0, land0, dw_glu, name="rs_cores_wait_l0")
    tok0b = rs_chips_begin(idx0 + [0], src0 + [dw_glu], land0 + list(_rs_cores([dw_glu], name="rs_cores_glu")), "l0")
    du0, db_acc, dc_acc, dl_acc, dd_s5 = _s5_bwd(u0, y0, dz0, ck0, tb, s5_dv + tok0b[0, 0], name="s5_bwd")
    dxp, dsh1_0, dsc1_0, dg_mix0 = _rows(norm_bwd, [xp, du0, dxp], [vec(norm_mix[0]), m0[1]], [(D, F32)], vD * 3,
                                         name="norm_mix0_bwd")
    grad_x = _from_segments(dxp)[None]

    dmod = jnp.concatenate([dsh1_0, dsc1_0, dgt1_0, dsh2_0, dsc2_0, dgt2_0,
                            dsh1_1, dsc1_1, dgt1_1, dsh2_1, dsc2_1, dgt2_1], axis=1)
    dbb_re, dbb_im = _s5_untable(db_acc)
    dc_re, dc_im_neg = _s5_untable(dc_acc)
    nbk = D // 128
    dl = dl_acc.reshape(nbk, NSEG, 2, GPB * S5_P).sum(axis=1)
    smalls = [dmod, dg_mix0, dg_mix1, dg_mlp0, dg_mlp1, d_norm_final, dd_s5, dbg, dgn,
              dwg2p[:GATE_RANK].reshape(1, -1), dbb_re.reshape(1, -1), dbb_im.reshape(1, -1),
              dc_re.reshape(1, -1), dc_im_neg.reshape(1, -1), dl.reshape(1, -1), loss_part]
    ssz = [t.shape[1] for t in smalls]
    stot = sum(ssz)
    spad = -(-stot // 8192) * 8192
    svec = jnp.concatenate(smalls + [jnp.zeros((1, spad - stot), F32)], axis=1).reshape(NCH, spad // (128 * NCH), 128)


    rs_begin([7], [svec], "last")
    landed = _rs_chips_wait([(g[1], g[2], g[3]) for g in rs_groups], grad_x, name="rs_chips_wait")
    s1, r2 = {}, {}
    for (idx, _, _, _), (parts, lands) in zip(rs_groups, landed):
        for k, part, land in zip(idx, parts, lands):
            s1[k], r2[k] = part, land

    def fin(k, **kw):
        return _sum_chips(r2[k], s1[k], chip, ci, name=f"rs_sum_chips_{tags[k]}", **kw)

    f_ff1 = fin(4, nlead=2, lead=1, prev=fin(3, nlead=2, lead=0))
    f_ff2 = fin(6, nlead=2, lead=1, prev=fin(5, nlead=2, lead=0))
    finals = [fin(0), fin(1), fin(2), f_ff1, f_ff2, fin(7, spread=True)]
    halves = [t.shape[1] for t in (s1[0], s1[1], s1[2], s1[3], s1[5], s1[7])]
    gpair, gthru, gtok = _rs_gather_start(finals, halves, [False] * 5 + [True], name="rs_gather_cores_start")

    dmod_all = _exchange((dmod + gtok[0, 0]).reshape(12 * D // 128, 128), MASK_ALL, name="gather_dmod").reshape(8, 2, 6 * D)
    dmod_cols = lax.dynamic_slice_in_dim(dmod_all, chip * acols, acols, axis=2).transpose(1, 0, 2)
    g_w_ada = _ada_grad(c_all, dmod_cols, name="ada_grad")
    upd_w_ada = _adamw(w_ada, g_w_ada, m_w_ada, v_w_ada, name="adamw_w_ada")

    g_glu, g_in, g_out, g_w_ff1, g_w_ff2, s_own = _rs_gather_wait(
        gpair, gthru, halves, [False] * 5 + [True], upd_w_ada[0], name="rs_gather_cores_wait")
    srows = spad // (128 * NCH)
    (s_sum,) = _gather_weights([s_own.reshape(NCH * srows, 128)], [(srows, 128)], ["row"], name="gather_small_grads")
    s_sum = s_sum.reshape(-1)
    so = [sum(ssz[:k]) for k in range(len(ssz))]
    sm = [s_sum[o:o + n] for o, n in zip(so, ssz)]
    (dmod_s, g_mix0, g_mix1, g_mlp0, g_mlp1, g_nf, g_d, g_bg, g_gn, g_wg2, g_bbre, g_bbim, g_cre, g_cimn, g_dl, loss_s) = sm
    loss = loss_s[0]
    g_b_ada = dmod_s.reshape(2, 6 * D)

    G = D // S5_H
    _, disc_vjp = jax.vjp(_s5_disc, s5_a_re[0], s5_a_im[0], s5_log_dt[0], s5_b_re[0], s5_b_im[0])
    g_dl = g_dl.reshape(nbk, 2, GPB, S5_P)
    ct = (g_dl[:, 0].reshape(G, S5_P), g_dl[:, 1].reshape(G, S5_P),
          g_bbre.reshape(G, S5_H, S5_P).transpose(0, 2, 1), g_bbim.reshape(G, S5_H, S5_P).transpose(0, 2, 1))
    g_a_re, g_a_im, g_log_dt, g_b_re, g_b_im = disc_vjp(ct)
    g_c_re = g_cre.reshape(G, S5_H, S5_P)
    g_c_im = -g_cimn.reshape(G, S5_H, S5_P)
    g_wg2_s = lax.dynamic_slice_in_dim(g_wg2.reshape(GATE_RANK, QK), chip * qk4, qk4, axis=1)
    g_bg_s = lax.dynamic_slice_in_dim(g_bg.reshape(1, QK), chip * qk4, qk4, axis=1)
    g_gn_s = lax.dynamic_slice_in_dim(g_gn.reshape(1, D), chip * (D // NCH), D // NCH, axis=1)

    grads = dict(
        w_ada=g_w_ada, b_ada=g_b_ada, norm_mix=jnp.stack([g_mix0, g_mix1]), norm_mlp=jnp.stack([g_mlp0, g_mlp1]),
        s5_a_re=g_a_re[None], s5_a_im=g_a_im[None], s5_log_dt=g_log_dt[None], s5_b_re=g_b_re[None], s5_b_im=g_b_im[None],
        s5_c_re=g_c_re[None], s5_c_im=g_c_im[None], s5_d=g_d[None], s5_w_glu=g_glu,
        gla_w_in=g_in, gla_w_gate2=g_wg2_s[None], gla_b_gate=g_bg_s, gla_g_norm=g_gn_s,
        gla_w_out=g_out, w_ff1=g_w_ff1, w_ff2=g_w_ff2, norm_final=g_nf)

    names = list(grads)
    large = ("w_ada", "s5_w_glu", "gla_w_in", "gla_w_out", "w_ff1", "w_ff2")
    delta, new_m, new_v = {}, {}, {}
    delta["w_ada"], new_m["w_ada"], new_v["w_ada"] = upd_w_ada
    for nm in large[1:]:
        delta[nm], new_m[nm], new_v[nm] = _adamw(args[nm], grads[nm], args["m_" + nm], args["v_" + nm], name=f"adamw_{nm}")
    grads = {nm: grads[nm].reshape(args[nm].shape) for nm in names}
    for nm in names:
        if nm not in large:
            shp = args[nm].shape
            as2d = (1, -1) if len(shp) == 1 else shp
            outs = _adamw_whole(*[t.reshape(as2d) for t in (args[nm], grads[nm], args["m_" + nm], args["v_" + nm])],
                                name=f"adamw_{nm}")
            delta[nm], new_m[nm], new_v[nm] = (t.reshape(shp) for t in outs)
    return (loss, grad_x, *[grads[n] for n in names], *[delta[n] for n in names], *[new_m[n] for n in names],
            *[new_v[n] for n in names])
```

```python
import math

import jax
import jax.numpy as jnp
from jax import lax
from jax.experimental import pallas as pl
from jax.experimental.pallas import tpu as pltpu

F32 = jnp.float32
BF16 = jnp.bfloat16
MESH = pl.DeviceIdType.MESH

EPS = 1e-6
CHUNK = 64
GLA_NB = 4
S5_H = 16
S5_P = 64
GPB = 8
NSEG = 8
HEADS = 4
GATE_RANK = 16
GATE_TAU = 16.0
NCH = 4
LR, B1, B2, AEPS, WD, ASTEP = 0.001, 0.9, 0.999, 1e-08, 0.01, 10
VMEM_LIMIT = 56 << 20
ROW_SUB = 64

MASK_CHIPS = ((1, 0, 0), (0, 1, 0), (1, 1, 0))
MASK_ALL = ((0, 0, 1), (0, 1, 0), (0, 1, 1), (1, 0, 0), (1, 0, 1), (1, 1, 0), (1, 1, 1))


def _params(*sem):
    return pltpu.CompilerParams(dimension_semantics=sem or None, vmem_limit_bytes=VMEM_LIMIT)


def _tile_rows(rows, cap=512):
    best = 8
    for t in range(8, cap + 1, 8):
        if rows % t == 0:
            best = t
    return best


def _whole(shape):
    return pl.BlockSpec(shape, lambda i, _n=len(shape): (0,) * _n)


def _matmul(a, b, *, name, ta=False, tb=False, tm=1024, tn=1024, tk=4096, out_dtypes=(F32,),
            a_fn=None, epi=None, epi_ins=(), col_shards=1):
    M, K = (a.shape[1], a.shape[0]) if ta else a.shape
    N = b.shape[0] if tb else b.shape[1]
    tm, tn, tk = min(tm, M), min(tn, N), min(tk, K)
    assert M % tm == 0 and N % tn == 0 and K % tk == 0, (name, M, N, K)
    nk = K // tk
    ne = len(epi_ins)
    dn = (((0 if ta else 1,), (1 if tb else 0,)), ((), ()))

    def body(a_ref, b_ref, *rest):
        e_refs, o_refs = rest[:ne], rest[ne:ne + len(out_dtypes)]
        at = a_ref[...]
        if a_fn is not None:
            at = a_fn(at)
        part = lax.dot_general(at.astype(BF16), b_ref[...].astype(BF16), dn, preferred_element_type=F32)

        def finish(total):
            outs = (total,) if epi is None else epi(total, *[r[...] for r in e_refs])
            for r, o in zip(o_refs, outs):
                r[...] = o.astype(r.dtype)

        if nk == 1:
            finish(part)
            return
        acc = rest[-1]
        k = pl.program_id(2)

        @pl.when(k == 0)
        def _():
            acc[...] = part

        @pl.when(k > 0)
        def _():
            acc[...] += part

        @pl.when(k == nk - 1)
        def _():
            finish(acc[...])

    a_spec = pl.BlockSpec((tk, tm), lambda i, j, k: (k, i)) if ta else pl.BlockSpec((tm, tk), lambda i, j, k: (i, k))
    b_spec = pl.BlockSpec((tn, tk), lambda i, j, k: (j, k)) if tb else pl.BlockSpec((tk, tn), lambda i, j, k: (k, j))
    o_spec = pl.BlockSpec((tm, tn), lambda i, j, k: (i, j))
    if col_shards > 1:
        per = N // col_shards // tn
        assert ne == 0 and per * tn * col_shards == N
        w_spec = pl.BlockSpec((None, tm, tn), lambda i, j, k: (j // per, i, j % per))
        o_shape = (col_shards, M, N // col_shards)
    else:
        w_spec, o_shape = o_spec, (M, N)
    outs = pl.pallas_call(
        body, name=name, grid=(M // tm, N // tn, nk),
        in_specs=[a_spec, b_spec] + [o_spec] * ne,
        out_specs=[w_spec] * len(out_dtypes),
        out_shape=[jax.ShapeDtypeStruct(o_shape, d) for d in out_dtypes],
        scratch_shapes=[pltpu.VMEM((tm, tn), F32)] if nk > 1 else [],
        compiler_params=_params("parallel", "parallel", "arbitrary"),
    )(a, b, *epi_ins)
    return outs[0] if len(outs) == 1 else outs


def _rows(fn, rows_in, vecs_in, rows_out, acc_out, *, name, tm=512):
    L = rows_in[0].shape[0]
    tm = min(tm, L)
    assert L % tm == 0
    nr, nv, no, na = len(rows_in), len(vecs_in), len(rows_out), len(acc_out)

    sub = ROW_SUB if tm % ROW_SUB == 0 else tm

    def body(*refs):
        rin, vin = refs[:nr], refs[nr:nr + nv]
        rout, aout = refs[nr + nv:nr + nv + no], refs[nr + nv + no:]
        if na:
            @pl.when(pl.program_id(0) == 0)
            def _():
                for r in aout:
                    r[...] = jnp.zeros_like(r)

        vecs = [v[...] for v in vin]
        sums = None
        for s in range(tm // sub):
            rows = pl.ds(s * sub, sub)
            outs = fn(*[r[rows, :] for r in rin], *vecs)
            for r, o in zip(rout, outs[:no]):
                r[rows, :] = o.astype(r.dtype)
            sums = list(outs[no:]) if sums is None else [t + o for t, o in zip(sums, outs[no:])]
        for r, t in zip(aout, sums):
            r[...] += t

    outs = pl.pallas_call(
        body, name=name, grid=(L // tm,),
        in_specs=[pl.BlockSpec((tm, r.shape[1]), lambda i: (i, 0)) for r in rows_in] + [_whole(v.shape) for v in vecs_in],
        out_specs=[pl.BlockSpec((tm, c), lambda i: (i, 0)) for c, _ in rows_out] + [_whole(s) for s in acc_out],
        out_shape=[jax.ShapeDtypeStruct((L, c), d) for c, d in rows_out] + [jax.ShapeDtypeStruct(s, F32) for s in acc_out],
        compiler_params=_params("arbitrary"),
    )(*rows_in, *vecs_in)
    return outs


def _rsum(t):
    return jnp.sum(t, axis=0, keepdims=True)


def _norm_mod(x, g, sc, sh):
    rs = lax.rsqrt(jnp.mean(x * x, axis=-1, keepdims=True) + EPS)
    return x * rs * g * (1.0 + sc) + sh


def _norm_mod_bwd(x, dh, g, sc):
    rs = lax.rsqrt(jnp.mean(x * x, axis=-1, keepdims=True) + EPS)
    xh = x * rs
    dn = dh * (1.0 + sc)
    dxh = dn * g
    dx = rs * (dxh - xh * jnp.mean(dxh * xh, axis=-1, keepdims=True))
    return dx, _rsum(dh), _rsum(dh * xh * g), _rsum(dn * xh)


def _sigmoid(x):
    return jax.nn.sigmoid(x)


def _gelu(y):
    return jax.nn.gelu(y, approximate=True)


def _gelu_grad(y):
    c = math.sqrt(2.0 / math.pi)
    t = jnp.tanh(c * (y + 0.044715 * y * y * y))
    return 0.5 * (1.0 + t) + 0.5 * y * (1.0 - t * t) * c * (1.0 + 3.0 * 0.044715 * y * y)


def _s5_tables(lb_re, lb_im, bb_re, bb_im, c_re, c_im, seg_len):
    G = lb_re.shape[0]
    nb = G // GPB
    eye = jnp.eye(GPB, dtype=F32)

    def bdiag(t):
        a, b = t.shape[1:]
        t = t.reshape(nb, GPB, a, b)
        return (t[:, :, :, None, :] * eye[None, :, None, :, None]).reshape(nb, GPB * a, GPB * b)

    bbd = jnp.concatenate([bdiag(bb_re.transpose(0, 2, 1)), bdiag(bb_im.transpose(0, 2, 1))], axis=2)
    cbd = jnp.concatenate([bdiag(c_re.transpose(0, 2, 1)), -bdiag(c_im.transpose(0, 2, 1))], axis=1)

    def lanes(re, im):
        t = jnp.concatenate([re.reshape(nb, GPB * S5_P), im.reshape(nb, GPB * S5_P)], axis=1)
        return jnp.repeat(t, NSEG, axis=0)

    tr, ti = lb_re, lb_im
    for _ in range(int(math.log2(seg_len))):
        tr, ti = tr * tr - ti * ti, 2.0 * tr * ti
    return dict(bbd=bbd.astype(BF16), bbdT=bbd.transpose(0, 2, 1).astype(BF16), cbd=cbd.astype(BF16),
                cbdT=cbd.transpose(0, 2, 1).astype(BF16), lam=lanes(lb_re, lb_im), lamT=lanes(tr, ti))


def _s5_untable(acc):
    nb = acc.shape[0]
    t = acc.reshape(nb, GPB, S5_H, 2, GPB, S5_P)
    d = jnp.diagonal(t, axis1=1, axis2=4)
    d = d.transpose(0, 4, 2, 1, 3).reshape(nb * GPB, 2, S5_H, S5_P)
    return d[:, 0], d[:, 1]


S5_R = 256


def _s5_carries(ends, first, t_ref, rws, SW, cfx, *, reverse):
    er, ei = ends
    tr, ti = t_ref[rws, :SW][0:1], t_ref[rws, SW:][0:1]
    cr, ci = first
    order = range(NSEG - 1, -1, -1) if reverse else range(NSEG)
    for n, s in enumerate(order):
        if n > 0:
            p = s + 1 if reverse else s - 1
            if reverse:
                cr, ci = tr * cr + ti * ci + er[p:p + 1], tr * ci - ti * cr + ei[p:p + 1]
            else:
                cr, ci = tr * cr - ti * ci + er[p:p + 1], tr * ci + ti * cr + ei[p:p + 1]
        cfx[s:s + 1, :SW] = cr
        cfx[s:s + 1, SW:] = ci


def _s5_fwd(up, tb, dvec, *, name):
    L, D = up.shape
    R = min(S5_R, L)
    nb, ta, ngb = L // R, R // NSEG, D // 128
    SW = GPB * S5_P
    crows = ngb * NSEG

    def body(u_ref, lam_ref, t_ref, b_ref, c_ref, d_ref, y_ref, z_ref, ck_ref, carry, xbuf2, cfx2):
        @pl.when(pl.program_id(0) == 0)
        def _():
            carry[...] = jnp.zeros_like(carry)

        zero = jnp.zeros((NSEG, SW), F32)
        for g0 in range(0, ngb, 2):
            pair = (g0, g0 + 1)
            xb = [xbuf2.at[g % 4] for g in pair]
            cf = [cfx2.at[g % 4] for g in pair]
            cols = [slice(g * 128, (g + 1) * 128) for g in pair]
            rws = [slice(g * NSEG, (g + 1) * NSEG) for g in pair]
            ug = [u_ref[:, cols[q]] for q in range(2)]
            for q in range(2):
                xb[q][...] = jnp.dot(ug[q].astype(BF16), b_ref[pair[q]], preferred_element_type=F32)
            lam = [(lam_ref[rws[q], :SW], lam_ref[rws[q], SW:]) for q in range(2)]

            def scan(c, store, xb=xb, lam=lam):
                c = list(c)
                for a in range(ta):
                    o = slice(a * NSEG, (a + 1) * NSEG)
                    for q in range(2):
                        (lr, li), (cr, ci) = lam[q], c[q]
                        nr = lr * cr - li * ci + xb[q][o, :SW]
                        ni = lr * ci + li * cr + xb[q][o, SW:]
                        if store:
                            xb[q][o, :SW] = nr
                            xb[q][o, SW:] = ni
                        c[q] = (nr, ni)
                return c

            ends = scan([(zero, zero)] * 2, False)
            for q in range(2):
                prev = (carry[rws[q], :SW][NSEG - 1:NSEG], carry[rws[q], SW:][NSEG - 1:NSEG])
                _s5_carries(ends[q], prev, t_ref, rws[q], SW, cf[q], reverse=False)
                ck_ref[0, rws[q], :] = cf[q][...]
            fin = scan([(cf[q][:, :SW], cf[q][:, SW:]) for q in range(2)], True)
            for q in range(2):
                carry[rws[q], :SW] = fin[q][0]
                carry[rws[q], SW:] = fin[q][1]
            for q in range(2):
                y = (jnp.dot(xb[q][...].astype(BF16), c_ref[pair[q]], preferred_element_type=F32)
                     + d_ref[:, cols[q]] * ug[q])
                y_ref[:, cols[q]] = y
                z_ref[:, cols[q]] = _gelu(y).astype(BF16)

    rowblk = pl.BlockSpec((R, D), lambda i: (i, 0))
    return pl.pallas_call(
        body, name=name, grid=(nb,),
        in_specs=[rowblk, _whole(tb["lam"].shape), _whole(tb["lamT"].shape), _whole(tb["bbd"].shape),
                  _whole(tb["cbd"].shape), _whole(dvec.shape)],
        out_specs=[rowblk, rowblk, pl.BlockSpec((1, crows, 2 * SW), lambda i: (i, 0, 0))],
        out_shape=[jax.ShapeDtypeStruct((L, D), F32), jax.ShapeDtypeStruct((L, D), BF16),
                   jax.ShapeDtypeStruct((nb, crows, 2 * SW), F32)],
        scratch_shapes=[pltpu.VMEM((crows, 2 * SW), F32), pltpu.VMEM((4, R, 2 * SW), F32),
                        pltpu.VMEM((4, NSEG, 2 * SW), F32)],
        compiler_params=_params("arbitrary"),
    )(up, tb["lam"], tb["lamT"], tb["bbd"], tb["cbd"], dvec)


def _s5_bwd(up, y, dz, ck, tb, dvec, *, name):
    L, D = up.shape
    R = min(S5_R, L)
    nb, ta, ngb = L // R, R // NSEG, D // 128
    SW = GPB * S5_P
    crows = ngb * NSEG

    def body(u_ref, y_ref, dz_ref, ck_ref, lam_ref, t_ref, b_ref, bt_ref, ct_ref, d_ref,
             du_ref, db_ref, dc_ref, dl_ref, dd_ref, gcarry, xbuf2, gbuf2, dybuf, cfx2):
        @pl.when(pl.program_id(0) == 0)
        def _():
            gcarry[...] = jnp.zeros_like(gcarry)
            db_ref[...] = jnp.zeros_like(db_ref)
            dc_ref[...] = jnp.zeros_like(dc_ref)
            dl_ref[...] = jnp.zeros_like(dl_ref)
            dd_ref[...] = jnp.zeros_like(dd_ref)

        zero = jnp.zeros((NSEG, SW), F32)
        dybuf[...] = dz_ref[...] * _gelu_grad(y_ref[...])
        for g0 in range(0, ngb, 2):
            pair = (g0, g0 + 1)
            xb = [xbuf2.at[g % 4] for g in pair]
            gbf = [gbuf2.at[g % 4] for g in pair]
            cf = [cfx2.at[g % 4] for g in pair]
            cols = [slice(g * 128, (g + 1) * 128) for g in pair]
            rws = [slice(g * NSEG, (g + 1) * NSEG) for g in pair]
            dyg = [dybuf[:, cols[q]] for q in range(2)]
            ug = [u_ref[:, cols[q]] for q in range(2)]
            lam = [(lam_ref[rws[q], :SW], lam_ref[rws[q], SW:]) for q in range(2)]
            for q in range(2):
                gbf[q][...] = jnp.dot(dyg[q].astype(BF16), ct_ref[pair[q]], preferred_element_type=F32)
                xb[q][0:NSEG, :] = ck_ref[0, rws[q], :]
                xb[q][NSEG:, :] = jnp.dot(ug[q].astype(BF16), b_ref[pair[q]], preferred_element_type=F32)

            c = [(xb[q][0:NSEG, :SW], xb[q][0:NSEG, SW:]) for q in range(2)]
            for a in range(ta):
                o = slice((a + 1) * NSEG, (a + 2) * NSEG)
                for q in range(2):
                    (lr, li), (cr, ci) = lam[q], c[q]
                    nr = lr * cr - li * ci + xb[q][o, :SW]
                    ni = lr * ci + li * cr + xb[q][o, SW:]
                    xb[q][o, :SW] = nr
                    xb[q][o, SW:] = ni
                    c[q] = (nr, ni)

            def rscan(c, store, qs=(0, 1), xb=xb, gbf=gbf, lam=lam):
                c = list(c)
                for a in range(ta - 1, -1, -1):
                    o = slice(a * NSEG, (a + 1) * NSEG)
                    for q in qs:
                        lr, li = lam[q]
                        gr = gbf[q][o, :SW] + lr * c[q][0] + li * c[q][1]
                        gi = gbf[q][o, SW:] - li * c[q][0] + lr * c[q][1]
                        if store:
                            gbf[q][o, :SW] = gr
                            gbf[q][o, SW:] = gi
                            xr, xi = xb[q][o, :SW], xb[q][o, SW:]
                            c[q] = (gr, gi, c[q][2] + gr * xr + gi * xi, c[q][3] + gi * xr - gr * xi)
                        else:
                            c[q] = (gr, gi)
                return c

            gends = rscan([(zero, zero)] * 2, False)
            for q in range(2):
                nxt = (gcarry[rws[q], :SW][0:1], gcarry[rws[q], SW:][0:1])
                _s5_carries(gends[q], nxt, t_ref, rws[q], SW, cf[q], reverse=True)
            fin = [(cf[q][:, :SW], cf[q][:, SW:], zero, zero) for q in range(2)]
            for q in range(2):
                fin = rscan(fin, True, qs=(q,))
            for q in range(2):
                gcarry[rws[q], :SW] = fin[q][0]
                gcarry[rws[q], SW:] = fin[q][1]
                dl_ref[rws[q], :SW] += fin[q][2]
                dl_ref[rws[q], SW:] += fin[q][3]
            gb16 = [gbf[q][...].astype(BF16) for q in range(2)]
            for q in range(2):
                du_ref[:, cols[q]] = (jnp.dot(gb16[q], bt_ref[pair[q]], preferred_element_type=F32)
                                      + d_ref[:, cols[q]] * dyg[q])
            for q in range(2):
                db_ref[pair[q]] += lax.dot_general(ug[q].astype(BF16), gb16[q], TN, preferred_element_type=F32)
            for q in range(2):
                dc_ref[pair[q]] += lax.dot_general(dyg[q].astype(BF16), xb[q][NSEG:, :].astype(BF16), TN,
                                                   preferred_element_type=F32)
                dd_ref[:, cols[q]] += _rsum(dyg[q] * ug[q])

    rev = pl.BlockSpec((R, D), lambda i: (nb - 1 - i, 0))
    acc3 = (ngb, 128, 2 * SW)
    return pl.pallas_call(
        body, name=name, grid=(nb,),
        in_specs=[rev, rev, rev, pl.BlockSpec((1, crows, 2 * SW), lambda i: (nb - 1 - i, 0, 0)),
                  _whole(tb["lam"].shape), _whole(tb["lamT"].shape), _whole(tb["bbd"].shape),
                  _whole(tb["bbdT"].shape), _whole(tb["cbdT"].shape), _whole(dvec.shape)],
        out_specs=[rev, _whole(acc3), _whole(acc3), _whole((crows, 2 * SW)), _whole((1, D))],
        out_shape=[jax.ShapeDtypeStruct((L, D), F32), jax.ShapeDtypeStruct(acc3, F32), jax.ShapeDtypeStruct(acc3, F32),
                   jax.ShapeDtypeStruct((crows, 2 * SW), F32), jax.ShapeDtypeStruct((1, D), F32)],
        scratch_shapes=[pltpu.VMEM((crows, 2 * SW), F32), pltpu.VMEM((4, R + NSEG, 2 * SW), F32),
                        pltpu.VMEM((4, R, 2 * SW), F32), pltpu.VMEM((R, D), F32), pltpu.VMEM((4, NSEG, 2 * SW), F32)],
        compiler_params=_params("arbitrary"),
    )(up, y, dz, ck, tb["lam"], tb["lamT"], tb["bbd"], tb["bbdT"], tb["cbdT"], dvec)


NN = (((1,), (0,)), ((), ()))
TN = (((0,), (0,)), ((), ()))
NT = (((1,), (1,)), ((), ()))


def _dot3(lhs, rhs, dn, split):
    x = rhs if split == "rhs" else lhs
    hi = x.astype(BF16)
    r1 = x - hi.astype(F32)
    mid = r1.astype(BF16)
    lo = (r1 - mid.astype(F32)).astype(BF16)
    out = None
    for part in (hi, mid, lo):
        ops = (lhs, part) if split == "rhs" else (part, rhs)
        t = lax.dot_general(ops[0], ops[1], dn, preferred_element_type=F32)
        out = t if out is None else out + t
    return out


def _log_sigmoid(x):
    return jnp.minimum(x, 0.0) - jnp.log(1.0 + jnp.exp(-jnp.abs(x)))


def _chunk_tri(rows, upper):
    r = lax.broadcasted_iota(jnp.int32, (rows, rows), 0)
    c = lax.broadcasted_iota(jnp.int32, (rows, rows), 1)
    same = (r // CHUNK) == (c // CHUNK)
    return (same & ((c >= r) if upper else (r >= c))).astype(BF16)


def _gla_block_gates(p_ref, wg_ref, bg_ref, QK, wbuf, gebuf):
    RB = p_ref.shape[0]
    glr = p_ref[:, 6 * QK:6 * QK + 128].astype(BF16)
    gpre = jnp.dot(glr, wg_ref[...], preferred_element_type=F32) + bg_ref[...]
    la = _log_sigmoid(gpre) * (1.0 / GATE_TAU)
    gc = _dot3(_chunk_tri(RB, False), la, NN, "rhs")
    for cc in range(RB // CHUNK):
        rows = slice(cc * CHUNK, (cc + 1) * CHUNK)
        ge = gc[(cc + 1) * CHUNK - 1:(cc + 1) * CHUNK, :]
        gebuf[cc:cc + 1, :] = ge
        wbuf[rows, :] = jnp.exp(ge - gc[rows, :])
    return glr, gpre, la


def _as_column(row, lanes):
    t = jnp.transpose(jnp.broadcast_to(row, (row.shape[1], row.shape[1])))
    return jnp.concatenate([t] * (lanes // row.shape[1]), axis=1)


def _as_row(col):
    return jnp.transpose(jnp.broadcast_to(col, (col.shape[0], col.shape[0])))[0:1, :]


def _gla_fwd(proj, wg2p, bg, gn, *, name):
    L = proj.shape[0]
    QK = wg2p.shape[1]
    DK, DV = QK // HEADS, 2 * QK // HEADS
    nC = L // CHUNK
    NB = min(GLA_NB, nC)
    assert nC % NB == 0
    scale = DK ** -0.5

    def body(p_ref, wg_ref, bg_ref, gn_ref, og_ref, s_ref, sst, wbuf, gebuf):
        @pl.when(pl.program_id(0) == 0)
        def _():
            sst[...] = jnp.zeros_like(sst)

        _gla_block_gates(p_ref, wg_ref, bg_ref, QK, wbuf, gebuf)
        heads = range(HEADS)
        ks = [slice(h * DK, (h + 1) * DK) for h in heads]
        vs = [slice(h * DV, (h + 1) * DV) for h in heads]
        units = [(cc, h) for cc in range(NB) for h in heads]
        rows = [slice(cc * CHUNK, (cc + 1) * CHUNK) for cc in range(NB)]
        kv = {(cc, h): lax.dot_general(
            (p_ref[rows[cc], QK + h * DK:QK + (h + 1) * DK] * wbuf[rows[cc], ks[h]]).astype(BF16),
            p_ref[rows[cc], 2 * QK + h * DV:2 * QK + (h + 1) * DV].astype(BF16), TN, preferred_element_type=F32)
            for cc, h in units}
        S16 = {}
        for cc, h in units:
            S = jnp.exp(_as_column(gebuf[cc:cc + 1, ks[h]], DV)) * sst[ks[h], :] + kv[cc, h]
            sst[ks[h], :] = S
            s_ref[cc, ks[h], :] = S
            S16[cc, h] = S.astype(BF16)
        o = {(cc, h): jnp.dot((p_ref[rows[cc], h * DK:(h + 1) * DK] * scale).astype(BF16), S16[cc, h],
                              preferred_element_type=F32) for cc, h in units}
        for cc, h in units:
            r = p_ref[rows[cc], 4 * QK + h * DV:4 * QK + (h + 1) * DV]
            on = o[cc, h] * lax.rsqrt(jnp.mean(o[cc, h] * o[cc, h], axis=-1, keepdims=True) + EPS)
            og_ref[rows[cc], vs[h]] = (on * gn_ref[:, vs[h]] * (r * _sigmoid(r))).astype(BF16)

    RB = NB * CHUNK
    return pl.pallas_call(
        body, name=name, grid=(nC // NB,),
        in_specs=[pl.BlockSpec((RB, proj.shape[1]), lambda i: (i, 0)), _whole(wg2p.shape), _whole(bg.shape), _whole(gn.shape)],
        out_specs=[pl.BlockSpec((RB, 2 * QK), lambda i: (i, 0)), pl.BlockSpec((NB, QK, DV), lambda i: (i, 0, 0))],
        out_shape=[jax.ShapeDtypeStruct((L, 2 * QK), BF16), jax.ShapeDtypeStruct((nC, QK, DV), F32)],
        scratch_shapes=[pltpu.VMEM((QK, DV), F32), pltpu.VMEM((RB, QK), F32), pltpu.VMEM((8, QK), F32)],
        compiler_params=_params("arbitrary"),
    )(proj, wg2p, bg, gn)


def _gla_bwd(proj, dog, states, wg2p, bg, gn, *, name):
    L, W = proj.shape
    QK = wg2p.shape[1]
    DK, DV = QK // HEADS, 2 * QK // HEADS
    nC = L // CHUNK
    NB = min(GLA_NB, nC)
    nB = nC // NB
    scale = DK ** -0.5

    def body(p_ref, dog_ref, sc_ref, sp_ref, wg_ref, bg_ref, gn_ref, dp_ref, dwg_ref, dbg_ref, dgn_ref,
             gst, wbuf, gebuf, dwwbuf, dgebuf):
        i = pl.program_id(0)

        @pl.when(i == 0)
        def _():
            gst[...] = jnp.zeros_like(gst)
            dwg_ref[...] = jnp.zeros_like(dwg_ref)
            dbg_ref[...] = jnp.zeros_like(dbg_ref)
            dgn_ref[...] = jnp.zeros_like(dgn_ref)

        RB = NB * CHUNK
        glr, gpre, _ = _gla_block_gates(p_ref, wg_ref, bg_ref, QK, wbuf, gebuf)
        heads = range(HEADS)
        ks = [slice(h * DK, (h + 1) * DK) for h in heads]
        vs = [slice(h * DV, (h + 1) * DV) for h in heads]
        units = [(cc, h) for cc in range(NB) for h in heads]
        rws = [slice(cc * CHUNK, (cc + 1) * CHUNK) for cc in range(NB)]
        qs16 = {(cc, h): (p_ref[rws[cc], h * DK:(h + 1) * DK] * scale).astype(BF16) for cc, h in units}
        S16a = {(cc, h): sc_ref[cc, ks[h], :].astype(BF16) for cc, h in units}
        oa = {u: jnp.dot(qs16[u], S16a[u], preferred_element_type=F32) for u in units}
        doa = {}
        for cc, h in units:
            r = p_ref[rws[cc], 4 * QK + h * DV:4 * QK + (h + 1) * DV]
            o = oa[cc, h]
            rs = lax.rsqrt(jnp.mean(o * o, axis=-1, keepdims=True) + EPS)
            on = o * rs
            sr = _sigmoid(r)
            dg = dog_ref[rws[cc], vs[h]]
            gnh = gn_ref[:, vs[h]]
            dp_ref[rws[cc], 4 * QK + h * DV:4 * QK + (h + 1) * DV] = (
                dg * on * gnh * (sr * (1.0 + r * (1.0 - sr)))).astype(BF16)
            dt = dg * (r * sr)
            dgn_ref[:, vs[h]] += _rsum(dt * on)
            don = dt * gnh
            doa[cc, h] = (rs * (don - on * jnp.mean(don * on, axis=-1, keepdims=True))).astype(BF16)
        dqa = {u: lax.dot_general(doa[u], S16a[u], NT, preferred_element_type=F32) for u in units}
        for cc, h in units:
            dp_ref[rws[cc], h * DK:(h + 1) * DK] = (dqa[cc, h] * scale).astype(BF16)
        for cc in range(NB - 1, -1, -1):
            rows = rws[cc]
            do = [doa[cc, h] for h in heads]
            Gc = [gst[ks[h], :] + lax.dot_general(qs16[cc, h], do[h], TN, preferred_element_type=F32) for h in heads]
            G16 = [g.astype(BF16) for g in Gc]
            kd = [p_ref[rows, QK + h * DK:QK + (h + 1) * DK] * wbuf[rows, ks[h]] for h in heads]
            dkd = [lax.dot_general(p_ref[rows, 2 * QK + h * DV:2 * QK + (h + 1) * DV].astype(BF16), G16[h], NT,
                                   preferred_element_type=F32) for h in heads]
            dv = [jnp.dot(kd[h].astype(BF16), G16[h], preferred_element_type=F32) for h in heads]
            for h in heads:
                if cc > 0:
                    Sp = sc_ref[cc - 1, ks[h], :]
                else:
                    Sp = jnp.where(i < nB - 1, sp_ref[0, ks[h], :], 0.0)
                dp_ref[rows, 2 * QK + h * DV:2 * QK + (h + 1) * DV] = dv[h].astype(BF16)
                ge = gebuf[cc:cc + 1, ks[h]]
                gst[ks[h], :] = jnp.exp(_as_column(ge, DV)) * Gc[h]
                ddec = _as_row(jnp.sum(Gc[h] * Sp, axis=1, keepdims=True))
                dp_ref[rows, QK + h * DK:QK + (h + 1) * DK] = (dkd[h] * wbuf[rows, ks[h]]).astype(BF16)
                dww = dkd[h] * kd[h]
                dwwbuf[rows, ks[h]] = dww
                dgebuf[cc:cc + 1, ks[h]] = jnp.exp(ge) * ddec + _rsum(dww)
        rev = _dot3(_chunk_tri(RB, True), dwwbuf[...], NN, "rhs")
        for cc in range(NB):
            rows = slice(cc * CHUNK, (cc + 1) * CHUNK)
            wbuf[rows, :] = dgebuf[cc:cc + 1, :] - rev[rows, :]
        dgpre = wbuf[...] * (1.0 / GATE_TAU) * (1.0 - _sigmoid(gpre))
        d16 = dgpre.astype(BF16)
        dp_ref[:, 6 * QK:6 * QK + 128] = lax.dot_general(d16, wg_ref[...], NT, preferred_element_type=F32).astype(BF16)
        dwg_ref[...] += lax.dot_general(glr, d16, TN, preferred_element_type=F32)
        dbg_ref[...] += _rsum(dgpre)

    RB = NB * CHUNK
    rev_idx = lambda i: (nB - 1 - i, 0)
    return pl.pallas_call(
        body, name=name, grid=(nB,),
        in_specs=[pl.BlockSpec((RB, W), rev_idx), pl.BlockSpec((RB, 2 * QK), rev_idx),
                  pl.BlockSpec((NB, QK, DV), lambda i: (nB - 1 - i, 0, 0)),
                  pl.BlockSpec((1, QK, DV), lambda i: (jnp.maximum(NB * (nB - 1 - i) - 1, 0), 0, 0)),
                  _whole(wg2p.shape), _whole(bg.shape), _whole(gn.shape)],
        out_specs=[pl.BlockSpec((RB, W), rev_idx), _whole((128, QK)), _whole((1, QK)), _whole((1, 2 * QK))],
        out_shape=[jax.ShapeDtypeStruct((L, W), BF16), jax.ShapeDtypeStruct((128, QK), F32),
                   jax.ShapeDtypeStruct((1, QK), F32), jax.ShapeDtypeStruct((1, 2 * QK), F32)],
        scratch_shapes=[pltpu.VMEM((QK, DV), F32), pltpu.VMEM((RB, QK), F32), pltpu.VMEM((8, QK), F32),
                        pltpu.VMEM((RB, QK), F32), pltpu.VMEM((8, QK), F32)],
        compiler_params=_params("arbitrary"),
    )(proj, dog, states, states, wg2p, bg, gn)


def _coords():
    return lax.axis_index("x"), lax.axis_index("y"), lax.axis_index("c")


def _other_chips(x, y):
    return [(1 - x, y, 2 * (1 - x) + y), (x, 1 - y, 2 * x + 1 - y), (1 - x, 1 - y, 2 * (1 - x) + 1 - y)]


def _hbm_call(body, ins, out_shapes, n_sems, *, name, alias=False):
    any_spec = pl.BlockSpec(memory_space=pl.ANY)
    return pl.pallas_call(
        body, name=name, in_specs=[any_spec] * len(ins), out_specs=[any_spec] * len(out_shapes), out_shape=out_shapes,
        scratch_shapes=[pltpu.SemaphoreType.DMA((n,)) for n in n_sems],
        input_output_aliases={k: k for k in range(len(ins))} if alias else {},
    )(*ins)


def _exchange(src, masks, *, name):
    n = len(masks)

    def body(src_ref, dst_ref, send_sems, recv_sems, loc_sem):
        copies, mine = _exchange_copies(src_ref, dst_ref, masks, send_sems, recv_sems)
        loc = pltpu.make_async_copy(src_ref, dst_ref.at[mine], loc_sem.at[0])
        loc.start()
        for cp in copies:
            cp.start()
        for cp in copies:
            cp.wait()
        loc.wait()

    nslots = 2 ** sum(any(m[k] for m in masks) for k in range(3))
    return _hbm_call(body, [src], [jax.ShapeDtypeStruct((nslots,) + tuple(src.shape), src.dtype)], (n, n, 1), name=name)[0]


def _exchange_copies(src_ref, dst_ref, masks, send_sems, recv_sems):
    vary = [any(m[k] for m in masks) for k in range(3)]
    me = _coords()
    mine = 0
    for k in range(3):
        if vary[k]:
            mine = mine * 2 + me[k]
    copies = []
    for k, m in enumerate(masks):
        peer = tuple(1 - me[d] if m[d] else me[d] for d in range(3))
        copies.append(pltpu.make_async_remote_copy(src_ref=src_ref, dst_ref=dst_ref.at[mine], send_sem=send_sems.at[k],
                                                   recv_sem=recv_sems.at[k], device_id=peer, device_id_type=MESH))
    return copies, mine


def _exchange_start(src, masks, *, name):
    nslots = 2 ** sum(any(m[k] for m in masks) for k in range(3))

    def start(refs, sems):
        for cp in _exchange_copies(refs[0], refs[1], masks, sems[0], sems[1])[0]:
            cp.start()

    out = lax.empty((nslots,) + tuple(src.shape), src.dtype)
    sems, thru, token = _split_start(start, [src, out], [len(masks)] * 2, name=name)
    return (sems[0], sems[1]), thru[0], thru[1], token


def _exchange_wait(pair, src, out, masks, after, *, name):
    def wait(refs, sems):
        for cp in _exchange_copies(refs[0], refs[1], masks, sems[0], sems[1])[0]:
            cp.wait()

    return _split_wait(wait, [src, out], list(pair), after, name=name)


def _cast_into(t, lead, kind, chip, after, *, name, tm=256):
    r, cc = t.shape[-2:]
    tm = min(tm, r)
    nblk = r // tm
    if kind == "col":
        shp, o_spec = (r, NCH * cc), pl.BlockSpec((tm, cc), lambda i, s: (i, s[0]))
    elif kind == "row":
        shp, o_spec = (NCH * r, cc), pl.BlockSpec((tm, cc), lambda i, s: (s[0] * nblk + i, 0))
    else:
        shp, o_spec = (NCH, r, cc), pl.BlockSpec((None, tm, cc), lambda i, s: (s[0], i, 0))

    def body(s_ref, t_ref, after_ref, o_ref):
        o_ref[...] = t_ref[...].astype(o_ref.dtype)

    return pl.pallas_call(
        body, name=name,
        grid_spec=pltpu.PrefetchScalarGridSpec(
            num_scalar_prefetch=1, grid=(nblk,),
            in_specs=[pl.BlockSpec((None, tm, cc), lambda i, s: (lead, i, 0)), pl.BlockSpec(memory_space=pl.ANY)],
            out_specs=o_spec),
        out_shape=jax.ShapeDtypeStruct(shp, BF16), compiler_params=_params("parallel"),
    )(chip.reshape(1).astype(jnp.int32), t, after)


def _gather_weights(arrs, shard_shapes, kinds, *, name):
    n = len(arrs)

    def body(*refs):
        dst = refs[n:2 * n]
        send_sems, recv_sems = refs[2 * n:]
        x, y, c = _coords()
        chip = 2 * x + y
        others = _other_chips(x, y)
        sib = (x, y, 1 - c)

        def window(p, chip_id, cc):
            r, cols = shard_shapes[p]
            h = r // 2
            if kinds[p] == "col":
                return dst[p].at[pl.ds(cc * h, h), pl.ds(pl.multiple_of(chip_id * cols, 128), cols)]
            if kinds[p] == "row":
                return dst[p].at[pl.ds(chip_id * r + cc * h, h), :]
            return dst[p].at[chip_id, pl.ds(cc * h, h), :]

        def copy(p, k, win, to):
            return pltpu.make_async_remote_copy(src_ref=win, dst_ref=win, send_sem=send_sems.at[6 * p + k],
                                                recv_sem=recv_sems.at[6 * p + k], device_id=to, device_id_type=MESH)

        sends = []
        for p in range(n):
            for j, (ox, oy, _) in enumerate(others):
                cp = copy(p, j, window(p, chip, c), (ox, oy, c))
                cp.start()
                sends.append(cp)
        for j, (_, _, oc) in enumerate(others):
            for p in range(n):
                copy(p, j, window(p, oc, c), (x, y, c)).wait_recv()
                fw = copy(p, 3 + j, window(p, oc, c), sib)
                fw.start()
                sends.append(fw)
        for p in range(n):
            for j, (_, _, oc) in enumerate(others):
                copy(p, 3 + j, window(p, oc, 1 - c), sib).wait_recv()
        for cp in sends:
            cp.wait_send()

    outs = [jax.ShapeDtypeStruct(a.shape, a.dtype) for a in arrs]
    return _hbm_call(body, arrs, outs, (6 * n, 6 * n), name=name, alias=True)


HBM_SPEC = pl.BlockSpec(memory_space=pltpu.HBM)
SEM_SPEC = pl.BlockSpec(memory_space=pltpu.SEMAPHORE)
EFFECT = pltpu.SideEffectType.DATAFLOW_SIDE_EFFECTING


def _window(ref, shard_shape, kind, chip_id, cc):
    r, cols = shard_shape
    h = r // 2
    if kind == "col":
        return ref.at[pl.ds(cc * h, h), pl.ds(pl.multiple_of(chip_id * cols, 128), cols)]
    if kind == "row":
        return ref.at[pl.ds(chip_id * r + cc * h, h), :]
    return ref.at[chip_id, pl.ds(cc * h, h), :]


def _split_start(start, arrs, n_sems, *, name):
    n, ns = len(arrs), len(n_sems)

    def body(*refs):
        start(refs[:n], refs[n:n + ns])
        refs[-1][...] = jnp.zeros_like(refs[-1])

    outs = pl.pallas_call(
        body, name=name,
        out_shape=tuple([pltpu.SemaphoreType.DMA((k,)) for k in n_sems] + [pltpu.HBM(a.shape, a.dtype) for a in arrs]
                        + [jax.ShapeDtypeStruct((8, 128), F32)]),
        in_specs=[HBM_SPEC] * n, out_specs=tuple([SEM_SPEC] * ns + [HBM_SPEC] * n + [pl.BlockSpec(memory_space=pltpu.VMEM)]),
        input_output_aliases={k: ns + k for k in range(n)},
        compiler_params=pltpu.CompilerParams(has_side_effects=EFFECT),
    )(*[pltpu.with_memory_space_constraint(a, pltpu.HBM) for a in arrs])
    return list(outs[:ns]), list(outs[ns:ns + n]), outs[-1]


def _split_wait(wait, arrs, sems, after, *, name):
    n, ns = len(arrs), len(sems)

    def body(*refs):
        wait(refs[:n], refs[n:n + ns])

    return pl.pallas_call(
        body, name=name, out_shape=tuple(pltpu.HBM(a.shape, a.dtype) for a in arrs),
        in_specs=[HBM_SPEC] * n + [SEM_SPEC] * ns + [pl.BlockSpec(memory_space=pl.ANY)], out_specs=tuple([HBM_SPEC] * n),
        input_output_aliases={k: k for k in range(n)},
        compiler_params=pltpu.CompilerParams(has_side_effects=EFFECT),
    )(*arrs, *sems, after)


def _gw_copies(refs, send_sems, recv_sems, shard_shapes, kinds, outgoing):
    x, y, c = _coords()
    chip = 2 * x + y
    out = []
    for p in range(len(refs)):
        for j, (ox, oy, oc) in enumerate(_other_chips(x, y)):
            win = _window(refs[p], shard_shapes[p], kinds[p], chip if outgoing else oc, c)
            out.append(pltpu.make_async_remote_copy(
                src_ref=win, dst_ref=win, send_sem=send_sems.at[3 * p + j], recv_sem=recv_sems.at[3 * p + j],
                device_id=(ox, oy, c), device_id_type=MESH))
    return out


def _gw_start(arrs, shard_shapes, kinds, groups, *, name):
    def start(refs, sems):
        for g, idx in enumerate(groups):
            for cp in _gw_copies([refs[p] for p in idx], sems[2 * g], sems[2 * g + 1], [shard_shapes[p] for p in idx],
                                 [kinds[p] for p in idx], True):
                cp.start()

    n_sems = [3 * len(idx) for idx in groups for _ in range(2)]
    sems, thru, token = _split_start(start, arrs, n_sems, name=name)
    return [(sems[2 * g], sems[2 * g + 1]) for g in range(len(groups))], thru, token


def _gw_wait(arrs, shard_shapes, kinds, sem_pair, after, *, name):
    def wait(refs, sems):
        for cp in _gw_copies(refs, sems[0], sems[1], shard_shapes, kinds, True):
            cp.wait_send()
        for cp in _gw_copies(refs, sems[0], sems[1], shard_shapes, kinds, False):
            cp.wait_recv()

    return _split_wait(wait, arrs, list(sem_pair), after, name=name)


def _gw_forward_copies(refs, shard_shapes, kinds, send_sems, recv_sems, incoming):
    x, y, c = _coords()
    out = []
    for p in range(len(refs)):
        for j, (_, _, oc) in enumerate(_other_chips(x, y)):
            win = _window(refs[p], shard_shapes[p], kinds[p], oc, 1 - c if incoming else c)
            out.append(pltpu.make_async_remote_copy(
                src_ref=win, dst_ref=win, send_sem=send_sems.at[3 * p + j], recv_sem=recv_sems.at[3 * p + j],
                device_id=(x, y, 1 - c), device_id_type=MESH))
    return out


def _gw_forward(arrs, shard_shapes, kinds, *, name):
    n = len(arrs)

    def body(*refs):
        dst, (send_sems, recv_sems) = refs[n:2 * n], refs[2 * n:]
        sends = _gw_forward_copies(dst, shard_shapes, kinds, send_sems, recv_sems, False)
        for cp in sends:
            cp.start()
        for cp in _gw_forward_copies(dst, shard_shapes, kinds, send_sems, recv_sems, True):
            cp.wait_recv()
        for cp in sends:
            cp.wait_send()

    outs = [jax.ShapeDtypeStruct(a.shape, a.dtype) for a in arrs]
    return _hbm_call(body, arrs, outs, (3 * n, 3 * n), name=name, alias=True)


def _gw_forward_start(arrs, shard_shapes, kinds, *, name):
    def start(refs, sems):
        for cp in _gw_forward_copies(refs, shard_shapes, kinds, sems[0], sems[1], False):
            cp.start()

    sems, thru, token = _split_start(start, arrs, [3 * len(arrs)] * 2, name=name)
    return (sems[0], sems[1]), thru, token


def _gw_forward_wait(pair, arrs, shard_shapes, kinds, after, *, name):
    def wait(refs, sems):
        for cp in _gw_forward_copies(refs, shard_shapes, kinds, sems[0], sems[1], False):
            cp.wait_send()
        for cp in _gw_forward_copies(refs, shard_shapes, kinds, sems[0], sems[1], True):
            cp.wait_recv()

    return _split_wait(wait, arrs, list(pair), after, name=name)


def _rs_chips_copies(parts, lands, send_sems, recv_sems):
    x, y, c = _coords()
    chip = 2 * x + y
    out = []
    for p in range(len(parts)):
        for j, (ox, oy, oc) in enumerate(_other_chips(x, y)):
            out.append(pltpu.make_async_remote_copy(
                src_ref=parts[p].at[oc], dst_ref=lands[p].at[chip], send_sem=send_sems.at[3 * p + j],
                recv_sem=recv_sems.at[3 * p + j], device_id=(ox, oy, c), device_id_type=MESH))
    return out


def _rs_chips_start(parts, *, name):
    n = len(parts)

    def start(refs, sems):
        for cp in _rs_chips_copies(refs[:n], refs[n:], sems[0], sems[1]):
            cp.start()

    lands = [lax.empty(t.shape, t.dtype) for t in parts]
    sems, thru, token = _split_start(start, list(parts) + lands, [3 * n, 3 * n], name=name)
    return (sems[0], sems[1]), thru[:n], thru[n:], token


def _rs_chips_wait(groups, after, *, name):
    sizes = [len(g[1]) for g in groups]
    arrs = [a for g in groups for a in list(g[1]) + list(g[2])]
    sems = [s for g in groups for s in g[0]]

    def wait(refs, sem_refs):
        o = 0
        for k, n in enumerate(sizes):
            for cp in _rs_chips_copies(refs[o:o + n], refs[o + n:o + 2 * n], sem_refs[2 * k], sem_refs[2 * k + 1]):
                cp.wait()
            o += 2 * n

    outs = _split_wait(wait, arrs, sems, after, name=name)
    res, o = [], 0
    for n in sizes:
        res.append((list(outs[o:o + n]), list(outs[o + n:o + 2 * n])))
        o += 2 * n
    return res


def _rs_cores_copies(grads, lands, send_sems, recv_sems):
    x, y, c = _coords()
    out, o = [], 0
    for p in range(len(grads)):
        nsh, h = lands[p].shape[0], lands[p].shape[1]
        for j in range(nsh):
            out.append(pltpu.make_async_remote_copy(
                src_ref=grads[p].at[j, pl.ds((1 - c) * h, h), :], dst_ref=lands[p].at[j],
                send_sem=send_sems.at[o + j], recv_sem=recv_sems.at[o + j], device_id=(x, y, 1 - c), device_id_type=MESH))
        o += nsh
    return out


def _rs_cores_start(grads, *, name):
    n = len(grads)
    tot = sum(g.shape[0] for g in grads)

    def start(refs, sems):
        for cp in _rs_cores_copies(refs[:n], refs[n:], sems[0], sems[1]):
            cp.start()

    lands = [lax.empty((g.shape[0], g.shape[1] // 2, g.shape[2]), g.dtype) for g in grads]
    sems, thru, token = _split_start(start, list(grads) + lands, [tot, tot], name=name)
    return (sems[0], sems[1]), thru[:n], thru[n:], token


def _rs_cores_wait(pair, grads, lands, after, *, name):
    n = len(grads)

    def wait(refs, sems):
        for cp in _rs_cores_copies(refs[:n], refs[n:], sems[0], sems[1]):
            cp.wait()

    outs = _split_wait(wait, list(grads) + list(lands), list(pair), after, name=name)
    return list(outs[:n]), list(outs[n:])


def _rs_cores(grads, *, name):
    n = len(grads)
    outs = [jax.ShapeDtypeStruct((g.shape[0], g.shape[1] // 2, g.shape[2]), g.dtype) for g in grads]

    def body(*refs):
        src, dst = refs[:n], refs[n:2 * n]
        send_sems, recv_sems = refs[2 * n:]
        x, y, c = _coords()
        copies = []
        for p in range(n):
            nsh, r, _ = grads[p].shape
            h = r // 2
            for j in range(nsh):
                cp = pltpu.make_async_remote_copy(
                    src_ref=src[p].at[j, pl.ds((1 - c) * h, h), :], dst_ref=dst[p].at[j],
                    send_sem=send_sems.at[nsh * p + j], recv_sem=recv_sems.at[nsh * p + j],
                    device_id=(x, y, 1 - c), device_id_type=MESH)
                cp.start()
                copies.append(cp)
        for cp in copies:
            cp.wait()

    tot = sum(g.shape[0] for g in grads)
    return _hbm_call(body, grads, outs, (tot, tot), name=name)


def _sum_own_half(full, recv, ci, out_dtype, *, name):
    nsh, h, cols = recv.shape
    tm = h if nsh * h * cols * 4 <= (2 << 20) else _tile_rows(h, 256)
    nblk = h // tm

    def body(c_ref, f_ref, r_ref, o_ref):
        o_ref[...] = (f_ref[...] + r_ref[...]).astype(o_ref.dtype)

    return pl.pallas_call(
        body, name=name,
        grid_spec=pltpu.PrefetchScalarGridSpec(
            num_scalar_prefetch=1, grid=(nsh, nblk),
            in_specs=[pl.BlockSpec((1, tm, cols), lambda j, i, c_ref: (j, c_ref[0] * nblk + i, 0)),
                      pl.BlockSpec((1, tm, cols), lambda j, i, c_ref: (j, i, 0))],
            out_specs=pl.BlockSpec((1, tm, cols), lambda j, i, c_ref: (j, i, 0))),
        out_shape=jax.ShapeDtypeStruct((nsh, h, cols), out_dtype), compiler_params=_params("parallel", "parallel"),
    )(ci.reshape(1).astype(jnp.int32), full, recv)


def _sum_chips(recv, own, chip, ci, *, name, nlead=1, lead=0, prev=None, spread=False):
    nsh, h, cols = recv.shape
    tm = h if nsh * h * cols * 4 <= (2 << 20) else _tile_rows(h, 256)
    nblk = h // tm
    rows_out = 2 * h * (nsh if spread else 1)

    def body(s_ref, r_ref, o_ref, *rest):
        out_ref = rest[-1]
        t = None
        for s in range(nsh):
            v = jnp.where(s_ref[0] == s, o_ref[s], r_ref[s]).astype(F32)
            t = v if t is None else t + v
        out_ref[...] = t

    def out_idx(i, s):
        return (lead, (s[0] * 2 * nblk if spread else 0) + s[1] * nblk + i, 0)

    blk = pl.BlockSpec((nsh, tm, cols), lambda i, s: (0, i, 0))
    ins = [recv, own] + ([prev] if prev is not None else [])
    return pl.pallas_call(
        body, name=name,
        grid_spec=pltpu.PrefetchScalarGridSpec(
            num_scalar_prefetch=1, grid=(nblk,),
            in_specs=[blk, blk] + ([pl.BlockSpec(memory_space=pl.ANY)] if prev is not None else []),
            out_specs=pl.BlockSpec((None, tm, cols), out_idx)),
        out_shape=jax.ShapeDtypeStruct((nlead, rows_out, cols), F32),
        input_output_aliases={3: 0} if prev is not None else {},
        compiler_params=_params("arbitrary"),
    )(jnp.stack([chip, ci]).astype(jnp.int32), *ins)


RS_GATHER_CHUNKS = 4


def _rs_gather_copies(refs, nleads, halves, spread, send_sems, recv_sems, incoming):
    x, y, c = _coords()
    chip = 2 * x + y
    out, sem = [], 0
    for p in range(len(refs)):
        h = halves[p]
        q = h // RS_GATHER_CHUNKS
        base = (chip * 2 * h if spread[p] else 0) + (1 - c if incoming else c) * h
        for l in range(nleads[p]):
            for k in range(RS_GATHER_CHUNKS):
                win = refs[p].at[l, pl.ds(base + k * q, q), :]
                out.append(pltpu.make_async_remote_copy(src_ref=win, dst_ref=win, send_sem=send_sems.at[sem],
                                                        recv_sem=recv_sems.at[sem], device_id=(x, y, 1 - c),
                                                        device_id_type=MESH))
                sem += 1
    return out


def _rs_gather_start(arrs, halves, spread, *, name):
    nleads = [a.shape[0] for a in arrs]
    tot = sum(nleads) * RS_GATHER_CHUNKS

    def start(refs, sems):
        for cp in _rs_gather_copies(refs, nleads, halves, spread, sems[0], sems[1], False):
            cp.start()

    sems, thru, token = _split_start(start, arrs, [tot, tot], name=name)
    return (sems[0], sems[1]), thru, token


def _rs_gather_wait(pair, arrs, halves, spread, after, *, name):
    nleads = [a.shape[0] for a in arrs]

    def wait(refs, sems):
        for cp in _rs_gather_copies(refs, nleads, halves, spread, sems[0], sems[1], False):
            cp.wait_send()
        for cp in _rs_gather_copies(refs, nleads, halves, spread, sems[0], sems[1], True):
            cp.wait_recv()

    return _split_wait(wait, arrs, list(pair), after, name=name)


def _adamw(w, g, m, v, *, name):
    nl, R, C = w.shape
    tm = _tile_rows(R, 256)

    blk = pl.BlockSpec((None, tm, C), lambda l, i: (l, i, 0))
    return pl.pallas_call(
        _adamw_body_copy(), name=name, grid=(nl, R // tm), in_specs=[blk] * 4, out_specs=[blk] * 3,
        out_shape=[jax.ShapeDtypeStruct((nl, R, C), F32)] * 3, compiler_params=_params("parallel", "parallel"),
    )(w, g, m, v)


def _adamw_body(w_ref, g_ref, m_ref, v_ref, d_ref, nm_ref, nv_ref):
    gg = g_ref[...]
    nm = B1 * m_ref[...] + (1.0 - B1) * gg
    nv = B2 * v_ref[...] + (1.0 - B2) * (gg * gg)
    m_hat = nm / (1.0 - B1 ** ASTEP)
    v_hat = nv / (1.0 - B2 ** ASTEP)
    d_ref[...] = -LR * (m_hat / (jnp.sqrt(v_hat) + AEPS) + WD * w_ref[...])
    nm_ref[...] = nm
    nv_ref[...] = nv


def _adamw_whole(w, g, m, v, *, name):
    return pl.pallas_call(_adamw_body_copy(), name=name, out_shape=[jax.ShapeDtypeStruct(w.shape, F32)] * 3,
                          compiler_params=_params())(w, g, m, v)


def _adamw_body_copy():
    def body(*refs):
        _adamw_body(*refs)
    return body


def _mod_cols(c_all, w_ada, b_cols, *, name):
    nl, D, cols = w_ada.shape
    B = c_all.shape[0]

    def body(c_ref, w_ref, b_ref, o_ref):
        cc = c_ref[...]
        cs = (cc * _sigmoid(cc)).astype(BF16)
        o_ref[0] = jnp.dot(cs, w_ref[0].astype(BF16), preferred_element_type=F32) + b_ref[0]

    return pl.pallas_call(
        body, name=name, grid=(nl,),
        in_specs=[_whole(c_all.shape), pl.BlockSpec((1, D, cols), lambda i: (i, 0, 0)), pl.BlockSpec((1, 1, cols), lambda i: (i, 0, 0))],
        out_specs=pl.BlockSpec((1, B, cols), lambda i: (i, 0, 0)),
        out_shape=jax.ShapeDtypeStruct((nl, B, cols), F32), compiler_params=_params("arbitrary"),
    )(c_all, w_ada, b_cols)


def _ada_grad(c_all, dmod_cols, *, name):
    nl, B, cols = dmod_cols.shape
    D = c_all.shape[1]

    def body(c_ref, d_ref, o_ref):
        cc = c_ref[...]
        cs = (cc * _sigmoid(cc)).astype(BF16)
        o_ref[0] = lax.dot_general(cs, d_ref[0].astype(BF16), TN, preferred_element_type=F32)

    return pl.pallas_call(
        body, name=name, grid=(nl,),
        in_specs=[_whole(c_all.shape), pl.BlockSpec((1, B, cols), lambda i: (i, 0, 0))],
        out_specs=pl.BlockSpec((1, D, cols), lambda i: (i, 0, 0)),
        out_shape=jax.ShapeDtypeStruct((nl, D, cols), F32), compiler_params=_params("arbitrary"),
    )(c_all, dmod_cols)


def _s5_disc(a_re, a_im, log_dt, b_re, b_im):
    dt = jnp.exp(log_dt)[:, None]
    mag = jnp.exp(a_re * dt)
    ph = a_im * dt
    lb_re = mag * jnp.cos(ph)
    lb_im = mag * jnp.sin(ph)
    den = a_re * a_re + a_im * a_im
    nr = lb_re - 1.0
    ni = lb_im
    f_re = (nr * a_re + ni * a_im) / den
    f_im = (ni * a_re - nr * a_im) / den
    bb_re = f_re[..., None] * b_re - f_im[..., None] * b_im
    bb_im = f_re[..., None] * b_im + f_im[..., None] * b_re
    return lb_re, lb_im, bb_re, bb_im


def _to_segments(t):
    L, D = t.shape
    R = min(S5_R, L)
    return t.reshape(L // R, NSEG, R // NSEG, D).transpose(0, 2, 1, 3).reshape(L, D)


def _from_segments(t):
    L, D = t.shape
    R = min(S5_R, L)
    return t.reshape(L // R, R // NSEG, NSEG, D).transpose(0, 2, 1, 3).reshape(L, D)


def _w_in_layout(QK, D, ncols):
    segs = [(0, 4 * QK, 0), (4 * QK + GATE_RANK, 4 * QK + GATE_RANK + D, 4 * QK), (4 * QK, 4 * QK + GATE_RANK, 4 * QK + D)]
    fwd = []
    for lo, hi, _ in segs:
        col = lo
        while col < hi:
            j = col // ncols
            end = min(hi, (j + 1) * ncols)
            fwd.append((j, col - j * ncols, end - j * ncols))
            col = end
    bwd = []
    for j in range(NCH):
        ranges, col = [], j * ncols
        while col < (j + 1) * ncols:
            lo, hi, rlo = next(sg for sg in segs if sg[0] <= col < sg[1])
            end = min((j + 1) * ncols, hi)
            ranges.append((rlo + col - lo, rlo + end - lo))
            col = end
        bwd.append(ranges)
    return fwd, bwd


def _mlp_fwd(h2, w1, w2, tag):
    a = _matmul(h2, w1, name=f"ff1_{tag}", tn=2048, out_dtypes=(BF16,), epi=lambda acc: (jnp.maximum(acc, 0.0),))
    f = _matmul(a, w2, name=f"ff2_{tag}", a_fn=jnp.square)
    return a, f


def _mlp_bwd(df, h2, a, w1, w2, tag):
    da = _matmul(df, w2, tb=True, name=f"ff2_dx_{tag}", tn=2048, out_dtypes=(BF16,), epi_ins=(a,),
                 epi=lambda acc, at: (acc * (2.0 * at.astype(F32)),))
    dw2 = _matmul(a, df, ta=True, name=f"ff2_dw_{tag}", a_fn=jnp.square)
    dh2 = _matmul(da, w1, tb=True, name=f"ff1_dx_{tag}")
    dw1 = _matmul(h2, da, ta=True, name=f"ff1_dw_{tag}", col_shards=NCH)
    return dh2, dw1, dw2


def kernel(x, c, w_ada, b_ada, norm_mix, norm_mlp, s5_a_re, s5_a_im, s5_log_dt, s5_b_re, s5_b_im, s5_c_re, s5_c_im, s5_d, s5_w_glu, gla_w_in, gla_w_gate2, gla_b_gate, gla_g_norm, gla_w_out, w_ff1, w_ff2, norm_final, loss_target, m_w_ada, m_b_ada, m_norm_mix, m_norm_mlp, m_s5_a_re, m_s5_a_im, m_s5_log_dt, m_s5_b_re, m_s5_b_im, m_s5_c_re, m_s5_c_im, m_s5_d, m_s5_w_glu, m_gla_w_in, m_gla_w_gate2, m_gla_b_gate, m_gla_g_norm, m_gla_w_out, m_w_ff1, m_w_ff2, m_norm_final, v_w_ada, v_b_ada, v_norm_mix, v_norm_mlp, v_s5_a_re, v_s5_a_im, v_s5_log_dt, v_s5_b_re, v_s5_b_im, v_s5_c_re, v_s5_c_im, v_s5_d, v_s5_w_glu, v_gla_w_in, v_gla_w_gate2, v_gla_b_gate, v_gla_g_norm, v_gla_w_out, v_w_ff1, v_w_ff2, v_norm_final):
    args = dict(locals())
    L, D = x.shape[1], x.shape[2]
    QK = D // 2
    xi, yi, ci = _coords()
    chip = 2 * xi + yi
    dev = 2 * chip + ci

    cat = jnp.concatenate([gla_w_gate2[0].reshape(1, -1), gla_b_gate, gla_g_norm], axis=1)
    cpair, csrc, cout, ctoken = _exchange_start(
        jnp.concatenate([c.reshape(8, D // 8), jnp.tile(cat, (8, 1))], axis=1), MASK_ALL, name="gather_c_start")

    big = [("s5_w_glu", s5_w_glu, 0, "col"), ("gla_w_in", gla_w_in, 0, "slot"), ("gla_w_out", gla_w_out, 0, "row"),
           ("w_ff1_0", w_ff1, 0, "col"), ("w_ff1_1", w_ff1, 1, "col"), ("w_ff2_0", w_ff2, 0, "row"), ("w_ff2_1", w_ff2, 1, "row")]
    own16 = [_cast_into(t, lead, kind, chip, ctoken, name=f"cast_{nm}") for nm, t, lead, kind in big]
    wshapes, wkinds = [b[1].shape[-2:] for b in big], [b[3] for b in big]
    wgroups = [[0], [3, 5], [1, 2, 4, 6]]
    wsems, wthru, wtoken = _gw_start(own16, wshapes, wkinds, wgroups, name="gather_w_start")
    W = {}

    csrc, cout = _exchange_wait(cpair, csrc, cout, MASK_ALL, wtoken, name="gather_c_wait")
    first = lax.dynamic_update_index_in_dim(cout, csrc, dev, 0)
    c_all = first[:, :, :D // 8].reshape(8, D)
    cat_all = first[0::2, 0, D // 8:]
    acols = w_ada.shape[2]
    b_cols = lax.dynamic_slice_in_dim(b_ada, chip * acols, acols, axis=1)[:, None, :]
    mod_cols = _mod_cols(c_all, w_ada, b_cols, name="ada_mod")
    mod_all = _exchange(mod_cols.reshape(16, acols), MASK_CHIPS, name="gather_mod")
    mod_all = mod_all.reshape(NCH, 2, 8, acols).transpose(1, 2, 0, 3).reshape(2, 8, NCH * acols)
    mod = lax.dynamic_index_in_dim(mod_all, dev, axis=1, keepdims=False).reshape(2, 6, 1, D)

    def weights_landed(g, after):
        idx = wgroups[g]
        shp, knd = [wshapes[p] for p in idx], [wkinds[p] for p in idx]
        return _gw_wait([wthru[p] for p in idx], shp, knd, wsems[g], after, name=f"gather_w_wait{g}"), shp, knd

    def weights_ready(g, arrs):
        for p, w in zip(wgroups[g], arrs):
            W[big[p][0]] = w

    def forward_start(g, after):
        got, shp, knd = weights_landed(g, after)
        pair, thru, token = _gw_forward_start(got, shp, knd, name=f"gather_w_cores_start{g}")
        return (pair, thru, shp, knd), token

    def forward_finish(g, state, after):
        pair, thru, shp, knd = state
        weights_ready(g, _gw_forward_wait(pair, thru, shp, knd, after, name=f"gather_w_cores_wait{g}"))

    qk4 = QK // NCH
    wg2 = cat_all[:, :GATE_RANK * qk4].reshape(NCH, GATE_RANK, qk4).transpose(1, 0, 2).reshape(GATE_RANK, QK)
    bg = cat_all[:, GATE_RANK * qk4:(GATE_RANK + 1) * qk4].reshape(1, QK)
    gn = cat_all[:, (GATE_RANK + 1) * qk4:].reshape(1, D)
    wg2p = jnp.concatenate([wg2, jnp.zeros((128 - GATE_RANK, QK), F32)], axis=0).astype(BF16)

    lb_re, lb_im, bb_re, bb_im = _s5_disc(s5_a_re[0], s5_a_im[0], s5_log_dt[0], s5_b_re[0], s5_b_im[0])
    tb = _s5_tables(lb_re, lb_im, bb_re, bb_im, s5_c_re[0], s5_c_im[0], min(S5_R, L) // NSEG)
    s5_dv = s5_d + wtoken[0, 0]

    def vec(t):
        return t.reshape(1, -1)

    m0, m1 = mod[0], mod[1]
    xp = _to_segments(x[0])
    (u0,) = _rows(lambda t, g, sc, sh: (_norm_mod(t, g, sc, sh),), [xp], [vec(norm_mix[0]), m0[1], m0[0]],
                  [(D, F32)], [], name="pre_mix0")
    y0, z0, ck0 = _s5_fwd(u0, tb, s5_dv, name="s5_fwd")
    got, shp, knd = weights_landed(0, z0)
    weights_ready(0, _gw_forward(got, shp, knd, name="gather_w_cores0"))
    vg0 = _matmul(z0, W["s5_w_glu"], name="glu", tn=2048)
    fwd1, ftok1 = forward_start(1, vg0)

    def res_glu_pre(xt, vgt, gt, g, sc, sh):
        xn = xt + gt * (vgt[:, :D] * _sigmoid(vgt[:, D:]))
        return xn, _norm_mod(xn, g, sc, sh)

    x2_0, h2_0 = _rows(res_glu_pre, [xp, vg0], [m0[2] + ftok1[0, 0], vec(norm_mlp[0]), m0[4], m0[3]],
                       [(D, F32), (D, BF16)], [], name="res_mix0")
    forward_finish(1, fwd1, h2_0)
    a_0, f0 = _mlp_fwd(h2_0, W["w_ff1_0"], W["w_ff2_0"], "0")
    fwd2, ftok2 = forward_start(2, f0)

    def res_pre(xt, bt, gt, g, sc, sh):
        xn = xt + gt * bt
        return xn, _norm_mod(xn, g, sc, sh)

    x3p, h1p = _rows(res_pre, [x2_0, f0], [m0[5] + ftok2[0, 0], vec(norm_mix[1]), m1[1], m1[0]],
                     [(D, F32), (D, BF16)], [], name="res_mlp0")
    x3, h1 = _from_segments(x3p), _from_segments(h1p)
    forward_finish(2, fwd2, h1)
    in_fwd, in_bwd = _w_in_layout(QK, D, gla_w_in.shape[2])
    w_in_r = jnp.concatenate([W["gla_w_in"][j, :, lo:hi] for j, lo, hi in in_fwd]
                             + [jnp.zeros((D, 128 - GATE_RANK), BF16)], axis=1)
    proj = _matmul(h1, w_in_r, name="gla_in", tm=2048, tn=640)
    og, states = _gla_fwd(proj, wg2p, bg, gn, name="gla_fwd")
    ymix = _matmul(og, W["gla_w_out"], name="gla_out")
    x2_1, h2_1 = _rows(res_pre, [x3, ymix], [m1[2], vec(norm_mlp[1]), m1[4], m1[3]], [(D, F32), (D, BF16)], [],
                       name="res_mix1")
    a_1, f1 = _mlp_fwd(h2_1, W["w_ff1_1"], W["w_ff2_1"], "1")

    def final(xt, ft, tgt, gt, g):
        xn = xt + gt * ft
        rs = lax.rsqrt(jnp.mean(xn * xn, axis=-1, keepdims=True) + EPS)
        xh = xn * rs
        e = xh * g - tgt
        dout = e * (1.0 / D)
        dxh = dout * g
        dx = rs * (dxh - xh * jnp.mean(dxh * xh, axis=-1, keepdims=True))
        lsum = 0.5 * jnp.sum(jnp.sum(e * e, axis=-1, keepdims=True), axis=0, keepdims=True) * (1.0 / D)
        return dx, dx * gt, jnp.broadcast_to(lsum, (1, 128)), _rsum(dout * xh), _rsum(dx * ft)

    dx, df1, loss_part, d_norm_final, dgt2_1 = _rows(
        final, [x2_1, f1, loss_target[0]], [m1[5], vec(norm_final)], [(D, F32), (D, BF16)],
        [(1, 128), (1, D), (1, D)], name="loss_head")

    def gate_bwd(dxt, bt, gt):
        return dxt * gt, _rsum(dxt * bt)

    def norm_bwd(xt, dht, drt, g, sc):
        dxn, dsh, dsc, dg = _norm_mod_bwd(xt, dht, g, sc)
        return drt + dxn, dsh, dsc, dg

    def norm_gate_bwd(xt, dht, drt, bt, g, sc, gt):
        dxn, dsh, dsc, dg = _norm_mod_bwd(xt, dht, g, sc)
        dxt = drt + dxn
        return dxt, dxt * gt, dsh, dsc, dg, _rsum(dxt * bt)

    vD = [(1, D)]
    dh2_1, dw_ff1_1, dw_ff2_1 = _mlp_bwd(df1, h2_1, a_1, W["w_ff1_1"], W["w_ff2_1"], "1")
    dx, dmix1, dsh2_1, dsc2_1, dg_mlp1, dgt1_1 = _rows(
        norm_gate_bwd, [x2_1, dh2_1, dx, ymix], [vec(norm_mlp[1]), m1[4], m1[2]], [(D, F32), (D, BF16)], vD * 4,
        name="norm_mlp1_bwd")
    dog = _matmul(dmix1, W["gla_w_out"], tb=True, name="gla_out_dx")
    dw_out = _matmul(og, dmix1, ta=True, name="gla_out_dw")
    dproj, dwg2p, dbg, dgn = _gla_bwd(proj, dog, states, wg2p, bg, gn, name="gla_bwd")
    dh1 = _matmul(dproj, w_in_r, tb=True, name="gla_in_dx", tk=3200)
    dw_in_r = _matmul(h1, dproj, ta=True, name="gla_in_dw", tn=640)
    dx, dsh1_1, dsc1_1, dg_mix1 = _rows(norm_bwd, [x3, dh1, dx], [vec(norm_mix[1]), m1[1]], [(D, F32)], vD * 3,
                                        name="norm_mix1_bwd")
    dxp = _to_segments(dx)
    tags = [b[0] for b in big] + ["small"]
    rs_groups = []

    def rs_chips_begin(idx, srcs, r1, gname):
        s1 = [_sum_own_half(g, r, ci, F32 if tags[k] == "small" else BF16, name=f"rs_sum_cores_{tags[k]}")
              for g, r, k in zip(srcs, r1, idx)]
        pair, parts, lands, token = _rs_chips_start(s1, name=f"rs_chips_start_{gname}")
        rs_groups.append((idx, pair, parts, lands))
        return token

    def rs_begin(idx, srcs, gname):
        return rs_chips_begin(idx, srcs, _rs_cores(srcs, name=f"rs_cores_{gname}"), gname)

    dw_in = jnp.stack([jnp.concatenate([dw_in_r[:, lo:hi] for lo, hi in in_bwd[j]], axis=1) for j in range(NCH)])
    idx1 = [1, 2, 4, 6]
    pair1, src1, land1, tok1 = _rs_cores_start(
        [dw_in, dw_out.reshape(NCH, -1, D), dw_ff1_1, dw_ff2_1.reshape(NCH, -1, D)], name="rs_cores_start_l1")

    df0, dgt2_0 = _rows(gate_bwd, [dxp, f0], [m0[5] + tok1[0, 0]], [(D, BF16)], vD, name="gate_mlp0")
    dh2_0, dw_ff1_0, dw_ff2_0 = _mlp_bwd(df0, h2_0, a_0, W["w_ff1_0"], W["w_ff2_0"], "0")
    src1, land1 = _rs_cores_wait(pair1, src1, land1, dh2_0, name="rs_cores_wait_l1")
    tok1b = rs_chips_begin(idx1, src1, land1, "l1")
    idx0 = [3, 5]
    pair0, src0, land0, tok0 = _rs_cores_start([dw_ff1_0, dw_ff2_0.reshape(NCH, -1, D)], name="rs_cores_start_l0")
    tok2 = tok1b + tok0

    def norm_glu_bwd(xt, dht, drt, vgt, g, sc, gt):
        dxn, dsh, dsc, dg = _norm_mod_bwd(xt, dht, g, sc)
        dxt = drt + dxn
        val, sg = vgt[:, :D], _sigmoid(vgt[:, D:])
        dbr = dxt * gt
        dvg = jnp.concatenate([dbr * sg, dbr * val * sg * (1.0 - sg)], axis=1)
        return dxt, dvg, dsh, dsc, dg, _rsum(dxt * val * sg)

    dxp, dvg0, dsh2_0, dsc2_0, dg_mlp0, dgt1_0 = _rows(
        norm_glu_bwd, [x2_0, dh2_0, dxp, vg0], [vec(norm_mlp[0]), m0[4] + tok2[0, 0], m0[2]], [(D, F32), (2 * D, BF16)],
        vD * 4, name="norm_mlp0_bwd")
    dz0 = _matmul(dvg0, W["s5_w_glu"], tb=True, name="glu_dx")
    dw_glu = _matmul(z0, dvg0, ta=True, name="glu_dw", tn=512, col_shards=NCH)
    src0, land0 = _rs_cores_wait(pair0, src0, land0, dw_glu, name="rs_cores_wait_l0")
    tok0b = rs_chips_begin(idx0 + [0], src0 + [dw_glu], land0 + list(_rs_cores([dw_glu], name="rs_cores_glu")), "l0")
    du0, db_acc, dc_acc, dl_acc, dd_s5 = _s5_bwd(u0, y0, dz0, ck0, tb, s5_dv + tok0b[0, 0], name="s5_bwd")
    dxp, dsh1_0, dsc1_0, dg_mix0 = _rows(norm_bwd, [xp, du0, dxp], [vec(norm_mix[0]), m0[1]], [(D, F32)], vD * 3,
                                         name="norm_mix0_bwd")
    grad_x = _from_segments(dxp)[None]

    dmod = jnp.concatenate([dsh1_0, dsc1_0, dgt1_0, dsh2_0, dsc2_0, dgt2_0,
                            dsh1_1, dsc1_1, dgt1_1, dsh2_1, dsc2_1, dgt2_1], axis=1)
    dbb_re, dbb_im = _s5_untable(db_acc)
    dc_re, dc_im_neg = _s5_untable(dc_acc)
    nbk = D // 128
    dl = dl_acc.reshape(nbk, NSEG, 2, GPB * S5_P).sum(axis=1)
    smalls = [dmod, dg_mix0, dg_mix1, dg_mlp0, dg_mlp1, d_norm_final, dd_s5, dbg, dgn,
              dwg2p[:GATE_RANK].reshape(1, -1), dbb_re.reshape(1, -1), dbb_im.reshape(1, -1),
              dc_re.reshape(1, -1), dc_im_neg.reshape(1, -1), dl.reshape(1, -1), loss_part]
    ssz = [t.shape[1] for t in smalls]
    stot = sum(ssz)
    spad = -(-stot // 8192) * 8192
    svec = jnp.concatenate(smalls + [jnp.zeros((1, spad - stot), F32)], axis=1).reshape(NCH, spad // (128 * NCH), 128)


    rs_begin([7], [svec], "last")
    landed = _rs_chips_wait([(g[1], g[2], g[3]) for g in rs_groups], grad_x, name="rs_chips_wait")
    s1, r2 = {}, {}
    for (idx, _, _, _), (parts, lands) in zip(rs_groups, landed):
        for k, part, land in zip(idx, parts, lands):
            s1[k], r2[k] = part, land

    def fin(k, **kw):
        return _sum_chips(r2[k], s1[k], chip, ci, name=f"rs_sum_chips_{tags[k]}", **kw)

    f_ff1 = fin(4, nlead=2, lead=1, prev=fin(3, nlead=2, lead=0))
    f_ff2 = fin(6, nlead=2, lead=1, prev=fin(5, nlead=2, lead=0))
    finals = [fin(0), fin(1), fin(2), f_ff1, f_ff2, fin(7, spread=True)]
    halves = [t.shape[1] for t in (s1[0], s1[1], s1[2], s1[3], s1[5], s1[7])]
    gpair, gthru, gtok = _rs_gather_start(finals, halves, [False] * 5 + [True], name="rs_gather_cores_start")

    dmod_all = _exchange((dmod + gtok[0, 0]).reshape(12 * D // 128, 128), MASK_ALL, name="gather_dmod").reshape(8, 2, 6 * D)
    dmod_cols = lax.dynamic_slice_in_dim(dmod_all, chip * acols, acols, axis=2).transpose(1, 0, 2)
    g_w_ada = _ada_grad(c_all, dmod_cols, name="ada_grad")
    upd_w_ada = _adamw(w_ada, g_w_ada, m_w_ada, v_w_ada, name="adamw_w_ada")

    g_glu, g_in, g_out, g_w_ff1, g_w_ff2, s_own = _rs_gather_wait(
        gpair, gthru, halves, [False] * 5 + [True], upd_w_ada[0], name="rs_gather_cores_wait")
    srows = spad // (128 * NCH)
    (s_sum,) = _gather_weights([s_own.reshape(NCH * srows, 128)], [(srows, 128)], ["row"], name="gather_small_grads")
    s_sum = s_sum.reshape(-1)
    so = [sum(ssz[:k]) for k in range(len(ssz))]
    sm = [s_sum[o:o + n] for o, n in zip(so, ssz)]
    (dmod_s, g_mix0, g_mix1, g_mlp0, g_mlp1, g_nf, g_d, g_bg, g_gn, g_wg2, g_bbre, g_bbim, g_cre, g_cimn, g_dl, loss_s) = sm
    loss = loss_s[0]
    g_b_ada = dmod_s.reshape(2, 6 * D)

    G = D // S5_H
    _, disc_vjp = jax.vjp(_s5_disc, s5_a_re[0], s5_a_im[0], s5_log_dt[0], s5_b_re[0], s5_b_im[0])
    g_dl = g_dl.reshape(nbk, 2, GPB, S5_P)
    ct = (g_dl[:, 0].reshape(G, S5_P), g_dl[:, 1].reshape(G, S5_P),
          g_bbre.reshape(G, S5_H, S5_P).transpose(0, 2, 1), g_bbim.reshape(G, S5_H, S5_P).transpose(0, 2, 1))
    g_a_re, g_a_im, g_log_dt, g_b_re, g_b_im = disc_vjp(ct)
    g_c_re = g_cre.reshape(G, S5_H, S5_P)
    g_c_im = -g_cimn.reshape(G, S5_H, S5_P)
    g_wg2_s = lax.dynamic_slice_in_dim(g_wg2.reshape(GATE_RANK, QK), chip * qk4, qk4, axis=1)
    g_bg_s = lax.dynamic_slice_in_dim(g_bg.reshape(1, QK), chip * qk4, qk4, axis=1)
    g_gn_s = lax.dynamic_slice_in_dim(g_gn.reshape(1, D), chip * (D // NCH), D // NCH, axis=1)

    grads = dict(
        w_ada=g_w_ada, b_ada=g_b_ada, norm_mix=jnp.stack([g_mix0, g_mix1]), norm_mlp=jnp.stack([g_mlp0, g_mlp1]),
        s5_a_re=g_a_re[None], s5_a_im=g_a_im[None], s5_log_dt=g_log_dt[None], s5_b_re=g_b_re[None], s5_b_im=g_b_im[None],
        s5_c_re=g_c_re[None], s5_c_im=g_c_im[None], s5_d=g_d[None], s5_w_glu=g_glu,
        gla_w_in=g_in, gla_w_gate2=g_wg2_s[None], gla_b_gate=g_bg_s, gla_g_norm=g_gn_s,
        gla_w_out=g_out, w_ff1=g_w_ff1, w_ff2=g_w_ff2, norm_final=g_nf)

    names = list(grads)
    large = ("w_ada", "s5_w_glu", "gla_w_in", "gla_w_out", "w_ff1", "w_ff2")
    delta, new_m, new_v = {}, {}, {}
    delta["w_ada"], new_m["w_ada"], new_v["w_ada"] = upd_w_ada
    for nm in large[1:]:
        delta[nm], new_m[nm], new_v[nm] = _adamw(args[nm], grads[nm], args["m_" + nm], args["v_" + nm], name=f"adamw_{nm}")
    grads = {nm: grads[nm].reshape(args[nm].shape) for nm in names}
    for nm in names:
        if nm not in large:
            shp = args[nm].shape
            as2d = (1, -1) if len(shp) == 1 else shp
            outs = _adamw_whole(*[t.reshape(as2d) for t in (args[nm], grads[nm], args["m_" + nm], args["v_" + nm])],
                                name=f"adamw_{nm}")
            delta[nm], new_m[nm], new_v[nm] = (t.reshape(shp) for t in outs)
    return (loss, grad_x, *[grads[n] for n in names], *[delta[n] for n in names], *[new_m[n] for n in names],
            *[new_v[n] for n in names])
```

```python
import math

import jax
import jax.numpy as jnp
from jax import lax
from jax.experimental import pallas as pl
from jax.experimental.pallas import tpu as pltpu

F32 = jnp.float32
BF16 = jnp.bfloat16
MESH = pl.DeviceIdType.MESH

EPS = 1e-6
CHUNK = 64
GLA_NB = 4
S5_H = 16
S5_P = 64
GPB = 8
NSEG = 8
HEADS = 4
GATE_RANK = 16
GATE_TAU = 16.0
NCH = 4
LR, B1, B2, AEPS, WD, ASTEP = 0.001, 0.9, 0.999, 1e-08, 0.01, 10
VMEM_LIMIT = 56 << 20
ROW_SUB = 64

MASK_CHIPS = ((1, 0, 0), (0, 1, 0), (1, 1, 0))
MASK_ALL = ((0, 0, 1), (0, 1, 0), (0, 1, 1), (1, 0, 0), (1, 0, 1), (1, 1, 0), (1, 1, 1))


def _params(*sem):
    return pltpu.CompilerParams(dimension_semantics=sem or None, vmem_limit_bytes=VMEM_LIMIT)


def _tile_rows(rows, cap=512):
    best = 8
    for t in range(8, cap + 1, 8):
        if rows % t == 0:
            best = t
    return best


def _whole(shape):
    return pl.BlockSpec(shape, lambda i, _n=len(shape): (0,) * _n)


def _matmul(a, b, *, name, ta=False, tb=False, tm=1024, tn=1024, tk=4096, out_dtypes=(F32,),
            a_fn=None, epi=None, epi_ins=(), col_shards=1):
    M, K = (a.shape[1], a.shape[0]) if ta else a.shape
    N = b.shape[0] if tb else b.shape[1]
    tm, tn, tk = min(tm, M), min(tn, N), min(tk, K)
    assert M % tm == 0 and N % tn == 0 and K % tk == 0, (name, M, N, K)
    nk = K // tk
    ne = len(epi_ins)
    dn = (((0 if ta else 1,), (1 if tb else 0,)), ((), ()))

    def body(a_ref, b_ref, *rest):
        e_refs, o_refs = rest[:ne], rest[ne:ne + len(out_dtypes)]
        at = a_ref[...]
        if a_fn is not None:
            at = a_fn(at)
        part = lax.dot_general(at.astype(BF16), b_ref[...].astype(BF16), dn, preferred_element_type=F32)

        def finish(total):
            outs = (total,) if epi is None else epi(total, *[r[...] for r in e_refs])
            for r, o in zip(o_refs, outs):
                r[...] = o.astype(r.dtype)

        if nk == 1:
            finish(part)
            return
        acc = rest[-1]
        k = pl.program_id(2)

        @pl.when(k == 0)
        def _():
            acc[...] = part

        @pl.when(k > 0)
        def _():
            acc[...] += part

        @pl.when(k == nk - 1)
        def _():
            finish(acc[...])

    a_spec = pl.BlockSpec((tk, tm), lambda i, j, k: (k, i)) if ta else pl.BlockSpec((tm, tk), lambda i, j, k: (i, k))
    b_spec = pl.BlockSpec((tn, tk), lambda i, j, k: (j, k)) if tb else pl.BlockSpec((tk, tn), lambda i, j, k: (k, j))
    o_spec = pl.BlockSpec((tm, tn), lambda i, j, k: (i, j))
    if col_shards > 1:
        per = N // col_shards // tn
        assert ne == 0 and per * tn * col_shards == N
        w_spec = pl.BlockSpec((None, tm, tn), lambda i, j, k: (j // per, i, j % per))
        o_shape = (col_shards, M, N // col_shards)
    else:
        w_spec, o_shape = o_spec, (M, N)
    outs = pl.pallas_call(
        body, name=name, grid=(M // tm, N // tn, nk),
        in_specs=[a_spec, b_spec] + [o_spec] * ne,
        out_specs=[w_spec] * len(out_dtypes),
        out_shape=[jax.ShapeDtypeStruct(o_shape, d) for d in out_dtypes],
        scratch_shapes=[pltpu.VMEM((tm, tn), F32)] if nk > 1 else [],
        compiler_params=_params("parallel", "parallel", "arbitrary"),
    )(a, b, *epi_ins)
    return outs[0] if len(outs) == 1 else outs


def _rows(fn, rows_in, vecs_in, rows_out, acc_out, *, name, tm=512):
    L = rows_in[0].shape[0]
    tm = min(tm, L)
    assert L % tm == 0
    nr, nv, no, na = len(rows_in), len(vecs_in), len(rows_out), len(acc_out)

    sub = ROW_SUB if tm % ROW_SUB == 0 else tm

    def body(*refs):
        rin, vin = refs[:nr], refs[nr:nr + nv]
        rout, aout = refs[nr + nv:nr + nv + no], refs[nr + nv + no:]
        if na:
            @pl.when(pl.program_id(0) == 0)
            def _():
                for r in aout:
                    r[...] = jnp.zeros_like(r)

        vecs = [v[...] for v in vin]
        sums = None
        for s in range(tm // sub):
            rows = pl.ds(s * sub, sub)
            outs = fn(*[r[rows, :] for r in rin], *vecs)
            for r, o in zip(rout, outs[:no]):
                r[rows, :] = o.astype(r.dtype)
            sums = list(outs[no:]) if sums is None else [t + o for t, o in zip(sums, outs[no:])]
        for r, t in zip(aout, sums):
            r[...] += t

    outs = pl.pallas_call(
        body, name=name, grid=(L // tm,),
        in_specs=[pl.BlockSpec((tm, r.shape[1]), lambda i: (i, 0)) for r in rows_in] + [_whole(v.shape) for v in vecs_in],
        out_specs=[pl.BlockSpec((tm, c), lambda i: (i, 0)) for c, _ in rows_out] + [_whole(s) for s in acc_out],
        out_shape=[jax.ShapeDtypeStruct((L, c), d) for c, d in rows_out] + [jax.ShapeDtypeStruct(s, F32) for s in acc_out],
        compiler_params=_params("arbitrary"),
    )(*rows_in, *vecs_in)
    return outs


def _rsum(t):
    return jnp.sum(t, axis=0, keepdims=True)


def _norm_mod(x, g, sc, sh):
    rs = lax.rsqrt(jnp.mean(x * x, axis=-1, keepdims=True) + EPS)
    return x * rs * g * (1.0 + sc) + sh


def _norm_mod_bwd(x, dh, g, sc):
    rs = lax.rsqrt(jnp.mean(x * x, axis=-1, keepdims=True) + EPS)
    xh = x * rs
    dn = dh * (1.0 + sc)
    dxh = dn * g
    dx = rs * (dxh - xh * jnp.mean(dxh * xh, axis=-1, keepdims=True))
    return dx, _rsum(dh), _rsum(dh * xh * g), _rsum(dn * xh)


def _sigmoid(x):
    return jax.nn.sigmoid(x)


def _gelu(y):
    return jax.nn.gelu(y, approximate=True)


def _gelu_grad(y):
    c = math.sqrt(2.0 / math.pi)
    t = jnp.tanh(c * (y + 0.044715 * y * y * y))
    return 0.5 * (1.0 + t) + 0.5 * y * (1.0 - t * t) * c * (1.0 + 3.0 * 0.044715 * y * y)


def _s5_tables(lb_re, lb_im, bb_re, bb_im, c_re, c_im, seg_len):
    G = lb_re.shape[0]
    nb = G // GPB
    eye = jnp.eye(GPB, dtype=F32)

    def bdiag(t):
        a, b = t.shape[1:]
        t = t.reshape(nb, GPB, a, b)
        return (t[:, :, :, None, :] * eye[None, :, None, :, None]).reshape(nb, GPB * a, GPB * b)

    bbd = jnp.concatenate([bdiag(bb_re.transpose(0, 2, 1)), bdiag(bb_im.transpose(0, 2, 1))], axis=2)
    cbd = jnp.concatenate([bdiag(c_re.transpose(0, 2, 1)), -bdiag(c_im.transpose(0, 2, 1))], axis=1)

    def lanes(re, im):
        t = jnp.concatenate([re.reshape(nb, GPB * S5_P), im.reshape(nb, GPB * S5_P)], axis=1)
        return jnp.repeat(t, NSEG, axis=0)

    tr, ti = lb_re, lb_im
    for _ in range(int(math.log2(seg_len))):
        tr, ti = tr * tr - ti * ti, 2.0 * tr * ti
    return dict(bbd=bbd.astype(BF16), bbdT=bbd.transpose(0, 2, 1).astype(BF16), cbd=cbd.astype(BF16),
                cbdT=cbd.transpose(0, 2, 1).astype(BF16), lam=lanes(lb_re, lb_im), lamT=lanes(tr, ti))


def _s5_untable(acc):
    nb = acc.shape[0]
    t = acc.reshape(nb, GPB, S5_H, 2, GPB, S5_P)
    d = jnp.diagonal(t, axis1=1, axis2=4)
    d = d.transpose(0, 4, 2, 1, 3).reshape(nb * GPB, 2, S5_H, S5_P)
    return d[:, 0], d[:, 1]


S5_R = 256


def _s5_carries(ends, first, t_ref, rws, SW, cfx, *, reverse):
    er, ei = ends
    tr, ti = t_ref[rws, :SW][0:1], t_ref[rws, SW:][0:1]
    cr, ci = first
    order = range(NSEG - 1, -1, -1) if reverse else range(NSEG)
    for n, s in enumerate(order):
        if n > 0:
            p = s + 1 if reverse else s - 1
            if reverse:
                cr, ci = tr * cr + ti * ci + er[p:p + 1], tr * ci - ti * cr + ei[p:p + 1]
            else:
                cr, ci = tr * cr - ti * ci + er[p:p + 1], tr * ci + ti * cr + ei[p:p + 1]
        cfx[s:s + 1, :SW] = cr
        cfx[s:s + 1, SW:] = ci


def _s5_fwd(up, tb, dvec, *, name):
    L, D = up.shape
    R = min(S5_R, L)
    nb, ta, ngb = L // R, R // NSEG, D // 128
    SW = GPB * S5_P
    crows = ngb * NSEG

    def body(u_ref, lam_ref, t_ref, b_ref, c_ref, d_ref, y_ref, z_ref, ck_ref, carry, xbuf2, cfx2):
        @pl.when(pl.program_id(0) == 0)
        def _():
            carry[...] = jnp.zeros_like(carry)

        zero = jnp.zeros((NSEG, SW), F32)
        for g0 in range(0, ngb, 2):
            pair = (g0, g0 + 1)
            xb = [xbuf2.at[g % 4] for g in pair]
            cf = [cfx2.at[g % 4] for g in pair]
            cols = [slice(g * 128, (g + 1) * 128) for g in pair]
            rws = [slice(g * NSEG, (g + 1) * NSEG) for g in pair]
            ug = [u_ref[:, cols[q]] for q in range(2)]
            for q in range(2):
                xb[q][...] = jnp.dot(ug[q].astype(BF16), b_ref[pair[q]], preferred_element_type=F32)
            lam = [(lam_ref[rws[q], :SW], lam_ref[rws[q], SW:]) for q in range(2)]

            def scan(c, store, xb=xb, lam=lam):
                c = list(c)
                for a in range(ta):
                    o = slice(a * NSEG, (a + 1) * NSEG)
                    for q in range(2):
                        (lr, li), (cr, ci) = lam[q], c[q]
                        nr = lr * cr - li * ci + xb[q][o, :SW]
                        ni = lr * ci + li * cr + xb[q][o, SW:]
                        if store:
                            xb[q][o, :SW] = nr
                            xb[q][o, SW:] = ni
                        c[q] = (nr, ni)
                return c

            ends = scan([(zero, zero)] * 2, False)
            for q in range(2):
                prev = (carry[rws[q], :SW][NSEG - 1:NSEG], carry[rws[q], SW:][NSEG - 1:NSEG])
                _s5_carries(ends[q], prev, t_ref, rws[q], SW, cf[q], reverse=False)
                ck_ref[0, rws[q], :] = cf[q][...]
            fin = scan([(cf[q][:, :SW], cf[q][:, SW:]) for q in range(2)], True)
            for q in range(2):
                carry[rws[q], :SW] = fin[q][0]
                carry[rws[q], SW:] = fin[q][1]
            for q in range(2):
                y = (jnp.dot(xb[q][...].astype(BF16), c_ref[pair[q]], preferred_element_type=F32)
                     + d_ref[:, cols[q]] * ug[q])
                y_ref[:, cols[q]] = y
                z_ref[:, cols[q]] = _gelu(y).astype(BF16)

    rowblk = pl.BlockSpec((R, D), lambda i: (i, 0))
    return pl.pallas_call(
        body, name=name, grid=(nb,),
        in_specs=[rowblk, _whole(tb["lam"].shape), _whole(tb["lamT"].shape), _whole(tb["bbd"].shape),
                  _whole(tb["cbd"].shape), _whole(dvec.shape)],
        out_specs=[rowblk, rowblk, pl.BlockSpec((1, crows, 2 * SW), lambda i: (i, 0, 0))],
        out_shape=[jax.ShapeDtypeStruct((L, D), F32), jax.ShapeDtypeStruct((L, D), BF16),
                   jax.ShapeDtypeStruct((nb, crows, 2 * SW), F32)],
        scratch_shapes=[pltpu.VMEM((crows, 2 * SW), F32), pltpu.VMEM((4, R, 2 * SW), F32),
                        pltpu.VMEM((4, NSEG, 2 * SW), F32)],
        compiler_params=_params("arbitrary"),
    )(up, tb["lam"], tb["lamT"], tb["bbd"], tb["cbd"], dvec)


def _s5_bwd(up, y, dz, ck, tb, dvec, *, name):
    L, D = up.shape
    R = min(S5_R, L)
    nb, ta, ngb = L // R, R // NSEG, D // 128
    SW = GPB * S5_P
    crows = ngb * NSEG

    def body(u_ref, y_ref, dz_ref, ck_ref, lam_ref, t_ref, b_ref, bt_ref, ct_ref, d_ref,
             du_ref, db_ref, dc_ref, dl_ref, dd_ref, gcarry, xbuf2, gbuf2, dybuf, cfx2):
        @pl.when(pl.program_id(0) == 0)
        def _():
            gcarry[...] = jnp.zeros_like(gcarry)
            db_ref[...] = jnp.zeros_like(db_ref)
            dc_ref[...] = jnp.zeros_like(dc_ref)
            dl_ref[...] = jnp.zeros_like(dl_ref)
            dd_ref[...] = jnp.zeros_like(dd_ref)

        zero = jnp.zeros((NSEG, SW), F32)
        dybuf[...] = dz_ref[...] * _gelu_grad(y_ref[...])
        for g0 in range(0, ngb, 2):
            pair = (g0, g0 + 1)
            xb = [xbuf2.at[g % 4] for g in pair]
            gbf = [gbuf2.at[g % 4] for g in pair]
            cf = [cfx2.at[g % 4] for g in pair]
            cols = [slice(g * 128, (g + 1) * 128) for g in pair]
            rws = [slice(g * NSEG, (g + 1) * NSEG) for g in pair]
            dyg = [dybuf[:, cols[q]] for q in range(2)]
            ug = [u_ref[:, cols[q]] for q in range(2)]
            lam = [(lam_ref[rws[q], :SW], lam_ref[rws[q], SW:]) for q in range(2)]
            for q in range(2):
                gbf[q][...] = jnp.dot(dyg[q].astype(BF16), ct_ref[pair[q]], preferred_element_type=F32)
                xb[q][0:NSEG, :] = ck_ref[0, rws[q], :]
                xb[q][NSEG:, :] = jnp.dot(ug[q].astype(BF16), b_ref[pair[q]], preferred_element_type=F32)

            c = [(xb[q][0:NSEG, :SW], xb[q][0:NSEG, SW:]) for q in range(2)]
            for a in range(ta):
                o = slice((a + 1) * NSEG, (a + 2) * NSEG)
                for q in range(2):
                    (lr, li), (cr, ci) = lam[q], c[q]
                    nr = lr * cr - li * ci + xb[q][o, :SW]
                    ni = lr * ci + li * cr + xb[q][o, SW:]
                    xb[q][o, :SW] = nr
                    xb[q][o, SW:] = ni
                    c[q] = (nr, ni)

            def rscan(c, store, qs=(0, 1), xb=xb, gbf=gbf, lam=lam):
                c = list(c)
                for a in range(ta - 1, -1, -1):
                    o = slice(a * NSEG, (a + 1) * NSEG)
                    for q in qs:
                        lr, li = lam[q]
                        gr = gbf[q][o, :SW] + lr * c[q][0] + li * c[q][1]
                        gi = gbf[q][o, SW:] - li * c[q][0] + lr * c[q][1]
                        if store:
                            gbf[q][o, :SW] = gr
                            gbf[q][o, SW:] = gi
                        c[q] = (gr, gi)
                return c

            gends = rscan([(zero, zero)] * 2, False)
            for q in range(2):
                nxt = (gcarry[rws[q], :SW][0:1], gcarry[rws[q], SW:][0:1])
                _s5_carries(gends[q], nxt, t_ref, rws[q], SW, cf[q], reverse=True)
            fin = rscan([(cf[q][:, :SW], cf[q][:, SW:]) for q in range(2)], True)
            for q in range(2):
                gcarry[rws[q], :SW] = fin[q][0]
                gcarry[rws[q], SW:] = fin[q][1]
            for q in range(2):
                sr, si = zero, zero
                for a in range(ta):
                    o = slice(a * NSEG, (a + 1) * NSEG)
                    gr, gi = gbf[q][o, :SW], gbf[q][o, SW:]
                    xr, xi = xb[q][o, :SW], xb[q][o, SW:]
                    sr = sr + gr * xr + gi * xi
                    si = si + gi * xr - gr * xi
                dl_ref[rws[q], :SW] += sr
                dl_ref[rws[q], SW:] += si
            gb16 = [gbf[q][...].astype(BF16) for q in range(2)]
            for q in range(2):
                du_ref[:, cols[q]] = (jnp.dot(gb16[q], bt_ref[pair[q]], preferred_element_type=F32)
                                      + d_ref[:, cols[q]] * dyg[q])
            for q in range(2):
                db_ref[pair[q]] += lax.dot_general(ug[q].astype(BF16), gb16[q], TN, preferred_element_type=F32)
            for q in range(2):
                dc_ref[pair[q]] += lax.dot_general(dyg[q].astype(BF16), xb[q][NSEG:, :].astype(BF16), TN,
                                                   preferred_element_type=F32)
                dd_ref[:, cols[q]] += _rsum(dyg[q] * ug[q])

    rev = pl.BlockSpec((R, D), lambda i: (nb - 1 - i, 0))
    acc3 = (ngb, 128, 2 * SW)
    return pl.pallas_call(
        body, name=name, grid=(nb,),
        in_specs=[rev, rev, rev, pl.BlockSpec((1, crows, 2 * SW), lambda i: (nb - 1 - i, 0, 0)),
                  _whole(tb["lam"].shape), _whole(tb["lamT"].shape), _whole(tb["bbd"].shape),
                  _whole(tb["bbdT"].shape), _whole(tb["cbdT"].shape), _whole(dvec.shape)],
        out_specs=[rev, _whole(acc3), _whole(acc3), _whole((crows, 2 * SW)), _whole((1, D))],
        out_shape=[jax.ShapeDtypeStruct((L, D), F32), jax.ShapeDtypeStruct(acc3, F32), jax.ShapeDtypeStruct(acc3, F32),
                   jax.ShapeDtypeStruct((crows, 2 * SW), F32), jax.ShapeDtypeStruct((1, D), F32)],
        scratch_shapes=[pltpu.VMEM((crows, 2 * SW), F32), pltpu.VMEM((4, R + NSEG, 2 * SW), F32),
                        pltpu.VMEM((4, R, 2 * SW), F32), pltpu.VMEM((R, D), F32), pltpu.VMEM((4, NSEG, 2 * SW), F32)],
        compiler_params=_params("arbitrary"),
    )(up, y, dz, ck, tb["lam"], tb["lamT"], tb["bbd"], tb["bbdT"], tb["cbdT"], dvec)


NN = (((1,), (0,)), ((), ()))
TN = (((0,), (0,)), ((), ()))
NT = (((1,), (1,)), ((), ()))


def _dot3(lhs, rhs, dn, split):
    x = rhs if split == "rhs" else lhs
    hi = x.astype(BF16)
    r1 = x - hi.astype(F32)
    mid = r1.astype(BF16)
    lo = (r1 - mid.astype(F32)).astype(BF16)
    out = None
    for part in (hi, mid, lo):
        ops = (lhs, part) if split == "rhs" else (part, rhs)
        t = lax.dot_general(ops[0], ops[1], dn, preferred_element_type=F32)
        out = t if out is None else out + t
    return out


def _log_sigmoid(x):
    return jnp.minimum(x, 0.0) - jnp.log(1.0 + jnp.exp(-jnp.abs(x)))


def _chunk_tri(rows, upper):
    r = lax.broadcasted_iota(jnp.int32, (rows, rows), 0)
    c = lax.broadcasted_iota(jnp.int32, (rows, rows), 1)
    same = (r // CHUNK) == (c // CHUNK)
    return (same & ((c >= r) if upper else (r >= c))).astype(BF16)


def _gla_block_gates(p_ref, wg_ref, bg_ref, QK, wbuf, gebuf):
    RB = p_ref.shape[0]
    glr = p_ref[:, 6 * QK:6 * QK + 128].astype(BF16)
    gpre = jnp.dot(glr, wg_ref[...], preferred_element_type=F32) + bg_ref[...]
    la = _log_sigmoid(gpre) * (1.0 / GATE_TAU)
    gc = _dot3(_chunk_tri(RB, False), la, NN, "rhs")
    for cc in range(RB // CHUNK):
        rows = slice(cc * CHUNK, (cc + 1) * CHUNK)
        ge = gc[(cc + 1) * CHUNK - 1:(cc + 1) * CHUNK, :]
        gebuf[cc:cc + 1, :] = ge
        wbuf[rows, :] = jnp.exp(ge - gc[rows, :])
    return glr, gpre, la


def _as_column(row, lanes):
    t = jnp.transpose(jnp.broadcast_to(row, (row.shape[1], row.shape[1])))
    return jnp.concatenate([t] * (lanes // row.shape[1]), axis=1)


def _as_row(col):
    return jnp.transpose(jnp.broadcast_to(col, (col.shape[0], col.shape[0])))[0:1, :]


def _gla_fwd(proj, wg2p, bg, gn, *, name):
    L = proj.shape[0]
    QK = wg2p.shape[1]
    DK, DV = QK // HEADS, 2 * QK // HEADS
    nC = L // CHUNK
    NB = min(GLA_NB, nC)
    assert nC % NB == 0
    scale = DK ** -0.5

    def body(p_ref, wg_ref, bg_ref, gn_ref, og_ref, s_ref, sst, wbuf, gebuf):
        @pl.when(pl.program_id(0) == 0)
        def _():
            sst[...] = jnp.zeros_like(sst)

        _gla_block_gates(p_ref, wg_ref, bg_ref, QK, wbuf, gebuf)
        heads = range(HEADS)
        ks = [slice(h * DK, (h + 1) * DK) for h in heads]
        vs = [slice(h * DV, (h + 1) * DV) for h in heads]
        units = [(cc, h) for cc in range(NB) for h in heads]
        rows = [slice(cc * CHUNK, (cc + 1) * CHUNK) for cc in range(NB)]
        kv = {(cc, h): lax.dot_general(
            (p_ref[rows[cc], QK + h * DK:QK + (h + 1) * DK] * wbuf[rows[cc], ks[h]]).astype(BF16),
            p_ref[rows[cc], 2 * QK + h * DV:2 * QK + (h + 1) * DV].astype(BF16), TN, preferred_element_type=F32)
            for cc, h in units}
        S16 = {}
        for cc, h in units:
            S = jnp.exp(_as_column(gebuf[cc:cc + 1, ks[h]], DV)) * sst[ks[h], :] + kv[cc, h]
            sst[ks[h], :] = S
            s_ref[cc, ks[h], :] = S
            S16[cc, h] = S.astype(BF16)
        o = {(cc, h): jnp.dot((p_ref[rows[cc], h * DK:(h + 1) * DK] * scale).astype(BF16), S16[cc, h],
                              preferred_element_type=F32) for cc, h in units}
        for cc, h in units:
            r = p_ref[rows[cc], 4 * QK + h * DV:4 * QK + (h + 1) * DV]
            on = o[cc, h] * lax.rsqrt(jnp.mean(o[cc, h] * o[cc, h], axis=-1, keepdims=True) + EPS)
            og_ref[rows[cc], vs[h]] = (on * gn_ref[:, vs[h]] * (r * _sigmoid(r))).astype(BF16)

    RB = NB * CHUNK
    return pl.pallas_call(
        body, name=name, grid=(nC // NB,),
        in_specs=[pl.BlockSpec((RB, proj.shape[1]), lambda i: (i, 0)), _whole(wg2p.shape), _whole(bg.shape), _whole(gn.shape)],
        out_specs=[pl.BlockSpec((RB, 2 * QK), lambda i: (i, 0)), pl.BlockSpec((NB, QK, DV), lambda i: (i, 0, 0))],
        out_shape=[jax.ShapeDtypeStruct((L, 2 * QK), BF16), jax.ShapeDtypeStruct((nC, QK, DV), F32)],
        scratch_shapes=[pltpu.VMEM((QK, DV), F32), pltpu.VMEM((RB, QK), F32), pltpu.VMEM((8, QK), F32)],
        compiler_params=_params("arbitrary"),
    )(proj, wg2p, bg, gn)


def _gla_bwd(proj, dog, states, wg2p, bg, gn, *, name):
    L, W = proj.shape
    QK = wg2p.shape[1]
    DK, DV = QK // HEADS, 2 * QK // HEADS
    nC = L // CHUNK
    NB = min(GLA_NB, nC)
    nB = nC // NB
    scale = DK ** -0.5

    def body(p_ref, dog_ref, sc_ref, sp_ref, wg_ref, bg_ref, gn_ref, dp_ref, dwg_ref, dbg_ref, dgn_ref,
             gst, wbuf, gebuf, dwwbuf, dgebuf):
        i = pl.program_id(0)

        @pl.when(i == 0)
        def _():
            gst[...] = jnp.zeros_like(gst)
            dwg_ref[...] = jnp.zeros_like(dwg_ref)
            dbg_ref[...] = jnp.zeros_like(dbg_ref)
            dgn_ref[...] = jnp.zeros_like(dgn_ref)

        RB = NB * CHUNK
        glr, gpre, _ = _gla_block_gates(p_ref, wg_ref, bg_ref, QK, wbuf, gebuf)
        heads = range(HEADS)
        ks = [slice(h * DK, (h + 1) * DK) for h in heads]
        vs = [slice(h * DV, (h + 1) * DV) for h in heads]
        units = [(cc, h) for cc in range(NB) for h in heads]
        rws = [slice(cc * CHUNK, (cc + 1) * CHUNK) for cc in range(NB)]
        qs16 = {(cc, h): (p_ref[rws[cc], h * DK:(h + 1) * DK] * scale).astype(BF16) for cc, h in units}
        S16a = {(cc, h): sc_ref[cc, ks[h], :].astype(BF16) for cc, h in units}
        oa = {u: jnp.dot(qs16[u], S16a[u], preferred_element_type=F32) for u in units}
        doa = {}
        for cc, h in units:
            r = p_ref[rws[cc], 4 * QK + h * DV:4 * QK + (h + 1) * DV]
            o = oa[cc, h]
            rs = lax.rsqrt(jnp.mean(o * o, axis=-1, keepdims=True) + EPS)
            on = o * rs
            sr = _sigmoid(r)
            dg = dog_ref[rws[cc], vs[h]]
            gnh = gn_ref[:, vs[h]]
            dp_ref[rws[cc], 4 * QK + h * DV:4 * QK + (h + 1) * DV] = (
                dg * on * gnh * (sr * (1.0 + r * (1.0 - sr)))).astype(BF16)
            dt = dg * (r * sr)
            dgn_ref[:, vs[h]] += _rsum(dt * on)
            don = dt * gnh
            doa[cc, h] = (rs * (don - on * jnp.mean(don * on, axis=-1, keepdims=True))).astype(BF16)
        dqa = {u: lax.dot_general(doa[u], S16a[u], NT, preferred_element_type=F32) for u in units}
        for cc, h in units:
            dp_ref[rws[cc], h * DK:(h + 1) * DK] = (dqa[cc, h] * scale).astype(BF16)
        for cc in range(NB - 1, -1, -1):
            rows = rws[cc]
            do = [doa[cc, h] for h in heads]
            Gc = [gst[ks[h], :] + lax.dot_general(qs16[cc, h], do[h], TN, preferred_element_type=F32) for h in heads]
            G16 = [g.astype(BF16) for g in Gc]
            kd = [p_ref[rows, QK + h * DK:QK + (h + 1) * DK] * wbuf[rows, ks[h]] for h in heads]
            dkd = [lax.dot_general(p_ref[rows, 2 * QK + h * DV:2 * QK + (h + 1) * DV].astype(BF16), G16[h], NT,
                                   preferred_element_type=F32) for h in heads]
            dv = [jnp.dot(kd[h].astype(BF16), G16[h], preferred_element_type=F32) for h in heads]
            for h in heads:
                if cc > 0:
                    Sp = sc_ref[cc - 1, ks[h], :]
                else:
                    Sp = jnp.where(i < nB - 1, sp_ref[0, ks[h], :], 0.0)
                dp_ref[rows, 2 * QK + h * DV:2 * QK + (h + 1) * DV] = dv[h].astype(BF16)
                ge = gebuf[cc:cc + 1, ks[h]]
                gst[ks[h], :] = jnp.exp(_as_column(ge, DV)) * Gc[h]
                ddec = _as_row(jnp.sum(Gc[h] * Sp, axis=1, keepdims=True))
                dp_ref[rows, QK + h * DK:QK + (h + 1) * DK] = (dkd[h] * wbuf[rows, ks[h]]).astype(BF16)
                dww = dkd[h] * kd[h]
                dwwbuf[rows, ks[h]] = dww
                dgebuf[cc:cc + 1, ks[h]] = jnp.exp(ge) * ddec + _rsum(dww)
        rev = _dot3(_chunk_tri(RB, True), dwwbuf[...], NN, "rhs")
        for cc in range(NB):
            rows = slice(cc * CHUNK, (cc + 1) * CHUNK)
            wbuf[rows, :] = dgebuf[cc:cc + 1, :] - rev[rows, :]
        dgpre = wbuf[...] * (1.0 / GATE_TAU) * (1.0 - _sigmoid(gpre))
        d16 = dgpre.astype(BF16)
        dp_ref[:, 6 * QK:6 * QK + 128] = lax.dot_general(d16, wg_ref[...], NT, preferred_element_type=F32).astype(BF16)
        dwg_ref[...] += lax.dot_general(glr, d16, TN, preferred_element_type=F32)
        dbg_ref[...] += _rsum(dgpre)

    RB = NB * CHUNK
    rev_idx = lambda i: (nB - 1 - i, 0)
    return pl.pallas_call(
        body, name=name, grid=(nB,),
        in_specs=[pl.BlockSpec((RB, W), rev_idx), pl.BlockSpec((RB, 2 * QK), rev_idx),
                  pl.BlockSpec((NB, QK, DV), lambda i: (nB - 1 - i, 0, 0)),
                  pl.BlockSpec((1, QK, DV), lambda i: (jnp.maximum(NB * (nB - 1 - i) - 1, 0), 0, 0)),
                  _whole(wg2p.shape), _whole(bg.shape), _whole(gn.shape)],
        out_specs=[pl.BlockSpec((RB, W), rev_idx), _whole((128, QK)), _whole((1, QK)), _whole((1, 2 * QK))],
        out_shape=[jax.ShapeDtypeStruct((L, W), BF16), jax.ShapeDtypeStruct((128, QK), F32),
                   jax.ShapeDtypeStruct((1, QK), F32), jax.ShapeDtypeStruct((1, 2 * QK), F32)],
        scratch_shapes=[pltpu.VMEM((QK, DV), F32), pltpu.VMEM((RB, QK), F32), pltpu.VMEM((8, QK), F32),
                        pltpu.VMEM((RB, QK), F32), pltpu.VMEM((8, QK), F32)],
        compiler_params=_params("arbitrary"),
    )(proj, dog, states, states, wg2p, bg, gn)


def _coords():
    return lax.axis_index("x"), lax.axis_index("y"), lax.axis_index("c")


def _other_chips(x, y):
    return [(1 - x, y, 2 * (1 - x) + y), (x, 1 - y, 2 * x + 1 - y), (1 - x, 1 - y, 2 * (1 - x) + 1 - y)]


def _hbm_call(body, ins, out_shapes, n_sems, *, name, alias=False):
    any_spec = pl.BlockSpec(memory_space=pl.ANY)
    return pl.pallas_call(
        body, name=name, in_specs=[any_spec] * len(ins), out_specs=[any_spec] * len(out_shapes), out_shape=out_shapes,
        scratch_shapes=[pltpu.SemaphoreType.DMA((n,)) for n in n_sems],
        input_output_aliases={k: k for k in range(len(ins))} if alias else {},
    )(*ins)


def _exchange(src, masks, *, name):
    vary = [any(m[k] for m in masks) for k in range(3)]
    nslots = 2 ** sum(vary)
    n = len(masks)

    def slot(coords):
        s = 0
        for k in range(3):
            if vary[k]:
                s = s * 2 + coords[k]
        return s

    def body(src_ref, dst_ref, send_sems, recv_sems, loc_sem):
        me = _coords()
        mine = slot(me)
        loc = pltpu.make_async_copy(src_ref, dst_ref.at[mine], loc_sem.at[0])
        loc.start()
        copies = []
        for k, m in enumerate(masks):
            peer = tuple(1 - me[d] if m[d] else me[d] for d in range(3))
            cp = pltpu.make_async_remote_copy(src_ref=src_ref, dst_ref=dst_ref.at[mine], send_sem=send_sems.at[k],
                                              recv_sem=recv_sems.at[k], device_id=peer, device_id_type=MESH)
            cp.start()
            copies.append(cp)
        for cp in copies:
            cp.wait()
        loc.wait()

    return _hbm_call(body, [src], [jax.ShapeDtypeStruct((nslots,) + tuple(src.shape), src.dtype)], (n, n, 1), name=name)[0]


def _cast_into(t, lead, kind, chip, *, name, tm=256):
    r, cc = t.shape[-2:]
    tm = min(tm, r)
    nblk = r // tm
    if kind == "col":
        shp, o_spec = (r, NCH * cc), pl.BlockSpec((tm, cc), lambda i, s: (i, s[0]))
    elif kind == "row":
        shp, o_spec = (NCH * r, cc), pl.BlockSpec((tm, cc), lambda i, s: (s[0] * nblk + i, 0))
    else:
        shp, o_spec = (NCH, r, cc), pl.BlockSpec((None, tm, cc), lambda i, s: (s[0], i, 0))

    def body(s_ref, t_ref, o_ref):
        o_ref[...] = t_ref[...].astype(o_ref.dtype)

    return pl.pallas_call(
        body, name=name,
        grid_spec=pltpu.PrefetchScalarGridSpec(
            num_scalar_prefetch=1, grid=(nblk,),
            in_specs=[pl.BlockSpec((None, tm, cc), lambda i, s: (lead, i, 0))], out_specs=o_spec),
        out_shape=jax.ShapeDtypeStruct(shp, BF16), compiler_params=_params("parallel"),
    )(chip.reshape(1).astype(jnp.int32), t)


def _gather_weights(arrs, shard_shapes, kinds, *, name):
    n = len(arrs)

    def body(*refs):
        dst = refs[n:2 * n]
        send_sems, recv_sems = refs[2 * n:]
        x, y, c = _coords()
        chip = 2 * x + y
        others = _other_chips(x, y)
        sib = (x, y, 1 - c)

        def window(p, chip_id, cc):
            r, cols = shard_shapes[p]
            h = r // 2
            if kinds[p] == "col":
                return dst[p].at[pl.ds(cc * h, h), pl.ds(pl.multiple_of(chip_id * cols, 128), cols)]
            if kinds[p] == "row":
                return dst[p].at[pl.ds(chip_id * r + cc * h, h), :]
            return dst[p].at[chip_id, pl.ds(cc * h, h), :]

        def copy(p, k, win, to):
            return pltpu.make_async_remote_copy(src_ref=win, dst_ref=win, send_sem=send_sems.at[6 * p + k],
                                                recv_sem=recv_sems.at[6 * p + k], device_id=to, device_id_type=MESH)

        sends = []
        for p in range(n):
            for j, (ox, oy, _) in enumerate(others):
                cp = copy(p, j, window(p, chip, c), (ox, oy, c))
                cp.start()
                sends.append(cp)
        for j, (_, _, oc) in enumerate(others):
            for p in range(n):
                copy(p, j, window(p, oc, c), (x, y, c)).wait_recv()
                fw = copy(p, 3 + j, window(p, oc, c), sib)
                fw.start()
                sends.append(fw)
        for p in range(n):
            for j, (_, _, oc) in enumerate(others):
                copy(p, 3 + j, window(p, oc, 1 - c), sib).wait_recv()
        for cp in sends:
            cp.wait_send()

    outs = [jax.ShapeDtypeStruct(a.shape, a.dtype) for a in arrs]
    return _hbm_call(body, arrs, outs, (6 * n, 6 * n), name=name, alias=True)


HBM_SPEC = pl.BlockSpec(memory_space=pltpu.HBM)
SEM_SPEC = pl.BlockSpec(memory_space=pltpu.SEMAPHORE)
EFFECT = pltpu.SideEffectType.DATAFLOW_SIDE_EFFECTING


def _window(ref, shard_shape, kind, chip_id, cc):
    r, cols = shard_shape
    h = r // 2
    if kind == "col":
        return ref.at[pl.ds(cc * h, h), pl.ds(pl.multiple_of(chip_id * cols, 128), cols)]
    if kind == "row":
        return ref.at[pl.ds(chip_id * r + cc * h, h), :]
    return ref.at[chip_id, pl.ds(cc * h, h), :]


def _split_start(start, arrs, n_sems, *, name):
    n, ns = len(arrs), len(n_sems)

    def body(*refs):
        start(refs[:n], refs[n:n + ns])
        refs[-1][...] = jnp.zeros_like(refs[-1])

    outs = pl.pallas_call(
        body, name=name,
        out_shape=tuple([pltpu.SemaphoreType.DMA((k,)) for k in n_sems] + [pltpu.HBM(a.shape, a.dtype) for a in arrs]
                        + [jax.ShapeDtypeStruct((8, 128), F32)]),
        in_specs=[HBM_SPEC] * n, out_specs=tuple([SEM_SPEC] * ns + [HBM_SPEC] * n + [pl.BlockSpec(memory_space=pltpu.VMEM)]),
        input_output_aliases={k: ns + k for k in range(n)},
        compiler_params=pltpu.CompilerParams(has_side_effects=EFFECT),
    )(*[pltpu.with_memory_space_constraint(a, pltpu.HBM) for a in arrs])
    return list(outs[:ns]), list(outs[ns:ns + n]), outs[-1]


def _split_wait(wait, arrs, sems, after, *, name):
    n, ns = len(arrs), len(sems)

    def body(*refs):
        wait(refs[:n], refs[n:n + ns])

    return pl.pallas_call(
        body, name=name, out_shape=tuple(pltpu.HBM(a.shape, a.dtype) for a in arrs),
        in_specs=[HBM_SPEC] * n + [SEM_SPEC] * ns + [pl.BlockSpec(memory_space=pl.ANY)], out_specs=tuple([HBM_SPEC] * n),
        input_output_aliases={k: k for k in range(n)},
        compiler_params=pltpu.CompilerParams(has_side_effects=EFFECT),
    )(*arrs, *sems, after)


def _gw_copies(refs, send_sems, recv_sems, shard_shapes, kinds, outgoing):
    x, y, c = _coords()
    chip = 2 * x + y
    out = []
    for p in range(len(refs)):
        for j, (ox, oy, oc) in enumerate(_other_chips(x, y)):
            win = _window(refs[p], shard_shapes[p], kinds[p], chip if outgoing else oc, c)
            out.append(pltpu.make_async_remote_copy(
                src_ref=win, dst_ref=win, send_sem=send_sems.at[3 * p + j], recv_sem=recv_sems.at[3 * p + j],
                device_id=(ox, oy, c), device_id_type=MESH))
    return out


def _gw_start(arrs, shard_shapes, kinds, groups, *, name):
    def start(refs, sems):
        for g, idx in enumerate(groups):
            for cp in _gw_copies([refs[p] for p in idx], sems[2 * g], sems[2 * g + 1], [shard_shapes[p] for p in idx],
                                 [kinds[p] for p in idx], True):
                cp.start()

    n_sems = [3 * len(idx) for idx in groups for _ in range(2)]
    sems, thru, token = _split_start(start, arrs, n_sems, name=name)
    return [(sems[2 * g], sems[2 * g + 1]) for g in range(len(groups))], thru, token


def _gw_wait(arrs, shard_shapes, kinds, sem_pair, after, *, name):
    def wait(refs, sems):
        for cp in _gw_copies(refs, sems[0], sems[1], shard_shapes, kinds, True):
            cp.wait_send()
        for cp in _gw_copies(refs, sems[0], sems[1], shard_shapes, kinds, False):
            cp.wait_recv()

    return _split_wait(wait, arrs, list(sem_pair), after, name=name)


def _gw_forward_copies(refs, shard_shapes, kinds, send_sems, recv_sems, incoming):
    x, y, c = _coords()
    out = []
    for p in range(len(refs)):
        for j, (_, _, oc) in enumerate(_other_chips(x, y)):
            win = _window(refs[p], shard_shapes[p], kinds[p], oc, 1 - c if incoming else c)
            out.append(pltpu.make_async_remote_copy(
                src_ref=win, dst_ref=win, send_sem=send_sems.at[3 * p + j], recv_sem=recv_sems.at[3 * p + j],
                device_id=(x, y, 1 - c), device_id_type=MESH))
    return out


def _gw_forward(arrs, shard_shapes, kinds, *, name):
    n = len(arrs)

    def body(*refs):
        dst, (send_sems, recv_sems) = refs[n:2 * n], refs[2 * n:]
        sends = _gw_forward_copies(dst, shard_shapes, kinds, send_sems, recv_sems, False)
        for cp in sends:
            cp.start()
        for cp in _gw_forward_copies(dst, shard_shapes, kinds, send_sems, recv_sems, True):
            cp.wait_recv()
        for cp in sends:
            cp.wait_send()

    outs = [jax.ShapeDtypeStruct(a.shape, a.dtype) for a in arrs]
    return _hbm_call(body, arrs, outs, (3 * n, 3 * n), name=name, alias=True)


def _gw_forward_start(arrs, shard_shapes, kinds, *, name):
    def start(refs, sems):
        for cp in _gw_forward_copies(refs, shard_shapes, kinds, sems[0], sems[1], False):
            cp.start()

    sems, thru, token = _split_start(start, arrs, [3 * len(arrs)] * 2, name=name)
    return (sems[0], sems[1]), thru, token


def _gw_forward_wait(pair, arrs, shard_shapes, kinds, after, *, name):
    def wait(refs, sems):
        for cp in _gw_forward_copies(refs, shard_shapes, kinds, sems[0], sems[1], False):
            cp.wait_send()
        for cp in _gw_forward_copies(refs, shard_shapes, kinds, sems[0], sems[1], True):
            cp.wait_recv()

    return _split_wait(wait, arrs, list(pair), after, name=name)


def _rs_chips_copies(parts, lands, send_sems, recv_sems):
    x, y, c = _coords()
    chip = 2 * x + y
    out = []
    for p in range(len(parts)):
        for j, (ox, oy, oc) in enumerate(_other_chips(x, y)):
            out.append(pltpu.make_async_remote_copy(
                src_ref=parts[p].at[oc], dst_ref=lands[p].at[chip], send_sem=send_sems.at[3 * p + j],
                recv_sem=recv_sems.at[3 * p + j], device_id=(ox, oy, c), device_id_type=MESH))
    return out


def _rs_chips_start(parts, *, name):
    n = len(parts)

    def start(refs, sems):
        for cp in _rs_chips_copies(refs[:n], refs[n:], sems[0], sems[1]):
            cp.start()

    lands = [lax.empty(t.shape, t.dtype) for t in parts]
    sems, thru, token = _split_start(start, list(parts) + lands, [3 * n, 3 * n], name=name)
    return (sems[0], sems[1]), thru[:n], thru[n:], token


def _rs_chips_wait(groups, after, *, name):
    sizes = [len(g[1]) for g in groups]
    arrs = [a for g in groups for a in list(g[1]) + list(g[2])]
    sems = [s for g in groups for s in g[0]]

    def wait(refs, sem_refs):
        o = 0
        for k, n in enumerate(sizes):
            for cp in _rs_chips_copies(refs[o:o + n], refs[o + n:o + 2 * n], sem_refs[2 * k], sem_refs[2 * k + 1]):
                cp.wait()
            o += 2 * n

    outs = _split_wait(wait, arrs, sems, after, name=name)
    res, o = [], 0
    for n in sizes:
        res.append((list(outs[o:o + n]), list(outs[o + n:o + 2 * n])))
        o += 2 * n
    return res


def _rs_cores_copies(grads, lands, send_sems, recv_sems):
    x, y, c = _coords()
    out, o = [], 0
    for p in range(len(grads)):
        nsh, h = lands[p].shape[0], lands[p].shape[1]
        for j in range(nsh):
            out.append(pltpu.make_async_remote_copy(
                src_ref=grads[p].at[j, pl.ds((1 - c) * h, h), :], dst_ref=lands[p].at[j],
                send_sem=send_sems.at[o + j], recv_sem=recv_sems.at[o + j], device_id=(x, y, 1 - c), device_id_type=MESH))
        o += nsh
    return out


def _rs_cores_start(grads, *, name):
    n = len(grads)
    tot = sum(g.shape[0] for g in grads)

    def start(refs, sems):
        for cp in _rs_cores_copies(refs[:n], refs[n:], sems[0], sems[1]):
            cp.start()

    lands = [lax.empty((g.shape[0], g.shape[1] // 2, g.shape[2]), g.dtype) for g in grads]
    sems, thru, token = _split_start(start, list(grads) + lands, [tot, tot], name=name)
    return (sems[0], sems[1]), thru[:n], thru[n:], token


def _rs_cores_wait(pair, grads, lands, after, *, name):
    n = len(grads)

    def wait(refs, sems):
        for cp in _rs_cores_copies(refs[:n], refs[n:], sems[0], sems[1]):
            cp.wait()

    outs = _split_wait(wait, list(grads) + list(lands), list(pair), after, name=name)
    return list(outs[:n]), list(outs[n:])


def _rs_cores(grads, *, name):
    n = len(grads)
    outs = [jax.ShapeDtypeStruct((g.shape[0], g.shape[1] // 2, g.shape[2]), g.dtype) for g in grads]

    def body(*refs):
        src, dst = refs[:n], refs[n:2 * n]
        send_sems, recv_sems = refs[2 * n:]
        x, y, c = _coords()
        copies = []
        for p in range(n):
            nsh, r, _ = grads[p].shape
            h = r // 2
            for j in range(nsh):
                cp = pltpu.make_async_remote_copy(
                    src_ref=src[p].at[j, pl.ds((1 - c) * h, h), :], dst_ref=dst[p].at[j],
                    send_sem=send_sems.at[nsh * p + j], recv_sem=recv_sems.at[nsh * p + j],
                    device_id=(x, y, 1 - c), device_id_type=MESH)
                cp.start()
                copies.append(cp)
        for cp in copies:
            cp.wait()

    tot = sum(g.shape[0] for g in grads)
    return _hbm_call(body, grads, outs, (tot, tot), name=name)


def _sum_own_half(full, recv, ci, out_dtype, *, name):
    nsh, h, cols = recv.shape
    tm = h if nsh * h * cols * 4 <= (2 << 20) else _tile_rows(h, 256)
    nblk = h // tm

    def body(c_ref, f_ref, r_ref, o_ref):
        o_ref[...] = (f_ref[...] + r_ref[...]).astype(o_ref.dtype)

    return pl.pallas_call(
        body, name=name,
        grid_spec=pltpu.PrefetchScalarGridSpec(
            num_scalar_prefetch=1, grid=(nsh, nblk),
            in_specs=[pl.BlockSpec((1, tm, cols), lambda j, i, c_ref: (j, c_ref[0] * nblk + i, 0)),
                      pl.BlockSpec((1, tm, cols), lambda j, i, c_ref: (j, i, 0))],
            out_specs=pl.BlockSpec((1, tm, cols), lambda j, i, c_ref: (j, i, 0))),
        out_shape=jax.ShapeDtypeStruct((nsh, h, cols), out_dtype), compiler_params=_params("parallel", "parallel"),
    )(ci.reshape(1).astype(jnp.int32), full, recv)


def _sum_chips(recv, own, chip, ci, *, name, nlead=1, lead=0, prev=None, spread=False):
    nsh, h, cols = recv.shape
    tm = h if nsh * h * cols * 4 <= (2 << 20) else _tile_rows(h, 256)
    nblk = h // tm
    rows_out = 2 * h * (nsh if spread else 1)

    def body(s_ref, r_ref, o_ref, *rest):
        out_ref = rest[-1]
        t = None
        for s in range(nsh):
            v = jnp.where(s_ref[0] == s, o_ref[s], r_ref[s]).astype(F32)
            t = v if t is None else t + v
        out_ref[...] = t

    def out_idx(i, s):
        return (lead, (s[0] * 2 * nblk if spread else 0) + s[1] * nblk + i, 0)

    blk = pl.BlockSpec((nsh, tm, cols), lambda i, s: (0, i, 0))
    ins = [recv, own] + ([prev] if prev is not None else [])
    return pl.pallas_call(
        body, name=name,
        grid_spec=pltpu.PrefetchScalarGridSpec(
            num_scalar_prefetch=1, grid=(nblk,),
            in_specs=[blk, blk] + ([pl.BlockSpec(memory_space=pl.ANY)] if prev is not None else []),
            out_specs=pl.BlockSpec((None, tm, cols), out_idx)),
        out_shape=jax.ShapeDtypeStruct((nlead, rows_out, cols), F32),
        input_output_aliases={3: 0} if prev is not None else {},
        compiler_params=_params("arbitrary"),
    )(jnp.stack([chip, ci]).astype(jnp.int32), *ins)


RS_GATHER_CHUNKS = 4


def _rs_gather_copies(refs, nleads, halves, spread, send_sems, recv_sems, incoming):
    x, y, c = _coords()
    chip = 2 * x + y
    out, sem = [], 0
    for p in range(len(refs)):
        h = halves[p]
        q = h // RS_GATHER_CHUNKS
        base = (chip * 2 * h if spread[p] else 0) + (1 - c if incoming else c) * h
        for l in range(nleads[p]):
            for k in range(RS_GATHER_CHUNKS):
                win = refs[p].at[l, pl.ds(base + k * q, q), :]
                out.append(pltpu.make_async_remote_copy(src_ref=win, dst_ref=win, send_sem=send_sems.at[sem],
                                                        recv_sem=recv_sems.at[sem], device_id=(x, y, 1 - c),
                                                        device_id_type=MESH))
                sem += 1
    return out


def _rs_gather_start(arrs, halves, spread, *, name):
    nleads = [a.shape[0] for a in arrs]
    tot = sum(nleads) * RS_GATHER_CHUNKS

    def start(refs, sems):
        for cp in _rs_gather_copies(refs, nleads, halves, spread, sems[0], sems[1], False):
            cp.start()

    sems, thru, token = _split_start(start, arrs, [tot, tot], name=name)
    return (sems[0], sems[1]), thru, token


def _rs_gather_wait(pair, arrs, halves, spread, after, *, name):
    nleads = [a.shape[0] for a in arrs]

    def wait(refs, sems):
        for cp in _rs_gather_copies(refs, nleads, halves, spread, sems[0], sems[1], False):
            cp.wait_send()
        for cp in _rs_gather_copies(refs, nleads, halves, spread, sems[0], sems[1], True):
            cp.wait_recv()

    return _split_wait(wait, arrs, list(pair), after, name=name)


def _adamw(w, g, m, v, *, name):
    nl, R, C = w.shape
    tm = _tile_rows(R, 256)

    blk = pl.BlockSpec((None, tm, C), lambda l, i: (l, i, 0))
    return pl.pallas_call(
        _adamw_body_copy(), name=name, grid=(nl, R // tm), in_specs=[blk] * 4, out_specs=[blk] * 3,
        out_shape=[jax.ShapeDtypeStruct((nl, R, C), F32)] * 3, compiler_params=_params("parallel", "parallel"),
    )(w, g, m, v)


def _adamw_body(w_ref, g_ref, m_ref, v_ref, d_ref, nm_ref, nv_ref):
    gg = g_ref[...]
    nm = B1 * m_ref[...] + (1.0 - B1) * gg
    nv = B2 * v_ref[...] + (1.0 - B2) * (gg * gg)
    m_hat = nm / (1.0 - B1 ** ASTEP)
    v_hat = nv / (1.0 - B2 ** ASTEP)
    d_ref[...] = -LR * (m_hat / (jnp.sqrt(v_hat) + AEPS) + WD * w_ref[...])
    nm_ref[...] = nm
    nv_ref[...] = nv


def _adamw_whole(w, g, m, v, *, name):
    return pl.pallas_call(_adamw_body_copy(), name=name, out_shape=[jax.ShapeDtypeStruct(w.shape, F32)] * 3,
                          compiler_params=_params())(w, g, m, v)


def _adamw_body_copy():
    def body(*refs):
        _adamw_body(*refs)
    return body


def _mod_cols(c_all, w_ada, b_cols, *, name):
    nl, D, cols = w_ada.shape
    B = c_all.shape[0]

    def body(c_ref, w_ref, b_ref, o_ref):
        cc = c_ref[...]
        cs = (cc * _sigmoid(cc)).astype(BF16)
        o_ref[0] = jnp.dot(cs, w_ref[0].astype(BF16), preferred_element_type=F32) + b_ref[0]

    return pl.pallas_call(
        body, name=name, grid=(nl,),
        in_specs=[_whole(c_all.shape), pl.BlockSpec((1, D, cols), lambda i: (i, 0, 0)), pl.BlockSpec((1, 1, cols), lambda i: (i, 0, 0))],
        out_specs=pl.BlockSpec((1, B, cols), lambda i: (i, 0, 0)),
        out_shape=jax.ShapeDtypeStruct((nl, B, cols), F32), compiler_params=_params("arbitrary"),
    )(c_all, w_ada, b_cols)


def _ada_grad(c_all, dmod_cols, *, name):
    nl, B, cols = dmod_cols.shape
    D = c_all.shape[1]

    def body(c_ref, d_ref, o_ref):
        cc = c_ref[...]
        cs = (cc * _sigmoid(cc)).astype(BF16)
        o_ref[0] = lax.dot_general(cs, d_ref[0].astype(BF16), TN, preferred_element_type=F32)

    return pl.pallas_call(
        body, name=name, grid=(nl,),
        in_specs=[_whole(c_all.shape), pl.BlockSpec((1, B, cols), lambda i: (i, 0, 0))],
        out_specs=pl.BlockSpec((1, D, cols), lambda i: (i, 0, 0)),
        out_shape=jax.ShapeDtypeStruct((nl, D, cols), F32), compiler_params=_params("arbitrary"),
    )(c_all, dmod_cols)


def _s5_disc(a_re, a_im, log_dt, b_re, b_im):
    dt = jnp.exp(log_dt)[:, None]
    mag = jnp.exp(a_re * dt)
    ph = a_im * dt
    lb_re = mag * jnp.cos(ph)
    lb_im = mag * jnp.sin(ph)
    den = a_re * a_re + a_im * a_im
    nr = lb_re - 1.0
    ni = lb_im
    f_re = (nr * a_re + ni * a_im) / den
    f_im = (ni * a_re - nr * a_im) / den
    bb_re = f_re[..., None] * b_re - f_im[..., None] * b_im
    bb_im = f_re[..., None] * b_im + f_im[..., None] * b_re
    return lb_re, lb_im, bb_re, bb_im


def _to_segments(t):
    L, D = t.shape
    R = min(S5_R, L)
    return t.reshape(L // R, NSEG, R // NSEG, D).transpose(0, 2, 1, 3).reshape(L, D)


def _from_segments(t):
    L, D = t.shape
    R = min(S5_R, L)
    return t.reshape(L // R, R // NSEG, NSEG, D).transpose(0, 2, 1, 3).reshape(L, D)


def _w_in_layout(QK, D, ncols):
    segs = [(0, 4 * QK, 0), (4 * QK + GATE_RANK, 4 * QK + GATE_RANK + D, 4 * QK), (4 * QK, 4 * QK + GATE_RANK, 4 * QK + D)]
    fwd = []
    for lo, hi, _ in segs:
        col = lo
        while col < hi:
            j = col // ncols
            end = min(hi, (j + 1) * ncols)
            fwd.append((j, col - j * ncols, end - j * ncols))
            col = end
    bwd = []
    for j in range(NCH):
        ranges, col = [], j * ncols
        while col < (j + 1) * ncols:
            lo, hi, rlo = next(sg for sg in segs if sg[0] <= col < sg[1])
            end = min((j + 1) * ncols, hi)
            ranges.append((rlo + col - lo, rlo + end - lo))
            col = end
        bwd.append(ranges)
    return fwd, bwd


def _mlp_fwd(h2, w1, w2, tag):
    a = _matmul(h2, w1, name=f"ff1_{tag}", tn=2048, out_dtypes=(BF16,), epi=lambda acc: (jnp.maximum(acc, 0.0),))
    f = _matmul(a, w2, name=f"ff2_{tag}", a_fn=jnp.square)
    return a, f


def _mlp_bwd(df, h2, a, w1, w2, tag):
    da = _matmul(df, w2, tb=True, name=f"ff2_dx_{tag}", tn=2048, out_dtypes=(BF16,), epi_ins=(a,),
                 epi=lambda acc, at: (acc * (2.0 * at.astype(F32)),))
    dw2 = _matmul(a, df, ta=True, name=f"ff2_dw_{tag}", a_fn=jnp.square)
    dh2 = _matmul(da, w1, tb=True, name=f"ff1_dx_{tag}")
    dw1 = _matmul(h2, da, ta=True, name=f"ff1_dw_{tag}", col_shards=NCH)
    return dh2, dw1, dw2


def kernel(x, c, w_ada, b_ada, norm_mix, norm_mlp, s5_a_re, s5_a_im, s5_log_dt, s5_b_re, s5_b_im, s5_c_re, s5_c_im, s5_d, s5_w_glu, gla_w_in, gla_w_gate2, gla_b_gate, gla_g_norm, gla_w_out, w_ff1, w_ff2, norm_final, loss_target, m_w_ada, m_b_ada, m_norm_mix, m_norm_mlp, m_s5_a_re, m_s5_a_im, m_s5_log_dt, m_s5_b_re, m_s5_b_im, m_s5_c_re, m_s5_c_im, m_s5_d, m_s5_w_glu, m_gla_w_in, m_gla_w_gate2, m_gla_b_gate, m_gla_g_norm, m_gla_w_out, m_w_ff1, m_w_ff2, m_norm_final, v_w_ada, v_b_ada, v_norm_mix, v_norm_mlp, v_s5_a_re, v_s5_a_im, v_s5_log_dt, v_s5_b_re, v_s5_b_im, v_s5_c_re, v_s5_c_im, v_s5_d, v_s5_w_glu, v_gla_w_in, v_gla_w_gate2, v_gla_b_gate, v_gla_g_norm, v_gla_w_out, v_w_ff1, v_w_ff2, v_norm_final):
    args = dict(locals())
    L, D = x.shape[1], x.shape[2]
    QK = D // 2
    xi, yi, ci = _coords()
    chip = 2 * xi + yi
    dev = 2 * chip + ci

    cat = jnp.concatenate([gla_w_gate2[0].reshape(1, -1), gla_b_gate, gla_g_norm], axis=1)
    first = _exchange(jnp.concatenate([c.reshape(8, D // 8), jnp.tile(cat, (8, 1))], axis=1), MASK_ALL, name="gather_c")
    c_all = first[:, :, :D // 8].reshape(8, D)
    cat_all = first[0::2, 0, D // 8:]
    acols = w_ada.shape[2]
    b_cols = lax.dynamic_slice_in_dim(b_ada, chip * acols, acols, axis=1)[:, None, :]
    mod_cols = _mod_cols(c_all, w_ada, b_cols, name="ada_mod")
    mod_all = _exchange(mod_cols.reshape(16, acols), MASK_CHIPS, name="gather_mod")
    mod_all = mod_all.reshape(NCH, 2, 8, acols).transpose(1, 2, 0, 3).reshape(2, 8, NCH * acols)
    mod = lax.dynamic_index_in_dim(mod_all, dev, axis=1, keepdims=False).reshape(2, 6, 1, D)

    big = [("s5_w_glu", s5_w_glu, 0, "col"), ("gla_w_in", gla_w_in, 0, "slot"), ("gla_w_out", gla_w_out, 0, "row"),
           ("w_ff1_0", w_ff1, 0, "col"), ("w_ff1_1", w_ff1, 1, "col"), ("w_ff2_0", w_ff2, 0, "row"), ("w_ff2_1", w_ff2, 1, "row")]
    own16 = [_cast_into(t, lead, kind, chip, name=f"cast_{nm}") for nm, t, lead, kind in big]
    wshapes, wkinds = [b[1].shape[-2:] for b in big], [b[3] for b in big]
    wgroups = [[0], [3, 5], [1, 2, 4, 6]]
    wsems, wthru, wtoken = _gw_start(own16, wshapes, wkinds, wgroups, name="gather_w_start")
    W = {}

    def weights_landed(g, after):
        idx = wgroups[g]
        shp, knd = [wshapes[p] for p in idx], [wkinds[p] for p in idx]
        return _gw_wait([wthru[p] for p in idx], shp, knd, wsems[g], after, name=f"gather_w_wait{g}"), shp, knd

    def weights_ready(g, arrs):
        for p, w in zip(wgroups[g], arrs):
            W[big[p][0]] = w

    def forward_start(g, after):
        got, shp, knd = weights_landed(g, after)
        pair, thru, token = _gw_forward_start(got, shp, knd, name=f"gather_w_cores_start{g}")
        return (pair, thru, shp, knd), token

    def forward_finish(g, state, after):
        pair, thru, shp, knd = state
        weights_ready(g, _gw_forward_wait(pair, thru, shp, knd, after, name=f"gather_w_cores_wait{g}"))

    qk4 = QK // NCH
    wg2 = cat_all[:, :GATE_RANK * qk4].reshape(NCH, GATE_RANK, qk4).transpose(1, 0, 2).reshape(GATE_RANK, QK)
    bg = cat_all[:, GATE_RANK * qk4:(GATE_RANK + 1) * qk4].reshape(1, QK)
    gn = cat_all[:, (GATE_RANK + 1) * qk4:].reshape(1, D)
    wg2p = jnp.concatenate([wg2, jnp.zeros((128 - GATE_RANK, QK), F32)], axis=0).astype(BF16)

    lb_re, lb_im, bb_re, bb_im = _s5_disc(s5_a_re[0], s5_a_im[0], s5_log_dt[0], s5_b_re[0], s5_b_im[0])
    tb = _s5_tables(lb_re, lb_im, bb_re, bb_im, s5_c_re[0], s5_c_im[0], min(S5_R, L) // NSEG)
    s5_dv = s5_d + wtoken[0, 0]

    def vec(t):
        return t.reshape(1, -1)

    m0, m1 = mod[0], mod[1]
    xp = _to_segments(x[0])
    (u0,) = _rows(lambda t, g, sc, sh: (_norm_mod(t, g, sc, sh),), [xp], [vec(norm_mix[0]), m0[1], m0[0]],
                  [(D, F32)], [], name="pre_mix0")
    y0, z0, ck0 = _s5_fwd(u0, tb, s5_dv, name="s5_fwd")
    got, shp, knd = weights_landed(0, z0)
    weights_ready(0, _gw_forward(got, shp, knd, name="gather_w_cores0"))
    vg0 = _matmul(z0, W["s5_w_glu"], name="glu", tn=2048)
    fwd1, ftok1 = forward_start(1, vg0)

    def res_glu_pre(xt, vgt, gt, g, sc, sh):
        xn = xt + gt * (vgt[:, :D] * _sigmoid(vgt[:, D:]))
        return xn, _norm_mod(xn, g, sc, sh)

    x2_0, h2_0 = _rows(res_glu_pre, [xp, vg0], [m0[2] + ftok1[0, 0], vec(norm_mlp[0]), m0[4], m0[3]],
                       [(D, F32), (D, BF16)], [], name="res_mix0")
    forward_finish(1, fwd1, h2_0)
    a_0, f0 = _mlp_fwd(h2_0, W["w_ff1_0"], W["w_ff2_0"], "0")
    fwd2, ftok2 = forward_start(2, f0)

    def res_pre(xt, bt, gt, g, sc, sh):
        xn = xt + gt * bt
        return xn, _norm_mod(xn, g, sc, sh)

    x3p, h1p = _rows(res_pre, [x2_0, f0], [m0[5] + ftok2[0, 0], vec(norm_mix[1]), m1[1], m1[0]],
                     [(D, F32), (D, BF16)], [], name="res_mlp0")
    x3, h1 = _from_segments(x3p), _from_segments(h1p)
    forward_finish(2, fwd2, h1)
    in_fwd, in_bwd = _w_in_layout(QK, D, gla_w_in.shape[2])
    w_in_r = jnp.concatenate([W["gla_w_in"][j, :, lo:hi] for j, lo, hi in in_fwd]
                             + [jnp.zeros((D, 128 - GATE_RANK), BF16)], axis=1)
    proj = _matmul(h1, w_in_r, name="gla_in", tm=2048, tn=640)
    og, states = _gla_fwd(proj, wg2p, bg, gn, name="gla_fwd")
    ymix = _matmul(og, W["gla_w_out"], name="gla_out")
    x2_1, h2_1 = _rows(res_pre, [x3, ymix], [m1[2], vec(norm_mlp[1]), m1[4], m1[3]], [(D, F32), (D, BF16)], [],
                       name="res_mix1")
    a_1, f1 = _mlp_fwd(h2_1, W["w_ff1_1"], W["w_ff2_1"], "1")

    def final(xt, ft, tgt, gt, g):
        xn = xt + gt * ft
        rs = lax.rsqrt(jnp.mean(xn * xn, axis=-1, keepdims=True) + EPS)
        xh = xn * rs
        e = xh * g - tgt
        dout = e * (1.0 / D)
        dxh = dout * g
        dx = rs * (dxh - xh * jnp.mean(dxh * xh, axis=-1, keepdims=True))
        lsum = 0.5 * jnp.sum(jnp.sum(e * e, axis=-1, keepdims=True), axis=0, keepdims=True) * (1.0 / D)
        return dx, dx * gt, jnp.broadcast_to(lsum, (1, 128)), _rsum(dout * xh), _rsum(dx * ft)

    dx, df1, loss_part, d_norm_final, dgt2_1 = _rows(
        final, [x2_1, f1, loss_target[0]], [m1[5], vec(norm_final)], [(D, F32), (D, BF16)],
        [(1, 128), (1, D), (1, D)], name="loss_head")

    def gate_bwd(dxt, bt, gt):
        return dxt * gt, _rsum(dxt * bt)

    def norm_bwd(xt, dht, drt, g, sc):
        dxn, dsh, dsc, dg = _norm_mod_bwd(xt, dht, g, sc)
        return drt + dxn, dsh, dsc, dg

    def norm_gate_bwd(xt, dht, drt, bt, g, sc, gt):
        dxn, dsh, dsc, dg = _norm_mod_bwd(xt, dht, g, sc)
        dxt = drt + dxn
        return dxt, dxt * gt, dsh, dsc, dg, _rsum(dxt * bt)

    vD = [(1, D)]
    dh2_1, dw_ff1_1, dw_ff2_1 = _mlp_bwd(df1, h2_1, a_1, W["w_ff1_1"], W["w_ff2_1"], "1")
    dx, dmix1, dsh2_1, dsc2_1, dg_mlp1, dgt1_1 = _rows(
        norm_gate_bwd, [x2_1, dh2_1, dx, ymix], [vec(norm_mlp[1]), m1[4], m1[2]], [(D, F32), (D, BF16)], vD * 4,
        name="norm_mlp1_bwd")
    dog = _matmul(dmix1, W["gla_w_out"], tb=True, name="gla_out_dx")
    dw_out = _matmul(og, dmix1, ta=True, name="gla_out_dw")
    dproj, dwg2p, dbg, dgn = _gla_bwd(proj, dog, states, wg2p, bg, gn, name="gla_bwd")
    dh1 = _matmul(dproj, w_in_r, tb=True, name="gla_in_dx", tk=3200)
    dw_in_r = _matmul(h1, dproj, ta=True, name="gla_in_dw", tn=640)
    dx, dsh1_1, dsc1_1, dg_mix1 = _rows(norm_bwd, [x3, dh1, dx], [vec(norm_mix[1]), m1[1]], [(D, F32)], vD * 3,
                                        name="norm_mix1_bwd")
    dxp = _to_segments(dx)
    tags = [b[0] for b in big] + ["small"]
    rs_groups = []

    def rs_chips_begin(idx, srcs, r1, gname):
        s1 = [_sum_own_half(g, r, ci, F32 if tags[k] == "small" else BF16, name=f"rs_sum_cores_{tags[k]}")
              for g, r, k in zip(srcs, r1, idx)]
        pair, parts, lands, token = _rs_chips_start(s1, name=f"rs_chips_start_{gname}")
        rs_groups.append((idx, pair, parts, lands))
        return token

    def rs_begin(idx, srcs, gname):
        return rs_chips_begin(idx, srcs, _rs_cores(srcs, name=f"rs_cores_{gname}"), gname)

    dw_in = jnp.stack([jnp.concatenate([dw_in_r[:, lo:hi] for lo, hi in in_bwd[j]], axis=1) for j in range(NCH)])
    idx1 = [1, 2, 4, 6]
    pair1, src1, land1, tok1 = _rs_cores_start(
        [dw_in, dw_out.reshape(NCH, -1, D), dw_ff1_1, dw_ff2_1.reshape(NCH, -1, D)], name="rs_cores_start_l1")

    df0, dgt2_0 = _rows(gate_bwd, [dxp, f0], [m0[5] + tok1[0, 0]], [(D, BF16)], vD, name="gate_mlp0")
    dh2_0, dw_ff1_0, dw_ff2_0 = _mlp_bwd(df0, h2_0, a_0, W["w_ff1_0"], W["w_ff2_0"], "0")
    src1, land1 = _rs_cores_wait(pair1, src1, land1, dh2_0, name="rs_cores_wait_l1")
    tok1b = rs_chips_begin(idx1, src1, land1, "l1")
    idx0 = [3, 5]
    pair0, src0, land0, tok0 = _rs_cores_start([dw_ff1_0, dw_ff2_0.reshape(NCH, -1, D)], name="rs_cores_start_l0")
    tok2 = tok1b + tok0

    def norm_glu_bwd(xt, dht, drt, vgt, g, sc, gt):
        dxn, dsh, dsc, dg = _norm_mod_bwd(xt, dht, g, sc)
        dxt = drt + dxn
        val, sg = vgt[:, :D], _sigmoid(vgt[:, D:])
        dbr = dxt * gt
        dvg = jnp.concatenate([dbr * sg, dbr * val * sg * (1.0 - sg)], axis=1)
        return dxt, dvg, dsh, dsc, dg, _rsum(dxt * val * sg)

    dxp, dvg0, dsh2_0, dsc2_0, dg_mlp0, dgt1_0 = _rows(
        norm_glu_bwd, [x2_0, dh2_0, dxp, vg0], [vec(norm_mlp[0]), m0[4] + tok2[0, 0], m0[2]], [(D, F32), (2 * D, BF16)],
        vD * 4, name="norm_mlp0_bwd")
    dz0 = _matmul(dvg0, W["s5_w_glu"], tb=True, name="glu_dx")
    dw_glu = _matmul(z0, dvg0, ta=True, name="glu_dw", tn=512, col_shards=NCH)
    src0, land0 = _rs_cores_wait(pair0, src0, land0, dw_glu, name="rs_cores_wait_l0")
    tok0b = rs_chips_begin(idx0 + [0], src0 + [dw_glu], land0 + list(_rs_cores([dw_glu], name="rs_cores_glu")), "l0")
    du0, db_acc, dc_acc, dl_acc, dd_s5 = _s5_bwd(u0, y0, dz0, ck0, tb, s5_dv + tok0b[0, 0], name="s5_bwd")
    dxp, dsh1_0, dsc1_0, dg_mix0 = _rows(norm_bwd, [xp, du0, dxp], [vec(norm_mix[0]), m0[1]], [(D, F32)], vD * 3,
                                         name="norm_mix0_bwd")
    grad_x = _from_segments(dxp)[None]

    dmod = jnp.concatenate([dsh1_0, dsc1_0, dgt1_0, dsh2_0, dsc2_0, dgt2_0,
                            dsh1_1, dsc1_1, dgt1_1, dsh2_1, dsc2_1, dgt2_1], axis=1)
    dbb_re, dbb_im = _s5_untable(db_acc)
    dc_re, dc_im_neg = _s5_untable(dc_acc)
    nbk = D // 128
    dl = dl_acc.reshape(nbk, NSEG, 2, GPB * S5_P).sum(axis=1)
    smalls = [dmod, dg_mix0, dg_mix1, dg_mlp0, dg_mlp1, d_norm_final, dd_s5, dbg, dgn,
              dwg2p[:GATE_RANK].reshape(1, -1), dbb_re.reshape(1, -1), dbb_im.reshape(1, -1),
              dc_re.reshape(1, -1), dc_im_neg.reshape(1, -1), dl.reshape(1, -1), loss_part]
    ssz = [t.shape[1] for t in smalls]
    stot = sum(ssz)
    spad = -(-stot // 8192) * 8192
    svec = jnp.concatenate(smalls + [jnp.zeros((1, spad - stot), F32)], axis=1).reshape(NCH, spad // (128 * NCH), 128)


    rs_begin([7], [svec], "last")
    landed = _rs_chips_wait([(g[1], g[2], g[3]) for g in rs_groups], grad_x, name="rs_chips_wait")
    s1, r2 = {}, {}
    for (idx, _, _, _), (parts, lands) in zip(rs_groups, landed):
        for k, part, land in zip(idx, parts, lands):
            s1[k], r2[k] = part, land

    def fin(k, **kw):
        return _sum_chips(r2[k], s1[k], chip, ci, name=f"rs_sum_chips_{tags[k]}", **kw)

    f_ff1 = fin(4, nlead=2, lead=1, prev=fin(3, nlead=2, lead=0))
    f_ff2 = fin(6, nlead=2, lead=1, prev=fin(5, nlead=2, lead=0))
    finals = [fin(0), fin(1), fin(2), f_ff1, f_ff2, fin(7, spread=True)]
    halves = [t.shape[1] for t in (s1[0], s1[1], s1[2], s1[3], s1[5], s1[7])]
    gpair, gthru, gtok = _rs_gather_start(finals, halves, [False] * 5 + [True], name="rs_gather_cores_start")

    dmod_all = _exchange((dmod + gtok[0, 0]).reshape(12 * D // 128, 128), MASK_ALL, name="gather_dmod").reshape(8, 2, 6 * D)
    dmod_cols = lax.dynamic_slice_in_dim(dmod_all, chip * acols, acols, axis=2).transpose(1, 0, 2)
    g_w_ada = _ada_grad(c_all, dmod_cols, name="ada_grad")
    upd_w_ada = _adamw(w_ada, g_w_ada, m_w_ada, v_w_ada, name="adamw_w_ada")

    g_glu, g_in, g_out, g_w_ff1, g_w_ff2, s_own = _rs_gather_wait(
        gpair, gthru, halves, [False] * 5 + [True], upd_w_ada[0], name="rs_gather_cores_wait")
    srows = spad // (128 * NCH)
    (s_sum,) = _gather_weights([s_own.reshape(NCH * srows, 128)], [(srows, 128)], ["row"], name="gather_small_grads")
    s_sum = s_sum.reshape(-1)
    so = [sum(ssz[:k]) for k in range(len(ssz))]
    sm = [s_sum[o:o + n] for o, n in zip(so, ssz)]
    (dmod_s, g_mix0, g_mix1, g_mlp0, g_mlp1, g_nf, g_d, g_bg, g_gn, g_wg2, g_bbre, g_bbim, g_cre, g_cimn, g_dl, loss_s) = sm
    loss = loss_s[0]
    g_b_ada = dmod_s.reshape(2, 6 * D)

    G = D // S5_H
    _, disc_vjp = jax.vjp(_s5_disc, s5_a_re[0], s5_a_im[0], s5_log_dt[0], s5_b_re[0], s5_b_im[0])
    g_dl = g_dl.reshape(nbk, 2, GPB, S5_P)
    ct = (g_dl[:, 0].reshape(G, S5_P), g_dl[:, 1].reshape(G, S5_P),
          g_bbre.reshape(G, S5_H, S5_P).transpose(0, 2, 1), g_bbim.reshape(G, S5_H, S5_P).transpose(0, 2, 1))
    g_a_re, g_a_im, g_log_dt, g_b_re, g_b_im = disc_vjp(ct)
    g_c_re = g_cre.reshape(G, S5_H, S5_P)
    g_c_im = -g_cimn.reshape(G, S5_H, S5_P)
    g_wg2_s = lax.dynamic_slice_in_dim(g_wg2.reshape(GATE_RANK, QK), chip * qk4, qk4, axis=1)
    g_bg_s = lax.dynamic_slice_in_dim(g_bg.reshape(1, QK), chip * qk4, qk4, axis=1)
    g_gn_s = lax.dynamic_slice_in_dim(g_gn.reshape(1, D), chip * (D // NCH), D // NCH, axis=1)

    grads = dict(
        w_ada=g_w_ada, b_ada=g_b_ada, norm_mix=jnp.stack([g_mix0, g_mix1]), norm_mlp=jnp.stack([g_mlp0, g_mlp1]),
        s5_a_re=g_a_re[None], s5_a_im=g_a_im[None], s5_log_dt=g_log_dt[None], s5_b_re=g_b_re[None], s5_b_im=g_b_im[None],
        s5_c_re=g_c_re[None], s5_c_im=g_c_im[None], s5_d=g_d[None], s5_w_glu=g_glu,
        gla_w_in=g_in, gla_w_gate2=g_wg2_s[None], gla_b_gate=g_bg_s, gla_g_norm=g_gn_s,
        gla_w_out=g_out, w_ff1=g_w_ff1, w_ff2=g_w_ff2, norm_final=g_nf)

    names = list(grads)
    large = ("w_ada", "s5_w_glu", "gla_w_in", "gla_w_out", "w_ff1", "w_ff2")
    delta, new_m, new_v = {}, {}, {}
    delta["w_ada"], new_m["w_ada"], new_v["w_ada"] = upd_w_ada
    for nm in large[1:]:
        delta[nm], new_m[nm], new_v[nm] = _adamw(args[nm], grads[nm], args["m_" + nm], args["v_" + nm], name=f"adamw_{nm}")
    grads = {nm: grads[nm].reshape(args[nm].shape) for nm in names}
    for nm in names:
        if nm not in large:
            shp = args[nm].shape
            as2d = (1, -1) if len(shp) == 1 else shp
            outs = _adamw_whole(*[t.reshape(as2d) for t in (args[nm], grads[nm], args["m_" + nm], args["v_" + nm])],
                                name=f"adamw_{nm}")
            delta[nm], new_m[nm], new_v[nm] = (t.reshape(shp) for t in outs)
    return (loss, grad_x, *[grads[n] for n in names], *[delta[n] for n in names], *[new_m[n] for n in names],
            *[new_v[n] for n in names])
```
